```python
import math
import jax, jax.numpy as jnp
from jax import lax
import numpy as np

D_MODEL = 1024
BATCH = 1
SEQ = 16384
DEPTH = 2

GRID_W = 64
CTX_LEN = 256
EPS = 1e-6

HY_WIDTH = D_MODEL // 4
HY_COLS = 3 * HY_WIDTH
HY_SHORT = 3
HY_FILTER_HIDDEN = 64
HY_POS_BANDS = 16
HY_POS_DIM = 1 + 2 * HY_POS_BANDS
HY_DECAY_TARGET = 1e-2
HY_SHORT_DECAY_PCT = 0.3
HY_LONG_DECAY_PCT = 1.5

RET_HEAD_DIM = 64
RET_HEADS = (D_MODEL // 4) // RET_HEAD_DIM
RET_WIDTH = RET_HEADS * RET_HEAD_DIM
RET_COLS = 4 * RET_WIDTH
RET_CHUNK = 128

ATT_HEAD_DIM = 64
ATT_HEADS = (D_MODEL // 2) // ATT_HEAD_DIM
ATT_KV_HEADS = ATT_HEADS // 4
ATT_WIDTH = ATT_HEADS * ATT_HEAD_DIM
ATT_KV_WIDTH = ATT_KV_HEADS * ATT_HEAD_DIM
Q_BLOCK = 128
ROPE_AXIS_DIM = ATT_HEAD_DIM // 2
ROPE_BASE = 10000.0

MIX_WIDTH = HY_WIDTH + RET_WIDTH + ATT_WIDTH
IN_COLS = HY_COLS + RET_COLS + ATT_WIDTH + 2 * ATT_KV_WIDTH

FFN_HIDDEN = ((8 * D_MODEL // 3 + 255) // 256) * 256
N_EXPERTS = 8
TOP_K = 2
EXPERT_HIDDEN = FFN_HIDDEN
MOE_BLOCK = 512
N_DENSE = (DEPTH + 1) // 2
N_MOE = DEPTH // 2

kernel_name = 'hybrid_hyena_retention_gqa_moe_dit'


def rms_norm(x, g):
    x32 = x.astype(jnp.float32)
    y = x32 * lax.rsqrt(jnp.mean(x32 * x32, axis=-1, keepdims=True) + EPS)
    return (y * g.astype(jnp.float32)).astype(x.dtype)


def modulate(h, shift, scale):
    return h * (1 + scale) + shift


def ada_modulation(cvec, w, b):
    m = jax.nn.silu(cvec) @ w + b
    return m.reshape(m.shape[:-1] + (6, D_MODEL))


def split_projection(p):
    return (p[..., :HY_COLS], p[..., HY_COLS:HY_COLS + RET_COLS], p[..., HY_COLS + RET_COLS:])


def short_conv3(u, w, b):
    up = jnp.pad(u, ((0, 0), (1, 1), (0, 0)))
    return up[:, :-2] * w[0] + up[:, 1:-1] * w[1] + up[:, 2:] * w[2] + b


def hyena_two_sided_filter(L, w1, b1, w2, b2, w3, b3, freq):
    n = jnp.arange(L, dtype=jnp.float32)
    t = n / max(L - 1, 1)
    bands = jnp.linspace(1e-4, HY_POS_BANDS - 1, HY_POS_BANDS, dtype=jnp.float32)
    ang = (2.0 * math.pi / L) * n[:, None] * bands[None, :]
    feat = jnp.concatenate([t[:, None], jnp.cos(ang), -jnp.sin(ang)], axis=-1)
    h = jnp.sin(freq * (feat @ w1 + b1))
    h = jnp.sin(freq * (h @ w2 + b2))
    h = (h @ w3 + b3).astype(jnp.float32).reshape(L, 2, HY_WIDTH)
    deltas = jnp.abs(jnp.linspace(math.log(HY_DECAY_TARGET) / HY_LONG_DECAY_PCT,
                                  math.log(HY_DECAY_TARGET) / HY_SHORT_DECAY_PCT,
                                  HY_WIDTH, dtype=jnp.float32))
    h = h * jnp.exp(-t[:, None] * deltas[None, :])[:, None, :]
    h_fwd, h_bwd = h[:, 0], h[:, 1]
    taps = jnp.concatenate([h_fwd, jnp.zeros((1, HY_WIDTH), jnp.float32), h_bwd[:0:-1]], axis=0)
    return taps / (jnp.sum(jnp.abs(taps), axis=0, keepdims=True) + EPS)


def hyena_mixer(p_hy, conv_w, conv_b, filt, bias):
    u = short_conv3(p_hy, conv_w, conv_b)
    x0, x1, v = jnp.split(u, 3, axis=-1)
    z = (x1 * v).astype(jnp.float32)
    L = z.shape[1]
    taps = hyena_two_sided_filter(L, *filt)
    zf = jnp.fft.rfft(z, n=2 * L, axis=1)
    tf = jnp.fft.rfft(taps, axis=0)
    conv = jnp.fft.irfft(zf * tf[None], n=2 * L, axis=1)[:, :L]
    y = x0.astype(jnp.float32) * (conv + bias.astype(jnp.float32) * z)
    return y.astype(p_hy.dtype)


def retention_heads(p_ret):
    B, L, _ = p_ret.shape
    q, k, v, g = jnp.split(p_ret, 4, axis=-1)

    def heads(t):
        return t.reshape(B, L, RET_HEADS, RET_HEAD_DIM).transpose(0, 2, 1, 3).astype(jnp.float32)
    return heads(q), heads(k) * RET_HEAD_DIM ** -0.5, heads(v), g


def retention_final_state(k, v, log_gamma):
    L = k.shape[2]
    w = jnp.exp(log_gamma[:, None] * (L - 1 - jnp.arange(L, dtype=jnp.float32))[None, :])
    return jnp.einsum('bhmd,hm,bhme->bhde', k, w, v)


def retention_chunkwise(q, k, v, log_gamma, s0):
    B, H, L, dk = q.shape
    dv = v.shape[-1]
    nc = L // RET_CHUNK
    qc = q.reshape(B, H, nc, RET_CHUNK, dk)
    kc = k.reshape(B, H, nc, RET_CHUNK, dk)
    vc = v.reshape(B, H, nc, RET_CHUNK, dv)
    pos = jnp.arange(RET_CHUNK, dtype=jnp.float32)
    lg = log_gamma[:, None]
    rel = pos[:, None] - pos[None, :]
    decay_mask = jnp.where(rel >= 0, jnp.exp(lg[:, :, None] * jnp.maximum(rel, 0.0)), 0.0)
    scores = jnp.einsum('bhcnd,bhcmd->bhcnm', qc, kc) * decay_mask[None, :, None]
    inner = jnp.einsum('bhcnm,bhcme->bhcne', scores, vc)
    zeta = jnp.exp(lg * (RET_CHUNK - 1 - pos)[None, :])
    xi = jnp.exp(lg * (pos + 1)[None, :])
    u = jnp.einsum('bhcmd,hm,bhcme->bhcde', kc, zeta, vc)
    chunk_decay = jnp.exp(log_gamma * RET_CHUNK)[None, :, None, None]

    def step(s, u_c):
        return chunk_decay * s + u_c, s
    s_final, s_prev = lax.scan(step, s0.astype(jnp.float32), jnp.moveaxis(u, 2, 0))
    cross = jnp.einsum('bhcnd,hn,bhcde->bhcne', qc, xi, jnp.moveaxis(s_prev, 0, 2))
    return (inner + cross).reshape(B, H, L, dv), s_final


def head_layer_norm(y):
    mu = jnp.mean(y, axis=-1, keepdims=True)
    var = jnp.mean(jnp.square(y - mu), axis=-1, keepdims=True)
    return (y - mu) * lax.rsqrt(var + EPS)


def bidirectional_retention(q, k, v, g, lg_f, lg_b, s_f, s_b):
    out_f, _ = retention_chunkwise(q, k, v, lg_f, s_f)
    out_b, _ = retention_chunkwise(jnp.flip(q, 2), jnp.flip(k, 2), jnp.flip(v, 2), lg_b, s_b)
    y = head_layer_norm(out_f + jnp.flip(out_b, 2))
    B, H, L, d = y.shape
    y = y.transpose(0, 2, 1, 3).reshape(B, L, H * d)
    return (jax.nn.silu(g.astype(jnp.float32)) * y).astype(g.dtype)


def rope_tables_2d(n_tokens):
    rows = n_tokens // GRID_W
    row_idx = jnp.repeat(jnp.arange(rows, dtype=jnp.float32), GRID_W)
    col_idx = jnp.tile(jnp.arange(GRID_W, dtype=jnp.float32), rows)
    inv_freq = ROPE_BASE ** (-jnp.arange(0, ROPE_AXIS_DIM, 2, dtype=jnp.float32) / ROPE_AXIS_DIM)
    ang_r = row_idx[:, None] * inv_freq[None, :]
    ang_c = col_idx[:, None] * inv_freq[None, :]
    return (jnp.cos(ang_r), jnp.sin(ang_r), jnp.cos(ang_c), jnp.sin(ang_c))


def apply_rope_2d(x, tables):
    cos_r, sin_r, cos_c, sin_c = tables

    def rotate(xp, cos, sin):
        half = xp.shape[-1] // 2
        x1, x2 = xp[..., :half], xp[..., half:]
        cos = cos[None, :, None, :]
        sin = sin[None, :, None, :]
        return jnp.concatenate([x1 * cos - x2 * sin, x2 * cos + x1 * sin], axis=-1)
    out = jnp.concatenate([rotate(x[..., :ROPE_AXIS_DIM], cos_r, sin_r),
                           rotate(x[..., ROPE_AXIS_DIM:], cos_c, sin_c)], axis=-1)
    return out.astype(x.dtype)


def attention_heads(p_att, q_g, k_g):
    B, L, _ = p_att.shape
    q = p_att[..., :ATT_WIDTH].reshape(B, L, ATT_HEADS, ATT_HEAD_DIM)
    k = p_att[..., ATT_WIDTH:ATT_WIDTH + ATT_KV_WIDTH].reshape(B, L, ATT_KV_HEADS, ATT_HEAD_DIM)
    v = p_att[..., ATT_WIDTH + ATT_KV_WIDTH:].reshape(B, L, ATT_KV_HEADS, ATT_HEAD_DIM)
    return rms_norm(q, q_g), rms_norm(k, k_g), v


def block_attention(q, k, v):
    B, Lq, H, d = q.shape
    Hkv = k.shape[2]
    G = H // Hkv
    nb = Lq // Q_BLOCK
    qb = q.reshape(B, nb, Q_BLOCK, Hkv, G, d).transpose(1, 0, 2, 3, 4, 5)
    scale = d ** -0.5

    def one_block(q_blk):
        s = jnp.einsum('bqhgd,bkhd->bhgqk', q_blk, k).astype(jnp.float32) * scale
        p = jax.nn.softmax(s, axis=-1).astype(v.dtype)
        return jnp.einsum('bhgqk,bkhd->bqhgd', p, v)
    o = lax.map(one_block, qb)
    return o.transpose(1, 0, 2, 3, 4, 5).reshape(B, Lq, H * d)


def swiglu(x, w_gate, w_up, w_down):
    return (jax.nn.silu(x @ w_gate) * (x @ w_up)) @ w_down


def moe_swiglu(x, router_w, router_b, w_gate, w_up, w_down):
    B, L, D = x.shape
    n_tok = B * L
    n_asg = n_tok * TOP_K
    xt = x.reshape(n_tok, D)
    logits = (xt @ router_w + router_b).astype(jnp.float32)
    top_val, top_idx = lax.top_k(logits, TOP_K)
    gates = jax.nn.softmax(top_val, axis=-1)
    expert = top_idx.reshape(-1).astype(jnp.int32)
    token = jnp.repeat(jnp.arange(n_tok, dtype=jnp.int32), TOP_K)
    order = jnp.argsort(expert)
    e_s = expert[order]
    t_s = token[order]
    g_s = gates.reshape(-1)[order]
    counts = jnp.bincount(expert, length=N_EXPERTS)
    starts = jnp.cumsum(counts) - counts
    padded = (counts + MOE_BLOCK - 1) // MOE_BLOCK * MOE_BLOCK
    pad_end = jnp.cumsum(padded)
    pad_start = pad_end - padded
    dest = pad_start[e_s] + jnp.arange(n_asg, dtype=jnp.int32) - starts[e_s]
    cap = -(-n_asg // MOE_BLOCK) * MOE_BLOCK + N_EXPERTS * MOE_BLOCK
    n_blk = cap // MOE_BLOCK
    buf = jnp.zeros((cap, D), x.dtype).at[dest].set(xt[t_s])
    blk_start = jnp.arange(n_blk, dtype=pad_end.dtype) * MOE_BLOCK
    blk_expert = jnp.minimum(jnp.searchsorted(pad_end, blk_start, side='right'), N_EXPERTS - 1)

    def expert_block(args):
        xb, e = args
        return swiglu(xb, w_gate[e], w_up[e], w_down[e])
    yb = lax.map(expert_block, (buf.reshape(n_blk, MOE_BLOCK, D), blk_expert))
    y_asg = yb.reshape(cap, D)[dest] * g_s[:, None].astype(yb.dtype)
    out = jnp.zeros((n_tok, D), y_asg.dtype).at[t_s].add(y_asg)
    return out.reshape(B, L, D)


def channel_mixer(h, l, ffn_w_gate, ffn_w_up, ffn_w_down,
                  moe_router_w, moe_router_b, moe_w_gate, moe_w_up, moe_w_down):
    i = l // 2
    if l % 2 == 0:
        return swiglu(h, ffn_w_gate[i], ffn_w_up[i], ffn_w_down[i])
    return moe_swiglu(h, moe_router_w[i], moe_router_b[i], moe_w_gate[i], moe_w_up[i], moe_w_down[i])


def setup_inputs(seed: int = 0) -> dict:
    key = jax.random.key(seed)
    keys = iter(jax.random.split(key, 40))

    def nrm(shape, scale):
        return jax.random.normal(next(keys), shape, jnp.float32) * scale
    D = D_MODEL
    h_idx = jnp.arange(RET_HEADS, dtype=jnp.float32)
    rate_base = jnp.stack([-(5.0 + h_idx), -(5.5 + h_idx)]) * math.log(2.0)
    return {
        'x': nrm((BATCH, SEQ, D), 1.0),
        'c': nrm((BATCH, D), 1.0),
        'ctx': nrm((BATCH, CTX_LEN, D), 1.0),
        'c_ctx': nrm((D,), 1.0),
        'ada_w': nrm((DEPTH, D, 6 * D), 0.5 * D ** -0.5),
        'ada_b': nrm((DEPTH, 6 * D), 0.02),
        'norm1_g': 1.0 + nrm((DEPTH, D), 0.02),
        'norm2_g': 1.0 + nrm((DEPTH, D), 0.02),
        'w_in': nrm((DEPTH, D, IN_COLS), D ** -0.5),
        'w_out': nrm((DEPTH, MIX_WIDTH, D), MIX_WIDTH ** -0.5),
        'hy_conv_w': nrm((DEPTH, HY_SHORT, HY_COLS), HY_SHORT ** -0.5),
        'hy_conv_b': nrm((DEPTH, HY_COLS), 0.02),
        'hy_filt_w1': nrm((DEPTH, HY_POS_DIM, HY_FILTER_HIDDEN), HY_POS_DIM ** -0.5),
        'hy_filt_b1': nrm((DEPTH, HY_FILTER_HIDDEN), 0.1),
        'hy_filt_w2': nrm((DEPTH, HY_FILTER_HIDDEN, HY_FILTER_HIDDEN), HY_FILTER_HIDDEN ** -0.5),
        'hy_filt_b2': nrm((DEPTH, HY_FILTER_HIDDEN), 0.1),
        'hy_filt_w3': nrm((DEPTH, HY_FILTER_HIDDEN, 2 * HY_WIDTH), HY_FILTER_HIDDEN ** -0.5),
        'hy_filt_b3': nrm((DEPTH, 2 * HY_WIDTH), 0.02),
        'hy_filt_freq': 1.0 + nrm((DEPTH, HY_FILTER_HIDDEN), 0.05),
        'hy_bias': nrm((DEPTH, HY_WIDTH), 0.3),
        'ret_log_rate': rate_base[None] + nrm((DEPTH, 2, RET_HEADS), 0.05),
        'attn_q_g': 1.0 + nrm((DEPTH, ATT_HEAD_DIM), 0.02),
        'attn_k_g': 1.0 + nrm((DEPTH, ATT_HEAD_DIM), 0.02),
        'ffn_w_gate': nrm((N_DENSE, D, FFN_HIDDEN), D ** -0.5),
        'ffn_w_up': nrm((N_DENSE, D, FFN_HIDDEN), D ** -0.5),
        'ffn_w_down': nrm((N_DENSE, FFN_HIDDEN, D), FFN_HIDDEN ** -0.5),
        'moe_router_w': nrm((N_MOE, D, N_EXPERTS), D ** -0.5),
        'moe_router_b': nrm((N_MOE, N_EXPERTS), 0.01),
        'moe_w_gate': nrm((N_MOE, N_EXPERTS, D, EXPERT_HIDDEN), D ** -0.5),
        'moe_w_up': nrm((N_MOE, N_EXPERTS, D, EXPERT_HIDDEN), D ** -0.5),
        'moe_w_down': nrm((N_MOE, N_EXPERTS, EXPERT_HIDDEN, D), EXPERT_HIDDEN ** -0.5),
        'final_norm_g': 1.0 + nrm((D,), 0.02),
    }


def reference(x, c, ctx, c_ctx, ada_w, ada_b, norm1_g, norm2_g, w_in, w_out,
              hy_conv_w, hy_conv_b, hy_filt_w1, hy_filt_b1, hy_filt_w2, hy_filt_b2,
              hy_filt_w3, hy_filt_b3, hy_filt_freq, hy_bias, ret_log_rate,
              attn_q_g, attn_k_g, ffn_w_gate, ffn_w_up, ffn_w_down,
              moe_router_w, moe_router_b, moe_w_gate, moe_w_up, moe_w_down, final_norm_g):
    rope = rope_tables_2d(x.shape[1])
    x_lat, x_ctx = x, ctx
    for l in range(DEPTH):
        last = l == DEPTH - 1
        mod = ada_modulation(c, ada_w[l], ada_b[l])[:, :, None, :]
        mod_c = ada_modulation(c_ctx, ada_w[l], ada_b[l])
        filt = (hy_filt_w1[l], hy_filt_b1[l], hy_filt_w2[l], hy_filt_b2[l],
                hy_filt_w3[l], hy_filt_b3[l], hy_filt_freq[l])
        lg_f = -jnp.exp(ret_log_rate[l, 0].astype(jnp.float32))
        lg_b = -jnp.exp(ret_log_rate[l, 1].astype(jnp.float32))

        h = modulate(rms_norm(x_lat, norm1_g[l]), mod[:, 0], mod[:, 1])
        hc = modulate(rms_norm(x_ctx, norm1_g[l]), mod_c[0], mod_c[1])
        p_hy, p_ret, p_att = split_projection(h @ w_in[l])
        pc_hy, pc_ret, pc_att = split_projection(hc @ w_in[l])

        qc_r, kc_r, vc_r, gc_r = retention_heads(pc_ret)
        s_fwd = retention_final_state(kc_r, vc_r, lg_f)
        s_bwd = retention_final_state(jnp.flip(kc_r, 2), jnp.flip(vc_r, 2), lg_b)
        qc_a, kc_a, vc_a = attention_heads(pc_att, attn_q_g[l], attn_k_g[l])

        y_hy = hyena_mixer(p_hy, hy_conv_w[l], hy_conv_b[l], filt, hy_bias[l])
        q_r, k_r, v_r, g_r = retention_heads(p_ret)
        y_ret = bidirectional_retention(q_r, k_r, v_r, g_r, lg_f, lg_b, s_fwd, s_bwd)
        q_a, k_a, v_a = attention_heads(p_att, attn_q_g[l], attn_k_g[l])
        y_att = block_attention(apply_rope_2d(q_a, rope),
                                jnp.concatenate([kc_a, apply_rope_2d(k_a, rope)], axis=1),
                                jnp.concatenate([vc_a, v_a], axis=1))
        y = jnp.concatenate([y_hy, y_ret, y_att], axis=-1) @ w_out[l]

        if not last:
            zero_state = jnp.zeros_like(s_fwd)
            yc_hy = hyena_mixer(pc_hy, hy_conv_w[l], hy_conv_b[l], filt, hy_bias[l])
            yc_ret = bidirectional_retention(qc_r, kc_r, vc_r, gc_r, lg_f, lg_b, zero_state, zero_state)
            yc_att = block_attention(qc_a, kc_a, vc_a)
            yc = jnp.concatenate([yc_hy, yc_ret, yc_att], axis=-1) @ w_out[l]
            x_ctx = x_ctx + mod_c[2] * yc
        x_lat = x_lat + mod[:, 2] * y

        h2 = modulate(rms_norm(x_lat, norm2_g[l]), mod[:, 3], mod[:, 4])
        x_lat = x_lat + mod[:, 5] * channel_mixer(h2, l, ffn_w_gate, ffn_w_up, ffn_w_down,
                                                  moe_router_w, moe_router_b, moe_w_gate, moe_w_up, moe_w_down)
        if not last:
            hc2 = modulate(rms_norm(x_ctx, norm2_g[l]), mod_c[3], mod_c[4])
            x_ctx = x_ctx + mod_c[5] * channel_mixer(hc2, l, ffn_w_gate, ffn_w_up, ffn_w_down,
                                                     moe_router_w, moe_router_b, moe_w_gate, moe_w_up, moe_w_down)
    return rms_norm(x_lat, final_norm_g)
```

```python
import functools
import math

import numpy as np
import jax
import jax.numpy as jnp
from jax import lax
from jax.experimental import pallas as pl
from jax.experimental.pallas import tpu as pltpu

F32 = jnp.float32
BF16 = jnp.bfloat16

D_MODEL = 1024
DEPTH = 2
GRID_W = 64
EPS = 1e-6

HY_WIDTH = 256
HY_COLS = 3 * HY_WIDTH
HY_POS_BANDS = 16
HY_DECAY_TARGET = 1e-2
HY_SHORT_DECAY_PCT = 0.3
HY_LONG_DECAY_PCT = 1.5

RET_HEAD_DIM = 64
RET_HEADS = 4
RET_WIDTH = RET_HEADS * RET_HEAD_DIM
RET_COLS = 4 * RET_WIDTH
RET_CHUNK = 128

ATT_HEAD_DIM = 64
ATT_HEADS = 8
ATT_KV_HEADS = 2
ATT_GROUP = ATT_HEADS // ATT_KV_HEADS
ATT_WIDTH = ATT_HEADS * ATT_HEAD_DIM
ATT_KV_WIDTH = ATT_KV_HEADS * ATT_HEAD_DIM
ATT_COLS = ATT_WIDTH + 2 * ATT_KV_WIDTH
ROPE_AXIS_DIM = ATT_HEAD_DIM // 2
ROPE_BASE = 10000.0

MIX_WIDTH = HY_WIDTH + RET_WIDTH + ATT_WIDTH
IN_COLS = HY_COLS + RET_COLS + ATT_COLS

FFN_HIDDEN = 2816
N_EXPERTS = 8
TOP_K = 2
MOE_BLOCK = 512

LANES = 128
VMEM_LIMIT = 56 * 1024 * 1024


def _cparams(*sem):
    return pltpu.CompilerParams(dimension_semantics=sem, vmem_limit_bytes=VMEM_LIMIT)


def _dot(a, b):
    return jnp.dot(a, b, preferred_element_type=F32)


def _split(a):
    hi = a.astype(BF16)
    lo = (a - hi.astype(F32)).astype(BF16)
    return hi, lo


def _dot3(a, b):
    ah, al = _split(a)
    bh, bl = _split(b)
    return _dot(ah, bh) + _dot(al, bh) + _dot(ah, bl)


def _silu(x):
    return x * (1.0 / (1.0 + jnp.exp(-x)))


def _norm_mod(x, g, shift, scale):
    ms = jnp.mean(x * x, axis=-1, keepdims=True)
    h = x * lax.rsqrt(ms + EPS) * g
    return h * (1.0 + scale) + shift


def _mm3_kernel(a_ref, b_ref, o_ref):
    o_ref[...] = _dot3(a_ref[...], b_ref[...])


def _mm3(a, b, tn, name):
    m, k = a.shape
    n = b.shape[1]
    return pl.pallas_call(
        _mm3_kernel,
        out_shape=jax.ShapeDtypeStruct((m, n), F32),
        grid=(n // tn,),
        in_specs=[pl.BlockSpec((m, k), lambda j: (0, 0)),
                  pl.BlockSpec((k, tn), lambda j: (0, j))],
        out_specs=pl.BlockSpec((m, tn), lambda j: (0, j)),
        compiler_params=_cparams("parallel"),
        name=name,
    )(a, b)


def _ada_kernel(c_ref, w_ref, b_ref, o_ref):
    o_ref[...] = _dot3(c_ref[...], w_ref[...]) + b_ref[...]


def _ada_modulation(cc, w, b):
    n = w.shape[1]
    tn = 1536
    return pl.pallas_call(
        _ada_kernel,
        out_shape=jax.ShapeDtypeStruct((8, n), F32),
        grid=(n // tn,),
        in_specs=[pl.BlockSpec((8, D_MODEL), lambda j: (0, 0)),
                  pl.BlockSpec((D_MODEL, tn), lambda j: (0, j)),
                  pl.BlockSpec((1, tn), lambda j: (0, j))],
        out_specs=pl.BlockSpec((8, tn), lambda j: (0, j)),
        compiler_params=_cparams("parallel"),
        name="ada_modulation",
    )(cc, w, b.reshape(1, n))


def _inproj_kernel(x_ref, g_ref, sh_ref, sc_ref, w_ref, hy_ref, ret_ref, att_ref):
    h = _norm_mod(x_ref[...], g_ref[...], sh_ref[...], sc_ref[...])
    p = _dot(h.astype(BF16), w_ref[...])
    hy_ref[...] = p[:, :HY_COLS]
    ret_ref[...] = p[:, HY_COLS:HY_COLS + RET_COLS]
    att_ref[...] = p[:, HY_COLS + RET_COLS:]


def _in_projection(x, g, shift, scale, w_bf16):
    t = x.shape[0]
    tm = min(t, 512)
    vec = pl.BlockSpec((1, D_MODEL), lambda i: (0, 0))
    return pl.pallas_call(
        _inproj_kernel,
        out_shape=(jax.ShapeDtypeStruct((t, HY_COLS), F32),
                   jax.ShapeDtypeStruct((t, RET_COLS), F32),
                   jax.ShapeDtypeStruct((t, ATT_COLS), F32)),
        grid=(t // tm,),
        in_specs=[pl.BlockSpec((tm, D_MODEL), lambda i: (i, 0)), vec, vec, vec,
                  pl.BlockSpec((D_MODEL, IN_COLS), lambda i: (0, 0))],
        out_specs=(pl.BlockSpec((tm, HY_COLS), lambda i: (i, 0)),
                   pl.BlockSpec((tm, RET_COLS), lambda i: (i, 0)),
                   pl.BlockSpec((tm, ATT_COLS), lambda i: (i, 0))),
        compiler_params=_cparams("parallel"),
        name="in_projection",
    )(x, g, shift, scale, w_bf16)


def _head_mean_matrix(width):
    idx = np.arange(width) // ATT_HEAD_DIM
    return jnp.asarray((idx[:, None] == idx[None, :]).astype(np.float32) / ATT_HEAD_DIM, BF16)


def _head_rms(x, bd, g):
    x2h, x2l = _split(x * x)
    ms = _dot(x2h, bd) + _dot(x2l, bd)
    return x * lax.rsqrt(ms + EPS) * g


def _rope(x, cos, sin):
    n = x.shape[1]
    lane = lax.broadcasted_iota(jnp.int32, x.shape, 1)
    swapped = jnp.where((lane % 32) < 16, pltpu.roll(x, n - 16, 1), pltpu.roll(x, 16, 1))
    return x * cos + swapped * sin


def _attprep_kernel(p_ref, cos_ref, sin_ref, qg_ref, kg_ref, bdq_ref, bdk_ref,
                    q_ref, kt_ref, v_ref, *, rope):
    p = p_ref[...]
    q = _head_rms(p[:, :ATT_WIDTH], bdq_ref[...], qg_ref[...])
    k = _head_rms(p[:, ATT_WIDTH:ATT_WIDTH + ATT_KV_WIDTH], bdk_ref[...], kg_ref[...])
    if rope:
        cos = cos_ref[...]
        sin = sin_ref[...]
        q = jnp.concatenate(
            [_rope(q[:, j * LANES:(j + 1) * LANES], cos, sin) for j in range(ATT_WIDTH // LANES)],
            axis=1)
        k = _rope(k, cos, sin)
    q_ref[...] = (q * ATT_HEAD_DIM ** -0.5).astype(BF16)
    kt_ref[...] = k.T.astype(BF16)
    pv = p[:, ATT_WIDTH + ATT_KV_WIDTH:]
    lane = lax.broadcasted_iota(jnp.int32, pv.shape, 1)
    ones_col = jnp.where(lane == ATT_HEAD_DIM, 1.0, 0.0)
    v_ref[0] = jnp.where(lane < ATT_HEAD_DIM, pv, ones_col).astype(BF16)
    v_ref[1] = jnp.where(lane < ATT_HEAD_DIM, pltpu.roll(pv, ATT_HEAD_DIM, 1), ones_col).astype(BF16)


def _attention_prep(p_att, cos, sin, q_g, k_g, rope):
    t = p_att.shape[0]
    tm = min(t, 512)
    qg = jnp.tile(q_g, ATT_HEADS).reshape(1, ATT_WIDTH)
    kg = jnp.tile(k_g, ATT_KV_HEADS).reshape(1, ATT_KV_WIDTH)
    const = lambda shape: pl.BlockSpec(shape, lambda i: (0, 0))
    return pl.pallas_call(
        functools.partial(_attprep_kernel, rope=rope),
        out_shape=(jax.ShapeDtypeStruct((t, ATT_WIDTH), BF16),
                   jax.ShapeDtypeStruct((ATT_KV_WIDTH, t), BF16),
                   jax.ShapeDtypeStruct((ATT_KV_HEADS, t, LANES), BF16)),
        grid=(t // tm,),
        in_specs=[pl.BlockSpec((tm, ATT_COLS), lambda i: (i, 0)),
                  pl.BlockSpec((tm, LANES), lambda i: (i, 0)),
                  pl.BlockSpec((tm, LANES), lambda i: (i, 0)),
                  const((1, ATT_WIDTH)), const((1, ATT_KV_WIDTH)),
                  const((ATT_WIDTH, ATT_WIDTH)), const((ATT_KV_WIDTH, ATT_KV_WIDTH))],
        out_specs=(pl.BlockSpec((tm, ATT_WIDTH), lambda i: (i, 0)),
                   pl.BlockSpec((ATT_KV_WIDTH, tm), lambda i: (0, i)),
                   pl.BlockSpec((ATT_KV_HEADS, tm, LANES), lambda i: (0, i, 0))),
        compiler_params=_cparams("parallel"),
        name="attention_prep",
    )(p_att, cos, sin, qg, kg, _head_mean_matrix(ATT_WIDTH), _head_mean_matrix(ATT_KV_WIDTH))


def _rope_tables(n_tokens):
    n = np.arange(n_tokens)
    inv_freq = ROPE_BASE ** (-np.arange(0, ROPE_AXIS_DIM, 2, dtype=np.float64) / ROPE_AXIS_DIM)
    inv_freq = inv_freq.astype(np.float32)
    ang_r = jnp.asarray((n // GRID_W).astype(np.float32))[:, None] * jnp.asarray(inv_freq)[None, :]
    ang_c = jnp.asarray((n % GRID_W).astype(np.float32))[:, None] * jnp.asarray(inv_freq)[None, :]
    cos = jnp.concatenate([jnp.cos(ang_r)] * 2 + [jnp.cos(ang_c)] * 2, axis=1)
    sin = jnp.concatenate([-jnp.sin(ang_r), jnp.sin(ang_r), -jnp.sin(ang_c), jnp.sin(ang_c)], axis=1)
    return jnp.tile(cos, (1, 2)), jnp.tile(sin, (1, 2))


def _softmax_step(qs, kt, v, m_ref, acc_ref, first):
    s = _dot(qs, kt)
    m_tile = jnp.max(s, axis=-1, keepdims=True)
    if first:
        m_new = m_tile
        p = jnp.exp(s - m_new)
        acc_ref[...] = _dot(p.astype(BF16), v)
    else:
        m_old = m_ref[...]
        m_new = jnp.maximum(m_old, m_tile)
        p = jnp.exp(s - m_new)
        acc_ref[...] = jnp.exp(m_old - m_new) * acc_ref[...] + _dot(p.astype(BF16), v)
    m_ref[...] = m_new


def _attention_kernel(*refs, tq, n_lat_steps):
    if n_lat_steps:
        q_ref, ktc_ref, vc_ref, kt_ref, v_ref, o_ref, qs_ref, m_ref, acc_ref = refs
    else:
        q_ref, ktc_ref, vc_ref, o_ref, qs_ref, m_ref, acc_ref = refs
    j = pl.program_id(2)

    @pl.when(j == 0)
    def _():
        for g in range(ATT_GROUP):
            qs_ref[g * tq:(g + 1) * tq, :] = q_ref[:, g * ATT_HEAD_DIM:(g + 1) * ATT_HEAD_DIM]
        _softmax_step(qs_ref[...], ktc_ref[...], vc_ref[0], m_ref, acc_ref, True)

    if n_lat_steps:
        _softmax_step(qs_ref[...], kt_ref[...], v_ref[0], m_ref, acc_ref, False)

    @pl.when(j == max(n_lat_steps, 1) - 1)
    def _():
        acc = acc_ref[...]
        o = acc[:, :ATT_HEAD_DIM] / acc[:, ATT_HEAD_DIM:ATT_HEAD_DIM + 1]
        o_ref[...] = jnp.concatenate(
            [o[g * tq:(g + 1) * tq] for g in range(ATT_GROUP)], axis=1).astype(o_ref.dtype)


def _attention(q, ktc, vc, kt=None, v=None, *, tq, tk):
    t = q.shape[0]
    lc = ktc.shape[1]
    gw = ATT_GROUP * ATT_HEAD_DIM
    n_lat = 0 if kt is None else kt.shape[1] // tk
    in_specs = [pl.BlockSpec((tq, gw), lambda h, i, j: (i, h)),
                pl.BlockSpec((ATT_HEAD_DIM, lc), lambda h, i, j: (h, 0)),
                pl.BlockSpec((1, lc, LANES), lambda h, i, j: (h, 0, 0))]
    args = [q, ktc, vc]
    if n_lat:
        in_specs += [pl.BlockSpec((ATT_HEAD_DIM, tk), lambda h, i, j: (h, j)),
                     pl.BlockSpec((1, tk, LANES), lambda h, i, j: (h, j, 0))]
        args += [kt, v]
    return pl.pallas_call(
        functools.partial(_attention_kernel, tq=tq, n_lat_steps=n_lat),
        out_shape=jax.ShapeDtypeStruct((t, ATT_WIDTH), BF16),
        grid=(ATT_KV_HEADS, t // tq, max(n_lat, 1)),
        in_specs=in_specs,
        out_specs=pl.BlockSpec((tq, gw), lambda h, i, j: (i, h)),
        scratch_shapes=[pltpu.VMEM((ATT_GROUP * tq, ATT_HEAD_DIM), BF16),
                        pltpu.VMEM((ATT_GROUP * tq, 1), F32),
                        pltpu.VMEM((ATT_GROUP * tq, LANES), F32)],
        compiler_params=_cparams("parallel", "parallel", "arbitrary"),
        name="attention_lat" if n_lat else "attention_ctx",
    )(*args)


def _ret_tables(lg_f, lg_b):
    pos = jnp.arange(RET_CHUNK, dtype=F32)
    rel = pos[:, None] - pos[None, :]
    mask = jnp.where(rel > 0, jnp.exp(lg_f[:, None, None] * jnp.maximum(rel, 0.0)),
                     jnp.where(rel < 0, jnp.exp(lg_b[:, None, None] * jnp.maximum(-rel, 0.0)), 2.0))
    zeta = jnp.stack([jnp.exp(lg_f[:, None] * (RET_CHUNK - 1 - pos)[None, :]),
                      jnp.exp(lg_b[:, None] * pos[None, :])])
    xi = jnp.stack([jnp.exp(lg_f[:, None] * (pos + 1)[None, :]),
                    jnp.exp(lg_b[:, None] * (RET_CHUNK - pos)[None, :])])
    chunk_decay = jnp.stack([jnp.exp(lg_f * RET_CHUNK), jnp.exp(lg_b * RET_CHUNK)])
    zeta = jnp.repeat(zeta, RET_HEAD_DIM, axis=1).transpose(0, 2, 1)
    xi = jnp.repeat(xi, RET_HEAD_DIM, axis=1).transpose(0, 2, 1)
    chunk_decay = jnp.repeat(chunk_decay, RET_HEAD_DIM, axis=1)[:, :, None]
    return mask, zeta, xi, chunk_decay


def _ret_state_kernel(k_ref, v_ref, zeta_ref, cd_ref, s0_ref, sprev_ref, sfin_ref, s_ref):
    i = pl.program_id(1)

    @pl.when(i == 0)
    def _():
        s_ref[...] = s0_ref[0]

    s = s_ref[...]
    sprev_ref[0, 0] = s
    kz = (k_ref[...] * (RET_HEAD_DIM ** -0.5) * zeta_ref[0]).astype(BF16)
    v = v_ref[...].astype(BF16)
    us = []
    for h in range(RET_HEADS):
        sl = slice(h * RET_HEAD_DIM, (h + 1) * RET_HEAD_DIM)
        us.append(lax.dot_general(kz[:, sl], v[:, sl], (((0,), (0,)), ((), ())),
                                  preferred_element_type=F32))
    s_new = cd_ref[0] * s + jnp.concatenate(us, axis=0)
    s_ref[...] = s_new
    sfin_ref[0] = s_new


def _retention_states(p_ret, zeta, chunk_decay, s0):
    nc = p_ret.shape[0] // RET_CHUNK
    wb = RET_WIDTH // RET_WIDTH

    def chunk(d, i):
        return i + d * (nc - 1 - 2 * i)

    return pl.pallas_call(
        _ret_state_kernel,
        out_shape=(jax.ShapeDtypeStruct((2, nc, RET_WIDTH, RET_HEAD_DIM), F32),
                   jax.ShapeDtypeStruct((2, RET_WIDTH, RET_HEAD_DIM), F32)),
        grid=(2, nc),
        in_specs=[pl.BlockSpec((RET_CHUNK, RET_WIDTH), lambda d, i: (chunk(d, i), 1 * wb)),
                  pl.BlockSpec((RET_CHUNK, RET_WIDTH), lambda d, i: (chunk(d, i), 2 * wb)),
                  pl.BlockSpec((1, RET_CHUNK, RET_WIDTH), lambda d, i: (d, 0, 0)),
                  pl.BlockSpec((1, RET_WIDTH, 1), lambda d, i: (d, 0, 0)),
                  pl.BlockSpec((1, RET_WIDTH, RET_HEAD_DIM), lambda d, i: (d, 0, 0))],
        out_specs=(pl.BlockSpec((1, 1, RET_WIDTH, RET_HEAD_DIM), lambda d, i: (d, chunk(d, i), 0, 0)),
                   pl.BlockSpec((1, RET_WIDTH, RET_HEAD_DIM), lambda d, i: (d, 0, 0))),
        scratch_shapes=[pltpu.VMEM((RET_WIDTH, RET_HEAD_DIM), F32)],
        compiler_params=_cparams("arbitrary", "arbitrary"),
        name="retention_states",
    )(p_ret, p_ret, zeta, chunk_decay, s0)


def _ret_out_kernel(q_ref, k_ref, v_ref, g_ref, mask_ref, xi_ref, sf_ref, sb_ref, y_ref):
    q = q_ref[...]
    k = (k_ref[...] * (RET_HEAD_DIM ** -0.5)).astype(BF16)
    v = v_ref[...].astype(BF16)
    qb = q.astype(BF16)
    qf = (q * xi_ref[0]).astype(BF16)
    qr = (q * xi_ref[1]).astype(BF16)
    sf = sf_ref[0, 0].astype(BF16)
    sb = sb_ref[0, 0].astype(BF16)
    ys = []
    for h in range(RET_HEADS):
        sl = slice(h * RET_HEAD_DIM, (h + 1) * RET_HEAD_DIM)
        scores = lax.dot_general(qb[:, sl], k[:, sl], (((1,), (1,)), ((), ())),
                                 preferred_element_type=F32)
        y = _dot((scores * mask_ref[h]).astype(BF16), v[:, sl])
        y = y + _dot(qf[:, sl], sf[sl, :]) + _dot(qr[:, sl], sb[sl, :])
        mu = jnp.mean(y, axis=-1, keepdims=True)
        yc = y - mu
        var = jnp.mean(yc * yc, axis=-1, keepdims=True)
        ys.append(yc * lax.rsqrt(var + EPS))
    y_ref[...] = (_silu(g_ref[...]) * jnp.concatenate(ys, axis=1)).astype(y_ref.dtype)


def _retention_outputs(p_ret, mask, xi, sprev):
    nc = p_ret.shape[0] // RET_CHUNK
    col = lambda c: pl.BlockSpec((RET_CHUNK, RET_WIDTH), lambda i: (i, c))
    return pl.pallas_call(
        _ret_out_kernel,
        out_shape=jax.ShapeDtypeStruct((p_ret.shape[0], RET_WIDTH), BF16),
        grid=(nc,),
        in_specs=[col(0), col(1), col(2), col(3),
                  pl.BlockSpec((RET_HEADS, RET_CHUNK, RET_CHUNK), lambda i: (0, 0, 0)),
                  pl.BlockSpec((2, RET_CHUNK, RET_WIDTH), lambda i: (0, 0, 0)),
                  pl.BlockSpec((1, 1, RET_WIDTH, RET_HEAD_DIM), lambda i: (0, i, 0, 0)),
                  pl.BlockSpec((1, 1, RET_WIDTH, RET_HEAD_DIM), lambda i: (1, i, 0, 0))],
        out_specs=pl.BlockSpec((RET_CHUNK, RET_WIDTH), lambda i: (i, 0)),
        compiler_params=_cparams("parallel"),
        name="retention_outputs",
    )(p_ret, p_ret, p_ret, p_ret, mask, xi, sprev, sprev)


def _hy_pre_kernel(p_ref, prev_ref, next_ref, w_ref, b_ref, x0_ref, z_ref):
    i = pl.program_id(0)
    x = p_ref[...]
    tm = x.shape[0]
    row = lax.broadcasted_iota(jnp.int32, x.shape, 0)
    prev_row = jnp.where(i == 0, 0.0, prev_ref[7:8, :])
    next_row = jnp.where(i == pl.num_programs(0) - 1, 0.0, next_ref[0:1, :])
    up = jnp.where(row == 0, prev_row, pltpu.roll(x, 1, 0))
    dn = jnp.where(row == tm - 1, next_row, pltpu.roll(x, tm - 1, 0))
    w = w_ref[...]
    u = up * w[0:1] + x * w[1:2] + dn * w[2:3] + b_ref[...]
    x0_ref[...] = u[:, :HY_WIDTH]
    z_ref[...] = u[:, HY_WIDTH:2 * HY_WIDTH] * u[:, 2 * HY_WIDTH:]


def _hyena_pre(p_hy, conv_w, conv_b):
    t = p_hy.shape[0]
    tm = min(t, 512)
    nb8 = tm // 8
    last8 = t // 8 - 1
    return pl.pallas_call(
        _hy_pre_kernel,
        out_shape=(jax.ShapeDtypeStruct((t, HY_WIDTH), F32),
                   jax.ShapeDtypeStruct((t, HY_WIDTH), F32)),
        grid=(t // tm,),
        in_specs=[pl.BlockSpec((tm, HY_COLS), lambda i: (i, 0)),
                  pl.BlockSpec((8, HY_COLS), lambda i: (jnp.maximum(i * nb8 - 1, 0), 0)),
                  pl.BlockSpec((8, HY_COLS), lambda i: (jnp.minimum((i + 1) * nb8, last8), 0)),
                  pl.BlockSpec((3, HY_COLS), lambda i: (0, 0)),
                  pl.BlockSpec((1, HY_COLS), lambda i: (0, 0))],
        out_specs=(pl.BlockSpec((tm, HY_WIDTH), lambda i: (i, 0)),
                   pl.BlockSpec((tm, HY_WIDTH), lambda i: (i, 0))),
        compiler_params=_cparams("parallel"),
        name="hyena_pre",
    )(p_hy, p_hy, p_hy, conv_w, conv_b.reshape(1, HY_COLS))


def _hy_filter_kernel(feat_ref, w1_ref, b1_ref, w2_ref, b2_ref, w3_ref, b3_ref, freq_ref,
                      delta_ref, taps_ref, asum_ref, *, seq_len):
    i = pl.program_id(0)
    feat = feat_ref[...]
    freq = freq_ref[...]
    h = jnp.sin(freq * (_dot3(feat, w1_ref[...]) + b1_ref[...]))
    h = jnp.sin(freq * (_dot3(h, w2_ref[...]) + b2_ref[...]))
    h = _dot3(h, w3_ref[...]) + b3_ref[...]
    tm = feat.shape[0]
    j = i * tm + lax.broadcasted_iota(jnp.int32, (tm, HY_WIDTH), 0)
    t = feat[:, 0:1]
    hsel = jnp.where(j < seq_len, h[:, :HY_WIDTH], h[:, HY_WIDTH:])
    taps = jnp.where(j == seq_len, 0.0, hsel * jnp.exp(-t * delta_ref[...]))
    taps_ref[...] = taps

    @pl.when(i == 0)
    def _():
        asum_ref[...] = jnp.zeros_like(asum_ref)

    asum_ref[...] += jnp.sum(jnp.abs(taps).reshape(tm // 8, 8, HY_WIDTH), axis=0)


def _hyena_taps(seq_len, w1, b1, w2, b2, w3, b3, freq):
    n = 2 * seq_len
    j = np.arange(n)
    pos = np.where(j <= seq_len, j, n - j).astype(np.float32)
    pos = jnp.asarray(pos)
    tt = pos / max(seq_len - 1, 1)
    bands = jnp.linspace(1e-4, HY_POS_BANDS - 1, HY_POS_BANDS, dtype=F32)
    ang = (2.0 * math.pi / seq_len) * pos[:, None] * bands[None, :]
    feat = jnp.concatenate([tt[:, None], jnp.cos(ang), -jnp.sin(ang),
                            jnp.zeros((n, LANES - 1 - 2 * HY_POS_BANDS), F32)], axis=-1)
    w1p = jnp.concatenate([w1, jnp.zeros((LANES - w1.shape[0], w1.shape[1]), F32)], axis=0)
    deltas = jnp.abs(jnp.linspace(math.log(HY_DECAY_TARGET) / HY_LONG_DECAY_PCT,
                                  math.log(HY_DECAY_TARGET) / HY_SHORT_DECAY_PCT,
                                  HY_WIDTH, dtype=F32)).reshape(1, HY_WIDTH)
    tm = min(n, 1024)
    hid = w1.shape[1]
    const = lambda shape: pl.BlockSpec(shape, lambda i: (0, 0))
    return pl.pallas_call(
        functools.partial(_hy_filter_kernel, seq_len=seq_len),
        out_shape=(jax.ShapeDtypeStruct((n, HY_WIDTH), F32),
                   jax.ShapeDtypeStruct((8, HY_WIDTH), F32)),
        grid=(n // tm,),
        in_specs=[pl.BlockSpec((tm, LANES), lambda i: (i, 0)),
                  const((LANES, hid)), const((1, hid)), const((hid, hid)), const((1, hid)),
                  const((hid, 2 * HY_WIDTH)), const((1, 2 * HY_WIDTH)), const((1, hid)),
                  const((1, HY_WIDTH))],
        out_specs=(pl.BlockSpec((tm, HY_WIDTH), lambda i: (i, 0)), const((8, HY_WIDTH))),
        compiler_params=_cparams("arbitrary"),
        name="hyena_taps",
    )(feat, w1p, b1.reshape(1, hid), w2, b2.reshape(1, hid), w3, b3.reshape(1, 2 * HY_WIDTH),
      freq.reshape(1, hid), deltas)


FFT_N2 = 256


def _dft_tables(n_total, n1_in):
    n1 = n_total // FFT_N2
    k1 = np.arange(n1)
    a = 2.0 * np.pi * np.outer(k1, np.arange(n1_in)) / n1
    lvl1 = np.concatenate([np.cos(a), -np.sin(a)], axis=0)
    n2 = np.arange(FFT_N2)
    th = 2.0 * np.pi * np.outer(n2, n2) / FFT_N2
    fc, fs = np.cos(th), np.sin(th)
    g = np.block([[fc, fs], [-fs, fc]])
    ph = 2.0 * np.pi * np.outer(k1, n2) / n_total
    tw = np.stack([np.cos(ph), np.sin(ph)])[..., None]
    return (jnp.asarray(lvl1, F32), jnp.asarray(g, F32), jnp.asarray(g.T, F32), jnp.asarray(tw, F32))


def _inverse_lvl1_table(n_total, n1_out):
    n1 = n_total // FFT_N2
    a = 2.0 * np.pi * np.outer(np.arange(n1_out), np.arange(n1)) / n1
    return jnp.asarray(np.concatenate([np.cos(a), -np.sin(a)], axis=1), F32)


def _lvl2_kernel(b_ref, tw_ref, gh_ref, gl_ref, *rest, inverse):
    c = tw_ref[0, 0]
    s = tw_ref[1, 0]
    br = b_ref[0, 0]
    bi = b_ref[1, 0]
    d = jnp.concatenate([c * br + s * bi, c * bi - s * br], axis=0)
    dh, dl = _split(d)
    x = _dot(gh_ref[...], dh) + _dot(gh_ref[...], dl) + _dot(gl_ref[...], dh)
    if not inverse:
        (o_ref,) = rest
        o_ref[0, 0] = x[:FFT_N2]
        o_ref[1, 0] = x[FFT_N2:]
        return
    h_ref, gth_ref, gtl_ref, o_ref = rest
    xr, xi = x[:FFT_N2], x[FFT_N2:]
    hr, hi = h_ref[0, 0], h_ref[1, 0]
    y = jnp.concatenate([xr * hr - xi * hi, xr * hi + xi * hr], axis=0)
    yh, yl = _split(y)
    cc = _dot(gth_ref[...], yh) + _dot(gth_ref[...], yl) + _dot(gtl_ref[...], yh)
    cr, ci = cc[:FFT_N2], cc[FFT_N2:]
    o_ref[0, 0] = c * cr - s * ci
    o_ref[1, 0] = c * ci + s * cr


def _level2(b, tw, g, h=None, gt=None):
    _, n1, n2, ch = b.shape
    blk = pl.BlockSpec((2, 1, n2, ch), lambda i: (0, i, 0, 0))
    const = pl.BlockSpec((2 * n2, 2 * n2), lambda i: (0, 0))
    gh, gl = _split(g)
    in_specs = [blk, pl.BlockSpec((2, 1, n2, 1), lambda i: (0, i, 0, 0)), const, const]
    args = [b, tw, gh, gl]
    if h is not None:
        gth, gtl = _split(gt)
        in_specs += [blk, const, const]
        args += [h, gth, gtl]
    return pl.pallas_call(
        functools.partial(_lvl2_kernel, inverse=h is not None),
        out_shape=jax.ShapeDtypeStruct(b.shape, F32),
        grid=(n1,),
        in_specs=in_specs,
        out_specs=blk,
        compiler_params=_cparams("parallel"),
        name="hyena_fft_level2" + ("_conv" if h is not None else ""),
    )(*args)


def _hyena_long_conv(z, taps):
    seq_len, ch = z.shape
    n = 2 * seq_len
    n1 = n // FFT_N2
    lvl1_z, g, gt, tw = _dft_tables(n, n1 // 2)
    lvl1_t = _dft_tables(n, n1)[0]
    inv1 = _inverse_lvl1_table(n, n1 // 2)
    cols = FFT_N2 * ch
    tn = min(cols, 4096)
    hb = _mm3(lvl1_t, taps.reshape(n1, cols), tn, "hyena_fft_level1_taps")
    hspec = _level2(hb.reshape(2, n1, FFT_N2, ch), tw, g)
    zb = _mm3(lvl1_z, z.reshape(n1 // 2, cols), tn, "hyena_fft_level1_z")
    cspec = _level2(zb.reshape(2, n1, FFT_N2, ch), tw, g, hspec, gt)
    conv = _mm3(inv1, cspec.reshape(2 * n1, cols), tn, "hyena_fft_inverse_level1")
    return conv.reshape(seq_len, ch)


def _small_conv_kernel(z_ref, taps_ref, fc_ref, fs_ref, o_ref):
    seq_len = z_ref.shape[0]
    fc = fc_ref[...]
    fs = fs_ref[...]
    z = z_ref[...]
    taps = taps_ref[...]
    zr = _dot3(fc[:, :seq_len], z)
    zi = -_dot3(fs[:, :seq_len], z)
    hr = _dot3(fc, taps)
    hi = -_dot3(fs, taps)
    yr = zr * hr - zi * hi
    yi = zr * hi + zi * hr
    o_ref[...] = _dot3(fc[:seq_len, :], yr) - _dot3(fs[:seq_len, :], yi)


def _hyena_small_conv(z, taps):
    seq_len, ch = z.shape
    n = 2 * seq_len
    th = 2.0 * np.pi * np.outer(np.arange(n), np.arange(n)) / n
    full = lambda shape: pl.BlockSpec(shape, lambda: (0, 0))
    return pl.pallas_call(
        _small_conv_kernel,
        out_shape=jax.ShapeDtypeStruct((seq_len, ch), F32),
        in_specs=[full((seq_len, ch)), full((n, ch)), full((n, n)), full((n, n))],
        out_specs=full((seq_len, ch)),
        compiler_params=pltpu.CompilerParams(vmem_limit_bytes=VMEM_LIMIT),
        name="hyena_small_conv",
    )(z, taps, jnp.asarray(np.cos(th), F32), jnp.asarray(np.sin(th), F32))


def _hy_post_kernel(x0_ref, z_ref, conv_ref, asum_ref, bias_ref, y_ref, *, n_total):
    norm = jnp.sum(asum_ref[...], axis=0, keepdims=True) + EPS
    conv = conv_ref[...] * (1.0 / (n_total * norm))
    y_ref[...] = (x0_ref[...] * (conv + bias_ref[...] * z_ref[...])).astype(y_ref.dtype)


def _hyena_post(x0, z, conv, asum, bias):
    t = x0.shape[0]
    tm = min(t, 1024)
    row = pl.BlockSpec((tm, HY_WIDTH), lambda i: (i, 0))
    return pl.pallas_call(
        functools.partial(_hy_post_kernel, n_total=2 * t),
        out_shape=jax.ShapeDtypeStruct((t, HY_WIDTH), BF16),
        grid=(t // tm,),
        in_specs=[row, row, row, pl.BlockSpec((8, HY_WIDTH), lambda i: (0, 0)),
                  pl.BlockSpec((1, HY_WIDTH), lambda i: (0, 0))],
        out_specs=row,
        compiler_params=_cparams("parallel"),
        name="hyena_post",
    )(x0, z, conv, asum, bias.reshape(1, HY_WIDTH))


def _hyena_mixer(p_hy, conv_w, conv_b, filt, bias):
    seq_len = p_hy.shape[0]
    x0, z = _hyena_pre(p_hy, conv_w, conv_b)
    taps, asum = _hyena_taps(seq_len, *filt)
    if 2 * seq_len // FFT_N2 >= 16:
        conv = _hyena_long_conv(z, taps)
    else:
        conv = _hyena_small_conv(z, taps)
    return _hyena_post(x0, z, conv, asum, bias)


def _outproj_kernel(hy_ref, ret_ref, att_ref, w_ref, x_ref, gate_ref, o_ref):
    y = (_dot(hy_ref[...], w_ref[:HY_WIDTH, :])
         + _dot(ret_ref[...], w_ref[HY_WIDTH:HY_WIDTH + RET_WIDTH, :])
         + _dot(att_ref[...], w_ref[HY_WIDTH + RET_WIDTH:, :]))
    o_ref[...] = x_ref[...] + gate_ref[...] * y


def _out_projection(y_hy, y_ret, y_att, w_bf16, x, gate):
    t = x.shape[0]
    tm = min(t, 512)
    row = lambda w: pl.BlockSpec((tm, w), lambda i: (i, 0))
    return pl.pallas_call(
        _outproj_kernel,
        out_shape=jax.ShapeDtypeStruct((t, D_MODEL), F32),
        grid=(t // tm,),
        in_specs=[row(HY_WIDTH), row(RET_WIDTH), row(ATT_WIDTH),
                  pl.BlockSpec((MIX_WIDTH, D_MODEL), lambda i: (0, 0)),
                  row(D_MODEL), pl.BlockSpec((1, D_MODEL), lambda i: (0, 0))],
        out_specs=row(D_MODEL),
        compiler_params=_cparams("parallel"),
        name="out_projection",
    )(y_hy, y_ret, y_att, w_bf16, x, gate)


def _ffn_kernel(x_ref, g_ref, sh_ref, sc_ref, gate_ref, wg_ref, wu_ref, wd_ref, o_ref,
                h_ref, acc_ref):
    k = pl.program_id(1)

    @pl.when(k == 0)
    def _():
        h_ref[...] = _norm_mod(x_ref[...], g_ref[...], sh_ref[...], sc_ref[...]).astype(BF16)
        acc_ref[...] = jnp.zeros_like(acc_ref)

    h = h_ref[...]
    a = _silu(_dot(h, wg_ref[...])) * _dot(h, wu_ref[...])
    acc_ref[...] += _dot(a.astype(BF16), wd_ref[...])

    @pl.when(k == pl.num_programs(1) - 1)
    def _():
        o_ref[...] = x_ref[...] + gate_ref[...] * acc_ref[...]


def _dense_ffn(x, g, shift, scale, gate, wg, wu, wd):
    t = x.shape[0]
    tm = min(t, 1024)
    th = 256
    vec = pl.BlockSpec((1, D_MODEL), lambda i, k: (0, 0))
    return pl.pallas_call(
        _ffn_kernel,
        out_shape=jax.ShapeDtypeStruct((t, D_MODEL), F32),
        grid=(t // tm, FFN_HIDDEN // th),
        in_specs=[pl.BlockSpec((tm, D_MODEL), lambda i, k: (i, 0)), vec, vec, vec, vec,
                  pl.BlockSpec((D_MODEL, th), lambda i, k: (0, k)),
                  pl.BlockSpec((D_MODEL, th), lambda i, k: (0, k)),
                  pl.BlockSpec((th, D_MODEL), lambda i, k: (k, 0))],
        out_specs=pl.BlockSpec((tm, D_MODEL), lambda i, k: (i, 0)),
        scratch_shapes=[pltpu.VMEM((tm, D_MODEL), BF16), pltpu.VMEM((tm, D_MODEL), F32)],
        compiler_params=_cparams("parallel", "arbitrary"),
        name="dense_ffn",
    )(x, g, shift, scale, gate, wg, wu, wd)


def _moe_route_kernel(x_ref, g_ref, sh_ref, sc_ref, rw_ref, rb_ref, h_ref, info_ref):
    h = _norm_mod(x_ref[...], g_ref[...], sh_ref[...], sc_ref[...])
    h_ref[...] = h
    logits = _dot3(h, rw_ref[...]) + rb_ref[...]
    lane = lax.broadcasted_iota(jnp.int32, logits.shape, 1)
    v1 = jnp.max(logits, axis=-1, keepdims=True)
    i1 = jnp.min(jnp.where(logits == v1, lane, LANES), axis=-1, keepdims=True)
    rest = jnp.where(lane == i1, -jnp.inf, logits)
    v2 = jnp.max(rest, axis=-1, keepdims=True)
    i2 = jnp.min(jnp.where(rest == v2, lane, LANES), axis=-1, keepdims=True)
    e = jnp.exp(v2 - v1)
    g1 = 1.0 / (1.0 + e)
    g2 = e * g1
    info_ref[...] = jnp.where(lane == 0, i1.astype(F32),
                              jnp.where(lane == 1, i2.astype(F32),
                                        jnp.where(lane == 2, g1, jnp.where(lane == 3, g2, 0.0))))


def _moe_route(x, g, shift, scale, router_w, router_b):
    t = x.shape[0]
    tm = min(t, 512)
    rw = jnp.concatenate([router_w, jnp.zeros((D_MODEL, LANES - N_EXPERTS), F32)], axis=1)
    rb = jnp.concatenate([router_b, jnp.full((LANES - N_EXPERTS,), -jnp.inf, F32)]).reshape(1, LANES)
    vec = pl.BlockSpec((1, D_MODEL), lambda i: (0, 0))
    return pl.pallas_call(
        _moe_route_kernel,
        out_shape=(jax.ShapeDtypeStruct((t, D_MODEL), F32), jax.ShapeDtypeStruct((t, LANES), F32)),
        grid=(t // tm,),
        in_specs=[pl.BlockSpec((tm, D_MODEL), lambda i: (i, 0)), vec, vec, vec,
                  pl.BlockSpec((D_MODEL, LANES), lambda i: (0, 0)),
                  pl.BlockSpec((1, LANES), lambda i: (0, 0))],
        out_specs=(pl.BlockSpec((tm, D_MODEL), lambda i: (i, 0)),
                   pl.BlockSpec((tm, LANES), lambda i: (i, 0))),
        compiler_params=_cparams("parallel"),
        name="moe_route",
    )(x, g, shift, scale, rw, rb)


def _row_copy(src_hbm, dst_vmem, sem, src_row, dst_row):
    return pltpu.make_async_copy(src_hbm.at[pl.ds(src_row, 1)], dst_vmem.at[pl.ds(dst_row, 1)], sem)


def _moe_expert_kernel(blk_e_ref, nused_ref, src_ref, h_hbm, wg_ref, wu_ref, wd_ref, y_ref,
                       xbuf_ref, xb_ref, acc_ref, sem):
    b = pl.program_id(0)
    k = pl.program_id(1)
    last = pl.num_programs(1) - 1
    used = b < nused_ref[0]

    @pl.when(jnp.logical_and(used, k == 0))
    def _():
        def issue(r, carry):
            _row_copy(h_hbm, xbuf_ref, sem, src_ref[b * MOE_BLOCK + r], r).start()
            return carry

        lax.fori_loop(0, MOE_BLOCK, issue, 0)

        def wait(r, carry):
            _row_copy(h_hbm, xbuf_ref, sem, 0, r).wait()
            return carry

        lax.fori_loop(0, MOE_BLOCK, wait, 0)
        xb_ref[...] = xbuf_ref[...].astype(BF16)
        acc_ref[...] = jnp.zeros_like(acc_ref)

    @pl.when(used)
    def _():
        x = xb_ref[...]
        a = _silu(_dot(x, wg_ref[0])) * _dot(x, wu_ref[0])
        acc_ref[...] += _dot(a.astype(BF16), wd_ref[0])

    @pl.when(jnp.logical_and(used, k == last))
    def _():
        y_ref[...] = acc_ref[...]

    @pl.when(jnp.logical_and(jnp.logical_not(used), k == last))
    def _():
        y_ref[...] = jnp.zeros_like(y_ref)


def _moe_experts(h, blk_expert, n_used, src_tok, wg, wu, wd, n_blk):
    th = 256
    grid_spec = pltpu.PrefetchScalarGridSpec(
        num_scalar_prefetch=3,
        grid=(n_blk, FFN_HIDDEN // th),
        in_specs=[pl.BlockSpec(memory_space=pl.ANY),
                  pl.BlockSpec((1, D_MODEL, th), lambda b, k, be, nu, st: (be[b], 0, k)),
                  pl.BlockSpec((1, D_MODEL, th), lambda b, k, be, nu, st: (be[b], 0, k)),
                  pl.BlockSpec((1, th, D_MODEL), lambda b, k, be, nu, st: (be[b], k, 0))],
        out_specs=pl.BlockSpec((MOE_BLOCK, D_MODEL), lambda b, k, be, nu, st: (b, 0)),
        scratch_shapes=[pltpu.VMEM((MOE_BLOCK, D_MODEL), F32),
                        pltpu.VMEM((MOE_BLOCK, D_MODEL), BF16),
                        pltpu.VMEM((MOE_BLOCK, D_MODEL), F32),
                        pltpu.SemaphoreType.DMA],
    )
    return pl.pallas_call(
        _moe_expert_kernel,
        out_shape=jax.ShapeDtypeStruct((n_blk * MOE_BLOCK, D_MODEL), F32),
        grid_spec=grid_spec,
        compiler_params=_cparams("arbitrary", "arbitrary"),
        name="moe_experts",
    )(blk_expert, n_used, src_tok, h, wg, wu, wd)


def _moe_combine_kernel(dest_ref, x_ref, info_ref, gate_ref, ng_ref, y_hbm, o_ref,
                        y1_ref, y2_ref, sem, *, final_norm):
    i = pl.program_id(0)
    tm = x_ref.shape[0]

    def issue(r, carry):
        a = 2 * (i * tm + r)
        _row_copy(y_hbm, y1_ref, sem, dest_ref[a], r).start()
        _row_copy(y_hbm, y2_ref, sem, dest_ref[a + 1], r).start()
        return carry

    lax.fori_loop(0, tm, issue, 0)

    def wait(r, carry):
        _row_copy(y_hbm, y1_ref, sem, 0, r).wait()
        _row_copy(y_hbm, y2_ref, sem, 0, r).wait()
        return carry

    lax.fori_loop(0, tm, wait, 0)
    info = info_ref[...]
    y = y1_ref[...] * info[:, 2:3] + y2_ref[...] * info[:, 3:4]
    x = x_ref[...] + gate_ref[...] * y
    if final_norm:
        ms = jnp.mean(x * x, axis=-1, keepdims=True)
        x = x * lax.rsqrt(ms + EPS) * ng_ref[...]
    o_ref[...] = x


def _moe_combine(dest, x, info, gate, norm_g, y, final_norm):
    t = x.shape[0]
    tm = min(t, 256)
    vec = pl.BlockSpec((1, D_MODEL), lambda i, d: (0, 0))
    grid_spec = pltpu.PrefetchScalarGridSpec(
        num_scalar_prefetch=1,
        grid=(t // tm,),
        in_specs=[pl.BlockSpec((tm, D_MODEL), lambda i, d: (i, 0)),
                  pl.BlockSpec((tm, LANES), lambda i, d: (i, 0)), vec, vec,
                  pl.BlockSpec(memory_space=pl.ANY)],
        out_specs=pl.BlockSpec((tm, D_MODEL), lambda i, d: (i, 0)),
        scratch_shapes=[pltpu.VMEM((tm, D_MODEL), F32), pltpu.VMEM((tm, D_MODEL), F32),
                        pltpu.SemaphoreType.DMA],
    )
    return pl.pallas_call(
        functools.partial(_moe_combine_kernel, final_norm=final_norm),
        out_shape=jax.ShapeDtypeStruct((t, D_MODEL), F32),
        grid_spec=grid_spec,
        compiler_params=_cparams("arbitrary"),
        name="moe_combine",
    )(dest, x, info, gate, norm_g, y)


def _moe_layer(x, g, shift, scale, gate, router_w, router_b, wg, wu, wd, norm_g, final_norm):
    t = x.shape[0]
    n_asg = t * TOP_K
    h, info = _moe_route(x, g, shift, scale, router_w, router_b)
    expert = info[:, :TOP_K].astype(jnp.int32).reshape(-1)
    onehot = (expert[:, None] == jnp.arange(N_EXPERTS, dtype=jnp.int32)[None, :]).astype(jnp.int32)
    csum = jnp.cumsum(onehot, axis=0)
    counts = csum[-1]
    padded = (counts + MOE_BLOCK - 1) // MOE_BLOCK * MOE_BLOCK
    pad_end = jnp.cumsum(padded)
    pad_start = pad_end - padded
    dest = jnp.sum(onehot * (csum - 1 + pad_start[None, :]), axis=1).astype(jnp.int32)
    n_blk = -(-n_asg // MOE_BLOCK) + N_EXPERTS
    blk_start = jnp.arange(n_blk, dtype=jnp.int32) * MOE_BLOCK
    blk_expert = jnp.minimum(jnp.sum(blk_start[:, None] >= pad_end[None, :], axis=1),
                             N_EXPERTS - 1).astype(jnp.int32)
    n_used = (pad_end[-1:] // MOE_BLOCK).astype(jnp.int32)
    src_tok = jnp.zeros((n_blk * MOE_BLOCK,), jnp.int32).at[dest].set(
        jnp.arange(n_asg, dtype=jnp.int32) // TOP_K)
    y = _moe_experts(h, blk_expert, n_used, src_tok, wg, wu, wd, n_blk)
    return _moe_combine(dest, x, info, gate, norm_g, y, final_norm)


def _final_norm_kernel(x_ref, g_ref, o_ref):
    x = x_ref[...]
    ms = jnp.mean(x * x, axis=-1, keepdims=True)
    o_ref[...] = x * lax.rsqrt(ms + EPS) * g_ref[...]


def _final_norm(x, g):
    t = x.shape[0]
    tm = min(t, 1024)
    return pl.pallas_call(
        _final_norm_kernel,
        out_shape=jax.ShapeDtypeStruct((t, D_MODEL), F32),
        grid=(t // tm,),
        in_specs=[pl.BlockSpec((tm, D_MODEL), lambda i: (i, 0)),
                  pl.BlockSpec((1, D_MODEL), lambda i: (0, 0))],
        out_specs=pl.BlockSpec((tm, D_MODEL), lambda i: (i, 0)),
        compiler_params=_cparams("parallel"),
        name="final_norm",
    )(x, g)


def kernel(x, c, ctx, c_ctx, ada_w, ada_b, norm1_g, norm2_g, w_in, w_out, hy_conv_w, hy_conv_b, hy_filt_w1, hy_filt_b1, hy_filt_w2, hy_filt_b2, hy_filt_w3, hy_filt_b3, hy_filt_freq, hy_bias, ret_log_rate, attn_q_g, attn_k_g, ffn_w_gate, ffn_w_up, ffn_w_down, moe_router_w, moe_router_b, moe_w_gate, moe_w_up, moe_w_down, final_norm_g):
    assert x.shape[0] == 1 and c.shape[0] == 1
    seq_len = x.shape[1]
    x_lat = x[0]
    x_ctx = ctx[0]
    cos_t, sin_t = _rope_tables(seq_len)
    cvec = jnp.concatenate([c, c_ctx[None, :], jnp.zeros((6, D_MODEL), F32)], axis=0)
    cvec = _silu(cvec)
    row = lambda v: v.reshape(1, D_MODEL)
    zero_state = jnp.zeros((2, RET_WIDTH, RET_HEAD_DIM), F32)
    zero_tab = jnp.zeros((x_ctx.shape[0], LANES), F32)

    for l in range(DEPTH):
        last = l == DEPTH - 1
        mods = _ada_modulation(cvec, ada_w[l], ada_b[l]).reshape(8, 6, D_MODEL)
        mod = [row(mods[0, i]) for i in range(6)]
        mod_c = [row(mods[1, i]) for i in range(6)]
        filt = (hy_filt_w1[l], hy_filt_b1[l], hy_filt_w2[l], hy_filt_b2[l],
                hy_filt_w3[l], hy_filt_b3[l], hy_filt_freq[l])
        lg_f = -jnp.exp(ret_log_rate[l, 0].astype(F32))
        lg_b = -jnp.exp(ret_log_rate[l, 1].astype(F32))
        mask, zeta, xi, chunk_decay = _ret_tables(lg_f, lg_b)
        w_in_l = w_in[l].astype(BF16)
        w_out_l = w_out[l].astype(BF16)
        n1 = row(norm1_g[l])
        n2 = row(norm2_g[l])

        p_hy, p_ret, p_att = _in_projection(x_lat, n1, mod[0], mod[1], w_in_l)
        pc_hy, pc_ret, pc_att = _in_projection(x_ctx, n1, mod_c[0], mod_c[1], w_in_l)

        sprev_c, s_ctx = _retention_states(pc_ret, zeta, chunk_decay, zero_state)
        qc, ktc, vc = _attention_prep(pc_att, zero_tab, zero_tab, attn_q_g[l], attn_k_g[l], rope=False)

        y_hy = _hyena_mixer(p_hy, hy_conv_w[l], hy_conv_b[l], filt, hy_bias[l])
        sprev, _ = _retention_states(p_ret, zeta, chunk_decay, s_ctx)
        y_ret = _retention_outputs(p_ret, mask, xi, sprev)
        q, kt, v = _attention_prep(p_att, cos_t, sin_t, attn_q_g[l], attn_k_g[l], rope=True)
        y_att = _attention(q, ktc, vc, kt, v, tq=128, tk=1024)

        if not last:
            yc_hy = _hyena_mixer(pc_hy, hy_conv_w[l], hy_conv_b[l], filt, hy_bias[l])
            yc_ret = _retention_outputs(pc_ret, mask, xi, sprev_c)
            yc_att = _attention(qc, ktc, vc, tq=x_ctx.shape[0], tk=0)
            x_ctx = _out_projection(yc_hy, yc_ret, yc_att, w_out_l, x_ctx, mod_c[2])
        x_lat = _out_projection(y_hy, y_ret, y_att, w_out_l, x_lat, mod[2])

        i = l // 2
        if l % 2 == 0:
            wg, wu, wd = (ffn_w_gate[i].astype(BF16), ffn_w_up[i].astype(BF16),
                          ffn_w_down[i].astype(BF16))
            x_lat = _dense_ffn(x_lat, n2, mod[3], mod[4], mod[5], wg, wu, wd)
            if not last:
                x_ctx = _dense_ffn(x_ctx, n2, mod_c[3], mod_c[4], mod_c[5], wg, wu, wd)
        else:
            wg, wu, wd = (moe_w_gate[i].astype(BF16), moe_w_up[i].astype(BF16),
                          moe_w_down[i].astype(BF16))
            fg = row(final_norm_g)
            x_lat = _moe_layer(x_lat, n2, mod[3], mod[4], mod[5], moe_router_w[i], moe_router_b[i],
                               wg, wu, wd, fg, final_norm=last)
            if not last:
                x_ctx = _moe_layer(x_ctx, n2, mod_c[3], mod_c[4], mod_c[5], moe_router_w[i],
                                   moe_router_b[i], wg, wu, wd, fg, final_norm=False)
    if DEPTH % 2 == 1:
        x_lat = _final_norm(x_lat, row(final_norm_g))
    return x_lat[None]
```

```python
import functools
import math

import numpy as np
import jax
import jax.numpy as jnp
from jax import lax
from jax.experimental import pallas as pl
from jax.experimental.pallas import tpu as pltpu

F32 = jnp.float32
BF16 = jnp.bfloat16

D_MODEL = 1024
DEPTH = 2
GRID_W = 64
EPS = 1e-6

HY_WIDTH = 256
HY_COLS = 3 * HY_WIDTH
HY_POS_BANDS = 16
HY_DECAY_TARGET = 1e-2
HY_SHORT_DECAY_PCT = 0.3
HY_LONG_DECAY_PCT = 1.5

RET_HEAD_DIM = 64
RET_HEADS = 4
RET_WIDTH = RET_HEADS * RET_HEAD_DIM
RET_COLS = 4 * RET_WIDTH
RET_CHUNK = 128

ATT_HEAD_DIM = 64
ATT_HEADS = 8
ATT_KV_HEADS = 2
ATT_GROUP = ATT_HEADS // ATT_KV_HEADS
ATT_WIDTH = ATT_HEADS * ATT_HEAD_DIM
ATT_KV_WIDTH = ATT_KV_HEADS * ATT_HEAD_DIM
ATT_COLS = ATT_WIDTH + 2 * ATT_KV_WIDTH
ROPE_AXIS_DIM = ATT_HEAD_DIM // 2
ROPE_BASE = 10000.0

MIX_WIDTH = HY_WIDTH + RET_WIDTH + ATT_WIDTH
IN_COLS = HY_COLS + RET_COLS + ATT_COLS

FFN_HIDDEN = 2816
N_EXPERTS = 8
TOP_K = 2
MOE_BLOCK = 512

LOG2_E = 1.4426950408889634
EXP2_CAP = 60.0
ATT_TQ = 256
ATT_TK = 1024
ATT_SUB = 256

LANES = 128
VMEM_LIMIT = 56 * 1024 * 1024


def _cparams(*sem):
    return pltpu.CompilerParams(dimension_semantics=sem, vmem_limit_bytes=VMEM_LIMIT)


def _dot(a, b):
    return jnp.dot(a, b, preferred_element_type=F32)


def _split(a):
    hi = a.astype(BF16)
    lo = (a - hi.astype(F32)).astype(BF16)
    return hi, lo


def _dot3(a, b):
    ah, al = _split(a)
    bh, bl = _split(b)
    return _dot(ah, bh) + _dot(al, bh) + _dot(ah, bl)


def _silu(x):
    return x * (1.0 / (1.0 + jnp.exp(-x)))


def _norm_mod(x, g, shift, scale):
    ms = jnp.mean(x * x, axis=-1, keepdims=True)
    h = x * lax.rsqrt(ms + EPS) * g
    return h * (1.0 + scale) + shift


def _mm3_kernel(a_ref, b_ref, o_ref):
    o_ref[...] = _dot3(a_ref[...], b_ref[...])


def _mm3(a, b, tn, name):
    m, k = a.shape
    n = b.shape[1]
    return pl.pallas_call(
        _mm3_kernel,
        out_shape=jax.ShapeDtypeStruct((m, n), F32),
        grid=(n // tn,),
        in_specs=[pl.BlockSpec((m, k), lambda j: (0, 0)),
                  pl.BlockSpec((k, tn), lambda j: (0, j))],
        out_specs=pl.BlockSpec((m, tn), lambda j: (0, j)),
        compiler_params=_cparams("parallel"),
        name=name,
    )(a, b)


def _ada_kernel(c_ref, w_ref, b_ref, o_ref):
    o_ref[...] = _dot3(c_ref[...], w_ref[...]) + b_ref[...]


def _ada_modulation(cc, w, b):
    n = w.shape[1]
    tn = 1536
    return pl.pallas_call(
        _ada_kernel,
        out_shape=jax.ShapeDtypeStruct((8, n), F32),
        grid=(n // tn,),
        in_specs=[pl.BlockSpec((8, D_MODEL), lambda j: (0, 0)),
                  pl.BlockSpec((D_MODEL, tn), lambda j: (0, j)),
                  pl.BlockSpec((1, tn), lambda j: (0, j))],
        out_specs=pl.BlockSpec((8, tn), lambda j: (0, j)),
        compiler_params=_cparams("parallel"),
        name="ada_modulation",
    )(cc, w, b.reshape(1, n))


def _inproj_kernel(x_ref, g_ref, sh_ref, sc_ref, w_ref, hy_ref, ret_ref, att_ref):
    h = _norm_mod(x_ref[...], g_ref[...], sh_ref[...], sc_ref[...])
    p = _dot(h.astype(BF16), w_ref[...])
    hy_ref[...] = p[:, :HY_COLS]
    ret_ref[...] = p[:, HY_COLS:HY_COLS + RET_COLS]
    att_ref[...] = p[:, HY_COLS + RET_COLS:]


def _in_projection(x, g, shift, scale, w_bf16):
    t = x.shape[0]
    tm = min(t, 512)
    vec = pl.BlockSpec((1, D_MODEL), lambda i: (0, 0))
    return pl.pallas_call(
        _inproj_kernel,
        out_shape=(jax.ShapeDtypeStruct((t, HY_COLS), F32),
                   jax.ShapeDtypeStruct((t, RET_COLS), F32),
                   jax.ShapeDtypeStruct((t, ATT_COLS), F32)),
        grid=(t // tm,),
        in_specs=[pl.BlockSpec((tm, D_MODEL), lambda i: (i, 0)), vec, vec, vec,
                  pl.BlockSpec((D_MODEL, IN_COLS), lambda i: (0, 0))],
        out_specs=(pl.BlockSpec((tm, HY_COLS), lambda i: (i, 0)),
                   pl.BlockSpec((tm, RET_COLS), lambda i: (i, 0)),
                   pl.BlockSpec((tm, ATT_COLS), lambda i: (i, 0))),
        compiler_params=_cparams("parallel"),
        name="in_projection",
    )(x, g, shift, scale, w_bf16)


def _head_mean_matrix(width):
    idx = np.arange(width) // ATT_HEAD_DIM
    return jnp.asarray((idx[:, None] == idx[None, :]).astype(np.float32) / ATT_HEAD_DIM, BF16)


def _head_rms(x, bd, g):
    x2h, x2l = _split(x * x)
    ms = _dot(x2h, bd) + _dot(x2l, bd)
    return x * lax.rsqrt(ms + EPS) * g


def _rope(x, cos, sin):
    n = x.shape[1]
    lane = lax.broadcasted_iota(jnp.int32, x.shape, 1)
    swapped = jnp.where((lane % 32) < 16, pltpu.roll(x, n - 16, 1), pltpu.roll(x, 16, 1))
    return x * cos + swapped * sin


def _attprep_kernel(p_ref, cos_ref, sin_ref, qg_ref, kg_ref, bdq_ref, bdk_ref,
                    q_ref, kt_ref, v_ref, ksq_ref, *, rope):
    p = p_ref[...]
    q = _head_rms(p[:, :ATT_WIDTH], bdq_ref[...], qg_ref[...])
    k = _head_rms(p[:, ATT_WIDTH:ATT_WIDTH + ATT_KV_WIDTH], bdk_ref[...], kg_ref[...])
    if rope:
        cos = cos_ref[...]
        sin = sin_ref[...]
        q = jnp.concatenate(
            [_rope(q[:, j * LANES:(j + 1) * LANES], cos, sin) for j in range(ATT_WIDTH // LANES)],
            axis=1)
        k = _rope(k, cos, sin)
    q_ref[...] = (q * (ATT_HEAD_DIM ** -0.5 * LOG2_E)).astype(BF16)
    kt = k.T.astype(BF16)
    tm = kt.shape[1]
    sub = lax.broadcasted_iota(jnp.int32, (ATT_HEAD_DIM, tm), 0)
    minus_one_row = jnp.where(sub == 0, -1.0, 0.0).astype(BF16)
    sq = kt.astype(F32) ** 2
    norms = []
    for h in range(ATT_KV_HEADS):
        rows = slice(h * ATT_HEAD_DIM, (h + 1) * ATT_HEAD_DIM)
        kt_ref[h * LANES:h * LANES + ATT_HEAD_DIM, :] = kt[rows]
        kt_ref[h * LANES + ATT_HEAD_DIM:(h + 1) * LANES, :] = minus_one_row
        norms.append(jnp.sum(sq[rows], axis=0, keepdims=True))
    ksq_ref[...] = jnp.concatenate(norms + [jnp.zeros((8 - ATT_KV_HEADS, tm), F32)], axis=0)
    pv = p[:, ATT_WIDTH + ATT_KV_WIDTH:]
    lane = lax.broadcasted_iota(jnp.int32, pv.shape, 1)
    ones_col = jnp.where(lane == ATT_HEAD_DIM, 1.0, 0.0)
    v_ref[0] = jnp.where(lane < ATT_HEAD_DIM, pv, ones_col).astype(BF16)
    v_ref[1] = jnp.where(lane < ATT_HEAD_DIM, pltpu.roll(pv, ATT_HEAD_DIM, 1), ones_col).astype(BF16)


def _attention_prep(p_att, cos, sin, q_g, k_g, rope):
    t = p_att.shape[0]
    tm = min(t, 512)
    qg = jnp.tile(q_g, ATT_HEADS).reshape(1, ATT_WIDTH)
    kg = jnp.tile(k_g, ATT_KV_HEADS).reshape(1, ATT_KV_WIDTH)
    const = lambda shape: pl.BlockSpec(shape, lambda i: (0, 0))
    return pl.pallas_call(
        functools.partial(_attprep_kernel, rope=rope),
        out_shape=(jax.ShapeDtypeStruct((t, ATT_WIDTH), BF16),
                   jax.ShapeDtypeStruct((ATT_KV_HEADS * LANES, t), BF16),
                   jax.ShapeDtypeStruct((ATT_KV_HEADS, t, LANES), BF16),
                   jax.ShapeDtypeStruct((8, t), F32)),
        grid=(t // tm,),
        in_specs=[pl.BlockSpec((tm, ATT_COLS), lambda i: (i, 0)),
                  pl.BlockSpec((tm, LANES), lambda i: (i, 0)),
                  pl.BlockSpec((tm, LANES), lambda i: (i, 0)),
                  const((1, ATT_WIDTH)), const((1, ATT_KV_WIDTH)),
                  const((ATT_WIDTH, ATT_WIDTH)), const((ATT_KV_WIDTH, ATT_KV_WIDTH))],
        out_specs=(pl.BlockSpec((tm, ATT_WIDTH), lambda i: (i, 0)),
                   pl.BlockSpec((ATT_KV_HEADS * LANES, tm), lambda i: (0, i)),
                   pl.BlockSpec((ATT_KV_HEADS, tm, LANES), lambda i: (0, i, 0)),
                   pl.BlockSpec((8, tm), lambda i: (0, i))),
        compiler_params=_cparams("parallel"),
        name="attention_prep",
    )(p_att, cos, sin, qg, kg, _head_mean_matrix(ATT_WIDTH), _head_mean_matrix(ATT_KV_WIDTH))


def _rope_tables(n_tokens):
    n = np.arange(n_tokens)
    inv_freq = ROPE_BASE ** (-np.arange(0, ROPE_AXIS_DIM, 2, dtype=np.float64) / ROPE_AXIS_DIM)
    inv_freq = inv_freq.astype(np.float32)
    ang_r = jnp.asarray((n // GRID_W).astype(np.float32))[:, None] * jnp.asarray(inv_freq)[None, :]
    ang_c = jnp.asarray((n % GRID_W).astype(np.float32))[:, None] * jnp.asarray(inv_freq)[None, :]
    cos = jnp.concatenate([jnp.cos(ang_r)] * 2 + [jnp.cos(ang_c)] * 2, axis=1)
    sin = jnp.concatenate([-jnp.sin(ang_r), jnp.sin(ang_r), -jnp.sin(ang_c), jnp.sin(ang_c)], axis=1)
    return jnp.tile(cos, (1, 2)), jnp.tile(sin, (1, 2))


def _attention_kernel(*refs, tq, tk, sub, n_lat):
    if n_lat:
        kmax_ref, q_ref, ktc_ref, vc_ref, kt_ref, v_ref, o_ref, qs_ref, acc_ref, m_ref = refs
    else:
        q_ref, ktc_ref, vc_ref, o_ref, acc_ref = refs
    rows = ATT_GROUP * tq
    q = q_ref[...]
    qst = jnp.concatenate(
        [q[:, g * ATT_HEAD_DIM:(g + 1) * ATT_HEAD_DIM] for g in range(ATT_GROUP)], axis=0)
    pad = jnp.zeros((rows, LANES - ATT_HEAD_DIM), BF16)
    s = _dot(jnp.concatenate([qst, pad], axis=1), ktc_ref[...])
    mt = jnp.max(s, axis=-1, keepdims=True).astype(BF16)
    mtf = mt.astype(F32)
    acc_ref[...] = _dot(jnp.exp2(s - mtf).astype(BF16), vc_ref[0])

    if n_lat:
        lane = lax.broadcasted_iota(jnp.int32, pad.shape, 1)
        qs_ref[...] = jnp.concatenate([qst, jnp.where(lane == 0, mtf, 0.0).astype(BF16)], axis=1)
        qf = qst.astype(F32)
        qn = jnp.sqrt(jnp.sum(qf * qf, axis=-1, keepdims=True))
        fast = jnp.max(qn * kmax_ref[pl.program_id(0)] - mtf) <= EXP2_CAP

        def tile(j):
            off = pl.multiple_of(j * tk, tk)
            return kt_ref[:, pl.ds(off, tk)], v_ref[0, pl.ds(off, tk), :]

        @pl.when(fast)
        def _():
            def body(j, carry):
                kt, v = tile(j)
                qs = qs_ref[...]
                part = None
                for c in range(tk // sub):
                    p = jnp.exp2(_dot(qs, kt[:, c * sub:(c + 1) * sub])).astype(BF16)
                    pv = _dot(p, v[c * sub:(c + 1) * sub])
                    part = pv if part is None else part + pv
                acc_ref[...] += part
                return carry

            lax.fori_loop(0, n_lat, body, 0)

        @pl.when(jnp.logical_not(fast))
        def _():
            m_ref[...] = jnp.zeros_like(m_ref)

            def body(j, carry):
                kt, v = tile(j)
                s = _dot(qs_ref[...], kt)
                m_old = m_ref[...]
                m_new = jnp.maximum(m_old, jnp.max(s, axis=-1, keepdims=True))
                p = jnp.exp2(s - m_new).astype(BF16)
                acc_ref[...] = jnp.exp2(m_old - m_new) * acc_ref[...] + _dot(p, v)
                m_ref[...] = m_new
                return carry

            lax.fori_loop(0, n_lat, body, 0)

    acc = acc_ref[...]
    o = acc[:, :ATT_HEAD_DIM] / acc[:, ATT_HEAD_DIM:ATT_HEAD_DIM + 1]
    o_ref[...] = jnp.concatenate(
        [o[g * tq:(g + 1) * tq] for g in range(ATT_GROUP)], axis=1).astype(o_ref.dtype)


def _attention(q, ktc, vc, kmax=None, kt=None, v=None, *, tq, tk=0, sub=0):
    t = q.shape[0]
    lc = ktc.shape[1]
    gw = ATT_GROUP * ATT_HEAD_DIM
    rows = ATT_GROUP * tq
    in_specs = [pl.BlockSpec((tq, gw), lambda h, i: (i, h)),
                pl.BlockSpec((LANES, lc), lambda h, i: (h, 0)),
                pl.BlockSpec((1, lc, LANES), lambda h, i: (h, 0, 0))]
    args = [q, ktc, vc]
    scratch = [pltpu.VMEM((rows, LANES), F32)]
    n_lat = 0
    if kt is not None:
        lk = kt.shape[1]
        n_lat = lk // tk
        in_specs = ([pl.BlockSpec(memory_space=pltpu.SMEM)] + in_specs
                    + [pl.BlockSpec((LANES, lk), lambda h, i: (h, 0)),
                       pl.BlockSpec((1, lk, LANES), lambda h, i: (h, 0, 0))])
        args = [kmax] + args + [kt, v]
        scratch = [pltpu.VMEM((rows, LANES), BF16)] + scratch + [pltpu.VMEM((rows, 1), F32)]
    return pl.pallas_call(
        functools.partial(_attention_kernel, tq=tq, tk=tk, sub=sub, n_lat=n_lat),
        out_shape=jax.ShapeDtypeStruct((t, ATT_WIDTH), BF16),
        grid=(ATT_KV_HEADS, t // tq),
        in_specs=in_specs,
        out_specs=pl.BlockSpec((tq, gw), lambda h, i: (i, h)),
        scratch_shapes=scratch,
        compiler_params=_cparams("parallel", "parallel"),
        name="attention_lat" if n_lat else "attention_ctx",
    )(*args)


def _ret_tables(lg_f, lg_b):
    pos = jnp.arange(RET_CHUNK, dtype=F32)
    rel = pos[:, None] - pos[None, :]
    mask = jnp.where(rel > 0, jnp.exp(lg_f[:, None, None] * jnp.maximum(rel, 0.0)),
                     jnp.where(rel < 0, jnp.exp(lg_b[:, None, None] * jnp.maximum(-rel, 0.0)), 2.0))
    zeta = jnp.stack([jnp.exp(lg_f[:, None] * (RET_CHUNK - 1 - pos)[None, :]),
                      jnp.exp(lg_b[:, None] * pos[None, :])])
    xi = jnp.stack([jnp.exp(lg_f[:, None] * (pos + 1)[None, :]),
                    jnp.exp(lg_b[:, None] * (RET_CHUNK - pos)[None, :])])
    chunk_decay = jnp.stack([jnp.exp(lg_f * RET_CHUNK), jnp.exp(lg_b * RET_CHUNK)])
    zeta = jnp.repeat(zeta, RET_HEAD_DIM, axis=1).transpose(0, 2, 1)
    xi = jnp.repeat(xi, RET_HEAD_DIM, axis=1).transpose(0, 2, 1)
    chunk_decay = jnp.repeat(chunk_decay, RET_HEAD_DIM, axis=1)[:, :, None]
    return mask, zeta, xi, chunk_decay


def _ret_state_kernel(k_ref, v_ref, zeta_ref, cd_ref, s0_ref, sprev_ref, sfin_ref, s_ref):
    i = pl.program_id(1)

    @pl.when(i == 0)
    def _():
        s_ref[...] = s0_ref[0]

    s = s_ref[...]
    sprev_ref[0, 0] = s
    kz = (k_ref[...] * (RET_HEAD_DIM ** -0.5) * zeta_ref[0]).astype(BF16)
    v = v_ref[...].astype(BF16)
    us = []
    for h in range(RET_HEADS):
        sl = slice(h * RET_HEAD_DIM, (h + 1) * RET_HEAD_DIM)
        us.append(lax.dot_general(kz[:, sl], v[:, sl], (((0,), (0,)), ((), ())),
                                  preferred_element_type=F32))
    s_new = cd_ref[0] * s + jnp.concatenate(us, axis=0)
    s_ref[...] = s_new
    sfin_ref[0] = s_new


def _retention_states(p_ret, zeta, chunk_decay, s0):
    nc = p_ret.shape[0] // RET_CHUNK
    wb = RET_WIDTH // RET_WIDTH

    def chunk(d, i):
        return i + d * (nc - 1 - 2 * i)

    return pl.pallas_call(
        _ret_state_kernel,
        out_shape=(jax.ShapeDtypeStruct((2, nc, RET_WIDTH, RET_HEAD_DIM), F32),
                   jax.ShapeDtypeStruct((2, RET_WIDTH, RET_HEAD_DIM), F32)),
        grid=(2, nc),
        in_specs=[pl.BlockSpec((RET_CHUNK, RET_WIDTH), lambda d, i: (chunk(d, i), 1 * wb)),
                  pl.BlockSpec((RET_CHUNK, RET_WIDTH), lambda d, i: (chunk(d, i), 2 * wb)),
                  pl.BlockSpec((1, RET_CHUNK, RET_WIDTH), lambda d, i: (d, 0, 0)),
                  pl.BlockSpec((1, RET_WIDTH, 1), lambda d, i: (d, 0, 0)),
                  pl.BlockSpec((1, RET_WIDTH, RET_HEAD_DIM), lambda d, i: (d, 0, 0))],
        out_specs=(pl.BlockSpec((1, 1, RET_WIDTH, RET_HEAD_DIM), lambda d, i: (d, chunk(d, i), 0, 0)),
                   pl.BlockSpec((1, RET_WIDTH, RET_HEAD_DIM), lambda d, i: (d, 0, 0))),
        scratch_shapes=[pltpu.VMEM((RET_WIDTH, RET_HEAD_DIM), F32)],
        compiler_params=_cparams("arbitrary", "arbitrary"),
        name="retention_states",
    )(p_ret, p_ret, zeta, chunk_decay, s0)


def _ret_out_kernel(q_ref, k_ref, v_ref, g_ref, mask_ref, xi_ref, sf_ref, sb_ref, y_ref):
    q = q_ref[...]
    k = (k_ref[...] * (RET_HEAD_DIM ** -0.5)).astype(BF16)
    v = v_ref[...].astype(BF16)
    qb = q.astype(BF16)
    qf = (q * xi_ref[0]).astype(BF16)
    qr = (q * xi_ref[1]).astype(BF16)
    sf = sf_ref[0, 0].astype(BF16)
    sb = sb_ref[0, 0].astype(BF16)
    ys = []
    for h in range(RET_HEADS):
        sl = slice(h * RET_HEAD_DIM, (h + 1) * RET_HEAD_DIM)
        scores = lax.dot_general(qb[:, sl], k[:, sl], (((1,), (1,)), ((), ())),
                                 preferred_element_type=F32)
        y = _dot((scores * mask_ref[h]).astype(BF16), v[:, sl])
        y = y + _dot(qf[:, sl], sf[sl, :]) + _dot(qr[:, sl], sb[sl, :])
        mu = jnp.mean(y, axis=-1, keepdims=True)
        yc = y - mu
        var = jnp.mean(yc * yc, axis=-1, keepdims=True)
        ys.append(yc * lax.rsqrt(var + EPS))
    y_ref[...] = (_silu(g_ref[...]) * jnp.concatenate(ys, axis=1)).astype(y_ref.dtype)


def _retention_outputs(p_ret, mask, xi, sprev):
    nc = p_ret.shape[0] // RET_CHUNK
    col = lambda c: pl.BlockSpec((RET_CHUNK, RET_WIDTH), lambda i: (i, c))
    return pl.pallas_call(
        _ret_out_kernel,
        out_shape=jax.ShapeDtypeStruct((p_ret.shape[0], RET_WIDTH), BF16),
        grid=(nc,),
        in_specs=[col(0), col(1), col(2), col(3),
                  pl.BlockSpec((RET_HEADS, RET_CHUNK, RET_CHUNK), lambda i: (0, 0, 0)),
                  pl.BlockSpec((2, RET_CHUNK, RET_WIDTH), lambda i: (0, 0, 0)),
                  pl.BlockSpec((1, 1, RET_WIDTH, RET_HEAD_DIM), lambda i: (0, i, 0, 0)),
                  pl.BlockSpec((1, 1, RET_WIDTH, RET_HEAD_DIM), lambda i: (1, i, 0, 0))],
        out_specs=pl.BlockSpec((RET_CHUNK, RET_WIDTH), lambda i: (i, 0)),
        compiler_params=_cparams("parallel"),
        name="retention_outputs",
    )(p_ret, p_ret, p_ret, p_ret, mask, xi, sprev, sprev)


def _hy_pre_kernel(p_ref, prev_ref, next_ref, w_ref, b_ref, x0_ref, z_ref):
    i = pl.program_id(0)
    x = p_ref[...]
    tm = x.shape[0]
    row = lax.broadcasted_iota(jnp.int32, x.shape, 0)
    prev_row = jnp.where(i == 0, 0.0, prev_ref[7:8, :])
    next_row = jnp.where(i == pl.num_programs(0) - 1, 0.0, next_ref[0:1, :])
    up = jnp.where(row == 0, prev_row, pltpu.roll(x, 1, 0))
    dn = jnp.where(row == tm - 1, next_row, pltpu.roll(x, tm - 1, 0))
    w = w_ref[...]
    u = up * w[0:1] + x * w[1:2] + dn * w[2:3] + b_ref[...]
    x0_ref[...] = u[:, :HY_WIDTH]
    z_ref[...] = u[:, HY_WIDTH:2 * HY_WIDTH] * u[:, 2 * HY_WIDTH:]


def _hyena_pre(p_hy, conv_w, conv_b):
    t = p_hy.shape[0]
    tm = min(t, 512)
    nb8 = tm // 8
    last8 = t // 8 - 1
    return pl.pallas_call(
        _hy_pre_kernel,
        out_shape=(jax.ShapeDtypeStruct((t, HY_WIDTH), F32),
                   jax.ShapeDtypeStruct((t, HY_WIDTH), F32)),
        grid=(t // tm,),
        in_specs=[pl.BlockSpec((tm, HY_COLS), lambda i: (i, 0)),
                  pl.BlockSpec((8, HY_COLS), lambda i: (jnp.maximum(i * nb8 - 1, 0), 0)),
                  pl.BlockSpec((8, HY_COLS), lambda i: (jnp.minimum((i + 1) * nb8, last8), 0)),
                  pl.BlockSpec((3, HY_COLS), lambda i: (0, 0)),
                  pl.BlockSpec((1, HY_COLS), lambda i: (0, 0))],
        out_specs=(pl.BlockSpec((tm, HY_WIDTH), lambda i: (i, 0)),
                   pl.BlockSpec((tm, HY_WIDTH), lambda i: (i, 0))),
        compiler_params=_cparams("parallel"),
        name="hyena_pre",
    )(p_hy, p_hy, p_hy, conv_w, conv_b.reshape(1, HY_COLS))


def _hy_filter_kernel(feat_ref, w1_ref, b1_ref, w2_ref, b2_ref, w3_ref, b3_ref, freq_ref,
                      delta_ref, taps_ref, asum_ref, *, seq_len):
    i = pl.program_id(0)
    feat = feat_ref[...]
    freq = freq_ref[...]
    h = jnp.sin(freq * (_dot3(feat, w1_ref[...]) + b1_ref[...]))
    h = jnp.sin(freq * (_dot3(h, w2_ref[...]) + b2_ref[...]))
    h = _dot3(h, w3_ref[...]) + b3_ref[...]
    tm = feat.shape[0]
    j = i * tm + lax.broadcasted_iota(jnp.int32, (tm, HY_WIDTH), 0)
    t = feat[:, 0:1]
    hsel = jnp.where(j < seq_len, h[:, :HY_WIDTH], h[:, HY_WIDTH:])
    taps = jnp.where(j == seq_len, 0.0, hsel * jnp.exp(-t * delta_ref[...]))
    taps_ref[...] = taps

    @pl.when(i == 0)
    def _():
        asum_ref[...] = jnp.zeros_like(asum_ref)

    asum_ref[...] += jnp.sum(jnp.abs(taps).reshape(tm // 8, 8, HY_WIDTH), axis=0)


def _hyena_taps(seq_len, w1, b1, w2, b2, w3, b3, freq):
    n = 2 * seq_len
    j = np.arange(n)
    pos = np.where(j <= seq_len, j, n - j).astype(np.float32)
    pos = jnp.asarray(pos)
    tt = pos / max(seq_len - 1, 1)
    bands = jnp.linspace(1e-4, HY_POS_BANDS - 1, HY_POS_BANDS, dtype=F32)
    ang = (2.0 * math.pi / seq_len) * pos[:, None] * bands[None, :]
    feat = jnp.concatenate([tt[:, None], jnp.cos(ang), -jnp.sin(ang),
                            jnp.zeros((n, LANES - 1 - 2 * HY_POS_BANDS), F32)], axis=-1)
    w1p = jnp.concatenate([w1, jnp.zeros((LANES - w1.shape[0], w1.shape[1]), F32)], axis=0)
    deltas = jnp.abs(jnp.linspace(math.log(HY_DECAY_TARGET) / HY_LONG_DECAY_PCT,
                                  math.log(HY_DECAY_TARGET) / HY_SHORT_DECAY_PCT,
                                  HY_WIDTH, dtype=F32)).reshape(1, HY_WIDTH)
    tm = min(n, 1024)
    hid = w1.shape[1]
    const = lambda shape: pl.BlockSpec(shape, lambda i: (0, 0))
    return pl.pallas_call(
        functools.partial(_hy_filter_kernel, seq_len=seq_len),
        out_shape=(jax.ShapeDtypeStruct((n, HY_WIDTH), F32),
                   jax.ShapeDtypeStruct((8, HY_WIDTH), F32)),
        grid=(n // tm,),
        in_specs=[pl.BlockSpec((tm, LANES), lambda i: (i, 0)),
                  const((LANES, hid)), const((1, hid)), const((hid, hid)), const((1, hid)),
                  const((hid, 2 * HY_WIDTH)), const((1, 2 * HY_WIDTH)), const((1, hid)),
                  const((1, HY_WIDTH))],
        out_specs=(pl.BlockSpec((tm, HY_WIDTH), lambda i: (i, 0)), const((8, HY_WIDTH))),
        compiler_params=_cparams("arbitrary"),
        name="hyena_taps",
    )(feat, w1p, b1.reshape(1, hid), w2, b2.reshape(1, hid), w3, b3.reshape(1, 2 * HY_WIDTH),
      freq.reshape(1, hid), deltas)


FFT_N2 = 256


def _dft_tables(n_total, n1_in):
    n1 = n_total // FFT_N2
    k1 = np.arange(n1)
    a = 2.0 * np.pi * np.outer(k1, np.arange(n1_in)) / n1
    lvl1 = np.concatenate([np.cos(a), -np.sin(a)], axis=0)
    n2 = np.arange(FFT_N2)
    th = 2.0 * np.pi * np.outer(n2, n2) / FFT_N2
    fc, fs = np.cos(th), np.sin(th)
    g = np.block([[fc, fs], [-fs, fc]])
    ph = 2.0 * np.pi * np.outer(k1, n2) / n_total
    tw = np.stack([np.cos(ph), np.sin(ph)])[..., None]
    return (jnp.asarray(lvl1, F32), jnp.asarray(g, F32), jnp.asarray(g.T, F32), jnp.asarray(tw, F32))


def _inverse_lvl1_table(n_total, n1_out):
    n1 = n_total // FFT_N2
    a = 2.0 * np.pi * np.outer(np.arange(n1_out), np.arange(n1)) / n1
    return jnp.asarray(np.concatenate([np.cos(a), -np.sin(a)], axis=1), F32)


def _lvl2_kernel(b_ref, tw_ref, gh_ref, gl_ref, *rest, inverse):
    c = tw_ref[0, 0]
    s = tw_ref[1, 0]
    br = b_ref[0, 0]
    bi = b_ref[1, 0]
    d = jnp.concatenate([c * br + s * bi, c * bi - s * br], axis=0)
    dh, dl = _split(d)
    x = _dot(gh_ref[...], dh) + _dot(gh_ref[...], dl) + _dot(gl_ref[...], dh)
    if not inverse:
        (o_ref,) = rest
        o_ref[0, 0] = x[:FFT_N2]
        o_ref[1, 0] = x[FFT_N2:]
        return
    h_ref, gth_ref, gtl_ref, o_ref = rest
    xr, xi = x[:FFT_N2], x[FFT_N2:]
    hr, hi = h_ref[0, 0], h_ref[1, 0]
    y = jnp.concatenate([xr * hr - xi * hi, xr * hi + xi * hr], axis=0)
    yh, yl = _split(y)
    cc = _dot(gth_ref[...], yh) + _dot(gth_ref[...], yl) + _dot(gtl_ref[...], yh)
    cr, ci = cc[:FFT_N2], cc[FFT_N2:]
    o_ref[0, 0] = c * cr - s * ci
    o_ref[1, 0] = c * ci + s * cr


def _level2(b, tw, g, h=None, gt=None):
    _, n1, n2, ch = b.shape
    blk = pl.BlockSpec((2, 1, n2, ch), lambda i: (0, i, 0, 0))
    const = pl.BlockSpec((2 * n2, 2 * n2), lambda i: (0, 0))
    gh, gl = _split(g)
    in_specs = [blk, pl.BlockSpec((2, 1, n2, 1), lambda i: (0, i, 0, 0)), const, const]
    args = [b, tw, gh, gl]
    if h is not None:
        gth, gtl = _split(gt)
        in_specs += [blk, const, const]
        args += [h, gth, gtl]
    return pl.pallas_call(
        functools.partial(_lvl2_kernel, inverse=h is not None),
        out_shape=jax.ShapeDtypeStruct(b.shape, F32),
        grid=(n1,),
        in_specs=in_specs,
        out_specs=blk,
        compiler_params=_cparams("parallel"),
        name="hyena_fft_level2" + ("_conv" if h is not None else ""),
    )(*args)


def _hyena_long_conv(z, taps):
    seq_len, ch = z.shape
    n = 2 * seq_len
    n1 = n // FFT_N2
    lvl1_z, g, gt, tw = _dft_tables(n, n1 // 2)
    lvl1_t = _dft_tables(n, n1)[0]
    inv1 = _inverse_lvl1_table(n, n1 // 2)
    cols = FFT_N2 * ch
    tn = min(cols, 4096)
    hb = _mm3(lvl1_t, taps.reshape(n1, cols), tn, "hyena_fft_level1_taps")
    hspec = _level2(hb.reshape(2, n1, FFT_N2, ch), tw, g)
    zb = _mm3(lvl1_z, z.reshape(n1 // 2, cols), tn, "hyena_fft_level1_z")
    cspec = _level2(zb.reshape(2, n1, FFT_N2, ch), tw, g, hspec, gt)
    conv = _mm3(inv1, cspec.reshape(2 * n1, cols), tn, "hyena_fft_inverse_level1")
    return conv.reshape(seq_len, ch)


def _small_conv_kernel(z_ref, taps_ref, fc_ref, fs_ref, o_ref):
    seq_len = z_ref.shape[0]
    fc = fc_ref[...]
    fs = fs_ref[...]
    z = z_ref[...]
    taps = taps_ref[...]
    zr = _dot3(fc[:, :seq_len], z)
    zi = -_dot3(fs[:, :seq_len], z)
    hr = _dot3(fc, taps)
    hi = -_dot3(fs, taps)
    yr = zr * hr - zi * hi
    yi = zr * hi + zi * hr
    o_ref[...] = _dot3(fc[:seq_len, :], yr) - _dot3(fs[:seq_len, :], yi)


def _hyena_small_conv(z, taps):
    seq_len, ch = z.shape
    n = 2 * seq_len
    th = 2.0 * np.pi * np.outer(np.arange(n), np.arange(n)) / n
    full = lambda shape: pl.BlockSpec(shape, lambda: (0, 0))
    return pl.pallas_call(
        _small_conv_kernel,
        out_shape=jax.ShapeDtypeStruct((seq_len, ch), F32),
        in_specs=[full((seq_len, ch)), full((n, ch)), full((n, n)), full((n, n))],
        out_specs=full((seq_len, ch)),
        compiler_params=pltpu.CompilerParams(vmem_limit_bytes=VMEM_LIMIT),
        name="hyena_small_conv",
    )(z, taps, jnp.asarray(np.cos(th), F32), jnp.asarray(np.sin(th), F32))


def _hy_post_kernel(x0_ref, z_ref, conv_ref, asum_ref, bias_ref, y_ref, *, n_total):
    norm = jnp.sum(asum_ref[...], axis=0, keepdims=True) + EPS
    conv = conv_ref[...] * (1.0 / (n_total * norm))
    y_ref[...] = (x0_ref[...] * (conv + bias_ref[...] * z_ref[...])).astype(y_ref.dtype)


def _hyena_post(x0, z, conv, asum, bias):
    t = x0.shape[0]
    tm = min(t, 1024)
    row = pl.BlockSpec((tm, HY_WIDTH), lambda i: (i, 0))
    return pl.pallas_call(
        functools.partial(_hy_post_kernel, n_total=2 * t),
        out_shape=jax.ShapeDtypeStruct((t, HY_WIDTH), BF16),
        grid=(t // tm,),
        in_specs=[row, row, row, pl.BlockSpec((8, HY_WIDTH), lambda i: (0, 0)),
                  pl.BlockSpec((1, HY_WIDTH), lambda i: (0, 0))],
        out_specs=row,
        compiler_params=_cparams("parallel"),
        name="hyena_post",
    )(x0, z, conv, asum, bias.reshape(1, HY_WIDTH))


def _hyena_mixer(p_hy, conv_w, conv_b, filt, bias):
    seq_len = p_hy.shape[0]
    x0, z = _hyena_pre(p_hy, conv_w, conv_b)
    taps, asum = _hyena_taps(seq_len, *filt)
    if 2 * seq_len // FFT_N2 >= 16:
        conv = _hyena_long_conv(z, taps)
    else:
        conv = _hyena_small_conv(z, taps)
    return _hyena_post(x0, z, conv, asum, bias)


def _outproj_kernel(hy_ref, ret_ref, att_ref, w_ref, x_ref, gate_ref, o_ref):
    y = (_dot(hy_ref[...], w_ref[:HY_WIDTH, :])
         + _dot(ret_ref[...], w_ref[HY_WIDTH:HY_WIDTH + RET_WIDTH, :])
         + _dot(att_ref[...], w_ref[HY_WIDTH + RET_WIDTH:, :]))
    o_ref[...] = x_ref[...] + gate_ref[...] * y


def _out_projection(y_hy, y_ret, y_att, w_bf16, x, gate):
    t = x.shape[0]
    tm = min(t, 512)
    row = lambda w: pl.BlockSpec((tm, w), lambda i: (i, 0))
    return pl.pallas_call(
        _outproj_kernel,
        out_shape=jax.ShapeDtypeStruct((t, D_MODEL), F32),
        grid=(t // tm,),
        in_specs=[row(HY_WIDTH), row(RET_WIDTH), row(ATT_WIDTH),
                  pl.BlockSpec((MIX_WIDTH, D_MODEL), lambda i: (0, 0)),
                  row(D_MODEL), pl.BlockSpec((1, D_MODEL), lambda i: (0, 0))],
        out_specs=row(D_MODEL),
        compiler_params=_cparams("parallel"),
        name="out_projection",
    )(y_hy, y_ret, y_att, w_bf16, x, gate)


def _ffn_kernel(x_ref, g_ref, sh_ref, sc_ref, gate_ref, wg_ref, wu_ref, wd_ref, o_ref,
                h_ref, acc_ref):
    k = pl.program_id(1)

    @pl.when(k == 0)
    def _():
        h_ref[...] = _norm_mod(x_ref[...], g_ref[...], sh_ref[...], sc_ref[...]).astype(BF16)
        acc_ref[...] = jnp.zeros_like(acc_ref)

    h = h_ref[...]
    a = _silu(_dot(h, wg_ref[...])) * _dot(h, wu_ref[...])
    acc_ref[...] += _dot(a.astype(BF16), wd_ref[...])

    @pl.when(k == pl.num_programs(1) - 1)
    def _():
        o_ref[...] = x_ref[...] + gate_ref[...] * acc_ref[...]


def _dense_ffn(x, g, shift, scale, gate, wg, wu, wd):
    t = x.shape[0]
    tm = min(t, 1024)
    th = 256
    vec = pl.BlockSpec((1, D_MODEL), lambda i, k: (0, 0))
    return pl.pallas_call(
        _ffn_kernel,
        out_shape=jax.ShapeDtypeStruct((t, D_MODEL), F32),
        grid=(t // tm, FFN_HIDDEN // th),
        in_specs=[pl.BlockSpec((tm, D_MODEL), lambda i, k: (i, 0)), vec, vec, vec, vec,
                  pl.BlockSpec((D_MODEL, th), lambda i, k: (0, k)),
                  pl.BlockSpec((D_MODEL, th), lambda i, k: (0, k)),
                  pl.BlockSpec((th, D_MODEL), lambda i, k: (k, 0))],
        out_specs=pl.BlockSpec((tm, D_MODEL), lambda i, k: (i, 0)),
        scratch_shapes=[pltpu.VMEM((tm, D_MODEL), BF16), pltpu.VMEM((tm, D_MODEL), F32)],
        compiler_params=_cparams("parallel", "arbitrary"),
        name="dense_ffn",
    )(x, g, shift, scale, gate, wg, wu, wd)


def _moe_route_kernel(x_ref, g_ref, sh_ref, sc_ref, rw_ref, rb_ref, h_ref, info_ref):
    h = _norm_mod(x_ref[...], g_ref[...], sh_ref[...], sc_ref[...])
    h_ref[...] = h
    logits = _dot3(h, rw_ref[...]) + rb_ref[...]
    lane = lax.broadcasted_iota(jnp.int32, logits.shape, 1)
    v1 = jnp.max(logits, axis=-1, keepdims=True)
    i1 = jnp.min(jnp.where(logits == v1, lane, LANES), axis=-1, keepdims=True)
    rest = jnp.where(lane == i1, -jnp.inf, logits)
    v2 = jnp.max(rest, axis=-1, keepdims=True)
    i2 = jnp.min(jnp.where(rest == v2, lane, LANES), axis=-1, keepdims=True)
    e = jnp.exp(v2 - v1)
    g1 = 1.0 / (1.0 + e)
    g2 = e * g1
    info_ref[...] = jnp.where(lane == 0, i1.astype(F32),
                              jnp.where(lane == 1, i2.astype(F32),
                                        jnp.where(lane == 2, g1, jnp.where(lane == 3, g2, 0.0))))


def _moe_route(x, g, shift, scale, router_w, router_b):
    t = x.shape[0]
    tm = min(t, 512)
    rw = jnp.concatenate([router_w, jnp.zeros((D_MODEL, LANES - N_EXPERTS), F32)], axis=1)
    rb = jnp.concatenate([router_b, jnp.full((LANES - N_EXPERTS,), -jnp.inf, F32)]).reshape(1, LANES)
    vec = pl.BlockSpec((1, D_MODEL), lambda i: (0, 0))
    return pl.pallas_call(
        _moe_route_kernel,
        out_shape=(jax.ShapeDtypeStruct((t, D_MODEL), F32), jax.ShapeDtypeStruct((t, LANES), F32)),
        grid=(t // tm,),
        in_specs=[pl.BlockSpec((tm, D_MODEL), lambda i: (i, 0)), vec, vec, vec,
                  pl.BlockSpec((D_MODEL, LANES), lambda i: (0, 0)),
                  pl.BlockSpec((1, LANES), lambda i: (0, 0))],
        out_specs=(pl.BlockSpec((tm, D_MODEL), lambda i: (i, 0)),
                   pl.BlockSpec((tm, LANES), lambda i: (i, 0))),
        compiler_params=_cparams("parallel"),
        name="moe_route",
    )(x, g, shift, scale, rw, rb)


def _row_copy(src_hbm, dst_vmem, sem, src_row, dst_row):
    return pltpu.make_async_copy(src_hbm.at[pl.ds(src_row, 1)], dst_vmem.at[pl.ds(dst_row, 1)], sem)


def _moe_expert_kernel(blk_e_ref, nused_ref, src_ref, h_hbm, wg_ref, wu_ref, wd_ref, y_ref,
                       xbuf_ref, xb_ref, acc_ref, sem):
    b = pl.program_id(0)
    k = pl.program_id(1)
    last = pl.num_programs(1) - 1
    used = b < nused_ref[0]

    @pl.when(jnp.logical_and(used, k == 0))
    def _():
        def issue(r, carry):
            _row_copy(h_hbm, xbuf_ref, sem, src_ref[b * MOE_BLOCK + r], r).start()
            return carry

        lax.fori_loop(0, MOE_BLOCK, issue, 0)

        def wait(r, carry):
            _row_copy(h_hbm, xbuf_ref, sem, 0, r).wait()
            return carry

        lax.fori_loop(0, MOE_BLOCK, wait, 0)
        xb_ref[...] = xbuf_ref[...].astype(BF16)
        acc_ref[...] = jnp.zeros_like(acc_ref)

    @pl.when(used)
    def _():
        x = xb_ref[...]
        a = _silu(_dot(x, wg_ref[0])) * _dot(x, wu_ref[0])
        acc_ref[...] += _dot(a.astype(BF16), wd_ref[0])

    @pl.when(jnp.logical_and(used, k == last))
    def _():
        y_ref[...] = acc_ref[...]

    @pl.when(jnp.logical_and(jnp.logical_not(used), k == last))
    def _():
        y_ref[...] = jnp.zeros_like(y_ref)


def _moe_experts(h, blk_expert, n_used, src_tok, wg, wu, wd, n_blk):
    th = 256
    grid_spec = pltpu.PrefetchScalarGridSpec(
        num_scalar_prefetch=3,
        grid=(n_blk, FFN_HIDDEN // th),
        in_specs=[pl.BlockSpec(memory_space=pl.ANY),
                  pl.BlockSpec((1, D_MODEL, th), lambda b, k, be, nu, st: (be[b], 0, k)),
                  pl.BlockSpec((1, D_MODEL, th), lambda b, k, be, nu, st: (be[b], 0, k)),
                  pl.BlockSpec((1, th, D_MODEL), lambda b, k, be, nu, st: (be[b], k, 0))],
        out_specs=pl.BlockSpec((MOE_BLOCK, D_MODEL), lambda b, k, be, nu, st: (b, 0)),
        scratch_shapes=[pltpu.VMEM((MOE_BLOCK, D_MODEL), F32),
                        pltpu.VMEM((MOE_BLOCK, D_MODEL), BF16),
                        pltpu.VMEM((MOE_BLOCK, D_MODEL), F32),
                        pltpu.SemaphoreType.DMA],
    )
    return pl.pallas_call(
        _moe_expert_kernel,
        out_shape=jax.ShapeDtypeStruct((n_blk * MOE_BLOCK, D_MODEL), F32),
        grid_spec=grid_spec,
        compiler_params=_cparams("arbitrary", "arbitrary"),
        name="moe_experts",
    )(blk_expert, n_used, src_tok, h, wg, wu, wd)


def _moe_combine_kernel(dest_ref, x_ref, info_ref, gate_ref, ng_ref, y_hbm, o_ref,
                        y1_ref, y2_ref, sem, *, final_norm):
    i = pl.program_id(0)
    tm = x_ref.shape[0]

    def issue(r, carry):
        a = 2 * (i * tm + r)
        _row_copy(y_hbm, y1_ref, sem, dest_ref[a], r).start()
        _row_copy(y_hbm, y2_ref, sem, dest_ref[a + 1], r).start()
        return carry

    lax.fori_loop(0, tm, issue, 0)

    def wait(r, carry):
        _row_copy(y_hbm, y1_ref, sem, 0, r).wait()
        _row_copy(y_hbm, y2_ref, sem, 0, r).wait()
        return carry

    lax.fori_loop(0, tm, wait, 0)
    info = info_ref[...]
    y = y1_ref[...] * info[:, 2:3] + y2_ref[...] * info[:, 3:4]
    x = x_ref[...] + gate_ref[...] * y
    if final_norm:
        ms = jnp.mean(x * x, axis=-1, keepdims=True)
        x = x * lax.rsqrt(ms + EPS) * ng_ref[...]
    o_ref[...] = x


def _moe_combine(dest, x, info, gate, norm_g, y, final_norm):
    t = x.shape[0]
    tm = min(t, 256)
    vec = pl.BlockSpec((1, D_MODEL), lambda i, d: (0, 0))
    grid_spec = pltpu.PrefetchScalarGridSpec(
        num_scalar_prefetch=1,
        grid=(t // tm,),
        in_specs=[pl.BlockSpec((tm, D_MODEL), lambda i, d: (i, 0)),
                  pl.BlockSpec((tm, LANES), lambda i, d: (i, 0)), vec, vec,
                  pl.BlockSpec(memory_space=pl.ANY)],
        out_specs=pl.BlockSpec((tm, D_MODEL), lambda i, d: (i, 0)),
        scratch_shapes=[pltpu.VMEM((tm, D_MODEL), F32), pltpu.VMEM((tm, D_MODEL), F32),
                        pltpu.SemaphoreType.DMA],
    )
    return pl.pallas_call(
        functools.partial(_moe_combine_kernel, final_norm=final_norm),
        out_shape=jax.ShapeDtypeStruct((t, D_MODEL), F32),
        grid_spec=grid_spec,
        compiler_params=_cparams("arbitrary"),
        name="moe_combine",
    )(dest, x, info, gate, norm_g, y)


def _moe_layer(x, g, shift, scale, gate, router_w, router_b, wg, wu, wd, norm_g, final_norm):
    t = x.shape[0]
    n_asg = t * TOP_K
    h, info = _moe_route(x, g, shift, scale, router_w, router_b)
    expert = info[:, :TOP_K].astype(jnp.int32).reshape(-1)
    onehot = (expert[:, None] == jnp.arange(N_EXPERTS, dtype=jnp.int32)[None, :]).astype(jnp.int32)
    csum = jnp.cumsum(onehot, axis=0)
    counts = csum[-1]
    padded = (counts + MOE_BLOCK - 1) // MOE_BLOCK * MOE_BLOCK
    pad_end = jnp.cumsum(padded)
    pad_start = pad_end - padded
    dest = jnp.sum(onehot * (csum - 1 + pad_start[None, :]), axis=1).astype(jnp.int32)
    n_blk = -(-n_asg // MOE_BLOCK) + N_EXPERTS
    blk_start = jnp.arange(n_blk, dtype=jnp.int32) * MOE_BLOCK
    blk_expert = jnp.minimum(jnp.sum(blk_start[:, None] >= pad_end[None, :], axis=1),
                             N_EXPERTS - 1).astype(jnp.int32)
    n_used = (pad_end[-1:] // MOE_BLOCK).astype(jnp.int32)
    src_tok = jnp.zeros((n_blk * MOE_BLOCK,), jnp.int32).at[dest].set(
        jnp.arange(n_asg, dtype=jnp.int32) // TOP_K)
    y = _moe_experts(h, blk_expert, n_used, src_tok, wg, wu, wd, n_blk)
    return _moe_combine(dest, x, info, gate, norm_g, y, final_norm)


def _final_norm_kernel(x_ref, g_ref, o_ref):
    x = x_ref[...]
    ms = jnp.mean(x * x, axis=-1, keepdims=True)
    o_ref[...] = x * lax.rsqrt(ms + EPS) * g_ref[...]


def _final_norm(x, g):
    t = x.shape[0]
    tm = min(t, 1024)
    return pl.pallas_call(
        _final_norm_kernel,
        out_shape=jax.ShapeDtypeStruct((t, D_MODEL), F32),
        grid=(t // tm,),
        in_specs=[pl.BlockSpec((tm, D_MODEL), lambda i: (i, 0)),
                  pl.BlockSpec((1, D_MODEL), lambda i: (0, 0))],
        out_specs=pl.BlockSpec((tm, D_MODEL), lambda i: (i, 0)),
        compiler_params=_cparams("parallel"),
        name="final_norm",
    )(x, g)


def kernel(x, c, ctx, c_ctx, ada_w, ada_b, norm1_g, norm2_g, w_in, w_out, hy_conv_w, hy_conv_b, hy_filt_w1, hy_filt_b1, hy_filt_w2, hy_filt_b2, hy_filt_w3, hy_filt_b3, hy_filt_freq, hy_bias, ret_log_rate, attn_q_g, attn_k_g, ffn_w_gate, ffn_w_up, ffn_w_down, moe_router_w, moe_router_b, moe_w_gate, moe_w_up, moe_w_down, final_norm_g):
    assert x.shape[0] == 1 and c.shape[0] == 1
    seq_len = x.shape[1]
    x_lat = x[0]
    x_ctx = ctx[0]
    cos_t, sin_t = _rope_tables(seq_len)
    cvec = jnp.concatenate([c, c_ctx[None, :], jnp.zeros((6, D_MODEL), F32)], axis=0)
    cvec = _silu(cvec)
    row = lambda v: v.reshape(1, D_MODEL)
    zero_state = jnp.zeros((2, RET_WIDTH, RET_HEAD_DIM), F32)
    zero_tab = jnp.zeros((x_ctx.shape[0], LANES), F32)

    for l in range(DEPTH):
        last = l == DEPTH - 1
        mods = _ada_modulation(cvec, ada_w[l], ada_b[l]).reshape(8, 6, D_MODEL)
        mod = [row(mods[0, i]) for i in range(6)]
        mod_c = [row(mods[1, i]) for i in range(6)]
        filt = (hy_filt_w1[l], hy_filt_b1[l], hy_filt_w2[l], hy_filt_b2[l],
                hy_filt_w3[l], hy_filt_b3[l], hy_filt_freq[l])
        lg_f = -jnp.exp(ret_log_rate[l, 0].astype(F32))
        lg_b = -jnp.exp(ret_log_rate[l, 1].astype(F32))
        mask, zeta, xi, chunk_decay = _ret_tables(lg_f, lg_b)
        w_in_l = w_in[l].astype(BF16)
        w_out_l = w_out[l].astype(BF16)
        n1 = row(norm1_g[l])
        n2 = row(norm2_g[l])

        p_hy, p_ret, p_att = _in_projection(x_lat, n1, mod[0], mod[1], w_in_l)
        pc_hy, pc_ret, pc_att = _in_projection(x_ctx, n1, mod_c[0], mod_c[1], w_in_l)

        sprev_c, s_ctx = _retention_states(pc_ret, zeta, chunk_decay, zero_state)
        qc, ktc, vc, _ = _attention_prep(pc_att, zero_tab, zero_tab, attn_q_g[l], attn_k_g[l],
                                         rope=False)

        y_hy = _hyena_mixer(p_hy, hy_conv_w[l], hy_conv_b[l], filt, hy_bias[l])
        sprev, _ = _retention_states(p_ret, zeta, chunk_decay, s_ctx)
        y_ret = _retention_outputs(p_ret, mask, xi, sprev)
        q, kt, v, ksq = _attention_prep(p_att, cos_t, sin_t, attn_q_g[l], attn_k_g[l], rope=True)
        kmax = jnp.sqrt(jnp.max(ksq[:ATT_KV_HEADS], axis=1))
        y_att = _attention(q, ktc, vc, kmax, kt, v, tq=ATT_TQ, tk=ATT_TK, sub=ATT_SUB)

        if not last:
            yc_hy = _hyena_mixer(pc_hy, hy_conv_w[l], hy_conv_b[l], filt, hy_bias[l])
            yc_ret = _retention_outputs(pc_ret, mask, xi, sprev_c)
            yc_att = _attention(qc, ktc, vc, tq=x_ctx.shape[0])
            x_ctx = _out_projection(yc_hy, yc_ret, yc_att, w_out_l, x_ctx, mod_c[2])
        x_lat = _out_projection(y_hy, y_ret, y_att, w_out_l, x_lat, mod[2])

        i = l // 2
        if l % 2 == 0:
            wg, wu, wd = (ffn_w_gate[i].astype(BF16), ffn_w_up[i].astype(BF16),
                          ffn_w_down[i].astype(BF16))
            x_lat = _dense_ffn(x_lat, n2, mod[3], mod[4], mod[5], wg, wu, wd)
            if not last:
                x_ctx = _dense_ffn(x_ctx, n2, mod_c[3], mod_c[4], mod_c[5], wg, wu, wd)
        else:
            wg, wu, wd = (moe_w_gate[i].astype(BF16), moe_w_up[i].astype(BF16),
                          moe_w_down[i].astype(BF16))
            fg = row(final_norm_g)
            x_lat = _moe_layer(x_lat, n2, mod[3], mod[4], mod[5], moe_router_w[i], moe_router_b[i],
                               wg, wu, wd, fg, final_norm=last)
            if not last:
                x_ctx = _moe_layer(x_ctx, n2, mod_c[3], mod_c[4], mod_c[5], moe_router_w[i],
                                   moe_router_b[i], wg, wu, wd, fg, final_norm=False)
    if DEPTH % 2 == 1:
        x_lat = _final_norm(x_lat, row(final_norm_g))
    return x_lat[None]
```

```python
import functools
import math

import numpy as np
import jax
import jax.numpy as jnp
from jax import lax
from jax.experimental import pallas as pl
from jax.experimental.pallas import tpu as pltpu

F32 = jnp.float32
BF16 = jnp.bfloat16

D_MODEL = 1024
DEPTH = 2
GRID_W = 64
EPS = 1e-6

HY_WIDTH = 256
HY_COLS = 3 * HY_WIDTH
HY_POS_BANDS = 16
HY_DECAY_TARGET = 1e-2
HY_SHORT_DECAY_PCT = 0.3
HY_LONG_DECAY_PCT = 1.5

RET_HEAD_DIM = 64
RET_HEADS = 4
RET_WIDTH = RET_HEADS * RET_HEAD_DIM
RET_COLS = 4 * RET_WIDTH
RET_CHUNK = 128

ATT_HEAD_DIM = 64
ATT_HEADS = 8
ATT_KV_HEADS = 2
ATT_GROUP = ATT_HEADS // ATT_KV_HEADS
ATT_WIDTH = ATT_HEADS * ATT_HEAD_DIM
ATT_KV_WIDTH = ATT_KV_HEADS * ATT_HEAD_DIM
ATT_COLS = ATT_WIDTH + 2 * ATT_KV_WIDTH
ROPE_AXIS_DIM = ATT_HEAD_DIM // 2
ROPE_BASE = 10000.0

MIX_WIDTH = HY_WIDTH + RET_WIDTH + ATT_WIDTH
IN_COLS = HY_COLS + RET_COLS + ATT_COLS

FFN_HIDDEN = 2816
N_EXPERTS = 8
TOP_K = 2
MOE_BLOCK = 512

LOG2_E = 1.4426950408889634
EXP2_CAP = 60.0
ATT_TQ = 256
ATT_TK = 4096
ATT_SUB = 512
MOE_TH = 1408
DMA_UNROLL = 8

LANES = 128
VMEM_LIMIT = 56 * 1024 * 1024


def _cparams(*sem):
    return pltpu.CompilerParams(dimension_semantics=sem, vmem_limit_bytes=VMEM_LIMIT)


def _dot(a, b):
    return jnp.dot(a, b, preferred_element_type=F32)


def _split(a):
    hi = a.astype(BF16)
    lo = (a - hi.astype(F32)).astype(BF16)
    return hi, lo


def _dot3(a, b):
    ah, al = _split(a)
    bh, bl = _split(b)
    return _dot(ah, bh) + _dot(al, bh) + _dot(ah, bl)


def _silu(x):
    return x * (1.0 / (1.0 + jnp.exp(-x)))


def _norm_mod(x, g, shift, scale):
    ms = jnp.mean(x * x, axis=-1, keepdims=True)
    h = x * lax.rsqrt(ms + EPS) * g
    return h * (1.0 + scale) + shift


def _mm3_kernel(a_ref, b_ref, o_ref):
    o_ref[...] = _dot3(a_ref[...], b_ref[...])


def _mm3(a, b, tn, name):
    m, k = a.shape
    n = b.shape[1]
    return pl.pallas_call(
        _mm3_kernel,
        out_shape=jax.ShapeDtypeStruct((m, n), F32),
        grid=(n // tn,),
        in_specs=[pl.BlockSpec((m, k), lambda j: (0, 0)),
                  pl.BlockSpec((k, tn), lambda j: (0, j))],
        out_specs=pl.BlockSpec((m, tn), lambda j: (0, j)),
        compiler_params=_cparams("parallel"),
        name=name,
    )(a, b)


def _ada_kernel(c_ref, w_ref, b_ref, o_ref):
    o_ref[...] = _dot3(c_ref[...], w_ref[...]) + b_ref[...]


def _ada_modulation(cc, w, b):
    n = w.shape[1]
    tn = 1536
    return pl.pallas_call(
        _ada_kernel,
        out_shape=jax.ShapeDtypeStruct((8, n), F32),
        grid=(n // tn,),
        in_specs=[pl.BlockSpec((8, D_MODEL), lambda j: (0, 0)),
                  pl.BlockSpec((D_MODEL, tn), lambda j: (0, j)),
                  pl.BlockSpec((1, tn), lambda j: (0, j))],
        out_specs=pl.BlockSpec((8, tn), lambda j: (0, j)),
        compiler_params=_cparams("parallel"),
        name="ada_modulation",
    )(cc, w, b.reshape(1, n))


def _inproj_kernel(x_ref, g_ref, sh_ref, sc_ref, w_ref, hy_ref, ret_ref, att_ref):
    h = _norm_mod(x_ref[...], g_ref[...], sh_ref[...], sc_ref[...])
    p = _dot(h.astype(BF16), w_ref[...])
    hy_ref[...] = p[:, :HY_COLS]
    ret_ref[...] = p[:, HY_COLS:HY_COLS + RET_COLS]
    att_ref[...] = p[:, HY_COLS + RET_COLS:]


def _in_projection(x, g, shift, scale, w_bf16):
    t = x.shape[0]
    tm = min(t, 512)
    vec = pl.BlockSpec((1, D_MODEL), lambda i: (0, 0))
    return pl.pallas_call(
        _inproj_kernel,
        out_shape=(jax.ShapeDtypeStruct((t, HY_COLS), F32),
                   jax.ShapeDtypeStruct((t, RET_COLS), F32),
                   jax.ShapeDtypeStruct((t, ATT_COLS), F32)),
        grid=(t // tm,),
        in_specs=[pl.BlockSpec((tm, D_MODEL), lambda i: (i, 0)), vec, vec, vec,
                  pl.BlockSpec((D_MODEL, IN_COLS), lambda i: (0, 0))],
        out_specs=(pl.BlockSpec((tm, HY_COLS), lambda i: (i, 0)),
                   pl.BlockSpec((tm, RET_COLS), lambda i: (i, 0)),
                   pl.BlockSpec((tm, ATT_COLS), lambda i: (i, 0))),
        compiler_params=_cparams("parallel"),
        name="in_projection",
    )(x, g, shift, scale, w_bf16)


def _head_mean_matrix(width):
    idx = np.arange(width) // ATT_HEAD_DIM
    return jnp.asarray((idx[:, None] == idx[None, :]).astype(np.float32) / ATT_HEAD_DIM, BF16)


def _head_rms(x, bd, g):
    x2h, x2l = _split(x * x)
    ms = _dot(x2h, bd) + _dot(x2l, bd)
    return x * lax.rsqrt(ms + EPS) * g


def _rope(x, cos, sin):
    n = x.shape[1]
    lane = lax.broadcasted_iota(jnp.int32, x.shape, 1)
    swapped = jnp.where((lane % 32) < 16, pltpu.roll(x, n - 16, 1), pltpu.roll(x, 16, 1))
    return x * cos + swapped * sin


def _attprep_kernel(p_ref, cos_ref, sin_ref, qg_ref, kg_ref, bdq_ref, bdk_ref,
                    q_ref, kt_ref, v_ref, ksq_ref, *, rope):
    p = p_ref[...]
    q = _head_rms(p[:, :ATT_WIDTH], bdq_ref[...], qg_ref[...])
    k = _head_rms(p[:, ATT_WIDTH:ATT_WIDTH + ATT_KV_WIDTH], bdk_ref[...], kg_ref[...])
    if rope:
        cos = cos_ref[...]
        sin = sin_ref[...]
        q = jnp.concatenate(
            [_rope(q[:, j * LANES:(j + 1) * LANES], cos, sin) for j in range(ATT_WIDTH // LANES)],
            axis=1)
        k = _rope(k, cos, sin)
    q_ref[...] = (q * (ATT_HEAD_DIM ** -0.5 * LOG2_E)).astype(BF16)
    kt = k.T.astype(BF16)
    tm = kt.shape[1]
    sub = lax.broadcasted_iota(jnp.int32, (ATT_HEAD_DIM, tm), 0)
    minus_one_row = jnp.where(sub == 0, -1.0, 0.0).astype(BF16)
    sq = kt.astype(F32) ** 2
    norms = []
    for h in range(ATT_KV_HEADS):
        rows = slice(h * ATT_HEAD_DIM, (h + 1) * ATT_HEAD_DIM)
        kt_ref[h * LANES:h * LANES + ATT_HEAD_DIM, :] = kt[rows]
        kt_ref[h * LANES + ATT_HEAD_DIM:(h + 1) * LANES, :] = minus_one_row
        norms.append(jnp.sum(sq[rows], axis=0, keepdims=True))
    ksq_ref[...] = jnp.concatenate(norms + [jnp.zeros((8 - ATT_KV_HEADS, tm), F32)], axis=0)
    pv = p[:, ATT_WIDTH + ATT_KV_WIDTH:]
    lane = lax.broadcasted_iota(jnp.int32, pv.shape, 1)
    ones_col = jnp.where(lane == ATT_HEAD_DIM, 1.0, 0.0)
    v_ref[0] = jnp.where(lane < ATT_HEAD_DIM, pv, ones_col).astype(BF16)
    v_ref[1] = jnp.where(lane < ATT_HEAD_DIM, pltpu.roll(pv, ATT_HEAD_DIM, 1), ones_col).astype(BF16)


def _attention_prep(p_att, cos, sin, q_g, k_g, rope):
    t = p_att.shape[0]
    tm = min(t, 512)
    qg = jnp.tile(q_g, ATT_HEADS).reshape(1, ATT_WIDTH)
    kg = jnp.tile(k_g, ATT_KV_HEADS).reshape(1, ATT_KV_WIDTH)
    const = lambda shape: pl.BlockSpec(shape, lambda i: (0, 0))
    return pl.pallas_call(
        functools.partial(_attprep_kernel, rope=rope),
        out_shape=(jax.ShapeDtypeStruct((t, ATT_WIDTH), BF16),
                   jax.ShapeDtypeStruct((ATT_KV_HEADS * LANES, t), BF16),
                   jax.ShapeDtypeStruct((ATT_KV_HEADS, t, LANES), BF16),
                   jax.ShapeDtypeStruct((8, t), F32)),
        grid=(t // tm,),
        in_specs=[pl.BlockSpec((tm, ATT_COLS), lambda i: (i, 0)),
                  pl.BlockSpec((tm, LANES), lambda i: (i, 0)),
                  pl.BlockSpec((tm, LANES), lambda i: (i, 0)),
                  const((1, ATT_WIDTH)), const((1, ATT_KV_WIDTH)),
                  const((ATT_WIDTH, ATT_WIDTH)), const((ATT_KV_WIDTH, ATT_KV_WIDTH))],
        out_specs=(pl.BlockSpec((tm, ATT_WIDTH), lambda i: (i, 0)),
                   pl.BlockSpec((ATT_KV_HEADS * LANES, tm), lambda i: (0, i)),
                   pl.BlockSpec((ATT_KV_HEADS, tm, LANES), lambda i: (0, i, 0)),
                   pl.BlockSpec((8, tm), lambda i: (0, i))),
        compiler_params=_cparams("parallel"),
        name="attention_prep",
    )(p_att, cos, sin, qg, kg, _head_mean_matrix(ATT_WIDTH), _head_mean_matrix(ATT_KV_WIDTH))


def _rope_tables(n_tokens):
    n = np.arange(n_tokens)
    inv_freq = ROPE_BASE ** (-np.arange(0, ROPE_AXIS_DIM, 2, dtype=np.float64) / ROPE_AXIS_DIM)
    inv_freq = inv_freq.astype(np.float32)
    ang_r = jnp.asarray((n // GRID_W).astype(np.float32))[:, None] * jnp.asarray(inv_freq)[None, :]
    ang_c = jnp.asarray((n % GRID_W).astype(np.float32))[:, None] * jnp.asarray(inv_freq)[None, :]
    cos = jnp.concatenate([jnp.cos(ang_r)] * 2 + [jnp.cos(ang_c)] * 2, axis=1)
    sin = jnp.concatenate([-jnp.sin(ang_r), jnp.sin(ang_r), -jnp.sin(ang_c), jnp.sin(ang_c)], axis=1)
    return jnp.tile(cos, (1, 2)), jnp.tile(sin, (1, 2))


def _attention_kernel(*refs, tq, tk, sub, n_lat):
    if n_lat:
        kmax_ref, q_ref, ktc_ref, vc_ref, kt_ref, v_ref, o_ref, qs_ref, acc_ref, m_ref = refs
    else:
        q_ref, ktc_ref, vc_ref, o_ref, acc_ref = refs
    rows = ATT_GROUP * tq
    q = q_ref[...]
    qst = jnp.concatenate(
        [q[:, g * ATT_HEAD_DIM:(g + 1) * ATT_HEAD_DIM] for g in range(ATT_GROUP)], axis=0)
    pad = jnp.zeros((rows, LANES - ATT_HEAD_DIM), BF16)
    s = _dot(jnp.concatenate([qst, pad], axis=1), ktc_ref[...])
    mt = jnp.max(s, axis=-1, keepdims=True).astype(BF16)
    mtf = mt.astype(F32)
    acc_ref[...] = _dot(jnp.exp2(s - mtf).astype(BF16), vc_ref[0])

    if n_lat:
        lane = lax.broadcasted_iota(jnp.int32, pad.shape, 1)
        qs_ref[...] = jnp.concatenate([qst, jnp.where(lane == 0, mtf, 0.0).astype(BF16)], axis=1)
        qf = qst.astype(F32)
        qn = jnp.sqrt(jnp.sum(qf * qf, axis=-1, keepdims=True))
        fast = jnp.max(qn * kmax_ref[pl.program_id(0)] - mtf) <= EXP2_CAP

        def tile(j):
            off = pl.multiple_of(j * tk, tk)
            return kt_ref[:, pl.ds(off, tk)], v_ref[0, pl.ds(off, tk), :]

        @pl.when(fast)
        def _():
            def body(j, carry):
                kt, v = tile(j)
                qs = qs_ref[...]
                part = None
                for c in range(tk // sub):
                    p = jnp.exp2(_dot(qs, kt[:, c * sub:(c + 1) * sub])).astype(BF16)
                    pv = _dot(p, v[c * sub:(c + 1) * sub])
                    part = pv if part is None else part + pv
                acc_ref[...] += part
                return carry

            lax.fori_loop(0, n_lat, body, 0)

        @pl.when(jnp.logical_not(fast))
        def _():
            m_ref[...] = jnp.zeros_like(m_ref)

            def body(j, carry):
                kt, v = tile(j)
                s = _dot(qs_ref[...], kt)
                m_old = m_ref[...]
                m_new = jnp.maximum(m_old, jnp.max(s, axis=-1, keepdims=True))
                p = jnp.exp2(s - m_new).astype(BF16)
                acc_ref[...] = jnp.exp2(m_old - m_new) * acc_ref[...] + _dot(p, v)
                m_ref[...] = m_new
                return carry

            lax.fori_loop(0, n_lat, body, 0)

    acc = acc_ref[...]
    o = acc[:, :ATT_HEAD_DIM] / acc[:, ATT_HEAD_DIM:ATT_HEAD_DIM + 1]
    o_ref[...] = jnp.concatenate(
        [o[g * tq:(g + 1) * tq] for g in range(ATT_GROUP)], axis=1).astype(o_ref.dtype)


def _attention(q, ktc, vc, kmax=None, kt=None, v=None, *, tq, tk=0, sub=0):
    t = q.shape[0]
    lc = ktc.shape[1]
    gw = ATT_GROUP * ATT_HEAD_DIM
    rows = ATT_GROUP * tq
    in_specs = [pl.BlockSpec((tq, gw), lambda h, i: (i, h)),
                pl.BlockSpec((LANES, lc), lambda h, i: (h, 0)),
                pl.BlockSpec((1, lc, LANES), lambda h, i: (h, 0, 0))]
    args = [q, ktc, vc]
    scratch = [pltpu.VMEM((rows, LANES), F32)]
    n_lat = 0
    if kt is not None:
        lk = kt.shape[1]
        n_lat = lk // tk
        in_specs = ([pl.BlockSpec(memory_space=pltpu.SMEM)] + in_specs
                    + [pl.BlockSpec((LANES, lk), lambda h, i: (h, 0)),
                       pl.BlockSpec((1, lk, LANES), lambda h, i: (h, 0, 0))])
        args = [kmax] + args + [kt, v]
        scratch = [pltpu.VMEM((rows, LANES), BF16)] + scratch + [pltpu.VMEM((rows, 1), F32)]
    return pl.pallas_call(
        functools.partial(_attention_kernel, tq=tq, tk=tk, sub=sub, n_lat=n_lat),
        out_shape=jax.ShapeDtypeStruct((t, ATT_WIDTH), BF16),
        grid=(ATT_KV_HEADS, t // tq),
        in_specs=in_specs,
        out_specs=pl.BlockSpec((tq, gw), lambda h, i: (i, h)),
        scratch_shapes=scratch,
        compiler_params=_cparams("parallel", "parallel"),
        name="attention_lat" if n_lat else "attention_ctx",
    )(*args)


def _ret_tables(lg_f, lg_b):
    pos = jnp.arange(RET_CHUNK, dtype=F32)
    rel = pos[:, None] - pos[None, :]
    mask = jnp.where(rel > 0, jnp.exp(lg_f[:, None, None] * jnp.maximum(rel, 0.0)),
                     jnp.where(rel < 0, jnp.exp(lg_b[:, None, None] * jnp.maximum(-rel, 0.0)), 2.0))
    zeta = jnp.stack([jnp.exp(lg_f[:, None] * (RET_CHUNK - 1 - pos)[None, :]),
                      jnp.exp(lg_b[:, None] * pos[None, :])])
    xi = jnp.stack([jnp.exp(lg_f[:, None] * (pos + 1)[None, :]),
                    jnp.exp(lg_b[:, None] * (RET_CHUNK - pos)[None, :])])
    chunk_decay = jnp.stack([jnp.exp(lg_f * RET_CHUNK), jnp.exp(lg_b * RET_CHUNK)])
    zeta = jnp.repeat(zeta, RET_HEAD_DIM, axis=1).transpose(0, 2, 1)
    xi = jnp.repeat(xi, RET_HEAD_DIM, axis=1).transpose(0, 2, 1)
    chunk_decay = jnp.repeat(chunk_decay, RET_HEAD_DIM, axis=1)[:, :, None]
    return mask, zeta, xi, chunk_decay


def _ret_state_kernel(k_ref, v_ref, zeta_ref, cd_ref, s0_ref, sprev_ref, sfin_ref, s_ref):
    i = pl.program_id(1)

    @pl.when(i == 0)
    def _():
        s_ref[...] = s0_ref[0]

    s = s_ref[...]
    sprev_ref[0, 0] = s
    kz = (k_ref[...] * (RET_HEAD_DIM ** -0.5) * zeta_ref[0]).astype(BF16)
    v = v_ref[...].astype(BF16)
    us = []
    for h in range(RET_HEADS):
        sl = slice(h * RET_HEAD_DIM, (h + 1) * RET_HEAD_DIM)
        us.append(lax.dot_general(kz[:, sl], v[:, sl], (((0,), (0,)), ((), ())),
                                  preferred_element_type=F32))
    s_new = cd_ref[0] * s + jnp.concatenate(us, axis=0)
    s_ref[...] = s_new
    sfin_ref[0] = s_new


def _retention_states(p_ret, zeta, chunk_decay, s0):
    nc = p_ret.shape[0] // RET_CHUNK
    wb = RET_WIDTH // RET_WIDTH

    def chunk(d, i):
        return i + d * (nc - 1 - 2 * i)

    return pl.pallas_call(
        _ret_state_kernel,
        out_shape=(jax.ShapeDtypeStruct((2, nc, RET_WIDTH, RET_HEAD_DIM), F32),
                   jax.ShapeDtypeStruct((2, RET_WIDTH, RET_HEAD_DIM), F32)),
        grid=(2, nc),
        in_specs=[pl.BlockSpec((RET_CHUNK, RET_WIDTH), lambda d, i: (chunk(d, i), 1 * wb)),
                  pl.BlockSpec((RET_CHUNK, RET_WIDTH), lambda d, i: (chunk(d, i), 2 * wb)),
                  pl.BlockSpec((1, RET_CHUNK, RET_WIDTH), lambda d, i: (d, 0, 0)),
                  pl.BlockSpec((1, RET_WIDTH, 1), lambda d, i: (d, 0, 0)),
                  pl.BlockSpec((1, RET_WIDTH, RET_HEAD_DIM), lambda d, i: (d, 0, 0))],
        out_specs=(pl.BlockSpec((1, 1, RET_WIDTH, RET_HEAD_DIM), lambda d, i: (d, chunk(d, i), 0, 0)),
                   pl.BlockSpec((1, RET_WIDTH, RET_HEAD_DIM), lambda d, i: (d, 0, 0))),
        scratch_shapes=[pltpu.VMEM((RET_WIDTH, RET_HEAD_DIM), F32)],
        compiler_params=_cparams("arbitrary", "arbitrary"),
        name="retention_states",
    )(p_ret, p_ret, zeta, chunk_decay, s0)


def _ret_out_kernel(q_ref, k_ref, v_ref, g_ref, mask_ref, xi_ref, sf_ref, sb_ref, y_ref):
    q = q_ref[...]
    k = (k_ref[...] * (RET_HEAD_DIM ** -0.5)).astype(BF16)
    v = v_ref[...].astype(BF16)
    qb = q.astype(BF16)
    qf = (q * xi_ref[0]).astype(BF16)
    qr = (q * xi_ref[1]).astype(BF16)
    sf = sf_ref[0, 0].astype(BF16)
    sb = sb_ref[0, 0].astype(BF16)
    ys = []
    for h in range(RET_HEADS):
        sl = slice(h * RET_HEAD_DIM, (h + 1) * RET_HEAD_DIM)
        scores = lax.dot_general(qb[:, sl], k[:, sl], (((1,), (1,)), ((), ())),
                                 preferred_element_type=F32)
        y = _dot((scores * mask_ref[h]).astype(BF16), v[:, sl])
        y = y + _dot(qf[:, sl], sf[sl, :]) + _dot(qr[:, sl], sb[sl, :])
        mu = jnp.mean(y, axis=-1, keepdims=True)
        yc = y - mu
        var = jnp.mean(yc * yc, axis=-1, keepdims=True)
        ys.append(yc * lax.rsqrt(var + EPS))
    y_ref[...] = (_silu(g_ref[...]) * jnp.concatenate(ys, axis=1)).astype(y_ref.dtype)


def _retention_outputs(p_ret, mask, xi, sprev):
    nc = p_ret.shape[0] // RET_CHUNK
    col = lambda c: pl.BlockSpec((RET_CHUNK, RET_WIDTH), lambda i: (i, c))
    return pl.pallas_call(
        _ret_out_kernel,
        out_shape=jax.ShapeDtypeStruct((p_ret.shape[0], RET_WIDTH), BF16),
        grid=(nc,),
        in_specs=[col(0), col(1), col(2), col(3),
                  pl.BlockSpec((RET_HEADS, RET_CHUNK, RET_CHUNK), lambda i: (0, 0, 0)),
                  pl.BlockSpec((2, RET_CHUNK, RET_WIDTH), lambda i: (0, 0, 0)),
                  pl.BlockSpec((1, 1, RET_WIDTH, RET_HEAD_DIM), lambda i: (0, i, 0, 0)),
                  pl.BlockSpec((1, 1, RET_WIDTH, RET_HEAD_DIM), lambda i: (1, i, 0, 0))],
        out_specs=pl.BlockSpec((RET_CHUNK, RET_WIDTH), lambda i: (i, 0)),
        compiler_params=_cparams("parallel"),
        name="retention_outputs",
    )(p_ret, p_ret, p_ret, p_ret, mask, xi, sprev, sprev)


def _hy_pre_kernel(p_ref, prev_ref, next_ref, w_ref, b_ref, x0_ref, z_ref):
    i = pl.program_id(0)
    x = p_ref[...]
    tm = x.shape[0]
    row = lax.broadcasted_iota(jnp.int32, x.shape, 0)
    prev_row = jnp.where(i == 0, 0.0, prev_ref[7:8, :])
    next_row = jnp.where(i == pl.num_programs(0) - 1, 0.0, next_ref[0:1, :])
    up = jnp.where(row == 0, prev_row, pltpu.roll(x, 1, 0))
    dn = jnp.where(row == tm - 1, next_row, pltpu.roll(x, tm - 1, 0))
    w = w_ref[...]
    u = up * w[0:1] + x * w[1:2] + dn * w[2:3] + b_ref[...]
    x0_ref[...] = u[:, :HY_WIDTH]
    z_ref[...] = u[:, HY_WIDTH:2 * HY_WIDTH] * u[:, 2 * HY_WIDTH:]


def _hyena_pre(p_hy, conv_w, conv_b):
    t = p_hy.shape[0]
    tm = min(t, 512)
    nb8 = tm // 8
    last8 = t // 8 - 1
    return pl.pallas_call(
        _hy_pre_kernel,
        out_shape=(jax.ShapeDtypeStruct((t, HY_WIDTH), F32),
                   jax.ShapeDtypeStruct((t, HY_WIDTH), F32)),
        grid=(t // tm,),
        in_specs=[pl.BlockSpec((tm, HY_COLS), lambda i: (i, 0)),
                  pl.BlockSpec((8, HY_COLS), lambda i: (jnp.maximum(i * nb8 - 1, 0), 0)),
                  pl.BlockSpec((8, HY_COLS), lambda i: (jnp.minimum((i + 1) * nb8, last8), 0)),
                  pl.BlockSpec((3, HY_COLS), lambda i: (0, 0)),
                  pl.BlockSpec((1, HY_COLS), lambda i: (0, 0))],
        out_specs=(pl.BlockSpec((tm, HY_WIDTH), lambda i: (i, 0)),
                   pl.BlockSpec((tm, HY_WIDTH), lambda i: (i, 0))),
        compiler_params=_cparams("parallel"),
        name="hyena_pre",
    )(p_hy, p_hy, p_hy, conv_w, conv_b.reshape(1, HY_COLS))


def _hy_filter_kernel(feat_ref, w1_ref, b1_ref, w2_ref, b2_ref, w3_ref, b3_ref, freq_ref,
                      delta_ref, taps_ref, asum_ref, *, seq_len):
    i = pl.program_id(0)
    feat = feat_ref[...]
    freq = freq_ref[...]
    h = jnp.sin(freq * (_dot3(feat, w1_ref[...]) + b1_ref[...]))
    h = jnp.sin(freq * (_dot3(h, w2_ref[...]) + b2_ref[...]))
    h = _dot3(h, w3_ref[...]) + b3_ref[...]
    tm = feat.shape[0]
    j = i * tm + lax.broadcasted_iota(jnp.int32, (tm, HY_WIDTH), 0)
    t = feat[:, 0:1]
    hsel = jnp.where(j < seq_len, h[:, :HY_WIDTH], h[:, HY_WIDTH:])
    taps = jnp.where(j == seq_len, 0.0, hsel * jnp.exp(-t * delta_ref[...]))
    taps_ref[...] = taps

    @pl.when(i == 0)
    def _():
        asum_ref[...] = jnp.zeros_like(asum_ref)

    asum_ref[...] += jnp.sum(jnp.abs(taps).reshape(tm // 8, 8, HY_WIDTH), axis=0)


def _hyena_taps(seq_len, w1, b1, w2, b2, w3, b3, freq):
    n = 2 * seq_len
    j = np.arange(n)
    pos = np.where(j <= seq_len, j, n - j).astype(np.float32)
    pos = jnp.asarray(pos)
    tt = pos / max(seq_len - 1, 1)
    bands = jnp.linspace(1e-4, HY_POS_BANDS - 1, HY_POS_BANDS, dtype=F32)
    ang = (2.0 * math.pi / seq_len) * pos[:, None] * bands[None, :]
    feat = jnp.concatenate([tt[:, None], jnp.cos(ang), -jnp.sin(ang),
                            jnp.zeros((n, LANES - 1 - 2 * HY_POS_BANDS), F32)], axis=-1)
    w1p = jnp.concatenate([w1, jnp.zeros((LANES - w1.shape[0], w1.shape[1]), F32)], axis=0)
    deltas = jnp.abs(jnp.linspace(math.log(HY_DECAY_TARGET) / HY_LONG_DECAY_PCT,
                                  math.log(HY_DECAY_TARGET) / HY_SHORT_DECAY_PCT,
                                  HY_WIDTH, dtype=F32)).reshape(1, HY_WIDTH)
    tm = min(n, 1024)
    hid = w1.shape[1]
    const = lambda shape: pl.BlockSpec(shape, lambda i: (0, 0))
    return pl.pallas_call(
        functools.partial(_hy_filter_kernel, seq_len=seq_len),
        out_shape=(jax.ShapeDtypeStruct((n, HY_WIDTH), F32),
                   jax.ShapeDtypeStruct((8, HY_WIDTH), F32)),
        grid=(n // tm,),
        in_specs=[pl.BlockSpec((tm, LANES), lambda i: (i, 0)),
                  const((LANES, hid)), const((1, hid)), const((hid, hid)), const((1, hid)),
                  const((hid, 2 * HY_WIDTH)), const((1, 2 * HY_WIDTH)), const((1, hid)),
                  const((1, HY_WIDTH))],
        out_specs=(pl.BlockSpec((tm, HY_WIDTH), lambda i: (i, 0)), const((8, HY_WIDTH))),
        compiler_params=_cparams("arbitrary"),
        name="hyena_taps",
    )(feat, w1p, b1.reshape(1, hid), w2, b2.reshape(1, hid), w3, b3.reshape(1, 2 * HY_WIDTH),
      freq.reshape(1, hid), deltas)


FFT_N2 = 256


def _dft_tables(n_total, n1_in):
    n1 = n_total // FFT_N2
    k1 = np.arange(n1)
    a = 2.0 * np.pi * np.outer(k1, np.arange(n1_in)) / n1
    lvl1 = np.concatenate([np.cos(a), -np.sin(a)], axis=0)
    n2 = np.arange(FFT_N2)
    th = 2.0 * np.pi * np.outer(n2, n2) / FFT_N2
    fc, fs = np.cos(th), np.sin(th)
    g = np.block([[fc, fs], [-fs, fc]])
    ph = 2.0 * np.pi * np.outer(k1, n2) / n_total
    tw = np.stack([np.cos(ph), np.sin(ph)])[..., None]
    return (jnp.asarray(lvl1, F32), jnp.asarray(g, F32), jnp.asarray(g.T, F32), jnp.asarray(tw, F32))


def _inverse_lvl1_table(n_total, n1_out):
    n1 = n_total // FFT_N2
    a = 2.0 * np.pi * np.outer(np.arange(n1_out), np.arange(n1)) / n1
    return jnp.asarray(np.concatenate([np.cos(a), -np.sin(a)], axis=1), F32)


def _lvl2_kernel(b_ref, tw_ref, gh_ref, gl_ref, *rest, inverse):
    c = tw_ref[0, 0]
    s = tw_ref[1, 0]
    br = b_ref[0, 0]
    bi = b_ref[1, 0]
    d = jnp.concatenate([c * br + s * bi, c * bi - s * br], axis=0)
    dh, dl = _split(d)
    x = _dot(gh_ref[...], dh) + _dot(gh_ref[...], dl) + _dot(gl_ref[...], dh)
    if not inverse:
        (o_ref,) = rest
        o_ref[0, 0] = x[:FFT_N2]
        o_ref[1, 0] = x[FFT_N2:]
        return
    h_ref, gth_ref, gtl_ref, o_ref = rest
    xr, xi = x[:FFT_N2], x[FFT_N2:]
    hr, hi = h_ref[0, 0], h_ref[1, 0]
    y = jnp.concatenate([xr * hr - xi * hi, xr * hi + xi * hr], axis=0)
    yh, yl = _split(y)
    cc = _dot(gth_ref[...], yh) + _dot(gth_ref[...], yl) + _dot(gtl_ref[...], yh)
    cr, ci = cc[:FFT_N2], cc[FFT_N2:]
    o_ref[0, 0] = c * cr - s * ci
    o_ref[1, 0] = c * ci + s * cr


def _level2(b, tw, g, h=None, gt=None):
    _, n1, n2, ch = b.shape
    blk = pl.BlockSpec((2, 1, n2, ch), lambda i: (0, i, 0, 0))
    const = pl.BlockSpec((2 * n2, 2 * n2), lambda i: (0, 0))
    gh, gl = _split(g)
    in_specs = [blk, pl.BlockSpec((2, 1, n2, 1), lambda i: (0, i, 0, 0)), const, const]
    args = [b, tw, gh, gl]
    if h is not None:
        gth, gtl = _split(gt)
        in_specs += [blk, const, const]
        args += [h, gth, gtl]
    return pl.pallas_call(
        functools.partial(_lvl2_kernel, inverse=h is not None),
        out_shape=jax.ShapeDtypeStruct(b.shape, F32),
        grid=(n1,),
        in_specs=in_specs,
        out_specs=blk,
        compiler_params=_cparams("parallel"),
        name="hyena_fft_level2" + ("_conv" if h is not None else ""),
    )(*args)


def _hyena_long_conv(z, taps):
    seq_len, ch = z.shape
    n = 2 * seq_len
    n1 = n // FFT_N2
    lvl1_z, g, gt, tw = _dft_tables(n, n1 // 2)
    lvl1_t = _dft_tables(n, n1)[0]
    inv1 = _inverse_lvl1_table(n, n1 // 2)
    cols = FFT_N2 * ch
    tn = min(cols, 4096)
    hb = _mm3(lvl1_t, taps.reshape(n1, cols), tn, "hyena_fft_level1_taps")
    hspec = _level2(hb.reshape(2, n1, FFT_N2, ch), tw, g)
    zb = _mm3(lvl1_z, z.reshape(n1 // 2, cols), tn, "hyena_fft_level1_z")
    cspec = _level2(zb.reshape(2, n1, FFT_N2, ch), tw, g, hspec, gt)
    conv = _mm3(inv1, cspec.reshape(2 * n1, cols), tn, "hyena_fft_inverse_level1")
    return conv.reshape(seq_len, ch)


def _small_conv_kernel(z_ref, taps_ref, fc_ref, fs_ref, o_ref):
    seq_len = z_ref.shape[0]
    fc = fc_ref[...]
    fs = fs_ref[...]
    z = z_ref[...]
    taps = taps_ref[...]
    zr = _dot3(fc[:, :seq_len], z)
    zi = -_dot3(fs[:, :seq_len], z)
    hr = _dot3(fc, taps)
    hi = -_dot3(fs, taps)
    yr = zr * hr - zi * hi
    yi = zr * hi + zi * hr
    o_ref[...] = _dot3(fc[:seq_len, :], yr) - _dot3(fs[:seq_len, :], yi)


def _hyena_small_conv(z, taps):
    seq_len, ch = z.shape
    n = 2 * seq_len
    th = 2.0 * np.pi * np.outer(np.arange(n), np.arange(n)) / n
    full = lambda shape: pl.BlockSpec(shape, lambda: (0, 0))
    return pl.pallas_call(
        _small_conv_kernel,
        out_shape=jax.ShapeDtypeStruct((seq_len, ch), F32),
        in_specs=[full((seq_len, ch)), full((n, ch)), full((n, n)), full((n, n))],
        out_specs=full((seq_len, ch)),
        compiler_params=pltpu.CompilerParams(vmem_limit_bytes=VMEM_LIMIT),
        name="hyena_small_conv",
    )(z, taps, jnp.asarray(np.cos(th), F32), jnp.asarray(np.sin(th), F32))


def _hy_post_kernel(x0_ref, z_ref, conv_ref, asum_ref, bias_ref, y_ref, *, n_total):
    norm = jnp.sum(asum_ref[...], axis=0, keepdims=True) + EPS
    conv = conv_ref[...] * (1.0 / (n_total * norm))
    y_ref[...] = (x0_ref[...] * (conv + bias_ref[...] * z_ref[...])).astype(y_ref.dtype)


def _hyena_post(x0, z, conv, asum, bias):
    t = x0.shape[0]
    tm = min(t, 1024)
    row = pl.BlockSpec((tm, HY_WIDTH), lambda i: (i, 0))
    return pl.pallas_call(
        functools.partial(_hy_post_kernel, n_total=2 * t),
        out_shape=jax.ShapeDtypeStruct((t, HY_WIDTH), BF16),
        grid=(t // tm,),
        in_specs=[row, row, row, pl.BlockSpec((8, HY_WIDTH), lambda i: (0, 0)),
                  pl.BlockSpec((1, HY_WIDTH), lambda i: (0, 0))],
        out_specs=row,
        compiler_params=_cparams("parallel"),
        name="hyena_post",
    )(x0, z, conv, asum, bias.reshape(1, HY_WIDTH))


def _hyena_mixer(p_hy, conv_w, conv_b, filt, bias):
    seq_len = p_hy.shape[0]
    x0, z = _hyena_pre(p_hy, conv_w, conv_b)
    taps, asum = _hyena_taps(seq_len, *filt)
    if 2 * seq_len // FFT_N2 >= 16:
        conv = _hyena_long_conv(z, taps)
    else:
        conv = _hyena_small_conv(z, taps)
    return _hyena_post(x0, z, conv, asum, bias)


def _outproj_kernel(hy_ref, ret_ref, att_ref, w_ref, x_ref, gate_ref, o_ref):
    y = (_dot(hy_ref[...], w_ref[:HY_WIDTH, :])
         + _dot(ret_ref[...], w_ref[HY_WIDTH:HY_WIDTH + RET_WIDTH, :])
         + _dot(att_ref[...], w_ref[HY_WIDTH + RET_WIDTH:, :]))
    o_ref[...] = x_ref[...] + gate_ref[...] * y


def _out_projection(y_hy, y_ret, y_att, w_bf16, x, gate):
    t = x.shape[0]
    tm = min(t, 512)
    row = lambda w: pl.BlockSpec((tm, w), lambda i: (i, 0))
    return pl.pallas_call(
        _outproj_kernel,
        out_shape=jax.ShapeDtypeStruct((t, D_MODEL), F32),
        grid=(t // tm,),
        in_specs=[row(HY_WIDTH), row(RET_WIDTH), row(ATT_WIDTH),
                  pl.BlockSpec((MIX_WIDTH, D_MODEL), lambda i: (0, 0)),
                  row(D_MODEL), pl.BlockSpec((1, D_MODEL), lambda i: (0, 0))],
        out_specs=row(D_MODEL),
        compiler_params=_cparams("parallel"),
        name="out_projection",
    )(y_hy, y_ret, y_att, w_bf16, x, gate)


def _ffn_kernel(x_ref, g_ref, sh_ref, sc_ref, gate_ref, wg_ref, wu_ref, wd_ref, o_ref,
                h_ref, acc_ref):
    k = pl.program_id(1)

    @pl.when(k == 0)
    def _():
        h_ref[...] = _norm_mod(x_ref[...], g_ref[...], sh_ref[...], sc_ref[...]).astype(BF16)
        acc_ref[...] = jnp.zeros_like(acc_ref)

    h = h_ref[...]
    a = _silu(_dot(h, wg_ref[...])) * _dot(h, wu_ref[...])
    acc_ref[...] += _dot(a.astype(BF16), wd_ref[...])

    @pl.when(k == pl.num_programs(1) - 1)
    def _():
        o_ref[...] = x_ref[...] + gate_ref[...] * acc_ref[...]


def _dense_ffn(x, g, shift, scale, gate, wg, wu, wd):
    t = x.shape[0]
    tm = min(t, 1024)
    th = 256
    vec = pl.BlockSpec((1, D_MODEL), lambda i, k: (0, 0))
    return pl.pallas_call(
        _ffn_kernel,
        out_shape=jax.ShapeDtypeStruct((t, D_MODEL), F32),
        grid=(t // tm, FFN_HIDDEN // th),
        in_specs=[pl.BlockSpec((tm, D_MODEL), lambda i, k: (i, 0)), vec, vec, vec, vec,
                  pl.BlockSpec((D_MODEL, th), lambda i, k: (0, k)),
                  pl.BlockSpec((D_MODEL, th), lambda i, k: (0, k)),
                  pl.BlockSpec((th, D_MODEL), lambda i, k: (k, 0))],
        out_specs=pl.BlockSpec((tm, D_MODEL), lambda i, k: (i, 0)),
        scratch_shapes=[pltpu.VMEM((tm, D_MODEL), BF16), pltpu.VMEM((tm, D_MODEL), F32)],
        compiler_params=_cparams("parallel", "arbitrary"),
        name="dense_ffn",
    )(x, g, shift, scale, gate, wg, wu, wd)


def _moe_route_kernel(x_ref, g_ref, sh_ref, sc_ref, rw_ref, rb_ref, h_ref, info_ref):
    h = _norm_mod(x_ref[...], g_ref[...], sh_ref[...], sc_ref[...])
    h_ref[...] = h
    logits = _dot3(h, rw_ref[...]) + rb_ref[...]
    lane = lax.broadcasted_iota(jnp.int32, logits.shape, 1)
    v1 = jnp.max(logits, axis=-1, keepdims=True)
    i1 = jnp.min(jnp.where(logits == v1, lane, LANES), axis=-1, keepdims=True)
    rest = jnp.where(lane == i1, -jnp.inf, logits)
    v2 = jnp.max(rest, axis=-1, keepdims=True)
    i2 = jnp.min(jnp.where(rest == v2, lane, LANES), axis=-1, keepdims=True)
    e = jnp.exp(v2 - v1)
    g1 = 1.0 / (1.0 + e)
    g2 = e * g1
    info_ref[...] = jnp.where(lane == 0, i1.astype(F32),
                              jnp.where(lane == 1, i2.astype(F32),
                                        jnp.where(lane == 2, g1, jnp.where(lane == 3, g2, 0.0))))


def _moe_route(x, g, shift, scale, router_w, router_b):
    t = x.shape[0]
    tm = min(t, 512)
    rw = jnp.concatenate([router_w, jnp.zeros((D_MODEL, LANES - N_EXPERTS), F32)], axis=1)
    rb = jnp.concatenate([router_b, jnp.full((LANES - N_EXPERTS,), -jnp.inf, F32)]).reshape(1, LANES)
    vec = pl.BlockSpec((1, D_MODEL), lambda i: (0, 0))
    return pl.pallas_call(
        _moe_route_kernel,
        out_shape=(jax.ShapeDtypeStruct((t, D_MODEL), F32), jax.ShapeDtypeStruct((t, LANES), F32)),
        grid=(t // tm,),
        in_specs=[pl.BlockSpec((tm, D_MODEL), lambda i: (i, 0)), vec, vec, vec,
                  pl.BlockSpec((D_MODEL, LANES), lambda i: (0, 0)),
                  pl.BlockSpec((1, LANES), lambda i: (0, 0))],
        out_specs=(pl.BlockSpec((tm, D_MODEL), lambda i: (i, 0)),
                   pl.BlockSpec((tm, LANES), lambda i: (i, 0))),
        compiler_params=_cparams("parallel"),
        name="moe_route",
    )(x, g, shift, scale, rw, rb)


def _chunk_major_cols(w, th):
    *lead, d, hid = w.shape
    w = w.astype(BF16).reshape(*lead, d, hid // th, th)
    return jnp.swapaxes(w, -3, -2)


def _chunk_major_rows(w, th):
    *lead, hid, d = w.shape
    return w.astype(BF16).reshape(*lead, hid // th, th, d)


def _row_copy(src_hbm, dst_vmem, sem, src_row, dst_row):
    return pltpu.make_async_copy(src_hbm.at[pl.ds(src_row, 1)], dst_vmem.at[pl.ds(dst_row, 1)], sem)


def _moe_expert_kernel(blk_e_ref, nused_ref, src_ref, h_hbm, wg_ref, wu_ref, wd_ref, y_ref,
                       xbuf_ref, xb_ref, acc_ref, sem):
    b = pl.program_id(0)
    k = pl.program_id(1)
    last = pl.num_programs(1) - 1
    n_used = nused_ref[0]
    used = b < n_used

    def issue_rows(blk, slot):
        def body(r, carry):
            _row_copy(h_hbm, xbuf_ref.at[slot], sem.at[slot], src_ref[blk * MOE_BLOCK + r], r).start()
            return carry

        lax.fori_loop(0, MOE_BLOCK, body, 0, unroll=DMA_UNROLL)

    def wait_rows(slot):
        def body(r, carry):
            _row_copy(h_hbm, xbuf_ref.at[slot], sem.at[slot], 0, r).wait()
            return carry

        lax.fori_loop(0, MOE_BLOCK, body, 0, unroll=DMA_UNROLL)

    @pl.when(jnp.logical_and(used, k == 0))
    def _():
        slot = b % 2

        @pl.when(b == 0)
        def _():
            issue_rows(0, 0)

        wait_rows(slot)

        @pl.when(b + 1 < n_used)
        def _():
            issue_rows(b + 1, 1 - slot)

        xb_ref[...] = xbuf_ref[slot].astype(BF16)
        acc_ref[...] = jnp.zeros_like(acc_ref)

    @pl.when(used)
    def _():
        x = xb_ref[...]
        a = _silu(_dot(x, wg_ref[0, 0])) * _dot(x, wu_ref[0, 0])
        acc_ref[...] += _dot(a.astype(BF16), wd_ref[0, 0])

    @pl.when(jnp.logical_and(used, k == last))
    def _():
        y_ref[...] = acc_ref[...]

    @pl.when(jnp.logical_and(jnp.logical_not(used), k == last))
    def _():
        y_ref[...] = jnp.zeros_like(y_ref)


def _moe_experts(h, blk_expert, n_used, src_tok, wg, wu, wd, n_blk):
    nk, th = wg.shape[1], wg.shape[3]

    def w_index(b, k, be, nu, st):
        live = b < nu[0]
        return be[jnp.minimum(b, nu[0] - 1)], jnp.where(live, k, nk - 1), 0, 0

    grid_spec = pltpu.PrefetchScalarGridSpec(
        num_scalar_prefetch=3,
        grid=(n_blk, nk),
        in_specs=[pl.BlockSpec(memory_space=pl.ANY),
                  pl.BlockSpec((1, 1, D_MODEL, th), w_index),
                  pl.BlockSpec((1, 1, D_MODEL, th), w_index),
                  pl.BlockSpec((1, 1, th, D_MODEL), w_index)],
        out_specs=pl.BlockSpec((MOE_BLOCK, D_MODEL), lambda b, k, be, nu, st: (b, 0)),
        scratch_shapes=[pltpu.VMEM((2, MOE_BLOCK, D_MODEL), F32),
                        pltpu.VMEM((MOE_BLOCK, D_MODEL), BF16),
                        pltpu.VMEM((MOE_BLOCK, D_MODEL), F32),
                        pltpu.SemaphoreType.DMA((2,))],
    )
    return pl.pallas_call(
        _moe_expert_kernel,
        out_shape=jax.ShapeDtypeStruct((n_blk * MOE_BLOCK, D_MODEL), F32),
        grid_spec=grid_spec,
        compiler_params=_cparams("arbitrary", "arbitrary"),
        name="moe_experts",
    )(blk_expert, n_used, src_tok, h, wg, wu, wd)


def _moe_combine_kernel(dest_ref, x_ref, info_ref, gate_ref, ng_ref, y_hbm, o_ref,
                        y1_ref, y2_ref, sem, *, final_norm):
    i = pl.program_id(0)
    tm = x_ref.shape[0]
    slot = i % 2

    def issue_rows(blk, slot):
        def body(r, carry):
            a = 2 * (blk * tm + r)
            _row_copy(y_hbm, y1_ref.at[slot], sem.at[slot], dest_ref[a], r).start()
            _row_copy(y_hbm, y2_ref.at[slot], sem.at[slot], dest_ref[a + 1], r).start()
            return carry

        lax.fori_loop(0, tm, body, 0, unroll=DMA_UNROLL)

    def wait_rows(slot):
        def body(r, carry):
            _row_copy(y_hbm, y1_ref.at[slot], sem.at[slot], 0, r).wait()
            _row_copy(y_hbm, y2_ref.at[slot], sem.at[slot], 0, r).wait()
            return carry

        lax.fori_loop(0, tm, body, 0, unroll=DMA_UNROLL)

    @pl.when(i == 0)
    def _():
        issue_rows(0, 0)

    wait_rows(slot)

    @pl.when(i + 1 < pl.num_programs(0))
    def _():
        issue_rows(i + 1, 1 - slot)

    info = info_ref[...]
    y = y1_ref[slot] * info[:, 2:3] + y2_ref[slot] * info[:, 3:4]
    x = x_ref[...] + gate_ref[...] * y
    if final_norm:
        ms = jnp.mean(x * x, axis=-1, keepdims=True)
        x = x * lax.rsqrt(ms + EPS) * ng_ref[...]
    o_ref[...] = x


def _moe_combine(dest, x, info, gate, norm_g, y, final_norm):
    t = x.shape[0]
    tm = min(t, 256)
    vec = pl.BlockSpec((1, D_MODEL), lambda i, d: (0, 0))
    grid_spec = pltpu.PrefetchScalarGridSpec(
        num_scalar_prefetch=1,
        grid=(t // tm,),
        in_specs=[pl.BlockSpec((tm, D_MODEL), lambda i, d: (i, 0)),
                  pl.BlockSpec((tm, LANES), lambda i, d: (i, 0)), vec, vec,
                  pl.BlockSpec(memory_space=pl.ANY)],
        out_specs=pl.BlockSpec((tm, D_MODEL), lambda i, d: (i, 0)),
        scratch_shapes=[pltpu.VMEM((2, tm, D_MODEL), F32), pltpu.VMEM((2, tm, D_MODEL), F32),
                        pltpu.SemaphoreType.DMA((2,))],
    )
    return pl.pallas_call(
        functools.partial(_moe_combine_kernel, final_norm=final_norm),
        out_shape=jax.ShapeDtypeStruct((t, D_MODEL), F32),
        grid_spec=grid_spec,
        compiler_params=_cparams("arbitrary"),
        name="moe_combine",
    )(dest, x, info, gate, norm_g, y)


def _moe_layer(x, g, shift, scale, gate, router_w, router_b, wg, wu, wd, norm_g, final_norm):
    t = x.shape[0]
    n_asg = t * TOP_K
    h, info = _moe_route(x, g, shift, scale, router_w, router_b)
    expert = info[:, :TOP_K].astype(jnp.int32).reshape(-1)
    onehot = (expert[:, None] == jnp.arange(N_EXPERTS, dtype=jnp.int32)[None, :]).astype(jnp.int32)
    csum = jnp.cumsum(onehot, axis=0)
    counts = csum[-1]
    padded = (counts + MOE_BLOCK - 1) // MOE_BLOCK * MOE_BLOCK
    pad_end = jnp.cumsum(padded)
    pad_start = pad_end - padded
    dest = jnp.sum(onehot * (csum - 1 + pad_start[None, :]), axis=1).astype(jnp.int32)
    n_blk = -(-n_asg // MOE_BLOCK) + N_EXPERTS
    blk_start = jnp.arange(n_blk, dtype=jnp.int32) * MOE_BLOCK
    blk_expert = jnp.minimum(jnp.sum(blk_start[:, None] >= pad_end[None, :], axis=1),
                             N_EXPERTS - 1).astype(jnp.int32)
    n_used = (pad_end[-1:] // MOE_BLOCK).astype(jnp.int32)
    src_tok = jnp.zeros((n_blk * MOE_BLOCK,), jnp.int32).at[dest].set(
        jnp.arange(n_asg, dtype=jnp.int32) // TOP_K)
    y = _moe_experts(h, blk_expert, n_used, src_tok, wg, wu, wd, n_blk)
    return _moe_combine(dest, x, info, gate, norm_g, y, final_norm)


def _final_norm_kernel(x_ref, g_ref, o_ref):
    x = x_ref[...]
    ms = jnp.mean(x * x, axis=-1, keepdims=True)
    o_ref[...] = x * lax.rsqrt(ms + EPS) * g_ref[...]


def _final_norm(x, g):
    t = x.shape[0]
    tm = min(t, 1024)
    return pl.pallas_call(
        _final_norm_kernel,
        out_shape=jax.ShapeDtypeStruct((t, D_MODEL), F32),
        grid=(t // tm,),
        in_specs=[pl.BlockSpec((tm, D_MODEL), lambda i: (i, 0)),
                  pl.BlockSpec((1, D_MODEL), lambda i: (0, 0))],
        out_specs=pl.BlockSpec((tm, D_MODEL), lambda i: (i, 0)),
        compiler_params=_cparams("parallel"),
        name="final_norm",
    )(x, g)


def kernel(x, c, ctx, c_ctx, ada_w, ada_b, norm1_g, norm2_g, w_in, w_out, hy_conv_w, hy_conv_b, hy_filt_w1, hy_filt_b1, hy_filt_w2, hy_filt_b2, hy_filt_w3, hy_filt_b3, hy_filt_freq, hy_bias, ret_log_rate, attn_q_g, attn_k_g, ffn_w_gate, ffn_w_up, ffn_w_down, moe_router_w, moe_router_b, moe_w_gate, moe_w_up, moe_w_down, final_norm_g):
    assert x.shape[0] == 1 and c.shape[0] == 1
    seq_len = x.shape[1]
    x_lat = x[0]
    x_ctx = ctx[0]
    cos_t, sin_t = _rope_tables(seq_len)
    cvec = jnp.concatenate([c, c_ctx[None, :], jnp.zeros((6, D_MODEL), F32)], axis=0)
    cvec = _silu(cvec)
    row = lambda v: v.reshape(1, D_MODEL)
    zero_state = jnp.zeros((2, RET_WIDTH, RET_HEAD_DIM), F32)
    zero_tab = jnp.zeros((x_ctx.shape[0], LANES), F32)

    for l in range(DEPTH):
        last = l == DEPTH - 1
        mods = _ada_modulation(cvec, ada_w[l], ada_b[l]).reshape(8, 6, D_MODEL)
        mod = [row(mods[0, i]) for i in range(6)]
        mod_c = [row(mods[1, i]) for i in range(6)]
        filt = (hy_filt_w1[l], hy_filt_b1[l], hy_filt_w2[l], hy_filt_b2[l],
                hy_filt_w3[l], hy_filt_b3[l], hy_filt_freq[l])
        lg_f = -jnp.exp(ret_log_rate[l, 0].astype(F32))
        lg_b = -jnp.exp(ret_log_rate[l, 1].astype(F32))
        mask, zeta, xi, chunk_decay = _ret_tables(lg_f, lg_b)
        w_in_l = w_in[l].astype(BF16)
        w_out_l = w_out[l].astype(BF16)
        n1 = row(norm1_g[l])
        n2 = row(norm2_g[l])

        p_hy, p_ret, p_att = _in_projection(x_lat, n1, mod[0], mod[1], w_in_l)
        pc_hy, pc_ret, pc_att = _in_projection(x_ctx, n1, mod_c[0], mod_c[1], w_in_l)

        sprev_c, s_ctx = _retention_states(pc_ret, zeta, chunk_decay, zero_state)
        qc, ktc, vc, _ = _attention_prep(pc_att, zero_tab, zero_tab, attn_q_g[l], attn_k_g[l],
                                         rope=False)

        y_hy = _hyena_mixer(p_hy, hy_conv_w[l], hy_conv_b[l], filt, hy_bias[l])
        sprev, _ = _retention_states(p_ret, zeta, chunk_decay, s_ctx)
        y_ret = _retention_outputs(p_ret, mask, xi, sprev)
        q, kt, v, ksq = _attention_prep(p_att, cos_t, sin_t, attn_q_g[l], attn_k_g[l], rope=True)
        kmax = jnp.sqrt(jnp.max(ksq[:ATT_KV_HEADS], axis=1))
        y_att = _attention(q, ktc, vc, kmax, kt, v, tq=ATT_TQ, tk=ATT_TK, sub=ATT_SUB)

        if not last:
            yc_hy = _hyena_mixer(pc_hy, hy_conv_w[l], hy_conv_b[l], filt, hy_bias[l])
            yc_ret = _retention_outputs(pc_ret, mask, xi, sprev_c)
            yc_att = _attention(qc, ktc, vc, tq=x_ctx.shape[0])
            x_ctx = _out_projection(yc_hy, yc_ret, yc_att, w_out_l, x_ctx, mod_c[2])
        x_lat = _out_projection(y_hy, y_ret, y_att, w_out_l, x_lat, mod[2])

        i = l // 2
        if l % 2 == 0:
            wg, wu, wd = (ffn_w_gate[i].astype(BF16), ffn_w_up[i].astype(BF16),
                          ffn_w_down[i].astype(BF16))
            x_lat = _dense_ffn(x_lat, n2, mod[3], mod[4], mod[5], wg, wu, wd)
            if not last:
                x_ctx = _dense_ffn(x_ctx, n2, mod_c[3], mod_c[4], mod_c[5], wg, wu, wd)
        else:
            wg = _chunk_major_cols(moe_w_gate[i], MOE_TH)
            wu = _chunk_major_cols(moe_w_up[i], MOE_TH)
            wd = _chunk_major_rows(moe_w_down[i], MOE_TH)
            fg = row(final_norm_g)
            x_lat = _moe_layer(x_lat, n2, mod[3], mod[4], mod[5], moe_router_w[i], moe_router_b[i],
                               wg, wu, wd, fg, final_norm=last)
            if not last:
                x_ctx = _moe_layer(x_ctx, n2, mod_c[3], mod_c[4], mod_c[5], moe_router_w[i],
                                   moe_router_b[i], wg, wu, wd, fg, final_norm=False)
    if DEPTH % 2 == 1:
        x_lat = _final_norm(x_lat, row(final_norm_g))
    return x_lat[None]
```

```python
import functools
import math

import numpy as np
import jax
import jax.numpy as jnp
from jax import lax
from jax.experimental import pallas as pl
from jax.experimental.pallas import tpu as pltpu

F32 = jnp.float32
BF16 = jnp.bfloat16

D_MODEL = 1024
DEPTH = 2
GRID_W = 64
EPS = 1e-6

HY_WIDTH = 256
HY_COLS = 3 * HY_WIDTH
HY_POS_BANDS = 16
HY_DECAY_TARGET = 1e-2
HY_SHORT_DECAY_PCT = 0.3
HY_LONG_DECAY_PCT = 1.5

RET_HEAD_DIM = 64
RET_HEADS = 4
RET_WIDTH = RET_HEADS * RET_HEAD_DIM
RET_COLS = 4 * RET_WIDTH
RET_CHUNK = 128

ATT_HEAD_DIM = 64
ATT_HEADS = 8
ATT_KV_HEADS = 2
ATT_GROUP = ATT_HEADS // ATT_KV_HEADS
ATT_WIDTH = ATT_HEADS * ATT_HEAD_DIM
ATT_KV_WIDTH = ATT_KV_HEADS * ATT_HEAD_DIM
ATT_COLS = ATT_WIDTH + 2 * ATT_KV_WIDTH
ROPE_AXIS_DIM = ATT_HEAD_DIM // 2
ROPE_BASE = 10000.0

MIX_WIDTH = HY_WIDTH + RET_WIDTH + ATT_WIDTH
IN_COLS = HY_COLS + RET_COLS + ATT_COLS

FFN_HIDDEN = 2816
N_EXPERTS = 8
TOP_K = 2
MOE_BLOCK = 512

LOG2_E = 1.4426950408889634
EXP2_CAP = 60.0
KMAX_SLACK = 1.0 + 2.0 ** -7
ATT_TQ = 256
ATT_TK = 4096
ATT_SUB = 512
RET_STATE_CHUNKS = 8
RET_OUT_CHUNKS = 4
MOE_TH = 1408
DMA_UNROLL = 8

LANES = 128
VMEM_LIMIT = 56 * 1024 * 1024


def _cparams(*sem):
    return pltpu.CompilerParams(dimension_semantics=sem, vmem_limit_bytes=VMEM_LIMIT)


def _dot(a, b):
    return jnp.dot(a, b, preferred_element_type=F32)


def _split(a):
    hi = a.astype(BF16)
    lo = (a - hi.astype(F32)).astype(BF16)
    return hi, lo


def _dot3(a, b):
    ah, al = _split(a)
    bh, bl = _split(b)
    return _dot(ah, bh) + _dot(al, bh) + _dot(ah, bl)


def _silu(x):
    return x * (1.0 / (1.0 + jnp.exp(-x)))


def _norm_mod(x, g, shift, scale):
    ms = jnp.mean(x * x, axis=-1, keepdims=True)
    h = x * lax.rsqrt(ms + EPS) * g
    return h * (1.0 + scale) + shift


def _mm3_kernel(a_ref, b_ref, o_ref):
    o_ref[...] = _dot3(a_ref[...], b_ref[...])


def _mm3(a, b, tn, name):
    m, k = a.shape
    n = b.shape[1]
    return pl.pallas_call(
        _mm3_kernel,
        out_shape=jax.ShapeDtypeStruct((m, n), F32),
        grid=(n // tn,),
        in_specs=[pl.BlockSpec((m, k), lambda j: (0, 0)),
                  pl.BlockSpec((k, tn), lambda j: (0, j))],
        out_specs=pl.BlockSpec((m, tn), lambda j: (0, j)),
        compiler_params=_cparams("parallel"),
        name=name,
    )(a, b)


def _ada_kernel(c_ref, w_ref, b_ref, o_ref):
    o_ref[...] = _dot3(c_ref[...], w_ref[...]) + b_ref[...]


def _ada_modulation(cc, w, b):
    n = w.shape[1]
    tn = 1536
    return pl.pallas_call(
        _ada_kernel,
        out_shape=jax.ShapeDtypeStruct((8, n), F32),
        grid=(n // tn,),
        in_specs=[pl.BlockSpec((8, D_MODEL), lambda j: (0, 0)),
                  pl.BlockSpec((D_MODEL, tn), lambda j: (0, j)),
                  pl.BlockSpec((1, tn), lambda j: (0, j))],
        out_specs=pl.BlockSpec((8, tn), lambda j: (0, j)),
        compiler_params=_cparams("parallel"),
        name="ada_modulation",
    )(cc, w, b.reshape(1, n))


def _inproj_kernel(x_ref, g_ref, sh_ref, sc_ref, w_ref, hy_ref, ret_ref, att_ref):
    h = _norm_mod(x_ref[...], g_ref[...], sh_ref[...], sc_ref[...])
    p = _dot(h.astype(BF16), w_ref[...])
    hy_ref[...] = p[:, :HY_COLS]
    ret_ref[...] = p[:, HY_COLS:HY_COLS + RET_COLS]
    att_ref[...] = p[:, HY_COLS + RET_COLS:]


def _in_projection(x, g, shift, scale, w_bf16):
    t = x.shape[0]
    tm = min(t, 512)
    vec = pl.BlockSpec((1, D_MODEL), lambda i: (0, 0))
    return pl.pallas_call(
        _inproj_kernel,
        out_shape=(jax.ShapeDtypeStruct((t, HY_COLS), F32),
                   jax.ShapeDtypeStruct((t, RET_COLS), F32),
                   jax.ShapeDtypeStruct((t, ATT_COLS), F32)),
        grid=(t // tm,),
        in_specs=[pl.BlockSpec((tm, D_MODEL), lambda i: (i, 0)), vec, vec, vec,
                  pl.BlockSpec((D_MODEL, IN_COLS), lambda i: (0, 0))],
        out_specs=(pl.BlockSpec((tm, HY_COLS), lambda i: (i, 0)),
                   pl.BlockSpec((tm, RET_COLS), lambda i: (i, 0)),
                   pl.BlockSpec((tm, ATT_COLS), lambda i: (i, 0))),
        compiler_params=_cparams("parallel"),
        name="in_projection",
    )(x, g, shift, scale, w_bf16)


def _head_mean_matrix(width):
    idx = np.arange(width) // ATT_HEAD_DIM
    return jnp.asarray((idx[:, None] == idx[None, :]).astype(np.float32) / ATT_HEAD_DIM, BF16)


def _head_rms(x, bd, g):
    x2h, x2l = _split(x * x)
    ms = _dot(x2h, bd) + _dot(x2l, bd)
    return x * lax.rsqrt(ms + EPS) * g


def _rope(x, cos, sin):
    n = x.shape[1]
    lane = lax.broadcasted_iota(jnp.int32, x.shape, 1)
    swapped = jnp.where((lane % 32) < 16, pltpu.roll(x, n - 16, 1), pltpu.roll(x, 16, 1))
    return x * cos + swapped * sin


def _attprep_kernel(p_ref, cos_ref, sin_ref, qg_ref, kg_ref, bdq_ref, bdk_ref,
                    qt_ref, k_ref, vt_ref, kmax_ref, *, rope):
    p = p_ref[...]
    q = _head_rms(p[:, :ATT_WIDTH], bdq_ref[...], qg_ref[...])
    k = _head_rms(p[:, ATT_WIDTH:ATT_WIDTH + ATT_KV_WIDTH], bdk_ref[...], kg_ref[...])
    if rope:
        cos = cos_ref[...]
        sin = sin_ref[...]
        q = jnp.concatenate(
            [_rope(q[:, j * LANES:(j + 1) * LANES], cos, sin) for j in range(ATT_WIDTH // LANES)],
            axis=1)
        k = _rope(k, cos, sin)
    qt_ref[...] = (q * (ATT_HEAD_DIM ** -0.5 * LOG2_E)).T.astype(BF16)
    tm = k.shape[0]
    lane = lax.broadcasted_iota(jnp.int32, k.shape, 1)
    minus_one_col = jnp.where(lane == ATT_HEAD_DIM, -1.0, 0.0)
    kb = k.astype(BF16)
    k_ref[0] = jnp.where(lane < ATT_HEAD_DIM, k, minus_one_col).astype(BF16)
    k_ref[1] = jnp.where(lane < ATT_HEAD_DIM, pltpu.roll(k, ATT_HEAD_DIM, 1), minus_one_col).astype(BF16)
    sq_hi, sq_lo = _split(kb.astype(F32) ** 2)
    norm_sq = (_dot(sq_hi, bdk_ref[...]) + _dot(sq_lo, bdk_ref[...])) * ATT_HEAD_DIM
    kmax_ref[0] = jnp.broadcast_to(jnp.max(norm_sq, axis=0, keepdims=True), (8, LANES))
    vt = p[:, ATT_WIDTH + ATT_KV_WIDTH:].T.astype(BF16)
    sub = lax.broadcasted_iota(jnp.int32, (ATT_HEAD_DIM, tm), 0)
    ones_row = jnp.where(sub == 0, 1.0, 0.0).astype(BF16)
    for h in range(ATT_KV_HEADS):
        vt_ref[h * LANES:h * LANES + ATT_HEAD_DIM, :] = vt[h * ATT_HEAD_DIM:(h + 1) * ATT_HEAD_DIM]
        vt_ref[h * LANES + ATT_HEAD_DIM:(h + 1) * LANES, :] = ones_row


def _attention_prep(p_att, cos, sin, q_g, k_g, rope):
    t = p_att.shape[0]
    tm = min(t, 512)
    qg = jnp.tile(q_g, ATT_HEADS).reshape(1, ATT_WIDTH)
    kg = jnp.tile(k_g, ATT_KV_HEADS).reshape(1, ATT_KV_WIDTH)
    const = lambda shape: pl.BlockSpec(shape, lambda i: (0, 0))
    return pl.pallas_call(
        functools.partial(_attprep_kernel, rope=rope),
        out_shape=(jax.ShapeDtypeStruct((ATT_WIDTH, t), BF16),
                   jax.ShapeDtypeStruct((ATT_KV_HEADS, t, LANES), BF16),
                   jax.ShapeDtypeStruct((ATT_KV_HEADS * LANES, t), BF16),
                   jax.ShapeDtypeStruct((t // tm, 8, LANES), F32)),
        grid=(t // tm,),
        in_specs=[pl.BlockSpec((tm, ATT_COLS), lambda i: (i, 0)),
                  pl.BlockSpec((tm, LANES), lambda i: (i, 0)),
                  pl.BlockSpec((tm, LANES), lambda i: (i, 0)),
                  const((1, ATT_WIDTH)), const((1, ATT_KV_WIDTH)),
                  const((ATT_WIDTH, ATT_WIDTH)), const((ATT_KV_WIDTH, ATT_KV_WIDTH))],
        out_specs=(pl.BlockSpec((ATT_WIDTH, tm), lambda i: (0, i)),
                   pl.BlockSpec((ATT_KV_HEADS, tm, LANES), lambda i: (0, i, 0)),
                   pl.BlockSpec((ATT_KV_HEADS * LANES, tm), lambda i: (0, i)),
                   pl.BlockSpec((1, 8, LANES), lambda i: (i, 0, 0))),
        compiler_params=_cparams("parallel"),
        name="attention_prep",
    )(p_att, cos, sin, qg, kg, _head_mean_matrix(ATT_WIDTH), _head_mean_matrix(ATT_KV_WIDTH))


def _rope_tables(n_tokens):
    rows = n_tokens // GRID_W
    inv_freq = ROPE_BASE ** (-jnp.arange(0, ROPE_AXIS_DIM, 2, dtype=F32) / ROPE_AXIS_DIM)
    ang_r = jnp.arange(rows, dtype=F32)[:, None] * inv_freq[None, :]
    ang_c = jnp.arange(GRID_W, dtype=F32)[:, None] * inv_freq[None, :]
    nf = inv_freq.shape[0]
    by_row = lambda tab: jnp.broadcast_to(tab[:, None, :], (rows, GRID_W, nf)).reshape(n_tokens, nf)
    by_col = lambda tab: jnp.broadcast_to(tab[None, :, :], (rows, GRID_W, nf)).reshape(n_tokens, nf)
    cr, sr = by_row(jnp.cos(ang_r)), by_row(jnp.sin(ang_r))
    cc, sc = by_col(jnp.cos(ang_c)), by_col(jnp.sin(ang_c))
    cos = jnp.concatenate([cr, cr, cc, cc], axis=1)
    sin = jnp.concatenate([-sr, sr, -sc, sc], axis=1)
    return jnp.tile(cos, (1, 2)), jnp.tile(sin, (1, 2))


def _attention_kernel(*refs, tq, tk, sub, n_lat):
    if n_lat:
        kmax_ref, qt_ref, kc_ref, vct_ref, k_ref, vt_ref, o_ref, qs_ref, acc_ref, m_ref = refs
    else:
        qt_ref, kc_ref, vct_ref, o_ref, acc_ref = refs
    cols = ATT_GROUP * tq
    qt = qt_ref[...]
    qst = jnp.concatenate(
        [qt[g * ATT_HEAD_DIM:(g + 1) * ATT_HEAD_DIM, :] for g in range(ATT_GROUP)], axis=1)
    pad = jnp.zeros((LANES - ATT_HEAD_DIM, cols), BF16)
    s = _dot(kc_ref[0], jnp.concatenate([qst, pad], axis=0))
    mt = jnp.max(s, axis=0, keepdims=True).astype(BF16)
    mtf = mt.astype(F32)
    acc_ref[...] = _dot(vct_ref[...], jnp.exp2(s - mtf).astype(BF16))

    if n_lat:
        row = lax.broadcasted_iota(jnp.int32, pad.shape, 0)
        qs_ref[...] = jnp.concatenate([qst, jnp.where(row == 0, mtf, 0.0).astype(BF16)], axis=0)
        qf = qst.astype(F32)
        qn = jnp.sqrt(jnp.sum(qf * qf, axis=0, keepdims=True))
        fast = jnp.max(qn * kmax_ref[pl.program_id(0)] - mtf) <= EXP2_CAP

        def tile(j):
            off = pl.multiple_of(j * tk, tk)
            return k_ref[0, pl.ds(off, tk), :], vt_ref[:, pl.ds(off, tk)]

        @pl.when(fast)
        def _():
            def body(j, carry):
                k, vt = tile(j)
                qs = qs_ref[...]
                part = None
                for c in range(tk // sub):
                    p = jnp.exp2(_dot(k[c * sub:(c + 1) * sub], qs)).astype(BF16)
                    pv = _dot(vt[:, c * sub:(c + 1) * sub], p)
                    part = pv if part is None else part + pv
                acc_ref[...] += part
                return carry

            lax.fori_loop(0, n_lat, body, 0)

        @pl.when(jnp.logical_not(fast))
        def _():
            m_ref[...] = jnp.zeros_like(m_ref)

            def body(j, carry):
                k, vt = tile(j)
                s = _dot(k, qs_ref[...])
                m_old = m_ref[...]
                m_new = jnp.maximum(m_old, jnp.max(s, axis=0, keepdims=True))
                p = jnp.exp2(s - m_new).astype(BF16)
                acc_ref[...] = jnp.exp2(m_old - m_new) * acc_ref[...] + _dot(vt, p)
                m_ref[...] = m_new
                return carry

            lax.fori_loop(0, n_lat, body, 0)

    acc = acc_ref[...]
    o = acc[:ATT_HEAD_DIM] / acc[ATT_HEAD_DIM:ATT_HEAD_DIM + 1]
    o_ref[...] = jnp.concatenate(
        [o[:, g * tq:(g + 1) * tq].T for g in range(ATT_GROUP)], axis=1).astype(o_ref.dtype)


def _attention(qt, kc, vct, kmax=None, k=None, vt=None, *, tq, tk=0, sub=0):
    t = qt.shape[1]
    lc = kc.shape[1]
    gw = ATT_GROUP * ATT_HEAD_DIM
    cols = ATT_GROUP * tq
    in_specs = [pl.BlockSpec((gw, tq), lambda h, i: (h, i)),
                pl.BlockSpec((1, lc, LANES), lambda h, i: (h, 0, 0)),
                pl.BlockSpec((LANES, lc), lambda h, i: (h, 0))]
    args = [qt, kc, vct]
    scratch = [pltpu.VMEM((LANES, cols), F32)]
    n_lat = 0
    if k is not None:
        lk = k.shape[1]
        n_lat = lk // tk
        in_specs = ([pl.BlockSpec(memory_space=pltpu.SMEM)] + in_specs
                    + [pl.BlockSpec((1, lk, LANES), lambda h, i: (h, 0, 0)),
                       pl.BlockSpec((LANES, lk), lambda h, i: (h, 0))])
        args = [kmax] + args + [k, vt]
        scratch = [pltpu.VMEM((LANES, cols), BF16)] + scratch + [pltpu.VMEM((1, cols), F32)]
    return pl.pallas_call(
        functools.partial(_attention_kernel, tq=tq, tk=tk, sub=sub, n_lat=n_lat),
        out_shape=jax.ShapeDtypeStruct((t, ATT_WIDTH), BF16),
        grid=(ATT_KV_HEADS, t // tq),
        in_specs=in_specs,
        out_specs=pl.BlockSpec((tq, gw), lambda h, i: (i, h)),
        scratch_shapes=scratch,
        compiler_params=_cparams("parallel", "parallel"),
        name="attention_lat" if n_lat else "attention_ctx",
    )(*args)


def _ret_tables(lg_f, lg_b):
    pos = jnp.arange(RET_CHUNK, dtype=F32)
    rel = pos[:, None] - pos[None, :]
    mask = jnp.where(rel > 0, jnp.exp(lg_f[:, None, None] * jnp.maximum(rel, 0.0)),
                     jnp.where(rel < 0, jnp.exp(lg_b[:, None, None] * jnp.maximum(-rel, 0.0)), 2.0))
    zeta = jnp.stack([jnp.exp(lg_f[:, None] * (RET_CHUNK - 1 - pos)[None, :]),
                      jnp.exp(lg_b[:, None] * pos[None, :])])
    xi = jnp.stack([jnp.exp(lg_f[:, None] * (pos + 1)[None, :]),
                    jnp.exp(lg_b[:, None] * (RET_CHUNK - pos)[None, :])])
    chunk_decay = jnp.stack([jnp.exp(lg_f * RET_CHUNK), jnp.exp(lg_b * RET_CHUNK)])
    zeta = jnp.repeat(zeta, RET_HEAD_DIM, axis=1).transpose(0, 2, 1)
    xi = jnp.repeat(xi, RET_HEAD_DIM, axis=1).transpose(0, 2, 1)
    chunk_decay = jnp.repeat(chunk_decay, RET_HEAD_DIM, axis=1)[:, :, None]
    return mask, zeta, xi, chunk_decay


def _ret_state_kernel(k_ref, v_ref, zeta_ref, cd_ref, s0_ref, sprev_ref, sfin_ref, s_ref, *, cps):
    d = pl.program_id(0)

    @pl.when(pl.program_id(1) == 0)
    def _():
        s_ref[...] = s0_ref[0]

    zeta = zeta_ref[0] * (RET_HEAD_DIM ** -0.5)
    cd = cd_ref[0]
    us = []
    for c in range(cps):
        rows = slice(c * RET_CHUNK, (c + 1) * RET_CHUNK)
        kz = (k_ref[rows, :] * zeta).astype(BF16)
        v = v_ref[rows, :].astype(BF16)
        heads = []
        for h in range(RET_HEADS):
            sl = slice(h * RET_HEAD_DIM, (h + 1) * RET_HEAD_DIM)
            heads.append(lax.dot_general(kz[:, sl], v[:, sl], (((0,), (0,)), ((), ())),
                                         preferred_element_type=F32))
        us.append(jnp.concatenate(heads, axis=0))

    def scan(order):
        s = s_ref[...]
        for c in order:
            sprev_ref[0, c] = s
            s = cd * s + us[c]
        s_ref[...] = s
        sfin_ref[0] = s

    @pl.when(d == 0)
    def _():
        scan(range(cps))

    @pl.when(d == 1)
    def _():
        scan(reversed(range(cps)))


def _retention_states(p_ret, zeta, chunk_decay, s0):
    nc = p_ret.shape[0] // RET_CHUNK
    cps = min(nc, RET_STATE_CHUNKS)
    ng = nc // cps
    rows = cps * RET_CHUNK

    def group(d, i):
        return i + d * (ng - 1 - 2 * i)

    return pl.pallas_call(
        functools.partial(_ret_state_kernel, cps=cps),
        out_shape=(jax.ShapeDtypeStruct((2, nc, RET_WIDTH, RET_HEAD_DIM), F32),
                   jax.ShapeDtypeStruct((2, RET_WIDTH, RET_HEAD_DIM), F32)),
        grid=(2, ng),
        in_specs=[pl.BlockSpec((rows, RET_WIDTH), lambda d, i: (group(d, i), 1)),
                  pl.BlockSpec((rows, RET_WIDTH), lambda d, i: (group(d, i), 2)),
                  pl.BlockSpec((1, RET_CHUNK, RET_WIDTH), lambda d, i: (d, 0, 0)),
                  pl.BlockSpec((1, RET_WIDTH, 1), lambda d, i: (d, 0, 0)),
                  pl.BlockSpec((1, RET_WIDTH, RET_HEAD_DIM), lambda d, i: (d, 0, 0))],
        out_specs=(pl.BlockSpec((1, cps, RET_WIDTH, RET_HEAD_DIM), lambda d, i: (d, group(d, i), 0, 0)),
                   pl.BlockSpec((1, RET_WIDTH, RET_HEAD_DIM), lambda d, i: (d, 0, 0))),
        scratch_shapes=[pltpu.VMEM((RET_WIDTH, RET_HEAD_DIM), F32)],
        compiler_params=_cparams("arbitrary", "arbitrary"),
        name="retention_states",
    )(p_ret, p_ret, zeta, chunk_decay, s0)


def _ret_out_kernel(q_ref, k_ref, v_ref, g_ref, mask_ref, xi_ref, sf_ref, sb_ref, y_ref, *, cpo):
    for c in range(cpo):
        rows = slice(c * RET_CHUNK, (c + 1) * RET_CHUNK)
        q = q_ref[rows, :]
        k = (k_ref[rows, :] * (RET_HEAD_DIM ** -0.5)).astype(BF16)
        v = v_ref[rows, :].astype(BF16)
        qb = q.astype(BF16)
        qf = (q * xi_ref[0]).astype(BF16)
        qr = (q * xi_ref[1]).astype(BF16)
        sf = sf_ref[0, c].astype(BF16)
        sb = sb_ref[0, c].astype(BF16)
        ys = []
        for h in range(RET_HEADS):
            sl = slice(h * RET_HEAD_DIM, (h + 1) * RET_HEAD_DIM)
            scores = lax.dot_general(qb[:, sl], k[:, sl], (((1,), (1,)), ((), ())),
                                     preferred_element_type=F32)
            y = _dot((scores * mask_ref[h]).astype(BF16), v[:, sl])
            y = y + _dot(qf[:, sl], sf[sl, :]) + _dot(qr[:, sl], sb[sl, :])
            mu = jnp.mean(y, axis=-1, keepdims=True)
            yc = y - mu
            var = jnp.mean(yc * yc, axis=-1, keepdims=True)
            ys.append(yc * lax.rsqrt(var + EPS))
        y_ref[rows, :] = (_silu(g_ref[rows, :]) * jnp.concatenate(ys, axis=1)).astype(y_ref.dtype)


def _retention_outputs(p_ret, mask, xi, sprev):
    nc = p_ret.shape[0] // RET_CHUNK
    cpo = min(nc, RET_OUT_CHUNKS)
    rows = cpo * RET_CHUNK
    col = lambda c: pl.BlockSpec((rows, RET_WIDTH), lambda i: (i, c))
    return pl.pallas_call(
        functools.partial(_ret_out_kernel, cpo=cpo),
        out_shape=jax.ShapeDtypeStruct((p_ret.shape[0], RET_WIDTH), BF16),
        grid=(nc // cpo,),
        in_specs=[col(0), col(1), col(2), col(3),
                  pl.BlockSpec((RET_HEADS, RET_CHUNK, RET_CHUNK), lambda i: (0, 0, 0)),
                  pl.BlockSpec((2, RET_CHUNK, RET_WIDTH), lambda i: (0, 0, 0)),
                  pl.BlockSpec((1, cpo, RET_WIDTH, RET_HEAD_DIM), lambda i: (0, i, 0, 0)),
                  pl.BlockSpec((1, cpo, RET_WIDTH, RET_HEAD_DIM), lambda i: (1, i, 0, 0))],
        out_specs=pl.BlockSpec((rows, RET_WIDTH), lambda i: (i, 0)),
        compiler_params=_cparams("parallel"),
        name="retention_outputs",
    )(p_ret, p_ret, p_ret, p_ret, mask, xi, sprev, sprev)


def _hy_pre_kernel(p_ref, prev_ref, next_ref, w_ref, b_ref, x0_ref, z_ref):
    i = pl.program_id(0)
    x = p_ref[...]
    tm = x.shape[0]
    row = lax.broadcasted_iota(jnp.int32, x.shape, 0)
    prev_row = jnp.where(i == 0, 0.0, prev_ref[7:8, :])
    next_row = jnp.where(i == pl.num_programs(0) - 1, 0.0, next_ref[0:1, :])
    up = jnp.where(row == 0, prev_row, pltpu.roll(x, 1, 0))
    dn = jnp.where(row == tm - 1, next_row, pltpu.roll(x, tm - 1, 0))
    w = w_ref[...]
    u = up * w[0:1] + x * w[1:2] + dn * w[2:3] + b_ref[...]
    x0_ref[...] = u[:, :HY_WIDTH]
    z_ref[...] = u[:, HY_WIDTH:2 * HY_WIDTH] * u[:, 2 * HY_WIDTH:]


def _hyena_pre(p_hy, conv_w, conv_b):
    t = p_hy.shape[0]
    tm = min(t, 512)
    nb8 = tm // 8
    last8 = t // 8 - 1
    return pl.pallas_call(
        _hy_pre_kernel,
        out_shape=(jax.ShapeDtypeStruct((t, HY_WIDTH), F32),
                   jax.ShapeDtypeStruct((t, HY_WIDTH), F32)),
        grid=(t // tm,),
        in_specs=[pl.BlockSpec((tm, HY_COLS), lambda i: (i, 0)),
                  pl.BlockSpec((8, HY_COLS), lambda i: (jnp.maximum(i * nb8 - 1, 0), 0)),
                  pl.BlockSpec((8, HY_COLS), lambda i: (jnp.minimum((i + 1) * nb8, last8), 0)),
                  pl.BlockSpec((3, HY_COLS), lambda i: (0, 0)),
                  pl.BlockSpec((1, HY_COLS), lambda i: (0, 0))],
        out_specs=(pl.BlockSpec((tm, HY_WIDTH), lambda i: (i, 0)),
                   pl.BlockSpec((tm, HY_WIDTH), lambda i: (i, 0))),
        compiler_params=_cparams("parallel"),
        name="hyena_pre",
    )(p_hy, p_hy, p_hy, conv_w, conv_b.reshape(1, HY_COLS))


def _hy_filter_kernel(feat_ref, w1_ref, b1_ref, w2_ref, b2_ref, w3_ref, b3_ref, freq_ref,
                      delta_ref, taps_ref, asum_ref, *, seq_len):
    i = pl.program_id(0)
    feat = feat_ref[...]
    freq = freq_ref[...]
    h = jnp.sin(freq * (_dot3(feat, w1_ref[...]) + b1_ref[...]))
    h = jnp.sin(freq * (_dot3(h, w2_ref[...]) + b2_ref[...]))
    h = _dot3(h, w3_ref[...]) + b3_ref[...]
    tm = feat.shape[0]
    j = i * tm + lax.broadcasted_iota(jnp.int32, (tm, HY_WIDTH), 0)
    t = feat[:, 0:1]
    hsel = jnp.where(j < seq_len, h[:, :HY_WIDTH], h[:, HY_WIDTH:])
    taps = jnp.where(j == seq_len, 0.0, hsel * jnp.exp(-t * delta_ref[...]))
    taps_ref[...] = taps

    @pl.when(i == 0)
    def _():
        asum_ref[...] = jnp.zeros_like(asum_ref)

    asum_ref[...] += jnp.sum(jnp.abs(taps).reshape(tm // 8, 8, HY_WIDTH), axis=0)


def _hyena_taps(seq_len, w1, b1, w2, b2, w3, b3, freq):
    n = 2 * seq_len
    j = np.arange(n)
    pos = np.where(j <= seq_len, j, n - j).astype(np.float32)
    pos = jnp.asarray(pos)
    tt = pos / max(seq_len - 1, 1)
    bands = jnp.linspace(1e-4, HY_POS_BANDS - 1, HY_POS_BANDS, dtype=F32)
    ang = (2.0 * math.pi / seq_len) * pos[:, None] * bands[None, :]
    feat = jnp.concatenate([tt[:, None], jnp.cos(ang), -jnp.sin(ang),
                            jnp.zeros((n, LANES - 1 - 2 * HY_POS_BANDS), F32)], axis=-1)
    w1p = jnp.concatenate([w1, jnp.zeros((LANES - w1.shape[0], w1.shape[1]), F32)], axis=0)
    deltas = jnp.abs(jnp.linspace(math.log(HY_DECAY_TARGET) / HY_LONG_DECAY_PCT,
                                  math.log(HY_DECAY_TARGET) / HY_SHORT_DECAY_PCT,
                                  HY_WIDTH, dtype=F32)).reshape(1, HY_WIDTH)
    tm = min(n, 1024)
    hid = w1.shape[1]
    const = lambda shape: pl.BlockSpec(shape, lambda i: (0, 0))
    return pl.pallas_call(
        functools.partial(_hy_filter_kernel, seq_len=seq_len),
        out_shape=(jax.ShapeDtypeStruct((n, HY_WIDTH), F32),
                   jax.ShapeDtypeStruct((8, HY_WIDTH), F32)),
        grid=(n // tm,),
        in_specs=[pl.BlockSpec((tm, LANES), lambda i: (i, 0)),
                  const((LANES, hid)), const((1, hid)), const((hid, hid)), const((1, hid)),
                  const((hid, 2 * HY_WIDTH)), const((1, 2 * HY_WIDTH)), const((1, hid)),
                  const((1, HY_WIDTH))],
        out_specs=(pl.BlockSpec((tm, HY_WIDTH), lambda i: (i, 0)), const((8, HY_WIDTH))),
        compiler_params=_cparams("arbitrary"),
        name="hyena_taps",
    )(feat, w1p, b1.reshape(1, hid), w2, b2.reshape(1, hid), w3, b3.reshape(1, 2 * HY_WIDTH),
      freq.reshape(1, hid), deltas)


FFT_N2 = 256


def _dft_tables(n_total, n1_in):
    n1 = n_total // FFT_N2
    k1 = np.arange(n1)
    a = 2.0 * np.pi * np.outer(k1, np.arange(n1_in)) / n1
    lvl1 = np.concatenate([np.cos(a), -np.sin(a)], axis=0)
    n2 = np.arange(FFT_N2)
    th = 2.0 * np.pi * np.outer(n2, n2) / FFT_N2
    fc, fs = np.cos(th), np.sin(th)
    g = np.block([[fc, fs], [-fs, fc]])
    ph = 2.0 * np.pi * np.outer(k1, n2) / n_total
    tw = np.stack([np.cos(ph), np.sin(ph)])[..., None]
    return (jnp.asarray(lvl1, F32), jnp.asarray(g, F32), jnp.asarray(g.T, F32), jnp.asarray(tw, F32))


def _inverse_lvl1_table(n_total, n1_out):
    n1 = n_total // FFT_N2
    a = 2.0 * np.pi * np.outer(np.arange(n1_out), np.arange(n1)) / n1
    return jnp.asarray(np.concatenate([np.cos(a), -np.sin(a)], axis=1), F32)


def _lvl2_kernel(b_ref, tw_ref, gh_ref, gl_ref, *rest, inverse):
    c = tw_ref[0, 0]
    s = tw_ref[1, 0]
    br = b_ref[0, 0]
    bi = b_ref[1, 0]
    d = jnp.concatenate([c * br + s * bi, c * bi - s * br], axis=0)
    dh, dl = _split(d)
    x = _dot(gh_ref[...], dh) + _dot(gh_ref[...], dl) + _dot(gl_ref[...], dh)
    if not inverse:
        (o_ref,) = rest
        o_ref[0, 0] = x[:FFT_N2]
        o_ref[1, 0] = x[FFT_N2:]
        return
    h_ref, gth_ref, gtl_ref, o_ref = rest
    xr, xi = x[:FFT_N2], x[FFT_N2:]
    hr, hi = h_ref[0, 0], h_ref[1, 0]
    y = jnp.concatenate([xr * hr - xi * hi, xr * hi + xi * hr], axis=0)
    yh, yl = _split(y)
    cc = _dot(gth_ref[...], yh) + _dot(gth_ref[...], yl) + _dot(gtl_ref[...], yh)
    cr, ci = cc[:FFT_N2], cc[FFT_N2:]
    o_ref[0, 0] = c * cr - s * ci
    o_ref[1, 0] = c * ci + s * cr


def _level2(b, tw, g, h=None, gt=None):
    _, n1, n2, ch = b.shape
    blk = pl.BlockSpec((2, 1, n2, ch), lambda i: (0, i, 0, 0))
    const = pl.BlockSpec((2 * n2, 2 * n2), lambda i: (0, 0))
    gh, gl = _split(g)
    in_specs = [blk, pl.BlockSpec((2, 1, n2, 1), lambda i: (0, i, 0, 0)), const, const]
    args = [b, tw, gh, gl]
    if h is not None:
        gth, gtl = _split(gt)
        in_specs += [blk, const, const]
        args += [h, gth, gtl]
    return pl.pallas_call(
        functools.partial(_lvl2_kernel, inverse=h is not None),
        out_shape=jax.ShapeDtypeStruct(b.shape, F32),
        grid=(n1,),
        in_specs=in_specs,
        out_specs=blk,
        compiler_params=_cparams("parallel"),
        name="hyena_fft_level2" + ("_conv" if h is not None else ""),
    )(*args)


def _hyena_long_conv(z, taps):
    seq_len, ch = z.shape
    n = 2 * seq_len
    n1 = n // FFT_N2
    lvl1_z, g, gt, tw = _dft_tables(n, n1 // 2)
    lvl1_t = _dft_tables(n, n1)[0]
    inv1 = _inverse_lvl1_table(n, n1 // 2)
    cols = FFT_N2 * ch
    tn = min(cols, 4096)
    hb = _mm3(lvl1_t, taps.reshape(n1, cols), tn, "hyena_fft_level1_taps")
    hspec = _level2(hb.reshape(2, n1, FFT_N2, ch), tw, g)
    zb = _mm3(lvl1_z, z.reshape(n1 // 2, cols), tn, "hyena_fft_level1_z")
    cspec = _level2(zb.reshape(2, n1, FFT_N2, ch), tw, g, hspec, gt)
    conv = _mm3(inv1, cspec.reshape(2 * n1, cols), tn, "hyena_fft_inverse_level1")
    return conv.reshape(seq_len, ch)


def _small_conv_kernel(z_ref, taps_ref, fc_ref, fs_ref, o_ref):
    seq_len = z_ref.shape[0]
    fc = fc_ref[...]
    fs = fs_ref[...]
    z = z_ref[...]
    taps = taps_ref[...]
    zr = _dot3(fc[:, :seq_len], z)
    zi = -_dot3(fs[:, :seq_len], z)
    hr = _dot3(fc, taps)
    hi = -_dot3(fs, taps)
    yr = zr * hr - zi * hi
    yi = zr * hi + zi * hr
    o_ref[...] = _dot3(fc[:seq_len, :], yr) - _dot3(fs[:seq_len, :], yi)


def _hyena_small_conv(z, taps):
    seq_len, ch = z.shape
    n = 2 * seq_len
    th = 2.0 * np.pi * np.outer(np.arange(n), np.arange(n)) / n
    full = lambda shape: pl.BlockSpec(shape, lambda: (0, 0))
    return pl.pallas_call(
        _small_conv_kernel,
        out_shape=jax.ShapeDtypeStruct((seq_len, ch), F32),
        in_specs=[full((seq_len, ch)), full((n, ch)), full((n, n)), full((n, n))],
        out_specs=full((seq_len, ch)),
        compiler_params=pltpu.CompilerParams(vmem_limit_bytes=VMEM_LIMIT),
        name="hyena_small_conv",
    )(z, taps, jnp.asarray(np.cos(th), F32), jnp.asarray(np.sin(th), F32))


def _hy_post_kernel(x0_ref, z_ref, conv_ref, asum_ref, bias_ref, y_ref, *, n_total):
    norm = jnp.sum(asum_ref[...], axis=0, keepdims=True) + EPS
    conv = conv_ref[...] * (1.0 / (n_total * norm))
    y_ref[...] = (x0_ref[...] * (conv + bias_ref[...] * z_ref[...])).astype(y_ref.dtype)


def _hyena_post(x0, z, conv, asum, bias):
    t = x0.shape[0]
    tm = min(t, 1024)
    row = pl.BlockSpec((tm, HY_WIDTH), lambda i: (i, 0))
    return pl.pallas_call(
        functools.partial(_hy_post_kernel, n_total=2 * t),
        out_shape=jax.ShapeDtypeStruct((t, HY_WIDTH), BF16),
        grid=(t // tm,),
        in_specs=[row, row, row, pl.BlockSpec((8, HY_WIDTH), lambda i: (0, 0)),
                  pl.BlockSpec((1, HY_WIDTH), lambda i: (0, 0))],
        out_specs=row,
        compiler_params=_cparams("parallel"),
        name="hyena_post",
    )(x0, z, conv, asum, bias.reshape(1, HY_WIDTH))


def _hyena_mixer(p_hy, conv_w, conv_b, filt, bias):
    seq_len = p_hy.shape[0]
    x0, z = _hyena_pre(p_hy, conv_w, conv_b)
    taps, asum = _hyena_taps(seq_len, *filt)
    if 2 * seq_len // FFT_N2 >= 16:
        conv = _hyena_long_conv(z, taps)
    else:
        conv = _hyena_small_conv(z, taps)
    return _hyena_post(x0, z, conv, asum, bias)


def _outproj_kernel(hy_ref, ret_ref, att_ref, w_ref, x_ref, gate_ref, o_ref):
    y = (_dot(hy_ref[...], w_ref[:HY_WIDTH, :])
         + _dot(ret_ref[...], w_ref[HY_WIDTH:HY_WIDTH + RET_WIDTH, :])
         + _dot(att_ref[...], w_ref[HY_WIDTH + RET_WIDTH:, :]))
    o_ref[...] = x_ref[...] + gate_ref[...] * y


def _out_projection(y_hy, y_ret, y_att, w_bf16, x, gate):
    t = x.shape[0]
    tm = min(t, 512)
    row = lambda w: pl.BlockSpec((tm, w), lambda i: (i, 0))
    return pl.pallas_call(
        _outproj_kernel,
        out_shape=jax.ShapeDtypeStruct((t, D_MODEL), F32),
        grid=(t // tm,),
        in_specs=[row(HY_WIDTH), row(RET_WIDTH), row(ATT_WIDTH),
                  pl.BlockSpec((MIX_WIDTH, D_MODEL), lambda i: (0, 0)),
                  row(D_MODEL), pl.BlockSpec((1, D_MODEL), lambda i: (0, 0))],
        out_specs=row(D_MODEL),
        compiler_params=_cparams("parallel"),
        name="out_projection",
    )(y_hy, y_ret, y_att, w_bf16, x, gate)


def _ffn_kernel(x_ref, g_ref, sh_ref, sc_ref, gate_ref, wg_ref, wu_ref, wd_ref, o_ref,
                h_ref, acc_ref):
    k = pl.program_id(1)

    @pl.when(k == 0)
    def _():
        h_ref[...] = _norm_mod(x_ref[...], g_ref[...], sh_ref[...], sc_ref[...]).astype(BF16)
        acc_ref[...] = jnp.zeros_like(acc_ref)

    h = h_ref[...]
    a = _silu(_dot(h, wg_ref[...])) * _dot(h, wu_ref[...])
    acc_ref[...] += _dot(a.astype(BF16), wd_ref[...])

    @pl.when(k == pl.num_programs(1) - 1)
    def _():
        o_ref[...] = x_ref[...] + gate_ref[...] * acc_ref[...]


def _dense_ffn(x, g, shift, scale, gate, wg, wu, wd):
    t = x.shape[0]
    tm = min(t, 1024)
    th = 256
    vec = pl.BlockSpec((1, D_MODEL), lambda i, k: (0, 0))
    return pl.pallas_call(
        _ffn_kernel,
        out_shape=jax.ShapeDtypeStruct((t, D_MODEL), F32),
        grid=(t // tm, FFN_HIDDEN // th),
        in_specs=[pl.BlockSpec((tm, D_MODEL), lambda i, k: (i, 0)), vec, vec, vec, vec,
                  pl.BlockSpec((D_MODEL, th), lambda i, k: (0, k)),
                  pl.BlockSpec((D_MODEL, th), lambda i, k: (0, k)),
                  pl.BlockSpec((th, D_MODEL), lambda i, k: (k, 0))],
        out_specs=pl.BlockSpec((tm, D_MODEL), lambda i, k: (i, 0)),
        scratch_shapes=[pltpu.VMEM((tm, D_MODEL), BF16), pltpu.VMEM((tm, D_MODEL), F32)],
        compiler_params=_cparams("parallel", "arbitrary"),
        name="dense_ffn",
    )(x, g, shift, scale, gate, wg, wu, wd)


def _moe_route_kernel(x_ref, g_ref, sh_ref, sc_ref, rw_ref, rb_ref, h_ref, info_ref):
    h = _norm_mod(x_ref[...], g_ref[...], sh_ref[...], sc_ref[...])
    h_ref[...] = h
    logits = _dot3(h, rw_ref[...]) + rb_ref[...]
    lane = lax.broadcasted_iota(jnp.int32, logits.shape, 1)
    v1 = jnp.max(logits, axis=-1, keepdims=True)
    i1 = jnp.min(jnp.where(logits == v1, lane, LANES), axis=-1, keepdims=True)
    rest = jnp.where(lane == i1, -jnp.inf, logits)
    v2 = jnp.max(rest, axis=-1, keepdims=True)
    i2 = jnp.min(jnp.where(rest == v2, lane, LANES), axis=-1, keepdims=True)
    e = jnp.exp(v2 - v1)
    g1 = 1.0 / (1.0 + e)
    g2 = e * g1
    info_ref[...] = jnp.where(lane == 0, i1.astype(F32),
                              jnp.where(lane == 1, i2.astype(F32),
                                        jnp.where(lane == 2, g1, jnp.where(lane == 3, g2, 0.0))))


def _moe_route(x, g, shift, scale, router_w, router_b):
    t = x.shape[0]
    tm = min(t, 512)
    rw = jnp.concatenate([router_w, jnp.zeros((D_MODEL, LANES - N_EXPERTS), F32)], axis=1)
    rb = jnp.concatenate([router_b, jnp.full((LANES - N_EXPERTS,), -jnp.inf, F32)]).reshape(1, LANES)
    vec = pl.BlockSpec((1, D_MODEL), lambda i: (0, 0))
    return pl.pallas_call(
        _moe_route_kernel,
        out_shape=(jax.ShapeDtypeStruct((t, D_MODEL), F32), jax.ShapeDtypeStruct((t, LANES), F32)),
        grid=(t // tm,),
        in_specs=[pl.BlockSpec((tm, D_MODEL), lambda i: (i, 0)), vec, vec, vec,
                  pl.BlockSpec((D_MODEL, LANES), lambda i: (0, 0)),
                  pl.BlockSpec((1, LANES), lambda i: (0, 0))],
        out_specs=(pl.BlockSpec((tm, D_MODEL), lambda i: (i, 0)),
                   pl.BlockSpec((tm, LANES), lambda i: (i, 0))),
        compiler_params=_cparams("parallel"),
        name="moe_route",
    )(x, g, shift, scale, rw, rb)


def _row_copy(src_hbm, dst_vmem, sem, src_row, dst_row):
    return pltpu.make_async_copy(src_hbm.at[pl.ds(src_row, 1)], dst_vmem.at[pl.ds(dst_row, 1)], sem)


def _moe_expert_kernel(blk_e_ref, nused_ref, src_ref, h_hbm, wg_ref, wu_ref, wd_ref, y_ref,
                       xbuf_ref, xb_ref, acc_ref, sem):
    b = pl.program_id(0)
    k = pl.program_id(1)
    last = pl.num_programs(1) - 1
    n_used = nused_ref[0]
    used = b < n_used

    def issue_rows(blk, slot):
        def body(r, carry):
            _row_copy(h_hbm, xbuf_ref.at[slot], sem.at[slot], src_ref[blk * MOE_BLOCK + r], r).start()
            return carry

        lax.fori_loop(0, MOE_BLOCK, body, 0, unroll=DMA_UNROLL)

    def wait_rows(slot):
        def body(r, carry):
            _row_copy(h_hbm, xbuf_ref.at[slot], sem.at[slot], 0, r).wait()
            return carry

        lax.fori_loop(0, MOE_BLOCK, body, 0, unroll=DMA_UNROLL)

    @pl.when(jnp.logical_and(used, k == 0))
    def _():
        slot = b % 2

        @pl.when(b == 0)
        def _():
            issue_rows(0, 0)

        wait_rows(slot)

        @pl.when(b + 1 < n_used)
        def _():
            issue_rows(b + 1, 1 - slot)

        xb_ref[...] = xbuf_ref[slot].astype(BF16)
        acc_ref[...] = jnp.zeros_like(acc_ref)

    @pl.when(used)
    def _():
        x = xb_ref[...]
        a = _silu(_dot(x, wg_ref[0])) * _dot(x, wu_ref[0])
        acc_ref[...] += _dot(a.astype(BF16), wd_ref[0])

    @pl.when(jnp.logical_and(used, k == last))
    def _():
        y_ref[...] = acc_ref[...]

    @pl.when(jnp.logical_and(jnp.logical_not(used), k == last))
    def _():
        y_ref[...] = jnp.zeros_like(y_ref)


def _moe_experts(h, blk_expert, n_used, src_tok, wg, wu, wd, n_blk):
    th = MOE_TH
    nk = FFN_HIDDEN // th

    def w_index(b, k, be, nu, st):
        return be[jnp.minimum(b, nu[0] - 1)], jnp.where(b < nu[0], k, nk - 1)

    def col_index(b, k, be, nu, st):
        e, kk = w_index(b, k, be, nu, st)
        return e, 0, kk

    def row_index(b, k, be, nu, st):
        e, kk = w_index(b, k, be, nu, st)
        return e, kk, 0

    grid_spec = pltpu.PrefetchScalarGridSpec(
        num_scalar_prefetch=3,
        grid=(n_blk, nk),
        in_specs=[pl.BlockSpec(memory_space=pl.ANY),
                  pl.BlockSpec((1, D_MODEL, th), col_index),
                  pl.BlockSpec((1, D_MODEL, th), col_index),
                  pl.BlockSpec((1, th, D_MODEL), row_index)],
        out_specs=pl.BlockSpec((MOE_BLOCK, D_MODEL), lambda b, k, be, nu, st: (b, 0)),
        scratch_shapes=[pltpu.VMEM((2, MOE_BLOCK, D_MODEL), F32),
                        pltpu.VMEM((MOE_BLOCK, D_MODEL), BF16),
                        pltpu.VMEM((MOE_BLOCK, D_MODEL), F32),
                        pltpu.SemaphoreType.DMA((2,))],
    )
    return pl.pallas_call(
        _moe_expert_kernel,
        out_shape=jax.ShapeDtypeStruct((n_blk * MOE_BLOCK, D_MODEL), F32),
        grid_spec=grid_spec,
        compiler_params=_cparams("arbitrary", "arbitrary"),
        name="moe_experts",
    )(blk_expert, n_used, src_tok, h, wg, wu, wd)


def _moe_combine_kernel(dest_ref, x_ref, info_ref, gate_ref, ng_ref, y_hbm, o_ref,
                        y1_ref, y2_ref, sem, *, final_norm):
    i = pl.program_id(0)
    tm = x_ref.shape[0]
    slot = i % 2

    def issue_rows(blk, slot):
        def body(r, carry):
            a = 2 * (blk * tm + r)
            _row_copy(y_hbm, y1_ref.at[slot], sem.at[slot], dest_ref[a], r).start()
            _row_copy(y_hbm, y2_ref.at[slot], sem.at[slot], dest_ref[a + 1], r).start()
            return carry

        lax.fori_loop(0, tm, body, 0, unroll=DMA_UNROLL)

    def wait_rows(slot):
        def body(r, carry):
            _row_copy(y_hbm, y1_ref.at[slot], sem.at[slot], 0, r).wait()
            _row_copy(y_hbm, y2_ref.at[slot], sem.at[slot], 0, r).wait()
            return carry

        lax.fori_loop(0, tm, body, 0, unroll=DMA_UNROLL)

    @pl.when(i == 0)
    def _():
        issue_rows(0, 0)

    wait_rows(slot)

    @pl.when(i + 1 < pl.num_programs(0))
    def _():
        issue_rows(i + 1, 1 - slot)

    info = info_ref[...]
    y = y1_ref[slot] * info[:, 2:3] + y2_ref[slot] * info[:, 3:4]
    x = x_ref[...] + gate_ref[...] * y
    if final_norm:
        ms = jnp.mean(x * x, axis=-1, keepdims=True)
        x = x * lax.rsqrt(ms + EPS) * ng_ref[...]
    o_ref[...] = x


def _moe_combine(dest, x, info, gate, norm_g, y, final_norm):
    t = x.shape[0]
    tm = min(t, 256)
    vec = pl.BlockSpec((1, D_MODEL), lambda i, d: (0, 0))
    grid_spec = pltpu.PrefetchScalarGridSpec(
        num_scalar_prefetch=1,
        grid=(t // tm,),
        in_specs=[pl.BlockSpec((tm, D_MODEL), lambda i, d: (i, 0)),
                  pl.BlockSpec((tm, LANES), lambda i, d: (i, 0)), vec, vec,
                  pl.BlockSpec(memory_space=pl.ANY)],
        out_specs=pl.BlockSpec((tm, D_MODEL), lambda i, d: (i, 0)),
        scratch_shapes=[pltpu.VMEM((2, tm, D_MODEL), F32), pltpu.VMEM((2, tm, D_MODEL), F32),
                        pltpu.SemaphoreType.DMA((2,))],
    )
    return pl.pallas_call(
        functools.partial(_moe_combine_kernel, final_norm=final_norm),
        out_shape=jax.ShapeDtypeStruct((t, D_MODEL), F32),
        grid_spec=grid_spec,
        compiler_params=_cparams("arbitrary"),
        name="moe_combine",
    )(dest, x, info, gate, norm_g, y)


def _moe_layer(x, g, shift, scale, gate, router_w, router_b, wg, wu, wd, norm_g, final_norm):
    t = x.shape[0]
    n_asg = t * TOP_K
    h, info = _moe_route(x, g, shift, scale, router_w, router_b)
    expert = info[:, :TOP_K].astype(jnp.int32).reshape(-1)
    onehot = (expert[:, None] == jnp.arange(N_EXPERTS, dtype=jnp.int32)[None, :]).astype(jnp.int32)
    csum = jnp.cumsum(onehot, axis=0)
    counts = csum[-1]
    padded = (counts + MOE_BLOCK - 1) // MOE_BLOCK * MOE_BLOCK
    pad_end = jnp.cumsum(padded)
    pad_start = pad_end - padded
    dest = jnp.sum(onehot * (csum - 1 + pad_start[None, :]), axis=1).astype(jnp.int32)
    n_blk = -(-n_asg // MOE_BLOCK) + N_EXPERTS
    blk_start = jnp.arange(n_blk, dtype=jnp.int32) * MOE_BLOCK
    blk_expert = jnp.minimum(jnp.sum(blk_start[:, None] >= pad_end[None, :], axis=1),
                             N_EXPERTS - 1).astype(jnp.int32)
    n_used = (pad_end[-1:] // MOE_BLOCK).astype(jnp.int32)
    src_tok = jnp.zeros((n_blk * MOE_BLOCK,), jnp.int32).at[dest].set(
        jnp.arange(n_asg, dtype=jnp.int32) // TOP_K)
    y = _moe_experts(h, blk_expert, n_used, src_tok, wg, wu, wd, n_blk)
    return _moe_combine(dest, x, info, gate, norm_g, y, final_norm)


def _final_norm_kernel(x_ref, g_ref, o_ref):
    x = x_ref[...]
    ms = jnp.mean(x * x, axis=-1, keepdims=True)
    o_ref[...] = x * lax.rsqrt(ms + EPS) * g_ref[...]


def _final_norm(x, g):
    t = x.shape[0]
    tm = min(t, 1024)
    return pl.pallas_call(
        _final_norm_kernel,
        out_shape=jax.ShapeDtypeStruct((t, D_MODEL), F32),
        grid=(t // tm,),
        in_specs=[pl.BlockSpec((tm, D_MODEL), lambda i: (i, 0)),
                  pl.BlockSpec((1, D_MODEL), lambda i: (0, 0))],
        out_specs=pl.BlockSpec((tm, D_MODEL), lambda i: (i, 0)),
        compiler_params=_cparams("parallel"),
        name="final_norm",
    )(x, g)


def kernel(x, c, ctx, c_ctx, ada_w, ada_b, norm1_g, norm2_g, w_in, w_out, hy_conv_w, hy_conv_b, hy_filt_w1, hy_filt_b1, hy_filt_w2, hy_filt_b2, hy_filt_w3, hy_filt_b3, hy_filt_freq, hy_bias, ret_log_rate, attn_q_g, attn_k_g, ffn_w_gate, ffn_w_up, ffn_w_down, moe_router_w, moe_router_b, moe_w_gate, moe_w_up, moe_w_down, final_norm_g):
    assert x.shape[0] == 1 and c.shape[0] == 1
    seq_len = x.shape[1]
    x_lat = x[0]
    x_ctx = ctx[0]
    cos_t, sin_t = _rope_tables(seq_len)
    cvec = jnp.concatenate([c, c_ctx[None, :], jnp.zeros((6, D_MODEL), F32)], axis=0)
    cvec = _silu(cvec)
    row = lambda v: v.reshape(1, D_MODEL)
    zero_state = jnp.zeros((2, RET_WIDTH, RET_HEAD_DIM), F32)
    zero_tab = jnp.zeros((x_ctx.shape[0], LANES), F32)

    for l in range(DEPTH):
        last = l == DEPTH - 1
        mods = _ada_modulation(cvec, ada_w[l], ada_b[l]).reshape(8, 6, D_MODEL)
        mod = [row(mods[0, i]) for i in range(6)]
        mod_c = [row(mods[1, i]) for i in range(6)]
        filt = (hy_filt_w1[l], hy_filt_b1[l], hy_filt_w2[l], hy_filt_b2[l],
                hy_filt_w3[l], hy_filt_b3[l], hy_filt_freq[l])
        lg_f = -jnp.exp(ret_log_rate[l, 0].astype(F32))
        lg_b = -jnp.exp(ret_log_rate[l, 1].astype(F32))
        mask, zeta, xi, chunk_decay = _ret_tables(lg_f, lg_b)
        w_in_l = w_in[l].astype(BF16)
        w_out_l = w_out[l].astype(BF16)
        n1 = row(norm1_g[l])
        n2 = row(norm2_g[l])

        p_hy, p_ret, p_att = _in_projection(x_lat, n1, mod[0], mod[1], w_in_l)
        pc_hy, pc_ret, pc_att = _in_projection(x_ctx, n1, mod_c[0], mod_c[1], w_in_l)

        sprev_c, s_ctx = _retention_states(pc_ret, zeta, chunk_decay, zero_state)
        qct, kc, vct, _ = _attention_prep(pc_att, zero_tab, zero_tab, attn_q_g[l], attn_k_g[l],
                                          rope=False)

        y_hy = _hyena_mixer(p_hy, hy_conv_w[l], hy_conv_b[l], filt, hy_bias[l])
        sprev, _ = _retention_states(p_ret, zeta, chunk_decay, s_ctx)
        y_ret = _retention_outputs(p_ret, mask, xi, sprev)
        qt, k_aug, vt, kmax_sq = _attention_prep(p_att, cos_t, sin_t, attn_q_g[l], attn_k_g[l],
                                                 rope=True)
        kmax = jnp.sqrt(jnp.max(kmax_sq[:, 0, ::ATT_HEAD_DIM], axis=0)) * KMAX_SLACK
        y_att = _attention(qt, kc, vct, kmax, k_aug, vt, tq=ATT_TQ, tk=ATT_TK, sub=ATT_SUB)

        if not last:
            yc_hy = _hyena_mixer(pc_hy, hy_conv_w[l], hy_conv_b[l], filt, hy_bias[l])
            yc_ret = _retention_outputs(pc_ret, mask, xi, sprev_c)
            yc_att = _attention(qct, kc, vct, tq=x_ctx.shape[0])
            x_ctx = _out_projection(yc_hy, yc_ret, yc_att, w_out_l, x_ctx, mod_c[2])
        x_lat = _out_projection(y_hy, y_ret, y_att, w_out_l, x_lat, mod[2])

        i = l // 2
        if l % 2 == 0:
            wg, wu, wd = (ffn_w_gate[i].astype(BF16), ffn_w_up[i].astype(BF16),
                          ffn_w_down[i].astype(BF16))
            x_lat = _dense_ffn(x_lat, n2, mod[3], mod[4], mod[5], wg, wu, wd)
            if not last:
                x_ctx = _dense_ffn(x_ctx, n2, mod_c[3], mod_c[4], mod_c[5], wg, wu, wd)
        else:
            wg, wu, wd = (moe_w_gate[i].astype(BF16), moe_w_up[i].astype(BF16),
                          moe_w_down[i].astype(BF16))
            fg = row(final_norm_g)
            x_lat = _moe_layer(x_lat, n2, mod[3], mod[4], mod[5], moe_router_w[i], moe_router_b[i],
                               wg, wu, wd, fg, final_norm=last)
            if not last:
                x_ctx = _moe_layer(x_ctx, n2, mod_c[3], mod_c[4], mod_c[5], moe_router_w[i],
                                   moe_router_b[i], wg, wu, wd, fg, final_norm=False)
    if DEPTH % 2 == 1:
        x_lat = _final_norm(x_lat, row(final_norm_g))
    return x_lat[None]
```

```python
import functools
import math

import numpy as np
import jax
import jax.numpy as jnp
from jax import lax
from jax.experimental import pallas as pl
from jax.experimental.pallas import tpu as pltpu

F32 = jnp.float32
BF16 = jnp.bfloat16

D_MODEL = 1024
DEPTH = 2
GRID_W = 64
EPS = 1e-6

HY_WIDTH = 256
HY_COLS = 3 * HY_WIDTH
HY_POS_BANDS = 16
HY_DECAY_TARGET = 1e-2
HY_SHORT_DECAY_PCT = 0.3
HY_LONG_DECAY_PCT = 1.5

RET_HEAD_DIM = 64
RET_HEADS = 4
RET_WIDTH = RET_HEADS * RET_HEAD_DIM
RET_COLS = 4 * RET_WIDTH
RET_CHUNK = 128

ATT_HEAD_DIM = 64
ATT_HEADS = 8
ATT_KV_HEADS = 2
ATT_GROUP = ATT_HEADS // ATT_KV_HEADS
ATT_WIDTH = ATT_HEADS * ATT_HEAD_DIM
ATT_KV_WIDTH = ATT_KV_HEADS * ATT_HEAD_DIM
ATT_COLS = ATT_WIDTH + 2 * ATT_KV_WIDTH
ROPE_AXIS_DIM = ATT_HEAD_DIM // 2
ROPE_BASE = 10000.0

MIX_WIDTH = HY_WIDTH + RET_WIDTH + ATT_WIDTH
IN_COLS = HY_COLS + RET_COLS + ATT_COLS

FFN_HIDDEN = 2816
N_EXPERTS = 8
TOP_K = 2
MOE_BLOCK = 512

LOG2_E = 1.4426950408889634
EXP2_CAP = 60.0
KMAX_SLACK = 1.0 + 2.0 ** -7
ATT_TQ = 256
ATT_TK = 4096
ATT_SUB = 512
RET_STATE_CHUNKS = 8
RET_OUT_CHUNKS = 4
SWIGLU_CHUNK = 1024
DMA_UNROLL = 8

LANES = 128
VMEM_LIMIT = 56 * 1024 * 1024


def _cparams(*sem):
    return pltpu.CompilerParams(dimension_semantics=sem, vmem_limit_bytes=VMEM_LIMIT)


def _dot(a, b):
    return jnp.dot(a, b, preferred_element_type=F32)


def _split(a):
    hi = a.astype(BF16)
    lo = (a - hi.astype(F32)).astype(BF16)
    return hi, lo


def _dot3(a, b):
    ah, al = _split(a)
    bh, bl = _split(b)
    return _dot(ah, bh) + _dot(al, bh) + _dot(ah, bl)


def _silu(x):
    return x * (1.0 / (1.0 + jnp.exp(-x)))


def _norm_mod(x, g, shift, scale):
    ms = jnp.mean(x * x, axis=-1, keepdims=True)
    h = x * lax.rsqrt(ms + EPS) * g
    return h * (1.0 + scale) + shift


def _mm3_kernel(a_ref, b_ref, o_ref):
    o_ref[...] = _dot3(a_ref[...], b_ref[...])


def _mm3(a, b, tn, name):
    m, k = a.shape
    n = b.shape[1]
    return pl.pallas_call(
        _mm3_kernel,
        out_shape=jax.ShapeDtypeStruct((m, n), F32),
        grid=(n // tn,),
        in_specs=[pl.BlockSpec((m, k), lambda j: (0, 0)),
                  pl.BlockSpec((k, tn), lambda j: (0, j))],
        out_specs=pl.BlockSpec((m, tn), lambda j: (0, j)),
        compiler_params=_cparams("parallel"),
        name=name,
    )(a, b)


def _ada_kernel(c_ref, w_ref, b_ref, o_ref):
    o_ref[...] = _dot3(c_ref[...], w_ref[...]) + b_ref[...]


def _ada_modulation(cc, w, b):
    n = w.shape[1]
    tn = 1536
    return pl.pallas_call(
        _ada_kernel,
        out_shape=jax.ShapeDtypeStruct((8, n), F32),
        grid=(n // tn,),
        in_specs=[pl.BlockSpec((8, D_MODEL), lambda j: (0, 0)),
                  pl.BlockSpec((D_MODEL, tn), lambda j: (0, j)),
                  pl.BlockSpec((1, tn), lambda j: (0, j))],
        out_specs=pl.BlockSpec((8, tn), lambda j: (0, j)),
        compiler_params=_cparams("parallel"),
        name="ada_modulation",
    )(cc, w, b.reshape(1, n))


def _inproj_kernel(x_ref, g_ref, sh_ref, sc_ref, w_ref, hy_ref, ret_ref, att_ref):
    h = _norm_mod(x_ref[...], g_ref[...], sh_ref[...], sc_ref[...])
    p = _dot(h.astype(BF16), w_ref[...])
    hy_ref[...] = p[:, :HY_COLS]
    ret_ref[...] = p[:, HY_COLS:HY_COLS + RET_COLS]
    att_ref[...] = p[:, HY_COLS + RET_COLS:]


def _in_projection(x, g, shift, scale, w_bf16):
    t = x.shape[0]
    tm = min(t, 512)
    vec = pl.BlockSpec((1, D_MODEL), lambda i: (0, 0))
    return pl.pallas_call(
        _inproj_kernel,
        out_shape=(jax.ShapeDtypeStruct((t, HY_COLS), F32),
                   jax.ShapeDtypeStruct((t, RET_COLS), F32),
                   jax.ShapeDtypeStruct((t, ATT_COLS), F32)),
        grid=(t // tm,),
        in_specs=[pl.BlockSpec((tm, D_MODEL), lambda i: (i, 0)), vec, vec, vec,
                  pl.BlockSpec((D_MODEL, IN_COLS), lambda i: (0, 0))],
        out_specs=(pl.BlockSpec((tm, HY_COLS), lambda i: (i, 0)),
                   pl.BlockSpec((tm, RET_COLS), lambda i: (i, 0)),
                   pl.BlockSpec((tm, ATT_COLS), lambda i: (i, 0))),
        compiler_params=_cparams("parallel"),
        name="in_projection",
    )(x, g, shift, scale, w_bf16)


def _head_mean_matrix(width):
    idx = np.arange(width) // ATT_HEAD_DIM
    return jnp.asarray((idx[:, None] == idx[None, :]).astype(np.float32) / ATT_HEAD_DIM, BF16)


def _head_rms(x, bd, g):
    x2h, x2l = _split(x * x)
    ms = _dot(x2h, bd) + _dot(x2l, bd)
    return x * lax.rsqrt(ms + EPS) * g


def _rope(x, cos, sin):
    n = x.shape[1]
    lane = lax.broadcasted_iota(jnp.int32, x.shape, 1)
    swapped = jnp.where((lane % 32) < 16, pltpu.roll(x, n - 16, 1), pltpu.roll(x, 16, 1))
    return x * cos + swapped * sin


def _attprep_kernel(p_ref, cos_ref, sin_ref, qg_ref, kg_ref, bdq_ref, bdk_ref,
                    qt_ref, k_ref, vt_ref, kmax_ref, *, rope):
    p = p_ref[...]
    q = _head_rms(p[:, :ATT_WIDTH], bdq_ref[...], qg_ref[...])
    k = _head_rms(p[:, ATT_WIDTH:ATT_WIDTH + ATT_KV_WIDTH], bdk_ref[...], kg_ref[...])
    if rope:
        cos = cos_ref[...]
        sin = sin_ref[...]
        q = jnp.concatenate(
            [_rope(q[:, j * LANES:(j + 1) * LANES], cos, sin) for j in range(ATT_WIDTH // LANES)],
            axis=1)
        k = _rope(k, cos, sin)
    qt_ref[...] = (q * (ATT_HEAD_DIM ** -0.5 * LOG2_E)).T.astype(BF16)
    tm = k.shape[0]
    lane = lax.broadcasted_iota(jnp.int32, k.shape, 1)
    minus_one_col = jnp.where(lane == ATT_HEAD_DIM, -1.0, 0.0)
    kb = k.astype(BF16)
    k_ref[0] = jnp.where(lane < ATT_HEAD_DIM, k, minus_one_col).astype(BF16)
    k_ref[1] = jnp.where(lane < ATT_HEAD_DIM, pltpu.roll(k, ATT_HEAD_DIM, 1), minus_one_col).astype(BF16)
    sq_hi, sq_lo = _split(kb.astype(F32) ** 2)
    norm_sq = (_dot(sq_hi, bdk_ref[...]) + _dot(sq_lo, bdk_ref[...])) * ATT_HEAD_DIM
    kmax_ref[0] = jnp.broadcast_to(jnp.max(norm_sq, axis=0, keepdims=True), (8, LANES))
    vt = p[:, ATT_WIDTH + ATT_KV_WIDTH:].T.astype(BF16)
    sub = lax.broadcasted_iota(jnp.int32, (ATT_HEAD_DIM, tm), 0)
    ones_row = jnp.where(sub == 0, 1.0, 0.0).astype(BF16)
    for h in range(ATT_KV_HEADS):
        vt_ref[h * LANES:h * LANES + ATT_HEAD_DIM, :] = vt[h * ATT_HEAD_DIM:(h + 1) * ATT_HEAD_DIM]
        vt_ref[h * LANES + ATT_HEAD_DIM:(h + 1) * LANES, :] = ones_row


def _attention_prep(p_att, cos, sin, q_g, k_g, rope):
    t = p_att.shape[0]
    tm = min(t, 512)
    qg = jnp.tile(q_g, ATT_HEADS).reshape(1, ATT_WIDTH)
    kg = jnp.tile(k_g, ATT_KV_HEADS).reshape(1, ATT_KV_WIDTH)
    const = lambda shape: pl.BlockSpec(shape, lambda i: (0, 0))
    return pl.pallas_call(
        functools.partial(_attprep_kernel, rope=rope),
        out_shape=(jax.ShapeDtypeStruct((ATT_WIDTH, t), BF16),
                   jax.ShapeDtypeStruct((ATT_KV_HEADS, t, LANES), BF16),
                   jax.ShapeDtypeStruct((ATT_KV_HEADS * LANES, t), BF16),
                   jax.ShapeDtypeStruct((t // tm, 8, LANES), F32)),
        grid=(t // tm,),
        in_specs=[pl.BlockSpec((tm, ATT_COLS), lambda i: (i, 0)),
                  pl.BlockSpec((tm, LANES), lambda i: (i, 0)),
                  pl.BlockSpec((tm, LANES), lambda i: (i, 0)),
                  const((1, ATT_WIDTH)), const((1, ATT_KV_WIDTH)),
                  const((ATT_WIDTH, ATT_WIDTH)), const((ATT_KV_WIDTH, ATT_KV_WIDTH))],
        out_specs=(pl.BlockSpec((ATT_WIDTH, tm), lambda i: (0, i)),
                   pl.BlockSpec((ATT_KV_HEADS, tm, LANES), lambda i: (0, i, 0)),
                   pl.BlockSpec((ATT_KV_HEADS * LANES, tm), lambda i: (0, i)),
                   pl.BlockSpec((1, 8, LANES), lambda i: (i, 0, 0))),
        compiler_params=_cparams("parallel"),
        name="attention_prep",
    )(p_att, cos, sin, qg, kg, _head_mean_matrix(ATT_WIDTH), _head_mean_matrix(ATT_KV_WIDTH))


def _rope_tables(n_tokens):
    rows = n_tokens // GRID_W
    inv_freq = ROPE_BASE ** (-jnp.arange(0, ROPE_AXIS_DIM, 2, dtype=F32) / ROPE_AXIS_DIM)
    ang_r = jnp.arange(rows, dtype=F32)[:, None] * inv_freq[None, :]
    ang_c = jnp.arange(GRID_W, dtype=F32)[:, None] * inv_freq[None, :]
    nf = inv_freq.shape[0]
    by_row = lambda tab: jnp.broadcast_to(tab[:, None, :], (rows, GRID_W, nf)).reshape(n_tokens, nf)
    by_col = lambda tab: jnp.broadcast_to(tab[None, :, :], (rows, GRID_W, nf)).reshape(n_tokens, nf)
    cr, sr = by_row(jnp.cos(ang_r)), by_row(jnp.sin(ang_r))
    cc, sc = by_col(jnp.cos(ang_c)), by_col(jnp.sin(ang_c))
    cos = jnp.concatenate([cr, cr, cc, cc], axis=1)
    sin = jnp.concatenate([-sr, sr, -sc, sc], axis=1)
    return jnp.tile(cos, (1, 2)), jnp.tile(sin, (1, 2))


def _attention_kernel(*refs, tq, tk, sub, n_lat):
    if n_lat:
        kmax_ref, qt_ref, kc_ref, vct_ref, k_ref, vt_ref, o_ref, qs_ref, acc_ref, m_ref = refs
    else:
        qt_ref, kc_ref, vct_ref, o_ref, acc_ref = refs
    cols = ATT_GROUP * tq
    qt = qt_ref[...]
    qst = jnp.concatenate(
        [qt[g * ATT_HEAD_DIM:(g + 1) * ATT_HEAD_DIM, :] for g in range(ATT_GROUP)], axis=1)
    pad = jnp.zeros((LANES - ATT_HEAD_DIM, cols), BF16)
    s = _dot(kc_ref[0], jnp.concatenate([qst, pad], axis=0))
    mt = jnp.max(s, axis=0, keepdims=True).astype(BF16)
    mtf = mt.astype(F32)
    acc_ref[...] = _dot(vct_ref[...], jnp.exp2(s - mtf).astype(BF16))

    if n_lat:
        row = lax.broadcasted_iota(jnp.int32, pad.shape, 0)
        qs_ref[...] = jnp.concatenate([qst, jnp.where(row == 0, mtf, 0.0).astype(BF16)], axis=0)
        qf = qst.astype(F32)
        qn = jnp.sqrt(jnp.sum(qf * qf, axis=0, keepdims=True))
        fast = jnp.max(qn * kmax_ref[pl.program_id(0)] - mtf) <= EXP2_CAP

        def tile(j):
            off = pl.multiple_of(j * tk, tk)
            return k_ref[0, pl.ds(off, tk), :], vt_ref[:, pl.ds(off, tk)]

        @pl.when(fast)
        def _():
            def body(j, carry):
                k, vt = tile(j)
                qs = qs_ref[...]
                part = None
                for c in range(tk // sub):
                    p = jnp.exp2(_dot(k[c * sub:(c + 1) * sub], qs)).astype(BF16)
                    pv = _dot(vt[:, c * sub:(c + 1) * sub], p)
                    part = pv if part is None else part + pv
                acc_ref[...] += part
                return carry

            lax.fori_loop(0, n_lat, body, 0)

        @pl.when(jnp.logical_not(fast))
        def _():
            m_ref[...] = jnp.zeros_like(m_ref)

            def body(j, carry):
                k, vt = tile(j)
                s = _dot(k, qs_ref[...])
                m_old = m_ref[...]
                m_new = jnp.maximum(m_old, jnp.max(s, axis=0, keepdims=True))
                p = jnp.exp2(s - m_new).astype(BF16)
                acc_ref[...] = jnp.exp2(m_old - m_new) * acc_ref[...] + _dot(vt, p)
                m_ref[...] = m_new
                return carry

            lax.fori_loop(0, n_lat, body, 0)

    acc = acc_ref[...]
    o = acc[:ATT_HEAD_DIM] / acc[ATT_HEAD_DIM:ATT_HEAD_DIM + 1]
    o_ref[...] = jnp.concatenate(
        [o[:, g * tq:(g + 1) * tq].T for g in range(ATT_GROUP)], axis=1).astype(o_ref.dtype)


def _attention(qt, kc, vct, kmax=None, k=None, vt=None, *, tq, tk=0, sub=0):
    t = qt.shape[1]
    lc = kc.shape[1]
    gw = ATT_GROUP * ATT_HEAD_DIM
    cols = ATT_GROUP * tq
    in_specs = [pl.BlockSpec((gw, tq), lambda h, i: (h, i)),
                pl.BlockSpec((1, lc, LANES), lambda h, i: (h, 0, 0)),
                pl.BlockSpec((LANES, lc), lambda h, i: (h, 0))]
    args = [qt, kc, vct]
    scratch = [pltpu.VMEM((LANES, cols), F32)]
    n_lat = 0
    if k is not None:
        lk = k.shape[1]
        n_lat = lk // tk
        in_specs = ([pl.BlockSpec(memory_space=pltpu.SMEM)] + in_specs
                    + [pl.BlockSpec((1, lk, LANES), lambda h, i: (h, 0, 0)),
                       pl.BlockSpec((LANES, lk), lambda h, i: (h, 0))])
        args = [kmax] + args + [k, vt]
        scratch = [pltpu.VMEM((LANES, cols), BF16)] + scratch + [pltpu.VMEM((1, cols), F32)]
    return pl.pallas_call(
        functools.partial(_attention_kernel, tq=tq, tk=tk, sub=sub, n_lat=n_lat),
        out_shape=jax.ShapeDtypeStruct((t, ATT_WIDTH), BF16),
        grid=(ATT_KV_HEADS, t // tq),
        in_specs=in_specs,
        out_specs=pl.BlockSpec((tq, gw), lambda h, i: (i, h)),
        scratch_shapes=scratch,
        compiler_params=_cparams("parallel", "parallel"),
        name="attention_lat" if n_lat else "attention_ctx",
    )(*args)


def _ret_tables(lg_f, lg_b):
    pos = jnp.arange(RET_CHUNK, dtype=F32)
    rel = pos[:, None] - pos[None, :]
    mask = jnp.where(rel > 0, jnp.exp(lg_f[:, None, None] * jnp.maximum(rel, 0.0)),
                     jnp.where(rel < 0, jnp.exp(lg_b[:, None, None] * jnp.maximum(-rel, 0.0)), 2.0))
    zeta = jnp.stack([jnp.exp(lg_f[:, None] * (RET_CHUNK - 1 - pos)[None, :]),
                      jnp.exp(lg_b[:, None] * pos[None, :])])
    xi = jnp.stack([jnp.exp(lg_f[:, None] * (pos + 1)[None, :]),
                    jnp.exp(lg_b[:, None] * (RET_CHUNK - pos)[None, :])])
    chunk_decay = jnp.stack([jnp.exp(lg_f * RET_CHUNK), jnp.exp(lg_b * RET_CHUNK)])
    zeta = jnp.repeat(zeta, RET_HEAD_DIM, axis=1).transpose(0, 2, 1)
    xi = jnp.repeat(xi, RET_HEAD_DIM, axis=1).transpose(0, 2, 1)
    chunk_decay = jnp.repeat(chunk_decay, RET_HEAD_DIM, axis=1)[:, :, None]
    return mask, zeta, xi, chunk_decay


def _ret_state_kernel(k_ref, v_ref, zeta_ref, cd_ref, s0_ref, sprev_ref, sfin_ref, s_ref, *, cps):
    d = pl.program_id(0)

    @pl.when(pl.program_id(1) == 0)
    def _():
        s_ref[...] = s0_ref[0]

    zeta = zeta_ref[0] * (RET_HEAD_DIM ** -0.5)
    cd = cd_ref[0]
    us = []
    for c in range(cps):
        rows = slice(c * RET_CHUNK, (c + 1) * RET_CHUNK)
        kz = (k_ref[rows, :] * zeta).astype(BF16)
        v = v_ref[rows, :].astype(BF16)
        heads = []
        for h in range(RET_HEADS):
            sl = slice(h * RET_HEAD_DIM, (h + 1) * RET_HEAD_DIM)
            heads.append(lax.dot_general(kz[:, sl], v[:, sl], (((0,), (0,)), ((), ())),
                                         preferred_element_type=F32))
        us.append(jnp.concatenate(heads, axis=0))

    def scan(order):
        s = s_ref[...]
        for c in order:
            sprev_ref[0, c] = s
            s = cd * s + us[c]
        s_ref[...] = s
        sfin_ref[0] = s

    @pl.when(d == 0)
    def _():
        scan(range(cps))

    @pl.when(d == 1)
    def _():
        scan(reversed(range(cps)))


def _retention_states(p_ret, zeta, chunk_decay, s0):
    nc = p_ret.shape[0] // RET_CHUNK
    cps = min(nc, RET_STATE_CHUNKS)
    ng = nc // cps
    rows = cps * RET_CHUNK

    def group(d, i):
        return i + d * (ng - 1 - 2 * i)

    return pl.pallas_call(
        functools.partial(_ret_state_kernel, cps=cps),
        out_shape=(jax.ShapeDtypeStruct((2, nc, RET_WIDTH, RET_HEAD_DIM), F32),
                   jax.ShapeDtypeStruct((2, RET_WIDTH, RET_HEAD_DIM), F32)),
        grid=(2, ng),
        in_specs=[pl.BlockSpec((rows, RET_WIDTH), lambda d, i: (group(d, i), 1)),
                  pl.BlockSpec((rows, RET_WIDTH), lambda d, i: (group(d, i), 2)),
                  pl.BlockSpec((1, RET_CHUNK, RET_WIDTH), lambda d, i: (d, 0, 0)),
                  pl.BlockSpec((1, RET_WIDTH, 1), lambda d, i: (d, 0, 0)),
                  pl.BlockSpec((1, RET_WIDTH, RET_HEAD_DIM), lambda d, i: (d, 0, 0))],
        out_specs=(pl.BlockSpec((1, cps, RET_WIDTH, RET_HEAD_DIM), lambda d, i: (d, group(d, i), 0, 0)),
                   pl.BlockSpec((1, RET_WIDTH, RET_HEAD_DIM), lambda d, i: (d, 0, 0))),
        scratch_shapes=[pltpu.VMEM((RET_WIDTH, RET_HEAD_DIM), F32)],
        compiler_params=_cparams("arbitrary", "arbitrary"),
        name="retention_states",
    )(p_ret, p_ret, zeta, chunk_decay, s0)


def _ret_out_kernel(q_ref, k_ref, v_ref, g_ref, mask_ref, xi_ref, sf_ref, sb_ref, y_ref, *, cpo):
    for c in range(cpo):
        rows = slice(c * RET_CHUNK, (c + 1) * RET_CHUNK)
        q = q_ref[rows, :]
        k = (k_ref[rows, :] * (RET_HEAD_DIM ** -0.5)).astype(BF16)
        v = v_ref[rows, :].astype(BF16)
        qb = q.astype(BF16)
        qf = (q * xi_ref[0]).astype(BF16)
        qr = (q * xi_ref[1]).astype(BF16)
        sf = sf_ref[0, c].astype(BF16)
        sb = sb_ref[0, c].astype(BF16)
        ys = []
        for h in range(RET_HEADS):
            sl = slice(h * RET_HEAD_DIM, (h + 1) * RET_HEAD_DIM)
            scores = lax.dot_general(qb[:, sl], k[:, sl], (((1,), (1,)), ((), ())),
                                     preferred_element_type=F32)
            y = _dot((scores * mask_ref[h]).astype(BF16), v[:, sl])
            y = y + _dot(qf[:, sl], sf[sl, :]) + _dot(qr[:, sl], sb[sl, :])
            mu = jnp.mean(y, axis=-1, keepdims=True)
            yc = y - mu
            var = jnp.mean(yc * yc, axis=-1, keepdims=True)
            ys.append(yc * lax.rsqrt(var + EPS))
        y_ref[rows, :] = (_silu(g_ref[rows, :]) * jnp.concatenate(ys, axis=1)).astype(y_ref.dtype)


def _retention_outputs(p_ret, mask, xi, sprev):
    nc = p_ret.shape[0] // RET_CHUNK
    cpo = min(nc, RET_OUT_CHUNKS)
    rows = cpo * RET_CHUNK
    col = lambda c: pl.BlockSpec((rows, RET_WIDTH), lambda i: (i, c))
    return pl.pallas_call(
        functools.partial(_ret_out_kernel, cpo=cpo),
        out_shape=jax.ShapeDtypeStruct((p_ret.shape[0], RET_WIDTH), BF16),
        grid=(nc // cpo,),
        in_specs=[col(0), col(1), col(2), col(3),
                  pl.BlockSpec((RET_HEADS, RET_CHUNK, RET_CHUNK), lambda i: (0, 0, 0)),
                  pl.BlockSpec((2, RET_CHUNK, RET_WIDTH), lambda i: (0, 0, 0)),
                  pl.BlockSpec((1, cpo, RET_WIDTH, RET_HEAD_DIM), lambda i: (0, i, 0, 0)),
                  pl.BlockSpec((1, cpo, RET_WIDTH, RET_HEAD_DIM), lambda i: (1, i, 0, 0))],
        out_specs=pl.BlockSpec((rows, RET_WIDTH), lambda i: (i, 0)),
        compiler_params=_cparams("parallel"),
        name="retention_outputs",
    )(p_ret, p_ret, p_ret, p_ret, mask, xi, sprev, sprev)


def _hy_pre_kernel(p_ref, prev_ref, next_ref, w_ref, b_ref, x0_ref, z_ref):
    i = pl.program_id(0)
    x = p_ref[...]
    tm = x.shape[0]
    row = lax.broadcasted_iota(jnp.int32, x.shape, 0)
    prev_row = jnp.where(i == 0, 0.0, prev_ref[7:8, :])
    next_row = jnp.where(i == pl.num_programs(0) - 1, 0.0, next_ref[0:1, :])
    up = jnp.where(row == 0, prev_row, pltpu.roll(x, 1, 0))
    dn = jnp.where(row == tm - 1, next_row, pltpu.roll(x, tm - 1, 0))
    w = w_ref[...]
    u = up * w[0:1] + x * w[1:2] + dn * w[2:3] + b_ref[...]
    x0_ref[...] = u[:, :HY_WIDTH]
    z_ref[...] = u[:, HY_WIDTH:2 * HY_WIDTH] * u[:, 2 * HY_WIDTH:]


def _hyena_pre(p_hy, conv_w, conv_b):
    t = p_hy.shape[0]
    tm = min(t, 512)
    nb8 = tm // 8
    last8 = t // 8 - 1
    return pl.pallas_call(
        _hy_pre_kernel,
        out_shape=(jax.ShapeDtypeStruct((t, HY_WIDTH), F32),
                   jax.ShapeDtypeStruct((t, HY_WIDTH), F32)),
        grid=(t // tm,),
        in_specs=[pl.BlockSpec((tm, HY_COLS), lambda i: (i, 0)),
                  pl.BlockSpec((8, HY_COLS), lambda i: (jnp.maximum(i * nb8 - 1, 0), 0)),
                  pl.BlockSpec((8, HY_COLS), lambda i: (jnp.minimum((i + 1) * nb8, last8), 0)),
                  pl.BlockSpec((3, HY_COLS), lambda i: (0, 0)),
                  pl.BlockSpec((1, HY_COLS), lambda i: (0, 0))],
        out_specs=(pl.BlockSpec((tm, HY_WIDTH), lambda i: (i, 0)),
                   pl.BlockSpec((tm, HY_WIDTH), lambda i: (i, 0))),
        compiler_params=_cparams("parallel"),
        name="hyena_pre",
    )(p_hy, p_hy, p_hy, conv_w, conv_b.reshape(1, HY_COLS))


def _hy_filter_kernel(feat_ref, w1_ref, b1_ref, w2_ref, b2_ref, w3_ref, b3_ref, freq_ref,
                      delta_ref, taps_ref, asum_ref, *, seq_len):
    i = pl.program_id(0)
    feat = feat_ref[...]
    freq = freq_ref[...]
    h = jnp.sin(freq * (_dot3(feat, w1_ref[...]) + b1_ref[...]))
    h = jnp.sin(freq * (_dot3(h, w2_ref[...]) + b2_ref[...]))
    h = _dot3(h, w3_ref[...]) + b3_ref[...]
    tm = feat.shape[0]
    j = i * tm + lax.broadcasted_iota(jnp.int32, (tm, HY_WIDTH), 0)
    t = feat[:, 0:1]
    hsel = jnp.where(j < seq_len, h[:, :HY_WIDTH], h[:, HY_WIDTH:])
    taps = jnp.where(j == seq_len, 0.0, hsel * jnp.exp(-t * delta_ref[...]))
    taps_ref[...] = taps

    @pl.when(i == 0)
    def _():
        asum_ref[...] = jnp.zeros_like(asum_ref)

    asum_ref[...] += jnp.sum(jnp.abs(taps).reshape(tm // 8, 8, HY_WIDTH), axis=0)


def _hyena_taps(seq_len, w1, b1, w2, b2, w3, b3, freq):
    n = 2 * seq_len
    j = np.arange(n)
    pos = np.where(j <= seq_len, j, n - j).astype(np.float32)
    pos = jnp.asarray(pos)
    tt = pos / max(seq_len - 1, 1)
    bands = jnp.linspace(1e-4, HY_POS_BANDS - 1, HY_POS_BANDS, dtype=F32)
    ang = (2.0 * math.pi / seq_len) * pos[:, None] * bands[None, :]
    feat = jnp.concatenate([tt[:, None], jnp.cos(ang), -jnp.sin(ang),
                            jnp.zeros((n, LANES - 1 - 2 * HY_POS_BANDS), F32)], axis=-1)
    w1p = jnp.concatenate([w1, jnp.zeros((LANES - w1.shape[0], w1.shape[1]), F32)], axis=0)
    deltas = jnp.abs(jnp.linspace(math.log(HY_DECAY_TARGET) / HY_LONG_DECAY_PCT,
                                  math.log(HY_DECAY_TARGET) / HY_SHORT_DECAY_PCT,
                                  HY_WIDTH, dtype=F32)).reshape(1, HY_WIDTH)
    tm = min(n, 1024)
    hid = w1.shape[1]
    const = lambda shape: pl.BlockSpec(shape, lambda i: (0, 0))
    return pl.pallas_call(
        functools.partial(_hy_filter_kernel, seq_len=seq_len),
        out_shape=(jax.ShapeDtypeStruct((n, HY_WIDTH), F32),
                   jax.ShapeDtypeStruct((8, HY_WIDTH), F32)),
        grid=(n // tm,),
        in_specs=[pl.BlockSpec((tm, LANES), lambda i: (i, 0)),
                  const((LANES, hid)), const((1, hid)), const((hid, hid)), const((1, hid)),
                  const((hid, 2 * HY_WIDTH)), const((1, 2 * HY_WIDTH)), const((1, hid)),
                  const((1, HY_WIDTH))],
        out_specs=(pl.BlockSpec((tm, HY_WIDTH), lambda i: (i, 0)), const((8, HY_WIDTH))),
        compiler_params=_cparams("arbitrary"),
        name="hyena_taps",
    )(feat, w1p, b1.reshape(1, hid), w2, b2.reshape(1, hid), w3, b3.reshape(1, 2 * HY_WIDTH),
      freq.reshape(1, hid), deltas)


FFT_N2 = 256
FFT_K1_PER_STEP = 4


def _dft_tables(n_total, n1_in):
    n1 = n_total // FFT_N2
    k1 = np.arange(n1)
    a = 2.0 * np.pi * np.outer(k1, np.arange(n1_in)) / n1
    lvl1 = np.concatenate([np.cos(a), -np.sin(a)], axis=0)
    n2 = np.arange(FFT_N2)
    th = 2.0 * np.pi * np.outer(n2, n2) / FFT_N2
    fc, fs = np.cos(th), np.sin(th)
    g = np.block([[fc, fs], [-fs, fc]])
    ph = 2.0 * np.pi * np.outer(k1, n2) / n_total
    tw = np.stack([np.cos(ph), np.sin(ph)])[..., None]
    return (jnp.asarray(lvl1, F32), jnp.asarray(g, F32), jnp.asarray(g.T, F32), jnp.asarray(tw, F32))


def _inverse_lvl1_table(n_total, n1_out):
    n1 = n_total // FFT_N2
    a = 2.0 * np.pi * np.outer(np.arange(n1_out), np.arange(n1)) / n1
    return jnp.asarray(np.concatenate([np.cos(a), -np.sin(a)], axis=1), F32)


def _lvl2_kernel(b_ref, tw_ref, gh_ref, gl_ref, *rest, inverse):
    for j in range(b_ref.shape[1]):
        c = tw_ref[0, j]
        s = tw_ref[1, j]
        br = b_ref[0, j]
        bi = b_ref[1, j]
        d = jnp.concatenate([c * br + s * bi, c * bi - s * br], axis=0)
        dh, dl = _split(d)
        x = _dot(gh_ref[...], dh) + _dot(gh_ref[...], dl) + _dot(gl_ref[...], dh)
        if not inverse:
            (o_ref,) = rest
            o_ref[0, j] = x[:FFT_N2]
            o_ref[1, j] = x[FFT_N2:]
            continue
        h_ref, gth_ref, gtl_ref, o_ref = rest
        xr, xi = x[:FFT_N2], x[FFT_N2:]
        hr, hi = h_ref[0, j], h_ref[1, j]
        y = jnp.concatenate([xr * hr - xi * hi, xr * hi + xi * hr], axis=0)
        yh, yl = _split(y)
        cc = _dot(gth_ref[...], yh) + _dot(gth_ref[...], yl) + _dot(gtl_ref[...], yh)
        cr, ci = cc[:FFT_N2], cc[FFT_N2:]
        o_ref[0, j] = c * cr - s * ci
        o_ref[1, j] = c * ci + s * cr


def _level2(b, tw, g, h=None, gt=None):
    _, n1, n2, ch = b.shape
    kb = FFT_K1_PER_STEP
    blk = pl.BlockSpec((2, kb, n2, ch), lambda i: (0, i, 0, 0))
    const = pl.BlockSpec((2 * n2, 2 * n2), lambda i: (0, 0))
    gh, gl = _split(g)
    in_specs = [blk, pl.BlockSpec((2, kb, n2, 1), lambda i: (0, i, 0, 0)), const, const]
    args = [b, tw, gh, gl]
    if h is not None:
        gth, gtl = _split(gt)
        in_specs += [blk, const, const]
        args += [h, gth, gtl]
    return pl.pallas_call(
        functools.partial(_lvl2_kernel, inverse=h is not None),
        out_shape=jax.ShapeDtypeStruct(b.shape, F32),
        grid=(n1 // kb,),
        in_specs=in_specs,
        out_specs=blk,
        compiler_params=_cparams("parallel"),
        name="hyena_fft_level2" + ("_conv" if h is not None else ""),
    )(*args)


def _hyena_long_conv(z, taps):
    seq_len, ch = z.shape
    n = 2 * seq_len
    n1 = n // FFT_N2
    lvl1_z, g, gt, tw = _dft_tables(n, n1 // 2)
    lvl1_t = _dft_tables(n, n1)[0]
    inv1 = _inverse_lvl1_table(n, n1 // 2)
    cols = FFT_N2 * ch
    tn = min(cols, 4096)
    hb = _mm3(lvl1_t, taps.reshape(n1, cols), tn, "hyena_fft_level1_taps")
    hspec = _level2(hb.reshape(2, n1, FFT_N2, ch), tw, g)
    zb = _mm3(lvl1_z, z.reshape(n1 // 2, cols), tn, "hyena_fft_level1_z")
    cspec = _level2(zb.reshape(2, n1, FFT_N2, ch), tw, g, hspec, gt)
    conv = _mm3(inv1, cspec.reshape(2 * n1, cols), tn, "hyena_fft_inverse_level1")
    return conv.reshape(seq_len, ch)


def _small_conv_kernel(z_ref, taps_ref, fc_ref, fs_ref, o_ref):
    seq_len = z_ref.shape[0]
    fc = fc_ref[...]
    fs = fs_ref[...]
    z = z_ref[...]
    taps = taps_ref[...]
    zr = _dot3(fc[:, :seq_len], z)
    zi = -_dot3(fs[:, :seq_len], z)
    hr = _dot3(fc, taps)
    hi = -_dot3(fs, taps)
    yr = zr * hr - zi * hi
    yi = zr * hi + zi * hr
    o_ref[...] = _dot3(fc[:seq_len, :], yr) - _dot3(fs[:seq_len, :], yi)


def _hyena_small_conv(z, taps):
    seq_len, ch = z.shape
    n = 2 * seq_len
    th = 2.0 * np.pi * np.outer(np.arange(n), np.arange(n)) / n
    full = lambda shape: pl.BlockSpec(shape, lambda: (0, 0))
    return pl.pallas_call(
        _small_conv_kernel,
        out_shape=jax.ShapeDtypeStruct((seq_len, ch), F32),
        in_specs=[full((seq_len, ch)), full((n, ch)), full((n, n)), full((n, n))],
        out_specs=full((seq_len, ch)),
        compiler_params=pltpu.CompilerParams(vmem_limit_bytes=VMEM_LIMIT),
        name="hyena_small_conv",
    )(z, taps, jnp.asarray(np.cos(th), F32), jnp.asarray(np.sin(th), F32))


def _hy_post_kernel(x0_ref, z_ref, conv_ref, asum_ref, bias_ref, y_ref, *, n_total):
    norm = jnp.sum(asum_ref[...], axis=0, keepdims=True) + EPS
    conv = conv_ref[...] * (1.0 / (n_total * norm))
    y_ref[...] = (x0_ref[...] * (conv + bias_ref[...] * z_ref[...])).astype(y_ref.dtype)


def _hyena_post(x0, z, conv, asum, bias):
    t = x0.shape[0]
    tm = min(t, 1024)
    row = pl.BlockSpec((tm, HY_WIDTH), lambda i: (i, 0))
    return pl.pallas_call(
        functools.partial(_hy_post_kernel, n_total=2 * t),
        out_shape=jax.ShapeDtypeStruct((t, HY_WIDTH), BF16),
        grid=(t // tm,),
        in_specs=[row, row, row, pl.BlockSpec((8, HY_WIDTH), lambda i: (0, 0)),
                  pl.BlockSpec((1, HY_WIDTH), lambda i: (0, 0))],
        out_specs=row,
        compiler_params=_cparams("parallel"),
        name="hyena_post",
    )(x0, z, conv, asum, bias.reshape(1, HY_WIDTH))


def _hyena_mixer(p_hy, conv_w, conv_b, filt, bias):
    seq_len = p_hy.shape[0]
    x0, z = _hyena_pre(p_hy, conv_w, conv_b)
    taps, asum = _hyena_taps(seq_len, *filt)
    if 2 * seq_len // FFT_N2 >= 16:
        conv = _hyena_long_conv(z, taps)
    else:
        conv = _hyena_small_conv(z, taps)
    return _hyena_post(x0, z, conv, asum, bias)


def _outproj_kernel(hy_ref, ret_ref, att_ref, w_ref, x_ref, gate_ref, o_ref):
    y = (_dot(hy_ref[...], w_ref[:HY_WIDTH, :])
         + _dot(ret_ref[...], w_ref[HY_WIDTH:HY_WIDTH + RET_WIDTH, :])
         + _dot(att_ref[...], w_ref[HY_WIDTH + RET_WIDTH:, :]))
    o_ref[...] = x_ref[...] + gate_ref[...] * y


def _out_projection(y_hy, y_ret, y_att, w_bf16, x, gate):
    t = x.shape[0]
    tm = min(t, 512)
    row = lambda w: pl.BlockSpec((tm, w), lambda i: (i, 0))
    return pl.pallas_call(
        _outproj_kernel,
        out_shape=jax.ShapeDtypeStruct((t, D_MODEL), F32),
        grid=(t // tm,),
        in_specs=[row(HY_WIDTH), row(RET_WIDTH), row(ATT_WIDTH),
                  pl.BlockSpec((MIX_WIDTH, D_MODEL), lambda i: (0, 0)),
                  row(D_MODEL), pl.BlockSpec((1, D_MODEL), lambda i: (0, 0))],
        out_specs=row(D_MODEL),
        compiler_params=_cparams("parallel"),
        name="out_projection",
    )(y_hy, y_ret, y_att, w_bf16, x, gate)


def _swiglu(x, wg_ref, wu_ref, wd_ref, between=None):
    hidden = wg_ref.shape[1]
    starts = list(range(0, hidden, SWIGLU_CHUNK))
    y = None
    for i, lo in enumerate(starts):
        if between is not None:
            between(i, len(starts))
        hi = min(lo + SWIGLU_CHUNK, hidden)
        a = _silu(_dot(x, wg_ref[:, lo:hi])) * _dot(x, wu_ref[:, lo:hi])
        part = _dot(a.astype(BF16), wd_ref[lo:hi, :])
        y = part if y is None else y + part
    return y


def _ffn_kernel(x_ref, g_ref, sh_ref, sc_ref, gate_ref, wg_ref, wu_ref, wd_ref, o_ref):
    x = x_ref[...]
    h = _norm_mod(x, g_ref[...], sh_ref[...], sc_ref[...]).astype(BF16)
    o_ref[...] = x + gate_ref[...] * _swiglu(h, wg_ref, wu_ref, wd_ref)


def _dense_ffn(x, g, shift, scale, gate, wg, wu, wd):
    t = x.shape[0]
    tm = min(t, 512)
    vec = pl.BlockSpec((1, D_MODEL), lambda i: (0, 0))
    resident = lambda shape: pl.BlockSpec(shape, lambda i: (0, 0), pipeline_mode=pl.Buffered(1))
    return pl.pallas_call(
        _ffn_kernel,
        out_shape=jax.ShapeDtypeStruct((t, D_MODEL), F32),
        grid=(t // tm,),
        in_specs=[pl.BlockSpec((tm, D_MODEL), lambda i: (i, 0)), vec, vec, vec, vec,
                  resident((D_MODEL, FFN_HIDDEN)), resident((D_MODEL, FFN_HIDDEN)),
                  resident((FFN_HIDDEN, D_MODEL))],
        out_specs=pl.BlockSpec((tm, D_MODEL), lambda i: (i, 0)),
        compiler_params=_cparams("parallel"),
        name="dense_ffn",
    )(x, g, shift, scale, gate, wg, wu, wd)


def _moe_route_kernel(x_ref, g_ref, sh_ref, sc_ref, rw_ref, rb_ref, h_ref, info_ref):
    h = _norm_mod(x_ref[...], g_ref[...], sh_ref[...], sc_ref[...])
    h_ref[...] = h
    logits = _dot3(h, rw_ref[...]) + rb_ref[...]
    lane = lax.broadcasted_iota(jnp.int32, logits.shape, 1)
    v1 = jnp.max(logits, axis=-1, keepdims=True)
    i1 = jnp.min(jnp.where(logits == v1, lane, LANES), axis=-1, keepdims=True)
    rest = jnp.where(lane == i1, -jnp.inf, logits)
    v2 = jnp.max(rest, axis=-1, keepdims=True)
    i2 = jnp.min(jnp.where(rest == v2, lane, LANES), axis=-1, keepdims=True)
    e = jnp.exp(v2 - v1)
    g1 = 1.0 / (1.0 + e)
    g2 = e * g1
    info_ref[...] = jnp.where(lane == 0, i1.astype(F32),
                              jnp.where(lane == 1, i2.astype(F32),
                                        jnp.where(lane == 2, g1, jnp.where(lane == 3, g2, 0.0))))


def _moe_route(x, g, shift, scale, router_w, router_b):
    t = x.shape[0]
    tm = min(t, 512)
    rw = jnp.concatenate([router_w, jnp.zeros((D_MODEL, LANES - N_EXPERTS), F32)], axis=1)
    rb = jnp.concatenate([router_b, jnp.full((LANES - N_EXPERTS,), -jnp.inf, F32)]).reshape(1, LANES)
    vec = pl.BlockSpec((1, D_MODEL), lambda i: (0, 0))
    return pl.pallas_call(
        _moe_route_kernel,
        out_shape=(jax.ShapeDtypeStruct((t, D_MODEL), F32), jax.ShapeDtypeStruct((t, LANES), F32)),
        grid=(t // tm,),
        in_specs=[pl.BlockSpec((tm, D_MODEL), lambda i: (i, 0)), vec, vec, vec,
                  pl.BlockSpec((D_MODEL, LANES), lambda i: (0, 0)),
                  pl.BlockSpec((1, LANES), lambda i: (0, 0))],
        out_specs=(pl.BlockSpec((tm, D_MODEL), lambda i: (i, 0)),
                   pl.BlockSpec((tm, LANES), lambda i: (i, 0))),
        compiler_params=_cparams("parallel"),
        name="moe_route",
    )(x, g, shift, scale, rw, rb)


def _row_copy(src_hbm, dst_vmem, sem, src_row, dst_row):
    return pltpu.make_async_copy(src_hbm.at[pl.ds(src_row, 1)], dst_vmem.at[pl.ds(dst_row, 1)], sem)


def _moe_expert_kernel(blk_e_ref, nused_ref, src_ref, h_hbm, wg_ref, wu_ref, wd_ref, y_ref,
                       xbuf_ref, sem):
    b = pl.program_id(0)
    n_used = nused_ref[0]

    def issue_rows(blk, slot):
        def body(r, carry):
            _row_copy(h_hbm, xbuf_ref.at[slot], sem.at[slot], src_ref[blk * MOE_BLOCK + r], r).start()
            return carry

        lax.fori_loop(0, MOE_BLOCK, body, 0, unroll=DMA_UNROLL)

    def wait_rows(slot):
        def body(r, carry):
            _row_copy(h_hbm, xbuf_ref.at[slot], sem.at[slot], 0, r).wait()
            return carry

        lax.fori_loop(0, MOE_BLOCK, body, 0, unroll=DMA_UNROLL)

    @pl.when(b < n_used)
    def _():
        slot = b % 2

        @pl.when(b == 0)
        def _():
            issue_rows(0, 0)

        wait_rows(slot)

        nxt = jnp.minimum(b + 1, n_used - 1)

        def issue_some(i, n):
            per = MOE_BLOCK // n + 1
            for r in range(i * per, min((i + 1) * per, MOE_BLOCK)):
                _row_copy(h_hbm, xbuf_ref.at[1 - slot], sem.at[1 - slot],
                          src_ref[nxt * MOE_BLOCK + r], r).start()

        y_ref[...] = _swiglu(xbuf_ref[slot].astype(BF16), wg_ref.at[0], wu_ref.at[0], wd_ref.at[0],
                             between=issue_some)

        @pl.when(b + 1 == n_used)
        def _():
            wait_rows(1 - slot)

    @pl.when(b >= n_used)
    def _():
        y_ref[...] = jnp.zeros_like(y_ref)


def _moe_experts(h, blk_expert, n_used, src_tok, wg, wu, wd, n_blk):
    def w_index(b, be, nu, st):
        return be[jnp.minimum(b, nu[0] - 1)], 0, 0

    grid_spec = pltpu.PrefetchScalarGridSpec(
        num_scalar_prefetch=3,
        grid=(n_blk,),
        in_specs=[pl.BlockSpec(memory_space=pl.ANY),
                  pl.BlockSpec((1, D_MODEL, FFN_HIDDEN), w_index),
                  pl.BlockSpec((1, D_MODEL, FFN_HIDDEN), w_index),
                  pl.BlockSpec((1, FFN_HIDDEN, D_MODEL), w_index)],
        out_specs=pl.BlockSpec((MOE_BLOCK, D_MODEL), lambda b, be, nu, st: (b, 0)),
        scratch_shapes=[pltpu.VMEM((2, MOE_BLOCK, D_MODEL), F32),
                        pltpu.SemaphoreType.DMA((2,))],
    )
    return pl.pallas_call(
        _moe_expert_kernel,
        out_shape=jax.ShapeDtypeStruct((n_blk * MOE_BLOCK, D_MODEL), F32),
        grid_spec=grid_spec,
        compiler_params=_cparams("arbitrary"),
        name="moe_experts",
    )(blk_expert, n_used, src_tok, h, wg, wu, wd)


def _moe_combine_kernel(dest_ref, x_ref, info_ref, gate_ref, ng_ref, y_hbm, o_ref,
                        y1_ref, y2_ref, sem, *, final_norm):
    i = pl.program_id(0)
    tm = x_ref.shape[0]
    slot = i % 2

    def issue_rows(blk, slot):
        def body(r, carry):
            a = 2 * (blk * tm + r)
            _row_copy(y_hbm, y1_ref.at[slot], sem.at[slot], dest_ref[a], r).start()
            _row_copy(y_hbm, y2_ref.at[slot], sem.at[slot], dest_ref[a + 1], r).start()
            return carry

        lax.fori_loop(0, tm, body, 0, unroll=DMA_UNROLL)

    def wait_rows(slot):
        def body(r, carry):
            _row_copy(y_hbm, y1_ref.at[slot], sem.at[slot], 0, r).wait()
            _row_copy(y_hbm, y2_ref.at[slot], sem.at[slot], 0, r).wait()
            return carry

        lax.fori_loop(0, tm, body, 0, unroll=DMA_UNROLL)

    @pl.when(i == 0)
    def _():
        issue_rows(0, 0)

    wait_rows(slot)

    @pl.when(i + 1 < pl.num_programs(0))
    def _():
        issue_rows(i + 1, 1 - slot)

    info = info_ref[...]
    y = y1_ref[slot] * info[:, 2:3] + y2_ref[slot] * info[:, 3:4]
    x = x_ref[...] + gate_ref[...] * y
    if final_norm:
        ms = jnp.mean(x * x, axis=-1, keepdims=True)
        x = x * lax.rsqrt(ms + EPS) * ng_ref[...]
    o_ref[...] = x


def _moe_combine(dest, x, info, gate, norm_g, y, final_norm):
    t = x.shape[0]
    tm = min(t, 256)
    vec = pl.BlockSpec((1, D_MODEL), lambda i, d: (0, 0))
    grid_spec = pltpu.PrefetchScalarGridSpec(
        num_scalar_prefetch=1,
        grid=(t // tm,),
        in_specs=[pl.BlockSpec((tm, D_MODEL), lambda i, d: (i, 0)),
                  pl.BlockSpec((tm, LANES), lambda i, d: (i, 0)), vec, vec,
                  pl.BlockSpec(memory_space=pl.ANY)],
        out_specs=pl.BlockSpec((tm, D_MODEL), lambda i, d: (i, 0)),
        scratch_shapes=[pltpu.VMEM((2, tm, D_MODEL), F32), pltpu.VMEM((2, tm, D_MODEL), F32),
                        pltpu.SemaphoreType.DMA((2,))],
    )
    return pl.pallas_call(
        functools.partial(_moe_combine_kernel, final_norm=final_norm),
        out_shape=jax.ShapeDtypeStruct((t, D_MODEL), F32),
        grid_spec=grid_spec,
        compiler_params=_cparams("arbitrary"),
        name="moe_combine",
    )(dest, x, info, gate, norm_g, y)


def _moe_layer(x, g, shift, scale, gate, router_w, router_b, wg, wu, wd, norm_g, final_norm):
    t = x.shape[0]
    n_asg = t * TOP_K
    h, info = _moe_route(x, g, shift, scale, router_w, router_b)
    expert = info[:, :TOP_K].astype(jnp.int32).reshape(-1)
    onehot = (expert[:, None] == jnp.arange(N_EXPERTS, dtype=jnp.int32)[None, :]).astype(jnp.int32)
    csum = jnp.cumsum(onehot, axis=0)
    counts = csum[-1]
    padded = (counts + MOE_BLOCK - 1) // MOE_BLOCK * MOE_BLOCK
    pad_end = jnp.cumsum(padded)
    pad_start = pad_end - padded
    dest = jnp.sum(onehot * (csum - 1 + pad_start[None, :]), axis=1).astype(jnp.int32)
    n_blk = -(-n_asg // MOE_BLOCK) + N_EXPERTS
    blk_start = jnp.arange(n_blk, dtype=jnp.int32) * MOE_BLOCK
    blk_expert = jnp.minimum(jnp.sum(blk_start[:, None] >= pad_end[None, :], axis=1),
                             N_EXPERTS - 1).astype(jnp.int32)
    n_used = (pad_end[-1:] // MOE_BLOCK).astype(jnp.int32)
    src_tok = jnp.zeros((n_blk * MOE_BLOCK,), jnp.int32).at[dest].set(
        jnp.arange(n_asg, dtype=jnp.int32) // TOP_K)
    y = _moe_experts(h, blk_expert, n_used, src_tok, wg, wu, wd, n_blk)
    return _moe_combine(dest, x, info, gate, norm_g, y, final_norm)


def _final_norm_kernel(x_ref, g_ref, o_ref):
    x = x_ref[...]
    ms = jnp.mean(x * x, axis=-1, keepdims=True)
    o_ref[...] = x * lax.rsqrt(ms + EPS) * g_ref[...]


def _final_norm(x, g):
    t = x.shape[0]
    tm = min(t, 1024)
    return pl.pallas_call(
        _final_norm_kernel,
        out_shape=jax.ShapeDtypeStruct((t, D_MODEL), F32),
        grid=(t // tm,),
        in_specs=[pl.BlockSpec((tm, D_MODEL), lambda i: (i, 0)),
                  pl.BlockSpec((1, D_MODEL), lambda i: (0, 0))],
        out_specs=pl.BlockSpec((tm, D_MODEL), lambda i: (i, 0)),
        compiler_params=_cparams("parallel"),
        name="final_norm",
    )(x, g)


def kernel(x, c, ctx, c_ctx, ada_w, ada_b, norm1_g, norm2_g, w_in, w_out, hy_conv_w, hy_conv_b, hy_filt_w1, hy_filt_b1, hy_filt_w2, hy_filt_b2, hy_filt_w3, hy_filt_b3, hy_filt_freq, hy_bias, ret_log_rate, attn_q_g, attn_k_g, ffn_w_gate, ffn_w_up, ffn_w_down, moe_router_w, moe_router_b, moe_w_gate, moe_w_up, moe_w_down, final_norm_g):
    assert x.shape[0] == 1 and c.shape[0] == 1
    seq_len = x.shape[1]
    x_lat = x[0]
    x_ctx = ctx[0]
    cos_t, sin_t = _rope_tables(seq_len)
    cvec = jnp.concatenate([c, c_ctx[None, :], jnp.zeros((6, D_MODEL), F32)], axis=0)
    cvec = _silu(cvec)
    row = lambda v: v.reshape(1, D_MODEL)
    zero_state = jnp.zeros((2, RET_WIDTH, RET_HEAD_DIM), F32)
    zero_tab = jnp.zeros((x_ctx.shape[0], LANES), F32)

    for l in range(DEPTH):
        last = l == DEPTH - 1
        mods = _ada_modulation(cvec, ada_w[l], ada_b[l]).reshape(8, 6, D_MODEL)
        mod = [row(mods[0, i]) for i in range(6)]
        mod_c = [row(mods[1, i]) for i in range(6)]
        filt = (hy_filt_w1[l], hy_filt_b1[l], hy_filt_w2[l], hy_filt_b2[l],
                hy_filt_w3[l], hy_filt_b3[l], hy_filt_freq[l])
        lg_f = -jnp.exp(ret_log_rate[l, 0].astype(F32))
        lg_b = -jnp.exp(ret_log_rate[l, 1].astype(F32))
        mask, zeta, xi, chunk_decay = _ret_tables(lg_f, lg_b)
        w_in_l = w_in[l].astype(BF16)
        w_out_l = w_out[l].astype(BF16)
        n1 = row(norm1_g[l])
        n2 = row(norm2_g[l])

        p_hy, p_ret, p_att = _in_projection(x_lat, n1, mod[0], mod[1], w_in_l)
        pc_hy, pc_ret, pc_att = _in_projection(x_ctx, n1, mod_c[0], mod_c[1], w_in_l)

        sprev_c, s_ctx = _retention_states(pc_ret, zeta, chunk_decay, zero_state)
        qct, kc, vct, _ = _attention_prep(pc_att, zero_tab, zero_tab, attn_q_g[l], attn_k_g[l],
                                          rope=False)

        y_hy = _hyena_mixer(p_hy, hy_conv_w[l], hy_conv_b[l], filt, hy_bias[l])
        sprev, _ = _retention_states(p_ret, zeta, chunk_decay, s_ctx)
        y_ret = _retention_outputs(p_ret, mask, xi, sprev)
        qt, k_aug, vt, kmax_sq = _attention_prep(p_att, cos_t, sin_t, attn_q_g[l], attn_k_g[l],
                                                 rope=True)
        kmax = jnp.sqrt(jnp.max(kmax_sq[:, 0, ::ATT_HEAD_DIM], axis=0)) * KMAX_SLACK
        y_att = _attention(qt, kc, vct, kmax, k_aug, vt, tq=ATT_TQ, tk=ATT_TK, sub=ATT_SUB)

        if not last:
            yc_hy = _hyena_mixer(pc_hy, hy_conv_w[l], hy_conv_b[l], filt, hy_bias[l])
            yc_ret = _retention_outputs(pc_ret, mask, xi, sprev_c)
            yc_att = _attention(qct, kc, vct, tq=x_ctx.shape[0])
            x_ctx = _out_projection(yc_hy, yc_ret, yc_att, w_out_l, x_ctx, mod_c[2])
        x_lat = _out_projection(y_hy, y_ret, y_att, w_out_l, x_lat, mod[2])

        i = l // 2
        if l % 2 == 0:
            wg, wu, wd = (ffn_w_gate[i].astype(BF16), ffn_w_up[i].astype(BF16),
                          ffn_w_down[i].astype(BF16))
            x_lat = _dense_ffn(x_lat, n2, mod[3], mod[4], mod[5], wg, wu, wd)
            if not last:
                x_ctx = _dense_ffn(x_ctx, n2, mod_c[3], mod_c[4], mod_c[5], wg, wu, wd)
        else:
            wg, wu, wd = (moe_w_gate[i].astype(BF16), moe_w_up[i].astype(BF16),
                          moe_w_down[i].astype(BF16))
            fg = row(final_norm_g)
            x_lat = _moe_layer(x_lat, n2, mod[3], mod[4], mod[5], moe_router_w[i], moe_router_b[i],
                               wg, wu, wd, fg, final_norm=last)
            if not last:
                x_ctx = _moe_layer(x_ctx, n2, mod_c[3], mod_c[4], mod_c[5], moe_router_w[i],
                                   moe_router_b[i], wg, wu, wd, fg, final_norm=False)
    if DEPTH % 2 == 1:
        x_lat = _final_norm(x_lat, row(final_norm_g))
    return x_lat[None]
```

```python
import functools
import math

import numpy as np
import jax
import jax.numpy as jnp
from jax import lax
from jax.experimental import pallas as pl
from jax.experimental.pallas import tpu as pltpu

F32 = jnp.float32
BF16 = jnp.bfloat16

D_MODEL = 1024
DEPTH = 2
GRID_W = 64
EPS = 1e-6

HY_WIDTH = 256
HY_COLS = 3 * HY_WIDTH
HY_POS_BANDS = 16
HY_DECAY_TARGET = 1e-2
HY_SHORT_DECAY_PCT = 0.3
HY_LONG_DECAY_PCT = 1.5

RET_HEAD_DIM = 64
RET_HEADS = 4
RET_WIDTH = RET_HEADS * RET_HEAD_DIM
RET_COLS = 4 * RET_WIDTH
RET_CHUNK = 128

ATT_HEAD_DIM = 64
ATT_HEADS = 8
ATT_KV_HEADS = 2
ATT_GROUP = ATT_HEADS // ATT_KV_HEADS
ATT_WIDTH = ATT_HEADS * ATT_HEAD_DIM
ATT_KV_WIDTH = ATT_KV_HEADS * ATT_HEAD_DIM
ATT_COLS = ATT_WIDTH + 2 * ATT_KV_WIDTH
ROPE_AXIS_DIM = ATT_HEAD_DIM // 2
ROPE_BASE = 10000.0

MIX_WIDTH = HY_WIDTH + RET_WIDTH + ATT_WIDTH
IN_COLS = HY_COLS + RET_COLS + ATT_COLS

FFN_HIDDEN = 2816
N_EXPERTS = 8
TOP_K = 2
MOE_BLOCK = 512

LOG2_E = 1.4426950408889634
EXP2_CAP = 60.0
KMAX_SLACK = 1.0 + 2.0 ** -7
ATT_TQ = 256
ATT_TK = 4096
ATT_SUB = 512
RET_STATE_CHUNKS = 8
RET_OUT_CHUNKS = 4
SWIGLU_CHUNK = 1024
DMA_UNROLL = 8

LANES = 128
SUBLANES = 8
VMEM_LIMIT = 56 * 1024 * 1024


def _cparams(*sem):
    return pltpu.CompilerParams(dimension_semantics=sem, vmem_limit_bytes=VMEM_LIMIT)


def _dot(a, b):
    return jnp.dot(a, b, preferred_element_type=F32)


def _split(a):
    hi = a.astype(BF16)
    lo = (a - hi.astype(F32)).astype(BF16)
    return hi, lo


def _dot3(a, b):
    ah, al = _split(a)
    bh, bl = _split(b)
    return _dot(ah, bh) + _dot(al, bh) + _dot(ah, bl)


def _silu(x):
    return x * (1.0 / (1.0 + jnp.exp(-x)))


def _norm_mod(x, g, shift, scale):
    ms = jnp.mean(x * x, axis=-1, keepdims=True)
    h = x * lax.rsqrt(ms + EPS) * g
    return h * (1.0 + scale) + shift


def _mm3_kernel(a_ref, b_ref, o_ref):
    o_ref[...] = _dot3(a_ref[...], b_ref[...])


def _mm3(a, b, tn, name):
    m, k = a.shape
    n = b.shape[1]
    return pl.pallas_call(
        _mm3_kernel,
        out_shape=jax.ShapeDtypeStruct((m, n), F32),
        grid=(n // tn,),
        in_specs=[pl.BlockSpec((m, k), lambda j: (0, 0)),
                  pl.BlockSpec((k, tn), lambda j: (0, j))],
        out_specs=pl.BlockSpec((m, tn), lambda j: (0, j)),
        compiler_params=_cparams("parallel"),
        name=name,
    )(a, b)


def _ada_kernel(c_ref, w_ref, b_ref, o_ref):
    o_ref[...] = _dot3(c_ref[...], w_ref[0]) + b_ref[0]


def _ada_modulation(cc, w, b, layer):
    n = w.shape[2]
    tn = 1536
    return pl.pallas_call(
        _ada_kernel,
        out_shape=jax.ShapeDtypeStruct((8, n), F32),
        grid=(n // tn,),
        in_specs=[pl.BlockSpec((8, D_MODEL), lambda j: (0, 0)),
                  pl.BlockSpec((1, D_MODEL, tn), lambda j: (layer, 0, j)),
                  pl.BlockSpec((1, 1, tn), lambda j: (layer, 0, j))],
        out_specs=pl.BlockSpec((8, tn), lambda j: (0, j)),
        compiler_params=_cparams("parallel"),
        name="ada_modulation",
    )(cc, w, b.reshape(b.shape[0], 1, n))


def _inproj_kernel(x_ref, g_ref, sh_ref, sc_ref, w_ref, hy_ref, ret_ref, att_ref):
    h = _norm_mod(x_ref[...], g_ref[...], sh_ref[...], sc_ref[...])
    p = _dot(h.astype(BF16), w_ref[...])
    hy_ref[...] = p[:, :HY_COLS]
    ret_ref[...] = p[:, HY_COLS:HY_COLS + RET_COLS]
    att_ref[...] = p[:, HY_COLS + RET_COLS:]


def _in_projection(x, g, shift, scale, w_bf16):
    t = x.shape[0]
    tm = min(t, 512)
    vec = pl.BlockSpec((1, D_MODEL), lambda i: (0, 0))
    return pl.pallas_call(
        _inproj_kernel,
        out_shape=(jax.ShapeDtypeStruct((t, HY_COLS), F32),
                   jax.ShapeDtypeStruct((t, RET_COLS), F32),
                   jax.ShapeDtypeStruct((t, ATT_COLS), F32)),
        grid=(t // tm,),
        in_specs=[pl.BlockSpec((tm, D_MODEL), lambda i: (i, 0)), vec, vec, vec,
                  pl.BlockSpec((D_MODEL, IN_COLS), lambda i: (0, 0))],
        out_specs=(pl.BlockSpec((tm, HY_COLS), lambda i: (i, 0)),
                   pl.BlockSpec((tm, RET_COLS), lambda i: (i, 0)),
                   pl.BlockSpec((tm, ATT_COLS), lambda i: (i, 0))),
        compiler_params=_cparams("parallel"),
        name="in_projection",
    )(x, g, shift, scale, w_bf16)


def _head_mean_matrix(width):
    idx = np.arange(width) // ATT_HEAD_DIM
    return jnp.asarray((idx[:, None] == idx[None, :]).astype(np.float32) / ATT_HEAD_DIM, BF16)


def _head_rms(x, bd, g):
    x2h, x2l = _split(x * x)
    ms = _dot(x2h, bd) + _dot(x2l, bd)
    return x * lax.rsqrt(ms + EPS) * g


def _rope(x, cos, sin):
    n = x.shape[1]
    lane = lax.broadcasted_iota(jnp.int32, x.shape, 1)
    swapped = jnp.where((lane % 32) < 16, pltpu.roll(x, n - 16, 1), pltpu.roll(x, 16, 1))
    return x * cos + swapped * sin


def _attprep_kernel(p_ref, cos_ref, sin_ref, qg_ref, kg_ref, bdq_ref, bdk_ref,
                    qt_ref, k_ref, vt_ref, kmax_ref, *, rope):
    p = p_ref[...]
    q = _head_rms(p[:, :ATT_WIDTH], bdq_ref[...], qg_ref[...])
    k = _head_rms(p[:, ATT_WIDTH:ATT_WIDTH + ATT_KV_WIDTH], bdk_ref[...], kg_ref[...])
    if rope:
        cos = cos_ref[...]
        sin = sin_ref[...]
        q = jnp.concatenate(
            [_rope(q[:, j * LANES:(j + 1) * LANES], cos, sin) for j in range(ATT_WIDTH // LANES)],
            axis=1)
        k = _rope(k, cos, sin)
    qt_ref[...] = (q * (ATT_HEAD_DIM ** -0.5 * LOG2_E)).T.astype(BF16)
    tm = k.shape[0]
    lane = lax.broadcasted_iota(jnp.int32, k.shape, 1)
    minus_one_col = jnp.where(lane == ATT_HEAD_DIM, -1.0, 0.0)
    kb = k.astype(BF16)
    k_ref[0] = jnp.where(lane < ATT_HEAD_DIM, k, minus_one_col).astype(BF16)
    k_ref[1] = jnp.where(lane < ATT_HEAD_DIM, pltpu.roll(k, ATT_HEAD_DIM, 1), minus_one_col).astype(BF16)
    sq_hi, sq_lo = _split(kb.astype(F32) ** 2)
    norm_sq = (_dot(sq_hi, bdk_ref[...]) + _dot(sq_lo, bdk_ref[...])) * ATT_HEAD_DIM
    kmax_ref[0] = jnp.broadcast_to(jnp.max(norm_sq, axis=0, keepdims=True), (8, LANES))
    vt = p[:, ATT_WIDTH + ATT_KV_WIDTH:].T.astype(BF16)
    sub = lax.broadcasted_iota(jnp.int32, (ATT_HEAD_DIM, tm), 0)
    ones_row = jnp.where(sub == 0, 1.0, 0.0).astype(BF16)
    for h in range(ATT_KV_HEADS):
        vt_ref[h * LANES:h * LANES + ATT_HEAD_DIM, :] = vt[h * ATT_HEAD_DIM:(h + 1) * ATT_HEAD_DIM]
        vt_ref[h * LANES + ATT_HEAD_DIM:(h + 1) * LANES, :] = ones_row


def _attention_prep(p_att, cos, sin, q_g, k_g, rope):
    t = p_att.shape[0]
    tm = min(t, 512)
    qg = jnp.tile(q_g, ATT_HEADS).reshape(1, ATT_WIDTH)
    kg = jnp.tile(k_g, ATT_KV_HEADS).reshape(1, ATT_KV_WIDTH)
    const = lambda shape: pl.BlockSpec(shape, lambda i: (0, 0))
    return pl.pallas_call(
        functools.partial(_attprep_kernel, rope=rope),
        out_shape=(jax.ShapeDtypeStruct((ATT_WIDTH, t), BF16),
                   jax.ShapeDtypeStruct((ATT_KV_HEADS, t, LANES), BF16),
                   jax.ShapeDtypeStruct((ATT_KV_HEADS * LANES, t), BF16),
                   jax.ShapeDtypeStruct((t // tm, 8, LANES), F32)),
        grid=(t // tm,),
        in_specs=[pl.BlockSpec((tm, ATT_COLS), lambda i: (i, 0)),
                  pl.BlockSpec((tm, LANES), lambda i: (i, 0)),
                  pl.BlockSpec((tm, LANES), lambda i: (i, 0)),
                  const((1, ATT_WIDTH)), const((1, ATT_KV_WIDTH)),
                  const((ATT_WIDTH, ATT_WIDTH)), const((ATT_KV_WIDTH, ATT_KV_WIDTH))],
        out_specs=(pl.BlockSpec((ATT_WIDTH, tm), lambda i: (0, i)),
                   pl.BlockSpec((ATT_KV_HEADS, tm, LANES), lambda i: (0, i, 0)),
                   pl.BlockSpec((ATT_KV_HEADS * LANES, tm), lambda i: (0, i)),
                   pl.BlockSpec((1, 8, LANES), lambda i: (i, 0, 0))),
        compiler_params=_cparams("parallel"),
        name="attention_prep",
    )(p_att, cos, sin, qg, kg, _head_mean_matrix(ATT_WIDTH), _head_mean_matrix(ATT_KV_WIDTH))


def _rope_tables(n_tokens):
    rows = n_tokens // GRID_W
    inv_freq = ROPE_BASE ** (-jnp.arange(0, ROPE_AXIS_DIM, 2, dtype=F32) / ROPE_AXIS_DIM)
    ang_r = jnp.arange(rows, dtype=F32)[:, None] * inv_freq[None, :]
    ang_c = jnp.arange(GRID_W, dtype=F32)[:, None] * inv_freq[None, :]
    nf = inv_freq.shape[0]
    by_row = lambda tab: jnp.broadcast_to(tab[:, None, :], (rows, GRID_W, nf)).reshape(n_tokens, nf)
    by_col = lambda tab: jnp.broadcast_to(tab[None, :, :], (rows, GRID_W, nf)).reshape(n_tokens, nf)
    cr, sr = by_row(jnp.cos(ang_r)), by_row(jnp.sin(ang_r))
    cc, sc = by_col(jnp.cos(ang_c)), by_col(jnp.sin(ang_c))
    cos = jnp.concatenate([cr, cr, cc, cc], axis=1)
    sin = jnp.concatenate([-sr, sr, -sc, sc], axis=1)
    return jnp.tile(cos, (1, 2)), jnp.tile(sin, (1, 2))


def _attention_kernel(*refs, tq, tk, sub, n_lat):
    if n_lat:
        kmax_ref, qt_ref, kc_ref, vct_ref, k_ref, vt_ref, o_ref, qs_ref, acc_ref, m_ref = refs
    else:
        qt_ref, kc_ref, vct_ref, o_ref, acc_ref = refs
    cols = ATT_GROUP * tq
    qt = qt_ref[...]
    qst = jnp.concatenate(
        [qt[g * ATT_HEAD_DIM:(g + 1) * ATT_HEAD_DIM, :] for g in range(ATT_GROUP)], axis=1)
    pad = jnp.zeros((LANES - ATT_HEAD_DIM, cols), BF16)
    s = _dot(kc_ref[0], jnp.concatenate([qst, pad], axis=0))
    mt = jnp.max(s, axis=0, keepdims=True).astype(BF16)
    mtf = mt.astype(F32)
    acc_ref[...] = _dot(vct_ref[...], jnp.exp2(s - mtf).astype(BF16))

    if n_lat:
        row = lax.broadcasted_iota(jnp.int32, pad.shape, 0)
        qs_ref[...] = jnp.concatenate([qst, jnp.where(row == 0, mtf, 0.0).astype(BF16)], axis=0)
        qf = qst.astype(F32)
        qn = jnp.sqrt(jnp.sum(qf * qf, axis=0, keepdims=True))
        fast = jnp.max(qn * kmax_ref[pl.program_id(0)] - mtf) <= EXP2_CAP

        def tile(j):
            off = pl.multiple_of(j * tk, tk)
            return k_ref[0, pl.ds(off, tk), :], vt_ref[:, pl.ds(off, tk)]

        @pl.when(fast)
        def _():
            def body(j, carry):
                k, vt = tile(j)
                qs = qs_ref[...]
                part = None
                for c in range(tk // sub):
                    p = jnp.exp2(_dot(k[c * sub:(c + 1) * sub], qs)).astype(BF16)
                    pv = _dot(vt[:, c * sub:(c + 1) * sub], p)
                    part = pv if part is None else part + pv
                acc_ref[...] += part
                return carry

            lax.fori_loop(0, n_lat, body, 0)

        @pl.when(jnp.logical_not(fast))
        def _():
            m_ref[...] = jnp.zeros_like(m_ref)

            def body(j, carry):
                k, vt = tile(j)
                s = _dot(k, qs_ref[...])
                m_old = m_ref[...]
                m_new = jnp.maximum(m_old, jnp.max(s, axis=0, keepdims=True))
                p = jnp.exp2(s - m_new).astype(BF16)
                acc_ref[...] = jnp.exp2(m_old - m_new) * acc_ref[...] + _dot(vt, p)
                m_ref[...] = m_new
                return carry

            lax.fori_loop(0, n_lat, body, 0)

    acc = acc_ref[...]
    o = acc[:ATT_HEAD_DIM] / acc[ATT_HEAD_DIM:ATT_HEAD_DIM + 1]
    o_ref[...] = jnp.concatenate(
        [o[:, g * tq:(g + 1) * tq].T for g in range(ATT_GROUP)], axis=1).astype(o_ref.dtype)


def _attention(qt, kc, vct, kmax=None, k=None, vt=None, *, tq, tk=0, sub=0):
    t = qt.shape[1]
    lc = kc.shape[1]
    gw = ATT_GROUP * ATT_HEAD_DIM
    cols = ATT_GROUP * tq
    in_specs = [pl.BlockSpec((gw, tq), lambda h, i: (h, i)),
                pl.BlockSpec((1, lc, LANES), lambda h, i: (h, 0, 0)),
                pl.BlockSpec((LANES, lc), lambda h, i: (h, 0))]
    args = [qt, kc, vct]
    scratch = [pltpu.VMEM((LANES, cols), F32)]
    n_lat = 0
    if k is not None:
        lk = k.shape[1]
        n_lat = lk // tk
        in_specs = ([pl.BlockSpec(memory_space=pltpu.SMEM)] + in_specs
                    + [pl.BlockSpec((1, lk, LANES), lambda h, i: (h, 0, 0)),
                       pl.BlockSpec((LANES, lk), lambda h, i: (h, 0))])
        args = [kmax] + args + [k, vt]
        scratch = [pltpu.VMEM((LANES, cols), BF16)] + scratch + [pltpu.VMEM((1, cols), F32)]
    return pl.pallas_call(
        functools.partial(_attention_kernel, tq=tq, tk=tk, sub=sub, n_lat=n_lat),
        out_shape=jax.ShapeDtypeStruct((t, ATT_WIDTH), BF16),
        grid=(ATT_KV_HEADS, t // tq),
        in_specs=in_specs,
        out_specs=pl.BlockSpec((tq, gw), lambda h, i: (i, h)),
        scratch_shapes=scratch,
        compiler_params=_cparams("parallel", "parallel"),
        name="attention_lat" if n_lat else "attention_ctx",
    )(*args)


def _ret_tables(lg_f, lg_b):
    pos = jnp.arange(RET_CHUNK, dtype=F32)
    rel = pos[:, None] - pos[None, :]
    mask = jnp.where(rel > 0, jnp.exp(lg_f[:, None, None] * jnp.maximum(rel, 0.0)),
                     jnp.where(rel < 0, jnp.exp(lg_b[:, None, None] * jnp.maximum(-rel, 0.0)), 2.0))
    zeta = jnp.stack([jnp.exp(lg_f[:, None] * (RET_CHUNK - 1 - pos)[None, :]),
                      jnp.exp(lg_b[:, None] * pos[None, :])])
    xi = jnp.stack([jnp.exp(lg_f[:, None] * (pos + 1)[None, :]),
                    jnp.exp(lg_b[:, None] * (RET_CHUNK - pos)[None, :])])
    chunk_decay = jnp.stack([jnp.exp(lg_f * RET_CHUNK), jnp.exp(lg_b * RET_CHUNK)])
    zeta = jnp.repeat(zeta, RET_HEAD_DIM, axis=1).transpose(0, 2, 1)
    xi = jnp.repeat(xi, RET_HEAD_DIM, axis=1).transpose(0, 2, 1)
    chunk_decay = jnp.repeat(chunk_decay, RET_HEAD_DIM, axis=1)[:, :, None]
    return mask, zeta, xi, chunk_decay


def _ret_state_kernel(k_ref, v_ref, zeta_ref, cd_ref, s0_ref, sprev_ref, sfin_ref, s_ref, *, cps):
    d = pl.program_id(0)

    @pl.when(pl.program_id(1) == 0)
    def _():
        s_ref[...] = s0_ref[0]

    zeta = zeta_ref[0] * (RET_HEAD_DIM ** -0.5)
    cd = cd_ref[0]
    us = []
    for c in range(cps):
        rows = slice(c * RET_CHUNK, (c + 1) * RET_CHUNK)
        kz = (k_ref[rows, :] * zeta).astype(BF16)
        v = v_ref[rows, :].astype(BF16)
        heads = []
        for h in range(RET_HEADS):
            sl = slice(h * RET_HEAD_DIM, (h + 1) * RET_HEAD_DIM)
            heads.append(lax.dot_general(kz[:, sl], v[:, sl], (((0,), (0,)), ((), ())),
                                         preferred_element_type=F32))
        us.append(jnp.concatenate(heads, axis=0))

    def scan(order):
        s = s_ref[...]
        for c in order:
            sprev_ref[0, c] = s
            s = cd * s + us[c]
        s_ref[...] = s
        sfin_ref[0] = s

    @pl.when(d == 0)
    def _():
        scan(range(cps))

    @pl.when(d == 1)
    def _():
        scan(reversed(range(cps)))


def _retention_states(p_ret, zeta, chunk_decay, s0):
    nc = p_ret.shape[0] // RET_CHUNK
    cps = min(nc, RET_STATE_CHUNKS)
    ng = nc // cps
    rows = cps * RET_CHUNK

    def group(d, i):
        return i + d * (ng - 1 - 2 * i)

    return pl.pallas_call(
        functools.partial(_ret_state_kernel, cps=cps),
        out_shape=(jax.ShapeDtypeStruct((2, nc, RET_WIDTH, RET_HEAD_DIM), F32),
                   jax.ShapeDtypeStruct((2, RET_WIDTH, RET_HEAD_DIM), F32)),
        grid=(2, ng),
        in_specs=[pl.BlockSpec((rows, RET_WIDTH), lambda d, i: (group(d, i), 1)),
                  pl.BlockSpec((rows, RET_WIDTH), lambda d, i: (group(d, i), 2)),
                  pl.BlockSpec((1, RET_CHUNK, RET_WIDTH), lambda d, i: (d, 0, 0)),
                  pl.BlockSpec((1, RET_WIDTH, 1), lambda d, i: (d, 0, 0)),
                  pl.BlockSpec((1, RET_WIDTH, RET_HEAD_DIM), lambda d, i: (d, 0, 0))],
        out_specs=(pl.BlockSpec((1, cps, RET_WIDTH, RET_HEAD_DIM), lambda d, i: (d, group(d, i), 0, 0)),
                   pl.BlockSpec((1, RET_WIDTH, RET_HEAD_DIM), lambda d, i: (d, 0, 0))),
        scratch_shapes=[pltpu.VMEM((RET_WIDTH, RET_HEAD_DIM), F32)],
        compiler_params=_cparams("arbitrary", "arbitrary"),
        name="retention_states",
    )(p_ret, p_ret, zeta, chunk_decay, s0)


def _ret_out_kernel(q_ref, k_ref, v_ref, g_ref, mask_ref, xi_ref, sf_ref, sb_ref, y_ref, *, cpo):
    for c in range(cpo):
        rows = slice(c * RET_CHUNK, (c + 1) * RET_CHUNK)
        q = q_ref[rows, :]
        k = (k_ref[rows, :] * (RET_HEAD_DIM ** -0.5)).astype(BF16)
        v = v_ref[rows, :].astype(BF16)
        qb = q.astype(BF16)
        qf = (q * xi_ref[0]).astype(BF16)
        qr = (q * xi_ref[1]).astype(BF16)
        sf = sf_ref[0, c].astype(BF16)
        sb = sb_ref[0, c].astype(BF16)
        ys = []
        for h in range(RET_HEADS):
            sl = slice(h * RET_HEAD_DIM, (h + 1) * RET_HEAD_DIM)
            scores = lax.dot_general(qb[:, sl], k[:, sl], (((1,), (1,)), ((), ())),
                                     preferred_element_type=F32)
            y = _dot((scores * mask_ref[h]).astype(BF16), v[:, sl])
            y = y + _dot(qf[:, sl], sf[sl, :]) + _dot(qr[:, sl], sb[sl, :])
            mu = jnp.mean(y, axis=-1, keepdims=True)
            yc = y - mu
            var = jnp.mean(yc * yc, axis=-1, keepdims=True)
            ys.append(yc * lax.rsqrt(var + EPS))
        y_ref[rows, :] = (_silu(g_ref[rows, :]) * jnp.concatenate(ys, axis=1)).astype(y_ref.dtype)


def _retention_outputs(p_ret, mask, xi, sprev):
    nc = p_ret.shape[0] // RET_CHUNK
    cpo = min(nc, RET_OUT_CHUNKS)
    rows = cpo * RET_CHUNK
    col = lambda c: pl.BlockSpec((rows, RET_WIDTH), lambda i: (i, c))
    return pl.pallas_call(
        functools.partial(_ret_out_kernel, cpo=cpo),
        out_shape=jax.ShapeDtypeStruct((p_ret.shape[0], RET_WIDTH), BF16),
        grid=(nc // cpo,),
        in_specs=[col(0), col(1), col(2), col(3),
                  pl.BlockSpec((RET_HEADS, RET_CHUNK, RET_CHUNK), lambda i: (0, 0, 0)),
                  pl.BlockSpec((2, RET_CHUNK, RET_WIDTH), lambda i: (0, 0, 0)),
                  pl.BlockSpec((1, cpo, RET_WIDTH, RET_HEAD_DIM), lambda i: (0, i, 0, 0)),
                  pl.BlockSpec((1, cpo, RET_WIDTH, RET_HEAD_DIM), lambda i: (1, i, 0, 0))],
        out_specs=pl.BlockSpec((rows, RET_WIDTH), lambda i: (i, 0)),
        compiler_params=_cparams("parallel"),
        name="retention_outputs",
    )(p_ret, p_ret, p_ret, p_ret, mask, xi, sprev, sprev)


def _hy_pre_kernel(p_ref, prev_ref, next_ref, w_ref, b_ref, x0_ref, z_ref):
    i = pl.program_id(0)
    x = p_ref[...]
    tm = x.shape[0]
    row = lax.broadcasted_iota(jnp.int32, x.shape, 0)
    prev_row = jnp.where(i == 0, 0.0, prev_ref[7:8, :])
    next_row = jnp.where(i == pl.num_programs(0) - 1, 0.0, next_ref[0:1, :])
    up = jnp.where(row == 0, prev_row, pltpu.roll(x, 1, 0))
    dn = jnp.where(row == tm - 1, next_row, pltpu.roll(x, tm - 1, 0))
    w = w_ref[...]
    u = up * w[0:1] + x * w[1:2] + dn * w[2:3] + b_ref[...]
    x0_ref[...] = u[:, :HY_WIDTH]
    z_ref[...] = u[:, HY_WIDTH:2 * HY_WIDTH] * u[:, 2 * HY_WIDTH:]


def _hyena_pre(p_hy, conv_w, conv_b):
    t = p_hy.shape[0]
    tm = min(t, 512)
    nb8 = tm // 8
    last8 = t // 8 - 1
    return pl.pallas_call(
        _hy_pre_kernel,
        out_shape=(jax.ShapeDtypeStruct((t, HY_WIDTH), F32),
                   jax.ShapeDtypeStruct((t, HY_WIDTH), F32)),
        grid=(t // tm,),
        in_specs=[pl.BlockSpec((tm, HY_COLS), lambda i: (i, 0)),
                  pl.BlockSpec((8, HY_COLS), lambda i: (jnp.maximum(i * nb8 - 1, 0), 0)),
                  pl.BlockSpec((8, HY_COLS), lambda i: (jnp.minimum((i + 1) * nb8, last8), 0)),
                  pl.BlockSpec((3, HY_COLS), lambda i: (0, 0)),
                  pl.BlockSpec((1, HY_COLS), lambda i: (0, 0))],
        out_specs=(pl.BlockSpec((tm, HY_WIDTH), lambda i: (i, 0)),
                   pl.BlockSpec((tm, HY_WIDTH), lambda i: (i, 0))),
        compiler_params=_cparams("parallel"),
        name="hyena_pre",
    )(p_hy, p_hy, p_hy, conv_w, conv_b.reshape(1, HY_COLS))


def _hy_filter_kernel(feat_ref, w1_ref, b1_ref, w2_ref, b2_ref, w3_ref, b3_ref, freq_ref,
                      delta_ref, taps_ref, asum_ref, *, seq_len):
    i = pl.program_id(0)
    feat = feat_ref[...]
    freq = freq_ref[...]
    h = jnp.sin(freq * (_dot3(feat, w1_ref[...]) + b1_ref[...]))
    h = jnp.sin(freq * (_dot3(h, w2_ref[...]) + b2_ref[...]))
    h = _dot3(h, w3_ref[...]) + b3_ref[...]
    tm = feat.shape[0]
    j = i * tm + lax.broadcasted_iota(jnp.int32, (tm, HY_WIDTH), 0)
    t = feat[:, 0:1]
    hsel = jnp.where(j < seq_len, h[:, :HY_WIDTH], h[:, HY_WIDTH:])
    taps = jnp.where(j == seq_len, 0.0, hsel * jnp.exp(-t * delta_ref[...]))
    taps_ref[...] = taps

    @pl.when(i == 0)
    def _():
        asum_ref[...] = jnp.zeros_like(asum_ref)

    asum_ref[...] += jnp.sum(jnp.abs(taps).reshape(tm // 8, 8, HY_WIDTH), axis=0)


def _position_features(seq_len):
    lo = 256
    hi = seq_len // lo + 1
    bands = np.linspace(1e-4, HY_POS_BANDS - 1, HY_POS_BANDS).astype(np.float32).astype(np.float64)
    ang_hi = (2.0 * np.pi / seq_len) * lo * np.arange(hi)[:, None] * bands[None, :]
    ang_lo = (2.0 * np.pi / seq_len) * np.arange(lo)[:, None] * bands[None, :]
    tab = lambda a: jnp.asarray(a, F32)
    ch, sh = tab(np.cos(ang_hi))[:, None, :], tab(np.sin(ang_hi))[:, None, :]
    cl, sl = tab(np.cos(ang_lo))[None, :, :], tab(np.sin(ang_lo))[None, :, :]
    npos = seq_len + 1
    cos = (ch * cl - sh * sl).reshape(hi * lo, HY_POS_BANDS)[:npos]
    sin = (sh * cl + ch * sl).reshape(hi * lo, HY_POS_BANDS)[:npos]
    t = jnp.arange(npos, dtype=F32) / max(seq_len - 1, 1)
    feat = jnp.concatenate([t[:, None], cos, -sin,
                            jnp.zeros((npos, LANES - 1 - 2 * HY_POS_BANDS), F32)], axis=-1)
    return jnp.concatenate([feat, feat[1:seq_len][::-1]], axis=0)


def _hyena_taps(seq_len, w1, b1, w2, b2, w3, b3, freq):
    n = 2 * seq_len
    feat = _position_features(seq_len)
    w1p = jnp.concatenate([w1, jnp.zeros((LANES - w1.shape[0], w1.shape[1]), F32)], axis=0)
    deltas = jnp.abs(jnp.linspace(math.log(HY_DECAY_TARGET) / HY_LONG_DECAY_PCT,
                                  math.log(HY_DECAY_TARGET) / HY_SHORT_DECAY_PCT,
                                  HY_WIDTH, dtype=F32)).reshape(1, HY_WIDTH)
    tm = min(n, 1024)
    hid = w1.shape[1]
    const = lambda shape: pl.BlockSpec(shape, lambda i: (0, 0))
    return pl.pallas_call(
        functools.partial(_hy_filter_kernel, seq_len=seq_len),
        out_shape=(jax.ShapeDtypeStruct((n, HY_WIDTH), F32),
                   jax.ShapeDtypeStruct((8, HY_WIDTH), F32)),
        grid=(n // tm,),
        in_specs=[pl.BlockSpec((tm, LANES), lambda i: (i, 0)),
                  const((LANES, hid)), const((1, hid)), const((hid, hid)), const((1, hid)),
                  const((hid, 2 * HY_WIDTH)), const((1, 2 * HY_WIDTH)), const((1, hid)),
                  const((1, HY_WIDTH))],
        out_specs=(pl.BlockSpec((tm, HY_WIDTH), lambda i: (i, 0)), const((8, HY_WIDTH))),
        compiler_params=_cparams("arbitrary"),
        name="hyena_taps",
    )(feat, w1p, b1.reshape(1, hid), w2, b2.reshape(1, hid), w3, b3.reshape(1, 2 * HY_WIDTH),
      freq.reshape(1, hid), deltas)


FFT_N2 = 256
FFT_K1_PER_STEP = 4


def _dft_tables(n_total, n1_in):
    n1 = n_total // FFT_N2
    k1 = np.arange(n1)
    a = 2.0 * np.pi * np.outer(k1, np.arange(n1_in)) / n1
    lvl1 = np.concatenate([np.cos(a), -np.sin(a)], axis=0)
    n2 = np.arange(FFT_N2)
    th = 2.0 * np.pi * np.outer(n2, n2) / FFT_N2
    fc, fs = np.cos(th), np.sin(th)
    g = np.block([[fc, fs], [-fs, fc]])
    ph = 2.0 * np.pi * np.outer(k1, n2) / n_total
    tw = np.stack([np.cos(ph), np.sin(ph)])[..., None]
    return (jnp.asarray(lvl1, F32), jnp.asarray(g, F32), jnp.asarray(g.T, F32), jnp.asarray(tw, F32))


def _inverse_lvl1_table(n_total, n1_out):
    n1 = n_total // FFT_N2
    a = 2.0 * np.pi * np.outer(np.arange(n1_out), np.arange(n1)) / n1
    return jnp.asarray(np.concatenate([np.cos(a), -np.sin(a)], axis=1), F32)


def _lvl2_kernel(b_ref, tw_ref, gh_ref, gl_ref, *rest, inverse):
    for j in range(b_ref.shape[1]):
        c = tw_ref[0, j]
        s = tw_ref[1, j]
        br = b_ref[0, j]
        bi = b_ref[1, j]
        d = jnp.concatenate([c * br + s * bi, c * bi - s * br], axis=0)
        dh, dl = _split(d)
        x = _dot(gh_ref[...], dh) + _dot(gh_ref[...], dl) + _dot(gl_ref[...], dh)
        if not inverse:
            (o_ref,) = rest
            o_ref[0, j] = x[:FFT_N2]
            o_ref[1, j] = x[FFT_N2:]
            continue
        h_ref, gth_ref, gtl_ref, o_ref = rest
        xr, xi = x[:FFT_N2], x[FFT_N2:]
        hr, hi = h_ref[0, j], h_ref[1, j]
        y = jnp.concatenate([xr * hr - xi * hi, xr * hi + xi * hr], axis=0)
        yh, yl = _split(y)
        cc = _dot(gth_ref[...], yh) + _dot(gth_ref[...], yl) + _dot(gtl_ref[...], yh)
        cr, ci = cc[:FFT_N2], cc[FFT_N2:]
        o_ref[0, j] = c * cr - s * ci
        o_ref[1, j] = c * ci + s * cr


def _level2(b, tw, g, h=None, gt=None):
    _, n1, n2, ch = b.shape
    kb = FFT_K1_PER_STEP
    blk = pl.BlockSpec((2, kb, n2, ch), lambda i: (0, i, 0, 0))
    const = pl.BlockSpec((2 * n2, 2 * n2), lambda i: (0, 0))
    gh, gl = _split(g)
    in_specs = [blk, pl.BlockSpec((2, kb, n2, 1), lambda i: (0, i, 0, 0)), const, const]
    args = [b, tw, gh, gl]
    if h is not None:
        gth, gtl = _split(gt)
        in_specs += [blk, const, const]
        args += [h, gth, gtl]
    return pl.pallas_call(
        functools.partial(_lvl2_kernel, inverse=h is not None),
        out_shape=jax.ShapeDtypeStruct(b.shape, F32),
        grid=(n1 // kb,),
        in_specs=in_specs,
        out_specs=blk,
        compiler_params=_cparams("parallel"),
        name="hyena_fft_level2" + ("_conv" if h is not None else ""),
    )(*args)


def _hyena_long_conv(z, taps):
    seq_len, ch = z.shape
    n = 2 * seq_len
    n1 = n // FFT_N2
    lvl1_z, g, gt, tw = _dft_tables(n, n1 // 2)
    lvl1_t = _dft_tables(n, n1)[0]
    inv1 = _inverse_lvl1_table(n, n1 // 2)
    cols = FFT_N2 * ch
    tn = min(cols, 4096)
    hb = _mm3(lvl1_t, taps.reshape(n1, cols), tn, "hyena_fft_level1_taps")
    hspec = _level2(hb.reshape(2, n1, FFT_N2, ch), tw, g)
    zb = _mm3(lvl1_z, z.reshape(n1 // 2, cols), tn, "hyena_fft_level1_z")
    cspec = _level2(zb.reshape(2, n1, FFT_N2, ch), tw, g, hspec, gt)
    conv = _mm3(inv1, cspec.reshape(2 * n1, cols), tn, "hyena_fft_inverse_level1")
    return conv.reshape(seq_len, ch)


def _small_conv_kernel(z_ref, taps_ref, fc_ref, fs_ref, o_ref):
    seq_len = z_ref.shape[0]
    fc = fc_ref[...]
    fs = fs_ref[...]
    z = z_ref[...]
    taps = taps_ref[...]
    zr = _dot3(fc[:, :seq_len], z)
    zi = -_dot3(fs[:, :seq_len], z)
    hr = _dot3(fc, taps)
    hi = -_dot3(fs, taps)
    yr = zr * hr - zi * hi
    yi = zr * hi + zi * hr
    o_ref[...] = _dot3(fc[:seq_len, :], yr) - _dot3(fs[:seq_len, :], yi)


def _hyena_small_conv(z, taps):
    seq_len, ch = z.shape
    n = 2 * seq_len
    th = 2.0 * np.pi * np.outer(np.arange(n), np.arange(n)) / n
    full = lambda shape: pl.BlockSpec(shape, lambda: (0, 0))
    return pl.pallas_call(
        _small_conv_kernel,
        out_shape=jax.ShapeDtypeStruct((seq_len, ch), F32),
        in_specs=[full((seq_len, ch)), full((n, ch)), full((n, n)), full((n, n))],
        out_specs=full((seq_len, ch)),
        compiler_params=pltpu.CompilerParams(vmem_limit_bytes=VMEM_LIMIT),
        name="hyena_small_conv",
    )(z, taps, jnp.asarray(np.cos(th), F32), jnp.asarray(np.sin(th), F32))


def _hy_post_kernel(x0_ref, z_ref, conv_ref, asum_ref, bias_ref, y_ref, *, n_total):
    norm = jnp.sum(asum_ref[...], axis=0, keepdims=True) + EPS
    conv = conv_ref[...] * (1.0 / (n_total * norm))
    y_ref[...] = (x0_ref[...] * (conv + bias_ref[...] * z_ref[...])).astype(y_ref.dtype)


def _hyena_post(x0, z, conv, asum, bias):
    t = x0.shape[0]
    tm = min(t, 1024)
    row = pl.BlockSpec((tm, HY_WIDTH), lambda i: (i, 0))
    return pl.pallas_call(
        functools.partial(_hy_post_kernel, n_total=2 * t),
        out_shape=jax.ShapeDtypeStruct((t, HY_WIDTH), BF16),
        grid=(t // tm,),
        in_specs=[row, row, row, pl.BlockSpec((8, HY_WIDTH), lambda i: (0, 0)),
                  pl.BlockSpec((1, HY_WIDTH), lambda i: (0, 0))],
        out_specs=row,
        compiler_params=_cparams("parallel"),
        name="hyena_post",
    )(x0, z, conv, asum, bias.reshape(1, HY_WIDTH))


def _hyena_mixer(p_hy, conv_w, conv_b, filt, bias):
    seq_len = p_hy.shape[0]
    x0, z = _hyena_pre(p_hy, conv_w, conv_b)
    taps, asum = _hyena_taps(seq_len, *filt)
    if 2 * seq_len // FFT_N2 >= 16:
        conv = _hyena_long_conv(z, taps)
    else:
        conv = _hyena_small_conv(z, taps)
    return _hyena_post(x0, z, conv, asum, bias)


def _outproj_kernel(hy_ref, ret_ref, att_ref, w_ref, x_ref, gate_ref, o_ref):
    y = (_dot(hy_ref[...], w_ref[:HY_WIDTH, :])
         + _dot(ret_ref[...], w_ref[HY_WIDTH:HY_WIDTH + RET_WIDTH, :])
         + _dot(att_ref[...], w_ref[HY_WIDTH + RET_WIDTH:, :]))
    o_ref[...] = x_ref[...] + gate_ref[...] * y


def _out_projection(y_hy, y_ret, y_att, w_bf16, x, gate):
    t = x.shape[0]
    tm = min(t, 512)
    row = lambda w: pl.BlockSpec((tm, w), lambda i: (i, 0))
    return pl.pallas_call(
        _outproj_kernel,
        out_shape=jax.ShapeDtypeStruct((t, D_MODEL), F32),
        grid=(t // tm,),
        in_specs=[row(HY_WIDTH), row(RET_WIDTH), row(ATT_WIDTH),
                  pl.BlockSpec((MIX_WIDTH, D_MODEL), lambda i: (0, 0)),
                  row(D_MODEL), pl.BlockSpec((1, D_MODEL), lambda i: (0, 0))],
        out_specs=row(D_MODEL),
        compiler_params=_cparams("parallel"),
        name="out_projection",
    )(y_hy, y_ret, y_att, w_bf16, x, gate)


def _swiglu(x, wg_ref, wu_ref, wd_ref):
    hidden = wg_ref.shape[1]
    y = None
    for lo in range(0, hidden, SWIGLU_CHUNK):
        hi = min(lo + SWIGLU_CHUNK, hidden)
        a = _silu(_dot(x, wg_ref[:, lo:hi])) * _dot(x, wu_ref[:, lo:hi])
        part = _dot(a.astype(BF16), wd_ref[lo:hi, :])
        y = part if y is None else y + part
    return y


def _ffn_kernel(x_ref, g_ref, sh_ref, sc_ref, gate_ref, wg_ref, wu_ref, wd_ref, o_ref):
    x = x_ref[...]
    h = _norm_mod(x, g_ref[...], sh_ref[...], sc_ref[...]).astype(BF16)
    o_ref[...] = x + gate_ref[...] * _swiglu(h, wg_ref, wu_ref, wd_ref)


def _dense_ffn(x, g, shift, scale, gate, wg, wu, wd):
    t = x.shape[0]
    tm = min(t, 512)
    vec = pl.BlockSpec((1, D_MODEL), lambda i: (0, 0))
    resident = lambda shape: pl.BlockSpec(shape, lambda i: (0, 0), pipeline_mode=pl.Buffered(1))
    return pl.pallas_call(
        _ffn_kernel,
        out_shape=jax.ShapeDtypeStruct((t, D_MODEL), F32),
        grid=(t // tm,),
        in_specs=[pl.BlockSpec((tm, D_MODEL), lambda i: (i, 0)), vec, vec, vec, vec,
                  resident((D_MODEL, FFN_HIDDEN)), resident((D_MODEL, FFN_HIDDEN)),
                  resident((FFN_HIDDEN, D_MODEL))],
        out_specs=pl.BlockSpec((tm, D_MODEL), lambda i: (i, 0)),
        compiler_params=_cparams("parallel"),
        name="dense_ffn",
    )(x, g, shift, scale, gate, wg, wu, wd)


def _moe_route_kernel(x_ref, g_ref, sh_ref, sc_ref, rw_ref, rb_ref, h_ref, info_ref):
    h = _norm_mod(x_ref[...], g_ref[...], sh_ref[...], sc_ref[...])
    _to_row_tiles(h_ref, h)
    logits = _dot3(h, rw_ref[...]) + rb_ref[...]
    lane = lax.broadcasted_iota(jnp.int32, logits.shape, 1)
    v1 = jnp.max(logits, axis=-1, keepdims=True)
    i1 = jnp.min(jnp.where(logits == v1, lane, LANES), axis=-1, keepdims=True)
    rest = jnp.where(lane == i1, -jnp.inf, logits)
    v2 = jnp.max(rest, axis=-1, keepdims=True)
    i2 = jnp.min(jnp.where(rest == v2, lane, LANES), axis=-1, keepdims=True)
    e = jnp.exp(v2 - v1)
    g1 = 1.0 / (1.0 + e)
    g2 = e * g1
    info_ref[...] = jnp.where(lane == 0, i1.astype(F32),
                              jnp.where(lane == 1, i2.astype(F32),
                                        jnp.where(lane == 2, g1, jnp.where(lane == 3, g2, 0.0))))


def _moe_route(x, g, shift, scale, router_w, router_b):
    t = x.shape[0]
    tm = min(t, 512)
    rw = jnp.concatenate([router_w, jnp.zeros((D_MODEL, LANES - N_EXPERTS), F32)], axis=1)
    rb = jnp.concatenate([router_b, jnp.full((LANES - N_EXPERTS,), -jnp.inf, F32)]).reshape(1, LANES)
    vec = pl.BlockSpec((1, D_MODEL), lambda i: (0, 0))
    return pl.pallas_call(
        _moe_route_kernel,
        out_shape=(jax.ShapeDtypeStruct((t * SUBLANES, LANES), F32),
                   jax.ShapeDtypeStruct((t, LANES), F32)),
        grid=(t // tm,),
        in_specs=[pl.BlockSpec((tm, D_MODEL), lambda i: (i, 0)), vec, vec, vec,
                  pl.BlockSpec((D_MODEL, LANES), lambda i: (0, 0)),
                  pl.BlockSpec((1, LANES), lambda i: (0, 0))],
        out_specs=(pl.BlockSpec((tm * SUBLANES, LANES), lambda i: (i, 0)),
                   pl.BlockSpec((tm, LANES), lambda i: (i, 0))),
        compiler_params=_cparams("parallel"),
        name="moe_route",
    )(x, g, shift, scale, rw, rb)


def _to_row_tiles(ref, x):
    rows = x.shape[0]
    for k in range(D_MODEL // LANES):
        ref[pl.ds(k, rows, stride=SUBLANES), :] = x[:, k * LANES:(k + 1) * LANES]


def _from_row_tiles(ref, rows):
    return jnp.concatenate(
        [ref[pl.ds(k, rows, stride=SUBLANES), :] for k in range(D_MODEL // LANES)], axis=1)


def _row_copy(src_hbm, dst_vmem, sem, src_row, dst_row):
    src = pl.multiple_of(src_row * SUBLANES, SUBLANES)
    dst = pl.multiple_of(dst_row * SUBLANES, SUBLANES)
    return pltpu.make_async_copy(src_hbm.at[pl.ds(src, SUBLANES)], dst_vmem.at[pl.ds(dst, SUBLANES)], sem)


def _moe_expert_kernel(blk_e_ref, nused_ref, src_ref, h_hbm, wg_ref, wu_ref, wd_ref, y_ref,
                       xbuf_ref, sem):
    b = pl.program_id(0)
    n_used = nused_ref[0]

    def issue_rows(blk, slot):
        def body(r, carry):
            _row_copy(h_hbm, xbuf_ref.at[slot], sem.at[slot], src_ref[blk * MOE_BLOCK + r], r).start()
            return carry

        lax.fori_loop(0, MOE_BLOCK, body, 0, unroll=DMA_UNROLL)

    def wait_rows(slot):
        def body(r, carry):
            _row_copy(h_hbm, xbuf_ref.at[slot], sem.at[slot], 0, r).wait()
            return carry

        lax.fori_loop(0, MOE_BLOCK, body, 0, unroll=DMA_UNROLL)

    @pl.when(b < n_used)
    def _():
        slot = b % 2

        @pl.when(b == 0)
        def _():
            issue_rows(0, 0)

        wait_rows(slot)

        @pl.when(b + 1 < n_used)
        def _():
            issue_rows(b + 1, 1 - slot)

        x = _from_row_tiles(xbuf_ref.at[slot], MOE_BLOCK).astype(BF16)
        _to_row_tiles(y_ref, _swiglu(x, wg_ref.at[0], wu_ref.at[0], wd_ref.at[0]))

    @pl.when(b >= n_used)
    def _():
        y_ref[...] = jnp.zeros_like(y_ref)


def _moe_experts(h, blk_expert, n_used, src_tok, wg, wu, wd, n_blk):
    def w_index(b, be, nu, st):
        return be[jnp.minimum(b, nu[0] - 1)], 0, 0

    grid_spec = pltpu.PrefetchScalarGridSpec(
        num_scalar_prefetch=3,
        grid=(n_blk,),
        in_specs=[pl.BlockSpec(memory_space=pl.ANY),
                  pl.BlockSpec((1, D_MODEL, FFN_HIDDEN), w_index),
                  pl.BlockSpec((1, D_MODEL, FFN_HIDDEN), w_index),
                  pl.BlockSpec((1, FFN_HIDDEN, D_MODEL), w_index)],
        out_specs=pl.BlockSpec((MOE_BLOCK * SUBLANES, LANES), lambda b, be, nu, st: (b, 0)),
        scratch_shapes=[pltpu.VMEM((2, MOE_BLOCK * SUBLANES, LANES), F32),
                        pltpu.SemaphoreType.DMA((2,))],
    )
    return pl.pallas_call(
        _moe_expert_kernel,
        out_shape=jax.ShapeDtypeStruct((n_blk * MOE_BLOCK * SUBLANES, LANES), F32),
        grid_spec=grid_spec,
        compiler_params=_cparams("arbitrary"),
        name="moe_experts",
    )(blk_expert, n_used, src_tok, h, wg, wu, wd)


def _moe_combine_kernel(dest_ref, x_ref, info_ref, gate_ref, ng_ref, y_hbm, o_ref,
                        y1_ref, y2_ref, sem, *, final_norm):
    i = pl.program_id(0)
    tm = x_ref.shape[0]
    slot = i % 2

    def issue_rows(blk, slot):
        def body(r, carry):
            a = 2 * (blk * tm + r)
            _row_copy(y_hbm, y1_ref.at[slot], sem.at[slot], dest_ref[a], r).start()
            _row_copy(y_hbm, y2_ref.at[slot], sem.at[slot], dest_ref[a + 1], r).start()
            return carry

        lax.fori_loop(0, tm, body, 0, unroll=DMA_UNROLL)

    def wait_rows(slot):
        def body(r, carry):
            _row_copy(y_hbm, y1_ref.at[slot], sem.at[slot], 0, r).wait()
            _row_copy(y_hbm, y2_ref.at[slot], sem.at[slot], 0, r).wait()
            return carry

        lax.fori_loop(0, tm, body, 0, unroll=DMA_UNROLL)

    @pl.when(i == 0)
    def _():
        issue_rows(0, 0)

    wait_rows(slot)

    @pl.when(i + 1 < pl.num_programs(0))
    def _():
        issue_rows(i + 1, 1 - slot)

    info = info_ref[...]
    y = (_from_row_tiles(y1_ref.at[slot], tm) * info[:, 2:3]
         + _from_row_tiles(y2_ref.at[slot], tm) * info[:, 3:4])
    x = x_ref[...] + gate_ref[...] * y
    if final_norm:
        ms = jnp.mean(x * x, axis=-1, keepdims=True)
        x = x * lax.rsqrt(ms + EPS) * ng_ref[...]
    o_ref[...] = x


def _moe_combine(dest, x, info, gate, norm_g, y, final_norm):
    t = x.shape[0]
    tm = min(t, 256)
    vec = pl.BlockSpec((1, D_MODEL), lambda i, d: (0, 0))
    grid_spec = pltpu.PrefetchScalarGridSpec(
        num_scalar_prefetch=1,
        grid=(t // tm,),
        in_specs=[pl.BlockSpec((tm, D_MODEL), lambda i, d: (i, 0)),
                  pl.BlockSpec((tm, LANES), lambda i, d: (i, 0)), vec, vec,
                  pl.BlockSpec(memory_space=pl.ANY)],
        out_specs=pl.BlockSpec((tm, D_MODEL), lambda i, d: (i, 0)),
        scratch_shapes=[pltpu.VMEM((2, tm * SUBLANES, LANES), F32),
                        pltpu.VMEM((2, tm * SUBLANES, LANES), F32),
                        pltpu.SemaphoreType.DMA((2,))],
    )
    return pl.pallas_call(
        functools.partial(_moe_combine_kernel, final_norm=final_norm),
        out_shape=jax.ShapeDtypeStruct((t, D_MODEL), F32),
        grid_spec=grid_spec,
        compiler_params=_cparams("arbitrary"),
        name="moe_combine",
    )(dest, x, info, gate, norm_g, y)


def _moe_layer(x, g, shift, scale, gate, router_w, router_b, wg, wu, wd, norm_g, final_norm):
    t = x.shape[0]
    n_asg = t * TOP_K
    h, info = _moe_route(x, g, shift, scale, router_w, router_b)
    expert = info[:, :TOP_K].astype(jnp.int32).reshape(-1)
    onehot = (expert[:, None] == jnp.arange(N_EXPERTS, dtype=jnp.int32)[None, :]).astype(jnp.int32)
    csum = jnp.cumsum(onehot, axis=0)
    counts = csum[-1]
    padded = (counts + MOE_BLOCK - 1) // MOE_BLOCK * MOE_BLOCK
    pad_end = jnp.cumsum(padded)
    pad_start = pad_end - padded
    dest = jnp.sum(onehot * (csum - 1 + pad_start[None, :]), axis=1).astype(jnp.int32)
    n_blk = -(-n_asg // MOE_BLOCK) + N_EXPERTS
    blk_start = jnp.arange(n_blk, dtype=jnp.int32) * MOE_BLOCK
    blk_expert = jnp.minimum(jnp.sum(blk_start[:, None] >= pad_end[None, :], axis=1),
                             N_EXPERTS - 1).astype(jnp.int32)
    n_used = (pad_end[-1:] // MOE_BLOCK).astype(jnp.int32)
    src_tok = jnp.zeros((n_blk * MOE_BLOCK,), jnp.int32).at[dest].set(
        jnp.arange(n_asg, dtype=jnp.int32) // TOP_K)
    y = _moe_experts(h, blk_expert, n_used, src_tok, wg, wu, wd, n_blk)
    return _moe_combine(dest, x, info, gate, norm_g, y, final_norm)


def _final_norm_kernel(x_ref, g_ref, o_ref):
    x = x_ref[...]
    ms = jnp.mean(x * x, axis=-1, keepdims=True)
    o_ref[...] = x * lax.rsqrt(ms + EPS) * g_ref[...]


def _final_norm(x, g):
    t = x.shape[0]
    tm = min(t, 1024)
    return pl.pallas_call(
        _final_norm_kernel,
        out_shape=jax.ShapeDtypeStruct((t, D_MODEL), F32),
        grid=(t // tm,),
        in_specs=[pl.BlockSpec((tm, D_MODEL), lambda i: (i, 0)),
                  pl.BlockSpec((1, D_MODEL), lambda i: (0, 0))],
        out_specs=pl.BlockSpec((tm, D_MODEL), lambda i: (i, 0)),
        compiler_params=_cparams("parallel"),
        name="final_norm",
    )(x, g)


def kernel(x, c, ctx, c_ctx, ada_w, ada_b, norm1_g, norm2_g, w_in, w_out, hy_conv_w, hy_conv_b, hy_filt_w1, hy_filt_b1, hy_filt_w2, hy_filt_b2, hy_filt_w3, hy_filt_b3, hy_filt_freq, hy_bias, ret_log_rate, attn_q_g, attn_k_g, ffn_w_gate, ffn_w_up, ffn_w_down, moe_router_w, moe_router_b, moe_w_gate, moe_w_up, moe_w_down, final_norm_g):
    assert x.shape[0] == 1 and c.shape[0] == 1
    seq_len = x.shape[1]
    x_lat = x[0]
    x_ctx = ctx[0]
    cos_t, sin_t = _rope_tables(seq_len)
    cvec = jnp.concatenate([c, c_ctx[None, :], jnp.zeros((6, D_MODEL), F32)], axis=0)
    cvec = _silu(cvec)
    row = lambda v: v.reshape(1, D_MODEL)
    zero_state = jnp.zeros((2, RET_WIDTH, RET_HEAD_DIM), F32)
    zero_tab = jnp.zeros((x_ctx.shape[0], LANES), F32)

    for l in range(DEPTH):
        last = l == DEPTH - 1
        mods = _ada_modulation(cvec, ada_w, ada_b, l).reshape(8, 6, D_MODEL)
        mod = [row(mods[0, i]) for i in range(6)]
        mod_c = [row(mods[1, i]) for i in range(6)]
        filt = (hy_filt_w1[l], hy_filt_b1[l], hy_filt_w2[l], hy_filt_b2[l],
                hy_filt_w3[l], hy_filt_b3[l], hy_filt_freq[l])
        lg_f = -jnp.exp(ret_log_rate[l, 0].astype(F32))
        lg_b = -jnp.exp(ret_log_rate[l, 1].astype(F32))
        mask, zeta, xi, chunk_decay = _ret_tables(lg_f, lg_b)
        w_in_l = w_in[l].astype(BF16)
        w_out_l = w_out[l].astype(BF16)
        n1 = row(norm1_g[l])
        n2 = row(norm2_g[l])

        p_hy, p_ret, p_att = _in_projection(x_lat, n1, mod[0], mod[1], w_in_l)
        pc_hy, pc_ret, pc_att = _in_projection(x_ctx, n1, mod_c[0], mod_c[1], w_in_l)

        sprev_c, s_ctx = _retention_states(pc_ret, zeta, chunk_decay, zero_state)
        qct, kc, vct, _ = _attention_prep(pc_att, zero_tab, zero_tab, attn_q_g[l], attn_k_g[l],
                                          rope=False)

        y_hy = _hyena_mixer(p_hy, hy_conv_w[l], hy_conv_b[l], filt, hy_bias[l])
        sprev, _ = _retention_states(p_ret, zeta, chunk_decay, s_ctx)
        y_ret = _retention_outputs(p_ret, mask, xi, sprev)
        qt, k_aug, vt, kmax_sq = _attention_prep(p_att, cos_t, sin_t, attn_q_g[l], attn_k_g[l],
                                                 rope=True)
        kmax = jnp.sqrt(jnp.max(kmax_sq[:, 0, ::ATT_HEAD_DIM], axis=0)) * KMAX_SLACK
        y_att = _attention(qt, kc, vct, kmax, k_aug, vt, tq=ATT_TQ, tk=ATT_TK, sub=ATT_SUB)

        if not last:
            yc_hy = _hyena_mixer(pc_hy, hy_conv_w[l], hy_conv_b[l], filt, hy_bias[l])
            yc_ret = _retention_outputs(pc_ret, mask, xi, sprev_c)
            yc_att = _attention(qct, kc, vct, tq=x_ctx.shape[0])
            x_ctx = _out_projection(yc_hy, yc_ret, yc_att, w_out_l, x_ctx, mod_c[2])
        x_lat = _out_projection(y_hy, y_ret, y_att, w_out_l, x_lat, mod[2])

        i = l // 2
        if l % 2 == 0:
            wg, wu, wd = (ffn_w_gate[i].astype(BF16), ffn_w_up[i].astype(BF16),
                          ffn_w_down[i].astype(BF16))
            x_lat = _dense_ffn(x_lat, n2, mod[3], mod[4], mod[5], wg, wu, wd)
            if not last:
                x_ctx = _dense_ffn(x_ctx, n2, mod_c[3], mod_c[4], mod_c[5], wg, wu, wd)
        else:
            wg, wu, wd = (moe_w_gate[i].astype(BF16), moe_w_up[i].astype(BF16),
                          moe_w_down[i].astype(BF16))
            fg = row(final_norm_g)
            x_lat = _moe_layer(x_lat, n2, mod[3], mod[4], mod[5], moe_router_w[i], moe_router_b[i],
                               wg, wu, wd, fg, final_norm=last)
            if not last:
                x_ctx = _moe_layer(x_ctx, n2, mod_c[3], mod_c[4], mod_c[5], moe_router_w[i],
                                   moe_router_b[i], wg, wu, wd, fg, final_norm=False)
    if DEPTH % 2 == 1:
        x_lat = _final_norm(x_lat, row(final_norm_g))
    return x_lat[None]
```

```python
import functools
import math

import numpy as np
import jax
import jax.numpy as jnp
from jax import lax
from jax.experimental import pallas as pl
from jax.experimental.pallas import tpu as pltpu

F32 = jnp.float32
BF16 = jnp.bfloat16

D_MODEL = 1024
DEPTH = 2
GRID_W = 64
EPS = 1e-6

HY_WIDTH = 256
HY_COLS = 3 * HY_WIDTH
HY_POS_BANDS = 16
HY_DECAY_TARGET = 1e-2
HY_SHORT_DECAY_PCT = 0.3
HY_LONG_DECAY_PCT = 1.5

RET_HEAD_DIM = 64
RET_HEADS = 4
RET_WIDTH = RET_HEADS * RET_HEAD_DIM
RET_COLS = 4 * RET_WIDTH
RET_CHUNK = 128

ATT_HEAD_DIM = 64
ATT_HEADS = 8
ATT_KV_HEADS = 2
ATT_GROUP = ATT_HEADS // ATT_KV_HEADS
ATT_WIDTH = ATT_HEADS * ATT_HEAD_DIM
ATT_KV_WIDTH = ATT_KV_HEADS * ATT_HEAD_DIM
ATT_COLS = ATT_WIDTH + 2 * ATT_KV_WIDTH
ROPE_AXIS_DIM = ATT_HEAD_DIM // 2
ROPE_BASE = 10000.0

MIX_WIDTH = HY_WIDTH + RET_WIDTH + ATT_WIDTH
IN_COLS = HY_COLS + RET_COLS + ATT_COLS

FFN_HIDDEN = 2816
N_EXPERTS = 8
TOP_K = 2
MOE_BLOCK = 512

LOG2_E = 1.4426950408889634
EXP2_CAP = 60.0
KMAX_SLACK = 1.0 + 2.0 ** -7
ATT_TQ = 256
ATT_TK = 8192
ATT_SUB = 1024
ATT_VROWS = 80
RET_STATE_CHUNKS = 8
RET_OUT_CHUNKS = 4
SWIGLU_CHUNK = 1024
DMA_UNROLL = 8

LANES = 128
SUBLANES = 8
VMEM_LIMIT = 56 * 1024 * 1024


def _cparams(*sem):
    return pltpu.CompilerParams(dimension_semantics=sem, vmem_limit_bytes=VMEM_LIMIT)


def _dot(a, b):
    return jnp.dot(a, b, preferred_element_type=F32)


def _split(a):
    hi = a.astype(BF16)
    lo = (a - hi.astype(F32)).astype(BF16)
    return hi, lo


def _dot3(a, b):
    ah, al = _split(a)
    bh, bl = _split(b)
    return _dot(ah, bh) + _dot(al, bh) + _dot(ah, bl)


def _silu(x):
    return x * (1.0 / (1.0 + jnp.exp(-x)))


def _norm_mod(x, g, shift, scale):
    ms = jnp.mean(x * x, axis=-1, keepdims=True)
    h = x * lax.rsqrt(ms + EPS) * g
    return h * (1.0 + scale) + shift


def _mm3_kernel(a_ref, b_ref, o_ref):
    o_ref[...] = _dot3(a_ref[...], b_ref[...])


def _mm3(a, b, tn, name):
    m, k = a.shape
    n = b.shape[1]
    return pl.pallas_call(
        _mm3_kernel,
        out_shape=jax.ShapeDtypeStruct((m, n), F32),
        grid=(n // tn,),
        in_specs=[pl.BlockSpec((m, k), lambda j: (0, 0)),
                  pl.BlockSpec((k, tn), lambda j: (0, j))],
        out_specs=pl.BlockSpec((m, tn), lambda j: (0, j)),
        compiler_params=_cparams("parallel"),
        name=name,
    )(a, b)


def _ada_kernel(c_ref, w_ref, b_ref, o_ref):
    o_ref[...] = _dot3(c_ref[...], w_ref[0]) + b_ref[0]


def _ada_modulation(cc, w, b, layer):
    n = w.shape[2]
    tn = 1536
    return pl.pallas_call(
        _ada_kernel,
        out_shape=jax.ShapeDtypeStruct((8, n), F32),
        grid=(n // tn,),
        in_specs=[pl.BlockSpec((8, D_MODEL), lambda j: (0, 0)),
                  pl.BlockSpec((1, D_MODEL, tn), lambda j: (layer, 0, j)),
                  pl.BlockSpec((1, 1, tn), lambda j: (layer, 0, j))],
        out_specs=pl.BlockSpec((8, tn), lambda j: (0, j)),
        compiler_params=_cparams("parallel"),
        name="ada_modulation",
    )(cc, w, b.reshape(b.shape[0], 1, n))


def _inproj_kernel(x_ref, g_ref, sh_ref, sc_ref, w_ref, hy_ref, ret_ref, att_ref):
    h = _norm_mod(x_ref[...], g_ref[...], sh_ref[...], sc_ref[...])
    p = _dot(h.astype(BF16), w_ref[...])
    hy_ref[...] = p[:, :HY_COLS]
    ret_ref[...] = p[:, HY_COLS:HY_COLS + RET_COLS]
    att_ref[...] = p[:, HY_COLS + RET_COLS:]


def _in_projection(x, g, shift, scale, w_bf16):
    t = x.shape[0]
    tm = min(t, 512)
    vec = pl.BlockSpec((1, D_MODEL), lambda i: (0, 0))
    return pl.pallas_call(
        _inproj_kernel,
        out_shape=(jax.ShapeDtypeStruct((t, HY_COLS), F32),
                   jax.ShapeDtypeStruct((t, RET_COLS), F32),
                   jax.ShapeDtypeStruct((t, ATT_COLS), F32)),
        grid=(t // tm,),
        in_specs=[pl.BlockSpec((tm, D_MODEL), lambda i: (i, 0)), vec, vec, vec,
                  pl.BlockSpec((D_MODEL, IN_COLS), lambda i: (0, 0))],
        out_specs=(pl.BlockSpec((tm, HY_COLS), lambda i: (i, 0)),
                   pl.BlockSpec((tm, RET_COLS), lambda i: (i, 0)),
                   pl.BlockSpec((tm, ATT_COLS), lambda i: (i, 0))),
        compiler_params=_cparams("parallel"),
        name="in_projection",
    )(x, g, shift, scale, w_bf16)


def _head_mean_matrix(width):
    idx = np.arange(width) // ATT_HEAD_DIM
    return jnp.asarray((idx[:, None] == idx[None, :]).astype(np.float32) / ATT_HEAD_DIM, BF16)


def _head_rms(x, bd, g):
    x2h, x2l = _split(x * x)
    ms = _dot(x2h, bd) + _dot(x2l, bd)
    return x * lax.rsqrt(ms + EPS) * g


def _rope(x, cos, sin):
    n = x.shape[1]
    lane = lax.broadcasted_iota(jnp.int32, x.shape, 1)
    swapped = jnp.where((lane % 32) < 16, pltpu.roll(x, n - 16, 1), pltpu.roll(x, 16, 1))
    return x * cos + swapped * sin


def _attprep_kernel(p_ref, cos_ref, sin_ref, qg_ref, kg_ref, bdq_ref, bdk_ref,
                    qt_ref, k_ref, vt_ref, kmax_ref, *, rope):
    p = p_ref[...]
    q = _head_rms(p[:, :ATT_WIDTH], bdq_ref[...], qg_ref[...])
    k = _head_rms(p[:, ATT_WIDTH:ATT_WIDTH + ATT_KV_WIDTH], bdk_ref[...], kg_ref[...])
    if rope:
        cos = cos_ref[...]
        sin = sin_ref[...]
        q = jnp.concatenate(
            [_rope(q[:, j * LANES:(j + 1) * LANES], cos, sin) for j in range(ATT_WIDTH // LANES)],
            axis=1)
        k = _rope(k, cos, sin)
    qt_ref[...] = (q * (ATT_HEAD_DIM ** -0.5 * LOG2_E)).T.astype(BF16)
    tm = k.shape[0]
    lane = lax.broadcasted_iota(jnp.int32, k.shape, 1)
    minus_one_col = jnp.where(lane == ATT_HEAD_DIM, -1.0, 0.0)
    kb = k.astype(BF16)
    k_ref[0] = jnp.where(lane < ATT_HEAD_DIM, k, minus_one_col).astype(BF16)
    k_ref[1] = jnp.where(lane < ATT_HEAD_DIM, pltpu.roll(k, ATT_HEAD_DIM, 1), minus_one_col).astype(BF16)
    sq_hi, sq_lo = _split(kb.astype(F32) ** 2)
    norm_sq = (_dot(sq_hi, bdk_ref[...]) + _dot(sq_lo, bdk_ref[...])) * ATT_HEAD_DIM
    kmax_ref[0] = jnp.broadcast_to(jnp.max(norm_sq, axis=0, keepdims=True), (8, LANES))
    vt = p[:, ATT_WIDTH + ATT_KV_WIDTH:].T.astype(BF16)
    sub = lax.broadcasted_iota(jnp.int32, (ATT_VROWS - ATT_HEAD_DIM, tm), 0)
    ones_row = jnp.where(sub == 0, 1.0, 0.0).astype(BF16)
    for h in range(ATT_KV_HEADS):
        vt_ref[h * ATT_VROWS:h * ATT_VROWS + ATT_HEAD_DIM, :] = vt[h * ATT_HEAD_DIM:(h + 1) * ATT_HEAD_DIM]
        vt_ref[h * ATT_VROWS + ATT_HEAD_DIM:(h + 1) * ATT_VROWS, :] = ones_row


def _attention_prep(p_att, cos, sin, q_g, k_g, rope):
    t = p_att.shape[0]
    tm = min(t, 512)
    qg = jnp.tile(q_g, ATT_HEADS).reshape(1, ATT_WIDTH)
    kg = jnp.tile(k_g, ATT_KV_HEADS).reshape(1, ATT_KV_WIDTH)
    const = lambda shape: pl.BlockSpec(shape, lambda i: (0, 0))
    return pl.pallas_call(
        functools.partial(_attprep_kernel, rope=rope),
        out_shape=(jax.ShapeDtypeStruct((ATT_WIDTH, t), BF16),
                   jax.ShapeDtypeStruct((ATT_KV_HEADS, t, LANES), BF16),
                   jax.ShapeDtypeStruct((ATT_KV_HEADS * ATT_VROWS, t), BF16),
                   jax.ShapeDtypeStruct((t // tm, 8, LANES), F32)),
        grid=(t // tm,),
        in_specs=[pl.BlockSpec((tm, ATT_COLS), lambda i: (i, 0)),
                  pl.BlockSpec((tm, LANES), lambda i: (i, 0)),
                  pl.BlockSpec((tm, LANES), lambda i: (i, 0)),
                  const((1, ATT_WIDTH)), const((1, ATT_KV_WIDTH)),
                  const((ATT_WIDTH, ATT_WIDTH)), const((ATT_KV_WIDTH, ATT_KV_WIDTH))],
        out_specs=(pl.BlockSpec((ATT_WIDTH, tm), lambda i: (0, i)),
                   pl.BlockSpec((ATT_KV_HEADS, tm, LANES), lambda i: (0, i, 0)),
                   pl.BlockSpec((ATT_KV_HEADS * ATT_VROWS, tm), lambda i: (0, i)),
                   pl.BlockSpec((1, 8, LANES), lambda i: (i, 0, 0))),
        compiler_params=_cparams("parallel"),
        name="attention_prep",
    )(p_att, cos, sin, qg, kg, _head_mean_matrix(ATT_WIDTH), _head_mean_matrix(ATT_KV_WIDTH))


def _rope_tables(n_tokens):
    rows = n_tokens // GRID_W
    inv_freq = ROPE_BASE ** (-jnp.arange(0, ROPE_AXIS_DIM, 2, dtype=F32) / ROPE_AXIS_DIM)
    ang_r = jnp.arange(rows, dtype=F32)[:, None] * inv_freq[None, :]
    ang_c = jnp.arange(GRID_W, dtype=F32)[:, None] * inv_freq[None, :]
    nf = inv_freq.shape[0]
    by_row = lambda tab: jnp.broadcast_to(tab[:, None, :], (rows, GRID_W, nf)).reshape(n_tokens, nf)
    by_col = lambda tab: jnp.broadcast_to(tab[None, :, :], (rows, GRID_W, nf)).reshape(n_tokens, nf)
    cr, sr = by_row(jnp.cos(ang_r)), by_row(jnp.sin(ang_r))
    cc, sc = by_col(jnp.cos(ang_c)), by_col(jnp.sin(ang_c))
    cos = jnp.concatenate([cr, cr, cc, cc], axis=1)
    sin = jnp.concatenate([-sr, sr, -sc, sc], axis=1)
    return jnp.tile(cos, (1, 2)), jnp.tile(sin, (1, 2))


def _attention_kernel(*refs, tq, tk, sub, n_lat):
    if n_lat:
        kmax_ref, qt_ref, kc_ref, vct_ref, k_ref, vt_ref, o_ref, qs_ref, acc_ref, m_ref = refs
    else:
        qt_ref, kc_ref, vct_ref, o_ref, acc_ref = refs
    cols = ATT_GROUP * tq
    qt = qt_ref[...]
    qst = jnp.concatenate(
        [qt[g * ATT_HEAD_DIM:(g + 1) * ATT_HEAD_DIM, :] for g in range(ATT_GROUP)], axis=1)
    pad = jnp.zeros((LANES - ATT_HEAD_DIM, cols), BF16)
    s = _dot(kc_ref[0], jnp.concatenate([qst, pad], axis=0))
    mt = jnp.max(s, axis=0, keepdims=True).astype(BF16)
    mtf = mt.astype(F32)
    acc_ref[...] = _dot(vct_ref[...], jnp.exp2(s - mtf).astype(BF16))

    if n_lat:
        row = lax.broadcasted_iota(jnp.int32, pad.shape, 0)
        qs_ref[...] = jnp.concatenate([qst, jnp.where(row == 0, mtf, 0.0).astype(BF16)], axis=0)
        qf = qst.astype(F32)
        qn = jnp.sqrt(jnp.sum(qf * qf, axis=0, keepdims=True))
        fast = jnp.max(qn * kmax_ref[pl.program_id(0)] - mtf) <= EXP2_CAP

        def tile(j):
            off = pl.multiple_of(j * tk, tk)
            return k_ref[0, pl.ds(off, tk), :], vt_ref[:, pl.ds(off, tk)]

        @pl.when(fast)
        def _():
            def body(j, carry):
                k, vt = tile(j)
                qs = qs_ref[...]
                part = None
                for c in range(tk // sub):
                    p = jnp.exp2(_dot(k[c * sub:(c + 1) * sub], qs)).astype(BF16)
                    pv = _dot(vt[:, c * sub:(c + 1) * sub], p)
                    part = pv if part is None else part + pv
                acc_ref[...] += part
                return carry

            lax.fori_loop(0, n_lat, body, 0)

        @pl.when(jnp.logical_not(fast))
        def _():
            m_ref[...] = jnp.zeros_like(m_ref)

            def body(j, carry):
                k, vt = tile(j)
                s = _dot(k, qs_ref[...])
                m_old = m_ref[...]
                m_new = jnp.maximum(m_old, jnp.max(s, axis=0, keepdims=True))
                p = jnp.exp2(s - m_new).astype(BF16)
                acc_ref[...] = jnp.exp2(m_old - m_new) * acc_ref[...] + _dot(vt, p)
                m_ref[...] = m_new
                return carry

            lax.fori_loop(0, n_lat, body, 0)

    acc = acc_ref[...]
    o = acc[:ATT_HEAD_DIM] / acc[ATT_HEAD_DIM:ATT_HEAD_DIM + 1]
    o_ref[...] = jnp.concatenate(
        [o[:, g * tq:(g + 1) * tq].T for g in range(ATT_GROUP)], axis=1).astype(o_ref.dtype)


def _attention(qt, kc, vct, kmax=None, k=None, vt=None, *, tq, tk=0, sub=0):
    t = qt.shape[1]
    lc = kc.shape[1]
    gw = ATT_GROUP * ATT_HEAD_DIM
    cols = ATT_GROUP * tq
    in_specs = [pl.BlockSpec((gw, tq), lambda h, i: (h, i)),
                pl.BlockSpec((1, lc, LANES), lambda h, i: (h, 0, 0)),
                pl.BlockSpec((ATT_VROWS, lc), lambda h, i: (h, 0))]
    args = [qt, kc, vct]
    scratch = [pltpu.VMEM((ATT_VROWS, cols), F32)]
    n_lat = 0
    if k is not None:
        lk = k.shape[1]
        n_lat = lk // tk
        in_specs = ([pl.BlockSpec(memory_space=pltpu.SMEM)] + in_specs
                    + [pl.BlockSpec((1, lk, LANES), lambda h, i: (h, 0, 0)),
                       pl.BlockSpec((ATT_VROWS, lk), lambda h, i: (h, 0))])
        args = [kmax] + args + [k, vt]
        scratch = [pltpu.VMEM((LANES, cols), BF16)] + scratch + [pltpu.VMEM((1, cols), F32)]
    return pl.pallas_call(
        functools.partial(_attention_kernel, tq=tq, tk=tk, sub=sub, n_lat=n_lat),
        out_shape=jax.ShapeDtypeStruct((t, ATT_WIDTH), BF16),
        grid=(ATT_KV_HEADS, t // tq),
        in_specs=in_specs,
        out_specs=pl.BlockSpec((tq, gw), lambda h, i: (i, h)),
        scratch_shapes=scratch,
        compiler_params=_cparams("parallel", "parallel"),
        name="attention_lat" if n_lat else "attention_ctx",
    )(*args)


def _ret_tables(lg_f, lg_b):
    pos = jnp.arange(RET_CHUNK, dtype=F32)
    rel = pos[:, None] - pos[None, :]
    mask = jnp.where(rel > 0, jnp.exp(lg_f[:, None, None] * jnp.maximum(rel, 0.0)),
                     jnp.where(rel < 0, jnp.exp(lg_b[:, None, None] * jnp.maximum(-rel, 0.0)), 2.0))
    zeta = jnp.stack([jnp.exp(lg_f[:, None] * (RET_CHUNK - 1 - pos)[None, :]),
                      jnp.exp(lg_b[:, None] * pos[None, :])])
    xi = jnp.stack([jnp.exp(lg_f[:, None] * (pos + 1)[None, :]),
                    jnp.exp(lg_b[:, None] * (RET_CHUNK - pos)[None, :])])
    chunk_decay = jnp.stack([jnp.exp(lg_f * RET_CHUNK), jnp.exp(lg_b * RET_CHUNK)])
    zeta = jnp.repeat(zeta, RET_HEAD_DIM, axis=1).transpose(0, 2, 1)
    xi = jnp.repeat(xi, RET_HEAD_DIM, axis=1).transpose(0, 2, 1)
    chunk_decay = jnp.repeat(chunk_decay, RET_HEAD_DIM, axis=1)[:, :, None]
    return mask, zeta, xi, chunk_decay


def _ret_state_kernel(k_ref, v_ref, zeta_ref, cd_ref, s0_ref, sprev_ref, sfin_ref, s_ref, *, cps):
    d = pl.program_id(0)

    @pl.when(pl.program_id(1) == 0)
    def _():
        s_ref[...] = s0_ref[0]

    zeta = zeta_ref[0] * (RET_HEAD_DIM ** -0.5)
    cd = cd_ref[0]
    us = []
    for c in range(cps):
        rows = slice(c * RET_CHUNK, (c + 1) * RET_CHUNK)
        kz = (k_ref[rows, :] * zeta).astype(BF16)
        v = v_ref[rows, :].astype(BF16)
        heads = []
        for h in range(RET_HEADS):
            sl = slice(h * RET_HEAD_DIM, (h + 1) * RET_HEAD_DIM)
            heads.append(lax.dot_general(kz[:, sl], v[:, sl], (((0,), (0,)), ((), ())),
                                         preferred_element_type=F32))
        us.append(jnp.concatenate(heads, axis=0))

    def scan(order):
        s = s_ref[...]
        for c in order:
            sprev_ref[0, c] = s
            s = cd * s + us[c]
        s_ref[...] = s
        sfin_ref[0] = s

    @pl.when(d == 0)
    def _():
        scan(range(cps))

    @pl.when(d == 1)
    def _():
        scan(reversed(range(cps)))


def _retention_states(p_ret, zeta, chunk_decay, s0):
    nc = p_ret.shape[0] // RET_CHUNK
    cps = min(nc, RET_STATE_CHUNKS)
    ng = nc // cps
    rows = cps * RET_CHUNK

    def group(d, i):
        return i + d * (ng - 1 - 2 * i)

    return pl.pallas_call(
        functools.partial(_ret_state_kernel, cps=cps),
        out_shape=(jax.ShapeDtypeStruct((2, nc, RET_WIDTH, RET_HEAD_DIM), F32),
                   jax.ShapeDtypeStruct((2, RET_WIDTH, RET_HEAD_DIM), F32)),
        grid=(2, ng),
        in_specs=[pl.BlockSpec((rows, RET_WIDTH), lambda d, i: (group(d, i), 1)),
                  pl.BlockSpec((rows, RET_WIDTH), lambda d, i: (group(d, i), 2)),
                  pl.BlockSpec((1, RET_CHUNK, RET_WIDTH), lambda d, i: (d, 0, 0)),
                  pl.BlockSpec((1, RET_WIDTH, 1), lambda d, i: (d, 0, 0)),
                  pl.BlockSpec((1, RET_WIDTH, RET_HEAD_DIM), lambda d, i: (d, 0, 0))],
        out_specs=(pl.BlockSpec((1, cps, RET_WIDTH, RET_HEAD_DIM), lambda d, i: (d, group(d, i), 0, 0)),
                   pl.BlockSpec((1, RET_WIDTH, RET_HEAD_DIM), lambda d, i: (d, 0, 0))),
        scratch_shapes=[pltpu.VMEM((RET_WIDTH, RET_HEAD_DIM), F32)],
        compiler_params=_cparams("arbitrary", "arbitrary"),
        name="retention_states",
    )(p_ret, p_ret, zeta, chunk_decay, s0)


def _ret_out_kernel(q_ref, k_ref, v_ref, g_ref, mask_ref, xi_ref, sf_ref, sb_ref, y_ref, *, cpo):
    for c in range(cpo):
        rows = slice(c * RET_CHUNK, (c + 1) * RET_CHUNK)
        q = q_ref[rows, :]
        k = (k_ref[rows, :] * (RET_HEAD_DIM ** -0.5)).astype(BF16)
        v = v_ref[rows, :].astype(BF16)
        qb = q.astype(BF16)
        qf = (q * xi_ref[0]).astype(BF16)
        qr = (q * xi_ref[1]).astype(BF16)
        sf = sf_ref[0, c].astype(BF16)
        sb = sb_ref[0, c].astype(BF16)
        ys = []
        for h in range(RET_HEADS):
            sl = slice(h * RET_HEAD_DIM, (h + 1) * RET_HEAD_DIM)
            scores = lax.dot_general(qb[:, sl], k[:, sl], (((1,), (1,)), ((), ())),
                                     preferred_element_type=F32)
            y = _dot((scores * mask_ref[h]).astype(BF16), v[:, sl])
            y = y + _dot(qf[:, sl], sf[sl, :]) + _dot(qr[:, sl], sb[sl, :])
            mu = jnp.mean(y, axis=-1, keepdims=True)
            yc = y - mu
            var = jnp.mean(yc * yc, axis=-1, keepdims=True)
            ys.append(yc * lax.rsqrt(var + EPS))
        y_ref[rows, :] = (_silu(g_ref[rows, :]) * jnp.concatenate(ys, axis=1)).astype(y_ref.dtype)


def _retention_outputs(p_ret, mask, xi, sprev):
    nc = p_ret.shape[0] // RET_CHUNK
    cpo = min(nc, RET_OUT_CHUNKS)
    rows = cpo * RET_CHUNK
    col = lambda c: pl.BlockSpec((rows, RET_WIDTH), lambda i: (i, c))
    return pl.pallas_call(
        functools.partial(_ret_out_kernel, cpo=cpo),
        out_shape=jax.ShapeDtypeStruct((p_ret.shape[0], RET_WIDTH), BF16),
        grid=(nc // cpo,),
        in_specs=[col(0), col(1), col(2), col(3),
                  pl.BlockSpec((RET_HEADS, RET_CHUNK, RET_CHUNK), lambda i: (0, 0, 0)),
                  pl.BlockSpec((2, RET_CHUNK, RET_WIDTH), lambda i: (0, 0, 0)),
                  pl.BlockSpec((1, cpo, RET_WIDTH, RET_HEAD_DIM), lambda i: (0, i, 0, 0)),
                  pl.BlockSpec((1, cpo, RET_WIDTH, RET_HEAD_DIM), lambda i: (1, i, 0, 0))],
        out_specs=pl.BlockSpec((rows, RET_WIDTH), lambda i: (i, 0)),
        compiler_params=_cparams("parallel"),
        name="retention_outputs",
    )(p_ret, p_ret, p_ret, p_ret, mask, xi, sprev, sprev)


def _hy_pre_kernel(p_ref, prev_ref, next_ref, w_ref, b_ref, x0_ref, z_ref):
    i = pl.program_id(0)
    x = p_ref[...]
    tm = x.shape[0]
    row = lax.broadcasted_iota(jnp.int32, x.shape, 0)
    prev_row = jnp.where(i == 0, 0.0, prev_ref[7:8, :])
    next_row = jnp.where(i == pl.num_programs(0) - 1, 0.0, next_ref[0:1, :])
    up = jnp.where(row == 0, prev_row, pltpu.roll(x, 1, 0))
    dn = jnp.where(row == tm - 1, next_row, pltpu.roll(x, tm - 1, 0))
    w = w_ref[...]
    u = up * w[0:1] + x * w[1:2] + dn * w[2:3] + b_ref[...]
    x0_ref[...] = u[:, :HY_WIDTH]
    z_ref[...] = u[:, HY_WIDTH:2 * HY_WIDTH] * u[:, 2 * HY_WIDTH:]


def _hyena_pre(p_hy, conv_w, conv_b):
    t = p_hy.shape[0]
    tm = min(t, 512)
    nb8 = tm // 8
    last8 = t // 8 - 1
    return pl.pallas_call(
        _hy_pre_kernel,
        out_shape=(jax.ShapeDtypeStruct((t, HY_WIDTH), F32),
                   jax.ShapeDtypeStruct((t, HY_WIDTH), F32)),
        grid=(t // tm,),
        in_specs=[pl.BlockSpec((tm, HY_COLS), lambda i: (i, 0)),
                  pl.BlockSpec((8, HY_COLS), lambda i: (jnp.maximum(i * nb8 - 1, 0), 0)),
                  pl.BlockSpec((8, HY_COLS), lambda i: (jnp.minimum((i + 1) * nb8, last8), 0)),
                  pl.BlockSpec((3, HY_COLS), lambda i: (0, 0)),
                  pl.BlockSpec((1, HY_COLS), lambda i: (0, 0))],
        out_specs=(pl.BlockSpec((tm, HY_WIDTH), lambda i: (i, 0)),
                   pl.BlockSpec((tm, HY_WIDTH), lambda i: (i, 0))),
        compiler_params=_cparams("parallel"),
        name="hyena_pre",
    )(p_hy, p_hy, p_hy, conv_w, conv_b.reshape(1, HY_COLS))


def _hy_filter_kernel(feat_ref, w1_ref, b1_ref, w2_ref, b2_ref, w3_ref, b3_ref, freq_ref,
                      delta_ref, taps_ref, asum_ref, *, seq_len):
    i = pl.program_id(0)
    feat = feat_ref[...]
    tm = feat.shape[0]
    half = tm // 2
    freq = freq_ref[...]
    packed = jnp.concatenate([feat[:half], feat[half:]], axis=1)
    h = jnp.sin(freq * (_dot3(packed, w1_ref[...]) + b1_ref[...]))
    h = jnp.sin(freq * (_dot3(h, w2_ref[...]) + b2_ref[...]))
    h = _dot3(h, w3_ref[0])
    h = jnp.concatenate([h[:, :HY_WIDTH], h[:, HY_WIDTH:]], axis=0) + b3_ref[0]
    j = i * tm + lax.broadcasted_iota(jnp.int32, (tm, HY_WIDTH), 0)
    taps = jnp.where(j == seq_len, 0.0, h * jnp.exp(-feat[:, 0:1] * delta_ref[...]))
    taps_ref[...] = taps

    @pl.when(i == 0)
    def _():
        asum_ref[...] = jnp.zeros_like(asum_ref)

    asum_ref[...] += jnp.sum(jnp.abs(taps).reshape(tm // 8, 8, HY_WIDTH), axis=0)


def _position_features(seq_len):
    lo = 256
    hi = seq_len // lo + 1
    bands = np.linspace(1e-4, HY_POS_BANDS - 1, HY_POS_BANDS).astype(np.float32).astype(np.float64)
    ang_hi = (2.0 * np.pi / seq_len) * lo * np.arange(hi)[:, None] * bands[None, :]
    ang_lo = (2.0 * np.pi / seq_len) * np.arange(lo)[:, None] * bands[None, :]
    tab = lambda a: jnp.asarray(a, F32)
    ch, sh = tab(np.cos(ang_hi))[:, None, :], tab(np.sin(ang_hi))[:, None, :]
    cl, sl = tab(np.cos(ang_lo))[None, :, :], tab(np.sin(ang_lo))[None, :, :]
    npos = seq_len + 1
    cos = (ch * cl - sh * sl).reshape(hi * lo, HY_POS_BANDS)[:npos]
    sin = (sh * cl + ch * sl).reshape(hi * lo, HY_POS_BANDS)[:npos]
    c_end, s_end = tab(np.cos(2.0 * np.pi * bands))[None, :], tab(np.sin(2.0 * np.pi * bands))[None, :]
    cos_b = c_end * cos[1:seq_len] + s_end * sin[1:seq_len]
    sin_b = s_end * cos[1:seq_len] - c_end * sin[1:seq_len]
    lag = jnp.concatenate([jnp.arange(npos, dtype=F32), jnp.arange(seq_len - 1, 0, -1, dtype=F32)])
    t = lag / max(seq_len - 1, 1)
    cos = jnp.concatenate([cos, cos_b], axis=0)
    sin = jnp.concatenate([sin, sin_b], axis=0)
    return jnp.concatenate([t[:, None], cos, -sin,
                            jnp.zeros((2 * seq_len, LANES - 1 - 2 * HY_POS_BANDS), F32)], axis=-1)


def _hyena_taps(seq_len, w1, b1, w2, b2, w3, b3, freq):
    n = 2 * seq_len
    feat = _position_features(seq_len)
    w1p = jnp.concatenate([w1, jnp.zeros((LANES - w1.shape[0], w1.shape[1]), F32)], axis=0)
    pair = lambda w: jnp.concatenate(
        [jnp.concatenate([w, jnp.zeros_like(w)], axis=1),
         jnp.concatenate([jnp.zeros_like(w), w], axis=1)], axis=0)
    twice = lambda v: jnp.tile(v.reshape(1, -1), (1, 2))
    w3_dirs = jnp.stack([pair(w3[:, :HY_WIDTH]), pair(w3[:, HY_WIDTH:])])
    b3_dirs = b3.reshape(2, 1, HY_WIDTH)
    deltas = jnp.abs(jnp.linspace(math.log(HY_DECAY_TARGET) / HY_LONG_DECAY_PCT,
                                  math.log(HY_DECAY_TARGET) / HY_SHORT_DECAY_PCT,
                                  HY_WIDTH, dtype=F32)).reshape(1, HY_WIDTH)
    tm = min(seq_len, 1024)
    per_dir = seq_len // tm
    hid2 = 2 * w1.shape[1]
    const = lambda shape: pl.BlockSpec(shape, lambda i: (0, 0))
    by_dir = lambda shape: pl.BlockSpec(shape, lambda i: (i // per_dir, 0, 0))
    return pl.pallas_call(
        functools.partial(_hy_filter_kernel, seq_len=seq_len),
        out_shape=(jax.ShapeDtypeStruct((n, HY_WIDTH), F32),
                   jax.ShapeDtypeStruct((8, HY_WIDTH), F32)),
        grid=(n // tm,),
        in_specs=[pl.BlockSpec((tm, LANES), lambda i: (i, 0)),
                  const((2 * LANES, hid2)), const((1, hid2)), const((hid2, hid2)), const((1, hid2)),
                  by_dir((1, hid2, 2 * HY_WIDTH)), by_dir((1, 1, HY_WIDTH)), const((1, hid2)),
                  const((1, HY_WIDTH))],
        out_specs=(pl.BlockSpec((tm, HY_WIDTH), lambda i: (i, 0)), const((8, HY_WIDTH))),
        compiler_params=_cparams("arbitrary"),
        name="hyena_taps",
    )(feat, pair(w1p), twice(b1), pair(w2), twice(b2), w3_dirs, b3_dirs, twice(freq), deltas)


FFT_N2 = 256
FFT_K1_PER_STEP = 4


def _dft_tables(n_total, n1_in):
    n1 = n_total // FFT_N2
    k1 = np.arange(n1)
    a = 2.0 * np.pi * np.outer(k1, np.arange(n1_in)) / n1
    lvl1 = np.concatenate([np.cos(a), -np.sin(a)], axis=0)
    n2 = np.arange(FFT_N2)
    th = 2.0 * np.pi * np.outer(n2, n2) / FFT_N2
    fc, fs = np.cos(th), np.sin(th)
    g = np.block([[fc, fs], [-fs, fc]])
    ph = 2.0 * np.pi * np.outer(k1, n2) / n_total
    tw = np.stack([np.cos(ph), np.sin(ph)])[..., None]
    return (jnp.asarray(lvl1, F32), jnp.asarray(g, F32), jnp.asarray(g.T, F32), jnp.asarray(tw, F32))


def _inverse_lvl1_table(n_total, n1_out):
    n1 = n_total // FFT_N2
    a = 2.0 * np.pi * np.outer(np.arange(n1_out), np.arange(n1)) / n1
    return jnp.asarray(np.concatenate([np.cos(a), -np.sin(a)], axis=1), F32)


def _lvl2_kernel(b_ref, tw_ref, gh_ref, gl_ref, *rest, inverse):
    for j in range(b_ref.shape[1]):
        c = tw_ref[0, j]
        s = tw_ref[1, j]
        br = b_ref[0, j]
        bi = b_ref[1, j]
        d = jnp.concatenate([c * br + s * bi, c * bi - s * br], axis=0)
        dh, dl = _split(d)
        x = _dot(gh_ref[...], dh) + _dot(gh_ref[...], dl) + _dot(gl_ref[...], dh)
        if not inverse:
            (o_ref,) = rest
            o_ref[0, j] = x[:FFT_N2]
            o_ref[1, j] = x[FFT_N2:]
            continue
        h_ref, gth_ref, gtl_ref, o_ref = rest
        xr, xi = x[:FFT_N2], x[FFT_N2:]
        hr, hi = h_ref[0, j], h_ref[1, j]
        y = jnp.concatenate([xr * hr - xi * hi, xr * hi + xi * hr], axis=0)
        yh, yl = _split(y)
        cc = _dot(gth_ref[...], yh) + _dot(gth_ref[...], yl) + _dot(gtl_ref[...], yh)
        cr, ci = cc[:FFT_N2], cc[FFT_N2:]
        o_ref[0, j] = c * cr - s * ci
        o_ref[1, j] = c * ci + s * cr


def _level2(b, tw, g, h=None, gt=None):
    _, n1, n2, ch = b.shape
    kb = FFT_K1_PER_STEP
    blk = pl.BlockSpec((2, kb, n2, ch), lambda i: (0, i, 0, 0))
    const = pl.BlockSpec((2 * n2, 2 * n2), lambda i: (0, 0))
    gh, gl = _split(g)
    in_specs = [blk, pl.BlockSpec((2, kb, n2, 1), lambda i: (0, i, 0, 0)), const, const]
    args = [b, tw, gh, gl]
    if h is not None:
        gth, gtl = _split(gt)
        in_specs += [blk, const, const]
        args += [h, gth, gtl]
    return pl.pallas_call(
        functools.partial(_lvl2_kernel, inverse=h is not None),
        out_shape=jax.ShapeDtypeStruct(b.shape, F32),
        grid=(n1 // kb,),
        in_specs=in_specs,
        out_specs=blk,
        compiler_params=_cparams("parallel"),
        name="hyena_fft_level2" + ("_conv" if h is not None else ""),
    )(*args)


def _hyena_long_conv(z, taps):
    seq_len, ch = z.shape
    n = 2 * seq_len
    n1 = n // FFT_N2
    lvl1_z, g, gt, tw = _dft_tables(n, n1 // 2)
    lvl1_t = _dft_tables(n, n1)[0]
    inv1 = _inverse_lvl1_table(n, n1 // 2)
    cols = FFT_N2 * ch
    tn = min(cols, 4096)
    hb = _mm3(lvl1_t, taps.reshape(n1, cols), tn, "hyena_fft_level1_taps")
    hspec = _level2(hb.reshape(2, n1, FFT_N2, ch), tw, g)
    zb = _mm3(lvl1_z, z.reshape(n1 // 2, cols), tn, "hyena_fft_level1_z")
    cspec = _level2(zb.reshape(2, n1, FFT_N2, ch), tw, g, hspec, gt)
    conv = _mm3(inv1, cspec.reshape(2 * n1, cols), tn, "hyena_fft_inverse_level1")
    return conv.reshape(seq_len, ch)


def _small_conv_kernel(z_ref, taps_ref, fc_ref, fs_ref, o_ref):
    seq_len = z_ref.shape[0]
    fc = fc_ref[...]
    fs = fs_ref[...]
    z = z_ref[...]
    taps = taps_ref[...]
    zr = _dot3(fc[:, :seq_len], z)
    zi = -_dot3(fs[:, :seq_len], z)
    hr = _dot3(fc, taps)
    hi = -_dot3(fs, taps)
    yr = zr * hr - zi * hi
    yi = zr * hi + zi * hr
    o_ref[...] = _dot3(fc[:seq_len, :], yr) - _dot3(fs[:seq_len, :], yi)


def _hyena_small_conv(z, taps):
    seq_len, ch = z.shape
    n = 2 * seq_len
    th = 2.0 * np.pi * np.outer(np.arange(n), np.arange(n)) / n
    full = lambda shape: pl.BlockSpec(shape, lambda: (0, 0))
    return pl.pallas_call(
        _small_conv_kernel,
        out_shape=jax.ShapeDtypeStruct((seq_len, ch), F32),
        in_specs=[full((seq_len, ch)), full((n, ch)), full((n, n)), full((n, n))],
        out_specs=full((seq_len, ch)),
        compiler_params=pltpu.CompilerParams(vmem_limit_bytes=VMEM_LIMIT),
        name="hyena_small_conv",
    )(z, taps, jnp.asarray(np.cos(th), F32), jnp.asarray(np.sin(th), F32))


def _hy_post_kernel(x0_ref, z_ref, conv_ref, asum_ref, bias_ref, y_ref, *, n_total):
    norm = jnp.sum(asum_ref[...], axis=0, keepdims=True) + EPS
    conv = conv_ref[...] * (1.0 / (n_total * norm))
    y_ref[...] = (x0_ref[...] * (conv + bias_ref[...] * z_ref[...])).astype(y_ref.dtype)


def _hyena_post(x0, z, conv, asum, bias):
    t = x0.shape[0]
    tm = min(t, 1024)
    row = pl.BlockSpec((tm, HY_WIDTH), lambda i: (i, 0))
    return pl.pallas_call(
        functools.partial(_hy_post_kernel, n_total=2 * t),
        out_shape=jax.ShapeDtypeStruct((t, HY_WIDTH), BF16),
        grid=(t // tm,),
        in_specs=[row, row, row, pl.BlockSpec((8, HY_WIDTH), lambda i: (0, 0)),
                  pl.BlockSpec((1, HY_WIDTH), lambda i: (0, 0))],
        out_specs=row,
        compiler_params=_cparams("parallel"),
        name="hyena_post",
    )(x0, z, conv, asum, bias.reshape(1, HY_WIDTH))


def _hyena_mixer(p_hy, conv_w, conv_b, filt, bias):
    seq_len = p_hy.shape[0]
    x0, z = _hyena_pre(p_hy, conv_w, conv_b)
    taps, asum = _hyena_taps(seq_len, *filt)
    if 2 * seq_len // FFT_N2 >= 16:
        conv = _hyena_long_conv(z, taps)
    else:
        conv = _hyena_small_conv(z, taps)
    return _hyena_post(x0, z, conv, asum, bias)


def _outproj_kernel(hy_ref, ret_ref, att_ref, w_ref, x_ref, gate_ref, o_ref):
    y = (_dot(hy_ref[...], w_ref[:HY_WIDTH, :])
         + _dot(ret_ref[...], w_ref[HY_WIDTH:HY_WIDTH + RET_WIDTH, :])
         + _dot(att_ref[...], w_ref[HY_WIDTH + RET_WIDTH:, :]))
    o_ref[...] = x_ref[...] + gate_ref[...] * y


def _out_projection(y_hy, y_ret, y_att, w_bf16, x, gate):
    t = x.shape[0]
    tm = min(t, 512)
    row = lambda w: pl.BlockSpec((tm, w), lambda i: (i, 0))
    return pl.pallas_call(
        _outproj_kernel,
        out_shape=jax.ShapeDtypeStruct((t, D_MODEL), F32),
        grid=(t // tm,),
        in_specs=[row(HY_WIDTH), row(RET_WIDTH), row(ATT_WIDTH),
                  pl.BlockSpec((MIX_WIDTH, D_MODEL), lambda i: (0, 0)),
                  row(D_MODEL), pl.BlockSpec((1, D_MODEL), lambda i: (0, 0))],
        out_specs=row(D_MODEL),
        compiler_params=_cparams("parallel"),
        name="out_projection",
    )(y_hy, y_ret, y_att, w_bf16, x, gate)


def _swiglu(x, wg_ref, wu_ref, wd_ref):
    hidden = wg_ref.shape[1]
    y = None
    for lo in range(0, hidden, SWIGLU_CHUNK):
        hi = min(lo + SWIGLU_CHUNK, hidden)
        a = _silu(_dot(x, wg_ref[:, lo:hi])) * _dot(x, wu_ref[:, lo:hi])
        part = _dot(a.astype(BF16), wd_ref[lo:hi, :])
        y = part if y is None else y + part
    return y


def _ffn_kernel(x_ref, g_ref, sh_ref, sc_ref, gate_ref, wg_ref, wu_ref, wd_ref, o_ref):
    x = x_ref[...]
    h = _norm_mod(x, g_ref[...], sh_ref[...], sc_ref[...]).astype(BF16)
    o_ref[...] = x + gate_ref[...] * _swiglu(h, wg_ref, wu_ref, wd_ref)


def _dense_ffn(x, g, shift, scale, gate, wg, wu, wd):
    t = x.shape[0]
    tm = min(t, 512)
    vec = pl.BlockSpec((1, D_MODEL), lambda i: (0, 0))
    resident = lambda shape: pl.BlockSpec(shape, lambda i: (0, 0), pipeline_mode=pl.Buffered(1))
    return pl.pallas_call(
        _ffn_kernel,
        out_shape=jax.ShapeDtypeStruct((t, D_MODEL), F32),
        grid=(t // tm,),
        in_specs=[pl.BlockSpec((tm, D_MODEL), lambda i: (i, 0)), vec, vec, vec, vec,
                  resident((D_MODEL, FFN_HIDDEN)), resident((D_MODEL, FFN_HIDDEN)),
                  resident((FFN_HIDDEN, D_MODEL))],
        out_specs=pl.BlockSpec((tm, D_MODEL), lambda i: (i, 0)),
        compiler_params=_cparams("parallel"),
        name="dense_ffn",
    )(x, g, shift, scale, gate, wg, wu, wd)


def _moe_route_kernel(x_ref, g_ref, sh_ref, sc_ref, rw_ref, rb_ref, h_ref, info_ref):
    h = _norm_mod(x_ref[...], g_ref[...], sh_ref[...], sc_ref[...])
    _to_row_tiles(h_ref, h)
    logits = _dot3(h, rw_ref[...]) + rb_ref[...]
    lane = lax.broadcasted_iota(jnp.int32, logits.shape, 1)
    v1 = jnp.max(logits, axis=-1, keepdims=True)
    i1 = jnp.min(jnp.where(logits == v1, lane, LANES), axis=-1, keepdims=True)
    rest = jnp.where(lane == i1, -jnp.inf, logits)
    v2 = jnp.max(rest, axis=-1, keepdims=True)
    i2 = jnp.min(jnp.where(rest == v2, lane, LANES), axis=-1, keepdims=True)
    e = jnp.exp(v2 - v1)
    g1 = 1.0 / (1.0 + e)
    g2 = e * g1
    info_ref[...] = jnp.where(lane == 0, i1.astype(F32),
                              jnp.where(lane == 1, i2.astype(F32),
                                        jnp.where(lane == 2, g1, jnp.where(lane == 3, g2, 0.0))))


def _moe_route(x, g, shift, scale, router_w, router_b):
    t = x.shape[0]
    tm = min(t, 512)
    rw = jnp.concatenate([router_w, jnp.zeros((D_MODEL, LANES - N_EXPERTS), F32)], axis=1)
    rb = jnp.concatenate([router_b, jnp.full((LANES - N_EXPERTS,), -jnp.inf, F32)]).reshape(1, LANES)
    vec = pl.BlockSpec((1, D_MODEL), lambda i: (0, 0))
    return pl.pallas_call(
        _moe_route_kernel,
        out_shape=(jax.ShapeDtypeStruct((t * SUBLANES, LANES), F32),
                   jax.ShapeDtypeStruct((t, LANES), F32)),
        grid=(t // tm,),
        in_specs=[pl.BlockSpec((tm, D_MODEL), lambda i: (i, 0)), vec, vec, vec,
                  pl.BlockSpec((D_MODEL, LANES), lambda i: (0, 0)),
                  pl.BlockSpec((1, LANES), lambda i: (0, 0))],
        out_specs=(pl.BlockSpec((tm * SUBLANES, LANES), lambda i: (i, 0)),
                   pl.BlockSpec((tm, LANES), lambda i: (i, 0))),
        compiler_params=_cparams("parallel"),
        name="moe_route",
    )(x, g, shift, scale, rw, rb)


def _to_row_tiles(ref, x):
    rows = x.shape[0]
    for k in range(D_MODEL // LANES):
        ref[pl.ds(k, rows, stride=SUBLANES), :] = x[:, k * LANES:(k + 1) * LANES]


def _from_row_tiles(ref, rows):
    return jnp.concatenate(
        [ref[pl.ds(k, rows, stride=SUBLANES), :] for k in range(D_MODEL // LANES)], axis=1)


def _row_copy(src_hbm, dst_vmem, sem, src_row, dst_row):
    src = pl.multiple_of(src_row * SUBLANES, SUBLANES)
    dst = pl.multiple_of(dst_row * SUBLANES, SUBLANES)
    return pltpu.make_async_copy(src_hbm.at[pl.ds(src, SUBLANES)], dst_vmem.at[pl.ds(dst, SUBLANES)], sem)


def _moe_expert_kernel(blk_e_ref, nused_ref, src_ref, h_hbm, wg_ref, wu_ref, wd_ref, y_ref,
                       xbuf_ref, sem):
    b = pl.program_id(0)
    n_used = nused_ref[0]

    def issue_rows(blk, slot):
        def body(r, carry):
            _row_copy(h_hbm, xbuf_ref.at[slot], sem.at[slot], src_ref[blk * MOE_BLOCK + r], r).start()
            return carry

        lax.fori_loop(0, MOE_BLOCK, body, 0, unroll=DMA_UNROLL)

    def wait_rows(slot):
        def body(r, carry):
            _row_copy(h_hbm, xbuf_ref.at[slot], sem.at[slot], 0, r).wait()
            return carry

        lax.fori_loop(0, MOE_BLOCK, body, 0, unroll=DMA_UNROLL)

    @pl.when(b < n_used)
    def _():
        slot = b % 2

        @pl.when(b == 0)
        def _():
            issue_rows(0, 0)

        wait_rows(slot)

        @pl.when(b + 1 < n_used)
        def _():
            issue_rows(b + 1, 1 - slot)

        x = _from_row_tiles(xbuf_ref.at[slot], MOE_BLOCK).astype(BF16)
        _to_row_tiles(y_ref, _swiglu(x, wg_ref.at[0], wu_ref.at[0], wd_ref.at[0]))

    @pl.when(b >= n_used)
    def _():
        y_ref[...] = jnp.zeros_like(y_ref)


def _moe_experts(h, blk_expert, n_used, src_tok, wg, wu, wd, n_blk):
    def w_index(b, be, nu, st):
        return be[jnp.minimum(b, nu[0] - 1)], 0, 0

    grid_spec = pltpu.PrefetchScalarGridSpec(
        num_scalar_prefetch=3,
        grid=(n_blk,),
        in_specs=[pl.BlockSpec(memory_space=pl.ANY),
                  pl.BlockSpec((1, D_MODEL, FFN_HIDDEN), w_index),
                  pl.BlockSpec((1, D_MODEL, FFN_HIDDEN), w_index),
                  pl.BlockSpec((1, FFN_HIDDEN, D_MODEL), w_index)],
        out_specs=pl.BlockSpec((MOE_BLOCK * SUBLANES, LANES), lambda b, be, nu, st: (b, 0)),
        scratch_shapes=[pltpu.VMEM((2, MOE_BLOCK * SUBLANES, LANES), F32),
                        pltpu.SemaphoreType.DMA((2,))],
    )
    return pl.pallas_call(
        _moe_expert_kernel,
        out_shape=jax.ShapeDtypeStruct((n_blk * MOE_BLOCK * SUBLANES, LANES), F32),
        grid_spec=grid_spec,
        compiler_params=_cparams("arbitrary"),
        name="moe_experts",
    )(blk_expert, n_used, src_tok, h, wg, wu, wd)


def _moe_combine_kernel(dest_ref, x_ref, info_ref, gate_ref, ng_ref, y_hbm, o_ref,
                        y1_ref, y2_ref, sem, *, final_norm):
    i = pl.program_id(0)
    tm = x_ref.shape[0]
    slot = i % 2

    def issue_rows(blk, slot):
        def body(r, carry):
            a = 2 * (blk * tm + r)
            _row_copy(y_hbm, y1_ref.at[slot], sem.at[slot], dest_ref[a], r).start()
            _row_copy(y_hbm, y2_ref.at[slot], sem.at[slot], dest_ref[a + 1], r).start()
            return carry

        lax.fori_loop(0, tm, body, 0, unroll=DMA_UNROLL)

    def wait_rows(slot):
        def body(r, carry):
            _row_copy(y_hbm, y1_ref.at[slot], sem.at[slot], 0, r).wait()
            _row_copy(y_hbm, y2_ref.at[slot], sem.at[slot], 0, r).wait()
            return carry

        lax.fori_loop(0, tm, body, 0, unroll=DMA_UNROLL)

    @pl.when(i == 0)
    def _():
        issue_rows(0, 0)

    wait_rows(slot)

    @pl.when(i + 1 < pl.num_programs(0))
    def _():
        issue_rows(i + 1, 1 - slot)

    info = info_ref[...]
    y = (_from_row_tiles(y1_ref.at[slot], tm) * info[:, 2:3]
         + _from_row_tiles(y2_ref.at[slot], tm) * info[:, 3:4])
    x = x_ref[...] + gate_ref[...] * y
    if final_norm:
        ms = jnp.mean(x * x, axis=-1, keepdims=True)
        x = x * lax.rsqrt(ms + EPS) * ng_ref[...]
    o_ref[...] = x


def _moe_combine(dest, x, info, gate, norm_g, y, final_norm):
    t = x.shape[0]
    tm = min(t, 256)
    vec = pl.BlockSpec((1, D_MODEL), lambda i, d: (0, 0))
    grid_spec = pltpu.PrefetchScalarGridSpec(
        num_scalar_prefetch=1,
        grid=(t // tm,),
        in_specs=[pl.BlockSpec((tm, D_MODEL), lambda i, d: (i, 0)),
                  pl.BlockSpec((tm, LANES), lambda i, d: (i, 0)), vec, vec,
                  pl.BlockSpec(memory_space=pl.ANY)],
        out_specs=pl.BlockSpec((tm, D_MODEL), lambda i, d: (i, 0)),
        scratch_shapes=[pltpu.VMEM((2, tm * SUBLANES, LANES), F32),
                        pltpu.VMEM((2, tm * SUBLANES, LANES), F32),
                        pltpu.SemaphoreType.DMA((2,))],
    )
    return pl.pallas_call(
        functools.partial(_moe_combine_kernel, final_norm=final_norm),
        out_shape=jax.ShapeDtypeStruct((t, D_MODEL), F32),
        grid_spec=grid_spec,
        compiler_params=_cparams("arbitrary"),
        name="moe_combine",
    )(dest, x, info, gate, norm_g, y)


def _moe_layer(x, g, shift, scale, gate, router_w, router_b, wg, wu, wd, norm_g, final_norm):
    t = x.shape[0]
    n_asg = t * TOP_K
    h, info = _moe_route(x, g, shift, scale, router_w, router_b)
    expert = info[:, :TOP_K].astype(jnp.int32).reshape(-1)
    onehot = (expert[:, None] == jnp.arange(N_EXPERTS, dtype=jnp.int32)[None, :]).astype(jnp.int32)
    csum = jnp.cumsum(onehot, axis=0)
    counts = csum[-1]
    padded = (counts + MOE_BLOCK - 1) // MOE_BLOCK * MOE_BLOCK
    pad_end = jnp.cumsum(padded)
    pad_start = pad_end - padded
    dest = jnp.sum(onehot * (csum - 1 + pad_start[None, :]), axis=1).astype(jnp.int32)
    n_blk = -(-n_asg // MOE_BLOCK) + N_EXPERTS
    blk_start = jnp.arange(n_blk, dtype=jnp.int32) * MOE_BLOCK
    blk_expert = jnp.minimum(jnp.sum(blk_start[:, None] >= pad_end[None, :], axis=1),
                             N_EXPERTS - 1).astype(jnp.int32)
    n_used = (pad_end[-1:] // MOE_BLOCK).astype(jnp.int32)
    src_tok = jnp.zeros((n_blk * MOE_BLOCK,), jnp.int32).at[dest].set(
        jnp.arange(n_asg, dtype=jnp.int32) // TOP_K, unique_indices=True, mode="promise_in_bounds")
    y = _moe_experts(h, blk_expert, n_used, src_tok, wg, wu, wd, n_blk)
    return _moe_combine(dest, x, info, gate, norm_g, y, final_norm)


def _final_norm_kernel(x_ref, g_ref, o_ref):
    x = x_ref[...]
    ms = jnp.mean(x * x, axis=-1, keepdims=True)
    o_ref[...] = x * lax.rsqrt(ms + EPS) * g_ref[...]


def _final_norm(x, g):
    t = x.shape[0]
    tm = min(t, 1024)
    return pl.pallas_call(
        _final_norm_kernel,
        out_shape=jax.ShapeDtypeStruct((t, D_MODEL), F32),
        grid=(t // tm,),
        in_specs=[pl.BlockSpec((tm, D_MODEL), lambda i: (i, 0)),
                  pl.BlockSpec((1, D_MODEL), lambda i: (0, 0))],
        out_specs=pl.BlockSpec((tm, D_MODEL), lambda i: (i, 0)),
        compiler_params=_cparams("parallel"),
        name="final_norm",
    )(x, g)


def kernel(x, c, ctx, c_ctx, ada_w, ada_b, norm1_g, norm2_g, w_in, w_out, hy_conv_w, hy_conv_b, hy_filt_w1, hy_filt_b1, hy_filt_w2, hy_filt_b2, hy_filt_w3, hy_filt_b3, hy_filt_freq, hy_bias, ret_log_rate, attn_q_g, attn_k_g, ffn_w_gate, ffn_w_up, ffn_w_down, moe_router_w, moe_router_b, moe_w_gate, moe_w_up, moe_w_down, final_norm_g):
    assert x.shape[0] == 1 and c.shape[0] == 1
    seq_len = x.shape[1]
    x_lat = x[0]
    x_ctx = ctx[0]
    cos_t, sin_t = _rope_tables(seq_len)
    cvec = jnp.concatenate([c, c_ctx[None, :], jnp.zeros((6, D_MODEL), F32)], axis=0)
    cvec = _silu(cvec)
    row = lambda v: v.reshape(1, D_MODEL)
    zero_state = jnp.zeros((2, RET_WIDTH, RET_HEAD_DIM), F32)
    zero_tab = jnp.zeros((x_ctx.shape[0], LANES), F32)

    for l in range(DEPTH):
        last = l == DEPTH - 1
        mods = _ada_modulation(cvec, ada_w, ada_b, l).reshape(8, 6, D_MODEL)
        mod = [row(mods[0, i]) for i in range(6)]
        mod_c = [row(mods[1, i]) for i in range(6)]
        filt = (hy_filt_w1[l], hy_filt_b1[l], hy_filt_w2[l], hy_filt_b2[l],
                hy_filt_w3[l], hy_filt_b3[l], hy_filt_freq[l])
        lg_f = -jnp.exp(ret_log_rate[l, 0].astype(F32))
        lg_b = -jnp.exp(ret_log_rate[l, 1].astype(F32))
        mask, zeta, xi, chunk_decay = _ret_tables(lg_f, lg_b)
        w_in_l = w_in[l].astype(BF16)
        w_out_l = w_out[l].astype(BF16)
        n1 = row(norm1_g[l])
        n2 = row(norm2_g[l])

        p_hy, p_ret, p_att = _in_projection(x_lat, n1, mod[0], mod[1], w_in_l)
        pc_hy, pc_ret, pc_att = _in_projection(x_ctx, n1, mod_c[0], mod_c[1], w_in_l)

        sprev_c, s_ctx = _retention_states(pc_ret, zeta, chunk_decay, zero_state)
        qct, kc, vct, _ = _attention_prep(pc_att, zero_tab, zero_tab, attn_q_g[l], attn_k_g[l],
                                          rope=False)

        y_hy = _hyena_mixer(p_hy, hy_conv_w[l], hy_conv_b[l], filt, hy_bias[l])
        sprev, _ = _retention_states(p_ret, zeta, chunk_decay, s_ctx)
        y_ret = _retention_outputs(p_ret, mask, xi, sprev)
        qt, k_aug, vt, kmax_sq = _attention_prep(p_att, cos_t, sin_t, attn_q_g[l], attn_k_g[l],
                                                 rope=True)
        kmax = jnp.sqrt(jnp.max(kmax_sq[:, 0, ::ATT_HEAD_DIM], axis=0)) * KMAX_SLACK
        y_att = _attention(qt, kc, vct, kmax, k_aug, vt, tq=ATT_TQ, tk=ATT_TK, sub=ATT_SUB)

        if not last:
            yc_hy = _hyena_mixer(pc_hy, hy_conv_w[l], hy_conv_b[l], filt, hy_bias[l])
            yc_ret = _retention_outputs(pc_ret, mask, xi, sprev_c)
            yc_att = _attention(qct, kc, vct, tq=x_ctx.shape[0])
            x_ctx = _out_projection(yc_hy, yc_ret, yc_att, w_out_l, x_ctx, mod_c[2])
        x_lat = _out_projection(y_hy, y_ret, y_att, w_out_l, x_lat, mod[2])

        i = l // 2
        if l % 2 == 0:
            wg, wu, wd = (ffn_w_gate[i].astype(BF16), ffn_w_up[i].astype(BF16),
                          ffn_w_down[i].astype(BF16))
            x_lat = _dense_ffn(x_lat, n2, mod[3], mod[4], mod[5], wg, wu, wd)
            if not last:
                x_ctx = _dense_ffn(x_ctx, n2, mod_c[3], mod_c[4], mod_c[5], wg, wu, wd)
        else:
            wg, wu, wd = (moe_w_gate[i].astype(BF16), moe_w_up[i].astype(BF16),
                          moe_w_down[i].astype(BF16))
            fg = row(final_norm_g)
            x_lat = _moe_layer(x_lat, n2, mod[3], mod[4], mod[5], moe_router_w[i], moe_router_b[i],
                               wg, wu, wd, fg, final_norm=last)
            if not last:
                x_ctx = _moe_layer(x_ctx, n2, mod_c[3], mod_c[4], mod_c[5], moe_router_w[i],
                                   moe_router_b[i], wg, wu, wd, fg, final_norm=False)
    if DEPTH % 2 == 1:
        x_lat = _final_norm(x_lat, row(final_norm_g))
    return x_lat[None]
```

```python
import functools
import math

import numpy as np
import jax
import jax.numpy as jnp
from jax import lax
from jax.experimental import pallas as pl
from jax.experimental.pallas import tpu as pltpu

F32 = jnp.float32
BF16 = jnp.bfloat16

D_MODEL = 1024
DEPTH = 2
GRID_W = 64
EPS = 1e-6

HY_WIDTH = 256
HY_COLS = 3 * HY_WIDTH
HY_POS_BANDS = 16
HY_DECAY_TARGET = 1e-2
HY_SHORT_DECAY_PCT = 0.3
HY_LONG_DECAY_PCT = 1.5

RET_HEAD_DIM = 64
RET_HEADS = 4
RET_WIDTH = RET_HEADS * RET_HEAD_DIM
RET_COLS = 4 * RET_WIDTH
RET_CHUNK = 128

ATT_HEAD_DIM = 64
ATT_HEADS = 8
ATT_KV_HEADS = 2
ATT_GROUP = ATT_HEADS // ATT_KV_HEADS
ATT_WIDTH = ATT_HEADS * ATT_HEAD_DIM
ATT_KV_WIDTH = ATT_KV_HEADS * ATT_HEAD_DIM
ATT_COLS = ATT_WIDTH + 2 * ATT_KV_WIDTH
ROPE_AXIS_DIM = ATT_HEAD_DIM // 2
ROPE_BASE = 10000.0

MIX_WIDTH = HY_WIDTH + RET_WIDTH + ATT_WIDTH
IN_COLS = HY_COLS + RET_COLS + ATT_COLS

FFN_HIDDEN = 2816
N_EXPERTS = 8
TOP_K = 2
MOE_BLOCK = 512

LOG2_E = 1.4426950408889634
EXP2_CAP = 60.0
KMAX_SLACK = 1.0 + 2.0 ** -7
ATT_TQ = 512
ATT_TK = 4096
ATT_SUB = 512
ATT_VROWS = 80
RET_STATE_CHUNKS = 8
RET_OUT_CHUNKS = 4
SWIGLU_CHUNK = 1024
DMA_UNROLL = 8

LANES = 128
SUBLANES = 8
VMEM_LIMIT = 56 * 1024 * 1024


def _cparams(*sem):
    return pltpu.CompilerParams(dimension_semantics=sem, vmem_limit_bytes=VMEM_LIMIT)


def _dot(a, b):
    return jnp.dot(a, b, preferred_element_type=F32)


def _split(a):
    hi = a.astype(BF16)
    lo = (a - hi.astype(F32)).astype(BF16)
    return hi, lo


def _dot3(a, b):
    ah, al = _split(a)
    bh, bl = _split(b)
    return _dot(ah, bh) + _dot(al, bh) + _dot(ah, bl)


def _silu(x):
    return x * (1.0 / (1.0 + jnp.exp(-x)))


def _norm_mod(x, g, shift, scale):
    ms = jnp.mean(x * x, axis=-1, keepdims=True)
    h = x * lax.rsqrt(ms + EPS) * g
    return h * (1.0 + scale) + shift


def _mm3_kernel(a_ref, b_ref, o_ref):
    o_ref[...] = _dot3(a_ref[...], b_ref[...])


def _mm3(a, b, tn, name):
    m, k = a.shape
    n = b.shape[1]
    return pl.pallas_call(
        _mm3_kernel,
        out_shape=jax.ShapeDtypeStruct((m, n), F32),
        grid=(n // tn,),
        in_specs=[pl.BlockSpec((m, k), lambda j: (0, 0)),
                  pl.BlockSpec((k, tn), lambda j: (0, j))],
        out_specs=pl.BlockSpec((m, tn), lambda j: (0, j)),
        compiler_params=_cparams("parallel"),
        name=name,
    )(a, b)


def _ada_kernel(c_ref, w_ref, b_ref, o_ref):
    o_ref[...] = _dot3(c_ref[...], w_ref[0]) + b_ref[0]


def _ada_modulation(cc, w, b, layer):
    n = w.shape[2]
    tn = 1536
    return pl.pallas_call(
        _ada_kernel,
        out_shape=jax.ShapeDtypeStruct((8, n), F32),
        grid=(n // tn,),
        in_specs=[pl.BlockSpec((8, D_MODEL), lambda j: (0, 0)),
                  pl.BlockSpec((1, D_MODEL, tn), lambda j: (layer, 0, j)),
                  pl.BlockSpec((1, 1, tn), lambda j: (layer, 0, j))],
        out_specs=pl.BlockSpec((8, tn), lambda j: (0, j)),
        compiler_params=_cparams("parallel"),
        name="ada_modulation",
    )(cc, w, b.reshape(b.shape[0], 1, n))


def _inproj_kernel(x_ref, g_ref, sh_ref, sc_ref, w_ref, hy_ref, ret_ref, att_ref):
    h = _norm_mod(x_ref[...], g_ref[...], sh_ref[...], sc_ref[...])
    p = _dot(h.astype(BF16), w_ref[...])
    hy_ref[...] = p[:, :HY_COLS]
    ret_ref[...] = p[:, HY_COLS:HY_COLS + RET_COLS]
    att_ref[...] = p[:, HY_COLS + RET_COLS:]


def _in_projection(x, g, shift, scale, w_bf16):
    t = x.shape[0]
    tm = min(t, 512)
    vec = pl.BlockSpec((1, D_MODEL), lambda i: (0, 0))
    return pl.pallas_call(
        _inproj_kernel,
        out_shape=(jax.ShapeDtypeStruct((t, HY_COLS), F32),
                   jax.ShapeDtypeStruct((t, RET_COLS), F32),
                   jax.ShapeDtypeStruct((t, ATT_COLS), F32)),
        grid=(t // tm,),
        in_specs=[pl.BlockSpec((tm, D_MODEL), lambda i: (i, 0)), vec, vec, vec,
                  pl.BlockSpec((D_MODEL, IN_COLS), lambda i: (0, 0))],
        out_specs=(pl.BlockSpec((tm, HY_COLS), lambda i: (i, 0)),
                   pl.BlockSpec((tm, RET_COLS), lambda i: (i, 0)),
                   pl.BlockSpec((tm, ATT_COLS), lambda i: (i, 0))),
        compiler_params=_cparams("parallel"),
        name="in_projection",
    )(x, g, shift, scale, w_bf16)


def _head_mean_matrix(width):
    idx = np.arange(width) // ATT_HEAD_DIM
    return jnp.asarray((idx[:, None] == idx[None, :]).astype(np.float32) / ATT_HEAD_DIM, BF16)


def _head_rms(x, bd, g):
    x2h, x2l = _split(x * x)
    ms = _dot(x2h, bd) + _dot(x2l, bd)
    return x * lax.rsqrt(ms + EPS) * g


def _rope(x, cos, sin):
    n = x.shape[1]
    lane = lax.broadcasted_iota(jnp.int32, x.shape, 1)
    swapped = jnp.where((lane % 32) < 16, pltpu.roll(x, n - 16, 1), pltpu.roll(x, 16, 1))
    return x * cos + swapped * sin


def _attprep_kernel(p_ref, cos_ref, sin_ref, qg_ref, kg_ref, bdq_ref, bdk_ref,
                    qt_ref, k_ref, vt_ref, kmax_ref, *, rope):
    p = p_ref[...]
    q = _head_rms(p[:, :ATT_WIDTH], bdq_ref[...], qg_ref[...])
    k = _head_rms(p[:, ATT_WIDTH:ATT_WIDTH + ATT_KV_WIDTH], bdk_ref[...], kg_ref[...])
    if rope:
        cos = cos_ref[...]
        sin = sin_ref[...]
        q = jnp.concatenate(
            [_rope(q[:, j * LANES:(j + 1) * LANES], cos, sin) for j in range(ATT_WIDTH // LANES)],
            axis=1)
        k = _rope(k, cos, sin)
    qt_ref[...] = (q * (ATT_HEAD_DIM ** -0.5 * LOG2_E)).T.astype(BF16)
    tm = k.shape[0]
    lane = lax.broadcasted_iota(jnp.int32, k.shape, 1)
    minus_one_col = jnp.where(lane == ATT_HEAD_DIM, -1.0, 0.0)
    kb = k.astype(BF16)
    k_ref[0] = jnp.where(lane < ATT_HEAD_DIM, k, minus_one_col).astype(BF16)
    k_ref[1] = jnp.where(lane < ATT_HEAD_DIM, pltpu.roll(k, ATT_HEAD_DIM, 1), minus_one_col).astype(BF16)
    sq_hi, sq_lo = _split(kb.astype(F32) ** 2)
    norm_sq = (_dot(sq_hi, bdk_ref[...]) + _dot(sq_lo, bdk_ref[...])) * ATT_HEAD_DIM
    kmax_ref[0] = jnp.broadcast_to(jnp.max(norm_sq, axis=0, keepdims=True), (8, LANES))
    vt = p[:, ATT_WIDTH + ATT_KV_WIDTH:].T.astype(BF16)
    sub = lax.broadcasted_iota(jnp.int32, (ATT_VROWS - ATT_HEAD_DIM, tm), 0)
    ones_row = jnp.where(sub == 0, 1.0, 0.0).astype(BF16)
    for h in range(ATT_KV_HEADS):
        vt_ref[h * ATT_VROWS:h * ATT_VROWS + ATT_HEAD_DIM, :] = vt[h * ATT_HEAD_DIM:(h + 1) * ATT_HEAD_DIM]
        vt_ref[h * ATT_VROWS + ATT_HEAD_DIM:(h + 1) * ATT_VROWS, :] = ones_row


def _attention_prep(p_att, cos, sin, q_g, k_g, rope):
    t = p_att.shape[0]
    tm = min(t, 512)
    qg = jnp.tile(q_g, ATT_HEADS).reshape(1, ATT_WIDTH)
    kg = jnp.tile(k_g, ATT_KV_HEADS).reshape(1, ATT_KV_WIDTH)
    const = lambda shape: pl.BlockSpec(shape, lambda i: (0, 0))
    return pl.pallas_call(
        functools.partial(_attprep_kernel, rope=rope),
        out_shape=(jax.ShapeDtypeStruct((ATT_WIDTH, t), BF16),
                   jax.ShapeDtypeStruct((ATT_KV_HEADS, t, LANES), BF16),
                   jax.ShapeDtypeStruct((ATT_KV_HEADS * ATT_VROWS, t), BF16),
                   jax.ShapeDtypeStruct((t // tm, 8, LANES), F32)),
        grid=(t // tm,),
        in_specs=[pl.BlockSpec((tm, ATT_COLS), lambda i: (i, 0)),
                  pl.BlockSpec((tm, LANES), lambda i: (i, 0)),
                  pl.BlockSpec((tm, LANES), lambda i: (i, 0)),
                  const((1, ATT_WIDTH)), const((1, ATT_KV_WIDTH)),
                  const((ATT_WIDTH, ATT_WIDTH)), const((ATT_KV_WIDTH, ATT_KV_WIDTH))],
        out_specs=(pl.BlockSpec((ATT_WIDTH, tm), lambda i: (0, i)),
                   pl.BlockSpec((ATT_KV_HEADS, tm, LANES), lambda i: (0, i, 0)),
                   pl.BlockSpec((ATT_KV_HEADS * ATT_VROWS, tm), lambda i: (0, i)),
                   pl.BlockSpec((1, 8, LANES), lambda i: (i, 0, 0))),
        compiler_params=_cparams("parallel"),
        name="attention_prep",
    )(p_att, cos, sin, qg, kg, _head_mean_matrix(ATT_WIDTH), _head_mean_matrix(ATT_KV_WIDTH))


def _rope_tables(n_tokens):
    rows = n_tokens // GRID_W
    inv_freq = ROPE_BASE ** (-jnp.arange(0, ROPE_AXIS_DIM, 2, dtype=F32) / ROPE_AXIS_DIM)
    ang_r = jnp.arange(rows, dtype=F32)[:, None] * inv_freq[None, :]
    ang_c = jnp.arange(GRID_W, dtype=F32)[:, None] * inv_freq[None, :]
    nf = inv_freq.shape[0]
    by_row = lambda tab: jnp.broadcast_to(tab[:, None, :], (rows, GRID_W, nf)).reshape(n_tokens, nf)
    by_col = lambda tab: jnp.broadcast_to(tab[None, :, :], (rows, GRID_W, nf)).reshape(n_tokens, nf)
    cr, sr = by_row(jnp.cos(ang_r)), by_row(jnp.sin(ang_r))
    cc, sc = by_col(jnp.cos(ang_c)), by_col(jnp.sin(ang_c))
    cos = jnp.concatenate([cr, cr, cc, cc], axis=1)
    sin = jnp.concatenate([-sr, sr, -sc, sc], axis=1)
    return jnp.tile(cos, (1, 2)), jnp.tile(sin, (1, 2))


def _attention_kernel(*refs, tq, tk, sub, n_lat):
    if n_lat:
        kmax_ref, qt_ref, kc_ref, vct_ref, k_ref, vt_ref, o_ref, qs_ref, acc_ref, m_ref = refs
    else:
        qt_ref, kc_ref, vct_ref, o_ref, acc_ref = refs
    cols = ATT_GROUP * tq
    qt = qt_ref[...]
    qst = jnp.concatenate(
        [qt[g * ATT_HEAD_DIM:(g + 1) * ATT_HEAD_DIM, :] for g in range(ATT_GROUP)], axis=1)
    pad = jnp.zeros((LANES - ATT_HEAD_DIM, cols), BF16)
    s = _dot(kc_ref[0], jnp.concatenate([qst, pad], axis=0))
    mt = jnp.max(s, axis=0, keepdims=True).astype(BF16)
    mtf = mt.astype(F32)
    acc_ref[...] = _dot(vct_ref[...], jnp.exp2(s - mtf).astype(BF16))

    if n_lat:
        row = lax.broadcasted_iota(jnp.int32, pad.shape, 0)
        qs_ref[...] = jnp.concatenate([qst, jnp.where(row == 0, mtf, 0.0).astype(BF16)], axis=0)
        qf = qst.astype(F32)
        qn = jnp.sqrt(jnp.sum(qf * qf, axis=0, keepdims=True))
        fast = jnp.max(qn * kmax_ref[pl.program_id(0)] - mtf) <= EXP2_CAP

        def tile(j):
            off = pl.multiple_of(j * tk, tk)
            return k_ref[0, pl.ds(off, tk), :], vt_ref[:, pl.ds(off, tk)]

        @pl.when(fast)
        def _():
            def body(j, carry):
                k, vt = tile(j)
                qs = qs_ref[...]
                part = None
                for c in range(tk // sub):
                    p = jnp.exp2(_dot(k[c * sub:(c + 1) * sub], qs)).astype(BF16)
                    pv = _dot(vt[:, c * sub:(c + 1) * sub], p)
                    part = pv if part is None else part + pv
                acc_ref[...] += part
                return carry

            lax.fori_loop(0, n_lat, body, 0)

        @pl.when(jnp.logical_not(fast))
        def _():
            m_ref[...] = jnp.zeros_like(m_ref)

            def body(j, carry):
                k, vt = tile(j)
                s = _dot(k, qs_ref[...])
                m_old = m_ref[...]
                m_new = jnp.maximum(m_old, jnp.max(s, axis=0, keepdims=True))
                p = jnp.exp2(s - m_new).astype(BF16)
                acc_ref[...] = jnp.exp2(m_old - m_new) * acc_ref[...] + _dot(vt, p)
                m_ref[...] = m_new
                return carry

            lax.fori_loop(0, n_lat, body, 0)

    acc = acc_ref[...]
    o = acc[:ATT_HEAD_DIM] / acc[ATT_HEAD_DIM:ATT_HEAD_DIM + 1]
    o_ref[...] = jnp.concatenate(
        [o[:, g * tq:(g + 1) * tq].T for g in range(ATT_GROUP)], axis=1).astype(o_ref.dtype)


def _attention(qt, kc, vct, kmax=None, k=None, vt=None, *, tq, tk=0, sub=0):
    t = qt.shape[1]
    lc = kc.shape[1]
    gw = ATT_GROUP * ATT_HEAD_DIM
    cols = ATT_GROUP * tq
    in_specs = [pl.BlockSpec((gw, tq), lambda h, i: (h, i)),
                pl.BlockSpec((1, lc, LANES), lambda h, i: (h, 0, 0)),
                pl.BlockSpec((ATT_VROWS, lc), lambda h, i: (h, 0))]
    args = [qt, kc, vct]
    scratch = [pltpu.VMEM((ATT_VROWS, cols), F32)]
    n_lat = 0
    if k is not None:
        lk = k.shape[1]
        n_lat = lk // tk
        in_specs = ([pl.BlockSpec(memory_space=pltpu.SMEM)] + in_specs
                    + [pl.BlockSpec((1, lk, LANES), lambda h, i: (h, 0, 0)),
                       pl.BlockSpec((ATT_VROWS, lk), lambda h, i: (h, 0))])
        args = [kmax] + args + [k, vt]
        scratch = [pltpu.VMEM((LANES, cols), BF16)] + scratch + [pltpu.VMEM((1, cols), F32)]
    return pl.pallas_call(
        functools.partial(_attention_kernel, tq=tq, tk=tk, sub=sub, n_lat=n_lat),
        out_shape=jax.ShapeDtypeStruct((t, ATT_WIDTH), BF16),
        grid=(ATT_KV_HEADS, t // tq),
        in_specs=in_specs,
        out_specs=pl.BlockSpec((tq, gw), lambda h, i: (i, h)),
        scratch_shapes=scratch,
        compiler_params=_cparams("parallel", "parallel"),
        name="attention_lat" if n_lat else "attention_ctx",
    )(*args)


def _ret_tables(lg_f, lg_b):
    pos = jnp.arange(RET_CHUNK, dtype=F32)
    rel = pos[:, None] - pos[None, :]
    mask = jnp.where(rel > 0, jnp.exp(lg_f[:, None, None] * jnp.maximum(rel, 0.0)),
                     jnp.where(rel < 0, jnp.exp(lg_b[:, None, None] * jnp.maximum(-rel, 0.0)), 2.0))
    zeta = jnp.stack([jnp.exp(lg_f[:, None] * (RET_CHUNK - 1 - pos)[None, :]),
                      jnp.exp(lg_b[:, None] * pos[None, :])])
    xi = jnp.stack([jnp.exp(lg_f[:, None] * (pos + 1)[None, :]),
                    jnp.exp(lg_b[:, None] * (RET_CHUNK - pos)[None, :])])
    chunk_decay = jnp.stack([jnp.exp(lg_f * RET_CHUNK), jnp.exp(lg_b * RET_CHUNK)])
    zeta = jnp.repeat(zeta, RET_HEAD_DIM, axis=1).transpose(0, 2, 1)
    xi = jnp.repeat(xi, RET_HEAD_DIM, axis=1).transpose(0, 2, 1)
    chunk_decay = jnp.repeat(chunk_decay, RET_HEAD_DIM, axis=1)[:, :, None]
    return mask, zeta, xi, chunk_decay


def _ret_state_kernel(k_ref, v_ref, zeta_ref, cd_ref, s0_ref, sprev_ref, sfin_ref, s_ref, *, cps):
    d = pl.program_id(0)

    @pl.when(pl.program_id(1) == 0)
    def _():
        s_ref[...] = s0_ref[0]

    zeta = zeta_ref[0] * (RET_HEAD_DIM ** -0.5)
    cd = cd_ref[0]
    us = []
    for c in range(cps):
        rows = slice(c * RET_CHUNK, (c + 1) * RET_CHUNK)
        kz = (k_ref[rows, :] * zeta).astype(BF16)
        v = v_ref[rows, :].astype(BF16)
        heads = []
        for h in range(RET_HEADS):
            sl = slice(h * RET_HEAD_DIM, (h + 1) * RET_HEAD_DIM)
            heads.append(lax.dot_general(kz[:, sl], v[:, sl], (((0,), (0,)), ((), ())),
                                         preferred_element_type=F32))
        us.append(jnp.concatenate(heads, axis=0))

    def scan(order):
        s = s_ref[...]
        for c in order:
            sprev_ref[0, c] = s
            s = cd * s + us[c]
        s_ref[...] = s
        sfin_ref[0] = s

    @pl.when(d == 0)
    def _():
        scan(range(cps))

    @pl.when(d == 1)
    def _():
        scan(reversed(range(cps)))


def _retention_states(p_ret, zeta, chunk_decay, s0):
    nc = p_ret.shape[0] // RET_CHUNK
    cps = min(nc, RET_STATE_CHUNKS)
    ng = nc // cps
    rows = cps * RET_CHUNK

    def group(d, i):
        return i + d * (ng - 1 - 2 * i)

    return pl.pallas_call(
        functools.partial(_ret_state_kernel, cps=cps),
        out_shape=(jax.ShapeDtypeStruct((2, nc, RET_WIDTH, RET_HEAD_DIM), F32),
                   jax.ShapeDtypeStruct((2, RET_WIDTH, RET_HEAD_DIM), F32)),
        grid=(2, ng),
        in_specs=[pl.BlockSpec((rows, RET_WIDTH), lambda d, i: (group(d, i), 1)),
                  pl.BlockSpec((rows, RET_WIDTH), lambda d, i: (group(d, i), 2)),
                  pl.BlockSpec((1, RET_CHUNK, RET_WIDTH), lambda d, i: (d, 0, 0)),
                  pl.BlockSpec((1, RET_WIDTH, 1), lambda d, i: (d, 0, 0)),
                  pl.BlockSpec((1, RET_WIDTH, RET_HEAD_DIM), lambda d, i: (d, 0, 0))],
        out_specs=(pl.BlockSpec((1, cps, RET_WIDTH, RET_HEAD_DIM), lambda d, i: (d, group(d, i), 0, 0)),
                   pl.BlockSpec((1, RET_WIDTH, RET_HEAD_DIM), lambda d, i: (d, 0, 0))),
        scratch_shapes=[pltpu.VMEM((RET_WIDTH, RET_HEAD_DIM), F32)],
        compiler_params=_cparams("arbitrary", "arbitrary"),
        name="retention_states",
    )(p_ret, p_ret, zeta, chunk_decay, s0)


def _ret_out_kernel(q_ref, k_ref, v_ref, g_ref, mask_ref, xi_ref, sf_ref, sb_ref, y_ref, *, cpo):
    for c in range(cpo):
        rows = slice(c * RET_CHUNK, (c + 1) * RET_CHUNK)
        q = q_ref[rows, :]
        k = (k_ref[rows, :] * (RET_HEAD_DIM ** -0.5)).astype(BF16)
        v = v_ref[rows, :].astype(BF16)
        qb = q.astype(BF16)
        qf = (q * xi_ref[0]).astype(BF16)
        qr = (q * xi_ref[1]).astype(BF16)
        sf = sf_ref[0, c].astype(BF16)
        sb = sb_ref[0, c].astype(BF16)
        ys = []
        for h in range(RET_HEADS):
            sl = slice(h * RET_HEAD_DIM, (h + 1) * RET_HEAD_DIM)
            scores = lax.dot_general(qb[:, sl], k[:, sl], (((1,), (1,)), ((), ())),
                                     preferred_element_type=F32)
            y = _dot((scores * mask_ref[h]).astype(BF16), v[:, sl])
            y = y + _dot(qf[:, sl], sf[sl, :]) + _dot(qr[:, sl], sb[sl, :])
            mu = jnp.mean(y, axis=-1, keepdims=True)
            yc = y - mu
            var = jnp.mean(yc * yc, axis=-1, keepdims=True)
            ys.append(yc * lax.rsqrt(var + EPS))
        y_ref[rows, :] = (_silu(g_ref[rows, :]) * jnp.concatenate(ys, axis=1)).astype(y_ref.dtype)


def _retention_outputs(p_ret, mask, xi, sprev):
    nc = p_ret.shape[0] // RET_CHUNK
    cpo = min(nc, RET_OUT_CHUNKS)
    rows = cpo * RET_CHUNK
    col = lambda c: pl.BlockSpec((rows, RET_WIDTH), lambda i: (i, c))
    return pl.pallas_call(
        functools.partial(_ret_out_kernel, cpo=cpo),
        out_shape=jax.ShapeDtypeStruct((p_ret.shape[0], RET_WIDTH), BF16),
        grid=(nc // cpo,),
        in_specs=[col(0), col(1), col(2), col(3),
                  pl.BlockSpec((RET_HEADS, RET_CHUNK, RET_CHUNK), lambda i: (0, 0, 0)),
                  pl.BlockSpec((2, RET_CHUNK, RET_WIDTH), lambda i: (0, 0, 0)),
                  pl.BlockSpec((1, cpo, RET_WIDTH, RET_HEAD_DIM), lambda i: (0, i, 0, 0)),
                  pl.BlockSpec((1, cpo, RET_WIDTH, RET_HEAD_DIM), lambda i: (1, i, 0, 0))],
        out_specs=pl.BlockSpec((rows, RET_WIDTH), lambda i: (i, 0)),
        compiler_params=_cparams("parallel"),
        name="retention_outputs",
    )(p_ret, p_ret, p_ret, p_ret, mask, xi, sprev, sprev)


def _hy_pre_kernel(p_ref, prev_ref, next_ref, w_ref, b_ref, x0_ref, z_ref):
    i = pl.program_id(0)
    x = p_ref[...]
    tm = x.shape[0]
    row = lax.broadcasted_iota(jnp.int32, x.shape, 0)
    prev_row = jnp.where(i == 0, 0.0, prev_ref[7:8, :])
    next_row = jnp.where(i == pl.num_programs(0) - 1, 0.0, next_ref[0:1, :])
    up = jnp.where(row == 0, prev_row, pltpu.roll(x, 1, 0))
    dn = jnp.where(row == tm - 1, next_row, pltpu.roll(x, tm - 1, 0))
    w = w_ref[...]
    u = up * w[0:1] + x * w[1:2] + dn * w[2:3] + b_ref[...]
    x0_ref[...] = u[:, :HY_WIDTH]
    z_ref[...] = u[:, HY_WIDTH:2 * HY_WIDTH] * u[:, 2 * HY_WIDTH:]


def _hyena_pre(p_hy, conv_w, conv_b):
    t = p_hy.shape[0]
    tm = min(t, 512)
    nb8 = tm // 8
    last8 = t // 8 - 1
    return pl.pallas_call(
        _hy_pre_kernel,
        out_shape=(jax.ShapeDtypeStruct((t, HY_WIDTH), F32),
                   jax.ShapeDtypeStruct((t, HY_WIDTH), F32)),
        grid=(t // tm,),
        in_specs=[pl.BlockSpec((tm, HY_COLS), lambda i: (i, 0)),
                  pl.BlockSpec((8, HY_COLS), lambda i: (jnp.maximum(i * nb8 - 1, 0), 0)),
                  pl.BlockSpec((8, HY_COLS), lambda i: (jnp.minimum((i + 1) * nb8, last8), 0)),
                  pl.BlockSpec((3, HY_COLS), lambda i: (0, 0)),
                  pl.BlockSpec((1, HY_COLS), lambda i: (0, 0))],
        out_specs=(pl.BlockSpec((tm, HY_WIDTH), lambda i: (i, 0)),
                   pl.BlockSpec((tm, HY_WIDTH), lambda i: (i, 0))),
        compiler_params=_cparams("parallel"),
        name="hyena_pre",
    )(p_hy, p_hy, p_hy, conv_w, conv_b.reshape(1, HY_COLS))


def _hy_filter_kernel(feat_ref, w1_ref, b1_ref, w2_ref, b2_ref, w3_ref, b3_ref, freq_ref,
                      delta_ref, taps_ref, asum_ref, *, seq_len):
    i = pl.program_id(0)
    feat = feat_ref[...]
    tm = feat.shape[0]
    half = tm // 2
    freq = freq_ref[...]
    packed = jnp.concatenate([feat[:half], feat[half:]], axis=1)
    h = jnp.sin(freq * (_dot3(packed, w1_ref[...]) + b1_ref[...]))
    h = jnp.sin(freq * (_dot3(h, w2_ref[...]) + b2_ref[...]))
    h = _dot3(h, w3_ref[0])
    h = jnp.concatenate([h[:, :HY_WIDTH], h[:, HY_WIDTH:]], axis=0) + b3_ref[0]
    j = i * tm + lax.broadcasted_iota(jnp.int32, (tm, HY_WIDTH), 0)
    taps = jnp.where(j == seq_len, 0.0, h * jnp.exp(-feat[:, 0:1] * delta_ref[...]))
    taps_ref[...] = taps

    @pl.when(i == 0)
    def _():
        asum_ref[...] = jnp.zeros_like(asum_ref)

    asum_ref[...] += jnp.sum(jnp.abs(taps).reshape(tm // SUBLANES, SUBLANES, HY_WIDTH), axis=0)


def _position_features(seq_len):
    lo = 256
    hi = seq_len // lo + 1
    bands = np.linspace(1e-4, HY_POS_BANDS - 1, HY_POS_BANDS).astype(np.float32).astype(np.float64)
    ang_hi = (2.0 * np.pi / seq_len) * lo * np.arange(hi)[:, None] * bands[None, :]
    ang_lo = (2.0 * np.pi / seq_len) * np.arange(lo)[:, None] * bands[None, :]
    tab = lambda a: jnp.asarray(a, F32)
    ch, sh = tab(np.cos(ang_hi))[:, None, :], tab(np.sin(ang_hi))[:, None, :]
    cl, sl = tab(np.cos(ang_lo))[None, :, :], tab(np.sin(ang_lo))[None, :, :]
    npos = seq_len + 1
    cos = (ch * cl - sh * sl).reshape(hi * lo, HY_POS_BANDS)[:npos]
    sin = (sh * cl + ch * sl).reshape(hi * lo, HY_POS_BANDS)[:npos]
    c_end, s_end = tab(np.cos(2.0 * np.pi * bands))[None, :], tab(np.sin(2.0 * np.pi * bands))[None, :]
    cos_b = c_end * cos[1:seq_len] + s_end * sin[1:seq_len]
    sin_b = s_end * cos[1:seq_len] - c_end * sin[1:seq_len]
    lag = jnp.concatenate([jnp.arange(npos, dtype=F32), jnp.arange(seq_len - 1, 0, -1, dtype=F32)])
    t = lag / max(seq_len - 1, 1)
    cos = jnp.concatenate([cos, cos_b], axis=0)
    sin = jnp.concatenate([sin, sin_b], axis=0)
    return jnp.concatenate([t[:, None], cos, -sin,
                            jnp.zeros((2 * seq_len, LANES - 1 - 2 * HY_POS_BANDS), F32)], axis=-1)


def _hyena_taps(seq_len, w1, b1, w2, b2, w3, b3, freq):
    n = 2 * seq_len
    feat = _position_features(seq_len)
    w1p = jnp.concatenate([w1, jnp.zeros((LANES - w1.shape[0], w1.shape[1]), F32)], axis=0)
    pair = lambda w: jnp.concatenate(
        [jnp.concatenate([w, jnp.zeros_like(w)], axis=1),
         jnp.concatenate([jnp.zeros_like(w), w], axis=1)], axis=0)
    twice = lambda v: jnp.tile(v.reshape(1, -1), (1, 2))
    w3_dirs = jnp.stack([pair(w3[:, :HY_WIDTH]), pair(w3[:, HY_WIDTH:])])
    b3_dirs = b3.reshape(2, 1, HY_WIDTH)
    deltas = jnp.abs(jnp.linspace(math.log(HY_DECAY_TARGET) / HY_LONG_DECAY_PCT,
                                  math.log(HY_DECAY_TARGET) / HY_SHORT_DECAY_PCT,
                                  HY_WIDTH, dtype=F32)).reshape(1, HY_WIDTH)
    tm = min(seq_len, 1024)
    per_dir = seq_len // tm
    hid2 = 2 * w1.shape[1]
    const = lambda shape: pl.BlockSpec(shape, lambda i: (0, 0))
    by_dir = lambda shape: pl.BlockSpec(shape, lambda i: (i // per_dir, 0, 0))
    return pl.pallas_call(
        functools.partial(_hy_filter_kernel, seq_len=seq_len),
        out_shape=(jax.ShapeDtypeStruct((n, HY_WIDTH), F32),
                   jax.ShapeDtypeStruct((8, HY_WIDTH), F32)),
        grid=(n // tm,),
        in_specs=[pl.BlockSpec((tm, LANES), lambda i: (i, 0)),
                  const((2 * LANES, hid2)), const((1, hid2)), const((hid2, hid2)), const((1, hid2)),
                  by_dir((1, hid2, 2 * HY_WIDTH)), by_dir((1, 1, HY_WIDTH)), const((1, hid2)),
                  const((1, HY_WIDTH))],
        out_specs=(pl.BlockSpec((tm, HY_WIDTH), lambda i: (i, 0)), const((8, HY_WIDTH))),
        compiler_params=_cparams("arbitrary"),
        name="hyena_taps",
    )(feat, pair(w1p), twice(b1), pair(w2), twice(b2), w3_dirs, b3_dirs, twice(freq), deltas)


FFT_N2 = 256
FFT_K1_PER_STEP = 4


def _dft_tables(n_total, n1_in):
    n1 = n_total // FFT_N2
    k1 = np.arange(n1)
    a = 2.0 * np.pi * np.outer(k1, np.arange(n1_in)) / n1
    lvl1 = np.concatenate([np.cos(a), -np.sin(a)], axis=0)
    n2 = np.arange(FFT_N2)
    th = 2.0 * np.pi * np.outer(n2, n2) / FFT_N2
    fc, fs = np.cos(th), np.sin(th)
    g = np.block([[fc, fs], [-fs, fc]])
    ph = 2.0 * np.pi * np.outer(k1, n2) / n_total
    tw = np.stack([np.cos(ph), np.sin(ph)])[..., None]
    return (jnp.asarray(lvl1, F32), jnp.asarray(g, F32), jnp.asarray(g.T, F32), jnp.asarray(tw, F32))


def _inverse_lvl1_table(n_total, n1_out):
    n1 = n_total // FFT_N2
    a = 2.0 * np.pi * np.outer(np.arange(n1_out), np.arange(n1)) / n1
    return jnp.asarray(np.concatenate([np.cos(a), -np.sin(a)], axis=1), F32)


def _lvl2_kernel(b_ref, tw_ref, gh_ref, gl_ref, *rest, inverse):
    for j in range(b_ref.shape[1]):
        c = tw_ref[0, j]
        s = tw_ref[1, j]
        br = b_ref[0, j]
        bi = b_ref[1, j]
        d = jnp.concatenate([c * br + s * bi, c * bi - s * br], axis=0)
        dh, dl = _split(d)
        x = _dot(gh_ref[...], dh) + _dot(gh_ref[...], dl) + _dot(gl_ref[...], dh)
        if not inverse:
            (o_ref,) = rest
            o_ref[0, j] = x[:FFT_N2]
            o_ref[1, j] = x[FFT_N2:]
            continue
        h_ref, gth_ref, gtl_ref, o_ref = rest
        xr, xi = x[:FFT_N2], x[FFT_N2:]
        hr, hi = h_ref[0, j], h_ref[1, j]
        y = jnp.concatenate([xr * hr - xi * hi, xr * hi + xi * hr], axis=0)
        yh, yl = _split(y)
        cc = _dot(gth_ref[...], yh) + _dot(gth_ref[...], yl) + _dot(gtl_ref[...], yh)
        cr, ci = cc[:FFT_N2], cc[FFT_N2:]
        o_ref[0, j] = c * cr - s * ci
        o_ref[1, j] = c * ci + s * cr


def _level2(b, tw, g, h=None, gt=None):
    _, n1, n2, ch = b.shape
    kb = FFT_K1_PER_STEP
    blk = pl.BlockSpec((2, kb, n2, ch), lambda i: (0, i, 0, 0))
    const = pl.BlockSpec((2 * n2, 2 * n2), lambda i: (0, 0))
    gh, gl = _split(g)
    in_specs = [blk, pl.BlockSpec((2, kb, n2, 1), lambda i: (0, i, 0, 0)), const, const]
    args = [b, tw, gh, gl]
    if h is not None:
        gth, gtl = _split(gt)
        in_specs += [blk, const, const]
        args += [h, gth, gtl]
    return pl.pallas_call(
        functools.partial(_lvl2_kernel, inverse=h is not None),
        out_shape=jax.ShapeDtypeStruct(b.shape, F32),
        grid=(n1 // kb,),
        in_specs=in_specs,
        out_specs=blk,
        compiler_params=_cparams("parallel"),
        name="hyena_fft_level2" + ("_conv" if h is not None else ""),
    )(*args)


def _hyena_long_conv(z, taps):
    seq_len, ch = z.shape
    n = 2 * seq_len
    n1 = n // FFT_N2
    lvl1_z, g, gt, tw = _dft_tables(n, n1 // 2)
    lvl1_t = _dft_tables(n, n1)[0]
    inv1 = _inverse_lvl1_table(n, n1 // 2)
    cols = FFT_N2 * ch
    tn = min(cols, 4096)
    hb = _mm3(lvl1_t, taps.reshape(n1, cols), tn, "hyena_fft_level1_taps")
    hspec = _level2(hb.reshape(2, n1, FFT_N2, ch), tw, g)
    zb = _mm3(lvl1_z, z.reshape(n1 // 2, cols), tn, "hyena_fft_level1_z")
    cspec = _level2(zb.reshape(2, n1, FFT_N2, ch), tw, g, hspec, gt)
    conv = _mm3(inv1, cspec.reshape(2 * n1, cols), tn, "hyena_fft_inverse_level1")
    return conv.reshape(seq_len, ch)


def _small_conv_kernel(z_ref, taps_ref, fc_ref, fs_ref, o_ref):
    seq_len = z_ref.shape[0]
    fc = fc_ref[...]
    fs = fs_ref[...]
    z = z_ref[...]
    taps = taps_ref[...]
    zr = _dot3(fc[:, :seq_len], z)
    zi = -_dot3(fs[:, :seq_len], z)
    hr = _dot3(fc, taps)
    hi = -_dot3(fs, taps)
    yr = zr * hr - zi * hi
    yi = zr * hi + zi * hr
    o_ref[...] = _dot3(fc[:seq_len, :], yr) - _dot3(fs[:seq_len, :], yi)


def _hyena_small_conv(z, taps):
    seq_len, ch = z.shape
    n = 2 * seq_len
    th = 2.0 * np.pi * np.outer(np.arange(n), np.arange(n)) / n
    full = lambda shape: pl.BlockSpec(shape, lambda: (0, 0))
    return pl.pallas_call(
        _small_conv_kernel,
        out_shape=jax.ShapeDtypeStruct((seq_len, ch), F32),
        in_specs=[full((seq_len, ch)), full((n, ch)), full((n, n)), full((n, n))],
        out_specs=full((seq_len, ch)),
        compiler_params=pltpu.CompilerParams(vmem_limit_bytes=VMEM_LIMIT),
        name="hyena_small_conv",
    )(z, taps, jnp.asarray(np.cos(th), F32), jnp.asarray(np.sin(th), F32))


def _hy_post_kernel(x0_ref, z_ref, conv_ref, asum_ref, bias_ref, y_ref, *, n_total):
    norm = jnp.sum(asum_ref[...], axis=0, keepdims=True) + EPS
    conv = conv_ref[...] * (1.0 / (n_total * norm))
    y_ref[...] = (x0_ref[...] * (conv + bias_ref[...] * z_ref[...])).astype(y_ref.dtype)


def _hyena_post(x0, z, conv, asum, bias):
    t = x0.shape[0]
    tm = min(t, 1024)
    row = pl.BlockSpec((tm, HY_WIDTH), lambda i: (i, 0))
    return pl.pallas_call(
        functools.partial(_hy_post_kernel, n_total=2 * t),
        out_shape=jax.ShapeDtypeStruct((t, HY_WIDTH), BF16),
        grid=(t // tm,),
        in_specs=[row, row, row, pl.BlockSpec((8, HY_WIDTH), lambda i: (0, 0)),
                  pl.BlockSpec((1, HY_WIDTH), lambda i: (0, 0))],
        out_specs=row,
        compiler_params=_cparams("parallel"),
        name="hyena_post",
    )(x0, z, conv, asum, bias.reshape(1, HY_WIDTH))


def _hyena_mixer(p_hy, conv_w, conv_b, filt, bias):
    seq_len = p_hy.shape[0]
    x0, z = _hyena_pre(p_hy, conv_w, conv_b)
    taps, asum = _hyena_taps(seq_len, *filt)
    if 2 * seq_len // FFT_N2 >= 16:
        conv = _hyena_long_conv(z, taps)
    else:
        conv = _hyena_small_conv(z, taps)
    return _hyena_post(x0, z, conv, asum, bias)


def _outproj_kernel(hy_ref, ret_ref, att_ref, w_ref, x_ref, gate_ref, o_ref):
    y = (_dot(hy_ref[...], w_ref[:HY_WIDTH, :])
         + _dot(ret_ref[...], w_ref[HY_WIDTH:HY_WIDTH + RET_WIDTH, :])
         + _dot(att_ref[...], w_ref[HY_WIDTH + RET_WIDTH:, :]))
    o_ref[...] = x_ref[...] + gate_ref[...] * y


def _out_projection(y_hy, y_ret, y_att, w_bf16, x, gate):
    t = x.shape[0]
    tm = min(t, 512)
    row = lambda w: pl.BlockSpec((tm, w), lambda i: (i, 0))
    return pl.pallas_call(
        _outproj_kernel,
        out_shape=jax.ShapeDtypeStruct((t, D_MODEL), F32),
        grid=(t // tm,),
        in_specs=[row(HY_WIDTH), row(RET_WIDTH), row(ATT_WIDTH),
                  pl.BlockSpec((MIX_WIDTH, D_MODEL), lambda i: (0, 0)),
                  row(D_MODEL), pl.BlockSpec((1, D_MODEL), lambda i: (0, 0))],
        out_specs=row(D_MODEL),
        compiler_params=_cparams("parallel"),
        name="out_projection",
    )(y_hy, y_ret, y_att, w_bf16, x, gate)


def _swiglu(x, wg_ref, wu_ref, wd_ref):
    hidden = wg_ref.shape[1]
    y = None
    for lo in range(0, hidden, SWIGLU_CHUNK):
        hi = min(lo + SWIGLU_CHUNK, hidden)
        a = _silu(_dot(x, wg_ref[:, lo:hi])) * _dot(x, wu_ref[:, lo:hi])
        part = _dot(a.astype(BF16), wd_ref[lo:hi, :])
        y = part if y is None else y + part
    return y


def _ffn_kernel(x_ref, g_ref, sh_ref, sc_ref, gate_ref, wg_ref, wu_ref, wd_ref, o_ref):
    x = x_ref[...]
    h = _norm_mod(x, g_ref[...], sh_ref[...], sc_ref[...]).astype(BF16)
    o_ref[...] = x + gate_ref[...] * _swiglu(h, wg_ref, wu_ref, wd_ref)


def _dense_ffn(x, g, shift, scale, gate, wg, wu, wd):
    t = x.shape[0]
    tm = min(t, 512)
    vec = pl.BlockSpec((1, D_MODEL), lambda i: (0, 0))
    resident = lambda shape: pl.BlockSpec(shape, lambda i: (0, 0), pipeline_mode=pl.Buffered(1))
    return pl.pallas_call(
        _ffn_kernel,
        out_shape=jax.ShapeDtypeStruct((t, D_MODEL), F32),
        grid=(t // tm,),
        in_specs=[pl.BlockSpec((tm, D_MODEL), lambda i: (i, 0)), vec, vec, vec, vec,
                  resident((D_MODEL, FFN_HIDDEN)), resident((D_MODEL, FFN_HIDDEN)),
                  resident((FFN_HIDDEN, D_MODEL))],
        out_specs=pl.BlockSpec((tm, D_MODEL), lambda i: (i, 0)),
        compiler_params=_cparams("parallel"),
        name="dense_ffn",
    )(x, g, shift, scale, gate, wg, wu, wd)


def _moe_route_kernel(x_ref, g_ref, sh_ref, sc_ref, rw_ref, rb_ref, h_ref, info_ref):
    h = _norm_mod(x_ref[...], g_ref[...], sh_ref[...], sc_ref[...])
    _to_row_tiles(h_ref, h)
    logits = _dot3(h, rw_ref[...]) + rb_ref[...]
    lane = lax.broadcasted_iota(jnp.int32, logits.shape, 1)
    v1 = jnp.max(logits, axis=-1, keepdims=True)
    i1 = jnp.min(jnp.where(logits == v1, lane, LANES), axis=-1, keepdims=True)
    rest = jnp.where(lane == i1, -jnp.inf, logits)
    v2 = jnp.max(rest, axis=-1, keepdims=True)
    i2 = jnp.min(jnp.where(rest == v2, lane, LANES), axis=-1, keepdims=True)
    e = jnp.exp(v2 - v1)
    g1 = 1.0 / (1.0 + e)
    g2 = e * g1
    info_ref[...] = jnp.where(lane == 0, i1.astype(F32),
                              jnp.where(lane == 1, i2.astype(F32),
                                        jnp.where(lane == 2, g1, jnp.where(lane == 3, g2, 0.0))))


def _moe_route(x, g, shift, scale, router_w, router_b):
    t = x.shape[0]
    tm = min(t, 512)
    rw = jnp.concatenate([router_w, jnp.zeros((D_MODEL, LANES - N_EXPERTS), F32)], axis=1)
    rb = jnp.concatenate([router_b, jnp.full((LANES - N_EXPERTS,), -jnp.inf, F32)]).reshape(1, LANES)
    vec = pl.BlockSpec((1, D_MODEL), lambda i: (0, 0))
    return pl.pallas_call(
        _moe_route_kernel,
        out_shape=(jax.ShapeDtypeStruct((t * SUBLANES, LANES), F32),
                   jax.ShapeDtypeStruct((t, LANES), F32)),
        grid=(t // tm,),
        in_specs=[pl.BlockSpec((tm, D_MODEL), lambda i: (i, 0)), vec, vec, vec,
                  pl.BlockSpec((D_MODEL, LANES), lambda i: (0, 0)),
                  pl.BlockSpec((1, LANES), lambda i: (0, 0))],
        out_specs=(pl.BlockSpec((tm * SUBLANES, LANES), lambda i: (i, 0)),
                   pl.BlockSpec((tm, LANES), lambda i: (i, 0))),
        compiler_params=_cparams("parallel"),
        name="moe_route",
    )(x, g, shift, scale, rw, rb)


def _to_row_tiles(ref, x):
    rows = x.shape[0]
    for k in range(D_MODEL // LANES):
        ref[pl.ds(k, rows, stride=SUBLANES), :] = x[:, k * LANES:(k + 1) * LANES]


def _from_row_tiles(ref, rows):
    return jnp.concatenate(
        [ref[pl.ds(k, rows, stride=SUBLANES), :] for k in range(D_MODEL // LANES)], axis=1)


def _row_copy(src_hbm, dst_vmem, sem, src_row, dst_row):
    src = pl.multiple_of(src_row * SUBLANES, SUBLANES)
    dst = pl.multiple_of(dst_row * SUBLANES, SUBLANES)
    return pltpu.make_async_copy(src_hbm.at[pl.ds(src, SUBLANES)], dst_vmem.at[pl.ds(dst, SUBLANES)], sem)


def _moe_expert_kernel(blk_e_ref, nused_ref, dest_ref, fill_ref, h_hbm, wg_ref, wu_ref, wd_ref,
                       y_ref, xbuf_ref, src_ref, sem):
    b = pl.program_id(0)
    n_used = nused_ref[0]

    @pl.when(b == 0)
    def _():
        def invert(t, carry):
            for j in range(TOP_K):
                src_ref[dest_ref[TOP_K * t + j]] = t
            return carry

        lax.fori_loop(0, dest_ref.shape[0] // TOP_K, invert, 0, unroll=DMA_UNROLL)
        for e in range(N_EXPERTS):
            def pad(d, carry):
                src_ref[d] = 0
                return carry

            lax.fori_loop(fill_ref[0, e], fill_ref[1, e], pad, 0)

    def issue_rows(blk, slot):
        def body(r, carry):
            _row_copy(h_hbm, xbuf_ref.at[slot], sem.at[slot], src_ref[blk * MOE_BLOCK + r], r).start()
            return carry

        lax.fori_loop(0, MOE_BLOCK, body, 0, unroll=DMA_UNROLL)

    def wait_rows(slot):
        def body(r, carry):
            _row_copy(h_hbm, xbuf_ref.at[slot], sem.at[slot], 0, r).wait()
            return carry

        lax.fori_loop(0, MOE_BLOCK, body, 0, unroll=DMA_UNROLL)

    @pl.when(b < n_used)
    def _():
        slot = b % 2

        @pl.when(b == 0)
        def _():
            issue_rows(0, 0)

        wait_rows(slot)

        @pl.when(b + 1 < n_used)
        def _():
            issue_rows(b + 1, 1 - slot)

        x = _from_row_tiles(xbuf_ref.at[slot], MOE_BLOCK).astype(BF16)
        _to_row_tiles(y_ref, _swiglu(x, wg_ref.at[0], wu_ref.at[0], wd_ref.at[0]))

    @pl.when(b >= n_used)
    def _():
        y_ref[...] = jnp.zeros_like(y_ref)


def _moe_experts(h, blk_expert, n_used, dest, fill, wg, wu, wd, n_blk):
    def w_index(b, be, nu, de, fi):
        return be[jnp.minimum(b, nu[0] - 1)], 0, 0

    grid_spec = pltpu.PrefetchScalarGridSpec(
        num_scalar_prefetch=4,
        grid=(n_blk,),
        in_specs=[pl.BlockSpec(memory_space=pl.ANY),
                  pl.BlockSpec((1, D_MODEL, FFN_HIDDEN), w_index),
                  pl.BlockSpec((1, D_MODEL, FFN_HIDDEN), w_index),
                  pl.BlockSpec((1, FFN_HIDDEN, D_MODEL), w_index)],
        out_specs=pl.BlockSpec((MOE_BLOCK * SUBLANES, LANES), lambda b, be, nu, de, fi: (b, 0)),
        scratch_shapes=[pltpu.VMEM((2, MOE_BLOCK * SUBLANES, LANES), F32),
                        pltpu.SMEM((n_blk * MOE_BLOCK,), jnp.int32),
                        pltpu.SemaphoreType.DMA((2,))],
    )
    return pl.pallas_call(
        _moe_expert_kernel,
        out_shape=jax.ShapeDtypeStruct((n_blk * MOE_BLOCK * SUBLANES, LANES), F32),
        grid_spec=grid_spec,
        compiler_params=_cparams("arbitrary"),
        name="moe_experts",
    )(blk_expert, n_used, dest, fill, h, wg, wu, wd)


def _moe_combine_kernel(dest_ref, x_ref, info_ref, gate_ref, ng_ref, y_hbm, o_ref,
                        y1_ref, y2_ref, sem, *, final_norm):
    i = pl.program_id(0)
    tm = x_ref.shape[0]
    slot = i % 2

    def issue_rows(blk, slot):
        def body(r, carry):
            a = 2 * (blk * tm + r)
            _row_copy(y_hbm, y1_ref.at[slot], sem.at[slot], dest_ref[a], r).start()
            _row_copy(y_hbm, y2_ref.at[slot], sem.at[slot], dest_ref[a + 1], r).start()
            return carry

        lax.fori_loop(0, tm, body, 0, unroll=DMA_UNROLL)

    def wait_rows(slot):
        def body(r, carry):
            _row_copy(y_hbm, y1_ref.at[slot], sem.at[slot], 0, r).wait()
            _row_copy(y_hbm, y2_ref.at[slot], sem.at[slot], 0, r).wait()
            return carry

        lax.fori_loop(0, tm, body, 0, unroll=DMA_UNROLL)

    @pl.when(i == 0)
    def _():
        issue_rows(0, 0)

    wait_rows(slot)

    @pl.when(i + 1 < pl.num_programs(0))
    def _():
        issue_rows(i + 1, 1 - slot)

    info = info_ref[...]
    y = (_from_row_tiles(y1_ref.at[slot], tm) * info[:, 2:3]
         + _from_row_tiles(y2_ref.at[slot], tm) * info[:, 3:4])
    x = x_ref[...] + gate_ref[...] * y
    if final_norm:
        ms = jnp.mean(x * x, axis=-1, keepdims=True)
        x = x * lax.rsqrt(ms + EPS) * ng_ref[...]
    o_ref[...] = x


def _moe_combine(dest, x, info, gate, norm_g, y, final_norm):
    t = x.shape[0]
    tm = min(t, 256)
    vec = pl.BlockSpec((1, D_MODEL), lambda i, d: (0, 0))
    grid_spec = pltpu.PrefetchScalarGridSpec(
        num_scalar_prefetch=1,
        grid=(t // tm,),
        in_specs=[pl.BlockSpec((tm, D_MODEL), lambda i, d: (i, 0)),
                  pl.BlockSpec((tm, LANES), lambda i, d: (i, 0)), vec, vec,
                  pl.BlockSpec(memory_space=pl.ANY)],
        out_specs=pl.BlockSpec((tm, D_MODEL), lambda i, d: (i, 0)),
        scratch_shapes=[pltpu.VMEM((2, tm * SUBLANES, LANES), F32),
                        pltpu.VMEM((2, tm * SUBLANES, LANES), F32),
                        pltpu.SemaphoreType.DMA((2,))],
    )
    return pl.pallas_call(
        functools.partial(_moe_combine_kernel, final_norm=final_norm),
        out_shape=jax.ShapeDtypeStruct((t, D_MODEL), F32),
        grid_spec=grid_spec,
        compiler_params=_cparams("arbitrary"),
        name="moe_combine",
    )(dest, x, info, gate, norm_g, y)


def _moe_layer(x, g, shift, scale, gate, router_w, router_b, wg, wu, wd, norm_g, final_norm):
    t = x.shape[0]
    n_asg = t * TOP_K
    h, info = _moe_route(x, g, shift, scale, router_w, router_b)
    expert = info[:, :TOP_K].astype(jnp.int32).reshape(-1)
    onehot = (expert[:, None] == jnp.arange(N_EXPERTS, dtype=jnp.int32)[None, :]).astype(jnp.int32)
    csum = jnp.cumsum(onehot, axis=0)
    counts = csum[-1]
    padded = (counts + MOE_BLOCK - 1) // MOE_BLOCK * MOE_BLOCK
    pad_end = jnp.cumsum(padded)
    pad_start = pad_end - padded
    dest = jnp.sum(onehot * (csum - 1 + pad_start[None, :]), axis=1).astype(jnp.int32)
    n_blk = -(-n_asg // MOE_BLOCK) + N_EXPERTS
    blk_start = jnp.arange(n_blk, dtype=jnp.int32) * MOE_BLOCK
    blk_expert = jnp.minimum(jnp.sum(blk_start[:, None] >= pad_end[None, :], axis=1),
                             N_EXPERTS - 1).astype(jnp.int32)
    n_used = (pad_end[-1:] // MOE_BLOCK).astype(jnp.int32)
    fill = jnp.stack([pad_start + counts, pad_end]).astype(jnp.int32)
    y = _moe_experts(h, blk_expert, n_used, dest, fill, wg, wu, wd, n_blk)
    return _moe_combine(dest, x, info, gate, norm_g, y, final_norm)


def _final_norm_kernel(x_ref, g_ref, o_ref):
    x = x_ref[...]
    ms = jnp.mean(x * x, axis=-1, keepdims=True)
    o_ref[...] = x * lax.rsqrt(ms + EPS) * g_ref[...]


def _final_norm(x, g):
    t = x.shape[0]
    tm = min(t, 1024)
    return pl.pallas_call(
        _final_norm_kernel,
        out_shape=jax.ShapeDtypeStruct((t, D_MODEL), F32),
        grid=(t // tm,),
        in_specs=[pl.BlockSpec((tm, D_MODEL), lambda i: (i, 0)),
                  pl.BlockSpec((1, D_MODEL), lambda i: (0, 0))],
        out_specs=pl.BlockSpec((tm, D_MODEL), lambda i: (i, 0)),
        compiler_params=_cparams("parallel"),
        name="final_norm",
    )(x, g)


def kernel(x, c, ctx, c_ctx, ada_w, ada_b, norm1_g, norm2_g, w_in, w_out, hy_conv_w, hy_conv_b, hy_filt_w1, hy_filt_b1, hy_filt_w2, hy_filt_b2, hy_filt_w3, hy_filt_b3, hy_filt_freq, hy_bias, ret_log_rate, attn_q_g, attn_k_g, ffn_w_gate, ffn_w_up, ffn_w_down, moe_router_w, moe_router_b, moe_w_gate, moe_w_up, moe_w_down, final_norm_g):
    assert x.shape[0] == 1 and c.shape[0] == 1
    seq_len = x.shape[1]
    x_lat = x[0]
    x_ctx = ctx[0]
    cos_t, sin_t = _rope_tables(seq_len)
    cvec = jnp.concatenate([c, c_ctx[None, :], jnp.zeros((6, D_MODEL), F32)], axis=0)
    cvec = _silu(cvec)
    row = lambda v: v.reshape(1, D_MODEL)
    zero_state = jnp.zeros((2, RET_WIDTH, RET_HEAD_DIM), F32)
    zero_tab = jnp.zeros((x_ctx.shape[0], LANES), F32)

    for l in range(DEPTH):
        last = l == DEPTH - 1
        mods = _ada_modulation(cvec, ada_w, ada_b, l).reshape(8, 6, D_MODEL)
        mod = [row(mods[0, i]) for i in range(6)]
        mod_c = [row(mods[1, i]) for i in range(6)]
        filt = (hy_filt_w1[l], hy_filt_b1[l], hy_filt_w2[l], hy_filt_b2[l],
                hy_filt_w3[l], hy_filt_b3[l], hy_filt_freq[l])
        lg_f = -jnp.exp(ret_log_rate[l, 0].astype(F32))
        lg_b = -jnp.exp(ret_log_rate[l, 1].astype(F32))
        mask, zeta, xi, chunk_decay = _ret_tables(lg_f, lg_b)
        w_in_l = w_in[l].astype(BF16)
        w_out_l = w_out[l].astype(BF16)
        n1 = row(norm1_g[l])
        n2 = row(norm2_g[l])

        p_hy, p_ret, p_att = _in_projection(x_lat, n1, mod[0], mod[1], w_in_l)
        pc_hy, pc_ret, pc_att = _in_projection(x_ctx, n1, mod_c[0], mod_c[1], w_in_l)

        sprev_c, s_ctx = _retention_states(pc_ret, zeta, chunk_decay, zero_state)
        qct, kc, vct, _ = _attention_prep(pc_att, zero_tab, zero_tab, attn_q_g[l], attn_k_g[l],
                                          rope=False)

        y_hy = _hyena_mixer(p_hy, hy_conv_w[l], hy_conv_b[l], filt, hy_bias[l])
        sprev, _ = _retention_states(p_ret, zeta, chunk_decay, s_ctx)
        y_ret = _retention_outputs(p_ret, mask, xi, sprev)
        qt, k_aug, vt, kmax_sq = _attention_prep(p_att, cos_t, sin_t, attn_q_g[l], attn_k_g[l],
                                                 rope=True)
        kmax = jnp.sqrt(jnp.max(kmax_sq[:, 0, ::ATT_HEAD_DIM], axis=0)) * KMAX_SLACK
        y_att = _attention(qt, kc, vct, kmax, k_aug, vt, tq=ATT_TQ, tk=ATT_TK, sub=ATT_SUB)

        if not last:
            yc_hy = _hyena_mixer(pc_hy, hy_conv_w[l], hy_conv_b[l], filt, hy_bias[l])
            yc_ret = _retention_outputs(pc_ret, mask, xi, sprev_c)
            yc_att = _attention(qct, kc, vct, tq=x_ctx.shape[0])
            x_ctx = _out_projection(yc_hy, yc_ret, yc_att, w_out_l, x_ctx, mod_c[2])
        x_lat = _out_projection(y_hy, y_ret, y_att, w_out_l, x_lat, mod[2])

        i = l // 2
        if l % 2 == 0:
            wg, wu, wd = (ffn_w_gate[i].astype(BF16), ffn_w_up[i].astype(BF16),
                          ffn_w_down[i].astype(BF16))
            x_lat = _dense_ffn(x_lat, n2, mod[3], mod[4], mod[5], wg, wu, wd)
            if not last:
                x_ctx = _dense_ffn(x_ctx, n2, mod_c[3], mod_c[4], mod_c[5], wg, wu, wd)
        else:
            wg, wu, wd = (moe_w_gate[i].astype(BF16), moe_w_up[i].astype(BF16),
                          moe_w_down[i].astype(BF16))
            fg = row(final_norm_g)
            x_lat = _moe_layer(x_lat, n2, mod[3], mod[4], mod[5], moe_router_w[i], moe_router_b[i],
                               wg, wu, wd, fg, final_norm=last)
            if not last:
                x_ctx = _moe_layer(x_ctx, n2, mod_c[3], mod_c[4], mod_c[5], moe_router_w[i],
                                   moe_router_b[i], wg, wu, wd, fg, final_norm=False)
    if DEPTH % 2 == 1:
        x_lat = _final_norm(x_lat, row(final_norm_g))
    return x_lat[None]
```

```python
import functools
import math

import numpy as np
import jax
import jax.numpy as jnp
from jax import lax
from jax.experimental import pallas as pl
from jax.experimental.pallas import tpu as pltpu

F32 = jnp.float32
BF16 = jnp.bfloat16

D_MODEL = 1024
DEPTH = 2
GRID_W = 64
EPS = 1e-6

HY_WIDTH = 256
HY_COLS = 3 * HY_WIDTH
HY_POS_BANDS = 16
HY_DECAY_TARGET = 1e-2
HY_SHORT_DECAY_PCT = 0.3
HY_LONG_DECAY_PCT = 1.5

RET_HEAD_DIM = 64
RET_HEADS = 4
RET_WIDTH = RET_HEADS * RET_HEAD_DIM
RET_COLS = 4 * RET_WIDTH
RET_CHUNK = 128

ATT_HEAD_DIM = 64
ATT_HEADS = 8
ATT_KV_HEADS = 2
ATT_GROUP = ATT_HEADS // ATT_KV_HEADS
ATT_WIDTH = ATT_HEADS * ATT_HEAD_DIM
ATT_KV_WIDTH = ATT_KV_HEADS * ATT_HEAD_DIM
ATT_COLS = ATT_WIDTH + 2 * ATT_KV_WIDTH
ROPE_AXIS_DIM = ATT_HEAD_DIM // 2
ROPE_BASE = 10000.0

MIX_WIDTH = HY_WIDTH + RET_WIDTH + ATT_WIDTH
IN_COLS = HY_COLS + RET_COLS + ATT_COLS

FFN_HIDDEN = 2816
N_EXPERTS = 8
TOP_K = 2
MOE_BLOCK = 512

LOG2_E = 1.4426950408889634
EXP2_CAP = 60.0
KMAX_SLACK = 1.0 + 2.0 ** -7
ATT_TQ = 512
ATT_TK = 4096
ATT_SUB = 512
ATT_VROWS = 80
RET_STATE_CHUNKS = 8
RET_OUT_CHUNKS = 4
SWIGLU_CHUNK = 1024
DMA_UNROLL = 8

LANES = 128
SUBLANES = 8
VMEM_LIMIT = 56 * 1024 * 1024


def _cparams(*sem):
    return pltpu.CompilerParams(dimension_semantics=sem, vmem_limit_bytes=VMEM_LIMIT)


def _dot(a, b):
    return jnp.dot(a, b, preferred_element_type=F32)


def _split(a):
    hi = a.astype(BF16)
    lo = (a - hi.astype(F32)).astype(BF16)
    return hi, lo


def _dot3(a, b):
    ah, al = _split(a)
    bh, bl = _split(b)
    return _dot(ah, bh) + _dot(al, bh) + _dot(ah, bl)


def _silu(x):
    return x * (1.0 / (1.0 + jnp.exp(-x)))


def _norm_mod(x, g, shift, scale):
    ms = jnp.mean(x * x, axis=-1, keepdims=True)
    h = x * lax.rsqrt(ms + EPS) * g
    return h * (1.0 + scale) + shift


def _ada_kernel(c_ref, w_ref, b_ref, o_ref):
    o_ref[...] = _dot3(c_ref[...], w_ref[0]) + b_ref[0]


def _ada_modulation(cc, w, b, layer):
    n = w.shape[2]
    tn = 1536
    return pl.pallas_call(
        _ada_kernel,
        out_shape=jax.ShapeDtypeStruct((8, n), F32),
        grid=(n // tn,),
        in_specs=[pl.BlockSpec((8, D_MODEL), lambda j: (0, 0)),
                  pl.BlockSpec((1, D_MODEL, tn), lambda j: (layer, 0, j)),
                  pl.BlockSpec((1, 1, tn), lambda j: (layer, 0, j))],
        out_specs=pl.BlockSpec((8, tn), lambda j: (0, j)),
        compiler_params=_cparams("parallel"),
        name="ada_modulation",
    )(cc, w, b.reshape(b.shape[0], 1, n))


def _inproj_kernel(x_ref, g_ref, sh_ref, sc_ref, w_ref, hy_ref, ret_ref, att_ref):
    h = _norm_mod(x_ref[...], g_ref[...], sh_ref[...], sc_ref[...])
    p = _dot(h.astype(BF16), w_ref[...])
    hy_ref[...] = p[:, :HY_COLS]
    ret_ref[...] = p[:, HY_COLS:HY_COLS + RET_COLS]
    att_ref[...] = p[:, HY_COLS + RET_COLS:]


def _in_projection(x, g, shift, scale, w_bf16):
    t = x.shape[0]
    tm = min(t, 512)
    vec = pl.BlockSpec((1, D_MODEL), lambda i: (0, 0))
    return pl.pallas_call(
        _inproj_kernel,
        out_shape=(jax.ShapeDtypeStruct((t, HY_COLS), F32),
                   jax.ShapeDtypeStruct((t, RET_COLS), F32),
                   jax.ShapeDtypeStruct((t, ATT_COLS), F32)),
        grid=(t // tm,),
        in_specs=[pl.BlockSpec((tm, D_MODEL), lambda i: (i, 0)), vec, vec, vec,
                  pl.BlockSpec((D_MODEL, IN_COLS), lambda i: (0, 0))],
        out_specs=(pl.BlockSpec((tm, HY_COLS), lambda i: (i, 0)),
                   pl.BlockSpec((tm, RET_COLS), lambda i: (i, 0)),
                   pl.BlockSpec((tm, ATT_COLS), lambda i: (i, 0))),
        compiler_params=_cparams("parallel"),
        name="in_projection",
    )(x, g, shift, scale, w_bf16)


def _head_mean_matrix(width):
    idx = np.arange(width) // ATT_HEAD_DIM
    return jnp.asarray((idx[:, None] == idx[None, :]).astype(np.float32) / ATT_HEAD_DIM, BF16)


def _head_rms(x, bd, g):
    x2h, x2l = _split(x * x)
    ms = _dot(x2h, bd) + _dot(x2l, bd)
    return x * lax.rsqrt(ms + EPS) * g


def _rope(x, cos, sin):
    n = x.shape[1]
    lane = lax.broadcasted_iota(jnp.int32, x.shape, 1)
    swapped = jnp.where((lane % 32) < 16, pltpu.roll(x, n - 16, 1), pltpu.roll(x, 16, 1))
    return x * cos + swapped * sin


def _attprep_kernel(p_ref, cos_ref, sin_ref, qg_ref, kg_ref, bdq_ref, bdk_ref,
                    qt_ref, k_ref, vt_ref, kmax_ref, *, rope):
    p = p_ref[...]
    q = _head_rms(p[:, :ATT_WIDTH], bdq_ref[...], qg_ref[...])
    k = _head_rms(p[:, ATT_WIDTH:ATT_WIDTH + ATT_KV_WIDTH], bdk_ref[...], kg_ref[...])
    if rope:
        cos = cos_ref[...]
        sin = sin_ref[...]
        q = jnp.concatenate(
            [_rope(q[:, j * LANES:(j + 1) * LANES], cos, sin) for j in range(ATT_WIDTH // LANES)],
            axis=1)
        k = _rope(k, cos, sin)
    qt_ref[...] = (q * (ATT_HEAD_DIM ** -0.5 * LOG2_E)).T.astype(BF16)
    tm = k.shape[0]
    lane = lax.broadcasted_iota(jnp.int32, k.shape, 1)
    minus_one_col = jnp.where(lane == ATT_HEAD_DIM, -1.0, 0.0)
    kb = k.astype(BF16)
    k_ref[0] = jnp.where(lane < ATT_HEAD_DIM, k, minus_one_col).astype(BF16)
    k_ref[1] = jnp.where(lane < ATT_HEAD_DIM, pltpu.roll(k, ATT_HEAD_DIM, 1), minus_one_col).astype(BF16)
    sq_hi, sq_lo = _split(kb.astype(F32) ** 2)
    norm_sq = (_dot(sq_hi, bdk_ref[...]) + _dot(sq_lo, bdk_ref[...])) * ATT_HEAD_DIM
    kmax_ref[0] = jnp.broadcast_to(jnp.max(norm_sq, axis=0, keepdims=True), (8, LANES))
    vt = p[:, ATT_WIDTH + ATT_KV_WIDTH:].T.astype(BF16)
    sub = lax.broadcasted_iota(jnp.int32, (ATT_VROWS - ATT_HEAD_DIM, tm), 0)
    ones_row = jnp.where(sub == 0, 1.0, 0.0).astype(BF16)
    for h in range(ATT_KV_HEADS):
        vt_ref[h * ATT_VROWS:h * ATT_VROWS + ATT_HEAD_DIM, :] = vt[h * ATT_HEAD_DIM:(h + 1) * ATT_HEAD_DIM]
        vt_ref[h * ATT_VROWS + ATT_HEAD_DIM:(h + 1) * ATT_VROWS, :] = ones_row


def _attention_prep(p_att, cos, sin, q_g, k_g, rope):
    t = p_att.shape[0]
    tm = min(t, 512)
    qg = jnp.tile(q_g, ATT_HEADS).reshape(1, ATT_WIDTH)
    kg = jnp.tile(k_g, ATT_KV_HEADS).reshape(1, ATT_KV_WIDTH)
    const = lambda shape: pl.BlockSpec(shape, lambda i: (0, 0))
    return pl.pallas_call(
        functools.partial(_attprep_kernel, rope=rope),
        out_shape=(jax.ShapeDtypeStruct((ATT_WIDTH, t), BF16),
                   jax.ShapeDtypeStruct((ATT_KV_HEADS, t, LANES), BF16),
                   jax.ShapeDtypeStruct((ATT_KV_HEADS * ATT_VROWS, t), BF16),
                   jax.ShapeDtypeStruct((t // tm, 8, LANES), F32)),
        grid=(t // tm,),
        in_specs=[pl.BlockSpec((tm, ATT_COLS), lambda i: (i, 0)),
                  pl.BlockSpec((tm, LANES), lambda i: (i, 0)),
                  pl.BlockSpec((tm, LANES), lambda i: (i, 0)),
                  const((1, ATT_WIDTH)), const((1, ATT_KV_WIDTH)),
                  const((ATT_WIDTH, ATT_WIDTH)), const((ATT_KV_WIDTH, ATT_KV_WIDTH))],
        out_specs=(pl.BlockSpec((ATT_WIDTH, tm), lambda i: (0, i)),
                   pl.BlockSpec((ATT_KV_HEADS, tm, LANES), lambda i: (0, i, 0)),
                   pl.BlockSpec((ATT_KV_HEADS * ATT_VROWS, tm), lambda i: (0, i)),
                   pl.BlockSpec((1, 8, LANES), lambda i: (i, 0, 0))),
        compiler_params=_cparams("parallel"),
        name="attention_prep",
    )(p_att, cos, sin, qg, kg, _head_mean_matrix(ATT_WIDTH), _head_mean_matrix(ATT_KV_WIDTH))


def _rope_tables(n_tokens):
    rows = n_tokens // GRID_W
    inv_freq = ROPE_BASE ** (-jnp.arange(0, ROPE_AXIS_DIM, 2, dtype=F32) / ROPE_AXIS_DIM)
    ang_r = jnp.arange(rows, dtype=F32)[:, None] * inv_freq[None, :]
    ang_c = jnp.arange(GRID_W, dtype=F32)[:, None] * inv_freq[None, :]
    nf = inv_freq.shape[0]
    by_row = lambda tab: jnp.broadcast_to(tab[:, None, :], (rows, GRID_W, nf)).reshape(n_tokens, nf)
    by_col = lambda tab: jnp.broadcast_to(tab[None, :, :], (rows, GRID_W, nf)).reshape(n_tokens, nf)
    cr, sr = by_row(jnp.cos(ang_r)), by_row(jnp.sin(ang_r))
    cc, sc = by_col(jnp.cos(ang_c)), by_col(jnp.sin(ang_c))
    cos = jnp.concatenate([cr, cr, cc, cc], axis=1)
    sin = jnp.concatenate([-sr, sr, -sc, sc], axis=1)
    return jnp.tile(cos, (1, 2)), jnp.tile(sin, (1, 2))


def _attention_kernel(*refs, tq, tk, sub, n_lat):
    if n_lat:
        kmax_ref, qt_ref, kc_ref, vct_ref, k_ref, vt_ref, o_ref, qs_ref, acc_ref, m_ref = refs
    else:
        qt_ref, kc_ref, vct_ref, o_ref, acc_ref = refs
    cols = ATT_GROUP * tq
    qt = qt_ref[...]
    qst = jnp.concatenate(
        [qt[g * ATT_HEAD_DIM:(g + 1) * ATT_HEAD_DIM, :] for g in range(ATT_GROUP)], axis=1)
    pad = jnp.zeros((LANES - ATT_HEAD_DIM, cols), BF16)
    s = _dot(kc_ref[0], jnp.concatenate([qst, pad], axis=0))
    mt = jnp.max(s, axis=0, keepdims=True).astype(BF16)
    mtf = mt.astype(F32)
    acc_ref[...] = _dot(vct_ref[...], jnp.exp2(s - mtf).astype(BF16))

    if n_lat:
        row = lax.broadcasted_iota(jnp.int32, pad.shape, 0)
        qs_ref[...] = jnp.concatenate([qst, jnp.where(row == 0, mtf, 0.0).astype(BF16)], axis=0)
        qf = qst.astype(F32)
        qn = jnp.sqrt(jnp.sum(qf * qf, axis=0, keepdims=True))
        fast = jnp.max(qn * kmax_ref[pl.program_id(0)] - mtf) <= EXP2_CAP

        def tile(j):
            off = pl.multiple_of(j * tk, tk)
            return k_ref[0, pl.ds(off, tk), :], vt_ref[:, pl.ds(off, tk)]

        @pl.when(fast)
        def _():
            def body(j, carry):
                k, vt = tile(j)
                qs = qs_ref[...]
                part = None
                for c in range(tk // sub):
                    p = jnp.exp2(_dot(k[c * sub:(c + 1) * sub], qs)).astype(BF16)
                    pv = _dot(vt[:, c * sub:(c + 1) * sub], p)
                    part = pv if part is None else part + pv
                acc_ref[...] += part
                return carry

            lax.fori_loop(0, n_lat, body, 0)

        @pl.when(jnp.logical_not(fast))
        def _():
            m_ref[...] = jnp.zeros_like(m_ref)

            def body(j, carry):
                k, vt = tile(j)
                s = _dot(k, qs_ref[...])
                m_old = m_ref[...]
                m_new = jnp.maximum(m_old, jnp.max(s, axis=0, keepdims=True))
                p = jnp.exp2(s - m_new).astype(BF16)
                acc_ref[...] = jnp.exp2(m_old - m_new) * acc_ref[...] + _dot(vt, p)
                m_ref[...] = m_new
                return carry

            lax.fori_loop(0, n_lat, body, 0)

    acc = acc_ref[...]
    o = acc[:ATT_HEAD_DIM] / acc[ATT_HEAD_DIM:ATT_HEAD_DIM + 1]
    o_ref[...] = jnp.concatenate(
        [o[:, g * tq:(g + 1) * tq].T for g in range(ATT_GROUP)], axis=1).astype(o_ref.dtype)


def _attention(qt, kc, vct, kmax=None, k=None, vt=None, *, tq, tk=0, sub=0):
    t = qt.shape[1]
    lc = kc.shape[1]
    gw = ATT_GROUP * ATT_HEAD_DIM
    cols = ATT_GROUP * tq
    in_specs = [pl.BlockSpec((gw, tq), lambda h, i: (h, i)),
                pl.BlockSpec((1, lc, LANES), lambda h, i: (h, 0, 0)),
                pl.BlockSpec((ATT_VROWS, lc), lambda h, i: (h, 0))]
    args = [qt, kc, vct]
    scratch = [pltpu.VMEM((ATT_VROWS, cols), F32)]
    n_lat = 0
    if k is not None:
        lk = k.shape[1]
        n_lat = lk // tk
        in_specs = ([pl.BlockSpec(memory_space=pltpu.SMEM)] + in_specs
                    + [pl.BlockSpec((1, lk, LANES), lambda h, i: (h, 0, 0)),
                       pl.BlockSpec((ATT_VROWS, lk), lambda h, i: (h, 0))])
        args = [kmax] + args + [k, vt]
        scratch = [pltpu.VMEM((LANES, cols), BF16)] + scratch + [pltpu.VMEM((1, cols), F32)]
    return pl.pallas_call(
        functools.partial(_attention_kernel, tq=tq, tk=tk, sub=sub, n_lat=n_lat),
        out_shape=jax.ShapeDtypeStruct((t, ATT_WIDTH), BF16),
        grid=(ATT_KV_HEADS, t // tq),
        in_specs=in_specs,
        out_specs=pl.BlockSpec((tq, gw), lambda h, i: (i, h)),
        scratch_shapes=scratch,
        compiler_params=_cparams("parallel", "parallel"),
        name="attention_lat" if n_lat else "attention_ctx",
    )(*args)


def _ret_tables(lg_f, lg_b):
    pos = jnp.arange(RET_CHUNK, dtype=F32)
    rel = pos[:, None] - pos[None, :]
    mask = jnp.where(rel > 0, jnp.exp(lg_f[:, None, None] * jnp.maximum(rel, 0.0)),
                     jnp.where(rel < 0, jnp.exp(lg_b[:, None, None] * jnp.maximum(-rel, 0.0)), 2.0))
    zeta = jnp.stack([jnp.exp(lg_f[:, None] * (RET_CHUNK - 1 - pos)[None, :]),
                      jnp.exp(lg_b[:, None] * pos[None, :])])
    xi = jnp.stack([jnp.exp(lg_f[:, None] * (pos + 1)[None, :]),
                    jnp.exp(lg_b[:, None] * (RET_CHUNK - pos)[None, :])])
    chunk_decay = jnp.stack([jnp.exp(lg_f * RET_CHUNK), jnp.exp(lg_b * RET_CHUNK)])
    zeta = jnp.repeat(zeta, RET_HEAD_DIM, axis=1).transpose(0, 2, 1)
    xi = jnp.repeat(xi, RET_HEAD_DIM, axis=1).transpose(0, 2, 1)
    chunk_decay = jnp.repeat(chunk_decay, RET_HEAD_DIM, axis=1)[:, :, None]
    return mask, zeta, xi, chunk_decay


def _ret_state_kernel(k_ref, v_ref, zeta_ref, cd_ref, s0_ref, sprev_ref, sfin_ref, s_ref, *, cps):
    d = pl.program_id(0)

    @pl.when(pl.program_id(1) == 0)
    def _():
        s_ref[...] = s0_ref[0]

    zeta = zeta_ref[0] * (RET_HEAD_DIM ** -0.5)
    cd = cd_ref[0]
    us = []
    for c in range(cps):
        rows = slice(c * RET_CHUNK, (c + 1) * RET_CHUNK)
        kz = (k_ref[rows, :] * zeta).astype(BF16)
        v = v_ref[rows, :].astype(BF16)
        heads = []
        for h in range(RET_HEADS):
            sl = slice(h * RET_HEAD_DIM, (h + 1) * RET_HEAD_DIM)
            heads.append(lax.dot_general(kz[:, sl], v[:, sl], (((0,), (0,)), ((), ())),
                                         preferred_element_type=F32))
        us.append(jnp.concatenate(heads, axis=0))

    def scan(order):
        s = s_ref[...]
        for c in order:
            sprev_ref[0, c] = s
            s = cd * s + us[c]
        s_ref[...] = s
        sfin_ref[0] = s

    @pl.when(d == 0)
    def _():
        scan(range(cps))

    @pl.when(d == 1)
    def _():
        scan(reversed(range(cps)))


def _retention_states(p_ret, zeta, chunk_decay, s0):
    nc = p_ret.shape[0] // RET_CHUNK
    cps = min(nc, RET_STATE_CHUNKS)
    ng = nc // cps
    rows = cps * RET_CHUNK

    def group(d, i):
        return i + d * (ng - 1 - 2 * i)

    return pl.pallas_call(
        functools.partial(_ret_state_kernel, cps=cps),
        out_shape=(jax.ShapeDtypeStruct((2, nc, RET_WIDTH, RET_HEAD_DIM), F32),
                   jax.ShapeDtypeStruct((2, RET_WIDTH, RET_HEAD_DIM), F32)),
        grid=(2, ng),
        in_specs=[pl.BlockSpec((rows, RET_WIDTH), lambda d, i: (group(d, i), 1)),
                  pl.BlockSpec((rows, RET_WIDTH), lambda d, i: (group(d, i), 2)),
                  pl.BlockSpec((1, RET_CHUNK, RET_WIDTH), lambda d, i: (d, 0, 0)),
                  pl.BlockSpec((1, RET_WIDTH, 1), lambda d, i: (d, 0, 0)),
                  pl.BlockSpec((1, RET_WIDTH, RET_HEAD_DIM), lambda d, i: (d, 0, 0))],
        out_specs=(pl.BlockSpec((1, cps, RET_WIDTH, RET_HEAD_DIM), lambda d, i: (d, group(d, i), 0, 0)),
                   pl.BlockSpec((1, RET_WIDTH, RET_HEAD_DIM), lambda d, i: (d, 0, 0))),
        scratch_shapes=[pltpu.VMEM((RET_WIDTH, RET_HEAD_DIM), F32)],
        compiler_params=_cparams("arbitrary", "arbitrary"),
        name="retention_states",
    )(p_ret, p_ret, zeta, chunk_decay, s0)


def _ret_out_kernel(q_ref, k_ref, v_ref, g_ref, mask_ref, xi_ref, sf_ref, sb_ref, y_ref, *, cpo):
    for c in range(cpo):
        rows = slice(c * RET_CHUNK, (c + 1) * RET_CHUNK)
        q = q_ref[rows, :]
        k = (k_ref[rows, :] * (RET_HEAD_DIM ** -0.5)).astype(BF16)
        v = v_ref[rows, :].astype(BF16)
        qb = q.astype(BF16)
        qf = (q * xi_ref[0]).astype(BF16)
        qr = (q * xi_ref[1]).astype(BF16)
        sf = sf_ref[0, c].astype(BF16)
        sb = sb_ref[0, c].astype(BF16)
        ys = []
        for h in range(RET_HEADS):
            sl = slice(h * RET_HEAD_DIM, (h + 1) * RET_HEAD_DIM)
            scores = lax.dot_general(qb[:, sl], k[:, sl], (((1,), (1,)), ((), ())),
                                     preferred_element_type=F32)
            y = _dot((scores * mask_ref[h]).astype(BF16), v[:, sl])
            y = y + _dot(qf[:, sl], sf[sl, :]) + _dot(qr[:, sl], sb[sl, :])
            mu = jnp.mean(y, axis=-1, keepdims=True)
            yc = y - mu
            var = jnp.mean(yc * yc, axis=-1, keepdims=True)
            ys.append(yc * lax.rsqrt(var + EPS))
        y_ref[rows, :] = (_silu(g_ref[rows, :]) * jnp.concatenate(ys, axis=1)).astype(y_ref.dtype)


def _retention_outputs(p_ret, mask, xi, sprev):
    nc = p_ret.shape[0] // RET_CHUNK
    cpo = min(nc, RET_OUT_CHUNKS)
    rows = cpo * RET_CHUNK
    col = lambda c: pl.BlockSpec((rows, RET_WIDTH), lambda i: (i, c))
    return pl.pallas_call(
        functools.partial(_ret_out_kernel, cpo=cpo),
        out_shape=jax.ShapeDtypeStruct((p_ret.shape[0], RET_WIDTH), BF16),
        grid=(nc // cpo,),
        in_specs=[col(0), col(1), col(2), col(3),
                  pl.BlockSpec((RET_HEADS, RET_CHUNK, RET_CHUNK), lambda i: (0, 0, 0)),
                  pl.BlockSpec((2, RET_CHUNK, RET_WIDTH), lambda i: (0, 0, 0)),
                  pl.BlockSpec((1, cpo, RET_WIDTH, RET_HEAD_DIM), lambda i: (0, i, 0, 0)),
                  pl.BlockSpec((1, cpo, RET_WIDTH, RET_HEAD_DIM), lambda i: (1, i, 0, 0))],
        out_specs=pl.BlockSpec((rows, RET_WIDTH), lambda i: (i, 0)),
        compiler_params=_cparams("parallel"),
        name="retention_outputs",
    )(p_ret, p_ret, p_ret, p_ret, mask, xi, sprev, sprev)


def _hy_pre_kernel(p_ref, prev_ref, next_ref, w_ref, b_ref, x0_ref, z_ref):
    i = pl.program_id(0)
    x = p_ref[...]
    tm = x.shape[0]
    row = lax.broadcasted_iota(jnp.int32, x.shape, 0)
    prev_row = jnp.where(i == 0, 0.0, prev_ref[7:8, :])
    next_row = jnp.where(i == pl.num_programs(0) - 1, 0.0, next_ref[0:1, :])
    up = jnp.where(row == 0, prev_row, pltpu.roll(x, 1, 0))
    dn = jnp.where(row == tm - 1, next_row, pltpu.roll(x, tm - 1, 0))
    w = w_ref[...]
    u = up * w[0:1] + x * w[1:2] + dn * w[2:3] + b_ref[...]
    x0_ref[...] = u[:, :HY_WIDTH]
    z_ref[...] = u[:, HY_WIDTH:2 * HY_WIDTH] * u[:, 2 * HY_WIDTH:]


def _hyena_pre(p_hy, conv_w, conv_b):
    t = p_hy.shape[0]
    tm = min(t, 512)
    nb8 = tm // 8
    last8 = t // 8 - 1
    return pl.pallas_call(
        _hy_pre_kernel,
        out_shape=(jax.ShapeDtypeStruct((t, HY_WIDTH), F32),
                   jax.ShapeDtypeStruct((t, HY_WIDTH), F32)),
        grid=(t // tm,),
        in_specs=[pl.BlockSpec((tm, HY_COLS), lambda i: (i, 0)),
                  pl.BlockSpec((8, HY_COLS), lambda i: (jnp.maximum(i * nb8 - 1, 0), 0)),
                  pl.BlockSpec((8, HY_COLS), lambda i: (jnp.minimum((i + 1) * nb8, last8), 0)),
                  pl.BlockSpec((3, HY_COLS), lambda i: (0, 0)),
                  pl.BlockSpec((1, HY_COLS), lambda i: (0, 0))],
        out_specs=(pl.BlockSpec((tm, HY_WIDTH), lambda i: (i, 0)),
                   pl.BlockSpec((tm, HY_WIDTH), lambda i: (i, 0))),
        compiler_params=_cparams("parallel"),
        name="hyena_pre",
    )(p_hy, p_hy, p_hy, conv_w, conv_b.reshape(1, HY_COLS))


def _hy_filter_kernel(feat_ref, w1_ref, b1_ref, w2_ref, b2_ref, w3_ref, b3_ref, freq_ref,
                      delta_ref, taps_ref, asum_ref, *, seq_len):
    i = pl.program_id(0)
    feat = feat_ref[...]
    tm = feat.shape[0]
    half = tm // 2
    freq = freq_ref[...]
    packed = jnp.concatenate([feat[:half], feat[half:]], axis=1)
    h = jnp.sin(freq * (_dot3(packed, w1_ref[...]) + b1_ref[...]))
    h = jnp.sin(freq * (_dot3(h, w2_ref[...]) + b2_ref[...]))
    h = _dot3(h, w3_ref[0])
    h = jnp.concatenate([h[:, :HY_WIDTH], h[:, HY_WIDTH:]], axis=0) + b3_ref[0]
    j = i * tm + lax.broadcasted_iota(jnp.int32, (tm, HY_WIDTH), 0)
    taps = jnp.where(j == seq_len, 0.0, h * jnp.exp(-feat[:, 0:1] * delta_ref[...]))
    taps_ref[...] = taps

    @pl.when(i == 0)
    def _():
        asum_ref[...] = jnp.zeros_like(asum_ref)

    asum_ref[...] += jnp.sum(jnp.abs(taps).reshape(tm // SUBLANES, SUBLANES, HY_WIDTH), axis=0)


def _position_features(seq_len):
    lo = 256
    hi = seq_len // lo + 1
    bands = np.linspace(1e-4, HY_POS_BANDS - 1, HY_POS_BANDS).astype(np.float32).astype(np.float64)
    ang_hi = (2.0 * np.pi / seq_len) * lo * np.arange(hi)[:, None] * bands[None, :]
    ang_lo = (2.0 * np.pi / seq_len) * np.arange(lo)[:, None] * bands[None, :]
    tab = lambda a: jnp.asarray(a, F32)
    ch, sh = tab(np.cos(ang_hi))[:, None, :], tab(np.sin(ang_hi))[:, None, :]
    cl, sl = tab(np.cos(ang_lo))[None, :, :], tab(np.sin(ang_lo))[None, :, :]
    npos = seq_len + 1
    cos = (ch * cl - sh * sl).reshape(hi * lo, HY_POS_BANDS)[:npos]
    sin = (sh * cl + ch * sl).reshape(hi * lo, HY_POS_BANDS)[:npos]
    c_end, s_end = tab(np.cos(2.0 * np.pi * bands))[None, :], tab(np.sin(2.0 * np.pi * bands))[None, :]
    cos_b = c_end * cos[1:seq_len] + s_end * sin[1:seq_len]
    sin_b = s_end * cos[1:seq_len] - c_end * sin[1:seq_len]
    lag = jnp.concatenate([jnp.arange(npos, dtype=F32), jnp.arange(seq_len - 1, 0, -1, dtype=F32)])
    t = lag / max(seq_len - 1, 1)
    cos = jnp.concatenate([cos, cos_b], axis=0)
    sin = jnp.concatenate([sin, sin_b], axis=0)
    return jnp.concatenate([t[:, None], cos, -sin,
                            jnp.zeros((2 * seq_len, LANES - 1 - 2 * HY_POS_BANDS), F32)], axis=-1)


def _hyena_taps(seq_len, w1, b1, w2, b2, w3, b3, freq):
    n = 2 * seq_len
    feat = _position_features(seq_len)
    w1p = jnp.concatenate([w1, jnp.zeros((LANES - w1.shape[0], w1.shape[1]), F32)], axis=0)
    pair = lambda w: jnp.concatenate(
        [jnp.concatenate([w, jnp.zeros_like(w)], axis=1),
         jnp.concatenate([jnp.zeros_like(w), w], axis=1)], axis=0)
    twice = lambda v: jnp.tile(v.reshape(1, -1), (1, 2))
    w3_dirs = jnp.stack([pair(w3[:, :HY_WIDTH]), pair(w3[:, HY_WIDTH:])])
    b3_dirs = b3.reshape(2, 1, HY_WIDTH)
    deltas = jnp.abs(jnp.linspace(math.log(HY_DECAY_TARGET) / HY_LONG_DECAY_PCT,
                                  math.log(HY_DECAY_TARGET) / HY_SHORT_DECAY_PCT,
                                  HY_WIDTH, dtype=F32)).reshape(1, HY_WIDTH)
    tm = min(seq_len, 1024)
    per_dir = seq_len // tm
    hid2 = 2 * w1.shape[1]
    const = lambda shape: pl.BlockSpec(shape, lambda i: (0, 0))
    by_dir = lambda shape: pl.BlockSpec(shape, lambda i: (i // per_dir, 0, 0))
    return pl.pallas_call(
        functools.partial(_hy_filter_kernel, seq_len=seq_len),
        out_shape=(jax.ShapeDtypeStruct((n, HY_WIDTH), F32),
                   jax.ShapeDtypeStruct((8, HY_WIDTH), F32)),
        grid=(n // tm,),
        in_specs=[pl.BlockSpec((tm, LANES), lambda i: (i, 0)),
                  const((2 * LANES, hid2)), const((1, hid2)), const((hid2, hid2)), const((1, hid2)),
                  by_dir((1, hid2, 2 * HY_WIDTH)), by_dir((1, 1, HY_WIDTH)), const((1, hid2)),
                  const((1, HY_WIDTH))],
        out_specs=(pl.BlockSpec((tm, HY_WIDTH), lambda i: (i, 0)), const((8, HY_WIDTH))),
        compiler_params=_cparams("arbitrary"),
        name="hyena_taps",
    )(feat, pair(w1p), twice(b1), pair(w2), twice(b2), w3_dirs, b3_dirs, twice(freq), deltas)


FFT_N2 = 256
FFT_K1_PER_STEP = 4
FFT_N2_PER_STEP = 16


def _dft_tables(n_total, n1_in):
    n1 = n_total // FFT_N2
    k1 = np.arange(n1)
    a = 2.0 * np.pi * np.outer(k1, np.arange(n1_in)) / n1
    lvl1 = np.concatenate([np.cos(a), -np.sin(a)], axis=0)
    n2 = np.arange(FFT_N2)
    th = 2.0 * np.pi * np.outer(n2, n2) / FFT_N2
    fc, fs = np.cos(th), np.sin(th)
    g = np.block([[fc, fs], [-fs, fc]])
    ph = 2.0 * np.pi * np.outer(k1, n2) / n_total
    tw = np.stack([np.cos(ph), np.sin(ph)])[..., None]
    return (jnp.asarray(lvl1, F32), jnp.asarray(g, F32), jnp.asarray(g.T, F32), jnp.asarray(tw, F32))


def _inverse_lvl1_table(n_total, n1_out):
    n1 = n_total // FFT_N2
    a = 2.0 * np.pi * np.outer(np.arange(n1_out), np.arange(n1)) / n1
    return jnp.asarray(np.concatenate([np.cos(a), -np.sin(a)], axis=1), F32)


def _lvl2_kernel(b_ref, tw_ref, gh_ref, gl_ref, *rest, inverse):
    for j in range(b_ref.shape[1]):
        c = tw_ref[0, j]
        s = tw_ref[1, j]
        br = b_ref[0, j]
        bi = b_ref[1, j]
        d = jnp.concatenate([c * br + s * bi, c * bi - s * br], axis=0)
        dh, dl = _split(d)
        x = _dot(gh_ref[...], dh) + _dot(gh_ref[...], dl) + _dot(gl_ref[...], dh)
        if not inverse:
            (o_ref,) = rest
            o_ref[0, j] = x[:FFT_N2]
            o_ref[1, j] = x[FFT_N2:]
            continue
        h_ref, gth_ref, gtl_ref, o_ref = rest
        xr, xi = x[:FFT_N2], x[FFT_N2:]
        hr, hi = h_ref[0, j], h_ref[1, j]
        y = jnp.concatenate([xr * hr - xi * hi, xr * hi + xi * hr], axis=0)
        yh, yl = _split(y)
        cc = _dot(gth_ref[...], yh) + _dot(gth_ref[...], yl) + _dot(gtl_ref[...], yh)
        cr, ci = cc[:FFT_N2], cc[FFT_N2:]
        o_ref[0, j] = c * cr - s * ci
        o_ref[1, j] = c * ci + s * cr


def _level2(b, tw, g, h=None, gt=None):
    _, n1, n2, ch = b.shape
    kb = FFT_K1_PER_STEP
    blk = pl.BlockSpec((2, kb, n2, ch), lambda i: (0, i, 0, 0))
    const = pl.BlockSpec((2 * n2, 2 * n2), lambda i: (0, 0))
    gh, gl = _split(g)
    in_specs = [blk, pl.BlockSpec((2, kb, n2, 1), lambda i: (0, i, 0, 0)), const, const]
    args = [b, tw, gh, gl]
    if h is not None:
        gth, gtl = _split(gt)
        in_specs += [blk, const, const]
        args += [h, gth, gtl]
    return pl.pallas_call(
        functools.partial(_lvl2_kernel, inverse=h is not None),
        out_shape=jax.ShapeDtypeStruct(b.shape, F32),
        grid=(n1 // kb,),
        in_specs=in_specs,
        out_specs=blk,
        compiler_params=_cparams("parallel"),
        name="hyena_fft_level2" + ("_conv" if h is not None else ""),
    )(*args)


def _lvl1_kernel(a_ref, x_ref, o_ref, *, inverse):
    tn2 = x_ref.shape[-2]
    ch = x_ref.shape[-1]
    if inverse:
        x = jnp.concatenate([x_ref[:, :, j, :].reshape(-1, ch) for j in range(tn2)], axis=1)
    else:
        x = jnp.concatenate([x_ref[:, j, :] for j in range(tn2)], axis=1)
    out = _dot3(a_ref[...], x)
    for j in range(tn2):
        slab = out[:, j * ch:(j + 1) * ch]
        if inverse:
            o_ref[:, j, :] = slab
        else:
            o_ref[:, :, j, :] = slab.reshape(2, slab.shape[0] // 2, ch)


def _level1(a, x, name):
    n1_in, n2, ch = x.shape
    n1 = a.shape[0] // 2
    tn2 = FFT_N2_PER_STEP
    return pl.pallas_call(
        functools.partial(_lvl1_kernel, inverse=False),
        out_shape=jax.ShapeDtypeStruct((2, n1, n2, ch), F32),
        grid=(n2 // tn2,),
        in_specs=[pl.BlockSpec(a.shape, lambda i: (0, 0)),
                  pl.BlockSpec((n1_in, tn2, ch), lambda i: (0, i, 0))],
        out_specs=pl.BlockSpec((2, n1, tn2, ch), lambda i: (0, 0, i, 0)),
        compiler_params=_cparams("parallel"),
        name=name,
    )(a, x)


def _inverse_level1(a, x):
    _, n1, n2, ch = x.shape
    n1_out = a.shape[0]
    tn2 = FFT_N2_PER_STEP
    return pl.pallas_call(
        functools.partial(_lvl1_kernel, inverse=True),
        out_shape=jax.ShapeDtypeStruct((n1_out, n2, ch), F32),
        grid=(n2 // tn2,),
        in_specs=[pl.BlockSpec(a.shape, lambda i: (0, 0)),
                  pl.BlockSpec((2, n1, tn2, ch), lambda i: (0, 0, i, 0))],
        out_specs=pl.BlockSpec((n1_out, tn2, ch), lambda i: (0, i, 0)),
        compiler_params=_cparams("parallel"),
        name="hyena_fft_inverse_level1",
    )(a, x)


def _hyena_long_conv(z, taps):
    seq_len, ch = z.shape
    n = 2 * seq_len
    n1 = n // FFT_N2
    lvl1_z, g, gt, tw = _dft_tables(n, n1 // 2)
    lvl1_t = _dft_tables(n, n1)[0]
    inv1 = _inverse_lvl1_table(n, n1 // 2)
    hb = _level1(lvl1_t, taps.reshape(n1, FFT_N2, ch), "hyena_fft_level1_taps")
    hspec = _level2(hb, tw, g)
    zb = _level1(lvl1_z, z.reshape(n1 // 2, FFT_N2, ch), "hyena_fft_level1_z")
    cspec = _level2(zb, tw, g, hspec, gt)
    return _inverse_level1(inv1, cspec).reshape(seq_len, ch)


def _small_conv_kernel(z_ref, taps_ref, fc_ref, fs_ref, o_ref):
    seq_len = z_ref.shape[0]
    fc = fc_ref[...]
    fs = fs_ref[...]
    z = z_ref[...]
    taps = taps_ref[...]
    zr = _dot3(fc[:, :seq_len], z)
    zi = -_dot3(fs[:, :seq_len], z)
    hr = _dot3(fc, taps)
    hi = -_dot3(fs, taps)
    yr = zr * hr - zi * hi
    yi = zr * hi + zi * hr
    o_ref[...] = _dot3(fc[:seq_len, :], yr) - _dot3(fs[:seq_len, :], yi)


def _hyena_small_conv(z, taps):
    seq_len, ch = z.shape
    n = 2 * seq_len
    th = 2.0 * np.pi * np.outer(np.arange(n), np.arange(n)) / n
    full = lambda shape: pl.BlockSpec(shape, lambda: (0, 0))
    return pl.pallas_call(
        _small_conv_kernel,
        out_shape=jax.ShapeDtypeStruct((seq_len, ch), F32),
        in_specs=[full((seq_len, ch)), full((n, ch)), full((n, n)), full((n, n))],
        out_specs=full((seq_len, ch)),
        compiler_params=pltpu.CompilerParams(vmem_limit_bytes=VMEM_LIMIT),
        name="hyena_small_conv",
    )(z, taps, jnp.asarray(np.cos(th), F32), jnp.asarray(np.sin(th), F32))


def _hy_post_kernel(x0_ref, z_ref, conv_ref, asum_ref, bias_ref, y_ref, *, n_total):
    norm = jnp.sum(asum_ref[...], axis=0, keepdims=True) + EPS
    conv = conv_ref[...] * (1.0 / (n_total * norm))
    y_ref[...] = (x0_ref[...] * (conv + bias_ref[...] * z_ref[...])).astype(y_ref.dtype)


def _hyena_post(x0, z, conv, asum, bias):
    t = x0.shape[0]
    tm = min(t, 1024)
    row = pl.BlockSpec((tm, HY_WIDTH), lambda i: (i, 0))
    return pl.pallas_call(
        functools.partial(_hy_post_kernel, n_total=2 * t),
        out_shape=jax.ShapeDtypeStruct((t, HY_WIDTH), BF16),
        grid=(t // tm,),
        in_specs=[row, row, row, pl.BlockSpec((8, HY_WIDTH), lambda i: (0, 0)),
                  pl.BlockSpec((1, HY_WIDTH), lambda i: (0, 0))],
        out_specs=row,
        compiler_params=_cparams("parallel"),
        name="hyena_post",
    )(x0, z, conv, asum, bias.reshape(1, HY_WIDTH))


def _hyena_mixer(p_hy, conv_w, conv_b, filt, bias):
    seq_len = p_hy.shape[0]
    x0, z = _hyena_pre(p_hy, conv_w, conv_b)
    taps, asum = _hyena_taps(seq_len, *filt)
    if 2 * seq_len // FFT_N2 >= 16:
        conv = _hyena_long_conv(z, taps)
    else:
        conv = _hyena_small_conv(z, taps)
    return _hyena_post(x0, z, conv, asum, bias)


def _outproj_kernel(hy_ref, ret_ref, att_ref, w_ref, x_ref, gate_ref, o_ref):
    y = (_dot(hy_ref[...], w_ref[:HY_WIDTH, :])
         + _dot(ret_ref[...], w_ref[HY_WIDTH:HY_WIDTH + RET_WIDTH, :])
         + _dot(att_ref[...], w_ref[HY_WIDTH + RET_WIDTH:, :]))
    o_ref[...] = x_ref[...] + gate_ref[...] * y


def _out_projection(y_hy, y_ret, y_att, w_bf16, x, gate):
    t = x.shape[0]
    tm = min(t, 512)
    row = lambda w: pl.BlockSpec((tm, w), lambda i: (i, 0))
    return pl.pallas_call(
        _outproj_kernel,
        out_shape=jax.ShapeDtypeStruct((t, D_MODEL), F32),
        grid=(t // tm,),
        in_specs=[row(HY_WIDTH), row(RET_WIDTH), row(ATT_WIDTH),
                  pl.BlockSpec((MIX_WIDTH, D_MODEL), lambda i: (0, 0)),
                  row(D_MODEL), pl.BlockSpec((1, D_MODEL), lambda i: (0, 0))],
        out_specs=row(D_MODEL),
        compiler_params=_cparams("parallel"),
        name="out_projection",
    )(y_hy, y_ret, y_att, w_bf16, x, gate)


def _swiglu(x, wg_ref, wu_ref, wd_ref):
    hidden = wg_ref.shape[1]
    y = None
    for lo in range(0, hidden, SWIGLU_CHUNK):
        hi = min(lo + SWIGLU_CHUNK, hidden)
        a = _silu(_dot(x, wg_ref[:, lo:hi])) * _dot(x, wu_ref[:, lo:hi])
        part = _dot(a.astype(BF16), wd_ref[lo:hi, :])
        y = part if y is None else y + part
    return y


def _ffn_kernel(x_ref, g_ref, sh_ref, sc_ref, gate_ref, wg_ref, wu_ref, wd_ref, o_ref):
    x = x_ref[...]
    h = _norm_mod(x, g_ref[...], sh_ref[...], sc_ref[...]).astype(BF16)
    o_ref[...] = x + gate_ref[...] * _swiglu(h, wg_ref, wu_ref, wd_ref)


def _dense_ffn(x, g, shift, scale, gate, wg, wu, wd):
    t = x.shape[0]
    tm = min(t, 512)
    vec = pl.BlockSpec((1, D_MODEL), lambda i: (0, 0))
    resident = lambda shape: pl.BlockSpec(shape, lambda i: (0, 0), pipeline_mode=pl.Buffered(1))
    return pl.pallas_call(
        _ffn_kernel,
        out_shape=jax.ShapeDtypeStruct((t, D_MODEL), F32),
        grid=(t // tm,),
        in_specs=[pl.BlockSpec((tm, D_MODEL), lambda i: (i, 0)), vec, vec, vec, vec,
                  resident((D_MODEL, FFN_HIDDEN)), resident((D_MODEL, FFN_HIDDEN)),
                  resident((FFN_HIDDEN, D_MODEL))],
        out_specs=pl.BlockSpec((tm, D_MODEL), lambda i: (i, 0)),
        compiler_params=_cparams("parallel"),
        name="dense_ffn",
    )(x, g, shift, scale, gate, wg, wu, wd)


def _moe_route_kernel(x_ref, g_ref, sh_ref, sc_ref, rw_ref, rb_ref, h_ref, info_ref):
    h = _norm_mod(x_ref[...], g_ref[...], sh_ref[...], sc_ref[...])
    _to_row_tiles(h_ref, h)
    logits = _dot3(h, rw_ref[...]) + rb_ref[...]
    lane = lax.broadcasted_iota(jnp.int32, logits.shape, 1)
    v1 = jnp.max(logits, axis=-1, keepdims=True)
    i1 = jnp.min(jnp.where(logits == v1, lane, LANES), axis=-1, keepdims=True)
    rest = jnp.where(lane == i1, -jnp.inf, logits)
    v2 = jnp.max(rest, axis=-1, keepdims=True)
    i2 = jnp.min(jnp.where(rest == v2, lane, LANES), axis=-1, keepdims=True)
    e = jnp.exp(v2 - v1)
    g1 = 1.0 / (1.0 + e)
    g2 = e * g1
    info_ref[...] = jnp.where(lane == 0, i1.astype(F32),
                              jnp.where(lane == 1, i2.astype(F32),
                                        jnp.where(lane == 2, g1, jnp.where(lane == 3, g2, 0.0))))


def _moe_route(x, g, shift, scale, router_w, router_b):
    t = x.shape[0]
    tm = min(t, 512)
    rw = jnp.concatenate([router_w, jnp.zeros((D_MODEL, LANES - N_EXPERTS), F32)], axis=1)
    rb = jnp.concatenate([router_b, jnp.full((LANES - N_EXPERTS,), -jnp.inf, F32)]).reshape(1, LANES)
    vec = pl.BlockSpec((1, D_MODEL), lambda i: (0, 0))
    return pl.pallas_call(
        _moe_route_kernel,
        out_shape=(jax.ShapeDtypeStruct((t * SUBLANES, LANES), F32),
                   jax.ShapeDtypeStruct((t, LANES), F32)),
        grid=(t // tm,),
        in_specs=[pl.BlockSpec((tm, D_MODEL), lambda i: (i, 0)), vec, vec, vec,
                  pl.BlockSpec((D_MODEL, LANES), lambda i: (0, 0)),
                  pl.BlockSpec((1, LANES), lambda i: (0, 0))],
        out_specs=(pl.BlockSpec((tm * SUBLANES, LANES), lambda i: (i, 0)),
                   pl.BlockSpec((tm, LANES), lambda i: (i, 0))),
        compiler_params=_cparams("parallel"),
        name="moe_route",
    )(x, g, shift, scale, rw, rb)


def _to_row_tiles(ref, x):
    rows = x.shape[0]
    for k in range(D_MODEL // LANES):
        ref[pl.ds(k, rows, stride=SUBLANES), :] = x[:, k * LANES:(k + 1) * LANES]


def _from_row_tiles(ref, rows):
    return jnp.concatenate(
        [ref[pl.ds(k, rows, stride=SUBLANES), :] for k in range(D_MODEL // LANES)], axis=1)


def _row_copy(src_hbm, dst_vmem, sem, src_row, dst_row):
    src = pl.multiple_of(src_row * SUBLANES, SUBLANES)
    dst = pl.multiple_of(dst_row * SUBLANES, SUBLANES)
    return pltpu.make_async_copy(src_hbm.at[pl.ds(src, SUBLANES)], dst_vmem.at[pl.ds(dst, SUBLANES)], sem)


def _moe_expert_kernel(blk_e_ref, nused_ref, dest_ref, fill_ref, h_hbm, wg_ref, wu_ref, wd_ref,
                       y_ref, xbuf_ref, src_ref, sem):
    b = pl.program_id(0)
    n_used = nused_ref[0]

    @pl.when(b == 0)
    def _():
        def invert(t, carry):
            for j in range(TOP_K):
                src_ref[dest_ref[TOP_K * t + j]] = t
            return carry

        lax.fori_loop(0, dest_ref.shape[0] // TOP_K, invert, 0, unroll=DMA_UNROLL)
        for e in range(N_EXPERTS):
            def pad(d, carry):
                src_ref[d] = 0
                return carry

            lax.fori_loop(fill_ref[0, e], fill_ref[1, e], pad, 0)

    def issue_rows(blk, slot):
        def body(r, carry):
            _row_copy(h_hbm, xbuf_ref.at[slot], sem.at[slot], src_ref[blk * MOE_BLOCK + r], r).start()
            return carry

        lax.fori_loop(0, MOE_BLOCK, body, 0, unroll=DMA_UNROLL)

    def wait_rows(slot):
        def body(r, carry):
            _row_copy(h_hbm, xbuf_ref.at[slot], sem.at[slot], 0, r).wait()
            return carry

        lax.fori_loop(0, MOE_BLOCK, body, 0, unroll=DMA_UNROLL)

    @pl.when(b < n_used)
    def _():
        slot = b % 2

        @pl.when(b == 0)
        def _():
            issue_rows(0, 0)

        wait_rows(slot)

        @pl.when(b + 1 < n_used)
        def _():
            issue_rows(b + 1, 1 - slot)

        x = _from_row_tiles(xbuf_ref.at[slot], MOE_BLOCK).astype(BF16)
        _to_row_tiles(y_ref, _swiglu(x, wg_ref.at[0], wu_ref.at[0], wd_ref.at[0]))

    @pl.when(b >= n_used)
    def _():
        y_ref[...] = jnp.zeros_like(y_ref)


def _moe_experts(h, blk_expert, n_used, dest, fill, wg, wu, wd, n_blk):
    def w_index(b, be, nu, de, fi):
        return be[jnp.minimum(b, nu[0] - 1)], 0, 0

    grid_spec = pltpu.PrefetchScalarGridSpec(
        num_scalar_prefetch=4,
        grid=(n_blk,),
        in_specs=[pl.BlockSpec(memory_space=pl.ANY),
                  pl.BlockSpec((1, D_MODEL, FFN_HIDDEN), w_index),
                  pl.BlockSpec((1, D_MODEL, FFN_HIDDEN), w_index),
                  pl.BlockSpec((1, FFN_HIDDEN, D_MODEL), w_index)],
        out_specs=pl.BlockSpec((MOE_BLOCK * SUBLANES, LANES), lambda b, be, nu, de, fi: (b, 0)),
        scratch_shapes=[pltpu.VMEM((2, MOE_BLOCK * SUBLANES, LANES), F32),
                        pltpu.SMEM((n_blk * MOE_BLOCK,), jnp.int32),
                        pltpu.SemaphoreType.DMA((2,))],
    )
    return pl.pallas_call(
        _moe_expert_kernel,
        out_shape=jax.ShapeDtypeStruct((n_blk * MOE_BLOCK * SUBLANES, LANES), F32),
        grid_spec=grid_spec,
        compiler_params=_cparams("arbitrary"),
        name="moe_experts",
    )(blk_expert, n_used, dest, fill, h, wg, wu, wd)


def _moe_combine_kernel(dest_ref, x_ref, info_ref, gate_ref, ng_ref, y_hbm, o_ref,
                        y1_ref, y2_ref, sem, *, final_norm):
    i = pl.program_id(0)
    tm = x_ref.shape[0]
    slot = i % 2

    def issue_rows(blk, slot):
        def body(r, carry):
            a = 2 * (blk * tm + r)
            _row_copy(y_hbm, y1_ref.at[slot], sem.at[slot], dest_ref[a], r).start()
            _row_copy(y_hbm, y2_ref.at[slot], sem.at[slot], dest_ref[a + 1], r).start()
            return carry

        lax.fori_loop(0, tm, body, 0, unroll=DMA_UNROLL)

    def wait_rows(slot):
        def body(r, carry):
            _row_copy(y_hbm, y1_ref.at[slot], sem.at[slot], 0, r).wait()
            _row_copy(y_hbm, y2_ref.at[slot], sem.at[slot], 0, r).wait()
            return carry

        lax.fori_loop(0, tm, body, 0, unroll=DMA_UNROLL)

    @pl.when(i == 0)
    def _():
        issue_rows(0, 0)

    wait_rows(slot)

    @pl.when(i + 1 < pl.num_programs(0))
    def _():
        issue_rows(i + 1, 1 - slot)

    info = info_ref[...]
    y = (_from_row_tiles(y1_ref.at[slot], tm) * info[:, 2:3]
         + _from_row_tiles(y2_ref.at[slot], tm) * info[:, 3:4])
    x = x_ref[...] + gate_ref[...] * y
    if final_norm:
        ms = jnp.mean(x * x, axis=-1, keepdims=True)
        x = x * lax.rsqrt(ms + EPS) * ng_ref[...]
    o_ref[...] = x


def _moe_combine(dest, x, info, gate, norm_g, y, final_norm):
    t = x.shape[0]
    tm = min(t, 256)
    vec = pl.BlockSpec((1, D_MODEL), lambda i, d: (0, 0))
    grid_spec = pltpu.PrefetchScalarGridSpec(
        num_scalar_prefetch=1,
        grid=(t // tm,),
        in_specs=[pl.BlockSpec((tm, D_MODEL), lambda i, d: (i, 0)),
                  pl.BlockSpec((tm, LANES), lambda i, d: (i, 0)), vec, vec,
                  pl.BlockSpec(memory_space=pl.ANY)],
        out_specs=pl.BlockSpec((tm, D_MODEL), lambda i, d: (i, 0)),
        scratch_shapes=[pltpu.VMEM((2, tm * SUBLANES, LANES), F32),
                        pltpu.VMEM((2, tm * SUBLANES, LANES), F32),
                        pltpu.SemaphoreType.DMA((2,))],
    )
    return pl.pallas_call(
        functools.partial(_moe_combine_kernel, final_norm=final_norm),
        out_shape=jax.ShapeDtypeStruct((t, D_MODEL), F32),
        grid_spec=grid_spec,
        compiler_params=_cparams("arbitrary"),
        name="moe_combine",
    )(dest, x, info, gate, norm_g, y)


def _moe_layer(x, g, shift, scale, gate, router_w, router_b, wg, wu, wd, norm_g, final_norm):
    t = x.shape[0]
    n_asg = t * TOP_K
    h, info = _moe_route(x, g, shift, scale, router_w, router_b)
    expert = info[:, :TOP_K].astype(jnp.int32).reshape(-1)
    onehot = (expert[:, None] == jnp.arange(N_EXPERTS, dtype=jnp.int32)[None, :]).astype(jnp.int32)
    csum = jnp.cumsum(onehot, axis=0)
    counts = csum[-1]
    padded = (counts + MOE_BLOCK - 1) // MOE_BLOCK * MOE_BLOCK
    pad_end = jnp.cumsum(padded)
    pad_start = pad_end - padded
    dest = jnp.sum(onehot * (csum - 1 + pad_start[None, :]), axis=1).astype(jnp.int32)
    n_blk = -(-n_asg // MOE_BLOCK) + N_EXPERTS
    blk_start = jnp.arange(n_blk, dtype=jnp.int32) * MOE_BLOCK
    blk_expert = jnp.minimum(jnp.sum(blk_start[:, None] >= pad_end[None, :], axis=1),
                             N_EXPERTS - 1).astype(jnp.int32)
    n_used = (pad_end[-1:] // MOE_BLOCK).astype(jnp.int32)
    fill = jnp.stack([pad_start + counts, pad_end]).astype(jnp.int32)
    y = _moe_experts(h, blk_expert, n_used, dest, fill, wg, wu, wd, n_blk)
    return _moe_combine(dest, x, info, gate, norm_g, y, final_norm)


def _final_norm_kernel(x_ref, g_ref, o_ref):
    x = x_ref[...]
    ms = jnp.mean(x * x, axis=-1, keepdims=True)
    o_ref[...] = x * lax.rsqrt(ms + EPS) * g_ref[...]


def _final_norm(x, g):
    t = x.shape[0]
    tm = min(t, 1024)
    return pl.pallas_call(
        _final_norm_kernel,
        out_shape=jax.ShapeDtypeStruct((t, D_MODEL), F32),
        grid=(t // tm,),
        in_specs=[pl.BlockSpec((tm, D_MODEL), lambda i: (i, 0)),
                  pl.BlockSpec((1, D_MODEL), lambda i: (0, 0))],
        out_specs=pl.BlockSpec((tm, D_MODEL), lambda i: (i, 0)),
        compiler_params=_cparams("parallel"),
        name="final_norm",
    )(x, g)


def kernel(x, c, ctx, c_ctx, ada_w, ada_b, norm1_g, norm2_g, w_in, w_out, hy_conv_w, hy_conv_b, hy_filt_w1, hy_filt_b1, hy_filt_w2, hy_filt_b2, hy_filt_w3, hy_filt_b3, hy_filt_freq, hy_bias, ret_log_rate, attn_q_g, attn_k_g, ffn_w_gate, ffn_w_up, ffn_w_down, moe_router_w, moe_router_b, moe_w_gate, moe_w_up, moe_w_down, final_norm_g):
    assert x.shape[0] == 1 and c.shape[0] == 1
    seq_len = x.shape[1]
    x_lat = x[0]
    x_ctx = ctx[0]
    cos_t, sin_t = _rope_tables(seq_len)
    cvec = jnp.concatenate([c, c_ctx[None, :], jnp.zeros((6, D_MODEL), F32)], axis=0)
    cvec = _silu(cvec)
    row = lambda v: v.reshape(1, D_MODEL)
    zero_state = jnp.zeros((2, RET_WIDTH, RET_HEAD_DIM), F32)
    zero_tab = jnp.zeros((x_ctx.shape[0], LANES), F32)

    for l in range(DEPTH):
        last = l == DEPTH - 1
        mods = _ada_modulation(cvec, ada_w, ada_b, l).reshape(8, 6, D_MODEL)
        mod = [row(mods[0, i]) for i in range(6)]
        mod_c = [row(mods[1, i]) for i in range(6)]
        filt = (hy_filt_w1[l], hy_filt_b1[l], hy_filt_w2[l], hy_filt_b2[l],
                hy_filt_w3[l], hy_filt_b3[l], hy_filt_freq[l])
        lg_f = -jnp.exp(ret_log_rate[l, 0].astype(F32))
        lg_b = -jnp.exp(ret_log_rate[l, 1].astype(F32))
        mask, zeta, xi, chunk_decay = _ret_tables(lg_f, lg_b)
        w_in_l = w_in[l].astype(BF16)
        w_out_l = w_out[l].astype(BF16)
        n1 = row(norm1_g[l])
        n2 = row(norm2_g[l])

        p_hy, p_ret, p_att = _in_projection(x_lat, n1, mod[0], mod[1], w_in_l)
        pc_hy, pc_ret, pc_att = _in_projection(x_ctx, n1, mod_c[0], mod_c[1], w_in_l)

        sprev_c, s_ctx = _retention_states(pc_ret, zeta, chunk_decay, zero_state)
        qct, kc, vct, _ = _attention_prep(pc_att, zero_tab, zero_tab, attn_q_g[l], attn_k_g[l],
                                          rope=False)

        y_hy = _hyena_mixer(p_hy, hy_conv_w[l], hy_conv_b[l], filt, hy_bias[l])
        sprev, _ = _retention_states(p_ret, zeta, chunk_decay, s_ctx)
        y_ret = _retention_outputs(p_ret, mask, xi, sprev)
        qt, k_aug, vt, kmax_sq = _attention_prep(p_att, cos_t, sin_t, attn_q_g[l], attn_k_g[l],
                                                 rope=True)
        kmax = jnp.sqrt(jnp.max(kmax_sq[:, 0, ::ATT_HEAD_DIM], axis=0)) * KMAX_SLACK
        y_att = _attention(qt, kc, vct, kmax, k_aug, vt, tq=ATT_TQ, tk=ATT_TK, sub=ATT_SUB)

        if not last:
            yc_hy = _hyena_mixer(pc_hy, hy_conv_w[l], hy_conv_b[l], filt, hy_bias[l])
            yc_ret = _retention_outputs(pc_ret, mask, xi, sprev_c)
            yc_att = _attention(qct, kc, vct, tq=x_ctx.shape[0])
            x_ctx = _out_projection(yc_hy, yc_ret, yc_att, w_out_l, x_ctx, mod_c[2])
        x_lat = _out_projection(y_hy, y_ret, y_att, w_out_l, x_lat, mod[2])

        i = l // 2
        if l % 2 == 0:
            wg, wu, wd = (ffn_w_gate[i].astype(BF16), ffn_w_up[i].astype(BF16),
                          ffn_w_down[i].astype(BF16))
            x_lat = _dense_ffn(x_lat, n2, mod[3], mod[4], mod[5], wg, wu, wd)
            if not last:
                x_ctx = _dense_ffn(x_ctx, n2, mod_c[3], mod_c[4], mod_c[5], wg, wu, wd)
        else:
            wg, wu, wd = (moe_w_gate[i].astype(BF16), moe_w_up[i].astype(BF16),
                          moe_w_down[i].astype(BF16))
            fg = row(final_norm_g)
            x_lat = _moe_layer(x_lat, n2, mod[3], mod[4], mod[5], moe_router_w[i], moe_router_b[i],
                               wg, wu, wd, fg, final_norm=last)
            if not last:
                x_ctx = _moe_layer(x_ctx, n2, mod_c[3], mod_c[4], mod_c[5], moe_router_w[i],
                                   moe_router_b[i], wg, wu, wd, fg, final_norm=False)
    if DEPTH % 2 == 1:
        x_lat = _final_norm(x_lat, row(final_norm_g))
    return x_lat[None]
```

```python
import functools
import math

import numpy as np
import jax
import jax.numpy as jnp
from jax import lax
from jax.experimental import pallas as pl
from jax.experimental.pallas import tpu as pltpu

F32 = jnp.float32
BF16 = jnp.bfloat16

D_MODEL = 1024
DEPTH = 2
GRID_W = 64
EPS = 1e-6

HY_WIDTH = 256
HY_COLS = 3 * HY_WIDTH
HY_POS_BANDS = 16
HY_DECAY_TARGET = 1e-2
HY_SHORT_DECAY_PCT = 0.3
HY_LONG_DECAY_PCT = 1.5

RET_HEAD_DIM = 64
RET_HEADS = 4
RET_WIDTH = RET_HEADS * RET_HEAD_DIM
RET_COLS = 4 * RET_WIDTH
RET_CHUNK = 128

ATT_HEAD_DIM = 64
ATT_HEADS = 8
ATT_KV_HEADS = 2
ATT_GROUP = ATT_HEADS // ATT_KV_HEADS
ATT_WIDTH = ATT_HEADS * ATT_HEAD_DIM
ATT_KV_WIDTH = ATT_KV_HEADS * ATT_HEAD_DIM
ATT_COLS = ATT_WIDTH + 2 * ATT_KV_WIDTH
ROPE_AXIS_DIM = ATT_HEAD_DIM // 2
ROPE_BASE = 10000.0

MIX_WIDTH = HY_WIDTH + RET_WIDTH + ATT_WIDTH
IN_COLS = HY_COLS + RET_COLS + ATT_COLS

FFN_HIDDEN = 2816
N_EXPERTS = 8
TOP_K = 2
MOE_BLOCK = 512

LOG2_E = 1.4426950408889634
EXP2_CAP = 60.0
KMAX_SLACK = 1.0 + 2.0 ** -7
ATT_TQ = 512
ATT_TK = 4096
ATT_SUB = 512
ATT_VROWS = 80
RET_STATE_CHUNKS = 8
RET_OUT_CHUNKS = 4
SWIGLU_CHUNK = 1024
DMA_UNROLL = 8

LANES = 128
SUBLANES = 8
VMEM_LIMIT = 56 * 1024 * 1024


def _cparams(*sem):
    return pltpu.CompilerParams(dimension_semantics=sem, vmem_limit_bytes=VMEM_LIMIT)


def _dot(a, b):
    return jnp.dot(a, b, preferred_element_type=F32)


def _split(a):
    hi = a.astype(BF16)
    lo = (a - hi.astype(F32)).astype(BF16)
    return hi, lo


def _dot3(a, b):
    ah, al = _split(a)
    bh, bl = _split(b)
    return _dot(ah, bh) + _dot(al, bh) + _dot(ah, bl)


def _silu(x):
    return x * (1.0 / (1.0 + jnp.exp(-x)))


def _norm_mod(x, g, shift, scale):
    ms = jnp.mean(x * x, axis=-1, keepdims=True)
    h = x * lax.rsqrt(ms + EPS) * g
    return h * (1.0 + scale) + shift


def _ada_kernel(c_ref, w_ref, b_ref, o_ref):
    o_ref[...] = _dot3(c_ref[...], w_ref[0]) + b_ref[0]


def _ada_modulation(cc, w, b, layer):
    n = w.shape[2]
    tn = 1536
    return pl.pallas_call(
        _ada_kernel,
        out_shape=jax.ShapeDtypeStruct((8, n), F32),
        grid=(n // tn,),
        in_specs=[pl.BlockSpec((8, D_MODEL), lambda j: (0, 0)),
                  pl.BlockSpec((1, D_MODEL, tn), lambda j: (layer, 0, j)),
                  pl.BlockSpec((1, 1, tn), lambda j: (layer, 0, j))],
        out_specs=pl.BlockSpec((8, tn), lambda j: (0, j)),
        compiler_params=_cparams("parallel"),
        name="ada_modulation",
    )(cc, w, b.reshape(b.shape[0], 1, n))


def _head_mean_matrix(width):
    idx = np.arange(width) // ATT_HEAD_DIM
    return jnp.asarray((idx[:, None] == idx[None, :]).astype(np.float32) / ATT_HEAD_DIM, BF16)


def _head_rms(x, bd, g):
    x2h, x2l = _split(x * x)
    ms = _dot(x2h, bd) + _dot(x2l, bd)
    return x * lax.rsqrt(ms + EPS) * g


def _rope(x, cos, sin):
    n = x.shape[1]
    lane = lax.broadcasted_iota(jnp.int32, x.shape, 1)
    swapped = jnp.where((lane % 32) < 16, pltpu.roll(x, n - 16, 1), pltpu.roll(x, 16, 1))
    return x * cos + swapped * sin


def _attention_operands(p, cos_ref, sin_ref, qg_ref, kg_ref, bdq_ref, bdk_ref,
                        qt_ref, k_ref, vt_ref, kmax_ref, rope):
    q = _head_rms(p[:, :ATT_WIDTH], bdq_ref[...], qg_ref[...])
    k = _head_rms(p[:, ATT_WIDTH:ATT_WIDTH + ATT_KV_WIDTH], bdk_ref[...], kg_ref[...])
    if rope:
        cos = cos_ref[...]
        sin = sin_ref[...]
        q = jnp.concatenate(
            [_rope(q[:, j * LANES:(j + 1) * LANES], cos, sin) for j in range(ATT_WIDTH // LANES)],
            axis=1)
        k = _rope(k, cos, sin)
    qt_ref[...] = (q * (ATT_HEAD_DIM ** -0.5 * LOG2_E)).T.astype(BF16)
    tm = k.shape[0]
    lane = lax.broadcasted_iota(jnp.int32, k.shape, 1)
    minus_one_col = jnp.where(lane == ATT_HEAD_DIM, -1.0, 0.0)
    kb = k.astype(BF16)
    k_ref[0] = jnp.where(lane < ATT_HEAD_DIM, k, minus_one_col).astype(BF16)
    k_ref[1] = jnp.where(lane < ATT_HEAD_DIM, pltpu.roll(k, ATT_HEAD_DIM, 1), minus_one_col).astype(BF16)
    sq_hi, sq_lo = _split(kb.astype(F32) ** 2)
    norm_sq = (_dot(sq_hi, bdk_ref[...]) + _dot(sq_lo, bdk_ref[...])) * ATT_HEAD_DIM
    kmax_ref[0] = jnp.broadcast_to(jnp.max(norm_sq, axis=0, keepdims=True), (SUBLANES, LANES))
    vt = p[:, ATT_WIDTH + ATT_KV_WIDTH:].T.astype(BF16)
    sub = lax.broadcasted_iota(jnp.int32, (ATT_VROWS - ATT_HEAD_DIM, tm), 0)
    ones_row = jnp.where(sub == 0, 1.0, 0.0).astype(BF16)
    for h in range(ATT_KV_HEADS):
        vt_ref[h * ATT_VROWS:h * ATT_VROWS + ATT_HEAD_DIM, :] = vt[h * ATT_HEAD_DIM:(h + 1) * ATT_HEAD_DIM]
        vt_ref[h * ATT_VROWS + ATT_HEAD_DIM:(h + 1) * ATT_VROWS, :] = ones_row


def _inproj_kernel(x_ref, g_ref, sh_ref, sc_ref, w_ref, cos_ref, sin_ref, qg_ref, kg_ref, bdq_ref,
                   bdk_ref, hy_ref, ret_ref, qt_ref, k_ref, vt_ref, kmax_ref, *, rope):
    h = _norm_mod(x_ref[...], g_ref[...], sh_ref[...], sc_ref[...])
    p = _dot(h.astype(BF16), w_ref[...])
    hy_ref[...] = p[:, :HY_COLS]
    ret_ref[...] = p[:, HY_COLS:HY_COLS + RET_COLS]
    _attention_operands(p[:, HY_COLS + RET_COLS:], cos_ref, sin_ref, qg_ref, kg_ref, bdq_ref, bdk_ref,
                        qt_ref, k_ref, vt_ref, kmax_ref, rope)


def _in_projection(x, g, shift, scale, w_bf16, cos, sin, q_g, k_g, rope):
    t = x.shape[0]
    tm = min(t, 512)
    qg = jnp.tile(q_g, ATT_HEADS).reshape(1, ATT_WIDTH)
    kg = jnp.tile(k_g, ATT_KV_HEADS).reshape(1, ATT_KV_WIDTH)
    vec = pl.BlockSpec((1, D_MODEL), lambda i: (0, 0))
    const = lambda shape: pl.BlockSpec(shape, lambda i: (0, 0))
    return pl.pallas_call(
        functools.partial(_inproj_kernel, rope=rope),
        out_shape=(jax.ShapeDtypeStruct((t, HY_COLS), F32),
                   jax.ShapeDtypeStruct((t, RET_COLS), F32),
                   jax.ShapeDtypeStruct((ATT_WIDTH, t), BF16),
                   jax.ShapeDtypeStruct((ATT_KV_HEADS, t, LANES), BF16),
                   jax.ShapeDtypeStruct((ATT_KV_HEADS * ATT_VROWS, t), BF16),
                   jax.ShapeDtypeStruct((t // tm, SUBLANES, LANES), F32)),
        grid=(t // tm,),
        in_specs=[pl.BlockSpec((tm, D_MODEL), lambda i: (i, 0)), vec, vec, vec,
                  const((D_MODEL, IN_COLS)),
                  pl.BlockSpec((tm, LANES), lambda i: (i, 0)),
                  pl.BlockSpec((tm, LANES), lambda i: (i, 0)),
                  const((1, ATT_WIDTH)), const((1, ATT_KV_WIDTH)),
                  const((ATT_WIDTH, ATT_WIDTH)), const((ATT_KV_WIDTH, ATT_KV_WIDTH))],
        out_specs=(pl.BlockSpec((tm, HY_COLS), lambda i: (i, 0)),
                   pl.BlockSpec((tm, RET_COLS), lambda i: (i, 0)),
                   pl.BlockSpec((ATT_WIDTH, tm), lambda i: (0, i)),
                   pl.BlockSpec((ATT_KV_HEADS, tm, LANES), lambda i: (0, i, 0)),
                   pl.BlockSpec((ATT_KV_HEADS * ATT_VROWS, tm), lambda i: (0, i)),
                   pl.BlockSpec((1, SUBLANES, LANES), lambda i: (i, 0, 0))),
        compiler_params=_cparams("parallel"),
        name="in_projection",
    )(x, g, shift, scale, w_bf16, cos, sin, qg, kg,
      _head_mean_matrix(ATT_WIDTH), _head_mean_matrix(ATT_KV_WIDTH))


def _rope_tables(n_tokens):
    rows = n_tokens // GRID_W
    inv_freq = ROPE_BASE ** (-jnp.arange(0, ROPE_AXIS_DIM, 2, dtype=F32) / ROPE_AXIS_DIM)
    ang_r = jnp.arange(rows, dtype=F32)[:, None] * inv_freq[None, :]
    ang_c = jnp.arange(GRID_W, dtype=F32)[:, None] * inv_freq[None, :]
    nf = inv_freq.shape[0]
    by_row = lambda tab: jnp.broadcast_to(tab[:, None, :], (rows, GRID_W, nf)).reshape(n_tokens, nf)
    by_col = lambda tab: jnp.broadcast_to(tab[None, :, :], (rows, GRID_W, nf)).reshape(n_tokens, nf)
    cr, sr = by_row(jnp.cos(ang_r)), by_row(jnp.sin(ang_r))
    cc, sc = by_col(jnp.cos(ang_c)), by_col(jnp.sin(ang_c))
    cos = jnp.concatenate([cr, cr, cc, cc], axis=1)
    sin = jnp.concatenate([-sr, sr, -sc, sc], axis=1)
    return jnp.tile(cos, (1, 2)), jnp.tile(sin, (1, 2))


def _attention_kernel(*refs, tq, tk, sub, n_lat):
    if n_lat:
        kmax_ref, qt_ref, kc_ref, vct_ref, k_ref, vt_ref, o_ref, qs_ref, acc_ref, m_ref = refs
    else:
        qt_ref, kc_ref, vct_ref, o_ref, acc_ref = refs
    cols = ATT_GROUP * tq
    qt = qt_ref[...]
    qst = jnp.concatenate(
        [qt[g * ATT_HEAD_DIM:(g + 1) * ATT_HEAD_DIM, :] for g in range(ATT_GROUP)], axis=1)
    pad = jnp.zeros((LANES - ATT_HEAD_DIM, cols), BF16)
    s = _dot(kc_ref[0], jnp.concatenate([qst, pad], axis=0))
    mt = jnp.max(s, axis=0, keepdims=True).astype(BF16)
    mtf = mt.astype(F32)
    acc_ref[...] = _dot(vct_ref[...], jnp.exp2(s - mtf).astype(BF16))

    if n_lat:
        row = lax.broadcasted_iota(jnp.int32, pad.shape, 0)
        qs_ref[...] = jnp.concatenate([qst, jnp.where(row == 0, mtf, 0.0).astype(BF16)], axis=0)
        qf = qst.astype(F32)
        qn = jnp.sqrt(jnp.sum(qf * qf, axis=0, keepdims=True))
        fast = jnp.max(qn * kmax_ref[pl.program_id(0)] - mtf) <= EXP2_CAP

        def tile(j):
            off = pl.multiple_of(j * tk, tk)
            return k_ref[0, pl.ds(off, tk), :], vt_ref[:, pl.ds(off, tk)]

        @pl.when(fast)
        def _():
            def body(j, carry):
                k, vt = tile(j)
                qs = qs_ref[...]
                part = None
                for c in range(tk // sub):
                    p = jnp.exp2(_dot(k[c * sub:(c + 1) * sub], qs)).astype(BF16)
                    pv = _dot(vt[:, c * sub:(c + 1) * sub], p)
                    part = pv if part is None else part + pv
                acc_ref[...] += part
                return carry

            lax.fori_loop(0, n_lat, body, 0)

        @pl.when(jnp.logical_not(fast))
        def _():
            m_ref[...] = jnp.zeros_like(m_ref)

            def body(j, carry):
                k, vt = tile(j)
                s = _dot(k, qs_ref[...])
                m_old = m_ref[...]
                m_new = jnp.maximum(m_old, jnp.max(s, axis=0, keepdims=True))
                p = jnp.exp2(s - m_new).astype(BF16)
                acc_ref[...] = jnp.exp2(m_old - m_new) * acc_ref[...] + _dot(vt, p)
                m_ref[...] = m_new
                return carry

            lax.fori_loop(0, n_lat, body, 0)

    acc = acc_ref[...]
    o = acc[:ATT_HEAD_DIM] / acc[ATT_HEAD_DIM:ATT_HEAD_DIM + 1]
    o_ref[...] = jnp.concatenate(
        [o[:, g * tq:(g + 1) * tq].T for g in range(ATT_GROUP)], axis=1).astype(o_ref.dtype)


def _attention(qt, kc, vct, kmax=None, k=None, vt=None, *, tq, tk=0, sub=0):
    t = qt.shape[1]
    lc = kc.shape[1]
    gw = ATT_GROUP * ATT_HEAD_DIM
    cols = ATT_GROUP * tq
    in_specs = [pl.BlockSpec((gw, tq), lambda h, i: (h, i)),
                pl.BlockSpec((1, lc, LANES), lambda h, i: (h, 0, 0)),
                pl.BlockSpec((ATT_VROWS, lc), lambda h, i: (h, 0))]
    args = [qt, kc, vct]
    scratch = [pltpu.VMEM((ATT_VROWS, cols), F32)]
    n_lat = 0
    if k is not None:
        lk = k.shape[1]
        n_lat = lk // tk
        in_specs = ([pl.BlockSpec(memory_space=pltpu.SMEM)] + in_specs
                    + [pl.BlockSpec((1, lk, LANES), lambda h, i: (h, 0, 0)),
                       pl.BlockSpec((ATT_VROWS, lk), lambda h, i: (h, 0))])
        args = [kmax] + args + [k, vt]
        scratch = [pltpu.VMEM((LANES, cols), BF16)] + scratch + [pltpu.VMEM((1, cols), F32)]
    return pl.pallas_call(
        functools.partial(_attention_kernel, tq=tq, tk=tk, sub=sub, n_lat=n_lat),
        out_shape=jax.ShapeDtypeStruct((t, ATT_WIDTH), BF16),
        grid=(ATT_KV_HEADS, t // tq),
        in_specs=in_specs,
        out_specs=pl.BlockSpec((tq, gw), lambda h, i: (i, h)),
        scratch_shapes=scratch,
        compiler_params=_cparams("parallel", "parallel"),
        name="attention_lat" if n_lat else "attention_ctx",
    )(*args)


def _ret_tables(lg_f, lg_b):
    pos = jnp.arange(RET_CHUNK, dtype=F32)
    rel = pos[:, None] - pos[None, :]
    mask = jnp.where(rel > 0, jnp.exp(lg_f[:, None, None] * jnp.maximum(rel, 0.0)),
                     jnp.where(rel < 0, jnp.exp(lg_b[:, None, None] * jnp.maximum(-rel, 0.0)), 2.0))
    zeta = jnp.stack([jnp.exp(lg_f[:, None] * (RET_CHUNK - 1 - pos)[None, :]),
                      jnp.exp(lg_b[:, None] * pos[None, :])])
    xi = jnp.stack([jnp.exp(lg_f[:, None] * (pos + 1)[None, :]),
                    jnp.exp(lg_b[:, None] * (RET_CHUNK - pos)[None, :])])
    chunk_decay = jnp.stack([jnp.exp(lg_f * RET_CHUNK), jnp.exp(lg_b * RET_CHUNK)])
    zeta = jnp.repeat(zeta, RET_HEAD_DIM, axis=1).transpose(0, 2, 1)
    xi = jnp.repeat(xi, RET_HEAD_DIM, axis=1).transpose(0, 2, 1)
    chunk_decay = jnp.repeat(chunk_decay, RET_HEAD_DIM, axis=1)[:, :, None]
    return mask, zeta, xi, chunk_decay


def _ret_state_kernel(k_ref, v_ref, zeta_ref, cd_ref, s0_ref, sprev_ref, sfin_ref, s_ref, *, cps):
    d = pl.program_id(0)

    @pl.when(pl.program_id(1) == 0)
    def _():
        s_ref[...] = s0_ref[0]

    zeta = zeta_ref[0] * (RET_HEAD_DIM ** -0.5)
    cd = cd_ref[0]
    us = []
    for c in range(cps):
        rows = slice(c * RET_CHUNK, (c + 1) * RET_CHUNK)
        kz = (k_ref[rows, :] * zeta).astype(BF16)
        v = v_ref[rows, :].astype(BF16)
        heads = []
        for h in range(RET_HEADS):
            sl = slice(h * RET_HEAD_DIM, (h + 1) * RET_HEAD_DIM)
            heads.append(lax.dot_general(kz[:, sl], v[:, sl], (((0,), (0,)), ((), ())),
                                         preferred_element_type=F32))
        us.append(jnp.concatenate(heads, axis=0))

    def scan(order):
        s = s_ref[...]
        for c in order:
            sprev_ref[0, c] = s
            s = cd * s + us[c]
        s_ref[...] = s
        sfin_ref[0] = s

    @pl.when(d == 0)
    def _():
        scan(range(cps))

    @pl.when(d == 1)
    def _():
        scan(reversed(range(cps)))


def _retention_states(p_ret, zeta, chunk_decay, s0):
    nc = p_ret.shape[0] // RET_CHUNK
    cps = min(nc, RET_STATE_CHUNKS)
    ng = nc // cps
    rows = cps * RET_CHUNK

    def group(d, i):
        return i + d * (ng - 1 - 2 * i)

    return pl.pallas_call(
        functools.partial(_ret_state_kernel, cps=cps),
        out_shape=(jax.ShapeDtypeStruct((2, nc, RET_WIDTH, RET_HEAD_DIM), F32),
                   jax.ShapeDtypeStruct((2, RET_WIDTH, RET_HEAD_DIM), F32)),
        grid=(2, ng),
        in_specs=[pl.BlockSpec((rows, RET_WIDTH), lambda d, i: (group(d, i), 1)),
                  pl.BlockSpec((rows, RET_WIDTH), lambda d, i: (group(d, i), 2)),
                  pl.BlockSpec((1, RET_CHUNK, RET_WIDTH), lambda d, i: (d, 0, 0)),
                  pl.BlockSpec((1, RET_WIDTH, 1), lambda d, i: (d, 0, 0)),
                  pl.BlockSpec((1, RET_WIDTH, RET_HEAD_DIM), lambda d, i: (d, 0, 0))],
        out_specs=(pl.BlockSpec((1, cps, RET_WIDTH, RET_HEAD_DIM), lambda d, i: (d, group(d, i), 0, 0)),
                   pl.BlockSpec((1, RET_WIDTH, RET_HEAD_DIM), lambda d, i: (d, 0, 0))),
        scratch_shapes=[pltpu.VMEM((RET_WIDTH, RET_HEAD_DIM), F32)],
        compiler_params=_cparams("arbitrary", "arbitrary"),
        name="retention_states",
    )(p_ret, p_ret, zeta, chunk_decay, s0)


def _ret_out_kernel(q_ref, k_ref, v_ref, g_ref, mask_ref, xi_ref, sf_ref, sb_ref, y_ref, *, cpo):
    for c in range(cpo):
        rows = slice(c * RET_CHUNK, (c + 1) * RET_CHUNK)
        q = q_ref[rows, :]
        k = (k_ref[rows, :] * (RET_HEAD_DIM ** -0.5)).astype(BF16)
        v = v_ref[rows, :].astype(BF16)
        qb = q.astype(BF16)
        qf = (q * xi_ref[0]).astype(BF16)
        qr = (q * xi_ref[1]).astype(BF16)
        sf = sf_ref[0, c].astype(BF16)
        sb = sb_ref[0, c].astype(BF16)
        ys = []
        for h in range(RET_HEADS):
            sl = slice(h * RET_HEAD_DIM, (h + 1) * RET_HEAD_DIM)
            scores = lax.dot_general(qb[:, sl], k[:, sl], (((1,), (1,)), ((), ())),
                                     preferred_element_type=F32)
            y = _dot((scores * mask_ref[h]).astype(BF16), v[:, sl])
            y = y + _dot(qf[:, sl], sf[sl, :]) + _dot(qr[:, sl], sb[sl, :])
            mu = jnp.mean(y, axis=-1, keepdims=True)
            yc = y - mu
            var = jnp.mean(yc * yc, axis=-1, keepdims=True)
            ys.append(yc * lax.rsqrt(var + EPS))
        y_ref[rows, :] = (_silu(g_ref[rows, :]) * jnp.concatenate(ys, axis=1)).astype(y_ref.dtype)


def _retention_outputs(p_ret, mask, xi, sprev):
    nc = p_ret.shape[0] // RET_CHUNK
    cpo = min(nc, RET_OUT_CHUNKS)
    rows = cpo * RET_CHUNK
    col = lambda c: pl.BlockSpec((rows, RET_WIDTH), lambda i: (i, c))
    return pl.pallas_call(
        functools.partial(_ret_out_kernel, cpo=cpo),
        out_shape=jax.ShapeDtypeStruct((p_ret.shape[0], RET_WIDTH), BF16),
        grid=(nc // cpo,),
        in_specs=[col(0), col(1), col(2), col(3),
                  pl.BlockSpec((RET_HEADS, RET_CHUNK, RET_CHUNK), lambda i: (0, 0, 0)),
                  pl.BlockSpec((2, RET_CHUNK, RET_WIDTH), lambda i: (0, 0, 0)),
                  pl.BlockSpec((1, cpo, RET_WIDTH, RET_HEAD_DIM), lambda i: (0, i, 0, 0)),
                  pl.BlockSpec((1, cpo, RET_WIDTH, RET_HEAD_DIM), lambda i: (1, i, 0, 0))],
        out_specs=pl.BlockSpec((rows, RET_WIDTH), lambda i: (i, 0)),
        compiler_params=_cparams("parallel"),
        name="retention_outputs",
    )(p_ret, p_ret, p_ret, p_ret, mask, xi, sprev, sprev)


def _hy_pre_kernel(p_ref, prev_ref, next_ref, w_ref, b_ref, x0_ref, z_ref):
    i = pl.program_id(0)
    x = p_ref[...]
    tm = x.shape[0]
    row = lax.broadcasted_iota(jnp.int32, x.shape, 0)
    prev_row = jnp.where(i == 0, 0.0, prev_ref[7:8, :])
    next_row = jnp.where(i == pl.num_programs(0) - 1, 0.0, next_ref[0:1, :])
    up = jnp.where(row == 0, prev_row, pltpu.roll(x, 1, 0))
    dn = jnp.where(row == tm - 1, next_row, pltpu.roll(x, tm - 1, 0))
    w = w_ref[...]
    u = up * w[0:1] + x * w[1:2] + dn * w[2:3] + b_ref[...]
    x0_ref[...] = u[:, :HY_WIDTH]
    z_ref[...] = u[:, HY_WIDTH:2 * HY_WIDTH] * u[:, 2 * HY_WIDTH:]


def _hyena_pre(p_hy, conv_w, conv_b):
    t = p_hy.shape[0]
    tm = min(t, 512)
    nb8 = tm // 8
    last8 = t // 8 - 1
    return pl.pallas_call(
        _hy_pre_kernel,
        out_shape=(jax.ShapeDtypeStruct((t, HY_WIDTH), F32),
                   jax.ShapeDtypeStruct((t, HY_WIDTH), F32)),
        grid=(t // tm,),
        in_specs=[pl.BlockSpec((tm, HY_COLS), lambda i: (i, 0)),
                  pl.BlockSpec((8, HY_COLS), lambda i: (jnp.maximum(i * nb8 - 1, 0), 0)),
                  pl.BlockSpec((8, HY_COLS), lambda i: (jnp.minimum((i + 1) * nb8, last8), 0)),
                  pl.BlockSpec((3, HY_COLS), lambda i: (0, 0)),
                  pl.BlockSpec((1, HY_COLS), lambda i: (0, 0))],
        out_specs=(pl.BlockSpec((tm, HY_WIDTH), lambda i: (i, 0)),
                   pl.BlockSpec((tm, HY_WIDTH), lambda i: (i, 0))),
        compiler_params=_cparams("parallel"),
        name="hyena_pre",
    )(p_hy, p_hy, p_hy, conv_w, conv_b.reshape(1, HY_COLS))


def _hy_filter_kernel(feat_ref, w1_ref, b1_ref, w2_ref, b2_ref, w3_ref, b3_ref, freq_ref,
                      delta_ref, taps_ref, asum_ref, *, seq_len):
    i = pl.program_id(0)
    feat = feat_ref[...]
    tm = feat.shape[0]
    half = tm // 2
    freq = freq_ref[...]
    packed = jnp.concatenate([feat[:half], feat[half:]], axis=1)
    h = jnp.sin(freq * (_dot3(packed, w1_ref[...]) + b1_ref[...]))
    h = jnp.sin(freq * (_dot3(h, w2_ref[...]) + b2_ref[...]))
    h = _dot3(h, w3_ref[0])
    h = jnp.concatenate([h[:, :HY_WIDTH], h[:, HY_WIDTH:]], axis=0) + b3_ref[0]
    j = i * tm + lax.broadcasted_iota(jnp.int32, (tm, HY_WIDTH), 0)
    taps = jnp.where(j == seq_len, 0.0, h * jnp.exp(-feat[:, 0:1] * delta_ref[...]))
    taps_ref[...] = taps

    @pl.when(i == 0)
    def _():
        asum_ref[...] = jnp.zeros_like(asum_ref)

    asum_ref[...] += jnp.sum(jnp.abs(taps).reshape(tm // SUBLANES, SUBLANES, HY_WIDTH), axis=0)


def _position_features(seq_len):
    lo = 256
    hi = seq_len // lo + 1
    bands = np.linspace(1e-4, HY_POS_BANDS - 1, HY_POS_BANDS).astype(np.float32).astype(np.float64)
    ang_hi = (2.0 * np.pi / seq_len) * lo * np.arange(hi)[:, None] * bands[None, :]
    ang_lo = (2.0 * np.pi / seq_len) * np.arange(lo)[:, None] * bands[None, :]
    tab = lambda a: jnp.asarray(a, F32)
    ch, sh = tab(np.cos(ang_hi))[:, None, :], tab(np.sin(ang_hi))[:, None, :]
    cl, sl = tab(np.cos(ang_lo))[None, :, :], tab(np.sin(ang_lo))[None, :, :]
    npos = seq_len + 1
    cos = (ch * cl - sh * sl).reshape(hi * lo, HY_POS_BANDS)[:npos]
    sin = (sh * cl + ch * sl).reshape(hi * lo, HY_POS_BANDS)[:npos]
    c_end, s_end = tab(np.cos(2.0 * np.pi * bands))[None, :], tab(np.sin(2.0 * np.pi * bands))[None, :]
    cos_b = c_end * cos[1:seq_len] + s_end * sin[1:seq_len]
    sin_b = s_end * cos[1:seq_len] - c_end * sin[1:seq_len]
    lag = jnp.concatenate([jnp.arange(npos, dtype=F32), jnp.arange(seq_len - 1, 0, -1, dtype=F32)])
    t = lag / max(seq_len - 1, 1)
    cos = jnp.concatenate([cos, cos_b], axis=0)
    sin = jnp.concatenate([sin, sin_b], axis=0)
    return jnp.concatenate([t[:, None], cos, -sin,
                            jnp.zeros((2 * seq_len, LANES - 1 - 2 * HY_POS_BANDS), F32)], axis=-1)


def _hyena_taps(seq_len, w1, b1, w2, b2, w3, b3, freq):
    n = 2 * seq_len
    feat = _position_features(seq_len)
    w1p = jnp.concatenate([w1, jnp.zeros((LANES - w1.shape[0], w1.shape[1]), F32)], axis=0)
    pair = lambda w: jnp.concatenate(
        [jnp.concatenate([w, jnp.zeros_like(w)], axis=1),
         jnp.concatenate([jnp.zeros_like(w), w], axis=1)], axis=0)
    twice = lambda v: jnp.tile(v.reshape(1, -1), (1, 2))
    w3_dirs = jnp.stack([pair(w3[:, :HY_WIDTH]), pair(w3[:, HY_WIDTH:])])
    b3_dirs = b3.reshape(2, 1, HY_WIDTH)
    deltas = jnp.abs(jnp.linspace(math.log(HY_DECAY_TARGET) / HY_LONG_DECAY_PCT,
                                  math.log(HY_DECAY_TARGET) / HY_SHORT_DECAY_PCT,
                                  HY_WIDTH, dtype=F32)).reshape(1, HY_WIDTH)
    tm = min(seq_len, 1024)
    per_dir = seq_len // tm
    hid2 = 2 * w1.shape[1]
    const = lambda shape: pl.BlockSpec(shape, lambda i: (0, 0))
    by_dir = lambda shape: pl.BlockSpec(shape, lambda i: (i // per_dir, 0, 0))
    return pl.pallas_call(
        functools.partial(_hy_filter_kernel, seq_len=seq_len),
        out_shape=(jax.ShapeDtypeStruct((n, HY_WIDTH), F32),
                   jax.ShapeDtypeStruct((8, HY_WIDTH), F32)),
        grid=(n // tm,),
        in_specs=[pl.BlockSpec((tm, LANES), lambda i: (i, 0)),
                  const((2 * LANES, hid2)), const((1, hid2)), const((hid2, hid2)), const((1, hid2)),
                  by_dir((1, hid2, 2 * HY_WIDTH)), by_dir((1, 1, HY_WIDTH)), const((1, hid2)),
                  const((1, HY_WIDTH))],
        out_specs=(pl.BlockSpec((tm, HY_WIDTH), lambda i: (i, 0)), const((8, HY_WIDTH))),
        compiler_params=_cparams("arbitrary"),
        name="hyena_taps",
    )(feat, pair(w1p), twice(b1), pair(w2), twice(b2), w3_dirs, b3_dirs, twice(freq), deltas)


FFT_N2 = 256
FFT_K1_PER_STEP = 4
FFT_N2_PER_STEP = 16


def _dft_tables(n_total, n1_in):
    n1 = n_total // FFT_N2
    k1 = np.arange(n1)
    a = 2.0 * np.pi * np.outer(k1, np.arange(n1_in)) / n1
    lvl1 = np.concatenate([np.cos(a), -np.sin(a)], axis=0)
    n2 = np.arange(FFT_N2)
    th = 2.0 * np.pi * np.outer(n2, n2) / FFT_N2
    fc, fs = np.cos(th), np.sin(th)
    g = np.block([[fc, fs], [-fs, fc]])
    ph = 2.0 * np.pi * np.outer(k1, n2) / n_total
    tw = np.stack([np.cos(ph), np.sin(ph)])[..., None]
    return (jnp.asarray(lvl1, F32), jnp.asarray(g, F32), jnp.asarray(g.T, F32), jnp.asarray(tw, F32))


def _inverse_lvl1_table(n_total, n1_out):
    n1 = n_total // FFT_N2
    a = 2.0 * np.pi * np.outer(np.arange(n1_out), np.arange(n1)) / n1
    return jnp.asarray(np.concatenate([np.cos(a), -np.sin(a)], axis=1), F32)


def _lvl2_kernel(b_ref, tw_ref, gh_ref, gl_ref, *rest, inverse):
    for j in range(b_ref.shape[1]):
        c = tw_ref[0, j]
        s = tw_ref[1, j]
        br = b_ref[0, j]
        bi = b_ref[1, j]
        d = jnp.concatenate([c * br + s * bi, c * bi - s * br], axis=0)
        dh, dl = _split(d)
        x = _dot(gh_ref[...], dh) + _dot(gh_ref[...], dl) + _dot(gl_ref[...], dh)
        if not inverse:
            (o_ref,) = rest
            o_ref[0, j] = x[:FFT_N2]
            o_ref[1, j] = x[FFT_N2:]
            continue
        h_ref, gth_ref, gtl_ref, o_ref = rest
        xr, xi = x[:FFT_N2], x[FFT_N2:]
        hr, hi = h_ref[0, j], h_ref[1, j]
        y = jnp.concatenate([xr * hr - xi * hi, xr * hi + xi * hr], axis=0)
        yh, yl = _split(y)
        cc = _dot(gth_ref[...], yh) + _dot(gth_ref[...], yl) + _dot(gtl_ref[...], yh)
        cr, ci = cc[:FFT_N2], cc[FFT_N2:]
        o_ref[0, j] = c * cr - s * ci
        o_ref[1, j] = c * ci + s * cr


def _level2(b, tw, g, h=None, gt=None):
    _, n1, n2, ch = b.shape
    kb = FFT_K1_PER_STEP
    blk = pl.BlockSpec((2, kb, n2, ch), lambda i: (0, i, 0, 0))
    const = pl.BlockSpec((2 * n2, 2 * n2), lambda i: (0, 0))
    gh, gl = _split(g)
    in_specs = [blk, pl.BlockSpec((2, kb, n2, 1), lambda i: (0, i, 0, 0)), const, const]
    args = [b, tw, gh, gl]
    if h is not None:
        gth, gtl = _split(gt)
        in_specs += [blk, const, const]
        args += [h, gth, gtl]
    return pl.pallas_call(
        functools.partial(_lvl2_kernel, inverse=h is not None),
        out_shape=jax.ShapeDtypeStruct(b.shape, F32),
        grid=(n1 // kb,),
        in_specs=in_specs,
        out_specs=blk,
        compiler_params=_cparams("parallel"),
        name="hyena_fft_level2" + ("_conv" if h is not None else ""),
    )(*args)


def _lvl1_kernel(a_ref, x_ref, o_ref, *, inverse):
    tn2 = x_ref.shape[-2]
    ch = x_ref.shape[-1]
    if inverse:
        x = jnp.concatenate([x_ref[:, :, j, :].reshape(-1, ch) for j in range(tn2)], axis=1)
    else:
        x = jnp.concatenate([x_ref[:, j, :] for j in range(tn2)], axis=1)
    out = _dot3(a_ref[...], x)
    for j in range(tn2):
        slab = out[:, j * ch:(j + 1) * ch]
        if inverse:
            o_ref[:, j, :] = slab
        else:
            o_ref[:, :, j, :] = slab.reshape(2, slab.shape[0] // 2, ch)


def _level1(a, x, name):
    n1_in, n2, ch = x.shape
    n1 = a.shape[0] // 2
    tn2 = FFT_N2_PER_STEP
    return pl.pallas_call(
        functools.partial(_lvl1_kernel, inverse=False),
        out_shape=jax.ShapeDtypeStruct((2, n1, n2, ch), F32),
        grid=(n2 // tn2,),
        in_specs=[pl.BlockSpec(a.shape, lambda i: (0, 0)),
                  pl.BlockSpec((n1_in, tn2, ch), lambda i: (0, i, 0))],
        out_specs=pl.BlockSpec((2, n1, tn2, ch), lambda i: (0, 0, i, 0)),
        compiler_params=_cparams("parallel"),
        name=name,
    )(a, x)


def _inverse_level1(a, x):
    _, n1, n2, ch = x.shape
    n1_out = a.shape[0]
    tn2 = FFT_N2_PER_STEP
    return pl.pallas_call(
        functools.partial(_lvl1_kernel, inverse=True),
        out_shape=jax.ShapeDtypeStruct((n1_out, n2, ch), F32),
        grid=(n2 // tn2,),
        in_specs=[pl.BlockSpec(a.shape, lambda i: (0, 0)),
                  pl.BlockSpec((2, n1, tn2, ch), lambda i: (0, 0, i, 0))],
        out_specs=pl.BlockSpec((n1_out, tn2, ch), lambda i: (0, i, 0)),
        compiler_params=_cparams("parallel"),
        name="hyena_fft_inverse_level1",
    )(a, x)


def _hyena_long_conv(z, taps):
    seq_len, ch = z.shape
    n = 2 * seq_len
    n1 = n // FFT_N2
    lvl1_z, g, gt, tw = _dft_tables(n, n1 // 2)
    lvl1_t = _dft_tables(n, n1)[0]
    inv1 = _inverse_lvl1_table(n, n1 // 2)
    hb = _level1(lvl1_t, taps.reshape(n1, FFT_N2, ch), "hyena_fft_level1_taps")
    hspec = _level2(hb, tw, g)
    zb = _level1(lvl1_z, z.reshape(n1 // 2, FFT_N2, ch), "hyena_fft_level1_z")
    cspec = _level2(zb, tw, g, hspec, gt)
    return _inverse_level1(inv1, cspec).reshape(seq_len, ch)


def _small_conv_kernel(z_ref, taps_ref, fc_ref, fs_ref, o_ref):
    seq_len = z_ref.shape[0]
    fc = fc_ref[...]
    fs = fs_ref[...]
    z = z_ref[...]
    taps = taps_ref[...]
    zr = _dot3(fc[:, :seq_len], z)
    zi = -_dot3(fs[:, :seq_len], z)
    hr = _dot3(fc, taps)
    hi = -_dot3(fs, taps)
    yr = zr * hr - zi * hi
    yi = zr * hi + zi * hr
    o_ref[...] = _dot3(fc[:seq_len, :], yr) - _dot3(fs[:seq_len, :], yi)


def _hyena_small_conv(z, taps):
    seq_len, ch = z.shape
    n = 2 * seq_len
    th = 2.0 * np.pi * np.outer(np.arange(n), np.arange(n)) / n
    full = lambda shape: pl.BlockSpec(shape, lambda: (0, 0))
    return pl.pallas_call(
        _small_conv_kernel,
        out_shape=jax.ShapeDtypeStruct((seq_len, ch), F32),
        in_specs=[full((seq_len, ch)), full((n, ch)), full((n, n)), full((n, n))],
        out_specs=full((seq_len, ch)),
        compiler_params=pltpu.CompilerParams(vmem_limit_bytes=VMEM_LIMIT),
        name="hyena_small_conv",
    )(z, taps, jnp.asarray(np.cos(th), F32), jnp.asarray(np.sin(th), F32))


def _hy_post_kernel(x0_ref, z_ref, conv_ref, asum_ref, bias_ref, y_ref, *, n_total):
    norm = jnp.sum(asum_ref[...], axis=0, keepdims=True) + EPS
    conv = conv_ref[...] * (1.0 / (n_total * norm))
    y_ref[...] = (x0_ref[...] * (conv + bias_ref[...] * z_ref[...])).astype(y_ref.dtype)


def _hyena_post(x0, z, conv, asum, bias):
    t = x0.shape[0]
    tm = min(t, 1024)
    row = pl.BlockSpec((tm, HY_WIDTH), lambda i: (i, 0))
    return pl.pallas_call(
        functools.partial(_hy_post_kernel, n_total=2 * t),
        out_shape=jax.ShapeDtypeStruct((t, HY_WIDTH), BF16),
        grid=(t // tm,),
        in_specs=[row, row, row, pl.BlockSpec((8, HY_WIDTH), lambda i: (0, 0)),
                  pl.BlockSpec((1, HY_WIDTH), lambda i: (0, 0))],
        out_specs=row,
        compiler_params=_cparams("parallel"),
        name="hyena_post",
    )(x0, z, conv, asum, bias.reshape(1, HY_WIDTH))


def _hyena_mixer(p_hy, conv_w, conv_b, filt, bias):
    seq_len = p_hy.shape[0]
    x0, z = _hyena_pre(p_hy, conv_w, conv_b)
    taps, asum = _hyena_taps(seq_len, *filt)
    if 2 * seq_len // FFT_N2 >= 16:
        conv = _hyena_long_conv(z, taps)
    else:
        conv = _hyena_small_conv(z, taps)
    return _hyena_post(x0, z, conv, asum, bias)


def _mixed_residual(hy_ref, ret_ref, att_ref, wo_ref, x_ref, mgate_ref):
    y = (_dot(hy_ref[...], wo_ref[:HY_WIDTH, :])
         + _dot(ret_ref[...], wo_ref[HY_WIDTH:HY_WIDTH + RET_WIDTH, :])
         + _dot(att_ref[...], wo_ref[HY_WIDTH + RET_WIDTH:, :]))
    return x_ref[...] + mgate_ref[...] * y


def _mixer_specs(tm, index):
    row = lambda w: pl.BlockSpec((tm, w), lambda i, *_: (i, 0))
    return [row(HY_WIDTH), row(RET_WIDTH), row(ATT_WIDTH),
            pl.BlockSpec((MIX_WIDTH, D_MODEL), index), row(D_MODEL),
            pl.BlockSpec((1, D_MODEL), index)]


def _swiglu(x, wg_ref, wu_ref, wd_ref):
    hidden = wg_ref.shape[1]
    y = None
    for lo in range(0, hidden, SWIGLU_CHUNK):
        hi = min(lo + SWIGLU_CHUNK, hidden)
        a = _silu(_dot(x, wg_ref[:, lo:hi])) * _dot(x, wu_ref[:, lo:hi])
        part = _dot(a.astype(BF16), wd_ref[lo:hi, :])
        y = part if y is None else y + part
    return y


def _ffn_kernel(hy_ref, ret_ref, att_ref, wo_ref, x_ref, mgate_ref, g_ref, sh_ref, sc_ref, gate_ref,
                wg_ref, wu_ref, wd_ref, o_ref):
    x = _mixed_residual(hy_ref, ret_ref, att_ref, wo_ref, x_ref, mgate_ref)
    h = _norm_mod(x, g_ref[...], sh_ref[...], sc_ref[...]).astype(BF16)
    o_ref[...] = x + gate_ref[...] * _swiglu(h, wg_ref, wu_ref, wd_ref)


def _dense_ffn(mixed, g, shift, scale, gate, wg, wu, wd):
    t = mixed[4].shape[0]
    tm = min(t, 512)
    index = lambda i: (0, 0)
    vec = pl.BlockSpec((1, D_MODEL), index)
    resident = lambda shape: pl.BlockSpec(shape, index, pipeline_mode=pl.Buffered(1))
    return pl.pallas_call(
        _ffn_kernel,
        out_shape=jax.ShapeDtypeStruct((t, D_MODEL), F32),
        grid=(t // tm,),
        in_specs=_mixer_specs(tm, index) + [vec, vec, vec, vec,
                  resident((D_MODEL, FFN_HIDDEN)), resident((D_MODEL, FFN_HIDDEN)),
                  resident((FFN_HIDDEN, D_MODEL))],
        out_specs=pl.BlockSpec((tm, D_MODEL), lambda i: (i, 0)),
        compiler_params=_cparams("parallel"),
        name="dense_ffn",
    )(*mixed, g, shift, scale, gate, wg, wu, wd)


def _moe_route_kernel(hy_ref, ret_ref, att_ref, wo_ref, x_ref, mgate_ref, g_ref, sh_ref, sc_ref,
                      rw_ref, rb_ref, xo_ref, h_ref, info_ref):
    x = _mixed_residual(hy_ref, ret_ref, att_ref, wo_ref, x_ref, mgate_ref)
    xo_ref[...] = x
    h = _norm_mod(x, g_ref[...], sh_ref[...], sc_ref[...])
    _to_row_tiles(h_ref, h)
    logits = _dot3(h, rw_ref[...]) + rb_ref[...]
    lane = lax.broadcasted_iota(jnp.int32, logits.shape, 1)
    v1 = jnp.max(logits, axis=-1, keepdims=True)
    i1 = jnp.min(jnp.where(logits == v1, lane, LANES), axis=-1, keepdims=True)
    rest = jnp.where(lane == i1, -jnp.inf, logits)
    v2 = jnp.max(rest, axis=-1, keepdims=True)
    i2 = jnp.min(jnp.where(rest == v2, lane, LANES), axis=-1, keepdims=True)
    e = jnp.exp(v2 - v1)
    g1 = 1.0 / (1.0 + e)
    g2 = e * g1
    info_ref[...] = jnp.where(lane == 0, i1.astype(F32),
                              jnp.where(lane == 1, i2.astype(F32),
                                        jnp.where(lane == 2, g1, jnp.where(lane == 3, g2, 0.0))))


def _moe_route(mixed, g, shift, scale, router_w, router_b):
    t = mixed[4].shape[0]
    tm = min(t, 512)
    rw = jnp.concatenate([router_w, jnp.zeros((D_MODEL, LANES - N_EXPERTS), F32)], axis=1)
    rb = jnp.concatenate([router_b, jnp.full((LANES - N_EXPERTS,), -jnp.inf, F32)]).reshape(1, LANES)
    vec = pl.BlockSpec((1, D_MODEL), lambda i: (0, 0))
    return pl.pallas_call(
        _moe_route_kernel,
        out_shape=(jax.ShapeDtypeStruct((t, D_MODEL), F32),
                   jax.ShapeDtypeStruct((t * SUBLANES, LANES), F32),
                   jax.ShapeDtypeStruct((t, LANES), F32)),
        grid=(t // tm,),
        in_specs=_mixer_specs(tm, lambda i: (0, 0)) + [vec, vec, vec,
                  pl.BlockSpec((D_MODEL, LANES), lambda i: (0, 0)),
                  pl.BlockSpec((1, LANES), lambda i: (0, 0))],
        out_specs=(pl.BlockSpec((tm, D_MODEL), lambda i: (i, 0)),
                   pl.BlockSpec((tm * SUBLANES, LANES), lambda i: (i, 0)),
                   pl.BlockSpec((tm, LANES), lambda i: (i, 0))),
        compiler_params=_cparams("parallel"),
        name="moe_route",
    )(*mixed, g, shift, scale, rw, rb)


def _to_row_tiles(ref, x):
    rows = x.shape[0]
    for k in range(D_MODEL // LANES):
        ref[pl.ds(k, rows, stride=SUBLANES), :] = x[:, k * LANES:(k + 1) * LANES]


def _from_row_tiles(ref, rows):
    return jnp.concatenate(
        [ref[pl.ds(k, rows, stride=SUBLANES), :] for k in range(D_MODEL // LANES)], axis=1)


def _row_copy(src_hbm, dst_vmem, sem, src_row, dst_row):
    src = pl.multiple_of(src_row * SUBLANES, SUBLANES)
    dst = pl.multiple_of(dst_row * SUBLANES, SUBLANES)
    return pltpu.make_async_copy(src_hbm.at[pl.ds(src, SUBLANES)], dst_vmem.at[pl.ds(dst, SUBLANES)], sem)


def _moe_expert_kernel(blk_e_ref, nused_ref, dest_ref, fill_ref, h_hbm, wg_ref, wu_ref, wd_ref,
                       y_ref, xbuf_ref, src_ref, sem):
    b = pl.program_id(0)
    n_used = nused_ref[0]

    @pl.when(b == 0)
    def _():
        def invert(t, carry):
            for j in range(TOP_K):
                src_ref[dest_ref[TOP_K * t + j]] = t
            return carry

        lax.fori_loop(0, dest_ref.shape[0] // TOP_K, invert, 0, unroll=DMA_UNROLL)
        for e in range(N_EXPERTS):
            def pad(d, carry):
                src_ref[d] = 0
                return carry

            lax.fori_loop(fill_ref[0, e], fill_ref[1, e], pad, 0)

    def issue_rows(blk, slot):
        def body(r, carry):
            _row_copy(h_hbm, xbuf_ref.at[slot], sem.at[slot], src_ref[blk * MOE_BLOCK + r], r).start()
            return carry

        lax.fori_loop(0, MOE_BLOCK, body, 0, unroll=DMA_UNROLL)

    def wait_rows(slot):
        def body(r, carry):
            _row_copy(h_hbm, xbuf_ref.at[slot], sem.at[slot], 0, r).wait()
            return carry

        lax.fori_loop(0, MOE_BLOCK, body, 0, unroll=DMA_UNROLL)

    @pl.when(b < n_used)
    def _():
        slot = b % 2

        @pl.when(b == 0)
        def _():
            issue_rows(0, 0)

        wait_rows(slot)

        @pl.when(b + 1 < n_used)
        def _():
            issue_rows(b + 1, 1 - slot)

        x = _from_row_tiles(xbuf_ref.at[slot], MOE_BLOCK).astype(BF16)
        _to_row_tiles(y_ref, _swiglu(x, wg_ref.at[0], wu_ref.at[0], wd_ref.at[0]))

    @pl.when(b >= n_used)
    def _():
        y_ref[...] = jnp.zeros_like(y_ref)


def _moe_experts(h, blk_expert, n_used, dest, fill, wg, wu, wd, n_blk):
    def w_index(b, be, nu, de, fi):
        return be[jnp.minimum(b, nu[0] - 1)], 0, 0

    grid_spec = pltpu.PrefetchScalarGridSpec(
        num_scalar_prefetch=4,
        grid=(n_blk,),
        in_specs=[pl.BlockSpec(memory_space=pl.ANY),
                  pl.BlockSpec((1, D_MODEL, FFN_HIDDEN), w_index),
                  pl.BlockSpec((1, D_MODEL, FFN_HIDDEN), w_index),
                  pl.BlockSpec((1, FFN_HIDDEN, D_MODEL), w_index)],
        out_specs=pl.BlockSpec((MOE_BLOCK * SUBLANES, LANES), lambda b, be, nu, de, fi: (b, 0)),
        scratch_shapes=[pltpu.VMEM((2, MOE_BLOCK * SUBLANES, LANES), F32),
                        pltpu.SMEM((n_blk * MOE_BLOCK,), jnp.int32),
                        pltpu.SemaphoreType.DMA((2,))],
    )
    return pl.pallas_call(
        _moe_expert_kernel,
        out_shape=jax.ShapeDtypeStruct((n_blk * MOE_BLOCK * SUBLANES, LANES), F32),
        grid_spec=grid_spec,
        compiler_params=_cparams("arbitrary"),
        name="moe_experts",
    )(blk_expert, n_used, dest, fill, h, wg, wu, wd)


def _moe_combine_kernel(dest_ref, x_ref, info_ref, gate_ref, ng_ref, y_hbm, o_ref,
                        y1_ref, y2_ref, sem, *, final_norm):
    i = pl.program_id(0)
    tm = x_ref.shape[0]
    slot = i % 2

    def issue_rows(blk, slot):
        def body(r, carry):
            a = 2 * (blk * tm + r)
            _row_copy(y_hbm, y1_ref.at[slot], sem.at[slot], dest_ref[a], r).start()
            _row_copy(y_hbm, y2_ref.at[slot], sem.at[slot], dest_ref[a + 1], r).start()
            return carry

        lax.fori_loop(0, tm, body, 0, unroll=DMA_UNROLL)

    def wait_rows(slot):
        def body(r, carry):
            _row_copy(y_hbm, y1_ref.at[slot], sem.at[slot], 0, r).wait()
            _row_copy(y_hbm, y2_ref.at[slot], sem.at[slot], 0, r).wait()
            return carry

        lax.fori_loop(0, tm, body, 0, unroll=DMA_UNROLL)

    @pl.when(i == 0)
    def _():
        issue_rows(0, 0)

    wait_rows(slot)

    @pl.when(i + 1 < pl.num_programs(0))
    def _():
        issue_rows(i + 1, 1 - slot)

    info = info_ref[...]
    y = (_from_row_tiles(y1_ref.at[slot], tm) * info[:, 2:3]
         + _from_row_tiles(y2_ref.at[slot], tm) * info[:, 3:4])
    x = x_ref[...] + gate_ref[...] * y
    if final_norm:
        ms = jnp.mean(x * x, axis=-1, keepdims=True)
        x = x * lax.rsqrt(ms + EPS) * ng_ref[...]
    o_ref[...] = x


def _moe_combine(dest, x, info, gate, norm_g, y, final_norm):
    t = x.shape[0]
    tm = min(t, 256)
    vec = pl.BlockSpec((1, D_MODEL), lambda i, d: (0, 0))
    grid_spec = pltpu.PrefetchScalarGridSpec(
        num_scalar_prefetch=1,
        grid=(t // tm,),
        in_specs=[pl.BlockSpec((tm, D_MODEL), lambda i, d: (i, 0)),
                  pl.BlockSpec((tm, LANES), lambda i, d: (i, 0)), vec, vec,
                  pl.BlockSpec(memory_space=pl.ANY)],
        out_specs=pl.BlockSpec((tm, D_MODEL), lambda i, d: (i, 0)),
        scratch_shapes=[pltpu.VMEM((2, tm * SUBLANES, LANES), F32),
                        pltpu.VMEM((2, tm * SUBLANES, LANES), F32),
                        pltpu.SemaphoreType.DMA((2,))],
    )
    return pl.pallas_call(
        functools.partial(_moe_combine_kernel, final_norm=final_norm),
        out_shape=jax.ShapeDtypeStruct((t, D_MODEL), F32),
        grid_spec=grid_spec,
        compiler_params=_cparams("arbitrary"),
        name="moe_combine",
    )(dest, x, info, gate, norm_g, y)


def _moe_layer(mixed, g, shift, scale, gate, router_w, router_b, wg, wu, wd, norm_g, final_norm):
    t = mixed[4].shape[0]
    n_asg = t * TOP_K
    x, h, info = _moe_route(mixed, g, shift, scale, router_w, router_b)
    expert = info[:, :TOP_K].astype(jnp.int32).reshape(-1)
    onehot = (expert[:, None] == jnp.arange(N_EXPERTS, dtype=jnp.int32)[None, :]).astype(jnp.int32)
    csum = jnp.cumsum(onehot, axis=0)
    counts = csum[-1]
    padded = (counts + MOE_BLOCK - 1) // MOE_BLOCK * MOE_BLOCK
    pad_end = jnp.cumsum(padded)
    pad_start = pad_end - padded
    dest = jnp.sum(onehot * (csum - 1 + pad_start[None, :]), axis=1).astype(jnp.int32)
    n_blk = -(-n_asg // MOE_BLOCK) + N_EXPERTS
    blk_start = jnp.arange(n_blk, dtype=jnp.int32) * MOE_BLOCK
    blk_expert = jnp.minimum(jnp.sum(blk_start[:, None] >= pad_end[None, :], axis=1),
                             N_EXPERTS - 1).astype(jnp.int32)
    n_used = (pad_end[-1:] // MOE_BLOCK).astype(jnp.int32)
    fill = jnp.stack([pad_start + counts, pad_end]).astype(jnp.int32)
    y = _moe_experts(h, blk_expert, n_used, dest, fill, wg, wu, wd, n_blk)
    return _moe_combine(dest, x, info, gate, norm_g, y, final_norm)


def _final_norm_kernel(x_ref, g_ref, o_ref):
    x = x_ref[...]
    ms = jnp.mean(x * x, axis=-1, keepdims=True)
    o_ref[...] = x * lax.rsqrt(ms + EPS) * g_ref[...]


def _final_norm(x, g):
    t = x.shape[0]
    tm = min(t, 1024)
    return pl.pallas_call(
        _final_norm_kernel,
        out_shape=jax.ShapeDtypeStruct((t, D_MODEL), F32),
        grid=(t // tm,),
        in_specs=[pl.BlockSpec((tm, D_MODEL), lambda i: (i, 0)),
                  pl.BlockSpec((1, D_MODEL), lambda i: (0, 0))],
        out_specs=pl.BlockSpec((tm, D_MODEL), lambda i: (i, 0)),
        compiler_params=_cparams("parallel"),
        name="final_norm",
    )(x, g)


def kernel(x, c, ctx, c_ctx, ada_w, ada_b, norm1_g, norm2_g, w_in, w_out, hy_conv_w, hy_conv_b, hy_filt_w1, hy_filt_b1, hy_filt_w2, hy_filt_b2, hy_filt_w3, hy_filt_b3, hy_filt_freq, hy_bias, ret_log_rate, attn_q_g, attn_k_g, ffn_w_gate, ffn_w_up, ffn_w_down, moe_router_w, moe_router_b, moe_w_gate, moe_w_up, moe_w_down, final_norm_g):
    assert x.shape[0] == 1 and c.shape[0] == 1
    seq_len = x.shape[1]
    x_lat = x[0]
    x_ctx = ctx[0]
    cos_t, sin_t = _rope_tables(seq_len)
    cvec = jnp.concatenate([c, c_ctx[None, :], jnp.zeros((6, D_MODEL), F32)], axis=0)
    cvec = _silu(cvec)
    row = lambda v: v.reshape(1, D_MODEL)
    zero_state = jnp.zeros((2, RET_WIDTH, RET_HEAD_DIM), F32)
    zero_tab = jnp.zeros((x_ctx.shape[0], LANES), F32)

    for l in range(DEPTH):
        last = l == DEPTH - 1
        mods = _ada_modulation(cvec, ada_w, ada_b, l).reshape(8, 6, D_MODEL)
        mod = [row(mods[0, i]) for i in range(6)]
        mod_c = [row(mods[1, i]) for i in range(6)]
        filt = (hy_filt_w1[l], hy_filt_b1[l], hy_filt_w2[l], hy_filt_b2[l],
                hy_filt_w3[l], hy_filt_b3[l], hy_filt_freq[l])
        lg_f = -jnp.exp(ret_log_rate[l, 0].astype(F32))
        lg_b = -jnp.exp(ret_log_rate[l, 1].astype(F32))
        mask, zeta, xi, chunk_decay = _ret_tables(lg_f, lg_b)
        w_in_l = w_in[l].astype(BF16)
        w_out_l = w_out[l].astype(BF16)
        n1 = row(norm1_g[l])
        n2 = row(norm2_g[l])

        p_hy, p_ret, qt, k_aug, vt, kmax_sq = _in_projection(
            x_lat, n1, mod[0], mod[1], w_in_l, cos_t, sin_t, attn_q_g[l], attn_k_g[l], rope=True)
        pc_hy, pc_ret, qct, kc, vct, _ = _in_projection(
            x_ctx, n1, mod_c[0], mod_c[1], w_in_l, zero_tab, zero_tab, attn_q_g[l], attn_k_g[l],
            rope=False)

        sprev_c, s_ctx = _retention_states(pc_ret, zeta, chunk_decay, zero_state)

        y_hy = _hyena_mixer(p_hy, hy_conv_w[l], hy_conv_b[l], filt, hy_bias[l])
        sprev, _ = _retention_states(p_ret, zeta, chunk_decay, s_ctx)
        y_ret = _retention_outputs(p_ret, mask, xi, sprev)
        kmax = jnp.sqrt(jnp.max(kmax_sq[:, 0, ::ATT_HEAD_DIM], axis=0)) * KMAX_SLACK
        y_att = _attention(qt, kc, vct, kmax, k_aug, vt, tq=ATT_TQ, tk=ATT_TK, sub=ATT_SUB)

        if not last:
            yc_hy = _hyena_mixer(pc_hy, hy_conv_w[l], hy_conv_b[l], filt, hy_bias[l])
            yc_ret = _retention_outputs(pc_ret, mask, xi, sprev_c)
            yc_att = _attention(qct, kc, vct, tq=x_ctx.shape[0])
            mixed_c = (yc_hy, yc_ret, yc_att, w_out_l, x_ctx, mod_c[2])
        mixed = (y_hy, y_ret, y_att, w_out_l, x_lat, mod[2])

        i = l // 2
        if l % 2 == 0:
            wg, wu, wd = (ffn_w_gate[i].astype(BF16), ffn_w_up[i].astype(BF16),
                          ffn_w_down[i].astype(BF16))
            x_lat = _dense_ffn(mixed, n2, mod[3], mod[4], mod[5], wg, wu, wd)
            if not last:
                x_ctx = _dense_ffn(mixed_c, n2, mod_c[3], mod_c[4], mod_c[5], wg, wu, wd)
        else:
            wg, wu, wd = (moe_w_gate[i].astype(BF16), moe_w_up[i].astype(BF16),
                          moe_w_down[i].astype(BF16))
            fg = row(final_norm_g)
            x_lat = _moe_layer(mixed, n2, mod[3], mod[4], mod[5], moe_router_w[i], moe_router_b[i],
                               wg, wu, wd, fg, final_norm=last)
            if not last:
                x_ctx = _moe_layer(mixed_c, n2, mod_c[3], mod_c[4], mod_c[5], moe_router_w[i],
                                   moe_router_b[i], wg, wu, wd, fg, final_norm=False)
    if DEPTH % 2 == 1:
        x_lat = _final_norm(x_lat, row(final_norm_g))
    return x_lat[None]
```

```python
import functools
import math

import numpy as np
import jax
import jax.numpy as jnp
from jax import lax
from jax.experimental import pallas as pl
from jax.experimental.pallas import tpu as pltpu

F32 = jnp.float32
BF16 = jnp.bfloat16

D_MODEL = 1024
DEPTH = 2
GRID_W = 64
EPS = 1e-6

HY_WIDTH = 256
HY_COLS = 3 * HY_WIDTH
HY_POS_BANDS = 16
HY_DECAY_TARGET = 1e-2
HY_SHORT_DECAY_PCT = 0.3
HY_LONG_DECAY_PCT = 1.5

RET_HEAD_DIM = 64
RET_HEADS = 4
RET_WIDTH = RET_HEADS * RET_HEAD_DIM
RET_COLS = 4 * RET_WIDTH
RET_CHUNK = 128

ATT_HEAD_DIM = 64
ATT_HEADS = 8
ATT_KV_HEADS = 2
ATT_GROUP = ATT_HEADS // ATT_KV_HEADS
ATT_WIDTH = ATT_HEADS * ATT_HEAD_DIM
ATT_KV_WIDTH = ATT_KV_HEADS * ATT_HEAD_DIM
ATT_COLS = ATT_WIDTH + 2 * ATT_KV_WIDTH
ROPE_AXIS_DIM = ATT_HEAD_DIM // 2
ROPE_BASE = 10000.0

MIX_WIDTH = HY_WIDTH + RET_WIDTH + ATT_WIDTH
IN_COLS = HY_COLS + RET_COLS + ATT_COLS

FFN_HIDDEN = 2816
N_EXPERTS = 8
TOP_K = 2
MOE_BLOCK = 512

LOG2_E = 1.4426950408889634
EXP2_CAP = 60.0
KMAX_SLACK = 1.0 + 2.0 ** -7
ATT_TQ = 512
ATT_TK = 4096
ATT_SUB = 512
ATT_VROWS = 80
RET_STATE_CHUNKS = 16
RET_OUT_CHUNKS = 8
SWIGLU_CHUNK = 1024
DMA_UNROLL = 8

LANES = 128
SUBLANES = 8
VMEM_LIMIT = 56 * 1024 * 1024


def _cparams(*sem):
    return pltpu.CompilerParams(dimension_semantics=sem, vmem_limit_bytes=VMEM_LIMIT)


def _dot(a, b):
    return jnp.dot(a, b, preferred_element_type=F32)


def _split(a):
    hi = a.astype(BF16)
    lo = (a - hi.astype(F32)).astype(BF16)
    return hi, lo


def _dot3(a, b):
    ah, al = _split(a)
    bh, bl = _split(b)
    return _dot(ah, bh) + _dot(al, bh) + _dot(ah, bl)


def _silu(x):
    return x * (1.0 / (1.0 + jnp.exp(-x)))


def _norm_mod(x, g, shift, scale):
    ms = jnp.mean(x * x, axis=-1, keepdims=True)
    h = x * lax.rsqrt(ms + EPS) * g
    return h * (1.0 + scale) + shift


def _ada_kernel(c_ref, w_ref, b_ref, o_ref):
    o_ref[...] = _dot3(c_ref[...], w_ref[0]) + b_ref[0]


def _ada_modulation(cc, w, b, layer):
    n = w.shape[2]
    tn = 1536
    return pl.pallas_call(
        _ada_kernel,
        out_shape=jax.ShapeDtypeStruct((8, n), F32),
        grid=(n // tn,),
        in_specs=[pl.BlockSpec((8, D_MODEL), lambda j: (0, 0)),
                  pl.BlockSpec((1, D_MODEL, tn), lambda j: (layer, 0, j)),
                  pl.BlockSpec((1, 1, tn), lambda j: (layer, 0, j))],
        out_specs=pl.BlockSpec((8, tn), lambda j: (0, j)),
        compiler_params=_cparams("parallel"),
        name="ada_modulation",
    )(cc, w, b.reshape(b.shape[0], 1, n))


def _head_mean_matrix(width):
    idx = np.arange(width) // ATT_HEAD_DIM
    return jnp.asarray((idx[:, None] == idx[None, :]).astype(np.float32) / ATT_HEAD_DIM, BF16)


def _head_rms(x, bd, g):
    ms = _dot((x * x).astype(BF16), bd)
    return x * lax.rsqrt(ms + EPS) * g


def _rope(x, cos, sin):
    n = x.shape[1]
    lane = lax.broadcasted_iota(jnp.int32, x.shape, 1)
    swapped = jnp.where((lane % 32) < 16, pltpu.roll(x, n - 16, 1), pltpu.roll(x, 16, 1))
    return x * cos + swapped * sin


def _attention_operands(p, cos_ref, sin_ref, qg_ref, kg_ref, bdq_ref, bdk_ref,
                        qt_ref, k_ref, vt_ref, kmax_ref, rope):
    q = _head_rms(p[:, :ATT_WIDTH], bdq_ref[...], qg_ref[...])
    k = _head_rms(p[:, ATT_WIDTH:ATT_WIDTH + ATT_KV_WIDTH], bdk_ref[...], kg_ref[...])
    if rope:
        cos = cos_ref[...]
        sin = sin_ref[...]
        q = jnp.concatenate(
            [_rope(q[:, j * LANES:(j + 1) * LANES], cos, sin) for j in range(ATT_WIDTH // LANES)],
            axis=1)
        k = _rope(k, cos, sin)
    qt_ref[...] = (q * (ATT_HEAD_DIM ** -0.5 * LOG2_E)).T.astype(BF16)
    tm = k.shape[0]
    lane = lax.broadcasted_iota(jnp.int32, k.shape, 1)
    minus_one_col = jnp.where(lane == ATT_HEAD_DIM, -1.0, 0.0)
    kb = k.astype(BF16)
    k_ref[0] = jnp.where(lane < ATT_HEAD_DIM, k, minus_one_col).astype(BF16)
    k_ref[1] = jnp.where(lane < ATT_HEAD_DIM, pltpu.roll(k, ATT_HEAD_DIM, 1), minus_one_col).astype(BF16)
    norm_sq = _dot((kb.astype(F32) ** 2).astype(BF16), bdk_ref[...]) * ATT_HEAD_DIM
    kmax_ref[0] = jnp.broadcast_to(jnp.max(norm_sq, axis=0, keepdims=True), (SUBLANES, LANES))
    vt = p[:, ATT_WIDTH + ATT_KV_WIDTH:].T.astype(BF16)
    sub = lax.broadcasted_iota(jnp.int32, (ATT_VROWS - ATT_HEAD_DIM, tm), 0)
    ones_row = jnp.where(sub == 0, 1.0, 0.0).astype(BF16)
    for h in range(ATT_KV_HEADS):
        vt_ref[h * ATT_VROWS:h * ATT_VROWS + ATT_HEAD_DIM, :] = vt[h * ATT_HEAD_DIM:(h + 1) * ATT_HEAD_DIM]
        vt_ref[h * ATT_VROWS + ATT_HEAD_DIM:(h + 1) * ATT_VROWS, :] = ones_row


def _inproj_kernel(x_ref, g_ref, sh_ref, sc_ref, w_ref, cos_ref, sin_ref, qg_ref, kg_ref, bdq_ref,
                   bdk_ref, hy_ref, ret_ref, qt_ref, k_ref, vt_ref, kmax_ref, *, rope):
    h = _norm_mod(x_ref[...], g_ref[...], sh_ref[...], sc_ref[...])
    p = _dot(h.astype(BF16), w_ref[...])
    hy_ref[...] = p[:, :HY_COLS]
    ret_ref[...] = p[:, HY_COLS:HY_COLS + RET_COLS]
    _attention_operands(p[:, HY_COLS + RET_COLS:], cos_ref, sin_ref, qg_ref, kg_ref, bdq_ref, bdk_ref,
                        qt_ref, k_ref, vt_ref, kmax_ref, rope)


def _in_projection(x, g, shift, scale, w_bf16, cos, sin, q_g, k_g, rope):
    t = x.shape[0]
    tm = min(t, 512)
    qg = jnp.tile(q_g, ATT_HEADS).reshape(1, ATT_WIDTH)
    kg = jnp.tile(k_g, ATT_KV_HEADS).reshape(1, ATT_KV_WIDTH)
    vec = pl.BlockSpec((1, D_MODEL), lambda i: (0, 0))
    const = lambda shape: pl.BlockSpec(shape, lambda i: (0, 0))
    return pl.pallas_call(
        functools.partial(_inproj_kernel, rope=rope),
        out_shape=(jax.ShapeDtypeStruct((t, HY_COLS), F32),
                   jax.ShapeDtypeStruct((t, RET_COLS), F32),
                   jax.ShapeDtypeStruct((ATT_WIDTH, t), BF16),
                   jax.ShapeDtypeStruct((ATT_KV_HEADS, t, LANES), BF16),
                   jax.ShapeDtypeStruct((ATT_KV_HEADS * ATT_VROWS, t), BF16),
                   jax.ShapeDtypeStruct((t // tm, SUBLANES, LANES), F32)),
        grid=(t // tm,),
        in_specs=[pl.BlockSpec((tm, D_MODEL), lambda i: (i, 0)), vec, vec, vec,
                  const((D_MODEL, IN_COLS)),
                  pl.BlockSpec((tm, LANES), lambda i: (i, 0)),
                  pl.BlockSpec((tm, LANES), lambda i: (i, 0)),
                  const((1, ATT_WIDTH)), const((1, ATT_KV_WIDTH)),
                  const((ATT_WIDTH, ATT_WIDTH)), const((ATT_KV_WIDTH, ATT_KV_WIDTH))],
        out_specs=(pl.BlockSpec((tm, HY_COLS), lambda i: (i, 0)),
                   pl.BlockSpec((tm, RET_COLS), lambda i: (i, 0)),
                   pl.BlockSpec((ATT_WIDTH, tm), lambda i: (0, i)),
                   pl.BlockSpec((ATT_KV_HEADS, tm, LANES), lambda i: (0, i, 0)),
                   pl.BlockSpec((ATT_KV_HEADS * ATT_VROWS, tm), lambda i: (0, i)),
                   pl.BlockSpec((1, SUBLANES, LANES), lambda i: (i, 0, 0))),
        compiler_params=_cparams("parallel"),
        name="in_projection",
    )(x, g, shift, scale, w_bf16, cos, sin, qg, kg,
      _head_mean_matrix(ATT_WIDTH), _head_mean_matrix(ATT_KV_WIDTH))


def _rope_tables(n_tokens):
    rows = n_tokens // GRID_W
    inv_freq = ROPE_BASE ** (-jnp.arange(0, ROPE_AXIS_DIM, 2, dtype=F32) / ROPE_AXIS_DIM)
    ang_r = jnp.arange(rows, dtype=F32)[:, None] * inv_freq[None, :]
    ang_c = jnp.arange(GRID_W, dtype=F32)[:, None] * inv_freq[None, :]
    nf = inv_freq.shape[0]
    by_row = lambda tab: jnp.broadcast_to(tab[:, None, :], (rows, GRID_W, nf)).reshape(n_tokens, nf)
    by_col = lambda tab: jnp.broadcast_to(tab[None, :, :], (rows, GRID_W, nf)).reshape(n_tokens, nf)
    cr, sr = by_row(jnp.cos(ang_r)), by_row(jnp.sin(ang_r))
    cc, sc = by_col(jnp.cos(ang_c)), by_col(jnp.sin(ang_c))
    cos = jnp.concatenate([cr, cr, cc, cc], axis=1)
    sin = jnp.concatenate([-sr, sr, -sc, sc], axis=1)
    return jnp.tile(cos, (1, 2)), jnp.tile(sin, (1, 2))


def _attention_kernel(*refs, tq, tk, sub, n_lat):
    if n_lat:
        kmax_ref, qt_ref, kc_ref, vct_ref, k_ref, vt_ref, o_ref, qs_ref, acc_ref, m_ref = refs
    else:
        qt_ref, kc_ref, vct_ref, o_ref, acc_ref = refs
    cols = ATT_GROUP * tq
    qt = qt_ref[...]
    qst = jnp.concatenate(
        [qt[g * ATT_HEAD_DIM:(g + 1) * ATT_HEAD_DIM, :] for g in range(ATT_GROUP)], axis=1)
    pad = jnp.zeros((LANES - ATT_HEAD_DIM, cols), BF16)
    s = _dot(kc_ref[0], jnp.concatenate([qst, pad], axis=0))
    mt = jnp.max(s, axis=0, keepdims=True).astype(BF16)
    mtf = mt.astype(F32)
    acc_ref[...] = _dot(vct_ref[...], jnp.exp2(s - mtf).astype(BF16))

    if n_lat:
        row = lax.broadcasted_iota(jnp.int32, pad.shape, 0)
        qs_ref[...] = jnp.concatenate([qst, jnp.where(row == 0, mtf, 0.0).astype(BF16)], axis=0)
        qf = qst.astype(F32)
        qn = jnp.sqrt(jnp.sum(qf * qf, axis=0, keepdims=True))
        fast = jnp.max(qn * kmax_ref[pl.program_id(0)] - mtf) <= EXP2_CAP

        def tile(j):
            off = pl.multiple_of(j * tk, tk)
            return k_ref[0, pl.ds(off, tk), :], vt_ref[:, pl.ds(off, tk)]

        @pl.when(fast)
        def _():
            def body(j, carry):
                k, vt = tile(j)
                qs = qs_ref[...]
                part = None
                for c in range(tk // sub):
                    p = jnp.exp2(_dot(k[c * sub:(c + 1) * sub], qs)).astype(BF16)
                    pv = _dot(vt[:, c * sub:(c + 1) * sub], p)
                    part = pv if part is None else part + pv
                acc_ref[...] += part
                return carry

            lax.fori_loop(0, n_lat, body, 0)

        @pl.when(jnp.logical_not(fast))
        def _():
            m_ref[...] = jnp.zeros_like(m_ref)

            def body(j, carry):
                k, vt = tile(j)
                s = _dot(k, qs_ref[...])
                m_old = m_ref[...]
                m_new = jnp.maximum(m_old, jnp.max(s, axis=0, keepdims=True))
                p = jnp.exp2(s - m_new).astype(BF16)
                acc_ref[...] = jnp.exp2(m_old - m_new) * acc_ref[...] + _dot(vt, p)
                m_ref[...] = m_new
                return carry

            lax.fori_loop(0, n_lat, body, 0)

    acc = acc_ref[...]
    o = acc[:ATT_HEAD_DIM] / acc[ATT_HEAD_DIM:ATT_HEAD_DIM + 1]
    o_ref[...] = jnp.concatenate(
        [o[:, g * tq:(g + 1) * tq].T for g in range(ATT_GROUP)], axis=1).astype(o_ref.dtype)


def _attention(qt, kc, vct, kmax=None, k=None, vt=None, *, tq, tk=0, sub=0):
    t = qt.shape[1]
    lc = kc.shape[1]
    gw = ATT_GROUP * ATT_HEAD_DIM
    cols = ATT_GROUP * tq
    in_specs = [pl.BlockSpec((gw, tq), lambda h, i: (h, i)),
                pl.BlockSpec((1, lc, LANES), lambda h, i: (h, 0, 0)),
                pl.BlockSpec((ATT_VROWS, lc), lambda h, i: (h, 0))]
    args = [qt, kc, vct]
    scratch = [pltpu.VMEM((ATT_VROWS, cols), F32)]
    n_lat = 0
    if k is not None:
        lk = k.shape[1]
        n_lat = lk // tk
        in_specs = ([pl.BlockSpec(memory_space=pltpu.SMEM)] + in_specs
                    + [pl.BlockSpec((1, lk, LANES), lambda h, i: (h, 0, 0)),
                       pl.BlockSpec((ATT_VROWS, lk), lambda h, i: (h, 0))])
        args = [kmax] + args + [k, vt]
        scratch = [pltpu.VMEM((LANES, cols), BF16)] + scratch + [pltpu.VMEM((1, cols), F32)]
    return pl.pallas_call(
        functools.partial(_attention_kernel, tq=tq, tk=tk, sub=sub, n_lat=n_lat),
        out_shape=jax.ShapeDtypeStruct((t, ATT_WIDTH), BF16),
        grid=(ATT_KV_HEADS, t // tq),
        in_specs=in_specs,
        out_specs=pl.BlockSpec((tq, gw), lambda h, i: (i, h)),
        scratch_shapes=scratch,
        compiler_params=_cparams("parallel", "parallel"),
        name="attention_lat" if n_lat else "attention_ctx",
    )(*args)


def _ret_tables(lg_f, lg_b):
    pos = jnp.arange(RET_CHUNK, dtype=F32)
    rel = pos[:, None] - pos[None, :]
    mask = jnp.where(rel > 0, jnp.exp(lg_f[:, None, None] * jnp.maximum(rel, 0.0)),
                     jnp.where(rel < 0, jnp.exp(lg_b[:, None, None] * jnp.maximum(-rel, 0.0)), 2.0))
    zeta = jnp.stack([jnp.exp(lg_f[:, None] * (RET_CHUNK - 1 - pos)[None, :]),
                      jnp.exp(lg_b[:, None] * pos[None, :])])
    xi = jnp.stack([jnp.exp(lg_f[:, None] * (pos + 1)[None, :]),
                    jnp.exp(lg_b[:, None] * (RET_CHUNK - pos)[None, :])])
    chunk_decay = jnp.stack([jnp.exp(lg_f * RET_CHUNK), jnp.exp(lg_b * RET_CHUNK)])
    zeta = jnp.repeat(zeta, RET_HEAD_DIM, axis=1).transpose(0, 2, 1)
    xi = jnp.repeat(xi, RET_HEAD_DIM, axis=1).transpose(0, 2, 1)
    chunk_decay = jnp.repeat(chunk_decay, RET_HEAD_DIM, axis=1)[:, :, None]
    return mask, zeta, xi, chunk_decay


def _ret_state_kernel(k_ref, v_ref, zeta_ref, cd_ref, s0_ref, sprev_ref, sfin_ref, s_ref, *, cps):
    d = pl.program_id(0)

    @pl.when(pl.program_id(1) == 0)
    def _():
        s_ref[...] = s0_ref[0]

    zeta = zeta_ref[0] * (RET_HEAD_DIM ** -0.5)
    cd = cd_ref[0]
    us = []
    for c in range(cps):
        rows = slice(c * RET_CHUNK, (c + 1) * RET_CHUNK)
        kz = (k_ref[rows, :] * zeta).astype(BF16)
        v = v_ref[rows, :].astype(BF16)
        heads = []
        for h in range(RET_HEADS):
            sl = slice(h * RET_HEAD_DIM, (h + 1) * RET_HEAD_DIM)
            heads.append(lax.dot_general(kz[:, sl], v[:, sl], (((0,), (0,)), ((), ())),
                                         preferred_element_type=F32))
        us.append(jnp.concatenate(heads, axis=0))

    def scan(order):
        s = s_ref[...]
        for c in order:
            sprev_ref[0, c] = s
            s = cd * s + us[c]
        s_ref[...] = s
        sfin_ref[0] = s

    @pl.when(d == 0)
    def _():
        scan(range(cps))

    @pl.when(d == 1)
    def _():
        scan(reversed(range(cps)))


def _retention_states(p_ret, zeta, chunk_decay, s0):
    nc = p_ret.shape[0] // RET_CHUNK
    cps = min(nc, RET_STATE_CHUNKS)
    ng = nc // cps
    rows = cps * RET_CHUNK

    def group(d, i):
        return i + d * (ng - 1 - 2 * i)

    return pl.pallas_call(
        functools.partial(_ret_state_kernel, cps=cps),
        out_shape=(jax.ShapeDtypeStruct((2, nc, RET_WIDTH, RET_HEAD_DIM), F32),
                   jax.ShapeDtypeStruct((2, RET_WIDTH, RET_HEAD_DIM), F32)),
        grid=(2, ng),
        in_specs=[pl.BlockSpec((rows, RET_WIDTH), lambda d, i: (group(d, i), 1)),
                  pl.BlockSpec((rows, RET_WIDTH), lambda d, i: (group(d, i), 2)),
                  pl.BlockSpec((1, RET_CHUNK, RET_WIDTH), lambda d, i: (d, 0, 0)),
                  pl.BlockSpec((1, RET_WIDTH, 1), lambda d, i: (d, 0, 0)),
                  pl.BlockSpec((1, RET_WIDTH, RET_HEAD_DIM), lambda d, i: (d, 0, 0))],
        out_specs=(pl.BlockSpec((1, cps, RET_WIDTH, RET_HEAD_DIM), lambda d, i: (d, group(d, i), 0, 0)),
                   pl.BlockSpec((1, RET_WIDTH, RET_HEAD_DIM), lambda d, i: (d, 0, 0))),
        scratch_shapes=[pltpu.VMEM((RET_WIDTH, RET_HEAD_DIM), F32)],
        compiler_params=_cparams("arbitrary", "arbitrary"),
        name="retention_states",
    )(p_ret, p_ret, zeta, chunk_decay, s0)


def _ret_out_kernel(q_ref, k_ref, v_ref, g_ref, mask_ref, xi_ref, sf_ref, sb_ref, y_ref, *, cpo):
    for c in range(cpo):
        rows = slice(c * RET_CHUNK, (c + 1) * RET_CHUNK)
        q = q_ref[rows, :]
        k = (k_ref[rows, :] * (RET_HEAD_DIM ** -0.5)).astype(BF16)
        v = v_ref[rows, :].astype(BF16)
        qb = q.astype(BF16)
        qf = (q * xi_ref[0]).astype(BF16)
        qr = (q * xi_ref[1]).astype(BF16)
        sf = sf_ref[0, c].astype(BF16)
        sb = sb_ref[0, c].astype(BF16)
        ys = []
        for h in range(RET_HEADS):
            sl = slice(h * RET_HEAD_DIM, (h + 1) * RET_HEAD_DIM)
            scores = lax.dot_general(qb[:, sl], k[:, sl], (((1,), (1,)), ((), ())),
                                     preferred_element_type=F32)
            y = _dot((scores * mask_ref[h]).astype(BF16), v[:, sl])
            y = y + _dot(qf[:, sl], sf[sl, :]) + _dot(qr[:, sl], sb[sl, :])
            mu = jnp.mean(y, axis=-1, keepdims=True)
            yc = y - mu
            var = jnp.mean(yc * yc, axis=-1, keepdims=True)
            ys.append(yc * lax.rsqrt(var + EPS))
        y_ref[rows, :] = (_silu(g_ref[rows, :]) * jnp.concatenate(ys, axis=1)).astype(y_ref.dtype)


def _retention_outputs(p_ret, mask, xi, sprev):
    nc = p_ret.shape[0] // RET_CHUNK
    cpo = min(nc, RET_OUT_CHUNKS)
    rows = cpo * RET_CHUNK
    col = lambda c: pl.BlockSpec((rows, RET_WIDTH), lambda i: (i, c))
    return pl.pallas_call(
        functools.partial(_ret_out_kernel, cpo=cpo),
        out_shape=jax.ShapeDtypeStruct((p_ret.shape[0], RET_WIDTH), BF16),
        grid=(nc // cpo,),
        in_specs=[col(0), col(1), col(2), col(3),
                  pl.BlockSpec((RET_HEADS, RET_CHUNK, RET_CHUNK), lambda i: (0, 0, 0)),
                  pl.BlockSpec((2, RET_CHUNK, RET_WIDTH), lambda i: (0, 0, 0)),
                  pl.BlockSpec((1, cpo, RET_WIDTH, RET_HEAD_DIM), lambda i: (0, i, 0, 0)),
                  pl.BlockSpec((1, cpo, RET_WIDTH, RET_HEAD_DIM), lambda i: (1, i, 0, 0))],
        out_specs=pl.BlockSpec((rows, RET_WIDTH), lambda i: (i, 0)),
        compiler_params=_cparams("parallel"),
        name="retention_outputs",
    )(p_ret, p_ret, p_ret, p_ret, mask, xi, sprev, sprev)


def _hy_pre_kernel(p_ref, prev_ref, next_ref, w_ref, b_ref, x0_ref, z_ref):
    i = pl.program_id(0)
    x = p_ref[...]
    tm = x.shape[0]
    row = lax.broadcasted_iota(jnp.int32, x.shape, 0)
    prev_row = jnp.where(i == 0, 0.0, prev_ref[7:8, :])
    next_row = jnp.where(i == pl.num_programs(0) - 1, 0.0, next_ref[0:1, :])
    up = jnp.where(row == 0, prev_row, pltpu.roll(x, 1, 0))
    dn = jnp.where(row == tm - 1, next_row, pltpu.roll(x, tm - 1, 0))
    w = w_ref[...]
    u = up * w[0:1] + x * w[1:2] + dn * w[2:3] + b_ref[...]
    x0_ref[...] = u[:, :HY_WIDTH]
    z_ref[...] = u[:, HY_WIDTH:2 * HY_WIDTH] * u[:, 2 * HY_WIDTH:]


def _hyena_pre(p_hy, conv_w, conv_b):
    t = p_hy.shape[0]
    tm = min(t, 512)
    nb8 = tm // 8
    last8 = t // 8 - 1
    return pl.pallas_call(
        _hy_pre_kernel,
        out_shape=(jax.ShapeDtypeStruct((t, HY_WIDTH), F32),
                   jax.ShapeDtypeStruct((t, HY_WIDTH), F32)),
        grid=(t // tm,),
        in_specs=[pl.BlockSpec((tm, HY_COLS), lambda i: (i, 0)),
                  pl.BlockSpec((8, HY_COLS), lambda i: (jnp.maximum(i * nb8 - 1, 0), 0)),
                  pl.BlockSpec((8, HY_COLS), lambda i: (jnp.minimum((i + 1) * nb8, last8), 0)),
                  pl.BlockSpec((3, HY_COLS), lambda i: (0, 0)),
                  pl.BlockSpec((1, HY_COLS), lambda i: (0, 0))],
        out_specs=(pl.BlockSpec((tm, HY_WIDTH), lambda i: (i, 0)),
                   pl.BlockSpec((tm, HY_WIDTH), lambda i: (i, 0))),
        compiler_params=_cparams("parallel"),
        name="hyena_pre",
    )(p_hy, p_hy, p_hy, conv_w, conv_b.reshape(1, HY_COLS))


def _hy_filter_kernel(feat_ref, w1_ref, b1_ref, w2_ref, b2_ref, w3_ref, b3_ref, freq_ref,
                      delta_ref, taps_ref, asum_ref, *, seq_len):
    i = pl.program_id(0)
    feat = feat_ref[...]
    tm = feat.shape[0]
    half = tm // 2
    freq = freq_ref[...]
    packed = jnp.concatenate([feat[:half], feat[half:]], axis=1)
    h = jnp.sin(freq * (_dot3(packed, w1_ref[...]) + b1_ref[...]))
    h = jnp.sin(freq * (_dot3(h, w2_ref[...]) + b2_ref[...]))
    h = _dot3(h, w3_ref[0])
    h = jnp.concatenate([h[:, :HY_WIDTH], h[:, HY_WIDTH:]], axis=0) + b3_ref[0]
    j = i * tm + lax.broadcasted_iota(jnp.int32, (tm, HY_WIDTH), 0)
    taps = jnp.where(j == seq_len, 0.0, h * jnp.exp(-feat[:, 0:1] * delta_ref[...]))
    taps_ref[...] = taps

    @pl.when(i == 0)
    def _():
        asum_ref[...] = jnp.zeros_like(asum_ref)

    asum_ref[...] += jnp.sum(jnp.abs(taps).reshape(tm // SUBLANES, SUBLANES, HY_WIDTH), axis=0)


def _position_features(seq_len):
    lo = 256
    hi = seq_len // lo + 1
    bands = np.linspace(1e-4, HY_POS_BANDS - 1, HY_POS_BANDS).astype(np.float32).astype(np.float64)
    ang_hi = (2.0 * np.pi / seq_len) * lo * np.arange(hi)[:, None] * bands[None, :]
    ang_lo = (2.0 * np.pi / seq_len) * np.arange(lo)[:, None] * bands[None, :]
    tab = lambda a: jnp.asarray(a, F32)
    ch, sh = tab(np.cos(ang_hi))[:, None, :], tab(np.sin(ang_hi))[:, None, :]
    cl, sl = tab(np.cos(ang_lo))[None, :, :], tab(np.sin(ang_lo))[None, :, :]
    npos = seq_len + 1
    cos = (ch * cl - sh * sl).reshape(hi * lo, HY_POS_BANDS)[:npos]
    sin = (sh * cl + ch * sl).reshape(hi * lo, HY_POS_BANDS)[:npos]
    c_end, s_end = tab(np.cos(2.0 * np.pi * bands))[None, :], tab(np.sin(2.0 * np.pi * bands))[None, :]
    cos_b = c_end * cos[1:seq_len] + s_end * sin[1:seq_len]
    sin_b = s_end * cos[1:seq_len] - c_end * sin[1:seq_len]
    lag = jnp.concatenate([jnp.arange(npos, dtype=F32), jnp.arange(seq_len - 1, 0, -1, dtype=F32)])
    t = lag / max(seq_len - 1, 1)
    cos = jnp.concatenate([cos, cos_b], axis=0)
    sin = jnp.concatenate([sin, sin_b], axis=0)
    return jnp.concatenate([t[:, None], cos, -sin,
                            jnp.zeros((2 * seq_len, LANES - 1 - 2 * HY_POS_BANDS), F32)], axis=-1)


def _hyena_taps(seq_len, w1, b1, w2, b2, w3, b3, freq):
    n = 2 * seq_len
    feat = _position_features(seq_len)
    w1p = jnp.concatenate([w1, jnp.zeros((LANES - w1.shape[0], w1.shape[1]), F32)], axis=0)
    pair = lambda w: jnp.concatenate(
        [jnp.concatenate([w, jnp.zeros_like(w)], axis=1),
         jnp.concatenate([jnp.zeros_like(w), w], axis=1)], axis=0)
    twice = lambda v: jnp.tile(v.reshape(1, -1), (1, 2))
    w3_dirs = jnp.stack([pair(w3[:, :HY_WIDTH]), pair(w3[:, HY_WIDTH:])])
    b3_dirs = b3.reshape(2, 1, HY_WIDTH)
    deltas = jnp.abs(jnp.linspace(math.log(HY_DECAY_TARGET) / HY_LONG_DECAY_PCT,
                                  math.log(HY_DECAY_TARGET) / HY_SHORT_DECAY_PCT,
                                  HY_WIDTH, dtype=F32)).reshape(1, HY_WIDTH)
    tm = min(seq_len, 1024)
    per_dir = seq_len // tm
    hid2 = 2 * w1.shape[1]
    const = lambda shape: pl.BlockSpec(shape, lambda i: (0, 0))
    by_dir = lambda shape: pl.BlockSpec(shape, lambda i: (i // per_dir, 0, 0))
    return pl.pallas_call(
        functools.partial(_hy_filter_kernel, seq_len=seq_len),
        out_shape=(jax.ShapeDtypeStruct((n, HY_WIDTH), F32),
                   jax.ShapeDtypeStruct((8, HY_WIDTH), F32)),
        grid=(n // tm,),
        in_specs=[pl.BlockSpec((tm, LANES), lambda i: (i, 0)),
                  const((2 * LANES, hid2)), const((1, hid2)), const((hid2, hid2)), const((1, hid2)),
                  by_dir((1, hid2, 2 * HY_WIDTH)), by_dir((1, 1, HY_WIDTH)), const((1, hid2)),
                  const((1, HY_WIDTH))],
        out_specs=(pl.BlockSpec((tm, HY_WIDTH), lambda i: (i, 0)), const((8, HY_WIDTH))),
        compiler_params=_cparams("arbitrary"),
        name="hyena_taps",
    )(feat, pair(w1p), twice(b1), pair(w2), twice(b2), w3_dirs, b3_dirs, twice(freq), deltas)


FFT_N2 = 256
FFT_K1_PER_STEP = 8
FFT_N2_PER_STEP = 16


def _dft_tables(n_total, n1_in):
    n1 = n_total // FFT_N2
    k1 = np.arange(n1)
    a = 2.0 * np.pi * np.outer(k1, np.arange(n1_in)) / n1
    lvl1 = np.concatenate([np.cos(a), -np.sin(a)], axis=0)
    n2 = np.arange(FFT_N2)
    th = 2.0 * np.pi * np.outer(n2, n2) / FFT_N2
    fc, fs = np.cos(th), np.sin(th)
    g = np.block([[fc, fs], [-fs, fc]])
    ph = 2.0 * np.pi * np.outer(k1, n2) / n_total
    tw = np.stack([np.cos(ph), np.sin(ph)])[..., None]
    return (jnp.asarray(lvl1, F32), jnp.asarray(g, F32), jnp.asarray(g.T, F32), jnp.asarray(tw, F32))


def _inverse_lvl1_table(n_total, n1_out):
    n1 = n_total // FFT_N2
    a = 2.0 * np.pi * np.outer(np.arange(n1_out), np.arange(n1)) / n1
    return jnp.asarray(np.concatenate([np.cos(a), -np.sin(a)], axis=1), F32)


def _lvl2_kernel(b_ref, tw_ref, gh_ref, gl_ref, *rest, inverse):
    for j in range(b_ref.shape[1]):
        c = tw_ref[0, j]
        s = tw_ref[1, j]
        br = b_ref[0, j]
        bi = b_ref[1, j]
        d = jnp.concatenate([c * br + s * bi, c * bi - s * br], axis=0)
        dh, dl = _split(d)
        x = _dot(gh_ref[...], dh) + _dot(gh_ref[...], dl) + _dot(gl_ref[...], dh)
        if not inverse:
            (o_ref,) = rest
            o_ref[0, j] = x[:FFT_N2]
            o_ref[1, j] = x[FFT_N2:]
            continue
        h_ref, gth_ref, gtl_ref, o_ref = rest
        xr, xi = x[:FFT_N2], x[FFT_N2:]
        hr, hi = h_ref[0, j], h_ref[1, j]
        y = jnp.concatenate([xr * hr - xi * hi, xr * hi + xi * hr], axis=0)
        yh, yl = _split(y)
        cc = _dot(gth_ref[...], yh) + _dot(gth_ref[...], yl) + _dot(gtl_ref[...], yh)
        cr, ci = cc[:FFT_N2], cc[FFT_N2:]
        o_ref[0, j] = c * cr - s * ci
        o_ref[1, j] = c * ci + s * cr


def _level2(b, tw, g, h=None, gt=None):
    _, n1, n2, ch = b.shape
    kb = FFT_K1_PER_STEP
    blk = pl.BlockSpec((2, kb, n2, ch), lambda i: (0, i, 0, 0))
    const = pl.BlockSpec((2 * n2, 2 * n2), lambda i: (0, 0))
    gh, gl = _split(g)
    in_specs = [blk, pl.BlockSpec((2, kb, n2, 1), lambda i: (0, i, 0, 0)), const, const]
    args = [b, tw, gh, gl]
    if h is not None:
        gth, gtl = _split(gt)
        in_specs += [blk, const, const]
        args += [h, gth, gtl]
    return pl.pallas_call(
        functools.partial(_lvl2_kernel, inverse=h is not None),
        out_shape=jax.ShapeDtypeStruct(b.shape, F32),
        grid=(n1 // kb,),
        in_specs=in_specs,
        out_specs=blk,
        compiler_params=_cparams("parallel"),
        name="hyena_fft_level2" + ("_conv" if h is not None else ""),
    )(*args)


def _lvl1_kernel(a_ref, x_ref, o_ref, *, inverse):
    tn2 = x_ref.shape[-2]
    ch = x_ref.shape[-1]
    if inverse:
        x = jnp.concatenate([x_ref[:, :, j, :].reshape(-1, ch) for j in range(tn2)], axis=1)
    else:
        x = jnp.concatenate([x_ref[:, j, :] for j in range(tn2)], axis=1)
    out = _dot3(a_ref[...], x)
    for j in range(tn2):
        slab = out[:, j * ch:(j + 1) * ch]
        if inverse:
            o_ref[:, j, :] = slab
        else:
            o_ref[:, :, j, :] = slab.reshape(2, slab.shape[0] // 2, ch)


def _level1(a, x, name):
    n1_in, n2, ch = x.shape
    n1 = a.shape[0] // 2
    tn2 = FFT_N2_PER_STEP
    return pl.pallas_call(
        functools.partial(_lvl1_kernel, inverse=False),
        out_shape=jax.ShapeDtypeStruct((2, n1, n2, ch), F32),
        grid=(n2 // tn2,),
        in_specs=[pl.BlockSpec(a.shape, lambda i: (0, 0)),
                  pl.BlockSpec((n1_in, tn2, ch), lambda i: (0, i, 0))],
        out_specs=pl.BlockSpec((2, n1, tn2, ch), lambda i: (0, 0, i, 0)),
        compiler_params=_cparams("parallel"),
        name=name,
    )(a, x)


def _inverse_level1(a, x):
    _, n1, n2, ch = x.shape
    n1_out = a.shape[0]
    tn2 = FFT_N2_PER_STEP
    return pl.pallas_call(
        functools.partial(_lvl1_kernel, inverse=True),
        out_shape=jax.ShapeDtypeStruct((n1_out, n2, ch), F32),
        grid=(n2 // tn2,),
        in_specs=[pl.BlockSpec(a.shape, lambda i: (0, 0)),
                  pl.BlockSpec((2, n1, tn2, ch), lambda i: (0, 0, i, 0))],
        out_specs=pl.BlockSpec((n1_out, tn2, ch), lambda i: (0, i, 0)),
        compiler_params=_cparams("parallel"),
        name="hyena_fft_inverse_level1",
    )(a, x)


def _hyena_long_conv(z, taps):
    seq_len, ch = z.shape
    n = 2 * seq_len
    n1 = n // FFT_N2
    lvl1_z, g, gt, tw = _dft_tables(n, n1 // 2)
    lvl1_t = _dft_tables(n, n1)[0]
    inv1 = _inverse_lvl1_table(n, n1 // 2)
    hb = _level1(lvl1_t, taps.reshape(n1, FFT_N2, ch), "hyena_fft_level1_taps")
    hspec = _level2(hb, tw, g)
    zb = _level1(lvl1_z, z.reshape(n1 // 2, FFT_N2, ch), "hyena_fft_level1_z")
    cspec = _level2(zb, tw, g, hspec, gt)
    return _inverse_level1(inv1, cspec).reshape(seq_len, ch)


def _small_conv_kernel(z_ref, taps_ref, fc_ref, fs_ref, o_ref):
    seq_len = z_ref.shape[0]
    fc = fc_ref[...]
    fs = fs_ref[...]
    z = z_ref[...]
    taps = taps_ref[...]
    zr = _dot3(fc[:, :seq_len], z)
    zi = -_dot3(fs[:, :seq_len], z)
    hr = _dot3(fc, taps)
    hi = -_dot3(fs, taps)
    yr = zr * hr - zi * hi
    yi = zr * hi + zi * hr
    o_ref[...] = _dot3(fc[:seq_len, :], yr) - _dot3(fs[:seq_len, :], yi)


def _hyena_small_conv(z, taps):
    seq_len, ch = z.shape
    n = 2 * seq_len
    th = 2.0 * np.pi * np.outer(np.arange(n), np.arange(n)) / n
    full = lambda shape: pl.BlockSpec(shape, lambda: (0, 0))
    return pl.pallas_call(
        _small_conv_kernel,
        out_shape=jax.ShapeDtypeStruct((seq_len, ch), F32),
        in_specs=[full((seq_len, ch)), full((n, ch)), full((n, n)), full((n, n))],
        out_specs=full((seq_len, ch)),
        compiler_params=pltpu.CompilerParams(vmem_limit_bytes=VMEM_LIMIT),
        name="hyena_small_conv",
    )(z, taps, jnp.asarray(np.cos(th), F32), jnp.asarray(np.sin(th), F32))


def _hy_post_kernel(x0_ref, z_ref, conv_ref, asum_ref, bias_ref, y_ref, *, n_total):
    norm = jnp.sum(asum_ref[...], axis=0, keepdims=True) + EPS
    conv = conv_ref[...] * (1.0 / (n_total * norm))
    y_ref[...] = (x0_ref[...] * (conv + bias_ref[...] * z_ref[...])).astype(y_ref.dtype)


def _hyena_post(x0, z, conv, asum, bias):
    t = x0.shape[0]
    tm = min(t, 1024)
    row = pl.BlockSpec((tm, HY_WIDTH), lambda i: (i, 0))
    return pl.pallas_call(
        functools.partial(_hy_post_kernel, n_total=2 * t),
        out_shape=jax.ShapeDtypeStruct((t, HY_WIDTH), BF16),
        grid=(t // tm,),
        in_specs=[row, row, row, pl.BlockSpec((8, HY_WIDTH), lambda i: (0, 0)),
                  pl.BlockSpec((1, HY_WIDTH), lambda i: (0, 0))],
        out_specs=row,
        compiler_params=_cparams("parallel"),
        name="hyena_post",
    )(x0, z, conv, asum, bias.reshape(1, HY_WIDTH))


def _hyena_mixer(p_hy, conv_w, conv_b, filt, bias):
    seq_len = p_hy.shape[0]
    x0, z = _hyena_pre(p_hy, conv_w, conv_b)
    taps, asum = _hyena_taps(seq_len, *filt)
    if 2 * seq_len // FFT_N2 >= 16:
        conv = _hyena_long_conv(z, taps)
    else:
        conv = _hyena_small_conv(z, taps)
    return _hyena_post(x0, z, conv, asum, bias)


def _mixed_residual(hy_ref, ret_ref, att_ref, wo_ref, x_ref, mgate_ref):
    y = (_dot(hy_ref[...], wo_ref[:HY_WIDTH, :])
         + _dot(ret_ref[...], wo_ref[HY_WIDTH:HY_WIDTH + RET_WIDTH, :])
         + _dot(att_ref[...], wo_ref[HY_WIDTH + RET_WIDTH:, :]))
    return x_ref[...] + mgate_ref[...] * y


def _mixer_specs(tm, index):
    row = lambda w: pl.BlockSpec((tm, w), lambda i, *_: (i, 0))
    return [row(HY_WIDTH), row(RET_WIDTH), row(ATT_WIDTH),
            pl.BlockSpec((MIX_WIDTH, D_MODEL), index), row(D_MODEL),
            pl.BlockSpec((1, D_MODEL), index)]


def _swiglu(x, wg_ref, wu_ref, wd_ref):
    hidden = wg_ref.shape[1]
    y = None
    for lo in range(0, hidden, SWIGLU_CHUNK):
        hi = min(lo + SWIGLU_CHUNK, hidden)
        a = _silu(_dot(x, wg_ref[:, lo:hi])) * _dot(x, wu_ref[:, lo:hi])
        part = _dot(a.astype(BF16), wd_ref[lo:hi, :])
        y = part if y is None else y + part
    return y


def _ffn_kernel(hy_ref, ret_ref, att_ref, wo_ref, x_ref, mgate_ref, g_ref, sh_ref, sc_ref, gate_ref,
                wg_ref, wu_ref, wd_ref, o_ref):
    x = _mixed_residual(hy_ref, ret_ref, att_ref, wo_ref, x_ref, mgate_ref)
    h = _norm_mod(x, g_ref[...], sh_ref[...], sc_ref[...]).astype(BF16)
    o_ref[...] = x + gate_ref[...] * _swiglu(h, wg_ref, wu_ref, wd_ref)


def _dense_ffn(mixed, g, shift, scale, gate, wg, wu, wd):
    t = mixed[4].shape[0]
    tm = min(t, 512)
    index = lambda i: (0, 0)
    vec = pl.BlockSpec((1, D_MODEL), index)
    resident = lambda shape: pl.BlockSpec(shape, index, pipeline_mode=pl.Buffered(1))
    return pl.pallas_call(
        _ffn_kernel,
        out_shape=jax.ShapeDtypeStruct((t, D_MODEL), F32),
        grid=(t // tm,),
        in_specs=_mixer_specs(tm, index) + [vec, vec, vec, vec,
                  resident((D_MODEL, FFN_HIDDEN)), resident((D_MODEL, FFN_HIDDEN)),
                  resident((FFN_HIDDEN, D_MODEL))],
        out_specs=pl.BlockSpec((tm, D_MODEL), lambda i: (i, 0)),
        compiler_params=_cparams("parallel"),
        name="dense_ffn",
    )(*mixed, g, shift, scale, gate, wg, wu, wd)


def _moe_route_kernel(hy_ref, ret_ref, att_ref, wo_ref, x_ref, mgate_ref, g_ref, sh_ref, sc_ref,
                      rw_ref, rb_ref, xo_ref, h_ref, info_ref):
    x = _mixed_residual(hy_ref, ret_ref, att_ref, wo_ref, x_ref, mgate_ref)
    xo_ref[...] = x
    h = _norm_mod(x, g_ref[...], sh_ref[...], sc_ref[...])
    _to_row_tiles(h_ref, h)
    logits = _dot3(h, rw_ref[...]) + rb_ref[...]
    lane = lax.broadcasted_iota(jnp.int32, logits.shape, 1)
    v1 = jnp.max(logits, axis=-1, keepdims=True)
    i1 = jnp.min(jnp.where(logits == v1, lane, LANES), axis=-1, keepdims=True)
    rest = jnp.where(lane == i1, -jnp.inf, logits)
    v2 = jnp.max(rest, axis=-1, keepdims=True)
    i2 = jnp.min(jnp.where(rest == v2, lane, LANES), axis=-1, keepdims=True)
    e = jnp.exp(v2 - v1)
    g1 = 1.0 / (1.0 + e)
    g2 = e * g1
    info_ref[...] = jnp.where(lane == 0, i1.astype(F32),
                              jnp.where(lane == 1, i2.astype(F32),
                                        jnp.where(lane == 2, g1, jnp.where(lane == 3, g2, 0.0))))


def _moe_route(mixed, g, shift, scale, router_w, router_b):
    t = mixed[4].shape[0]
    tm = min(t, 512)
    rw = jnp.concatenate([router_w, jnp.zeros((D_MODEL, LANES - N_EXPERTS), F32)], axis=1)
    rb = jnp.concatenate([router_b, jnp.full((LANES - N_EXPERTS,), -jnp.inf, F32)]).reshape(1, LANES)
    vec = pl.BlockSpec((1, D_MODEL), lambda i: (0, 0))
    return pl.pallas_call(
        _moe_route_kernel,
        out_shape=(jax.ShapeDtypeStruct((t, D_MODEL), F32),
                   jax.ShapeDtypeStruct((t * SUBLANES, LANES), F32),
                   jax.ShapeDtypeStruct((t, LANES), F32)),
        grid=(t // tm,),
        in_specs=_mixer_specs(tm, lambda i: (0, 0)) + [vec, vec, vec,
                  pl.BlockSpec((D_MODEL, LANES), lambda i: (0, 0)),
                  pl.BlockSpec((1, LANES), lambda i: (0, 0))],
        out_specs=(pl.BlockSpec((tm, D_MODEL), lambda i: (i, 0)),
                   pl.BlockSpec((tm * SUBLANES, LANES), lambda i: (i, 0)),
                   pl.BlockSpec((tm, LANES), lambda i: (i, 0))),
        compiler_params=_cparams("parallel"),
        name="moe_route",
    )(*mixed, g, shift, scale, rw, rb)


def _to_row_tiles(ref, x):
    rows = x.shape[0]
    for k in range(D_MODEL // LANES):
        ref[pl.ds(k, rows, stride=SUBLANES), :] = x[:, k * LANES:(k + 1) * LANES]


def _from_row_tiles(ref, rows):
    return jnp.concatenate(
        [ref[pl.ds(k, rows, stride=SUBLANES), :] for k in range(D_MODEL // LANES)], axis=1)


def _row_copy(src_hbm, dst_vmem, sem, src_row, dst_row):
    src = pl.multiple_of(src_row * SUBLANES, SUBLANES)
    dst = pl.multiple_of(dst_row * SUBLANES, SUBLANES)
    return pltpu.make_async_copy(src_hbm.at[pl.ds(src, SUBLANES)], dst_vmem.at[pl.ds(dst, SUBLANES)], sem)


def _moe_expert_kernel(blk_e_ref, nused_ref, dest_ref, fill_ref, h_hbm, wg_ref, wu_ref, wd_ref,
                       y_ref, xbuf_ref, src_ref, sem):
    b = pl.program_id(0)
    n_used = nused_ref[0]

    @pl.when(b == 0)
    def _():
        def invert(t, carry):
            for j in range(TOP_K):
                src_ref[dest_ref[TOP_K * t + j]] = t
            return carry

        lax.fori_loop(0, dest_ref.shape[0] // TOP_K, invert, 0, unroll=DMA_UNROLL)
        for e in range(N_EXPERTS):
            def pad(d, carry):
                src_ref[d] = 0
                return carry

            lax.fori_loop(fill_ref[0, e], fill_ref[1, e], pad, 0)

    def issue_rows(blk, slot):
        def body(r, carry):
            _row_copy(h_hbm, xbuf_ref.at[slot], sem.at[slot], src_ref[blk * MOE_BLOCK + r], r).start()
            return carry

        lax.fori_loop(0, MOE_BLOCK, body, 0, unroll=DMA_UNROLL)

    def wait_rows(slot):
        def body(r, carry):
            _row_copy(h_hbm, xbuf_ref.at[slot], sem.at[slot], 0, r).wait()
            return carry

        lax.fori_loop(0, MOE_BLOCK, body, 0, unroll=DMA_UNROLL)

    @pl.when(b < n_used)
    def _():
        slot = b % 2

        @pl.when(b == 0)
        def _():
            issue_rows(0, 0)

        wait_rows(slot)

        @pl.when(b + 1 < n_used)
        def _():
            issue_rows(b + 1, 1 - slot)

        x = _from_row_tiles(xbuf_ref.at[slot], MOE_BLOCK).astype(BF16)
        _to_row_tiles(y_ref, _swiglu(x, wg_ref.at[0], wu_ref.at[0], wd_ref.at[0]))

    @pl.when(b >= n_used)
    def _():
        y_ref[...] = jnp.zeros_like(y_ref)


def _moe_experts(h, blk_expert, n_used, dest, fill, wg, wu, wd, n_blk):
    def w_index(b, be, nu, de, fi):
        return be[jnp.minimum(b, nu[0] - 1)], 0, 0

    grid_spec = pltpu.PrefetchScalarGridSpec(
        num_scalar_prefetch=4,
        grid=(n_blk,),
        in_specs=[pl.BlockSpec(memory_space=pl.ANY),
                  pl.BlockSpec((1, D_MODEL, FFN_HIDDEN), w_index),
                  pl.BlockSpec((1, D_MODEL, FFN_HIDDEN), w_index),
                  pl.BlockSpec((1, FFN_HIDDEN, D_MODEL), w_index)],
        out_specs=pl.BlockSpec((MOE_BLOCK * SUBLANES, LANES), lambda b, be, nu, de, fi: (b, 0)),
        scratch_shapes=[pltpu.VMEM((2, MOE_BLOCK * SUBLANES, LANES), F32),
                        pltpu.SMEM((n_blk * MOE_BLOCK,), jnp.int32),
                        pltpu.SemaphoreType.DMA((2,))],
    )
    return pl.pallas_call(
        _moe_expert_kernel,
        out_shape=jax.ShapeDtypeStruct((n_blk * MOE_BLOCK * SUBLANES, LANES), F32),
        grid_spec=grid_spec,
        compiler_params=_cparams("arbitrary"),
        name="moe_experts",
    )(blk_expert, n_used, dest, fill, h, wg, wu, wd)


def _moe_combine_kernel(dest_ref, x_ref, info_ref, gate_ref, ng_ref, y_hbm, o_ref,
                        y1_ref, y2_ref, sem, *, final_norm):
    i = pl.program_id(0)
    tm = x_ref.shape[0]
    slot = i % 2

    def issue_rows(blk, slot):
        def body(r, carry):
            a = 2 * (blk * tm + r)
            _row_copy(y_hbm, y1_ref.at[slot], sem.at[slot], dest_ref[a], r).start()
            _row_copy(y_hbm, y2_ref.at[slot], sem.at[slot], dest_ref[a + 1], r).start()
            return carry

        lax.fori_loop(0, tm, body, 0, unroll=DMA_UNROLL)

    def wait_rows(slot):
        def body(r, carry):
            _row_copy(y_hbm, y1_ref.at[slot], sem.at[slot], 0, r).wait()
            _row_copy(y_hbm, y2_ref.at[slot], sem.at[slot], 0, r).wait()
            return carry

        lax.fori_loop(0, tm, body, 0, unroll=DMA_UNROLL)

    @pl.when(i == 0)
    def _():
        issue_rows(0, 0)

    wait_rows(slot)

    @pl.when(i + 1 < pl.num_programs(0))
    def _():
        issue_rows(i + 1, 1 - slot)

    info = info_ref[...]
    y = (_from_row_tiles(y1_ref.at[slot], tm) * info[:, 2:3]
         + _from_row_tiles(y2_ref.at[slot], tm) * info[:, 3:4])
    x = x_ref[...] + gate_ref[...] * y
    if final_norm:
        ms = jnp.mean(x * x, axis=-1, keepdims=True)
        x = x * lax.rsqrt(ms + EPS) * ng_ref[...]
    o_ref[...] = x


def _moe_combine(dest, x, info, gate, norm_g, y, final_norm):
    t = x.shape[0]
    tm = min(t, 256)
    vec = pl.BlockSpec((1, D_MODEL), lambda i, d: (0, 0))
    grid_spec = pltpu.PrefetchScalarGridSpec(
        num_scalar_prefetch=1,
        grid=(t // tm,),
        in_specs=[pl.BlockSpec((tm, D_MODEL), lambda i, d: (i, 0)),
                  pl.BlockSpec((tm, LANES), lambda i, d: (i, 0)), vec, vec,
                  pl.BlockSpec(memory_space=pl.ANY)],
        out_specs=pl.BlockSpec((tm, D_MODEL), lambda i, d: (i, 0)),
        scratch_shapes=[pltpu.VMEM((2, tm * SUBLANES, LANES), F32),
                        pltpu.VMEM((2, tm * SUBLANES, LANES), F32),
                        pltpu.SemaphoreType.DMA((2,))],
    )
    return pl.pallas_call(
        functools.partial(_moe_combine_kernel, final_norm=final_norm),
        out_shape=jax.ShapeDtypeStruct((t, D_MODEL), F32),
        grid_spec=grid_spec,
        compiler_params=_cparams("arbitrary"),
        name="moe_combine",
    )(dest, x, info, gate, norm_g, y)


def _moe_layer(mixed, g, shift, scale, gate, router_w, router_b, wg, wu, wd, norm_g, final_norm):
    t = mixed[4].shape[0]
    n_asg = t * TOP_K
    x, h, info = _moe_route(mixed, g, shift, scale, router_w, router_b)
    expert = info[:, :TOP_K].astype(jnp.int32).reshape(-1)
    onehot = (expert[:, None] == jnp.arange(N_EXPERTS, dtype=jnp.int32)[None, :]).astype(jnp.int32)
    csum = jnp.cumsum(onehot, axis=0)
    counts = csum[-1]
    padded = (counts + MOE_BLOCK - 1) // MOE_BLOCK * MOE_BLOCK
    pad_end = jnp.cumsum(padded)
    pad_start = pad_end - padded
    dest = jnp.sum(onehot * (csum - 1 + pad_start[None, :]), axis=1).astype(jnp.int32)
    n_blk = -(-n_asg // MOE_BLOCK) + N_EXPERTS
    blk_start = jnp.arange(n_blk, dtype=jnp.int32) * MOE_BLOCK
    blk_expert = jnp.minimum(jnp.sum(blk_start[:, None] >= pad_end[None, :], axis=1),
                             N_EXPERTS - 1).astype(jnp.int32)
    n_used = (pad_end[-1:] // MOE_BLOCK).astype(jnp.int32)
    fill = jnp.stack([pad_start + counts, pad_end]).astype(jnp.int32)
    y = _moe_experts(h, blk_expert, n_used, dest, fill, wg, wu, wd, n_blk)
    return _moe_combine(dest, x, info, gate, norm_g, y, final_norm)


def _final_norm_kernel(x_ref, g_ref, o_ref):
    x = x_ref[...]
    ms = jnp.mean(x * x, axis=-1, keepdims=True)
    o_ref[...] = x * lax.rsqrt(ms + EPS) * g_ref[...]


def _final_norm(x, g):
    t = x.shape[0]
    tm = min(t, 1024)
    return pl.pallas_call(
        _final_norm_kernel,
        out_shape=jax.ShapeDtypeStruct((t, D_MODEL), F32),
        grid=(t // tm,),
        in_specs=[pl.BlockSpec((tm, D_MODEL), lambda i: (i, 0)),
                  pl.BlockSpec((1, D_MODEL), lambda i: (0, 0))],
        out_specs=pl.BlockSpec((tm, D_MODEL), lambda i: (i, 0)),
        compiler_params=_cparams("parallel"),
        name="final_norm",
    )(x, g)


def kernel(x, c, ctx, c_ctx, ada_w, ada_b, norm1_g, norm2_g, w_in, w_out, hy_conv_w, hy_conv_b, hy_filt_w1, hy_filt_b1, hy_filt_w2, hy_filt_b2, hy_filt_w3, hy_filt_b3, hy_filt_freq, hy_bias, ret_log_rate, attn_q_g, attn_k_g, ffn_w_gate, ffn_w_up, ffn_w_down, moe_router_w, moe_router_b, moe_w_gate, moe_w_up, moe_w_down, final_norm_g):
    assert x.shape[0] == 1 and c.shape[0] == 1
    seq_len = x.shape[1]
    x_lat = x[0]
    x_ctx = ctx[0]
    cos_t, sin_t = _rope_tables(seq_len)
    cvec = jnp.concatenate([c, c_ctx[None, :], jnp.zeros((6, D_MODEL), F32)], axis=0)
    cvec = _silu(cvec)
    row = lambda v: v.reshape(1, D_MODEL)
    zero_state = jnp.zeros((2, RET_WIDTH, RET_HEAD_DIM), F32)
    zero_tab = jnp.zeros((x_ctx.shape[0], LANES), F32)

    for l in range(DEPTH):
        last = l == DEPTH - 1
        mods = _ada_modulation(cvec, ada_w, ada_b, l).reshape(8, 6, D_MODEL)
        mod = [row(mods[0, i]) for i in range(6)]
        mod_c = [row(mods[1, i]) for i in range(6)]
        filt = (hy_filt_w1[l], hy_filt_b1[l], hy_filt_w2[l], hy_filt_b2[l],
                hy_filt_w3[l], hy_filt_b3[l], hy_filt_freq[l])
        lg_f = -jnp.exp(ret_log_rate[l, 0].astype(F32))
        lg_b = -jnp.exp(ret_log_rate[l, 1].astype(F32))
        mask, zeta, xi, chunk_decay = _ret_tables(lg_f, lg_b)
        w_in_l = w_in[l].astype(BF16)
        w_out_l = w_out[l].astype(BF16)
        n1 = row(norm1_g[l])
        n2 = row(norm2_g[l])

        p_hy, p_ret, qt, k_aug, vt, kmax_sq = _in_projection(
            x_lat, n1, mod[0], mod[1], w_in_l, cos_t, sin_t, attn_q_g[l], attn_k_g[l], rope=True)
        pc_hy, pc_ret, qct, kc, vct, _ = _in_projection(
            x_ctx, n1, mod_c[0], mod_c[1], w_in_l, zero_tab, zero_tab, attn_q_g[l], attn_k_g[l],
            rope=False)

        sprev_c, s_ctx = _retention_states(pc_ret, zeta, chunk_decay, zero_state)

        y_hy = _hyena_mixer(p_hy, hy_conv_w[l], hy_conv_b[l], filt, hy_bias[l])
        sprev, _ = _retention_states(p_ret, zeta, chunk_decay, s_ctx)
        y_ret = _retention_outputs(p_ret, mask, xi, sprev)
        kmax = jnp.sqrt(jnp.max(kmax_sq[:, 0, ::ATT_HEAD_DIM], axis=0)) * KMAX_SLACK
        y_att = _attention(qt, kc, vct, kmax, k_aug, vt, tq=ATT_TQ, tk=ATT_TK, sub=ATT_SUB)

        if not last:
            yc_hy = _hyena_mixer(pc_hy, hy_conv_w[l], hy_conv_b[l], filt, hy_bias[l])
            yc_ret = _retention_outputs(pc_ret, mask, xi, sprev_c)
            yc_att = _attention(qct, kc, vct, tq=x_ctx.shape[0])
            mixed_c = (yc_hy, yc_ret, yc_att, w_out_l, x_ctx, mod_c[2])
        mixed = (y_hy, y_ret, y_att, w_out_l, x_lat, mod[2])

        i = l // 2
        if l % 2 == 0:
            wg, wu, wd = (ffn_w_gate[i].astype(BF16), ffn_w_up[i].astype(BF16),
                          ffn_w_down[i].astype(BF16))
            x_lat = _dense_ffn(mixed, n2, mod[3], mod[4], mod[5], wg, wu, wd)
            if not last:
                x_ctx = _dense_ffn(mixed_c, n2, mod_c[3], mod_c[4], mod_c[5], wg, wu, wd)
        else:
            wg, wu, wd = (moe_w_gate[i].astype(BF16), moe_w_up[i].astype(BF16),
                          moe_w_down[i].astype(BF16))
            fg = row(final_norm_g)
            x_lat = _moe_layer(mixed, n2, mod[3], mod[4], mod[5], moe_router_w[i], moe_router_b[i],
                               wg, wu, wd, fg, final_norm=last)
            if not last:
                x_ctx = _moe_layer(mixed_c, n2, mod_c[3], mod_c[4], mod_c[5], moe_router_w[i],
                                   moe_router_b[i], wg, wu, wd, fg, final_norm=False)
    if DEPTH % 2 == 1:
        x_lat = _final_norm(x_lat, row(final_norm_g))
    return x_lat[None]
```

```python
import functools
import math

import numpy as np
import jax
import jax.numpy as jnp
from jax import lax
from jax.experimental import pallas as pl
from jax.experimental.pallas import tpu as pltpu

F32 = jnp.float32
BF16 = jnp.bfloat16

D_MODEL = 1024
DEPTH = 2
GRID_W = 64
EPS = 1e-6

HY_WIDTH = 256
HY_COLS = 3 * HY_WIDTH
HY_POS_BANDS = 16
HY_DECAY_TARGET = 1e-2
HY_SHORT_DECAY_PCT = 0.3
HY_LONG_DECAY_PCT = 1.5

RET_HEAD_DIM = 64
RET_HEADS = 4
RET_WIDTH = RET_HEADS * RET_HEAD_DIM
RET_COLS = 4 * RET_WIDTH
RET_CHUNK = 128

ATT_HEAD_DIM = 64
ATT_HEADS = 8
ATT_KV_HEADS = 2
ATT_GROUP = ATT_HEADS // ATT_KV_HEADS
ATT_WIDTH = ATT_HEADS * ATT_HEAD_DIM
ATT_KV_WIDTH = ATT_KV_HEADS * ATT_HEAD_DIM
ATT_COLS = ATT_WIDTH + 2 * ATT_KV_WIDTH
ROPE_AXIS_DIM = ATT_HEAD_DIM // 2
ROPE_BASE = 10000.0

MIX_WIDTH = HY_WIDTH + RET_WIDTH + ATT_WIDTH
IN_COLS = HY_COLS + RET_COLS + ATT_COLS

FFN_HIDDEN = 2816
N_EXPERTS = 8
TOP_K = 2
MOE_BLOCK = 512

LOG2_E = 1.4426950408889634
EXP2_CAP = 60.0
KMAX_SLACK = 1.0 + 2.0 ** -7
ATT_TQ = 512
ATT_TK = 4096
ATT_SUB = 512
ATT_VROWS = 80
RET_STATE_CHUNKS = 16
RET_OUT_CHUNKS = 8
SWIGLU_CHUNK = 1024
DMA_UNROLL = 8

LANES = 128
SUBLANES = 8
VMEM_LIMIT = 56 * 1024 * 1024


def _cparams(*sem):
    return pltpu.CompilerParams(dimension_semantics=sem, vmem_limit_bytes=VMEM_LIMIT)


def _dot(a, b):
    return jnp.dot(a, b, preferred_element_type=F32)


def _split(a):
    hi = a.astype(BF16)
    lo = (a - hi.astype(F32)).astype(BF16)
    return hi, lo


def _dot3(a, b):
    ah, al = _split(a)
    bh, bl = _split(b)
    return _dot(ah, bh) + _dot(al, bh) + _dot(ah, bl)


def _silu(x):
    return x * (1.0 / (1.0 + jnp.exp(-x)))


def _norm_mod(x, g, shift, scale):
    ms = jnp.mean(x * x, axis=-1, keepdims=True)
    h = x * lax.rsqrt(ms + EPS) * g
    return h * (1.0 + scale) + shift


def _ada_kernel(c_ref, w_ref, b_ref, o_ref):
    o_ref[...] = _dot3(c_ref[...], w_ref[0]) + b_ref[0]


def _ada_modulation(cc, w, b, layer):
    n = w.shape[2]
    tn = 1536
    return pl.pallas_call(
        _ada_kernel,
        out_shape=jax.ShapeDtypeStruct((8, n), F32),
        grid=(n // tn,),
        in_specs=[pl.BlockSpec((8, D_MODEL), lambda j: (0, 0)),
                  pl.BlockSpec((1, D_MODEL, tn), lambda j: (layer, 0, j)),
                  pl.BlockSpec((1, 1, tn), lambda j: (layer, 0, j))],
        out_specs=pl.BlockSpec((8, tn), lambda j: (0, j)),
        compiler_params=_cparams("parallel"),
        name="ada_modulation",
    )(cc, w, b.reshape(b.shape[0], 1, n))


def _head_mean_matrix(width):
    idx = np.arange(width) // ATT_HEAD_DIM
    return jnp.asarray((idx[:, None] == idx[None, :]).astype(np.float32) / ATT_HEAD_DIM, BF16)


def _head_rms(x, bd, g):
    ms = _dot((x * x).astype(BF16), bd)
    return x * lax.rsqrt(ms + EPS) * g


def _rope(x, cos, sin):
    n = x.shape[1]
    lane = lax.broadcasted_iota(jnp.int32, x.shape, 1)
    swapped = jnp.where((lane % 32) < 16, pltpu.roll(x, n - 16, 1), pltpu.roll(x, 16, 1))
    return x * cos + swapped * sin


def _attention_operands(p, cos_ref, sin_ref, qg_ref, kg_ref, bdq_ref, bdk_ref,
                        qt_ref, k_ref, vt_ref, kmax_ref, rope):
    q = _head_rms(p[:, :ATT_WIDTH], bdq_ref[...], qg_ref[...])
    k = _head_rms(p[:, ATT_WIDTH:ATT_WIDTH + ATT_KV_WIDTH], bdk_ref[...], kg_ref[...])
    if rope:
        cos = cos_ref[...]
        sin = sin_ref[...]
        q = jnp.concatenate(
            [_rope(q[:, j * LANES:(j + 1) * LANES], cos, sin) for j in range(ATT_WIDTH // LANES)],
            axis=1)
        k = _rope(k, cos, sin)
    qt_ref[...] = (q * (ATT_HEAD_DIM ** -0.5 * LOG2_E)).T.astype(BF16)
    tm = k.shape[0]
    lane = lax.broadcasted_iota(jnp.int32, k.shape, 1)
    minus_one_col = jnp.where(lane == ATT_HEAD_DIM, -1.0, 0.0)
    kb = k.astype(BF16)
    k_ref[0] = jnp.where(lane < ATT_HEAD_DIM, k, minus_one_col).astype(BF16)
    k_ref[1] = jnp.where(lane < ATT_HEAD_DIM, pltpu.roll(k, ATT_HEAD_DIM, 1), minus_one_col).astype(BF16)
    norm_sq = _dot((kb.astype(F32) ** 2).astype(BF16), bdk_ref[...]) * ATT_HEAD_DIM
    kmax_ref[0] = jnp.broadcast_to(jnp.max(norm_sq, axis=0, keepdims=True), (SUBLANES, LANES))
    vt = p[:, ATT_WIDTH + ATT_KV_WIDTH:].T.astype(BF16)
    sub = lax.broadcasted_iota(jnp.int32, (ATT_VROWS - ATT_HEAD_DIM, tm), 0)
    ones_row = jnp.where(sub == 0, 1.0, 0.0).astype(BF16)
    for h in range(ATT_KV_HEADS):
        vt_ref[h * ATT_VROWS:h * ATT_VROWS + ATT_HEAD_DIM, :] = vt[h * ATT_HEAD_DIM:(h + 1) * ATT_HEAD_DIM]
        vt_ref[h * ATT_VROWS + ATT_HEAD_DIM:(h + 1) * ATT_VROWS, :] = ones_row


def _inproj_kernel(x_ref, g_ref, sh_ref, sc_ref, w_ref, cos_ref, sin_ref, qg_ref, kg_ref, bdq_ref,
                   bdk_ref, hy_ref, ret_ref, qt_ref, k_ref, vt_ref, kmax_ref, *, rope):
    h = _norm_mod(x_ref[...], g_ref[...], sh_ref[...], sc_ref[...])
    p = _dot(h.astype(BF16), w_ref[...])
    hy_ref[...] = p[:, :HY_COLS]
    ret_ref[...] = p[:, HY_COLS:HY_COLS + RET_COLS]
    _attention_operands(p[:, HY_COLS + RET_COLS:], cos_ref, sin_ref, qg_ref, kg_ref, bdq_ref, bdk_ref,
                        qt_ref, k_ref, vt_ref, kmax_ref, rope)


def _in_projection(x, g, shift, scale, w_bf16, cos, sin, q_g, k_g, rope):
    t = x.shape[0]
    tm = min(t, 512)
    qg = jnp.tile(q_g, ATT_HEADS).reshape(1, ATT_WIDTH)
    kg = jnp.tile(k_g, ATT_KV_HEADS).reshape(1, ATT_KV_WIDTH)
    vec = pl.BlockSpec((1, D_MODEL), lambda i: (0, 0))
    const = lambda shape: pl.BlockSpec(shape, lambda i: (0, 0))
    return pl.pallas_call(
        functools.partial(_inproj_kernel, rope=rope),
        out_shape=(jax.ShapeDtypeStruct((t, HY_COLS), F32),
                   jax.ShapeDtypeStruct((t, RET_COLS), F32),
                   jax.ShapeDtypeStruct((ATT_WIDTH, t), BF16),
                   jax.ShapeDtypeStruct((ATT_KV_HEADS, t, LANES), BF16),
                   jax.ShapeDtypeStruct((ATT_KV_HEADS * ATT_VROWS, t), BF16),
                   jax.ShapeDtypeStruct((t // tm, SUBLANES, LANES), F32)),
        grid=(t // tm,),
        in_specs=[pl.BlockSpec((tm, D_MODEL), lambda i: (i, 0)), vec, vec, vec,
                  const((D_MODEL, IN_COLS)),
                  pl.BlockSpec((tm, LANES), lambda i: (i, 0)),
                  pl.BlockSpec((tm, LANES), lambda i: (i, 0)),
                  const((1, ATT_WIDTH)), const((1, ATT_KV_WIDTH)),
                  const((ATT_WIDTH, ATT_WIDTH)), const((ATT_KV_WIDTH, ATT_KV_WIDTH))],
        out_specs=(pl.BlockSpec((tm, HY_COLS), lambda i: (i, 0)),
                   pl.BlockSpec((tm, RET_COLS), lambda i: (i, 0)),
                   pl.BlockSpec((ATT_WIDTH, tm), lambda i: (0, i)),
                   pl.BlockSpec((ATT_KV_HEADS, tm, LANES), lambda i: (0, i, 0)),
                   pl.BlockSpec((ATT_KV_HEADS * ATT_VROWS, tm), lambda i: (0, i)),
                   pl.BlockSpec((1, SUBLANES, LANES), lambda i: (i, 0, 0))),
        compiler_params=_cparams("parallel"),
        name="in_projection",
    )(x, g, shift, scale, w_bf16, cos, sin, qg, kg,
      _head_mean_matrix(ATT_WIDTH), _head_mean_matrix(ATT_KV_WIDTH))


def _rope_tables(n_tokens):
    rows = n_tokens // GRID_W
    inv_freq = ROPE_BASE ** (-jnp.arange(0, ROPE_AXIS_DIM, 2, dtype=F32) / ROPE_AXIS_DIM)
    ang_r = jnp.arange(rows, dtype=F32)[:, None] * inv_freq[None, :]
    ang_c = jnp.arange(GRID_W, dtype=F32)[:, None] * inv_freq[None, :]
    nf = inv_freq.shape[0]
    by_row = lambda tab: jnp.broadcast_to(tab[:, None, :], (rows, GRID_W, nf)).reshape(n_tokens, nf)
    by_col = lambda tab: jnp.broadcast_to(tab[None, :, :], (rows, GRID_W, nf)).reshape(n_tokens, nf)
    cr, sr = by_row(jnp.cos(ang_r)), by_row(jnp.sin(ang_r))
    cc, sc = by_col(jnp.cos(ang_c)), by_col(jnp.sin(ang_c))
    cos = jnp.concatenate([cr, cr, cc, cc], axis=1)
    sin = jnp.concatenate([-sr, sr, -sc, sc], axis=1)
    return jnp.tile(cos, (1, 2)), jnp.tile(sin, (1, 2))


def _attention_kernel(*refs, tq, tk, sub, n_lat):
    if n_lat:
        kmax_ref, qt_ref, kc_ref, vct_ref, k_ref, vt_ref, o_ref, qs_ref, acc_ref, m_ref = refs
    else:
        qt_ref, kc_ref, vct_ref, o_ref, acc_ref = refs
    cols = ATT_GROUP * tq
    qt = qt_ref[...]
    qst = jnp.concatenate(
        [qt[g * ATT_HEAD_DIM:(g + 1) * ATT_HEAD_DIM, :] for g in range(ATT_GROUP)], axis=1)
    pad = jnp.zeros((LANES - ATT_HEAD_DIM, cols), BF16)
    s = _dot(kc_ref[0], jnp.concatenate([qst, pad], axis=0))
    mt = jnp.max(s, axis=0, keepdims=True).astype(BF16)
    mtf = mt.astype(F32)
    acc_ref[...] = _dot(vct_ref[...], jnp.exp2(s - mtf).astype(BF16))

    if n_lat:
        row = lax.broadcasted_iota(jnp.int32, pad.shape, 0)
        qs_ref[...] = jnp.concatenate([qst, jnp.where(row == 0, mtf, 0.0).astype(BF16)], axis=0)
        qf = qst.astype(F32)
        qn = jnp.sqrt(jnp.sum(qf * qf, axis=0, keepdims=True))
        fast = jnp.max(qn * kmax_ref[pl.program_id(0)] - mtf) <= EXP2_CAP

        def tile(j):
            off = pl.multiple_of(j * tk, tk)
            return k_ref[0, pl.ds(off, tk), :], vt_ref[:, pl.ds(off, tk)]

        @pl.when(fast)
        def _():
            def body(j, carry):
                k, vt = tile(j)
                qs = qs_ref[...]
                part = None
                for c in range(tk // sub):
                    p = jnp.exp2(_dot(k[c * sub:(c + 1) * sub], qs)).astype(BF16)
                    pv = _dot(vt[:, c * sub:(c + 1) * sub], p)
                    part = pv if part is None else part + pv
                acc_ref[...] += part
                return carry

            lax.fori_loop(0, n_lat, body, 0)

        @pl.when(jnp.logical_not(fast))
        def _():
            m_ref[...] = jnp.zeros_like(m_ref)

            def body(j, carry):
                k, vt = tile(j)
                s = _dot(k, qs_ref[...])
                m_old = m_ref[...]
                m_new = jnp.maximum(m_old, jnp.max(s, axis=0, keepdims=True))
                p = jnp.exp2(s - m_new).astype(BF16)
                acc_ref[...] = jnp.exp2(m_old - m_new) * acc_ref[...] + _dot(vt, p)
                m_ref[...] = m_new
                return carry

            lax.fori_loop(0, n_lat, body, 0)

    acc = acc_ref[...]
    o = acc[:ATT_HEAD_DIM] / acc[ATT_HEAD_DIM:ATT_HEAD_DIM + 1]
    o_ref[...] = jnp.concatenate(
        [o[:, g * tq:(g + 1) * tq].T for g in range(ATT_GROUP)], axis=1).astype(o_ref.dtype)


def _attention(qt, kc, vct, kmax=None, k=None, vt=None, *, tq, tk=0, sub=0):
    t = qt.shape[1]
    lc = kc.shape[1]
    gw = ATT_GROUP * ATT_HEAD_DIM
    cols = ATT_GROUP * tq
    in_specs = [pl.BlockSpec((gw, tq), lambda h, i: (h, i)),
                pl.BlockSpec((1, lc, LANES), lambda h, i: (h, 0, 0)),
                pl.BlockSpec((ATT_VROWS, lc), lambda h, i: (h, 0))]
    args = [qt, kc, vct]
    scratch = [pltpu.VMEM((ATT_VROWS, cols), F32)]
    n_lat = 0
    if k is not None:
        lk = k.shape[1]
        n_lat = lk // tk
        in_specs = ([pl.BlockSpec(memory_space=pltpu.SMEM)] + in_specs
                    + [pl.BlockSpec((1, lk, LANES), lambda h, i: (h, 0, 0)),
                       pl.BlockSpec((ATT_VROWS, lk), lambda h, i: (h, 0))])
        args = [kmax] + args + [k, vt]
        scratch = [pltpu.VMEM((LANES, cols), BF16)] + scratch + [pltpu.VMEM((1, cols), F32)]
    return pl.pallas_call(
        functools.partial(_attention_kernel, tq=tq, tk=tk, sub=sub, n_lat=n_lat),
        out_shape=jax.ShapeDtypeStruct((t, ATT_WIDTH), BF16),
        grid=(ATT_KV_HEADS, t // tq),
        in_specs=in_specs,
        out_specs=pl.BlockSpec((tq, gw), lambda h, i: (i, h)),
        scratch_shapes=scratch,
        compiler_params=_cparams("parallel", "parallel"),
        name="attention_lat" if n_lat else "attention_ctx",
    )(*args)


def _ret_tables(lg_f, lg_b):
    pos = jnp.arange(RET_CHUNK, dtype=F32)
    rel = pos[:, None] - pos[None, :]
    mask = jnp.where(rel > 0, jnp.exp(lg_f[:, None, None] * jnp.maximum(rel, 0.0)),
                     jnp.where(rel < 0, jnp.exp(lg_b[:, None, None] * jnp.maximum(-rel, 0.0)), 2.0))
    zeta = jnp.stack([jnp.exp(lg_f[:, None] * (RET_CHUNK - 1 - pos)[None, :]),
                      jnp.exp(lg_b[:, None] * pos[None, :])])
    xi = jnp.stack([jnp.exp(lg_f[:, None] * (pos + 1)[None, :]),
                    jnp.exp(lg_b[:, None] * (RET_CHUNK - pos)[None, :])])
    chunk_decay = jnp.stack([jnp.exp(lg_f * RET_CHUNK), jnp.exp(lg_b * RET_CHUNK)])
    zeta = jnp.repeat(zeta, RET_HEAD_DIM, axis=1).transpose(0, 2, 1)
    xi = jnp.repeat(xi, RET_HEAD_DIM, axis=1).transpose(0, 2, 1)
    chunk_decay = jnp.repeat(chunk_decay, RET_HEAD_DIM, axis=1)[:, :, None]
    return mask, zeta, xi, chunk_decay


def _ret_state_kernel(k_ref, v_ref, zeta_ref, cd_ref, s0_ref, sprev_ref, sfin_ref, s_ref, *, cps):
    d = pl.program_id(0)

    @pl.when(pl.program_id(1) == 0)
    def _():
        s_ref[...] = s0_ref[0]

    zeta = zeta_ref[0] * (RET_HEAD_DIM ** -0.5)
    cd = cd_ref[0]
    us = []
    for c in range(cps):
        rows = slice(c * RET_CHUNK, (c + 1) * RET_CHUNK)
        kz = (k_ref[rows, :] * zeta).astype(BF16)
        v = v_ref[rows, :].astype(BF16)
        heads = []
        for h in range(RET_HEADS):
            sl = slice(h * RET_HEAD_DIM, (h + 1) * RET_HEAD_DIM)
            heads.append(lax.dot_general(kz[:, sl], v[:, sl], (((0,), (0,)), ((), ())),
                                         preferred_element_type=F32))
        us.append(jnp.concatenate(heads, axis=0))

    def scan(order):
        s = s_ref[...]
        for c in order:
            sprev_ref[0, c] = s
            s = cd * s + us[c]
        s_ref[...] = s
        sfin_ref[0] = s

    @pl.when(d == 0)
    def _():
        scan(range(cps))

    @pl.when(d == 1)
    def _():
        scan(reversed(range(cps)))


def _retention_states(p_ret, zeta, chunk_decay, s0):
    nc = p_ret.shape[0] // RET_CHUNK
    cps = min(nc, RET_STATE_CHUNKS)
    ng = nc // cps
    rows = cps * RET_CHUNK

    def group(d, i):
        return i + d * (ng - 1 - 2 * i)

    return pl.pallas_call(
        functools.partial(_ret_state_kernel, cps=cps),
        out_shape=(jax.ShapeDtypeStruct((2, nc, RET_WIDTH, RET_HEAD_DIM), F32),
                   jax.ShapeDtypeStruct((2, RET_WIDTH, RET_HEAD_DIM), F32)),
        grid=(2, ng),
        in_specs=[pl.BlockSpec((rows, RET_WIDTH), lambda d, i: (group(d, i), 1)),
                  pl.BlockSpec((rows, RET_WIDTH), lambda d, i: (group(d, i), 2)),
                  pl.BlockSpec((1, RET_CHUNK, RET_WIDTH), lambda d, i: (d, 0, 0)),
                  pl.BlockSpec((1, RET_WIDTH, 1), lambda d, i: (d, 0, 0)),
                  pl.BlockSpec((1, RET_WIDTH, RET_HEAD_DIM), lambda d, i: (d, 0, 0))],
        out_specs=(pl.BlockSpec((1, cps, RET_WIDTH, RET_HEAD_DIM), lambda d, i: (d, group(d, i), 0, 0)),
                   pl.BlockSpec((1, RET_WIDTH, RET_HEAD_DIM), lambda d, i: (d, 0, 0))),
        scratch_shapes=[pltpu.VMEM((RET_WIDTH, RET_HEAD_DIM), F32)],
        compiler_params=_cparams("arbitrary", "arbitrary"),
        name="retention_states",
    )(p_ret, p_ret, zeta, chunk_decay, s0)


def _ret_out_kernel(q_ref, k_ref, v_ref, g_ref, mask_ref, xi_ref, sf_ref, sb_ref, y_ref, *, cpo):
    for c in range(cpo):
        rows = slice(c * RET_CHUNK, (c + 1) * RET_CHUNK)
        q = q_ref[rows, :]
        k = (k_ref[rows, :] * (RET_HEAD_DIM ** -0.5)).astype(BF16)
        v = v_ref[rows, :].astype(BF16)
        qb = q.astype(BF16)
        qf = (q * xi_ref[0]).astype(BF16)
        qr = (q * xi_ref[1]).astype(BF16)
        sf = sf_ref[0, c].astype(BF16)
        sb = sb_ref[0, c].astype(BF16)
        ys = []
        for h in range(RET_HEADS):
            sl = slice(h * RET_HEAD_DIM, (h + 1) * RET_HEAD_DIM)
            scores = lax.dot_general(qb[:, sl], k[:, sl], (((1,), (1,)), ((), ())),
                                     preferred_element_type=F32)
            y = _dot((scores * mask_ref[h]).astype(BF16), v[:, sl])
            y = y + _dot(qf[:, sl], sf[sl, :]) + _dot(qr[:, sl], sb[sl, :])
            mu = jnp.mean(y, axis=-1, keepdims=True)
            yc = y - mu
            var = jnp.mean(yc * yc, axis=-1, keepdims=True)
            ys.append(yc * lax.rsqrt(var + EPS))
        y_ref[rows, :] = (_silu(g_ref[rows, :]) * jnp.concatenate(ys, axis=1)).astype(y_ref.dtype)


def _retention_outputs(p_ret, mask, xi, sprev):
    nc = p_ret.shape[0] // RET_CHUNK
    cpo = min(nc, RET_OUT_CHUNKS)
    rows = cpo * RET_CHUNK
    col = lambda c: pl.BlockSpec((rows, RET_WIDTH), lambda i: (i, c))
    return pl.pallas_call(
        functools.partial(_ret_out_kernel, cpo=cpo),
        out_shape=jax.ShapeDtypeStruct((p_ret.shape[0], RET_WIDTH), BF16),
        grid=(nc // cpo,),
        in_specs=[col(0), col(1), col(2), col(3),
                  pl.BlockSpec((RET_HEADS, RET_CHUNK, RET_CHUNK), lambda i: (0, 0, 0)),
                  pl.BlockSpec((2, RET_CHUNK, RET_WIDTH), lambda i: (0, 0, 0)),
                  pl.BlockSpec((1, cpo, RET_WIDTH, RET_HEAD_DIM), lambda i: (0, i, 0, 0)),
                  pl.BlockSpec((1, cpo, RET_WIDTH, RET_HEAD_DIM), lambda i: (1, i, 0, 0))],
        out_specs=pl.BlockSpec((rows, RET_WIDTH), lambda i: (i, 0)),
        compiler_params=_cparams("parallel"),
        name="retention_outputs",
    )(p_ret, p_ret, p_ret, p_ret, mask, xi, sprev, sprev)


def _hy_pre_kernel(p_ref, prev_ref, next_ref, w_ref, b_ref, x0_ref, z_ref):
    i = pl.program_id(0)
    x = p_ref[...]
    tm = x.shape[0]
    row = lax.broadcasted_iota(jnp.int32, x.shape, 0)
    prev_row = jnp.where(i == 0, 0.0, prev_ref[7:8, :])
    next_row = jnp.where(i == pl.num_programs(0) - 1, 0.0, next_ref[0:1, :])
    up = jnp.where(row == 0, prev_row, pltpu.roll(x, 1, 0))
    dn = jnp.where(row == tm - 1, next_row, pltpu.roll(x, tm - 1, 0))
    w = w_ref[...]
    u = up * w[0:1] + x * w[1:2] + dn * w[2:3] + b_ref[...]
    x0_ref[...] = u[:, :HY_WIDTH]
    z_ref[...] = u[:, HY_WIDTH:2 * HY_WIDTH] * u[:, 2 * HY_WIDTH:]


def _hyena_pre(p_hy, conv_w, conv_b):
    t = p_hy.shape[0]
    tm = min(t, 512)
    nb8 = tm // 8
    last8 = t // 8 - 1
    return pl.pallas_call(
        _hy_pre_kernel,
        out_shape=(jax.ShapeDtypeStruct((t, HY_WIDTH), F32),
                   jax.ShapeDtypeStruct((t, HY_WIDTH), F32)),
        grid=(t // tm,),
        in_specs=[pl.BlockSpec((tm, HY_COLS), lambda i: (i, 0)),
                  pl.BlockSpec((8, HY_COLS), lambda i: (jnp.maximum(i * nb8 - 1, 0), 0)),
                  pl.BlockSpec((8, HY_COLS), lambda i: (jnp.minimum((i + 1) * nb8, last8), 0)),
                  pl.BlockSpec((3, HY_COLS), lambda i: (0, 0)),
                  pl.BlockSpec((1, HY_COLS), lambda i: (0, 0))],
        out_specs=(pl.BlockSpec((tm, HY_WIDTH), lambda i: (i, 0)),
                   pl.BlockSpec((tm, HY_WIDTH), lambda i: (i, 0))),
        compiler_params=_cparams("parallel"),
        name="hyena_pre",
    )(p_hy, p_hy, p_hy, conv_w, conv_b.reshape(1, HY_COLS))


def _hy_filter_kernel(feat_ref, w1_ref, b1_ref, w2_ref, b2_ref, w3_ref, b3_ref, freq_ref,
                      delta_ref, taps_ref, asum_ref, *, seq_len):
    i = pl.program_id(0)
    feat = feat_ref[...]
    tm = feat.shape[0]
    half = tm // 2
    freq = freq_ref[...]
    packed = jnp.concatenate([feat[:half], feat[half:]], axis=1)
    h = jnp.sin(freq * (_dot3(packed, w1_ref[...]) + b1_ref[...]))
    h = jnp.sin(freq * (_dot3(h, w2_ref[...]) + b2_ref[...]))
    h = _dot3(h, w3_ref[0])
    h = jnp.concatenate([h[:, :HY_WIDTH], h[:, HY_WIDTH:]], axis=0) + b3_ref[0]
    j = i * tm + lax.broadcasted_iota(jnp.int32, (tm, HY_WIDTH), 0)
    taps = jnp.where(j == seq_len, 0.0, h * jnp.exp(-feat[:, 0:1] * delta_ref[...]))
    taps_ref[...] = taps

    @pl.when(i == 0)
    def _():
        asum_ref[...] = jnp.zeros_like(asum_ref)

    asum_ref[...] += jnp.sum(jnp.abs(taps).reshape(tm // SUBLANES, SUBLANES, HY_WIDTH), axis=0)


def _position_features(seq_len):
    lo = 256
    hi = seq_len // lo + 1
    bands = np.linspace(1e-4, HY_POS_BANDS - 1, HY_POS_BANDS).astype(np.float32).astype(np.float64)
    ang_hi = (2.0 * np.pi / seq_len) * lo * np.arange(hi)[:, None] * bands[None, :]
    ang_lo = (2.0 * np.pi / seq_len) * np.arange(lo)[:, None] * bands[None, :]
    tab = lambda a: jnp.asarray(a, F32)
    ch, sh = tab(np.cos(ang_hi))[:, None, :], tab(np.sin(ang_hi))[:, None, :]
    cl, sl = tab(np.cos(ang_lo))[None, :, :], tab(np.sin(ang_lo))[None, :, :]
    npos = seq_len + 1
    cos = (ch * cl - sh * sl).reshape(hi * lo, HY_POS_BANDS)[:npos]
    sin = (sh * cl + ch * sl).reshape(hi * lo, HY_POS_BANDS)[:npos]
    c_end, s_end = tab(np.cos(2.0 * np.pi * bands))[None, :], tab(np.sin(2.0 * np.pi * bands))[None, :]
    cos_b = c_end * cos[1:seq_len] + s_end * sin[1:seq_len]
    sin_b = s_end * cos[1:seq_len] - c_end * sin[1:seq_len]
    lag = jnp.concatenate([jnp.arange(npos, dtype=F32), jnp.arange(seq_len - 1, 0, -1, dtype=F32)])
    t = lag / max(seq_len - 1, 1)
    cos = jnp.concatenate([cos, cos_b], axis=0)
    sin = jnp.concatenate([sin, sin_b], axis=0)
    return jnp.concatenate([t[:, None], cos, -sin,
                            jnp.zeros((2 * seq_len, LANES - 1 - 2 * HY_POS_BANDS), F32)], axis=-1)


def _hyena_taps(seq_len, w1, b1, w2, b2, w3, b3, freq):
    n = 2 * seq_len
    feat = _position_features(seq_len)
    w1p = jnp.concatenate([w1, jnp.zeros((LANES - w1.shape[0], w1.shape[1]), F32)], axis=0)
    pair = lambda w: jnp.concatenate(
        [jnp.concatenate([w, jnp.zeros_like(w)], axis=1),
         jnp.concatenate([jnp.zeros_like(w), w], axis=1)], axis=0)
    twice = lambda v: jnp.tile(v.reshape(1, -1), (1, 2))
    w3_dirs = jnp.stack([pair(w3[:, :HY_WIDTH]), pair(w3[:, HY_WIDTH:])])
    b3_dirs = b3.reshape(2, 1, HY_WIDTH)
    deltas = jnp.abs(jnp.linspace(math.log(HY_DECAY_TARGET) / HY_LONG_DECAY_PCT,
                                  math.log(HY_DECAY_TARGET) / HY_SHORT_DECAY_PCT,
                                  HY_WIDTH, dtype=F32)).reshape(1, HY_WIDTH)
    tm = min(seq_len, 1024)
    per_dir = seq_len // tm
    hid2 = 2 * w1.shape[1]
    const = lambda shape: pl.BlockSpec(shape, lambda i: (0, 0))
    by_dir = lambda shape: pl.BlockSpec(shape, lambda i: (i // per_dir, 0, 0))
    return pl.pallas_call(
        functools.partial(_hy_filter_kernel, seq_len=seq_len),
        out_shape=(jax.ShapeDtypeStruct((n, HY_WIDTH), F32),
                   jax.ShapeDtypeStruct((8, HY_WIDTH), F32)),
        grid=(n // tm,),
        in_specs=[pl.BlockSpec((tm, LANES), lambda i: (i, 0)),
                  const((2 * LANES, hid2)), const((1, hid2)), const((hid2, hid2)), const((1, hid2)),
                  by_dir((1, hid2, 2 * HY_WIDTH)), by_dir((1, 1, HY_WIDTH)), const((1, hid2)),
                  const((1, HY_WIDTH))],
        out_specs=(pl.BlockSpec((tm, HY_WIDTH), lambda i: (i, 0)), const((8, HY_WIDTH))),
        compiler_params=_cparams("arbitrary"),
        name="hyena_taps",
    )(feat, pair(w1p), twice(b1), pair(w2), twice(b2), w3_dirs, b3_dirs, twice(freq), deltas)


FFT_N2 = 256
FFT_K1_PER_STEP = 8
FFT_N2_PER_STEP = 16


def _kept_slabs(n1):
    kept = n1 // 2 + 1
    return kept, -(-kept // SUBLANES) * SUBLANES


def _dft_tables(n_total, n1_in):
    n1 = n_total // FFT_N2
    kept, padded = _kept_slabs(n1)
    k1 = np.arange(padded)
    a = 2.0 * np.pi * np.outer(k1, np.arange(n1_in)) / n1
    live = (k1 < kept)[:, None]
    lvl1 = np.concatenate([np.cos(a) * live, -np.sin(a) * live], axis=0)
    n2 = np.arange(FFT_N2)
    th = 2.0 * np.pi * np.outer(n2, n2) / FFT_N2
    fc, fs = np.cos(th), np.sin(th)
    g = np.block([[fc, fs], [-fs, fc]])
    ph = 2.0 * np.pi * np.outer(k1, n2) / n_total
    tw = np.stack([np.cos(ph), np.sin(ph)])[..., None]
    return (jnp.asarray(lvl1, F32), jnp.asarray(g, F32), jnp.asarray(g.T, F32), jnp.asarray(tw, F32))


def _inverse_lvl1_table(n_total, n1_out):
    n1 = n_total // FFT_N2
    kept, padded = _kept_slabs(n1)
    k1 = np.arange(padded)
    a = 2.0 * np.pi * np.outer(np.arange(n1_out), k1) / n1
    weight = np.where(k1 >= kept, 0.0, np.where((k1 == 0) | (k1 == n1 // 2), 1.0, 2.0))[None, :]
    return jnp.asarray(np.concatenate([np.cos(a) * weight, -np.sin(a) * weight], axis=1), F32)


def _lvl2_kernel(b_ref, tw_ref, gh_ref, gl_ref, *rest, inverse):
    for j in range(b_ref.shape[1]):
        c = tw_ref[0, j]
        s = tw_ref[1, j]
        br = b_ref[0, j]
        bi = b_ref[1, j]
        d = jnp.concatenate([c * br + s * bi, c * bi - s * br], axis=0)
        dh, dl = _split(d)
        x = _dot(gh_ref[...], dh) + _dot(gh_ref[...], dl) + _dot(gl_ref[...], dh)
        if not inverse:
            (o_ref,) = rest
            o_ref[0, j] = x[:FFT_N2]
            o_ref[1, j] = x[FFT_N2:]
            continue
        h_ref, gth_ref, gtl_ref, o_ref = rest
        xr, xi = x[:FFT_N2], x[FFT_N2:]
        hr, hi = h_ref[0, j], h_ref[1, j]
        y = jnp.concatenate([xr * hr - xi * hi, xr * hi + xi * hr], axis=0)
        yh, yl = _split(y)
        cc = _dot(gth_ref[...], yh) + _dot(gth_ref[...], yl) + _dot(gtl_ref[...], yh)
        cr, ci = cc[:FFT_N2], cc[FFT_N2:]
        o_ref[0, j] = c * cr - s * ci
        o_ref[1, j] = c * ci + s * cr


def _level2(b, tw, g, h=None, gt=None):
    _, n1, n2, ch = b.shape
    kb = FFT_K1_PER_STEP
    blk = pl.BlockSpec((2, kb, n2, ch), lambda i: (0, i, 0, 0))
    const = pl.BlockSpec((2 * n2, 2 * n2), lambda i: (0, 0))
    gh, gl = _split(g)
    in_specs = [blk, pl.BlockSpec((2, kb, n2, 1), lambda i: (0, i, 0, 0)), const, const]
    args = [b, tw, gh, gl]
    if h is not None:
        gth, gtl = _split(gt)
        in_specs += [blk, const, const]
        args += [h, gth, gtl]
    return pl.pallas_call(
        functools.partial(_lvl2_kernel, inverse=h is not None),
        out_shape=jax.ShapeDtypeStruct(b.shape, F32),
        grid=(n1 // kb,),
        in_specs=in_specs,
        out_specs=blk,
        compiler_params=_cparams("parallel"),
        name="hyena_fft_level2" + ("_conv" if h is not None else ""),
    )(*args)


def _lvl1_kernel(a_ref, x_ref, o_ref, *, inverse):
    tn2 = x_ref.shape[-2]
    ch = x_ref.shape[-1]
    if inverse:
        x = jnp.concatenate([x_ref[:, :, j, :].reshape(-1, ch) for j in range(tn2)], axis=1)
    else:
        x = jnp.concatenate([x_ref[:, j, :] for j in range(tn2)], axis=1)
    out = _dot3(a_ref[...], x)
    for j in range(tn2):
        slab = out[:, j * ch:(j + 1) * ch]
        if inverse:
            o_ref[:, j, :] = slab
        else:
            o_ref[:, :, j, :] = slab.reshape(2, slab.shape[0] // 2, ch)


def _level1(a, x, name):
    n1_in, n2, ch = x.shape
    n1 = a.shape[0] // 2
    tn2 = FFT_N2_PER_STEP
    return pl.pallas_call(
        functools.partial(_lvl1_kernel, inverse=False),
        out_shape=jax.ShapeDtypeStruct((2, n1, n2, ch), F32),
        grid=(n2 // tn2,),
        in_specs=[pl.BlockSpec(a.shape, lambda i: (0, 0)),
                  pl.BlockSpec((n1_in, tn2, ch), lambda i: (0, i, 0))],
        out_specs=pl.BlockSpec((2, n1, tn2, ch), lambda i: (0, 0, i, 0)),
        compiler_params=_cparams("parallel"),
        name=name,
    )(a, x)


def _inverse_level1(a, x):
    _, n1, n2, ch = x.shape
    n1_out = a.shape[0]
    tn2 = FFT_N2_PER_STEP
    return pl.pallas_call(
        functools.partial(_lvl1_kernel, inverse=True),
        out_shape=jax.ShapeDtypeStruct((n1_out, n2, ch), F32),
        grid=(n2 // tn2,),
        in_specs=[pl.BlockSpec(a.shape, lambda i: (0, 0)),
                  pl.BlockSpec((2, n1, tn2, ch), lambda i: (0, 0, i, 0))],
        out_specs=pl.BlockSpec((n1_out, tn2, ch), lambda i: (0, i, 0)),
        compiler_params=_cparams("parallel"),
        name="hyena_fft_inverse_level1",
    )(a, x)


def _hyena_long_conv(z, taps):
    seq_len, ch = z.shape
    n = 2 * seq_len
    n1 = n // FFT_N2
    lvl1_z, g, gt, tw = _dft_tables(n, n1 // 2)
    lvl1_t = _dft_tables(n, n1)[0]
    inv1 = _inverse_lvl1_table(n, n1 // 2)
    hb = _level1(lvl1_t, taps.reshape(n1, FFT_N2, ch), "hyena_fft_level1_taps")
    hspec = _level2(hb, tw, g)
    zb = _level1(lvl1_z, z.reshape(n1 // 2, FFT_N2, ch), "hyena_fft_level1_z")
    cspec = _level2(zb, tw, g, hspec, gt)
    return _inverse_level1(inv1, cspec).reshape(seq_len, ch)


def _small_conv_kernel(z_ref, taps_ref, fc_ref, fs_ref, o_ref):
    seq_len = z_ref.shape[0]
    fc = fc_ref[...]
    fs = fs_ref[...]
    z = z_ref[...]
    taps = taps_ref[...]
    zr = _dot3(fc[:, :seq_len], z)
    zi = -_dot3(fs[:, :seq_len], z)
    hr = _dot3(fc, taps)
    hi = -_dot3(fs, taps)
    yr = zr * hr - zi * hi
    yi = zr * hi + zi * hr
    o_ref[...] = _dot3(fc[:seq_len, :], yr) - _dot3(fs[:seq_len, :], yi)


def _hyena_small_conv(z, taps):
    seq_len, ch = z.shape
    n = 2 * seq_len
    th = 2.0 * np.pi * np.outer(np.arange(n), np.arange(n)) / n
    full = lambda shape: pl.BlockSpec(shape, lambda: (0, 0))
    return pl.pallas_call(
        _small_conv_kernel,
        out_shape=jax.ShapeDtypeStruct((seq_len, ch), F32),
        in_specs=[full((seq_len, ch)), full((n, ch)), full((n, n)), full((n, n))],
        out_specs=full((seq_len, ch)),
        compiler_params=pltpu.CompilerParams(vmem_limit_bytes=VMEM_LIMIT),
        name="hyena_small_conv",
    )(z, taps, jnp.asarray(np.cos(th), F32), jnp.asarray(np.sin(th), F32))


def _hy_post_kernel(x0_ref, z_ref, conv_ref, asum_ref, bias_ref, y_ref, *, n_total):
    norm = jnp.sum(asum_ref[...], axis=0, keepdims=True) + EPS
    conv = conv_ref[...] * (1.0 / (n_total * norm))
    y_ref[...] = (x0_ref[...] * (conv + bias_ref[...] * z_ref[...])).astype(y_ref.dtype)


def _hyena_post(x0, z, conv, asum, bias):
    t = x0.shape[0]
    tm = min(t, 1024)
    row = pl.BlockSpec((tm, HY_WIDTH), lambda i: (i, 0))
    return pl.pallas_call(
        functools.partial(_hy_post_kernel, n_total=2 * t),
        out_shape=jax.ShapeDtypeStruct((t, HY_WIDTH), BF16),
        grid=(t // tm,),
        in_specs=[row, row, row, pl.BlockSpec((8, HY_WIDTH), lambda i: (0, 0)),
                  pl.BlockSpec((1, HY_WIDTH), lambda i: (0, 0))],
        out_specs=row,
        compiler_params=_cparams("parallel"),
        name="hyena_post",
    )(x0, z, conv, asum, bias.reshape(1, HY_WIDTH))


def _hyena_mixer(p_hy, conv_w, conv_b, filt, bias):
    seq_len = p_hy.shape[0]
    x0, z = _hyena_pre(p_hy, conv_w, conv_b)
    taps, asum = _hyena_taps(seq_len, *filt)
    if 2 * seq_len // FFT_N2 >= 16:
        conv = _hyena_long_conv(z, taps)
    else:
        conv = _hyena_small_conv(z, taps)
    return _hyena_post(x0, z, conv, asum, bias)


def _mixed_residual(hy_ref, ret_ref, att_ref, wo_ref, x_ref, mgate_ref):
    y = (_dot(hy_ref[...], wo_ref[:HY_WIDTH, :])
         + _dot(ret_ref[...], wo_ref[HY_WIDTH:HY_WIDTH + RET_WIDTH, :])
         + _dot(att_ref[...], wo_ref[HY_WIDTH + RET_WIDTH:, :]))
    return x_ref[...] + mgate_ref[...] * y


def _mixer_specs(tm, index):
    row = lambda w: pl.BlockSpec((tm, w), lambda i, *_: (i, 0))
    return [row(HY_WIDTH), row(RET_WIDTH), row(ATT_WIDTH),
            pl.BlockSpec((MIX_WIDTH, D_MODEL), index), row(D_MODEL),
            pl.BlockSpec((1, D_MODEL), index)]


def _swiglu(x, wg_ref, wu_ref, wd_ref):
    hidden = wg_ref.shape[1]
    y = None
    for lo in range(0, hidden, SWIGLU_CHUNK):
        hi = min(lo + SWIGLU_CHUNK, hidden)
        a = _silu(_dot(x, wg_ref[:, lo:hi])) * _dot(x, wu_ref[:, lo:hi])
        part = _dot(a.astype(BF16), wd_ref[lo:hi, :])
        y = part if y is None else y + part
    return y


def _ffn_kernel(hy_ref, ret_ref, att_ref, wo_ref, x_ref, mgate_ref, g_ref, sh_ref, sc_ref, gate_ref,
                wg_ref, wu_ref, wd_ref, o_ref):
    x = _mixed_residual(hy_ref, ret_ref, att_ref, wo_ref, x_ref, mgate_ref)
    h = _norm_mod(x, g_ref[...], sh_ref[...], sc_ref[...]).astype(BF16)
    o_ref[...] = x + gate_ref[...] * _swiglu(h, wg_ref, wu_ref, wd_ref)


def _dense_ffn(mixed, g, shift, scale, gate, wg, wu, wd):
    t = mixed[4].shape[0]
    tm = min(t, 512)
    index = lambda i: (0, 0)
    vec = pl.BlockSpec((1, D_MODEL), index)
    resident = lambda shape: pl.BlockSpec(shape, index, pipeline_mode=pl.Buffered(1))
    return pl.pallas_call(
        _ffn_kernel,
        out_shape=jax.ShapeDtypeStruct((t, D_MODEL), F32),
        grid=(t // tm,),
        in_specs=_mixer_specs(tm, index) + [vec, vec, vec, vec,
                  resident((D_MODEL, FFN_HIDDEN)), resident((D_MODEL, FFN_HIDDEN)),
                  resident((FFN_HIDDEN, D_MODEL))],
        out_specs=pl.BlockSpec((tm, D_MODEL), lambda i: (i, 0)),
        compiler_params=_cparams("parallel"),
        name="dense_ffn",
    )(*mixed, g, shift, scale, gate, wg, wu, wd)


def _moe_route_kernel(hy_ref, ret_ref, att_ref, wo_ref, x_ref, mgate_ref, g_ref, sh_ref, sc_ref,
                      rw_ref, rb_ref, xo_ref, h_ref, info_ref):
    x = _mixed_residual(hy_ref, ret_ref, att_ref, wo_ref, x_ref, mgate_ref)
    xo_ref[...] = x
    h = _norm_mod(x, g_ref[...], sh_ref[...], sc_ref[...])
    _to_row_tiles(h_ref, h)
    logits = _dot3(h, rw_ref[...]) + rb_ref[...]
    lane = lax.broadcasted_iota(jnp.int32, logits.shape, 1)
    v1 = jnp.max(logits, axis=-1, keepdims=True)
    i1 = jnp.min(jnp.where(logits == v1, lane, LANES), axis=-1, keepdims=True)
    rest = jnp.where(lane == i1, -jnp.inf, logits)
    v2 = jnp.max(rest, axis=-1, keepdims=True)
    i2 = jnp.min(jnp.where(rest == v2, lane, LANES), axis=-1, keepdims=True)
    e = jnp.exp(v2 - v1)
    g1 = 1.0 / (1.0 + e)
    g2 = e * g1
    info_ref[...] = jnp.where(lane == 0, i1.astype(F32),
                              jnp.where(lane == 1, i2.astype(F32),
                                        jnp.where(lane == 2, g1, jnp.where(lane == 3, g2, 0.0))))


def _moe_route(mixed, g, shift, scale, router_w, router_b):
    t = mixed[4].shape[0]
    tm = min(t, 512)
    rw = jnp.concatenate([router_w, jnp.zeros((D_MODEL, LANES - N_EXPERTS), F32)], axis=1)
    rb = jnp.concatenate([router_b, jnp.full((LANES - N_EXPERTS,), -jnp.inf, F32)]).reshape(1, LANES)
    vec = pl.BlockSpec((1, D_MODEL), lambda i: (0, 0))
    return pl.pallas_call(
        _moe_route_kernel,
        out_shape=(jax.ShapeDtypeStruct((t, D_MODEL), F32),
                   jax.ShapeDtypeStruct((t * SUBLANES, LANES), F32),
                   jax.ShapeDtypeStruct((t, LANES), F32)),
        grid=(t // tm,),
        in_specs=_mixer_specs(tm, lambda i: (0, 0)) + [vec, vec, vec,
                  pl.BlockSpec((D_MODEL, LANES), lambda i: (0, 0)),
                  pl.BlockSpec((1, LANES), lambda i: (0, 0))],
        out_specs=(pl.BlockSpec((tm, D_MODEL), lambda i: (i, 0)),
                   pl.BlockSpec((tm * SUBLANES, LANES), lambda i: (i, 0)),
                   pl.BlockSpec((tm, LANES), lambda i: (i, 0))),
        compiler_params=_cparams("parallel"),
        name="moe_route",
    )(*mixed, g, shift, scale, rw, rb)


def _to_row_tiles(ref, x):
    rows = x.shape[0]
    for k in range(D_MODEL // LANES):
        ref[pl.ds(k, rows, stride=SUBLANES), :] = x[:, k * LANES:(k + 1) * LANES]


def _from_row_tiles(ref, rows):
    return jnp.concatenate(
        [ref[pl.ds(k, rows, stride=SUBLANES), :] for k in range(D_MODEL // LANES)], axis=1)


def _row_copy(src_hbm, dst_vmem, sem, src_row, dst_row):
    src = pl.multiple_of(src_row * SUBLANES, SUBLANES)
    dst = pl.multiple_of(dst_row * SUBLANES, SUBLANES)
    return pltpu.make_async_copy(src_hbm.at[pl.ds(src, SUBLANES)], dst_vmem.at[pl.ds(dst, SUBLANES)], sem)


def _moe_expert_kernel(blk_e_ref, nused_ref, dest_ref, fill_ref, h_hbm, wg_ref, wu_ref, wd_ref,
                       y_ref, xbuf_ref, src_ref, sem):
    b = pl.program_id(0)
    n_used = nused_ref[0]

    @pl.when(b == 0)
    def _():
        def invert(t, carry):
            for j in range(TOP_K):
                src_ref[dest_ref[TOP_K * t + j]] = t
            return carry

        lax.fori_loop(0, dest_ref.shape[0] // TOP_K, invert, 0, unroll=DMA_UNROLL)
        for e in range(N_EXPERTS):
            def pad(d, carry):
                src_ref[d] = 0
                return carry

            lax.fori_loop(fill_ref[0, e], fill_ref[1, e], pad, 0)

    def issue_rows(blk, slot):
        def body(r, carry):
            _row_copy(h_hbm, xbuf_ref.at[slot], sem.at[slot], src_ref[blk * MOE_BLOCK + r], r).start()
            return carry

        lax.fori_loop(0, MOE_BLOCK, body, 0, unroll=DMA_UNROLL)

    def wait_rows(slot):
        def body(r, carry):
            _row_copy(h_hbm, xbuf_ref.at[slot], sem.at[slot], 0, r).wait()
            return carry

        lax.fori_loop(0, MOE_BLOCK, body, 0, unroll=DMA_UNROLL)

    @pl.when(b < n_used)
    def _():
        slot = b % 2

        @pl.when(b == 0)
        def _():
            issue_rows(0, 0)

        wait_rows(slot)

        @pl.when(b + 1 < n_used)
        def _():
            issue_rows(b + 1, 1 - slot)

        x = _from_row_tiles(xbuf_ref.at[slot], MOE_BLOCK).astype(BF16)
        _to_row_tiles(y_ref, _swiglu(x, wg_ref.at[0], wu_ref.at[0], wd_ref.at[0]))

    @pl.when(b >= n_used)
    def _():
        y_ref[...] = jnp.zeros_like(y_ref)


def _moe_experts(h, blk_expert, n_used, dest, fill, wg, wu, wd, n_blk):
    def w_index(b, be, nu, de, fi):
        return be[jnp.minimum(b, nu[0] - 1)], 0, 0

    grid_spec = pltpu.PrefetchScalarGridSpec(
        num_scalar_prefetch=4,
        grid=(n_blk,),
        in_specs=[pl.BlockSpec(memory_space=pl.ANY),
                  pl.BlockSpec((1, D_MODEL, FFN_HIDDEN), w_index),
                  pl.BlockSpec((1, D_MODEL, FFN_HIDDEN), w_index),
                  pl.BlockSpec((1, FFN_HIDDEN, D_MODEL), w_index)],
        out_specs=pl.BlockSpec((MOE_BLOCK * SUBLANES, LANES), lambda b, be, nu, de, fi: (b, 0)),
        scratch_shapes=[pltpu.VMEM((2, MOE_BLOCK * SUBLANES, LANES), F32),
                        pltpu.SMEM((n_blk * MOE_BLOCK,), jnp.int32),
                        pltpu.SemaphoreType.DMA((2,))],
    )
    return pl.pallas_call(
        _moe_expert_kernel,
        out_shape=jax.ShapeDtypeStruct((n_blk * MOE_BLOCK * SUBLANES, LANES), F32),
        grid_spec=grid_spec,
        compiler_params=_cparams("arbitrary"),
        name="moe_experts",
    )(blk_expert, n_used, dest, fill, h, wg, wu, wd)


def _moe_combine_kernel(dest_ref, x_ref, info_ref, gate_ref, ng_ref, y_hbm, o_ref,
                        y1_ref, y2_ref, sem, *, final_norm):
    i = pl.program_id(0)
    tm = x_ref.shape[0]
    slot = i % 2

    def issue_rows(blk, slot):
        def body(r, carry):
            a = 2 * (blk * tm + r)
            _row_copy(y_hbm, y1_ref.at[slot], sem.at[slot], dest_ref[a], r).start()
            _row_copy(y_hbm, y2_ref.at[slot], sem.at[slot], dest_ref[a + 1], r).start()
            return carry

        lax.fori_loop(0, tm, body, 0, unroll=DMA_UNROLL)

    def wait_rows(slot):
        def body(r, carry):
            _row_copy(y_hbm, y1_ref.at[slot], sem.at[slot], 0, r).wait()
            _row_copy(y_hbm, y2_ref.at[slot], sem.at[slot], 0, r).wait()
            return carry

        lax.fori_loop(0, tm, body, 0, unroll=DMA_UNROLL)

    @pl.when(i == 0)
    def _():
        issue_rows(0, 0)

    wait_rows(slot)

    @pl.when(i + 1 < pl.num_programs(0))
    def _():
        issue_rows(i + 1, 1 - slot)

    info = info_ref[...]
    y = (_from_row_tiles(y1_ref.at[slot], tm) * info[:, 2:3]
         + _from_row_tiles(y2_ref.at[slot], tm) * info[:, 3:4])
    x = x_ref[...] + gate_ref[...] * y
    if final_norm:
        ms = jnp.mean(x * x, axis=-1, keepdims=True)
        x = x * lax.rsqrt(ms + EPS) * ng_ref[...]
    o_ref[...] = x


def _moe_combine(dest, x, info, gate, norm_g, y, final_norm):
    t = x.shape[0]
    tm = min(t, 256)
    vec = pl.BlockSpec((1, D_MODEL), lambda i, d: (0, 0))
    grid_spec = pltpu.PrefetchScalarGridSpec(
        num_scalar_prefetch=1,
        grid=(t // tm,),
        in_specs=[pl.BlockSpec((tm, D_MODEL), lambda i, d: (i, 0)),
                  pl.BlockSpec((tm, LANES), lambda i, d: (i, 0)), vec, vec,
                  pl.BlockSpec(memory_space=pl.ANY)],
        out_specs=pl.BlockSpec((tm, D_MODEL), lambda i, d: (i, 0)),
        scratch_shapes=[pltpu.VMEM((2, tm * SUBLANES, LANES), F32),
                        pltpu.VMEM((2, tm * SUBLANES, LANES), F32),
                        pltpu.SemaphoreType.DMA((2,))],
    )
    return pl.pallas_call(
        functools.partial(_moe_combine_kernel, final_norm=final_norm),
        out_shape=jax.ShapeDtypeStruct((t, D_MODEL), F32),
        grid_spec=grid_spec,
        compiler_params=_cparams("arbitrary"),
        name="moe_combine",
    )(dest, x, info, gate, norm_g, y)


def _moe_layer(mixed, g, shift, scale, gate, router_w, router_b, wg, wu, wd, norm_g, final_norm):
    t = mixed[4].shape[0]
    n_asg = t * TOP_K
    x, h, info = _moe_route(mixed, g, shift, scale, router_w, router_b)
    expert = info[:, :TOP_K].astype(jnp.int32).reshape(-1)
    onehot = (expert[:, None] == jnp.arange(N_EXPERTS, dtype=jnp.int32)[None, :]).astype(jnp.int32)
    csum = jnp.cumsum(onehot, axis=0)
    counts = csum[-1]
    padded = (counts + MOE_BLOCK - 1) // MOE_BLOCK * MOE_BLOCK
    pad_end = jnp.cumsum(padded)
    pad_start = pad_end - padded
    dest = jnp.sum(onehot * (csum - 1 + pad_start[None, :]), axis=1).astype(jnp.int32)
    n_blk = -(-n_asg // MOE_BLOCK) + N_EXPERTS
    blk_start = jnp.arange(n_blk, dtype=jnp.int32) * MOE_BLOCK
    blk_expert = jnp.minimum(jnp.sum(blk_start[:, None] >= pad_end[None, :], axis=1),
                             N_EXPERTS - 1).astype(jnp.int32)
    n_used = (pad_end[-1:] // MOE_BLOCK).astype(jnp.int32)
    fill = jnp.stack([pad_start + counts, pad_end]).astype(jnp.int32)
    y = _moe_experts(h, blk_expert, n_used, dest, fill, wg, wu, wd, n_blk)
    return _moe_combine(dest, x, info, gate, norm_g, y, final_norm)


def _final_norm_kernel(x_ref, g_ref, o_ref):
    x = x_ref[...]
    ms = jnp.mean(x * x, axis=-1, keepdims=True)
    o_ref[...] = x * lax.rsqrt(ms + EPS) * g_ref[...]


def _final_norm(x, g):
    t = x.shape[0]
    tm = min(t, 1024)
    return pl.pallas_call(
        _final_norm_kernel,
        out_shape=jax.ShapeDtypeStruct((t, D_MODEL), F32),
        grid=(t // tm,),
        in_specs=[pl.BlockSpec((tm, D_MODEL), lambda i: (i, 0)),
                  pl.BlockSpec((1, D_MODEL), lambda i: (0, 0))],
        out_specs=pl.BlockSpec((tm, D_MODEL), lambda i: (i, 0)),
        compiler_params=_cparams("parallel"),
        name="final_norm",
    )(x, g)


def kernel(x, c, ctx, c_ctx, ada_w, ada_b, norm1_g, norm2_g, w_in, w_out, hy_conv_w, hy_conv_b, hy_filt_w1, hy_filt_b1, hy_filt_w2, hy_filt_b2, hy_filt_w3, hy_filt_b3, hy_filt_freq, hy_bias, ret_log_rate, attn_q_g, attn_k_g, ffn_w_gate, ffn_w_up, ffn_w_down, moe_router_w, moe_router_b, moe_w_gate, moe_w_up, moe_w_down, final_norm_g):
    assert x.shape[0] == 1 and c.shape[0] == 1
    seq_len = x.shape[1]
    x_lat = x[0]
    x_ctx = ctx[0]
    cos_t, sin_t = _rope_tables(seq_len)
    cvec = jnp.concatenate([c, c_ctx[None, :], jnp.zeros((6, D_MODEL), F32)], axis=0)
    cvec = _silu(cvec)
    row = lambda v: v.reshape(1, D_MODEL)
    zero_state = jnp.zeros((2, RET_WIDTH, RET_HEAD_DIM), F32)
    zero_tab = jnp.zeros((x_ctx.shape[0], LANES), F32)

    for l in range(DEPTH):
        last = l == DEPTH - 1
        mods = _ada_modulation(cvec, ada_w, ada_b, l).reshape(8, 6, D_MODEL)
        mod = [row(mods[0, i]) for i in range(6)]
        mod_c = [row(mods[1, i]) for i in range(6)]
        filt = (hy_filt_w1[l], hy_filt_b1[l], hy_filt_w2[l], hy_filt_b2[l],
                hy_filt_w3[l], hy_filt_b3[l], hy_filt_freq[l])
        lg_f = -jnp.exp(ret_log_rate[l, 0].astype(F32))
        lg_b = -jnp.exp(ret_log_rate[l, 1].astype(F32))
        mask, zeta, xi, chunk_decay = _ret_tables(lg_f, lg_b)
        w_in_l = w_in[l].astype(BF16)
        w_out_l = w_out[l].astype(BF16)
        n1 = row(norm1_g[l])
        n2 = row(norm2_g[l])

        p_hy, p_ret, qt, k_aug, vt, kmax_sq = _in_projection(
            x_lat, n1, mod[0], mod[1], w_in_l, cos_t, sin_t, attn_q_g[l], attn_k_g[l], rope=True)
        pc_hy, pc_ret, qct, kc, vct, _ = _in_projection(
            x_ctx, n1, mod_c[0], mod_c[1], w_in_l, zero_tab, zero_tab, attn_q_g[l], attn_k_g[l],
            rope=False)

        sprev_c, s_ctx = _retention_states(pc_ret, zeta, chunk_decay, zero_state)

        y_hy = _hyena_mixer(p_hy, hy_conv_w[l], hy_conv_b[l], filt, hy_bias[l])
        sprev, _ = _retention_states(p_ret, zeta, chunk_decay, s_ctx)
        y_ret = _retention_outputs(p_ret, mask, xi, sprev)
        kmax = jnp.sqrt(jnp.max(kmax_sq[:, 0, ::ATT_HEAD_DIM], axis=0)) * KMAX_SLACK
        y_att = _attention(qt, kc, vct, kmax, k_aug, vt, tq=ATT_TQ, tk=ATT_TK, sub=ATT_SUB)

        if not last:
            yc_hy = _hyena_mixer(pc_hy, hy_conv_w[l], hy_conv_b[l], filt, hy_bias[l])
            yc_ret = _retention_outputs(pc_ret, mask, xi, sprev_c)
            yc_att = _attention(qct, kc, vct, tq=x_ctx.shape[0])
            mixed_c = (yc_hy, yc_ret, yc_att, w_out_l, x_ctx, mod_c[2])
        mixed = (y_hy, y_ret, y_att, w_out_l, x_lat, mod[2])

        i = l // 2
        if l % 2 == 0:
            wg, wu, wd = (ffn_w_gate[i].astype(BF16), ffn_w_up[i].astype(BF16),
                          ffn_w_down[i].astype(BF16))
            x_lat = _dense_ffn(mixed, n2, mod[3], mod[4], mod[5], wg, wu, wd)
            if not last:
                x_ctx = _dense_ffn(mixed_c, n2, mod_c[3], mod_c[4], mod_c[5], wg, wu, wd)
        else:
            wg, wu, wd = (moe_w_gate[i].astype(BF16), moe_w_up[i].astype(BF16),
                          moe_w_down[i].astype(BF16))
            fg = row(final_norm_g)
            x_lat = _moe_layer(mixed, n2, mod[3], mod[4], mod[5], moe_router_w[i], moe_router_b[i],
                               wg, wu, wd, fg, final_norm=last)
            if not last:
                x_ctx = _moe_layer(mixed_c, n2, mod_c[3], mod_c[4], mod_c[5], moe_router_w[i],
                                   moe_router_b[i], wg, wu, wd, fg, final_norm=False)
    if DEPTH % 2 == 1:
        x_lat = _final_norm(x_lat, row(final_norm_g))
    return x_lat[None]
```

```python
import functools
import math

import numpy as np
import jax
import jax.numpy as jnp
from jax import lax
from jax.experimental import pallas as pl
from jax.experimental.pallas import tpu as pltpu

F32 = jnp.float32
BF16 = jnp.bfloat16

D_MODEL = 1024
DEPTH = 2
GRID_W = 64
EPS = 1e-6

HY_WIDTH = 256
HY_COLS = 3 * HY_WIDTH
HY_POS_BANDS = 16
HY_DECAY_TARGET = 1e-2
HY_SHORT_DECAY_PCT = 0.3
HY_LONG_DECAY_PCT = 1.5

RET_HEAD_DIM = 64
RET_HEADS = 4
RET_WIDTH = RET_HEADS * RET_HEAD_DIM
RET_COLS = 4 * RET_WIDTH
RET_CHUNK = 128

ATT_HEAD_DIM = 64
ATT_HEADS = 8
ATT_KV_HEADS = 2
ATT_GROUP = ATT_HEADS // ATT_KV_HEADS
ATT_WIDTH = ATT_HEADS * ATT_HEAD_DIM
ATT_KV_WIDTH = ATT_KV_HEADS * ATT_HEAD_DIM
ATT_COLS = ATT_WIDTH + 2 * ATT_KV_WIDTH
ROPE_AXIS_DIM = ATT_HEAD_DIM // 2
ROPE_BASE = 10000.0

MIX_WIDTH = HY_WIDTH + RET_WIDTH + ATT_WIDTH
IN_COLS = HY_COLS + RET_COLS + ATT_COLS

FFN_HIDDEN = 2816
N_EXPERTS = 8
TOP_K = 2
MOE_BLOCK = 512

LOG2_E = 1.4426950408889634
EXP2_CAP = 60.0
KMAX_SLACK = 1.0 + 2.0 ** -7
ATT_TQ = 512
ATT_TK = 4096
ATT_SUB = 512
ATT_VROWS = 80
RET_STATE_CHUNKS = 16
RET_OUT_CHUNKS = 8
SWIGLU_CHUNK = 1024
DMA_UNROLL = 8

LANES = 128
SUBLANES = 8
VMEM_LIMIT = 56 * 1024 * 1024


def _cparams(*sem):
    return pltpu.CompilerParams(dimension_semantics=sem, vmem_limit_bytes=VMEM_LIMIT)


def _dot(a, b):
    return jnp.dot(a, b, preferred_element_type=F32)


def _split(a):
    hi = a.astype(BF16)
    lo = (a - hi.astype(F32)).astype(BF16)
    return hi, lo


def _dot3(a, b):
    ah, al = _split(a)
    bh, bl = _split(b)
    return _dot(ah, bh) + _dot(al, bh) + _dot(ah, bl)


def _silu(x):
    return x * (1.0 / (1.0 + jnp.exp(-x)))


def _norm_mod(x, g, shift, scale):
    ms = jnp.mean(x * x, axis=-1, keepdims=True)
    h = x * lax.rsqrt(ms + EPS) * g
    return h * (1.0 + scale) + shift


def _ada_kernel(c_ref, w_ref, b_ref, o_ref):
    o_ref[...] = _dot3(c_ref[...], w_ref[0]) + b_ref[0]


def _ada_modulation(cc, w, b, layer):
    n = w.shape[2]
    tn = 1536
    return pl.pallas_call(
        _ada_kernel,
        out_shape=jax.ShapeDtypeStruct((8, n), F32),
        grid=(n // tn,),
        in_specs=[pl.BlockSpec((8, D_MODEL), lambda j: (0, 0)),
                  pl.BlockSpec((1, D_MODEL, tn), lambda j: (layer, 0, j)),
                  pl.BlockSpec((1, 1, tn), lambda j: (layer, 0, j))],
        out_specs=pl.BlockSpec((8, tn), lambda j: (0, j)),
        compiler_params=_cparams("parallel"),
        name="ada_modulation",
    )(cc, w, b.reshape(b.shape[0], 1, n))


def _head_mean_matrix(width):
    idx = np.arange(width) // ATT_HEAD_DIM
    return jnp.asarray((idx[:, None] == idx[None, :]).astype(np.float32) / ATT_HEAD_DIM, BF16)


def _head_rms(x, bd, g):
    ms = _dot((x * x).astype(BF16), bd)
    return x * lax.rsqrt(ms + EPS) * g


def _rope(x, cos, sin):
    n = x.shape[1]
    lane = lax.broadcasted_iota(jnp.int32, x.shape, 1)
    swapped = jnp.where((lane % 32) < 16, pltpu.roll(x, n - 16, 1), pltpu.roll(x, 16, 1))
    return x * cos + swapped * sin


def _attention_operands(p, cos_ref, sin_ref, qg_ref, kg_ref, bdq_ref, bdk_ref,
                        qt_ref, k_ref, vt_ref, kmax_ref, rope):
    q = _head_rms(p[:, :ATT_WIDTH], bdq_ref[...], qg_ref[...])
    k = _head_rms(p[:, ATT_WIDTH:ATT_WIDTH + ATT_KV_WIDTH], bdk_ref[...], kg_ref[...])
    if rope:
        cos = cos_ref[...]
        sin = sin_ref[...]
        q = jnp.concatenate(
            [_rope(q[:, j * LANES:(j + 1) * LANES], cos, sin) for j in range(ATT_WIDTH // LANES)],
            axis=1)
        k = _rope(k, cos, sin)
    qt_ref[...] = (q * (ATT_HEAD_DIM ** -0.5 * LOG2_E)).T.astype(BF16)
    tm = k.shape[0]
    lane = lax.broadcasted_iota(jnp.int32, k.shape, 1)
    minus_one_col = jnp.where(lane == ATT_HEAD_DIM, -1.0, 0.0)
    kb = k.astype(BF16)
    k_ref[0] = jnp.where(lane < ATT_HEAD_DIM, k, minus_one_col).astype(BF16)
    k_ref[1] = jnp.where(lane < ATT_HEAD_DIM, pltpu.roll(k, ATT_HEAD_DIM, 1), minus_one_col).astype(BF16)
    norm_sq = _dot((kb.astype(F32) ** 2).astype(BF16), bdk_ref[...]) * ATT_HEAD_DIM
    kmax_ref[0] = jnp.broadcast_to(jnp.max(norm_sq, axis=0, keepdims=True), (SUBLANES, LANES))
    vt = p[:, ATT_WIDTH + ATT_KV_WIDTH:].T.astype(BF16)
    sub = lax.broadcasted_iota(jnp.int32, (ATT_VROWS - ATT_HEAD_DIM, tm), 0)
    ones_row = jnp.where(sub == 0, 1.0, 0.0).astype(BF16)
    for h in range(ATT_KV_HEADS):
        vt_ref[h * ATT_VROWS:h * ATT_VROWS + ATT_HEAD_DIM, :] = vt[h * ATT_HEAD_DIM:(h + 1) * ATT_HEAD_DIM]
        vt_ref[h * ATT_VROWS + ATT_HEAD_DIM:(h + 1) * ATT_VROWS, :] = ones_row


def _short_conv_gate(p, prev_row, next_row, w, b, x0_ref, z_ref):
    tm = p.shape[0]
    row = lax.broadcasted_iota(jnp.int32, p.shape, 0)
    up = jnp.where(row == 0, prev_row, pltpu.roll(p, 1, 0))
    dn = jnp.where(row == tm - 1, next_row, pltpu.roll(p, tm - 1, 0))
    u = up * w[0:1] + p * w[1:2] + dn * w[2:3] + b
    x0_ref[...] = u[:, :HY_WIDTH]
    z_ref[...] = u[:, HY_WIDTH:2 * HY_WIDTH] * u[:, 2 * HY_WIDTH:]


def _inproj_kernel(x_ref, xprev_ref, xnext_ref, g_ref, sh_ref, sc_ref, w_ref, cw_ref, cb_ref,
                   cos_ref, sin_ref, qg_ref, kg_ref, bdq_ref, bdk_ref,
                   x0_ref, z_ref, ret_ref, qt_ref, k_ref, vt_ref, kmax_ref, *, rope):
    i = pl.program_id(0)
    norm = lambda v: _norm_mod(v, g_ref[...], sh_ref[...], sc_ref[...]).astype(BF16)
    p = _dot(norm(x_ref[...]), w_ref[...])
    w_hy = w_ref[:, :HY_COLS]
    prev_row = jnp.where(i == 0, 0.0, _dot(norm(xprev_ref[...]), w_hy)[SUBLANES - 1:SUBLANES, :])
    next_row = jnp.where(i == pl.num_programs(0) - 1, 0.0, _dot(norm(xnext_ref[...]), w_hy)[0:1, :])
    _short_conv_gate(p[:, :HY_COLS], prev_row, next_row, cw_ref[...], cb_ref[...], x0_ref, z_ref)
    ret_ref[...] = p[:, HY_COLS:HY_COLS + RET_COLS]
    _attention_operands(p[:, HY_COLS + RET_COLS:], cos_ref, sin_ref, qg_ref, kg_ref, bdq_ref, bdk_ref,
                        qt_ref, k_ref, vt_ref, kmax_ref, rope)


def _in_projection(x, g, shift, scale, w_bf16, conv_w, conv_b, cos, sin, q_g, k_g, rope):
    t = x.shape[0]
    tm = min(t, 512)
    groups = tm // SUBLANES
    last_group = t // SUBLANES - 1
    qg = jnp.tile(q_g, ATT_HEADS).reshape(1, ATT_WIDTH)
    kg = jnp.tile(k_g, ATT_KV_HEADS).reshape(1, ATT_KV_WIDTH)
    vec = pl.BlockSpec((1, D_MODEL), lambda i: (0, 0))
    const = lambda shape: pl.BlockSpec(shape, lambda i: (0, 0))
    return pl.pallas_call(
        functools.partial(_inproj_kernel, rope=rope),
        out_shape=(jax.ShapeDtypeStruct((t, HY_WIDTH), F32),
                   jax.ShapeDtypeStruct((t, HY_WIDTH), F32),
                   jax.ShapeDtypeStruct((t, RET_COLS), F32),
                   jax.ShapeDtypeStruct((ATT_WIDTH, t), BF16),
                   jax.ShapeDtypeStruct((ATT_KV_HEADS, t, LANES), BF16),
                   jax.ShapeDtypeStruct((ATT_KV_HEADS * ATT_VROWS, t), BF16),
                   jax.ShapeDtypeStruct((t // tm, SUBLANES, LANES), F32)),
        grid=(t // tm,),
        in_specs=[pl.BlockSpec((tm, D_MODEL), lambda i: (i, 0)),
                  pl.BlockSpec((SUBLANES, D_MODEL), lambda i: (jnp.maximum(i * groups - 1, 0), 0)),
                  pl.BlockSpec((SUBLANES, D_MODEL),
                               lambda i: (jnp.minimum((i + 1) * groups, last_group), 0)),
                  vec, vec, vec, const((D_MODEL, IN_COLS)),
                  const((3, HY_COLS)), const((1, HY_COLS)),
                  pl.BlockSpec((tm, LANES), lambda i: (i, 0)),
                  pl.BlockSpec((tm, LANES), lambda i: (i, 0)),
                  const((1, ATT_WIDTH)), const((1, ATT_KV_WIDTH)),
                  const((ATT_WIDTH, ATT_WIDTH)), const((ATT_KV_WIDTH, ATT_KV_WIDTH))],
        out_specs=(pl.BlockSpec((tm, HY_WIDTH), lambda i: (i, 0)),
                   pl.BlockSpec((tm, HY_WIDTH), lambda i: (i, 0)),
                   pl.BlockSpec((tm, RET_COLS), lambda i: (i, 0)),
                   pl.BlockSpec((ATT_WIDTH, tm), lambda i: (0, i)),
                   pl.BlockSpec((ATT_KV_HEADS, tm, LANES), lambda i: (0, i, 0)),
                   pl.BlockSpec((ATT_KV_HEADS * ATT_VROWS, tm), lambda i: (0, i)),
                   pl.BlockSpec((1, SUBLANES, LANES), lambda i: (i, 0, 0))),
        compiler_params=_cparams("parallel"),
        name="in_projection",
    )(x, x, x, g, shift, scale, w_bf16, conv_w, conv_b.reshape(1, HY_COLS), cos, sin, qg, kg,
      _head_mean_matrix(ATT_WIDTH), _head_mean_matrix(ATT_KV_WIDTH))


def _rope_tables(n_tokens):
    rows = n_tokens // GRID_W
    inv_freq = ROPE_BASE ** (-jnp.arange(0, ROPE_AXIS_DIM, 2, dtype=F32) / ROPE_AXIS_DIM)
    ang_r = jnp.arange(rows, dtype=F32)[:, None] * inv_freq[None, :]
    ang_c = jnp.arange(GRID_W, dtype=F32)[:, None] * inv_freq[None, :]
    nf = inv_freq.shape[0]
    by_row = lambda tab: jnp.broadcast_to(tab[:, None, :], (rows, GRID_W, nf)).reshape(n_tokens, nf)
    by_col = lambda tab: jnp.broadcast_to(tab[None, :, :], (rows, GRID_W, nf)).reshape(n_tokens, nf)
    cr, sr = by_row(jnp.cos(ang_r)), by_row(jnp.sin(ang_r))
    cc, sc = by_col(jnp.cos(ang_c)), by_col(jnp.sin(ang_c))
    cos = jnp.concatenate([cr, cr, cc, cc], axis=1)
    sin = jnp.concatenate([-sr, sr, -sc, sc], axis=1)
    return jnp.tile(cos, (1, 2)), jnp.tile(sin, (1, 2))


def _attention_kernel(*refs, tq, tk, sub, n_lat):
    if n_lat:
        kmax_ref, qt_ref, kc_ref, vct_ref, k_ref, vt_ref, o_ref, qs_ref, acc_ref, m_ref = refs
    else:
        qt_ref, kc_ref, vct_ref, o_ref, acc_ref = refs
    cols = ATT_GROUP * tq
    qt = qt_ref[...]
    qst = jnp.concatenate(
        [qt[g * ATT_HEAD_DIM:(g + 1) * ATT_HEAD_DIM, :] for g in range(ATT_GROUP)], axis=1)
    pad = jnp.zeros((LANES - ATT_HEAD_DIM, cols), BF16)
    s = _dot(kc_ref[0], jnp.concatenate([qst, pad], axis=0))
    mt = jnp.max(s, axis=0, keepdims=True).astype(BF16)
    mtf = mt.astype(F32)
    acc_ref[...] = _dot(vct_ref[...], jnp.exp2(s - mtf).astype(BF16))

    if n_lat:
        row = lax.broadcasted_iota(jnp.int32, pad.shape, 0)
        qs_ref[...] = jnp.concatenate([qst, jnp.where(row == 0, mtf, 0.0).astype(BF16)], axis=0)
        qf = qst.astype(F32)
        qn = jnp.sqrt(jnp.sum(qf * qf, axis=0, keepdims=True))
        fast = jnp.max(qn * kmax_ref[pl.program_id(0)] - mtf) <= EXP2_CAP

        def tile(j):
            off = pl.multiple_of(j * tk, tk)
            return k_ref[0, pl.ds(off, tk), :], vt_ref[:, pl.ds(off, tk)]

        @pl.when(fast)
        def _():
            def body(j, carry):
                k, vt = tile(j)
                qs = qs_ref[...]
                part = None
                for c in range(tk // sub):
                    p = jnp.exp2(_dot(k[c * sub:(c + 1) * sub], qs)).astype(BF16)
                    pv = _dot(vt[:, c * sub:(c + 1) * sub], p)
                    part = pv if part is None else part + pv
                acc_ref[...] += part
                return carry

            lax.fori_loop(0, n_lat, body, 0)

        @pl.when(jnp.logical_not(fast))
        def _():
            m_ref[...] = jnp.zeros_like(m_ref)

            def body(j, carry):
                k, vt = tile(j)
                s = _dot(k, qs_ref[...])
                m_old = m_ref[...]
                m_new = jnp.maximum(m_old, jnp.max(s, axis=0, keepdims=True))
                p = jnp.exp2(s - m_new).astype(BF16)
                acc_ref[...] = jnp.exp2(m_old - m_new) * acc_ref[...] + _dot(vt, p)
                m_ref[...] = m_new
                return carry

            lax.fori_loop(0, n_lat, body, 0)

    acc = acc_ref[...]
    o = acc[:ATT_HEAD_DIM] / acc[ATT_HEAD_DIM:ATT_HEAD_DIM + 1]
    o_ref[...] = jnp.concatenate(
        [o[:, g * tq:(g + 1) * tq].T for g in range(ATT_GROUP)], axis=1).astype(o_ref.dtype)


def _attention(qt, kc, vct, kmax=None, k=None, vt=None, *, tq, tk=0, sub=0):
    t = qt.shape[1]
    lc = kc.shape[1]
    gw = ATT_GROUP * ATT_HEAD_DIM
    cols = ATT_GROUP * tq
    in_specs = [pl.BlockSpec((gw, tq), lambda h, i: (h, i)),
                pl.BlockSpec((1, lc, LANES), lambda h, i: (h, 0, 0)),
                pl.BlockSpec((ATT_VROWS, lc), lambda h, i: (h, 0))]
    args = [qt, kc, vct]
    scratch = [pltpu.VMEM((ATT_VROWS, cols), F32)]
    n_lat = 0
    if k is not None:
        lk = k.shape[1]
        n_lat = lk // tk
        in_specs = ([pl.BlockSpec(memory_space=pltpu.SMEM)] + in_specs
                    + [pl.BlockSpec((1, lk, LANES), lambda h, i: (h, 0, 0)),
                       pl.BlockSpec((ATT_VROWS, lk), lambda h, i: (h, 0))])
        args = [kmax] + args + [k, vt]
        scratch = [pltpu.VMEM((LANES, cols), BF16)] + scratch + [pltpu.VMEM((1, cols), F32)]
    return pl.pallas_call(
        functools.partial(_attention_kernel, tq=tq, tk=tk, sub=sub, n_lat=n_lat),
        out_shape=jax.ShapeDtypeStruct((t, ATT_WIDTH), BF16),
        grid=(ATT_KV_HEADS, t // tq),
        in_specs=in_specs,
        out_specs=pl.BlockSpec((tq, gw), lambda h, i: (i, h)),
        scratch_shapes=scratch,
        compiler_params=_cparams("parallel", "parallel"),
        name="attention_lat" if n_lat else "attention_ctx",
    )(*args)


def _ret_tables(lg_f, lg_b):
    pos = jnp.arange(RET_CHUNK, dtype=F32)
    rel = pos[:, None] - pos[None, :]
    mask = jnp.where(rel > 0, jnp.exp(lg_f[:, None, None] * jnp.maximum(rel, 0.0)),
                     jnp.where(rel < 0, jnp.exp(lg_b[:, None, None] * jnp.maximum(-rel, 0.0)), 2.0))
    zeta = jnp.stack([jnp.exp(lg_f[:, None] * (RET_CHUNK - 1 - pos)[None, :]),
                      jnp.exp(lg_b[:, None] * pos[None, :])])
    xi = jnp.stack([jnp.exp(lg_f[:, None] * (pos + 1)[None, :]),
                    jnp.exp(lg_b[:, None] * (RET_CHUNK - pos)[None, :])])
    chunk_decay = jnp.stack([jnp.exp(lg_f * RET_CHUNK), jnp.exp(lg_b * RET_CHUNK)])
    zeta = jnp.repeat(zeta, RET_HEAD_DIM, axis=1).transpose(0, 2, 1)
    xi = jnp.repeat(xi, RET_HEAD_DIM, axis=1).transpose(0, 2, 1)
    chunk_decay = jnp.repeat(chunk_decay, RET_HEAD_DIM, axis=1)[:, :, None]
    return mask, zeta, xi, chunk_decay


def _ret_state_kernel(k_ref, v_ref, zeta_ref, cd_ref, s0_ref, sprev_ref, sfin_ref, s_ref, *, cps):
    d = pl.program_id(0)

    @pl.when(pl.program_id(1) == 0)
    def _():
        s_ref[...] = s0_ref[0]

    zeta = zeta_ref[0] * (RET_HEAD_DIM ** -0.5)
    cd = cd_ref[0]
    us = []
    for c in range(cps):
        rows = slice(c * RET_CHUNK, (c + 1) * RET_CHUNK)
        kz = (k_ref[rows, :] * zeta).astype(BF16)
        v = v_ref[rows, :].astype(BF16)
        heads = []
        for h in range(RET_HEADS):
            sl = slice(h * RET_HEAD_DIM, (h + 1) * RET_HEAD_DIM)
            heads.append(lax.dot_general(kz[:, sl], v[:, sl], (((0,), (0,)), ((), ())),
                                         preferred_element_type=F32))
        us.append(jnp.concatenate(heads, axis=0))

    def scan(order):
        s = s_ref[...]
        for c in order:
            sprev_ref[0, c] = s
            s = cd * s + us[c]
        s_ref[...] = s
        sfin_ref[0] = s

    @pl.when(d == 0)
    def _():
        scan(range(cps))

    @pl.when(d == 1)
    def _():
        scan(reversed(range(cps)))


def _retention_states(p_ret, zeta, chunk_decay, s0):
    nc = p_ret.shape[0] // RET_CHUNK
    cps = min(nc, RET_STATE_CHUNKS)
    ng = nc // cps
    rows = cps * RET_CHUNK

    def group(d, i):
        return i + d * (ng - 1 - 2 * i)

    return pl.pallas_call(
        functools.partial(_ret_state_kernel, cps=cps),
        out_shape=(jax.ShapeDtypeStruct((2, nc, RET_WIDTH, RET_HEAD_DIM), F32),
                   jax.ShapeDtypeStruct((2, RET_WIDTH, RET_HEAD_DIM), F32)),
        grid=(2, ng),
        in_specs=[pl.BlockSpec((rows, RET_WIDTH), lambda d, i: (group(d, i), 1)),
                  pl.BlockSpec((rows, RET_WIDTH), lambda d, i: (group(d, i), 2)),
                  pl.BlockSpec((1, RET_CHUNK, RET_WIDTH), lambda d, i: (d, 0, 0)),
                  pl.BlockSpec((1, RET_WIDTH, 1), lambda d, i: (d, 0, 0)),
                  pl.BlockSpec((1, RET_WIDTH, RET_HEAD_DIM), lambda d, i: (d, 0, 0))],
        out_specs=(pl.BlockSpec((1, cps, RET_WIDTH, RET_HEAD_DIM), lambda d, i: (d, group(d, i), 0, 0)),
                   pl.BlockSpec((1, RET_WIDTH, RET_HEAD_DIM), lambda d, i: (d, 0, 0))),
        scratch_shapes=[pltpu.VMEM((RET_WIDTH, RET_HEAD_DIM), F32)],
        compiler_params=_cparams("arbitrary", "arbitrary"),
        name="retention_states",
    )(p_ret, p_ret, zeta, chunk_decay, s0)


def _ret_out_kernel(q_ref, k_ref, v_ref, g_ref, mask_ref, xi_ref, sf_ref, sb_ref, y_ref, *, cpo):
    for c in range(cpo):
        rows = slice(c * RET_CHUNK, (c + 1) * RET_CHUNK)
        q = q_ref[rows, :]
        k = (k_ref[rows, :] * (RET_HEAD_DIM ** -0.5)).astype(BF16)
        v = v_ref[rows, :].astype(BF16)
        qb = q.astype(BF16)
        qf = (q * xi_ref[0]).astype(BF16)
        qr = (q * xi_ref[1]).astype(BF16)
        sf = sf_ref[0, c].astype(BF16)
        sb = sb_ref[0, c].astype(BF16)
        ys = []
        for h in range(RET_HEADS):
            sl = slice(h * RET_HEAD_DIM, (h + 1) * RET_HEAD_DIM)
            scores = lax.dot_general(qb[:, sl], k[:, sl], (((1,), (1,)), ((), ())),
                                     preferred_element_type=F32)
            y = _dot((scores * mask_ref[h]).astype(BF16), v[:, sl])
            y = y + _dot(qf[:, sl], sf[sl, :]) + _dot(qr[:, sl], sb[sl, :])
            mu = jnp.mean(y, axis=-1, keepdims=True)
            yc = y - mu
            var = jnp.mean(yc * yc, axis=-1, keepdims=True)
            ys.append(yc * lax.rsqrt(var + EPS))
        y_ref[rows, :] = (_silu(g_ref[rows, :]) * jnp.concatenate(ys, axis=1)).astype(y_ref.dtype)


def _retention_outputs(p_ret, mask, xi, sprev):
    nc = p_ret.shape[0] // RET_CHUNK
    cpo = min(nc, RET_OUT_CHUNKS)
    rows = cpo * RET_CHUNK
    col = lambda c: pl.BlockSpec((rows, RET_WIDTH), lambda i: (i, c))
    return pl.pallas_call(
        functools.partial(_ret_out_kernel, cpo=cpo),
        out_shape=jax.ShapeDtypeStruct((p_ret.shape[0], RET_WIDTH), BF16),
        grid=(nc // cpo,),
        in_specs=[col(0), col(1), col(2), col(3),
                  pl.BlockSpec((RET_HEADS, RET_CHUNK, RET_CHUNK), lambda i: (0, 0, 0)),
                  pl.BlockSpec((2, RET_CHUNK, RET_WIDTH), lambda i: (0, 0, 0)),
                  pl.BlockSpec((1, cpo, RET_WIDTH, RET_HEAD_DIM), lambda i: (0, i, 0, 0)),
                  pl.BlockSpec((1, cpo, RET_WIDTH, RET_HEAD_DIM), lambda i: (1, i, 0, 0))],
        out_specs=pl.BlockSpec((rows, RET_WIDTH), lambda i: (i, 0)),
        compiler_params=_cparams("parallel"),
        name="retention_outputs",
    )(p_ret, p_ret, p_ret, p_ret, mask, xi, sprev, sprev)


def _hy_filter_kernel(feat_ref, w1_ref, b1_ref, w2_ref, b2_ref, w3_ref, b3_ref, freq_ref,
                      delta_ref, taps_ref, asum_ref, *, seq_len):
    i = pl.program_id(0)
    feat = feat_ref[...]
    tm = feat.shape[0]
    half = tm // 2
    freq = freq_ref[...]
    packed = jnp.concatenate([feat[:half], feat[half:]], axis=1)
    h = jnp.sin(freq * (_dot3(packed, w1_ref[...]) + b1_ref[...]))
    h = jnp.sin(freq * (_dot3(h, w2_ref[...]) + b2_ref[...]))
    h = _dot3(h, w3_ref[0])
    h = jnp.concatenate([h[:, :HY_WIDTH], h[:, HY_WIDTH:]], axis=0) + b3_ref[0]
    j = i * tm + lax.broadcasted_iota(jnp.int32, (tm, HY_WIDTH), 0)
    taps = jnp.where(j == seq_len, 0.0, h * jnp.exp(-feat[:, 0:1] * delta_ref[...]))
    taps_ref[...] = taps

    @pl.when(i == 0)
    def _():
        asum_ref[...] = jnp.zeros_like(asum_ref)

    asum_ref[...] += jnp.sum(jnp.abs(taps).reshape(tm // SUBLANES, SUBLANES, HY_WIDTH), axis=0)


def _position_features(seq_len):
    lo = 256
    hi = seq_len // lo + 1
    bands = np.linspace(1e-4, HY_POS_BANDS - 1, HY_POS_BANDS).astype(np.float32).astype(np.float64)
    ang_hi = (2.0 * np.pi / seq_len) * lo * np.arange(hi)[:, None] * bands[None, :]
    ang_lo = (2.0 * np.pi / seq_len) * np.arange(lo)[:, None] * bands[None, :]
    tab = lambda a: jnp.asarray(a, F32)
    ch, sh = tab(np.cos(ang_hi))[:, None, :], tab(np.sin(ang_hi))[:, None, :]
    cl, sl = tab(np.cos(ang_lo))[None, :, :], tab(np.sin(ang_lo))[None, :, :]
    npos = seq_len + 1
    cos = (ch * cl - sh * sl).reshape(hi * lo, HY_POS_BANDS)[:npos]
    sin = (sh * cl + ch * sl).reshape(hi * lo, HY_POS_BANDS)[:npos]
    c_end, s_end = tab(np.cos(2.0 * np.pi * bands))[None, :], tab(np.sin(2.0 * np.pi * bands))[None, :]
    cos_b = c_end * cos[1:seq_len] + s_end * sin[1:seq_len]
    sin_b = s_end * cos[1:seq_len] - c_end * sin[1:seq_len]
    lag = jnp.concatenate([jnp.arange(npos, dtype=F32), jnp.arange(seq_len - 1, 0, -1, dtype=F32)])
    t = lag / max(seq_len - 1, 1)
    cos = jnp.concatenate([cos, cos_b], axis=0)
    sin = jnp.concatenate([sin, sin_b], axis=0)
    return jnp.concatenate([t[:, None], cos, -sin,
                            jnp.zeros((2 * seq_len, LANES - 1 - 2 * HY_POS_BANDS), F32)], axis=-1)


def _hyena_taps(seq_len, w1, b1, w2, b2, w3, b3, freq):
    n = 2 * seq_len
    feat = _position_features(seq_len)
    w1p = jnp.concatenate([w1, jnp.zeros((LANES - w1.shape[0], w1.shape[1]), F32)], axis=0)
    pair = lambda w: jnp.concatenate(
        [jnp.concatenate([w, jnp.zeros_like(w)], axis=1),
         jnp.concatenate([jnp.zeros_like(w), w], axis=1)], axis=0)
    twice = lambda v: jnp.tile(v.reshape(1, -1), (1, 2))
    w3_dirs = jnp.stack([pair(w3[:, :HY_WIDTH]), pair(w3[:, HY_WIDTH:])])
    b3_dirs = b3.reshape(2, 1, HY_WIDTH)
    deltas = jnp.abs(jnp.linspace(math.log(HY_DECAY_TARGET) / HY_LONG_DECAY_PCT,
                                  math.log(HY_DECAY_TARGET) / HY_SHORT_DECAY_PCT,
                                  HY_WIDTH, dtype=F32)).reshape(1, HY_WIDTH)
    tm = min(seq_len, 1024)
    per_dir = seq_len // tm
    hid2 = 2 * w1.shape[1]
    const = lambda shape: pl.BlockSpec(shape, lambda i: (0, 0))
    by_dir = lambda shape: pl.BlockSpec(shape, lambda i: (i // per_dir, 0, 0))
    return pl.pallas_call(
        functools.partial(_hy_filter_kernel, seq_len=seq_len),
        out_shape=(jax.ShapeDtypeStruct((n, HY_WIDTH), F32),
                   jax.ShapeDtypeStruct((8, HY_WIDTH), F32)),
        grid=(n // tm,),
        in_specs=[pl.BlockSpec((tm, LANES), lambda i: (i, 0)),
                  const((2 * LANES, hid2)), const((1, hid2)), const((hid2, hid2)), const((1, hid2)),
                  by_dir((1, hid2, 2 * HY_WIDTH)), by_dir((1, 1, HY_WIDTH)), const((1, hid2)),
                  const((1, HY_WIDTH))],
        out_specs=(pl.BlockSpec((tm, HY_WIDTH), lambda i: (i, 0)), const((8, HY_WIDTH))),
        compiler_params=_cparams("arbitrary"),
        name="hyena_taps",
    )(feat, pair(w1p), twice(b1), pair(w2), twice(b2), w3_dirs, b3_dirs, twice(freq), deltas)


FFT_N2 = 256
FFT_K1_PER_STEP = 8
FFT_N2_PER_STEP = 16


def _kept_slabs(n1):
    kept = n1 // 2 + 1
    return kept, -(-kept // SUBLANES) * SUBLANES


def _dft_tables(n_total, n1_in):
    n1 = n_total // FFT_N2
    kept, padded = _kept_slabs(n1)
    k1 = np.arange(padded)
    a = 2.0 * np.pi * np.outer(k1, np.arange(n1_in)) / n1
    live = (k1 < kept)[:, None]
    lvl1 = np.concatenate([np.cos(a) * live, -np.sin(a) * live], axis=0)
    n2 = np.arange(FFT_N2)
    th = 2.0 * np.pi * np.outer(n2, n2) / FFT_N2
    fc, fs = np.cos(th), np.sin(th)
    g = np.block([[fc, fs], [-fs, fc]])
    ph = 2.0 * np.pi * np.outer(k1, n2) / n_total
    tw = np.stack([np.cos(ph), np.sin(ph)])[..., None]
    return (jnp.asarray(lvl1, F32), jnp.asarray(g, F32), jnp.asarray(g.T, F32), jnp.asarray(tw, F32))


def _inverse_lvl1_table(n_total, n1_out):
    n1 = n_total // FFT_N2
    kept, padded = _kept_slabs(n1)
    k1 = np.arange(padded)
    a = 2.0 * np.pi * np.outer(np.arange(n1_out), k1) / n1
    weight = np.where(k1 >= kept, 0.0, np.where((k1 == 0) | (k1 == n1 // 2), 1.0, 2.0))[None, :]
    return jnp.asarray(np.concatenate([np.cos(a) * weight, -np.sin(a) * weight], axis=1), F32)


def _lvl2_kernel(b_ref, tw_ref, gh_ref, gl_ref, *rest, inverse):
    for j in range(b_ref.shape[1]):
        c = tw_ref[0, j]
        s = tw_ref[1, j]
        br = b_ref[0, j]
        bi = b_ref[1, j]
        d = jnp.concatenate([c * br + s * bi, c * bi - s * br], axis=0)
        dh, dl = _split(d)
        x = _dot(gh_ref[...], dh) + _dot(gh_ref[...], dl) + _dot(gl_ref[...], dh)
        if not inverse:
            (o_ref,) = rest
            o_ref[0, j] = x[:FFT_N2]
            o_ref[1, j] = x[FFT_N2:]
            continue
        h_ref, gth_ref, gtl_ref, o_ref = rest
        xr, xi = x[:FFT_N2], x[FFT_N2:]
        hr, hi = h_ref[0, j], h_ref[1, j]
        y = jnp.concatenate([xr * hr - xi * hi, xr * hi + xi * hr], axis=0)
        yh, yl = _split(y)
        cc = _dot(gth_ref[...], yh) + _dot(gth_ref[...], yl) + _dot(gtl_ref[...], yh)
        cr, ci = cc[:FFT_N2], cc[FFT_N2:]
        o_ref[0, j] = c * cr - s * ci
        o_ref[1, j] = c * ci + s * cr


def _level2(b, tw, g, h=None, gt=None):
    _, n1, n2, ch = b.shape
    kb = FFT_K1_PER_STEP
    blk = pl.BlockSpec((2, kb, n2, ch), lambda i: (0, i, 0, 0))
    const = pl.BlockSpec((2 * n2, 2 * n2), lambda i: (0, 0))
    gh, gl = _split(g)
    in_specs = [blk, pl.BlockSpec((2, kb, n2, 1), lambda i: (0, i, 0, 0)), const, const]
    args = [b, tw, gh, gl]
    if h is not None:
        gth, gtl = _split(gt)
        in_specs += [blk, const, const]
        args += [h, gth, gtl]
    return pl.pallas_call(
        functools.partial(_lvl2_kernel, inverse=h is not None),
        out_shape=jax.ShapeDtypeStruct(b.shape, F32),
        grid=(n1 // kb,),
        in_specs=in_specs,
        out_specs=blk,
        compiler_params=_cparams("parallel"),
        name="hyena_fft_level2" + ("_conv" if h is not None else ""),
    )(*args)


def _lvl1_kernel(a_ref, x_ref, o_ref, *, inverse):
    tn2 = x_ref.shape[-2]
    ch = x_ref.shape[-1]
    if inverse:
        x = jnp.concatenate([x_ref[:, :, j, :].reshape(-1, ch) for j in range(tn2)], axis=1)
    else:
        x = jnp.concatenate([x_ref[:, j, :] for j in range(tn2)], axis=1)
    out = _dot3(a_ref[...], x)
    for j in range(tn2):
        slab = out[:, j * ch:(j + 1) * ch]
        if inverse:
            o_ref[:, j, :] = slab
        else:
            o_ref[:, :, j, :] = slab.reshape(2, slab.shape[0] // 2, ch)


def _level1(a, x, name):
    n1_in, n2, ch = x.shape
    n1 = a.shape[0] // 2
    tn2 = FFT_N2_PER_STEP
    return pl.pallas_call(
        functools.partial(_lvl1_kernel, inverse=False),
        out_shape=jax.ShapeDtypeStruct((2, n1, n2, ch), F32),
        grid=(n2 // tn2,),
        in_specs=[pl.BlockSpec(a.shape, lambda i: (0, 0)),
                  pl.BlockSpec((n1_in, tn2, ch), lambda i: (0, i, 0))],
        out_specs=pl.BlockSpec((2, n1, tn2, ch), lambda i: (0, 0, i, 0)),
        compiler_params=_cparams("parallel"),
        name=name,
    )(a, x)


def _inverse_level1(a, x):
    _, n1, n2, ch = x.shape
    n1_out = a.shape[0]
    tn2 = FFT_N2_PER_STEP
    return pl.pallas_call(
        functools.partial(_lvl1_kernel, inverse=True),
        out_shape=jax.ShapeDtypeStruct((n1_out, n2, ch), F32),
        grid=(n2 // tn2,),
        in_specs=[pl.BlockSpec(a.shape, lambda i: (0, 0)),
                  pl.BlockSpec((2, n1, tn2, ch), lambda i: (0, 0, i, 0))],
        out_specs=pl.BlockSpec((n1_out, tn2, ch), lambda i: (0, i, 0)),
        compiler_params=_cparams("parallel"),
        name="hyena_fft_inverse_level1",
    )(a, x)


def _hyena_long_conv(z, taps):
    seq_len, ch = z.shape
    n = 2 * seq_len
    n1 = n // FFT_N2
    lvl1_z, g, gt, tw = _dft_tables(n, n1 // 2)
    lvl1_t = _dft_tables(n, n1)[0]
    inv1 = _inverse_lvl1_table(n, n1 // 2)
    hb = _level1(lvl1_t, taps.reshape(n1, FFT_N2, ch), "hyena_fft_level1_taps")
    hspec = _level2(hb, tw, g)
    zb = _level1(lvl1_z, z.reshape(n1 // 2, FFT_N2, ch), "hyena_fft_level1_z")
    cspec = _level2(zb, tw, g, hspec, gt)
    return _inverse_level1(inv1, cspec).reshape(seq_len, ch)


def _small_conv_kernel(z_ref, taps_ref, fc_ref, fs_ref, o_ref):
    seq_len = z_ref.shape[0]
    fc = fc_ref[...]
    fs = fs_ref[...]
    z = z_ref[...]
    taps = taps_ref[...]
    zr = _dot3(fc[:, :seq_len], z)
    zi = -_dot3(fs[:, :seq_len], z)
    hr = _dot3(fc, taps)
    hi = -_dot3(fs, taps)
    yr = zr * hr - zi * hi
    yi = zr * hi + zi * hr
    o_ref[...] = _dot3(fc[:seq_len, :], yr) - _dot3(fs[:seq_len, :], yi)


def _hyena_small_conv(z, taps):
    seq_len, ch = z.shape
    n = 2 * seq_len
    th = 2.0 * np.pi * np.outer(np.arange(n), np.arange(n)) / n
    full = lambda shape: pl.BlockSpec(shape, lambda: (0, 0))
    return pl.pallas_call(
        _small_conv_kernel,
        out_shape=jax.ShapeDtypeStruct((seq_len, ch), F32),
        in_specs=[full((seq_len, ch)), full((n, ch)), full((n, n)), full((n, n))],
        out_specs=full((seq_len, ch)),
        compiler_params=pltpu.CompilerParams(vmem_limit_bytes=VMEM_LIMIT),
        name="hyena_small_conv",
    )(z, taps, jnp.asarray(np.cos(th), F32), jnp.asarray(np.sin(th), F32))


def _hy_post_kernel(x0_ref, z_ref, conv_ref, asum_ref, bias_ref, y_ref, *, n_total):
    norm = jnp.sum(asum_ref[...], axis=0, keepdims=True) + EPS
    conv = conv_ref[...] * (1.0 / (n_total * norm))
    y_ref[...] = (x0_ref[...] * (conv + bias_ref[...] * z_ref[...])).astype(y_ref.dtype)


def _hyena_post(x0, z, conv, asum, bias):
    t = x0.shape[0]
    tm = min(t, 1024)
    row = pl.BlockSpec((tm, HY_WIDTH), lambda i: (i, 0))
    return pl.pallas_call(
        functools.partial(_hy_post_kernel, n_total=2 * t),
        out_shape=jax.ShapeDtypeStruct((t, HY_WIDTH), BF16),
        grid=(t // tm,),
        in_specs=[row, row, row, pl.BlockSpec((8, HY_WIDTH), lambda i: (0, 0)),
                  pl.BlockSpec((1, HY_WIDTH), lambda i: (0, 0))],
        out_specs=row,
        compiler_params=_cparams("parallel"),
        name="hyena_post",
    )(x0, z, conv, asum, bias.reshape(1, HY_WIDTH))


def _hyena_mixer(x0, z, filt, bias):
    seq_len = x0.shape[0]
    taps, asum = _hyena_taps(seq_len, *filt)
    if 2 * seq_len // FFT_N2 >= 16:
        conv = _hyena_long_conv(z, taps)
    else:
        conv = _hyena_small_conv(z, taps)
    return _hyena_post(x0, z, conv, asum, bias)


def _mixed_residual(hy_ref, ret_ref, att_ref, wo_ref, x_ref, mgate_ref):
    y = (_dot(hy_ref[...], wo_ref[:HY_WIDTH, :])
         + _dot(ret_ref[...], wo_ref[HY_WIDTH:HY_WIDTH + RET_WIDTH, :])
         + _dot(att_ref[...], wo_ref[HY_WIDTH + RET_WIDTH:, :]))
    return x_ref[...] + mgate_ref[...] * y


def _mixer_specs(tm, index):
    row = lambda w: pl.BlockSpec((tm, w), lambda i, *_: (i, 0))
    return [row(HY_WIDTH), row(RET_WIDTH), row(ATT_WIDTH),
            pl.BlockSpec((MIX_WIDTH, D_MODEL), index), row(D_MODEL),
            pl.BlockSpec((1, D_MODEL), index)]


def _swiglu(x, wg_ref, wu_ref, wd_ref):
    hidden = wg_ref.shape[1]
    y = None
    for lo in range(0, hidden, SWIGLU_CHUNK):
        hi = min(lo + SWIGLU_CHUNK, hidden)
        a = _silu(_dot(x, wg_ref[:, lo:hi])) * _dot(x, wu_ref[:, lo:hi])
        part = _dot(a.astype(BF16), wd_ref[lo:hi, :])
        y = part if y is None else y + part
    return y


def _ffn_kernel(hy_ref, ret_ref, att_ref, wo_ref, x_ref, mgate_ref, g_ref, sh_ref, sc_ref, gate_ref,
                wg_ref, wu_ref, wd_ref, o_ref):
    x = _mixed_residual(hy_ref, ret_ref, att_ref, wo_ref, x_ref, mgate_ref)
    h = _norm_mod(x, g_ref[...], sh_ref[...], sc_ref[...]).astype(BF16)
    o_ref[...] = x + gate_ref[...] * _swiglu(h, wg_ref, wu_ref, wd_ref)


def _dense_ffn(mixed, g, shift, scale, gate, wg, wu, wd):
    t = mixed[4].shape[0]
    tm = min(t, 512)
    index = lambda i: (0, 0)
    vec = pl.BlockSpec((1, D_MODEL), index)
    resident = lambda shape: pl.BlockSpec(shape, index, pipeline_mode=pl.Buffered(1))
    return pl.pallas_call(
        _ffn_kernel,
        out_shape=jax.ShapeDtypeStruct((t, D_MODEL), F32),
        grid=(t // tm,),
        in_specs=_mixer_specs(tm, index) + [vec, vec, vec, vec,
                  resident((D_MODEL, FFN_HIDDEN)), resident((D_MODEL, FFN_HIDDEN)),
                  resident((FFN_HIDDEN, D_MODEL))],
        out_specs=pl.BlockSpec((tm, D_MODEL), lambda i: (i, 0)),
        compiler_params=_cparams("parallel"),
        name="dense_ffn",
    )(*mixed, g, shift, scale, gate, wg, wu, wd)


def _moe_route_kernel(hy_ref, ret_ref, att_ref, wo_ref, x_ref, mgate_ref, g_ref, sh_ref, sc_ref,
                      rw_ref, rb_ref, xo_ref, h_ref, info_ref):
    x = _mixed_residual(hy_ref, ret_ref, att_ref, wo_ref, x_ref, mgate_ref)
    xo_ref[...] = x
    h = _norm_mod(x, g_ref[...], sh_ref[...], sc_ref[...])
    _to_row_tiles(h_ref, h)
    logits = _dot3(h, rw_ref[...]) + rb_ref[...]
    lane = lax.broadcasted_iota(jnp.int32, logits.shape, 1)
    v1 = jnp.max(logits, axis=-1, keepdims=True)
    i1 = jnp.min(jnp.where(logits == v1, lane, LANES), axis=-1, keepdims=True)
    rest = jnp.where(lane == i1, -jnp.inf, logits)
    v2 = jnp.max(rest, axis=-1, keepdims=True)
    i2 = jnp.min(jnp.where(rest == v2, lane, LANES), axis=-1, keepdims=True)
    e = jnp.exp(v2 - v1)
    g1 = 1.0 / (1.0 + e)
    g2 = e * g1
    info_ref[...] = jnp.where(lane == 0, i1.astype(F32),
                              jnp.where(lane == 1, i2.astype(F32),
                                        jnp.where(lane == 2, g1, jnp.where(lane == 3, g2, 0.0))))


def _moe_route(mixed, g, shift, scale, router_w, router_b):
    t = mixed[4].shape[0]
    tm = min(t, 512)
    rw = jnp.concatenate([router_w, jnp.zeros((D_MODEL, LANES - N_EXPERTS), F32)], axis=1)
    rb = jnp.concatenate([router_b, jnp.full((LANES - N_EXPERTS,), -jnp.inf, F32)]).reshape(1, LANES)
    vec = pl.BlockSpec((1, D_MODEL), lambda i: (0, 0))
    return pl.pallas_call(
        _moe_route_kernel,
        out_shape=(jax.ShapeDtypeStruct((t, D_MODEL), F32),
                   jax.ShapeDtypeStruct((t * SUBLANES, LANES), F32),
                   jax.ShapeDtypeStruct((t, LANES), F32)),
        grid=(t // tm,),
        in_specs=_mixer_specs(tm, lambda i: (0, 0)) + [vec, vec, vec,
                  pl.BlockSpec((D_MODEL, LANES), lambda i: (0, 0)),
                  pl.BlockSpec((1, LANES), lambda i: (0, 0))],
        out_specs=(pl.BlockSpec((tm, D_MODEL), lambda i: (i, 0)),
                   pl.BlockSpec((tm * SUBLANES, LANES), lambda i: (i, 0)),
                   pl.BlockSpec((tm, LANES), lambda i: (i, 0))),
        compiler_params=_cparams("parallel"),
        name="moe_route",
    )(*mixed, g, shift, scale, rw, rb)


def _to_row_tiles(ref, x):
    rows = x.shape[0]
    for k in range(D_MODEL // LANES):
        ref[pl.ds(k, rows, stride=SUBLANES), :] = x[:, k * LANES:(k + 1) * LANES]


def _from_row_tiles(ref, rows):
    return jnp.concatenate(
        [ref[pl.ds(k, rows, stride=SUBLANES), :] for k in range(D_MODEL // LANES)], axis=1)


def _row_copy(src_hbm, dst_vmem, sem, src_row, dst_row):
    src = pl.multiple_of(src_row * SUBLANES, SUBLANES)
    dst = pl.multiple_of(dst_row * SUBLANES, SUBLANES)
    return pltpu.make_async_copy(src_hbm.at[pl.ds(src, SUBLANES)], dst_vmem.at[pl.ds(dst, SUBLANES)], sem)


def _moe_expert_kernel(blk_e_ref, nused_ref, dest_ref, fill_ref, h_hbm, wg_ref, wu_ref, wd_ref,
                       y_ref, xbuf_ref, src_ref, sem):
    b = pl.program_id(0)
    n_used = nused_ref[0]

    @pl.when(b == 0)
    def _():
        def invert(t, carry):
            for j in range(TOP_K):
                src_ref[dest_ref[TOP_K * t + j]] = t
            return carry

        lax.fori_loop(0, dest_ref.shape[0] // TOP_K, invert, 0, unroll=DMA_UNROLL)
        for e in range(N_EXPERTS):
            def pad(d, carry):
                src_ref[d] = 0
                return carry

            lax.fori_loop(fill_ref[0, e], fill_ref[1, e], pad, 0)

    def issue_rows(blk, slot):
        def body(r, carry):
            _row_copy(h_hbm, xbuf_ref.at[slot], sem.at[slot], src_ref[blk * MOE_BLOCK + r], r).start()
            return carry

        lax.fori_loop(0, MOE_BLOCK, body, 0, unroll=DMA_UNROLL)

    def wait_rows(slot):
        def body(r, carry):
            _row_copy(h_hbm, xbuf_ref.at[slot], sem.at[slot], 0, r).wait()
            return carry

        lax.fori_loop(0, MOE_BLOCK, body, 0, unroll=DMA_UNROLL)

    @pl.when(b < n_used)
    def _():
        slot = b % 2

        @pl.when(b == 0)
        def _():
            issue_rows(0, 0)

        wait_rows(slot)

        @pl.when(b + 1 < n_used)
        def _():
            issue_rows(b + 1, 1 - slot)

        x = _from_row_tiles(xbuf_ref.at[slot], MOE_BLOCK).astype(BF16)
        _to_row_tiles(y_ref, _swiglu(x, wg_ref.at[0], wu_ref.at[0], wd_ref.at[0]))

    @pl.when(b >= n_used)
    def _():
        y_ref[...] = jnp.zeros_like(y_ref)


def _moe_experts(h, blk_expert, n_used, dest, fill, wg, wu, wd, n_blk):
    def w_index(b, be, nu, de, fi):
        return be[jnp.minimum(b, nu[0] - 1)], 0, 0

    grid_spec = pltpu.PrefetchScalarGridSpec(
        num_scalar_prefetch=4,
        grid=(n_blk,),
        in_specs=[pl.BlockSpec(memory_space=pl.ANY),
                  pl.BlockSpec((1, D_MODEL, FFN_HIDDEN), w_index),
                  pl.BlockSpec((1, D_MODEL, FFN_HIDDEN), w_index),
                  pl.BlockSpec((1, FFN_HIDDEN, D_MODEL), w_index)],
        out_specs=pl.BlockSpec((MOE_BLOCK * SUBLANES, LANES), lambda b, be, nu, de, fi: (b, 0)),
        scratch_shapes=[pltpu.VMEM((2, MOE_BLOCK * SUBLANES, LANES), F32),
                        pltpu.SMEM((n_blk * MOE_BLOCK,), jnp.int32),
                        pltpu.SemaphoreType.DMA((2,))],
    )
    return pl.pallas_call(
        _moe_expert_kernel,
        out_shape=jax.ShapeDtypeStruct((n_blk * MOE_BLOCK * SUBLANES, LANES), F32),
        grid_spec=grid_spec,
        compiler_params=_cparams("arbitrary"),
        name="moe_experts",
    )(blk_expert, n_used, dest, fill, h, wg, wu, wd)


def _moe_combine_kernel(dest_ref, x_ref, info_ref, gate_ref, ng_ref, y_hbm, o_ref,
                        y1_ref, y2_ref, sem, *, final_norm):
    i = pl.program_id(0)
    tm = x_ref.shape[0]
    slot = i % 2

    def issue_rows(blk, slot):
        def body(r, carry):
            a = 2 * (blk * tm + r)
            _row_copy(y_hbm, y1_ref.at[slot], sem.at[slot], dest_ref[a], r).start()
            _row_copy(y_hbm, y2_ref.at[slot], sem.at[slot], dest_ref[a + 1], r).start()
            return carry

        lax.fori_loop(0, tm, body, 0, unroll=DMA_UNROLL)

    def wait_rows(slot):
        def body(r, carry):
            _row_copy(y_hbm, y1_ref.at[slot], sem.at[slot], 0, r).wait()
            _row_copy(y_hbm, y2_ref.at[slot], sem.at[slot], 0, r).wait()
            return carry

        lax.fori_loop(0, tm, body, 0, unroll=DMA_UNROLL)

    @pl.when(i == 0)
    def _():
        issue_rows(0, 0)

    wait_rows(slot)

    @pl.when(i + 1 < pl.num_programs(0))
    def _():
        issue_rows(i + 1, 1 - slot)

    info = info_ref[...]
    y = (_from_row_tiles(y1_ref.at[slot], tm) * info[:, 2:3]
         + _from_row_tiles(y2_ref.at[slot], tm) * info[:, 3:4])
    x = x_ref[...] + gate_ref[...] * y
    if final_norm:
        ms = jnp.mean(x * x, axis=-1, keepdims=True)
        x = x * lax.rsqrt(ms + EPS) * ng_ref[...]
    o_ref[...] = x


def _moe_combine(dest, x, info, gate, norm_g, y, final_norm):
    t = x.shape[0]
    tm = min(t, 256)
    vec = pl.BlockSpec((1, D_MODEL), lambda i, d: (0, 0))
    grid_spec = pltpu.PrefetchScalarGridSpec(
        num_scalar_prefetch=1,
        grid=(t // tm,),
        in_specs=[pl.BlockSpec((tm, D_MODEL), lambda i, d: (i, 0)),
                  pl.BlockSpec((tm, LANES), lambda i, d: (i, 0)), vec, vec,
                  pl.BlockSpec(memory_space=pl.ANY)],
        out_specs=pl.BlockSpec((tm, D_MODEL), lambda i, d: (i, 0)),
        scratch_shapes=[pltpu.VMEM((2, tm * SUBLANES, LANES), F32),
                        pltpu.VMEM((2, tm * SUBLANES, LANES), F32),
                        pltpu.SemaphoreType.DMA((2,))],
    )
    return pl.pallas_call(
        functools.partial(_moe_combine_kernel, final_norm=final_norm),
        out_shape=jax.ShapeDtypeStruct((t, D_MODEL), F32),
        grid_spec=grid_spec,
        compiler_params=_cparams("arbitrary"),
        name="moe_combine",
    )(dest, x, info, gate, norm_g, y)


def _moe_layer(mixed, g, shift, scale, gate, router_w, router_b, wg, wu, wd, norm_g, final_norm):
    t = mixed[4].shape[0]
    n_asg = t * TOP_K
    x, h, info = _moe_route(mixed, g, shift, scale, router_w, router_b)
    expert = info[:, :TOP_K].astype(jnp.int32).reshape(-1)
    onehot = (expert[:, None] == jnp.arange(N_EXPERTS, dtype=jnp.int32)[None, :]).astype(jnp.int32)
    csum = jnp.cumsum(onehot, axis=0)
    counts = csum[-1]
    padded = (counts + MOE_BLOCK - 1) // MOE_BLOCK * MOE_BLOCK
    pad_end = jnp.cumsum(padded)
    pad_start = pad_end - padded
    dest = jnp.sum(onehot * (csum - 1 + pad_start[None, :]), axis=1).astype(jnp.int32)
    n_blk = -(-n_asg // MOE_BLOCK) + N_EXPERTS
    blk_start = jnp.arange(n_blk, dtype=jnp.int32) * MOE_BLOCK
    blk_expert = jnp.minimum(jnp.sum(blk_start[:, None] >= pad_end[None, :], axis=1),
                             N_EXPERTS - 1).astype(jnp.int32)
    n_used = (pad_end[-1:] // MOE_BLOCK).astype(jnp.int32)
    fill = jnp.stack([pad_start + counts, pad_end]).astype(jnp.int32)
    y = _moe_experts(h, blk_expert, n_used, dest, fill, wg, wu, wd, n_blk)
    return _moe_combine(dest, x, info, gate, norm_g, y, final_norm)


def _final_norm_kernel(x_ref, g_ref, o_ref):
    x = x_ref[...]
    ms = jnp.mean(x * x, axis=-1, keepdims=True)
    o_ref[...] = x * lax.rsqrt(ms + EPS) * g_ref[...]


def _final_norm(x, g):
    t = x.shape[0]
    tm = min(t, 1024)
    return pl.pallas_call(
        _final_norm_kernel,
        out_shape=jax.ShapeDtypeStruct((t, D_MODEL), F32),
        grid=(t // tm,),
        in_specs=[pl.BlockSpec((tm, D_MODEL), lambda i: (i, 0)),
                  pl.BlockSpec((1, D_MODEL), lambda i: (0, 0))],
        out_specs=pl.BlockSpec((tm, D_MODEL), lambda i: (i, 0)),
        compiler_params=_cparams("parallel"),
        name="final_norm",
    )(x, g)


def kernel(x, c, ctx, c_ctx, ada_w, ada_b, norm1_g, norm2_g, w_in, w_out, hy_conv_w, hy_conv_b, hy_filt_w1, hy_filt_b1, hy_filt_w2, hy_filt_b2, hy_filt_w3, hy_filt_b3, hy_filt_freq, hy_bias, ret_log_rate, attn_q_g, attn_k_g, ffn_w_gate, ffn_w_up, ffn_w_down, moe_router_w, moe_router_b, moe_w_gate, moe_w_up, moe_w_down, final_norm_g):
    assert x.shape[0] == 1 and c.shape[0] == 1
    seq_len = x.shape[1]
    x_lat = x[0]
    x_ctx = ctx[0]
    cos_t, sin_t = _rope_tables(seq_len)
    cvec = jnp.concatenate([c, c_ctx[None, :], jnp.zeros((6, D_MODEL), F32)], axis=0)
    cvec = _silu(cvec)
    row = lambda v: v.reshape(1, D_MODEL)
    zero_state = jnp.zeros((2, RET_WIDTH, RET_HEAD_DIM), F32)
    zero_tab = jnp.zeros((x_ctx.shape[0], LANES), F32)

    for l in range(DEPTH):
        last = l == DEPTH - 1
        mods = _ada_modulation(cvec, ada_w, ada_b, l).reshape(8, 6, D_MODEL)
        mod = [row(mods[0, i]) for i in range(6)]
        mod_c = [row(mods[1, i]) for i in range(6)]
        filt = (hy_filt_w1[l], hy_filt_b1[l], hy_filt_w2[l], hy_filt_b2[l],
                hy_filt_w3[l], hy_filt_b3[l], hy_filt_freq[l])
        lg_f = -jnp.exp(ret_log_rate[l, 0].astype(F32))
        lg_b = -jnp.exp(ret_log_rate[l, 1].astype(F32))
        mask, zeta, xi, chunk_decay = _ret_tables(lg_f, lg_b)
        w_in_l = w_in[l].astype(BF16)
        w_out_l = w_out[l].astype(BF16)
        n1 = row(norm1_g[l])
        n2 = row(norm2_g[l])

        x0, z, p_ret, qt, k_aug, vt, kmax_sq = _in_projection(
            x_lat, n1, mod[0], mod[1], w_in_l, hy_conv_w[l], hy_conv_b[l], cos_t, sin_t,
            attn_q_g[l], attn_k_g[l], rope=True)
        x0_c, z_c, pc_ret, qct, kc, vct, _ = _in_projection(
            x_ctx, n1, mod_c[0], mod_c[1], w_in_l, hy_conv_w[l], hy_conv_b[l], zero_tab, zero_tab,
            attn_q_g[l], attn_k_g[l], rope=False)

        sprev_c, s_ctx = _retention_states(pc_ret, zeta, chunk_decay, zero_state)

        y_hy = _hyena_mixer(x0, z, filt, hy_bias[l])
        sprev, _ = _retention_states(p_ret, zeta, chunk_decay, s_ctx)
        y_ret = _retention_outputs(p_ret, mask, xi, sprev)
        kmax = jnp.sqrt(jnp.max(kmax_sq[:, 0, ::ATT_HEAD_DIM], axis=0)) * KMAX_SLACK
        y_att = _attention(qt, kc, vct, kmax, k_aug, vt, tq=ATT_TQ, tk=ATT_TK, sub=ATT_SUB)

        if not last:
            yc_hy = _hyena_mixer(x0_c, z_c, filt, hy_bias[l])
            yc_ret = _retention_outputs(pc_ret, mask, xi, sprev_c)
            yc_att = _attention(qct, kc, vct, tq=x_ctx.shape[0])
            mixed_c = (yc_hy, yc_ret, yc_att, w_out_l, x_ctx, mod_c[2])
        mixed = (y_hy, y_ret, y_att, w_out_l, x_lat, mod[2])

        i = l // 2
        if l % 2 == 0:
            wg, wu, wd = (ffn_w_gate[i].astype(BF16), ffn_w_up[i].astype(BF16),
                          ffn_w_down[i].astype(BF16))
            x_lat = _dense_ffn(mixed, n2, mod[3], mod[4], mod[5], wg, wu, wd)
            if not last:
                x_ctx = _dense_ffn(mixed_c, n2, mod_c[3], mod_c[4], mod_c[5], wg, wu, wd)
        else:
            wg, wu, wd = (moe_w_gate[i].astype(BF16), moe_w_up[i].astype(BF16),
                          moe_w_down[i].astype(BF16))
            fg = row(final_norm_g)
            x_lat = _moe_layer(mixed, n2, mod[3], mod[4], mod[5], moe_router_w[i], moe_router_b[i],
                               wg, wu, wd, fg, final_norm=last)
            if not last:
                x_ctx = _moe_layer(mixed_c, n2, mod_c[3], mod_c[4], mod_c[5], moe_router_w[i],
                                   moe_router_b[i], wg, wu, wd, fg, final_norm=False)
    if DEPTH % 2 == 1:
        x_lat = _final_norm(x_lat, row(final_norm_g))
    return x_lat[None]
```

```python
import functools
import math

import numpy as np
import jax
import jax.numpy as jnp
from jax import lax
from jax.experimental import pallas as pl
from jax.experimental.pallas import tpu as pltpu

F32 = jnp.float32
BF16 = jnp.bfloat16

D_MODEL = 1024
DEPTH = 2
GRID_W = 64
EPS = 1e-6

HY_WIDTH = 256
HY_COLS = 3 * HY_WIDTH
HY_POS_BANDS = 16
HY_DECAY_TARGET = 1e-2
HY_SHORT_DECAY_PCT = 0.3
HY_LONG_DECAY_PCT = 1.5

RET_HEAD_DIM = 64
RET_HEADS = 4
RET_WIDTH = RET_HEADS * RET_HEAD_DIM
RET_COLS = 4 * RET_WIDTH
RET_CHUNK = 128

ATT_HEAD_DIM = 64
ATT_HEADS = 8
ATT_KV_HEADS = 2
ATT_GROUP = ATT_HEADS // ATT_KV_HEADS
ATT_WIDTH = ATT_HEADS * ATT_HEAD_DIM
ATT_KV_WIDTH = ATT_KV_HEADS * ATT_HEAD_DIM
ATT_COLS = ATT_WIDTH + 2 * ATT_KV_WIDTH
ROPE_AXIS_DIM = ATT_HEAD_DIM // 2
ROPE_BASE = 10000.0

MIX_WIDTH = HY_WIDTH + RET_WIDTH + ATT_WIDTH
IN_COLS = HY_COLS + RET_COLS + ATT_COLS

FFN_HIDDEN = 2816
N_EXPERTS = 8
TOP_K = 2
MOE_BLOCK = 512

LOG2_E = 1.4426950408889634
EXP2_CAP = 60.0
KMAX_SLACK = 1.0 + 2.0 ** -7
ATT_TQ = 512
ATT_TK = 4096
ATT_SUB = 512
ATT_VROWS = 80
RET_STATE_CHUNKS = 16
RET_OUT_CHUNKS = 8
SWIGLU_CHUNK = 1024
DMA_UNROLL = 8

LANES = 128
SUBLANES = 8
VMEM_LIMIT = 56 * 1024 * 1024


def _cparams(*sem):
    return pltpu.CompilerParams(dimension_semantics=sem, vmem_limit_bytes=VMEM_LIMIT)


def _dot(a, b):
    return jnp.dot(a, b, preferred_element_type=F32)


def _split(a):
    hi = a.astype(BF16)
    lo = (a - hi.astype(F32)).astype(BF16)
    return hi, lo


def _dot3(a, b):
    ah, al = _split(a)
    bh, bl = _split(b)
    return _dot(ah, bh) + _dot(al, bh) + _dot(ah, bl)


def _silu(x):
    return x * (1.0 / (1.0 + jnp.exp(-x)))


def _norm_mod(x, g, shift, scale):
    ms = jnp.mean(x * x, axis=-1, keepdims=True)
    h = x * lax.rsqrt(ms + EPS) * g
    return h * (1.0 + scale) + shift


def _ada_kernel(c_ref, w_ref, b_ref, o_ref):
    o_ref[...] = _dot3(c_ref[...], w_ref[0]) + b_ref[0]


def _ada_modulation(cc, w, b, layer):
    n = w.shape[2]
    tn = 1536
    return pl.pallas_call(
        _ada_kernel,
        out_shape=jax.ShapeDtypeStruct((8, n), F32),
        grid=(n // tn,),
        in_specs=[pl.BlockSpec((8, D_MODEL), lambda j: (0, 0)),
                  pl.BlockSpec((1, D_MODEL, tn), lambda j: (layer, 0, j)),
                  pl.BlockSpec((1, 1, tn), lambda j: (layer, 0, j))],
        out_specs=pl.BlockSpec((8, tn), lambda j: (0, j)),
        compiler_params=_cparams("parallel"),
        name="ada_modulation",
    )(cc, w, b.reshape(b.shape[0], 1, n))


def _head_mean_matrix(width):
    idx = np.arange(width) // ATT_HEAD_DIM
    return jnp.asarray((idx[:, None] == idx[None, :]).astype(np.float32) / ATT_HEAD_DIM, BF16)


def _head_rms(x, bd, g):
    ms = _dot((x * x).astype(BF16), bd)
    return x * lax.rsqrt(ms + EPS) * g


def _rope(x, cos, sin):
    n = x.shape[1]
    lane = lax.broadcasted_iota(jnp.int32, x.shape, 1)
    swapped = jnp.where((lane % 32) < 16, pltpu.roll(x, n - 16, 1), pltpu.roll(x, 16, 1))
    return x * cos + swapped * sin


def _attention_operands(p, cos_ref, sin_ref, qg_ref, kg_ref, bdq_ref, bdk_ref,
                        qt_ref, k_ref, vt_ref, kmax_ref, rope):
    q = _head_rms(p[:, :ATT_WIDTH], bdq_ref[...], qg_ref[...])
    k = _head_rms(p[:, ATT_WIDTH:ATT_WIDTH + ATT_KV_WIDTH], bdk_ref[...], kg_ref[...])
    if rope:
        cos = cos_ref[...]
        sin = sin_ref[...]
        q = jnp.concatenate(
            [_rope(q[:, j * LANES:(j + 1) * LANES], cos, sin) for j in range(ATT_WIDTH // LANES)],
            axis=1)
        k = _rope(k, cos, sin)
    qt_ref[...] = (q * (ATT_HEAD_DIM ** -0.5 * LOG2_E)).T.astype(BF16)
    tm = k.shape[0]
    lane = lax.broadcasted_iota(jnp.int32, k.shape, 1)
    minus_one_col = jnp.where(lane == ATT_HEAD_DIM, -1.0, 0.0)
    kb = k.astype(BF16)
    k_ref[0] = jnp.where(lane < ATT_HEAD_DIM, k, minus_one_col).astype(BF16)
    k_ref[1] = jnp.where(lane < ATT_HEAD_DIM, pltpu.roll(k, ATT_HEAD_DIM, 1), minus_one_col).astype(BF16)
    norm_sq = _dot((kb.astype(F32) ** 2).astype(BF16), bdk_ref[...]) * ATT_HEAD_DIM
    kmax_ref[0] = jnp.broadcast_to(jnp.max(norm_sq, axis=0, keepdims=True), (SUBLANES, LANES))
    vt = p[:, ATT_WIDTH + ATT_KV_WIDTH:].T.astype(BF16)
    sub = lax.broadcasted_iota(jnp.int32, (ATT_VROWS - ATT_HEAD_DIM, tm), 0)
    ones_row = jnp.where(sub == 0, 1.0, 0.0).astype(BF16)
    for h in range(ATT_KV_HEADS):
        vt_ref[h * ATT_VROWS:h * ATT_VROWS + ATT_HEAD_DIM, :] = vt[h * ATT_HEAD_DIM:(h + 1) * ATT_HEAD_DIM]
        vt_ref[h * ATT_VROWS + ATT_HEAD_DIM:(h + 1) * ATT_VROWS, :] = ones_row


def _short_conv_gate(p, prev_row, next_row, w, b, x0_ref, z_ref):
    tm = p.shape[0]
    row = lax.broadcasted_iota(jnp.int32, p.shape, 0)
    up = jnp.where(row == 0, prev_row, pltpu.roll(p, 1, 0))
    dn = jnp.where(row == tm - 1, next_row, pltpu.roll(p, tm - 1, 0))
    u = up * w[0:1] + p * w[1:2] + dn * w[2:3] + b
    x0_ref[...] = u[:, :HY_WIDTH]
    z_ref[...] = u[:, HY_WIDTH:2 * HY_WIDTH] * u[:, 2 * HY_WIDTH:]


def _inproj_kernel(x_ref, xprev_ref, xnext_ref, g_ref, sh_ref, sc_ref, w_ref, cw_ref, cb_ref,
                   cos_ref, sin_ref, qg_ref, kg_ref, bdq_ref, bdk_ref,
                   x0_ref, z_ref, ret_ref, qt_ref, k_ref, vt_ref, kmax_ref, *, rope):
    i = pl.program_id(0)
    norm = lambda v: _norm_mod(v, g_ref[...], sh_ref[...], sc_ref[...]).astype(BF16)
    p = _dot(norm(x_ref[...]), w_ref[...])
    w_hy = w_ref[:, :HY_COLS]
    prev_row = jnp.where(i == 0, 0.0, _dot(norm(xprev_ref[...]), w_hy)[SUBLANES - 1:SUBLANES, :])
    next_row = jnp.where(i == pl.num_programs(0) - 1, 0.0, _dot(norm(xnext_ref[...]), w_hy)[0:1, :])
    _short_conv_gate(p[:, :HY_COLS], prev_row, next_row, cw_ref[...], cb_ref[...], x0_ref, z_ref)
    ret_ref[...] = p[:, HY_COLS:HY_COLS + RET_COLS]
    _attention_operands(p[:, HY_COLS + RET_COLS:], cos_ref, sin_ref, qg_ref, kg_ref, bdq_ref, bdk_ref,
                        qt_ref, k_ref, vt_ref, kmax_ref, rope)


def _in_projection(x, g, shift, scale, w_bf16, conv_w, conv_b, cos, sin, q_g, k_g, rope):
    t = x.shape[0]
    tm = min(t, 512)
    groups = tm // SUBLANES
    last_group = t // SUBLANES - 1
    qg = jnp.tile(q_g, ATT_HEADS).reshape(1, ATT_WIDTH)
    kg = jnp.tile(k_g, ATT_KV_HEADS).reshape(1, ATT_KV_WIDTH)
    vec = pl.BlockSpec((1, D_MODEL), lambda i: (0, 0))
    const = lambda shape: pl.BlockSpec(shape, lambda i: (0, 0))
    return pl.pallas_call(
        functools.partial(_inproj_kernel, rope=rope),
        out_shape=(jax.ShapeDtypeStruct((t, HY_WIDTH), F32),
                   jax.ShapeDtypeStruct((t, HY_WIDTH), F32),
                   jax.ShapeDtypeStruct((t, RET_COLS), F32),
                   jax.ShapeDtypeStruct((ATT_WIDTH, t), BF16),
                   jax.ShapeDtypeStruct((ATT_KV_HEADS, t, LANES), BF16),
                   jax.ShapeDtypeStruct((ATT_KV_HEADS * ATT_VROWS, t), BF16),
                   jax.ShapeDtypeStruct((t // tm, SUBLANES, LANES), F32)),
        grid=(t // tm,),
        in_specs=[pl.BlockSpec((tm, D_MODEL), lambda i: (i, 0)),
                  pl.BlockSpec((SUBLANES, D_MODEL), lambda i: (jnp.maximum(i * groups - 1, 0), 0)),
                  pl.BlockSpec((SUBLANES, D_MODEL),
                               lambda i: (jnp.minimum((i + 1) * groups, last_group), 0)),
                  vec, vec, vec, const((D_MODEL, IN_COLS)),
                  const((3, HY_COLS)), const((1, HY_COLS)),
                  pl.BlockSpec((tm, LANES), lambda i: (i, 0)),
                  pl.BlockSpec((tm, LANES), lambda i: (i, 0)),
                  const((1, ATT_WIDTH)), const((1, ATT_KV_WIDTH)),
                  const((ATT_WIDTH, ATT_WIDTH)), const((ATT_KV_WIDTH, ATT_KV_WIDTH))],
        out_specs=(pl.BlockSpec((tm, HY_WIDTH), lambda i: (i, 0)),
                   pl.BlockSpec((tm, HY_WIDTH), lambda i: (i, 0)),
                   pl.BlockSpec((tm, RET_COLS), lambda i: (i, 0)),
                   pl.BlockSpec((ATT_WIDTH, tm), lambda i: (0, i)),
                   pl.BlockSpec((ATT_KV_HEADS, tm, LANES), lambda i: (0, i, 0)),
                   pl.BlockSpec((ATT_KV_HEADS * ATT_VROWS, tm), lambda i: (0, i)),
                   pl.BlockSpec((1, SUBLANES, LANES), lambda i: (i, 0, 0))),
        compiler_params=_cparams("parallel"),
        name="in_projection",
    )(x, x, x, g, shift, scale, w_bf16, conv_w, conv_b.reshape(1, HY_COLS), cos, sin, qg, kg,
      _head_mean_matrix(ATT_WIDTH), _head_mean_matrix(ATT_KV_WIDTH))


def _rope_tables(n_tokens):
    rows = n_tokens // GRID_W
    inv_freq = ROPE_BASE ** (-jnp.arange(0, ROPE_AXIS_DIM, 2, dtype=F32) / ROPE_AXIS_DIM)
    ang_r = jnp.arange(rows, dtype=F32)[:, None] * inv_freq[None, :]
    ang_c = jnp.arange(GRID_W, dtype=F32)[:, None] * inv_freq[None, :]
    nf = inv_freq.shape[0]
    by_row = lambda tab: jnp.broadcast_to(tab[:, None, :], (rows, GRID_W, nf)).reshape(n_tokens, nf)
    by_col = lambda tab: jnp.broadcast_to(tab[None, :, :], (rows, GRID_W, nf)).reshape(n_tokens, nf)
    cr, sr = by_row(jnp.cos(ang_r)), by_row(jnp.sin(ang_r))
    cc, sc = by_col(jnp.cos(ang_c)), by_col(jnp.sin(ang_c))
    cos = jnp.concatenate([cr, cr, cc, cc], axis=1)
    sin = jnp.concatenate([-sr, sr, -sc, sc], axis=1)
    return jnp.tile(cos, (1, 2)), jnp.tile(sin, (1, 2))


def _attention_kernel(*refs, tq, tk, sub, n_lat):
    if n_lat:
        kmax_ref, qt_ref, kc_ref, vct_ref, k_ref, vt_ref, o_ref, qs_ref, acc_ref, m_ref = refs
    else:
        qt_ref, kc_ref, vct_ref, o_ref, acc_ref = refs
    cols = ATT_GROUP * tq
    qt = qt_ref[...]
    qst = jnp.concatenate(
        [qt[g * ATT_HEAD_DIM:(g + 1) * ATT_HEAD_DIM, :] for g in range(ATT_GROUP)], axis=1)
    pad = jnp.zeros((LANES - ATT_HEAD_DIM, cols), BF16)
    s = _dot(kc_ref[0], jnp.concatenate([qst, pad], axis=0))
    mt = jnp.max(s, axis=0, keepdims=True).astype(BF16)
    mtf = mt.astype(F32)
    acc_ref[...] = _dot(vct_ref[...], jnp.exp2(s - mtf).astype(BF16))

    if n_lat:
        row = lax.broadcasted_iota(jnp.int32, pad.shape, 0)
        qs_ref[...] = jnp.concatenate([qst, jnp.where(row == 0, mtf, 0.0).astype(BF16)], axis=0)
        qf = qst.astype(F32)
        qn = jnp.sqrt(jnp.sum(qf * qf, axis=0, keepdims=True))
        fast = jnp.max(qn * kmax_ref[pl.program_id(0)] - mtf) <= EXP2_CAP

        def tile(j):
            off = pl.multiple_of(j * tk, tk)
            return k_ref[0, pl.ds(off, tk), :], vt_ref[:, pl.ds(off, tk)]

        @pl.when(fast)
        def _():
            def body(j, carry):
                k, vt = tile(j)
                qs = qs_ref[...]
                part = None
                for c in range(tk // sub):
                    p = jnp.exp2(_dot(k[c * sub:(c + 1) * sub], qs)).astype(BF16)
                    pv = _dot(vt[:, c * sub:(c + 1) * sub], p)
                    part = pv if part is None else part + pv
                acc_ref[...] += part
                return carry

            lax.fori_loop(0, n_lat, body, 0)

        @pl.when(jnp.logical_not(fast))
        def _():
            m_ref[...] = jnp.zeros_like(m_ref)

            def body(j, carry):
                k, vt = tile(j)
                s = _dot(k, qs_ref[...])
                m_old = m_ref[...]
                m_new = jnp.maximum(m_old, jnp.max(s, axis=0, keepdims=True))
                p = jnp.exp2(s - m_new).astype(BF16)
                acc_ref[...] = jnp.exp2(m_old - m_new) * acc_ref[...] + _dot(vt, p)
                m_ref[...] = m_new
                return carry

            lax.fori_loop(0, n_lat, body, 0)

    acc = acc_ref[...]
    o = acc[:ATT_HEAD_DIM] / acc[ATT_HEAD_DIM:ATT_HEAD_DIM + 1]
    o_ref[...] = jnp.concatenate(
        [o[:, g * tq:(g + 1) * tq].T for g in range(ATT_GROUP)], axis=1).astype(o_ref.dtype)


def _attention(qt, kc, vct, kmax=None, k=None, vt=None, *, tq, tk=0, sub=0):
    t = qt.shape[1]
    lc = kc.shape[1]
    gw = ATT_GROUP * ATT_HEAD_DIM
    cols = ATT_GROUP * tq
    in_specs = [pl.BlockSpec((gw, tq), lambda h, i: (h, i)),
                pl.BlockSpec((1, lc, LANES), lambda h, i: (h, 0, 0)),
                pl.BlockSpec((ATT_VROWS, lc), lambda h, i: (h, 0))]
    args = [qt, kc, vct]
    scratch = [pltpu.VMEM((ATT_VROWS, cols), F32)]
    n_lat = 0
    if k is not None:
        lk = k.shape[1]
        n_lat = lk // tk
        in_specs = ([pl.BlockSpec(memory_space=pltpu.SMEM)] + in_specs
                    + [pl.BlockSpec((1, lk, LANES), lambda h, i: (h, 0, 0)),
                       pl.BlockSpec((ATT_VROWS, lk), lambda h, i: (h, 0))])
        args = [kmax] + args + [k, vt]
        scratch = [pltpu.VMEM((LANES, cols), BF16)] + scratch + [pltpu.VMEM((1, cols), F32)]
    return pl.pallas_call(
        functools.partial(_attention_kernel, tq=tq, tk=tk, sub=sub, n_lat=n_lat),
        out_shape=jax.ShapeDtypeStruct((t, ATT_WIDTH), BF16),
        grid=(ATT_KV_HEADS, t // tq),
        in_specs=in_specs,
        out_specs=pl.BlockSpec((tq, gw), lambda h, i: (i, h)),
        scratch_shapes=scratch,
        compiler_params=_cparams("parallel", "parallel"),
        name="attention_lat" if n_lat else "attention_ctx",
    )(*args)


def _ret_tables(lg_f, lg_b):
    pos = jnp.arange(RET_CHUNK, dtype=F32)
    rel = pos[:, None] - pos[None, :]
    mask = jnp.where(rel > 0, jnp.exp(lg_f[:, None, None] * jnp.maximum(rel, 0.0)),
                     jnp.where(rel < 0, jnp.exp(lg_b[:, None, None] * jnp.maximum(-rel, 0.0)), 2.0))
    zeta = jnp.stack([jnp.exp(lg_f[:, None] * (RET_CHUNK - 1 - pos)[None, :]),
                      jnp.exp(lg_b[:, None] * pos[None, :])])
    xi = jnp.stack([jnp.exp(lg_f[:, None] * (pos + 1)[None, :]),
                    jnp.exp(lg_b[:, None] * (RET_CHUNK - pos)[None, :])])
    chunk_decay = jnp.stack([jnp.exp(lg_f * RET_CHUNK), jnp.exp(lg_b * RET_CHUNK)])
    zeta = jnp.repeat(zeta, RET_HEAD_DIM, axis=1).transpose(0, 2, 1)
    xi = jnp.repeat(xi, RET_HEAD_DIM, axis=1).transpose(0, 2, 1)
    chunk_decay = jnp.repeat(chunk_decay, RET_HEAD_DIM, axis=1)[:, :, None]
    return mask, zeta, xi, chunk_decay


def _ret_state_kernel(k_ref, v_ref, zeta_ref, cd_ref, s0_ref, sprev_ref, sfin_ref, s_ref, *, cps):
    d = pl.program_id(0)

    @pl.when(pl.program_id(1) == 0)
    def _():
        s_ref[...] = s0_ref[0]

    zeta = zeta_ref[0] * (RET_HEAD_DIM ** -0.5)
    cd = cd_ref[0]
    us = []
    for c in range(cps):
        rows = slice(c * RET_CHUNK, (c + 1) * RET_CHUNK)
        kz = (k_ref[rows, :] * zeta).astype(BF16)
        v = v_ref[rows, :].astype(BF16)
        heads = []
        for h in range(RET_HEADS):
            sl = slice(h * RET_HEAD_DIM, (h + 1) * RET_HEAD_DIM)
            heads.append(lax.dot_general(kz[:, sl], v[:, sl], (((0,), (0,)), ((), ())),
                                         preferred_element_type=F32))
        us.append(jnp.concatenate(heads, axis=0))

    def scan(order):
        s = s_ref[...]
        for c in order:
            sprev_ref[0, c] = s
            s = cd * s + us[c]
        s_ref[...] = s
        sfin_ref[0] = s

    @pl.when(d == 0)
    def _():
        scan(range(cps))

    @pl.when(d == 1)
    def _():
        scan(reversed(range(cps)))


def _retention_states(p_ret, zeta, chunk_decay, s0):
    nc = p_ret.shape[0] // RET_CHUNK
    cps = min(nc, RET_STATE_CHUNKS)
    ng = nc // cps
    rows = cps * RET_CHUNK

    def group(d, i):
        return i + d * (ng - 1 - 2 * i)

    return pl.pallas_call(
        functools.partial(_ret_state_kernel, cps=cps),
        out_shape=(jax.ShapeDtypeStruct((2, nc, RET_WIDTH, RET_HEAD_DIM), F32),
                   jax.ShapeDtypeStruct((2, RET_WIDTH, RET_HEAD_DIM), F32)),
        grid=(2, ng),
        in_specs=[pl.BlockSpec((rows, RET_WIDTH), lambda d, i: (group(d, i), 1)),
                  pl.BlockSpec((rows, RET_WIDTH), lambda d, i: (group(d, i), 2)),
                  pl.BlockSpec((1, RET_CHUNK, RET_WIDTH), lambda d, i: (d, 0, 0)),
                  pl.BlockSpec((1, RET_WIDTH, 1), lambda d, i: (d, 0, 0)),
                  pl.BlockSpec((1, RET_WIDTH, RET_HEAD_DIM), lambda d, i: (d, 0, 0))],
        out_specs=(pl.BlockSpec((1, cps, RET_WIDTH, RET_HEAD_DIM), lambda d, i: (d, group(d, i), 0, 0)),
                   pl.BlockSpec((1, RET_WIDTH, RET_HEAD_DIM), lambda d, i: (d, 0, 0))),
        scratch_shapes=[pltpu.VMEM((RET_WIDTH, RET_HEAD_DIM), F32)],
        compiler_params=_cparams("arbitrary", "arbitrary"),
        name="retention_states",
    )(p_ret, p_ret, zeta, chunk_decay, s0)


def _ret_out_kernel(q_ref, k_ref, v_ref, g_ref, mask_ref, xi_ref, sf_ref, sb_ref, y_ref, *, cpo):
    for c in range(cpo):
        rows = slice(c * RET_CHUNK, (c + 1) * RET_CHUNK)
        q = q_ref[rows, :]
        k = (k_ref[rows, :] * (RET_HEAD_DIM ** -0.5)).astype(BF16)
        v = v_ref[rows, :].astype(BF16)
        qb = q.astype(BF16)
        qf = (q * xi_ref[0]).astype(BF16)
        qr = (q * xi_ref[1]).astype(BF16)
        sf = sf_ref[0, c].astype(BF16)
        sb = sb_ref[0, c].astype(BF16)
        ys = []
        for h in range(RET_HEADS):
            sl = slice(h * RET_HEAD_DIM, (h + 1) * RET_HEAD_DIM)
            scores = lax.dot_general(qb[:, sl], k[:, sl], (((1,), (1,)), ((), ())),
                                     preferred_element_type=F32)
            y = _dot((scores * mask_ref[h]).astype(BF16), v[:, sl])
            y = y + _dot(qf[:, sl], sf[sl, :]) + _dot(qr[:, sl], sb[sl, :])
            mu = jnp.mean(y, axis=-1, keepdims=True)
            yc = y - mu
            var = jnp.mean(yc * yc, axis=-1, keepdims=True)
            ys.append(yc * lax.rsqrt(var + EPS))
        y_ref[rows, :] = (_silu(g_ref[rows, :]) * jnp.concatenate(ys, axis=1)).astype(y_ref.dtype)


def _retention_outputs(p_ret, mask, xi, sprev):
    nc = p_ret.shape[0] // RET_CHUNK
    cpo = min(nc, RET_OUT_CHUNKS)
    rows = cpo * RET_CHUNK
    col = lambda c: pl.BlockSpec((rows, RET_WIDTH), lambda i: (i, c))
    return pl.pallas_call(
        functools.partial(_ret_out_kernel, cpo=cpo),
        out_shape=jax.ShapeDtypeStruct((p_ret.shape[0], RET_WIDTH), BF16),
        grid=(nc // cpo,),
        in_specs=[col(0), col(1), col(2), col(3),
                  pl.BlockSpec((RET_HEADS, RET_CHUNK, RET_CHUNK), lambda i: (0, 0, 0)),
                  pl.BlockSpec((2, RET_CHUNK, RET_WIDTH), lambda i: (0, 0, 0)),
                  pl.BlockSpec((1, cpo, RET_WIDTH, RET_HEAD_DIM), lambda i: (0, i, 0, 0)),
                  pl.BlockSpec((1, cpo, RET_WIDTH, RET_HEAD_DIM), lambda i: (1, i, 0, 0))],
        out_specs=pl.BlockSpec((rows, RET_WIDTH), lambda i: (i, 0)),
        compiler_params=_cparams("parallel"),
        name="retention_outputs",
    )(p_ret, p_ret, p_ret, p_ret, mask, xi, sprev, sprev)


def _hy_filter_kernel(feat_ref, w1_ref, b1_ref, w2_ref, b2_ref, w3_ref, b3_ref, freq_ref,
                      delta_ref, taps_ref, asum_ref, *, seq_len):
    i = pl.program_id(0)
    feat = feat_ref[...]
    tm = feat.shape[0]
    half = tm // 2
    freq = freq_ref[...]
    packed = jnp.concatenate([feat[:half], feat[half:]], axis=1)
    h = jnp.sin(freq * (_dot3(packed, w1_ref[...]) + b1_ref[...]))
    h = jnp.sin(freq * (_dot3(h, w2_ref[...]) + b2_ref[...]))
    h = _dot3(h, w3_ref[0])
    h = jnp.concatenate([h[:, :HY_WIDTH], h[:, HY_WIDTH:]], axis=0) + b3_ref[0]
    j = i * tm + lax.broadcasted_iota(jnp.int32, (tm, HY_WIDTH), 0)
    taps = jnp.where(j == seq_len, 0.0, h * jnp.exp(-feat[:, 0:1] * delta_ref[...]))
    taps_ref[...] = taps

    @pl.when(i == 0)
    def _():
        asum_ref[...] = jnp.zeros_like(asum_ref)

    asum_ref[...] += jnp.sum(jnp.abs(taps).reshape(tm // SUBLANES, SUBLANES, HY_WIDTH), axis=0)


def _position_features(seq_len):
    lo = 256
    hi = seq_len // lo + 1
    bands = np.linspace(1e-4, HY_POS_BANDS - 1, HY_POS_BANDS).astype(np.float32).astype(np.float64)
    ang_hi = (2.0 * np.pi / seq_len) * lo * np.arange(hi)[:, None] * bands[None, :]
    ang_lo = (2.0 * np.pi / seq_len) * np.arange(lo)[:, None] * bands[None, :]
    tab = lambda a: jnp.asarray(a, F32)
    ch, sh = tab(np.cos(ang_hi))[:, None, :], tab(np.sin(ang_hi))[:, None, :]
    cl, sl = tab(np.cos(ang_lo))[None, :, :], tab(np.sin(ang_lo))[None, :, :]
    npos = seq_len + 1
    cos = (ch * cl - sh * sl).reshape(hi * lo, HY_POS_BANDS)[:npos]
    sin = (sh * cl + ch * sl).reshape(hi * lo, HY_POS_BANDS)[:npos]
    c_end, s_end = tab(np.cos(2.0 * np.pi * bands))[None, :], tab(np.sin(2.0 * np.pi * bands))[None, :]
    cos_b = c_end * cos[1:seq_len] + s_end * sin[1:seq_len]
    sin_b = s_end * cos[1:seq_len] - c_end * sin[1:seq_len]
    lag = jnp.concatenate([jnp.arange(npos, dtype=F32), jnp.arange(seq_len - 1, 0, -1, dtype=F32)])
    t = lag / max(seq_len - 1, 1)
    cos = jnp.concatenate([cos, cos_b], axis=0)
    sin = jnp.concatenate([sin, sin_b], axis=0)
    return jnp.concatenate([t[:, None], cos, -sin,
                            jnp.zeros((2 * seq_len, LANES - 1 - 2 * HY_POS_BANDS), F32)], axis=-1)


def _hyena_taps(seq_len, w1, b1, w2, b2, w3, b3, freq):
    n = 2 * seq_len
    feat = _position_features(seq_len)
    w1p = jnp.concatenate([w1, jnp.zeros((LANES - w1.shape[0], w1.shape[1]), F32)], axis=0)
    pair = lambda w: jnp.concatenate(
        [jnp.concatenate([w, jnp.zeros_like(w)], axis=1),
         jnp.concatenate([jnp.zeros_like(w), w], axis=1)], axis=0)
    twice = lambda v: jnp.tile(v.reshape(1, -1), (1, 2))
    w3_dirs = jnp.stack([pair(w3[:, :HY_WIDTH]), pair(w3[:, HY_WIDTH:])])
    b3_dirs = b3.reshape(2, 1, HY_WIDTH)
    deltas = jnp.abs(jnp.linspace(math.log(HY_DECAY_TARGET) / HY_LONG_DECAY_PCT,
                                  math.log(HY_DECAY_TARGET) / HY_SHORT_DECAY_PCT,
                                  HY_WIDTH, dtype=F32)).reshape(1, HY_WIDTH)
    tm = min(seq_len, 1024)
    per_dir = seq_len // tm
    hid2 = 2 * w1.shape[1]
    const = lambda shape: pl.BlockSpec(shape, lambda i: (0, 0))
    by_dir = lambda shape: pl.BlockSpec(shape, lambda i: (i // per_dir, 0, 0))
    return pl.pallas_call(
        functools.partial(_hy_filter_kernel, seq_len=seq_len),
        out_shape=(jax.ShapeDtypeStruct((n, HY_WIDTH), F32),
                   jax.ShapeDtypeStruct((8, HY_WIDTH), F32)),
        grid=(n // tm,),
        in_specs=[pl.BlockSpec((tm, LANES), lambda i: (i, 0)),
                  const((2 * LANES, hid2)), const((1, hid2)), const((hid2, hid2)), const((1, hid2)),
                  by_dir((1, hid2, 2 * HY_WIDTH)), by_dir((1, 1, HY_WIDTH)), const((1, hid2)),
                  const((1, HY_WIDTH))],
        out_specs=(pl.BlockSpec((tm, HY_WIDTH), lambda i: (i, 0)), const((8, HY_WIDTH))),
        compiler_params=_cparams("arbitrary"),
        name="hyena_taps",
    )(feat, pair(w1p), twice(b1), pair(w2), twice(b2), w3_dirs, b3_dirs, twice(freq), deltas)


FFT_N2 = 256
FFT_K1_PER_STEP = 8
FFT_N2_PER_STEP = 16


def _kept_slabs(n1):
    kept = n1 // 2 + 1
    return kept, -(-kept // SUBLANES) * SUBLANES


def _dft_tables(n_total, n1_in):
    n1 = n_total // FFT_N2
    kept, padded = _kept_slabs(n1)
    k1 = np.arange(padded)
    a = 2.0 * np.pi * np.outer(k1, np.arange(n1_in)) / n1
    live = (k1 < kept)[:, None]
    lvl1 = np.concatenate([np.cos(a) * live, -np.sin(a) * live], axis=0)
    n2 = np.arange(FFT_N2)
    th = 2.0 * np.pi * np.outer(n2, n2) / FFT_N2
    fc, fs = np.cos(th), np.sin(th)
    g = np.block([[fc, fs], [-fs, fc]])
    ph = 2.0 * np.pi * np.outer(k1, n2) / n_total
    tw = np.stack([np.cos(ph), np.sin(ph)])[..., None]
    return (jnp.asarray(lvl1, F32), jnp.asarray(g, F32), jnp.asarray(g.T, F32), jnp.asarray(tw, F32))


def _inverse_lvl1_table(n_total, n1_out):
    n1 = n_total // FFT_N2
    kept, padded = _kept_slabs(n1)
    k1 = np.arange(padded)
    a = 2.0 * np.pi * np.outer(np.arange(n1_out), k1) / n1
    weight = np.where(k1 >= kept, 0.0, np.where((k1 == 0) | (k1 == n1 // 2), 1.0, 2.0))[None, :]
    return jnp.asarray(np.concatenate([np.cos(a) * weight, -np.sin(a) * weight], axis=1), F32)


def _lvl2_kernel(b_ref, tw_ref, gh_ref, gl_ref, *rest, inverse):
    for j in range(b_ref.shape[1]):
        c = tw_ref[0, j]
        s = tw_ref[1, j]
        br = b_ref[0, j]
        bi = b_ref[1, j]
        d = jnp.concatenate([c * br + s * bi, c * bi - s * br], axis=0)
        dh, dl = _split(d)
        x = _dot(gh_ref[...], dh) + _dot(gh_ref[...], dl) + _dot(gl_ref[...], dh)
        if not inverse:
            (o_ref,) = rest
            o_ref[0, j] = x[:FFT_N2]
            o_ref[1, j] = x[FFT_N2:]
            continue
        h_ref, gth_ref, gtl_ref, o_ref = rest
        xr, xi = x[:FFT_N2], x[FFT_N2:]
        hr, hi = h_ref[0, j], h_ref[1, j]
        y = jnp.concatenate([xr * hr - xi * hi, xr * hi + xi * hr], axis=0)
        yh, yl = _split(y)
        cc = _dot(gth_ref[...], yh) + _dot(gth_ref[...], yl) + _dot(gtl_ref[...], yh)
        cr, ci = cc[:FFT_N2], cc[FFT_N2:]
        o_ref[0, j] = c * cr - s * ci
        o_ref[1, j] = c * ci + s * cr


def _level2(b, tw, g, h=None, gt=None):
    _, n1, n2, ch = b.shape
    kb = FFT_K1_PER_STEP
    blk = pl.BlockSpec((2, kb, n2, ch), lambda i: (0, i, 0, 0))
    const = pl.BlockSpec((2 * n2, 2 * n2), lambda i: (0, 0))
    gh, gl = _split(g)
    in_specs = [blk, pl.BlockSpec((2, kb, n2, 1), lambda i: (0, i, 0, 0)), const, const]
    args = [b, tw, gh, gl]
    if h is not None:
        gth, gtl = _split(gt)
        in_specs += [blk, const, const]
        args += [h, gth, gtl]
    return pl.pallas_call(
        functools.partial(_lvl2_kernel, inverse=h is not None),
        out_shape=jax.ShapeDtypeStruct(b.shape, F32),
        grid=(n1 // kb,),
        in_specs=in_specs,
        out_specs=blk,
        compiler_params=_cparams("parallel"),
        name="hyena_fft_level2" + ("_conv" if h is not None else ""),
    )(*args)


def _lvl1_kernel(a_ref, x_ref, o_ref, *, inverse):
    tn2 = x_ref.shape[-2]
    ch = x_ref.shape[-1]
    if inverse:
        x = jnp.concatenate([x_ref[:, :, j, :].reshape(-1, ch) for j in range(tn2)], axis=1)
    else:
        x = jnp.concatenate([x_ref[:, j, :] for j in range(tn2)], axis=1)
    out = _dot3(a_ref[...], x)
    for j in range(tn2):
        slab = out[:, j * ch:(j + 1) * ch]
        if inverse:
            o_ref[:, j, :] = slab
        else:
            o_ref[:, :, j, :] = slab.reshape(2, slab.shape[0] // 2, ch)


def _level1(a, x, name):
    n1_in, n2, ch = x.shape
    n1 = a.shape[0] // 2
    tn2 = FFT_N2_PER_STEP
    return pl.pallas_call(
        functools.partial(_lvl1_kernel, inverse=False),
        out_shape=jax.ShapeDtypeStruct((2, n1, n2, ch), F32),
        grid=(n2 // tn2,),
        in_specs=[pl.BlockSpec(a.shape, lambda i: (0, 0)),
                  pl.BlockSpec((n1_in, tn2, ch), lambda i: (0, i, 0))],
        out_specs=pl.BlockSpec((2, n1, tn2, ch), lambda i: (0, 0, i, 0)),
        compiler_params=_cparams("parallel"),
        name=name,
    )(a, x)


def _inverse_level1(a, x):
    _, n1, n2, ch = x.shape
    n1_out = a.shape[0]
    tn2 = FFT_N2_PER_STEP
    return pl.pallas_call(
        functools.partial(_lvl1_kernel, inverse=True),
        out_shape=jax.ShapeDtypeStruct((n1_out, n2, ch), F32),
        grid=(n2 // tn2,),
        in_specs=[pl.BlockSpec(a.shape, lambda i: (0, 0)),
                  pl.BlockSpec((2, n1, tn2, ch), lambda i: (0, 0, i, 0))],
        out_specs=pl.BlockSpec((n1_out, tn2, ch), lambda i: (0, i, 0)),
        compiler_params=_cparams("parallel"),
        name="hyena_fft_inverse_level1",
    )(a, x)


def _hyena_long_conv(z, taps):
    seq_len, ch = z.shape
    n = 2 * seq_len
    n1 = n // FFT_N2
    lvl1_z, g, gt, tw = _dft_tables(n, n1 // 2)
    lvl1_t = _dft_tables(n, n1)[0]
    inv1 = _inverse_lvl1_table(n, n1 // 2)
    hb = _level1(lvl1_t, taps.reshape(n1, FFT_N2, ch), "hyena_fft_level1_taps")
    hspec = _level2(hb, tw, g)
    zb = _level1(lvl1_z, z.reshape(n1 // 2, FFT_N2, ch), "hyena_fft_level1_z")
    cspec = _level2(zb, tw, g, hspec, gt)
    return _inverse_level1(inv1, cspec).reshape(seq_len, ch)


def _small_conv_kernel(z_ref, taps_ref, fc_ref, fs_ref, o_ref):
    seq_len = z_ref.shape[0]
    fc = fc_ref[...]
    fs = fs_ref[...]
    z = z_ref[...]
    taps = taps_ref[...]
    zr = _dot3(fc[:, :seq_len], z)
    zi = -_dot3(fs[:, :seq_len], z)
    hr = _dot3(fc, taps)
    hi = -_dot3(fs, taps)
    yr = zr * hr - zi * hi
    yi = zr * hi + zi * hr
    o_ref[...] = _dot3(fc[:seq_len, :], yr) - _dot3(fs[:seq_len, :], yi)


def _hyena_small_conv(z, taps):
    seq_len, ch = z.shape
    n = 2 * seq_len
    th = 2.0 * np.pi * np.outer(np.arange(n), np.arange(n)) / n
    full = lambda shape: pl.BlockSpec(shape, lambda: (0, 0))
    return pl.pallas_call(
        _small_conv_kernel,
        out_shape=jax.ShapeDtypeStruct((seq_len, ch), F32),
        in_specs=[full((seq_len, ch)), full((n, ch)), full((n, n)), full((n, n))],
        out_specs=full((seq_len, ch)),
        compiler_params=pltpu.CompilerParams(vmem_limit_bytes=VMEM_LIMIT),
        name="hyena_small_conv",
    )(z, taps, jnp.asarray(np.cos(th), F32), jnp.asarray(np.sin(th), F32))


def _hyena_mixer(x0, z, filt, bias):
    seq_len = x0.shape[0]
    taps, asum = _hyena_taps(seq_len, *filt)
    if 2 * seq_len // FFT_N2 >= 16:
        conv = _hyena_long_conv(z, taps)
    else:
        conv = _hyena_small_conv(z, taps)
    return x0, z, conv, asum, bias.reshape(1, HY_WIDTH)


def _mixed_residual(x0_ref, z_ref, conv_ref, asum_ref, bias_ref, ret_ref, att_ref, wo_ref, x_ref,
                    mgate_ref):
    norm = jnp.sum(asum_ref[...], axis=0, keepdims=True) + EPS
    conv = conv_ref[...] * (1.0 / (2 * x_ref.shape[0] * pl.num_programs(0) * norm))
    y_hy = (x0_ref[...] * (conv + bias_ref[...] * z_ref[...])).astype(BF16)
    y = (_dot(y_hy, wo_ref[:HY_WIDTH, :])
         + _dot(ret_ref[...], wo_ref[HY_WIDTH:HY_WIDTH + RET_WIDTH, :])
         + _dot(att_ref[...], wo_ref[HY_WIDTH + RET_WIDTH:, :]))
    return x_ref[...] + mgate_ref[...] * y


N_MIXED = 10


def _mixer_specs(tm, index):
    row = lambda w: pl.BlockSpec((tm, w), lambda i, *_: (i, 0))
    return [row(HY_WIDTH), row(HY_WIDTH), row(HY_WIDTH),
            pl.BlockSpec((SUBLANES, HY_WIDTH), index), pl.BlockSpec((1, HY_WIDTH), index),
            row(RET_WIDTH), row(ATT_WIDTH),
            pl.BlockSpec((MIX_WIDTH, D_MODEL), index), row(D_MODEL),
            pl.BlockSpec((1, D_MODEL), index)]


def _swiglu(x, wg_ref, wu_ref, wd_ref):
    hidden = wg_ref.shape[1]
    y = None
    for lo in range(0, hidden, SWIGLU_CHUNK):
        hi = min(lo + SWIGLU_CHUNK, hidden)
        a = _silu(_dot(x, wg_ref[:, lo:hi])) * _dot(x, wu_ref[:, lo:hi])
        part = _dot(a.astype(BF16), wd_ref[lo:hi, :])
        y = part if y is None else y + part
    return y


def _ffn_kernel(*refs):
    g_ref, sh_ref, sc_ref, gate_ref, wg_ref, wu_ref, wd_ref, o_ref = refs[N_MIXED:]
    x = _mixed_residual(*refs[:N_MIXED])
    h = _norm_mod(x, g_ref[...], sh_ref[...], sc_ref[...]).astype(BF16)
    o_ref[...] = x + gate_ref[...] * _swiglu(h, wg_ref, wu_ref, wd_ref)


def _dense_ffn(mixed, g, shift, scale, gate, wg, wu, wd):
    t = mixed[N_MIXED - 2].shape[0]
    tm = min(t, 512)
    index = lambda i: (0, 0)
    vec = pl.BlockSpec((1, D_MODEL), index)
    resident = lambda shape: pl.BlockSpec(shape, index, pipeline_mode=pl.Buffered(1))
    return pl.pallas_call(
        _ffn_kernel,
        out_shape=jax.ShapeDtypeStruct((t, D_MODEL), F32),
        grid=(t // tm,),
        in_specs=_mixer_specs(tm, index) + [vec, vec, vec, vec,
                  resident((D_MODEL, FFN_HIDDEN)), resident((D_MODEL, FFN_HIDDEN)),
                  resident((FFN_HIDDEN, D_MODEL))],
        out_specs=pl.BlockSpec((tm, D_MODEL), lambda i: (i, 0)),
        compiler_params=_cparams("parallel"),
        name="dense_ffn",
    )(*mixed, g, shift, scale, gate, wg, wu, wd)


def _moe_route_kernel(*refs):
    g_ref, sh_ref, sc_ref, rw_ref, rb_ref, xo_ref, h_ref, info_ref = refs[N_MIXED:]
    x = _mixed_residual(*refs[:N_MIXED])
    xo_ref[...] = x
    h = _norm_mod(x, g_ref[...], sh_ref[...], sc_ref[...])
    _to_row_tiles(h_ref, h)
    logits = _dot3(h, rw_ref[...]) + rb_ref[...]
    lane = lax.broadcasted_iota(jnp.int32, logits.shape, 1)
    v1 = jnp.max(logits, axis=-1, keepdims=True)
    i1 = jnp.min(jnp.where(logits == v1, lane, LANES), axis=-1, keepdims=True)
    rest = jnp.where(lane == i1, -jnp.inf, logits)
    v2 = jnp.max(rest, axis=-1, keepdims=True)
    i2 = jnp.min(jnp.where(rest == v2, lane, LANES), axis=-1, keepdims=True)
    e = jnp.exp(v2 - v1)
    g1 = 1.0 / (1.0 + e)
    g2 = e * g1
    info_ref[...] = jnp.where(lane == 0, i1.astype(F32),
                              jnp.where(lane == 1, i2.astype(F32),
                                        jnp.where(lane == 2, g1, jnp.where(lane == 3, g2, 0.0))))


def _moe_route(mixed, g, shift, scale, router_w, router_b):
    t = mixed[N_MIXED - 2].shape[0]
    tm = min(t, 512)
    rw = jnp.concatenate([router_w, jnp.zeros((D_MODEL, LANES - N_EXPERTS), F32)], axis=1)
    rb = jnp.concatenate([router_b, jnp.full((LANES - N_EXPERTS,), -jnp.inf, F32)]).reshape(1, LANES)
    vec = pl.BlockSpec((1, D_MODEL), lambda i: (0, 0))
    return pl.pallas_call(
        _moe_route_kernel,
        out_shape=(jax.ShapeDtypeStruct((t, D_MODEL), F32),
                   jax.ShapeDtypeStruct((t * SUBLANES, LANES), F32),
                   jax.ShapeDtypeStruct((t, LANES), F32)),
        grid=(t // tm,),
        in_specs=_mixer_specs(tm, lambda i: (0, 0)) + [vec, vec, vec,
                  pl.BlockSpec((D_MODEL, LANES), lambda i: (0, 0)),
                  pl.BlockSpec((1, LANES), lambda i: (0, 0))],
        out_specs=(pl.BlockSpec((tm, D_MODEL), lambda i: (i, 0)),
                   pl.BlockSpec((tm * SUBLANES, LANES), lambda i: (i, 0)),
                   pl.BlockSpec((tm, LANES), lambda i: (i, 0))),
        compiler_params=_cparams("parallel"),
        name="moe_route",
    )(*mixed, g, shift, scale, rw, rb)


def _to_row_tiles(ref, x):
    rows = x.shape[0]
    for k in range(D_MODEL // LANES):
        ref[pl.ds(k, rows, stride=SUBLANES), :] = x[:, k * LANES:(k + 1) * LANES]


def _from_row_tiles(ref, rows):
    return jnp.concatenate(
        [ref[pl.ds(k, rows, stride=SUBLANES), :] for k in range(D_MODEL // LANES)], axis=1)


def _row_copy(src_hbm, dst_vmem, sem, src_row, dst_row):
    src = pl.multiple_of(src_row * SUBLANES, SUBLANES)
    dst = pl.multiple_of(dst_row * SUBLANES, SUBLANES)
    return pltpu.make_async_copy(src_hbm.at[pl.ds(src, SUBLANES)], dst_vmem.at[pl.ds(dst, SUBLANES)], sem)


def _moe_expert_kernel(blk_e_ref, nused_ref, dest_ref, fill_ref, h_hbm, wg_ref, wu_ref, wd_ref,
                       y_ref, xbuf_ref, src_ref, sem):
    b = pl.program_id(0)
    n_used = nused_ref[0]

    @pl.when(b == 0)
    def _():
        def invert(t, carry):
            for j in range(TOP_K):
                src_ref[dest_ref[TOP_K * t + j]] = t
            return carry

        lax.fori_loop(0, dest_ref.shape[0] // TOP_K, invert, 0, unroll=DMA_UNROLL)
        for e in range(N_EXPERTS):
            def pad(d, carry):
                src_ref[d] = 0
                return carry

            lax.fori_loop(fill_ref[0, e], fill_ref[1, e], pad, 0)

    def issue_rows(blk, slot):
        def body(r, carry):
            _row_copy(h_hbm, xbuf_ref.at[slot], sem.at[slot], src_ref[blk * MOE_BLOCK + r], r).start()
            return carry

        lax.fori_loop(0, MOE_BLOCK, body, 0, unroll=DMA_UNROLL)

    def wait_rows(slot):
        def body(r, carry):
            _row_copy(h_hbm, xbuf_ref.at[slot], sem.at[slot], 0, r).wait()
            return carry

        lax.fori_loop(0, MOE_BLOCK, body, 0, unroll=DMA_UNROLL)

    @pl.when(b < n_used)
    def _():
        slot = b % 2

        @pl.when(b == 0)
        def _():
            issue_rows(0, 0)

        wait_rows(slot)

        @pl.when(b + 1 < n_used)
        def _():
            issue_rows(b + 1, 1 - slot)

        x = _from_row_tiles(xbuf_ref.at[slot], MOE_BLOCK).astype(BF16)
        _to_row_tiles(y_ref, _swiglu(x, wg_ref.at[0], wu_ref.at[0], wd_ref.at[0]))

    @pl.when(b >= n_used)
    def _():
        y_ref[...] = jnp.zeros_like(y_ref)


def _moe_experts(h, blk_expert, n_used, dest, fill, wg, wu, wd, n_blk):
    def w_index(b, be, nu, de, fi):
        return be[jnp.minimum(b, nu[0] - 1)], 0, 0

    grid_spec = pltpu.PrefetchScalarGridSpec(
        num_scalar_prefetch=4,
        grid=(n_blk,),
        in_specs=[pl.BlockSpec(memory_space=pl.ANY),
                  pl.BlockSpec((1, D_MODEL, FFN_HIDDEN), w_index),
                  pl.BlockSpec((1, D_MODEL, FFN_HIDDEN), w_index),
                  pl.BlockSpec((1, FFN_HIDDEN, D_MODEL), w_index)],
        out_specs=pl.BlockSpec((MOE_BLOCK * SUBLANES, LANES), lambda b, be, nu, de, fi: (b, 0)),
        scratch_shapes=[pltpu.VMEM((2, MOE_BLOCK * SUBLANES, LANES), F32),
                        pltpu.SMEM((n_blk * MOE_BLOCK,), jnp.int32),
                        pltpu.SemaphoreType.DMA((2,))],
    )
    return pl.pallas_call(
        _moe_expert_kernel,
        out_shape=jax.ShapeDtypeStruct((n_blk * MOE_BLOCK * SUBLANES, LANES), F32),
        grid_spec=grid_spec,
        compiler_params=_cparams("arbitrary"),
        name="moe_experts",
    )(blk_expert, n_used, dest, fill, h, wg, wu, wd)


def _moe_combine_kernel(dest_ref, x_ref, info_ref, gate_ref, ng_ref, y_hbm, o_ref,
                        y1_ref, y2_ref, sem, *, final_norm):
    i = pl.program_id(0)
    tm = x_ref.shape[0]
    slot = i % 2

    def issue_rows(blk, slot):
        def body(r, carry):
            a = 2 * (blk * tm + r)
            _row_copy(y_hbm, y1_ref.at[slot], sem.at[slot], dest_ref[a], r).start()
            _row_copy(y_hbm, y2_ref.at[slot], sem.at[slot], dest_ref[a + 1], r).start()
            return carry

        lax.fori_loop(0, tm, body, 0, unroll=DMA_UNROLL)

    def wait_rows(slot):
        def body(r, carry):
            _row_copy(y_hbm, y1_ref.at[slot], sem.at[slot], 0, r).wait()
            _row_copy(y_hbm, y2_ref.at[slot], sem.at[slot], 0, r).wait()
            return carry

        lax.fori_loop(0, tm, body, 0, unroll=DMA_UNROLL)

    @pl.when(i == 0)
    def _():
        issue_rows(0, 0)

    wait_rows(slot)

    @pl.when(i + 1 < pl.num_programs(0))
    def _():
        issue_rows(i + 1, 1 - slot)

    info = info_ref[...]
    y = (_from_row_tiles(y1_ref.at[slot], tm) * info[:, 2:3]
         + _from_row_tiles(y2_ref.at[slot], tm) * info[:, 3:4])
    x = x_ref[...] + gate_ref[...] * y
    if final_norm:
        ms = jnp.mean(x * x, axis=-1, keepdims=True)
        x = x * lax.rsqrt(ms + EPS) * ng_ref[...]
    o_ref[...] = x


def _moe_combine(dest, x, info, gate, norm_g, y, final_norm):
    t = x.shape[0]
    tm = min(t, 256)
    vec = pl.BlockSpec((1, D_MODEL), lambda i, d: (0, 0))
    grid_spec = pltpu.PrefetchScalarGridSpec(
        num_scalar_prefetch=1,
        grid=(t // tm,),
        in_specs=[pl.BlockSpec((tm, D_MODEL), lambda i, d: (i, 0)),
                  pl.BlockSpec((tm, LANES), lambda i, d: (i, 0)), vec, vec,
                  pl.BlockSpec(memory_space=pl.ANY)],
        out_specs=pl.BlockSpec((tm, D_MODEL), lambda i, d: (i, 0)),
        scratch_shapes=[pltpu.VMEM((2, tm * SUBLANES, LANES), F32),
                        pltpu.VMEM((2, tm * SUBLANES, LANES), F32),
                        pltpu.SemaphoreType.DMA((2,))],
    )
    return pl.pallas_call(
        functools.partial(_moe_combine_kernel, final_norm=final_norm),
        out_shape=jax.ShapeDtypeStruct((t, D_MODEL), F32),
        grid_spec=grid_spec,
        compiler_params=_cparams("arbitrary"),
        name="moe_combine",
    )(dest, x, info, gate, norm_g, y)


def _moe_layer(mixed, g, shift, scale, gate, router_w, router_b, wg, wu, wd, norm_g, final_norm):
    t = mixed[N_MIXED - 2].shape[0]
    n_asg = t * TOP_K
    x, h, info = _moe_route(mixed, g, shift, scale, router_w, router_b)
    expert = info[:, :TOP_K].astype(jnp.int32).reshape(-1)
    onehot = (expert[:, None] == jnp.arange(N_EXPERTS, dtype=jnp.int32)[None, :]).astype(jnp.int32)
    csum = jnp.cumsum(onehot, axis=0)
    counts = csum[-1]
    padded = (counts + MOE_BLOCK - 1) // MOE_BLOCK * MOE_BLOCK
    pad_end = jnp.cumsum(padded)
    pad_start = pad_end - padded
    dest = jnp.sum(onehot * (csum - 1 + pad_start[None, :]), axis=1).astype(jnp.int32)
    n_blk = -(-n_asg // MOE_BLOCK) + N_EXPERTS
    blk_start = jnp.arange(n_blk, dtype=jnp.int32) * MOE_BLOCK
    blk_expert = jnp.minimum(jnp.sum(blk_start[:, None] >= pad_end[None, :], axis=1),
                             N_EXPERTS - 1).astype(jnp.int32)
    n_used = (pad_end[-1:] // MOE_BLOCK).astype(jnp.int32)
    fill = jnp.stack([pad_start + counts, pad_end]).astype(jnp.int32)
    y = _moe_experts(h, blk_expert, n_used, dest, fill, wg, wu, wd, n_blk)
    return _moe_combine(dest, x, info, gate, norm_g, y, final_norm)


def _final_norm_kernel(x_ref, g_ref, o_ref):
    x = x_ref[...]
    ms = jnp.mean(x * x, axis=-1, keepdims=True)
    o_ref[...] = x * lax.rsqrt(ms + EPS) * g_ref[...]


def _final_norm(x, g):
    t = x.shape[0]
    tm = min(t, 1024)
    return pl.pallas_call(
        _final_norm_kernel,
        out_shape=jax.ShapeDtypeStruct((t, D_MODEL), F32),
        grid=(t // tm,),
        in_specs=[pl.BlockSpec((tm, D_MODEL), lambda i: (i, 0)),
                  pl.BlockSpec((1, D_MODEL), lambda i: (0, 0))],
        out_specs=pl.BlockSpec((tm, D_MODEL), lambda i: (i, 0)),
        compiler_params=_cparams("parallel"),
        name="final_norm",
    )(x, g)


def kernel(x, c, ctx, c_ctx, ada_w, ada_b, norm1_g, norm2_g, w_in, w_out, hy_conv_w, hy_conv_b, hy_filt_w1, hy_filt_b1, hy_filt_w2, hy_filt_b2, hy_filt_w3, hy_filt_b3, hy_filt_freq, hy_bias, ret_log_rate, attn_q_g, attn_k_g, ffn_w_gate, ffn_w_up, ffn_w_down, moe_router_w, moe_router_b, moe_w_gate, moe_w_up, moe_w_down, final_norm_g):
    assert x.shape[0] == 1 and c.shape[0] == 1
    seq_len = x.shape[1]
    x_lat = x[0]
    x_ctx = ctx[0]
    cos_t, sin_t = _rope_tables(seq_len)
    cvec = jnp.concatenate([c, c_ctx[None, :], jnp.zeros((6, D_MODEL), F32)], axis=0)
    cvec = _silu(cvec)
    row = lambda v: v.reshape(1, D_MODEL)
    zero_state = jnp.zeros((2, RET_WIDTH, RET_HEAD_DIM), F32)
    zero_tab = jnp.zeros((x_ctx.shape[0], LANES), F32)

    for l in range(DEPTH):
        last = l == DEPTH - 1
        mods = _ada_modulation(cvec, ada_w, ada_b, l).reshape(8, 6, D_MODEL)
        mod = [row(mods[0, i]) for i in range(6)]
        mod_c = [row(mods[1, i]) for i in range(6)]
        filt = (hy_filt_w1[l], hy_filt_b1[l], hy_filt_w2[l], hy_filt_b2[l],
                hy_filt_w3[l], hy_filt_b3[l], hy_filt_freq[l])
        lg_f = -jnp.exp(ret_log_rate[l, 0].astype(F32))
        lg_b = -jnp.exp(ret_log_rate[l, 1].astype(F32))
        mask, zeta, xi, chunk_decay = _ret_tables(lg_f, lg_b)
        w_in_l = w_in[l].astype(BF16)
        w_out_l = w_out[l].astype(BF16)
        n1 = row(norm1_g[l])
        n2 = row(norm2_g[l])

        x0, z, p_ret, qt, k_aug, vt, kmax_sq = _in_projection(
            x_lat, n1, mod[0], mod[1], w_in_l, hy_conv_w[l], hy_conv_b[l], cos_t, sin_t,
            attn_q_g[l], attn_k_g[l], rope=True)
        x0_c, z_c, pc_ret, qct, kc, vct, _ = _in_projection(
            x_ctx, n1, mod_c[0], mod_c[1], w_in_l, hy_conv_w[l], hy_conv_b[l], zero_tab, zero_tab,
            attn_q_g[l], attn_k_g[l], rope=False)

        sprev_c, s_ctx = _retention_states(pc_ret, zeta, chunk_decay, zero_state)

        y_hy = _hyena_mixer(x0, z, filt, hy_bias[l])
        sprev, _ = _retention_states(p_ret, zeta, chunk_decay, s_ctx)
        y_ret = _retention_outputs(p_ret, mask, xi, sprev)
        kmax = jnp.sqrt(jnp.max(kmax_sq[:, 0, ::ATT_HEAD_DIM], axis=0)) * KMAX_SLACK
        y_att = _attention(qt, kc, vct, kmax, k_aug, vt, tq=ATT_TQ, tk=ATT_TK, sub=ATT_SUB)

        if not last:
            yc_hy = _hyena_mixer(x0_c, z_c, filt, hy_bias[l])
            yc_ret = _retention_outputs(pc_ret, mask, xi, sprev_c)
            yc_att = _attention(qct, kc, vct, tq=x_ctx.shape[0])
            mixed_c = (*yc_hy, yc_ret, yc_att, w_out_l, x_ctx, mod_c[2])
        mixed = (*y_hy, y_ret, y_att, w_out_l, x_lat, mod[2])

        i = l // 2
        if l % 2 == 0:
            wg, wu, wd = (ffn_w_gate[i].astype(BF16), ffn_w_up[i].astype(BF16),
                          ffn_w_down[i].astype(BF16))
            x_lat = _dense_ffn(mixed, n2, mod[3], mod[4], mod[5], wg, wu, wd)
            if not last:
                x_ctx = _dense_ffn(mixed_c, n2, mod_c[3], mod_c[4], mod_c[5], wg, wu, wd)
        else:
            wg, wu, wd = (moe_w_gate[i].astype(BF16), moe_w_up[i].astype(BF16),
                          moe_w_down[i].astype(BF16))
            fg = row(final_norm_g)
            x_lat = _moe_layer(mixed, n2, mod[3], mod[4], mod[5], moe_router_w[i], moe_router_b[i],
                               wg, wu, wd, fg, final_norm=last)
            if not last:
                x_ctx = _moe_layer(mixed_c, n2, mod_c[3], mod_c[4], mod_c[5], moe_router_w[i],
                                   moe_router_b[i], wg, wu, wd, fg, final_norm=False)
    if DEPTH % 2 == 1:
        x_lat = _final_norm(x_lat, row(final_norm_g))
    return x_lat[None]
```

```python
import functools
import math

import numpy as np
import jax
import jax.numpy as jnp
from jax import lax
from jax.experimental import pallas as pl
from jax.experimental.pallas import tpu as pltpu

F32 = jnp.float32
BF16 = jnp.bfloat16

D_MODEL = 1024
DEPTH = 2
GRID_W = 64
EPS = 1e-6

HY_WIDTH = 256
HY_COLS = 3 * HY_WIDTH
HY_POS_BANDS = 16
HY_DECAY_TARGET = 1e-2
HY_SHORT_DECAY_PCT = 0.3
HY_LONG_DECAY_PCT = 1.5

RET_HEAD_DIM = 64
RET_HEADS = 4
RET_WIDTH = RET_HEADS * RET_HEAD_DIM
RET_COLS = 4 * RET_WIDTH
RET_CHUNK = 128

ATT_HEAD_DIM = 64
ATT_HEADS = 8
ATT_KV_HEADS = 2
ATT_GROUP = ATT_HEADS // ATT_KV_HEADS
ATT_WIDTH = ATT_HEADS * ATT_HEAD_DIM
ATT_KV_WIDTH = ATT_KV_HEADS * ATT_HEAD_DIM
ATT_COLS = ATT_WIDTH + 2 * ATT_KV_WIDTH
ROPE_AXIS_DIM = ATT_HEAD_DIM // 2
ROPE_BASE = 10000.0

MIX_WIDTH = HY_WIDTH + RET_WIDTH + ATT_WIDTH
IN_COLS = HY_COLS + RET_COLS + ATT_COLS

FFN_HIDDEN = 2816
N_EXPERTS = 8
TOP_K = 2
MOE_BLOCK = 512

LOG2_E = 1.4426950408889634
EXP2_CAP = 60.0
KMAX_SLACK = 1.0 + 2.0 ** -7
ATT_TQ = 512
ATT_TK = 4096
ATT_SUB = 512
ATT_VROWS = 80
RET_STATE_CHUNKS = 16
RET_OUT_CHUNKS = 8
SWIGLU_CHUNK = 1024
DMA_UNROLL = 8

LANES = 128
SUBLANES = 8
VMEM_LIMIT = 56 * 1024 * 1024


def _cparams(*sem):
    return pltpu.CompilerParams(dimension_semantics=sem, vmem_limit_bytes=VMEM_LIMIT)


def _dot(a, b):
    return jnp.dot(a, b, preferred_element_type=F32)


def _split(a):
    hi = a.astype(BF16)
    lo = (a - hi.astype(F32)).astype(BF16)
    return hi, lo


def _dot3(a, b):
    ah, al = _split(a)
    bh, bl = _split(b)
    return _dot(ah, bh) + _dot(al, bh) + _dot(ah, bl)


def _silu(x):
    return x * (1.0 / (1.0 + jnp.exp(-x)))


def _norm_mod(x, g, shift, scale):
    ms = jnp.mean(x * x, axis=-1, keepdims=True)
    h = x * lax.rsqrt(ms + EPS) * g
    return h * (1.0 + scale) + shift


def _ada_kernel(c_ref, w_ref, b_ref, o_ref):
    o_ref[...] = _dot3(c_ref[...], w_ref[0]) + b_ref[0]


def _ada_modulation(cc, w, b, layer):
    n = w.shape[2]
    tn = 1536
    return pl.pallas_call(
        _ada_kernel,
        out_shape=jax.ShapeDtypeStruct((8, n), F32),
        grid=(n // tn,),
        in_specs=[pl.BlockSpec((8, D_MODEL), lambda j: (0, 0)),
                  pl.BlockSpec((1, D_MODEL, tn), lambda j: (layer, 0, j)),
                  pl.BlockSpec((1, 1, tn), lambda j: (layer, 0, j))],
        out_specs=pl.BlockSpec((8, tn), lambda j: (0, j)),
        compiler_params=_cparams("parallel"),
        name="ada_modulation",
    )(cc, w, b.reshape(b.shape[0], 1, n))


def _head_mean_matrix(width):
    idx = np.arange(width) // ATT_HEAD_DIM
    return jnp.asarray((idx[:, None] == idx[None, :]).astype(np.float32) / ATT_HEAD_DIM, BF16)


def _head_rms(x, bd, g):
    ms = _dot((x * x).astype(BF16), bd)
    return x * lax.rsqrt(ms + EPS) * g


def _rope(x, cos, sin):
    n = x.shape[1]
    lane = lax.broadcasted_iota(jnp.int32, x.shape, 1)
    swapped = jnp.where((lane % 32) < 16, pltpu.roll(x, n - 16, 1), pltpu.roll(x, 16, 1))
    return x * cos + swapped * sin


def _attention_operands(p, cos_ref, sin_ref, qg_ref, kg_ref, bdq_ref, bdk_ref,
                        qt_ref, k_ref, vt_ref, kmax_ref, rope):
    q = _head_rms(p[:, :ATT_WIDTH], bdq_ref[...], qg_ref[...])
    k = _head_rms(p[:, ATT_WIDTH:ATT_WIDTH + ATT_KV_WIDTH], bdk_ref[...], kg_ref[...])
    if rope:
        cos = cos_ref[...]
        sin = sin_ref[...]
        q = jnp.concatenate(
            [_rope(q[:, j * LANES:(j + 1) * LANES], cos, sin) for j in range(ATT_WIDTH // LANES)],
            axis=1)
        k = _rope(k, cos, sin)
    qt_ref[...] = (q * (ATT_HEAD_DIM ** -0.5 * LOG2_E)).T.astype(BF16)
    tm = k.shape[0]
    lane = lax.broadcasted_iota(jnp.int32, k.shape, 1)
    minus_one_col = jnp.where(lane == ATT_HEAD_DIM, -1.0, 0.0)
    kb = k.astype(BF16)
    k_ref[0] = jnp.where(lane < ATT_HEAD_DIM, k, minus_one_col).astype(BF16)
    k_ref[1] = jnp.where(lane < ATT_HEAD_DIM, pltpu.roll(k, ATT_HEAD_DIM, 1), minus_one_col).astype(BF16)
    norm_sq = _dot((kb.astype(F32) ** 2).astype(BF16), bdk_ref[...]) * ATT_HEAD_DIM
    kmax_ref[0] = jnp.broadcast_to(jnp.max(norm_sq, axis=0, keepdims=True), (SUBLANES, LANES))
    vt = p[:, ATT_WIDTH + ATT_KV_WIDTH:].T.astype(BF16)
    sub = lax.broadcasted_iota(jnp.int32, (ATT_VROWS - ATT_HEAD_DIM, tm), 0)
    ones_row = jnp.where(sub == 0, 1.0, 0.0).astype(BF16)
    for h in range(ATT_KV_HEADS):
        vt_ref[h * ATT_VROWS:h * ATT_VROWS + ATT_HEAD_DIM, :] = vt[h * ATT_HEAD_DIM:(h + 1) * ATT_HEAD_DIM]
        vt_ref[h * ATT_VROWS + ATT_HEAD_DIM:(h + 1) * ATT_VROWS, :] = ones_row


def _short_conv_gate(p, prev_row, next_row, w, b, x0_ref, z_ref):
    tm = p.shape[0]
    row = lax.broadcasted_iota(jnp.int32, p.shape, 0)
    up = jnp.where(row == 0, prev_row, pltpu.roll(p, 1, 0))
    dn = jnp.where(row == tm - 1, next_row, pltpu.roll(p, tm - 1, 0))
    u = up * w[0:1] + p * w[1:2] + dn * w[2:3] + b
    x0_ref[...] = u[:, :HY_WIDTH]
    z_ref[...] = u[:, HY_WIDTH:2 * HY_WIDTH] * u[:, 2 * HY_WIDTH:]


def _inproj_kernel(x_ref, xprev_ref, xnext_ref, g_ref, sh_ref, sc_ref, w_ref, cw_ref, cb_ref,
                   cos_ref, sin_ref, qg_ref, kg_ref, bdq_ref, bdk_ref,
                   x0_ref, z_ref, ret_ref, qt_ref, k_ref, vt_ref, kmax_ref, *, rope):
    i = pl.program_id(0)
    norm = lambda v: _norm_mod(v, g_ref[...], sh_ref[...], sc_ref[...]).astype(BF16)
    p = _dot(norm(x_ref[...]), w_ref[...])
    w_hy = w_ref[:, :HY_COLS]
    prev_row = jnp.where(i == 0, 0.0, _dot(norm(xprev_ref[...]), w_hy)[SUBLANES - 1:SUBLANES, :])
    next_row = jnp.where(i == pl.num_programs(0) - 1, 0.0, _dot(norm(xnext_ref[...]), w_hy)[0:1, :])
    _short_conv_gate(p[:, :HY_COLS], prev_row, next_row, cw_ref[...], cb_ref[...], x0_ref, z_ref)
    ret_ref[...] = p[:, HY_COLS:HY_COLS + RET_COLS]
    _attention_operands(p[:, HY_COLS + RET_COLS:], cos_ref, sin_ref, qg_ref, kg_ref, bdq_ref, bdk_ref,
                        qt_ref, k_ref, vt_ref, kmax_ref, rope)


def _in_projection(x, g, shift, scale, w_bf16, conv_w, conv_b, cos, sin, q_g, k_g, rope):
    t = x.shape[0]
    tm = min(t, 512)
    groups = tm // SUBLANES
    last_group = t // SUBLANES - 1
    qg = jnp.tile(q_g, ATT_HEADS).reshape(1, ATT_WIDTH)
    kg = jnp.tile(k_g, ATT_KV_HEADS).reshape(1, ATT_KV_WIDTH)
    vec = pl.BlockSpec((1, D_MODEL), lambda i: (0, 0))
    const = lambda shape: pl.BlockSpec(shape, lambda i: (0, 0))
    return pl.pallas_call(
        functools.partial(_inproj_kernel, rope=rope),
        out_shape=(jax.ShapeDtypeStruct((t, HY_WIDTH), F32),
                   jax.ShapeDtypeStruct((t, HY_WIDTH), F32),
                   jax.ShapeDtypeStruct((t, RET_COLS), F32),
                   jax.ShapeDtypeStruct((ATT_WIDTH, t), BF16),
                   jax.ShapeDtypeStruct((ATT_KV_HEADS, t, LANES), BF16),
                   jax.ShapeDtypeStruct((ATT_KV_HEADS * ATT_VROWS, t), BF16),
                   jax.ShapeDtypeStruct((t // tm, SUBLANES, LANES), F32)),
        grid=(t // tm,),
        in_specs=[pl.BlockSpec((tm, D_MODEL), lambda i: (i, 0)),
                  pl.BlockSpec((SUBLANES, D_MODEL), lambda i: (jnp.maximum(i * groups - 1, 0), 0)),
                  pl.BlockSpec((SUBLANES, D_MODEL),
                               lambda i: (jnp.minimum((i + 1) * groups, last_group), 0)),
                  vec, vec, vec, const((D_MODEL, IN_COLS)),
                  const((3, HY_COLS)), const((1, HY_COLS)),
                  pl.BlockSpec((tm, LANES), lambda i: (i, 0)),
                  pl.BlockSpec((tm, LANES), lambda i: (i, 0)),
                  const((1, ATT_WIDTH)), const((1, ATT_KV_WIDTH)),
                  const((ATT_WIDTH, ATT_WIDTH)), const((ATT_KV_WIDTH, ATT_KV_WIDTH))],
        out_specs=(pl.BlockSpec((tm, HY_WIDTH), lambda i: (i, 0)),
                   pl.BlockSpec((tm, HY_WIDTH), lambda i: (i, 0)),
                   pl.BlockSpec((tm, RET_COLS), lambda i: (i, 0)),
                   pl.BlockSpec((ATT_WIDTH, tm), lambda i: (0, i)),
                   pl.BlockSpec((ATT_KV_HEADS, tm, LANES), lambda i: (0, i, 0)),
                   pl.BlockSpec((ATT_KV_HEADS * ATT_VROWS, tm), lambda i: (0, i)),
                   pl.BlockSpec((1, SUBLANES, LANES), lambda i: (i, 0, 0))),
        compiler_params=_cparams("parallel"),
        name="in_projection",
    )(x, x, x, g, shift, scale, w_bf16, conv_w, conv_b.reshape(1, HY_COLS), cos, sin, qg, kg,
      _head_mean_matrix(ATT_WIDTH), _head_mean_matrix(ATT_KV_WIDTH))


def _rope_tables(n_tokens):
    rows = n_tokens // GRID_W
    inv_freq = ROPE_BASE ** (-jnp.arange(0, ROPE_AXIS_DIM, 2, dtype=F32) / ROPE_AXIS_DIM)
    ang_r = jnp.arange(rows, dtype=F32)[:, None] * inv_freq[None, :]
    ang_c = jnp.arange(GRID_W, dtype=F32)[:, None] * inv_freq[None, :]
    nf = inv_freq.shape[0]
    by_row = lambda tab: jnp.broadcast_to(tab[:, None, :], (rows, GRID_W, nf)).reshape(n_tokens, nf)
    by_col = lambda tab: jnp.broadcast_to(tab[None, :, :], (rows, GRID_W, nf)).reshape(n_tokens, nf)
    cr, sr = by_row(jnp.cos(ang_r)), by_row(jnp.sin(ang_r))
    cc, sc = by_col(jnp.cos(ang_c)), by_col(jnp.sin(ang_c))
    cos = jnp.concatenate([cr, cr, cc, cc], axis=1)
    sin = jnp.concatenate([-sr, sr, -sc, sc], axis=1)
    return jnp.tile(cos, (1, 2)), jnp.tile(sin, (1, 2))


def _attention_kernel(*refs, tq, tk, sub, n_lat):
    if n_lat:
        kmax_ref, qt_ref, kc_ref, vct_ref, k_ref, vt_ref, o_ref, qs_ref, acc_ref, m_ref = refs
    else:
        qt_ref, kc_ref, vct_ref, o_ref, acc_ref = refs
    cols = ATT_GROUP * tq
    qt = qt_ref[...]
    qst = jnp.concatenate(
        [qt[g * ATT_HEAD_DIM:(g + 1) * ATT_HEAD_DIM, :] for g in range(ATT_GROUP)], axis=1)
    pad = jnp.zeros((LANES - ATT_HEAD_DIM, cols), BF16)
    s = _dot(kc_ref[0], jnp.concatenate([qst, pad], axis=0))
    mt = jnp.max(s, axis=0, keepdims=True).astype(BF16)
    mtf = mt.astype(F32)
    acc_ref[...] = _dot(vct_ref[...], jnp.exp2(s - mtf).astype(BF16))

    if n_lat:
        row = lax.broadcasted_iota(jnp.int32, pad.shape, 0)
        qs_ref[...] = jnp.concatenate([qst, jnp.where(row == 0, mtf, 0.0).astype(BF16)], axis=0)
        qf = qst.astype(F32)
        qn = jnp.sqrt(jnp.sum(qf * qf, axis=0, keepdims=True))
        fast = jnp.max(qn * kmax_ref[pl.program_id(0)] - mtf) <= EXP2_CAP

        def tile(j):
            off = pl.multiple_of(j * tk, tk)
            return k_ref[0, pl.ds(off, tk), :], vt_ref[:, pl.ds(off, tk)]

        @pl.when(fast)
        def _():
            def body(j, carry):
                k, vt = tile(j)
                qs = qs_ref[...]
                part = None
                for c in range(tk // sub):
                    p = jnp.exp2(_dot(k[c * sub:(c + 1) * sub], qs)).astype(BF16)
                    pv = _dot(vt[:, c * sub:(c + 1) * sub], p)
                    part = pv if part is None else part + pv
                acc_ref[...] += part
                return carry

            lax.fori_loop(0, n_lat, body, 0)

        @pl.when(jnp.logical_not(fast))
        def _():
            m_ref[...] = jnp.zeros_like(m_ref)

            def body(j, carry):
                k, vt = tile(j)
                s = _dot(k, qs_ref[...])
                m_old = m_ref[...]
                m_new = jnp.maximum(m_old, jnp.max(s, axis=0, keepdims=True))
                p = jnp.exp2(s - m_new).astype(BF16)
                acc_ref[...] = jnp.exp2(m_old - m_new) * acc_ref[...] + _dot(vt, p)
                m_ref[...] = m_new
                return carry

            lax.fori_loop(0, n_lat, body, 0)

    acc = acc_ref[...]
    o = acc[:ATT_HEAD_DIM] / acc[ATT_HEAD_DIM:ATT_HEAD_DIM + 1]
    o_ref[...] = jnp.concatenate(
        [o[:, g * tq:(g + 1) * tq].T for g in range(ATT_GROUP)], axis=1).astype(o_ref.dtype)


def _attention(qt, kc, vct, kmax=None, k=None, vt=None, *, tq, tk=0, sub=0):
    t = qt.shape[1]
    lc = kc.shape[1]
    gw = ATT_GROUP * ATT_HEAD_DIM
    cols = ATT_GROUP * tq
    in_specs = [pl.BlockSpec((gw, tq), lambda h, i: (h, i)),
                pl.BlockSpec((1, lc, LANES), lambda h, i: (h, 0, 0)),
                pl.BlockSpec((ATT_VROWS, lc), lambda h, i: (h, 0))]
    args = [qt, kc, vct]
    scratch = [pltpu.VMEM((ATT_VROWS, cols), F32)]
    n_lat = 0
    if k is not None:
        lk = k.shape[1]
        n_lat = lk // tk
        in_specs = ([pl.BlockSpec(memory_space=pltpu.SMEM)] + in_specs
                    + [pl.BlockSpec((1, lk, LANES), lambda h, i: (h, 0, 0)),
                       pl.BlockSpec((ATT_VROWS, lk), lambda h, i: (h, 0))])
        args = [kmax] + args + [k, vt]
        scratch = [pltpu.VMEM((LANES, cols), BF16)] + scratch + [pltpu.VMEM((1, cols), F32)]
    return pl.pallas_call(
        functools.partial(_attention_kernel, tq=tq, tk=tk, sub=sub, n_lat=n_lat),
        out_shape=jax.ShapeDtypeStruct((t, ATT_WIDTH), BF16),
        grid=(ATT_KV_HEADS, t // tq),
        in_specs=in_specs,
        out_specs=pl.BlockSpec((tq, gw), lambda h, i: (i, h)),
        scratch_shapes=scratch,
        compiler_params=_cparams("parallel", "parallel"),
        name="attention_lat" if n_lat else "attention_ctx",
    )(*args)


def _ret_tables(lg_f, lg_b):
    pos = jnp.arange(RET_CHUNK, dtype=F32)
    rel = pos[:, None] - pos[None, :]
    mask = jnp.where(rel > 0, jnp.exp(lg_f[:, None, None] * jnp.maximum(rel, 0.0)),
                     jnp.where(rel < 0, jnp.exp(lg_b[:, None, None] * jnp.maximum(-rel, 0.0)), 2.0))
    zeta = jnp.stack([jnp.exp(lg_f[:, None] * (RET_CHUNK - 1 - pos)[None, :]),
                      jnp.exp(lg_b[:, None] * pos[None, :])])
    xi = jnp.stack([jnp.exp(lg_f[:, None] * (pos + 1)[None, :]),
                    jnp.exp(lg_b[:, None] * (RET_CHUNK - pos)[None, :])])
    chunk_decay = jnp.stack([jnp.exp(lg_f * RET_CHUNK), jnp.exp(lg_b * RET_CHUNK)])
    zeta = jnp.repeat(zeta, RET_HEAD_DIM, axis=1).transpose(0, 2, 1)
    xi = jnp.repeat(xi, RET_HEAD_DIM, axis=1).transpose(0, 2, 1)
    chunk_decay = jnp.repeat(chunk_decay, RET_HEAD_DIM, axis=1)[:, :, None]
    return mask, zeta, xi, chunk_decay


def _ret_state_kernel(k_ref, v_ref, zeta_ref, cd_ref, s0_ref, sprev_ref, sfin_ref, s_ref, *, cps):
    d = pl.program_id(0)

    @pl.when(pl.program_id(1) == 0)
    def _():
        s_ref[...] = s0_ref[0]

    zeta = zeta_ref[0] * (RET_HEAD_DIM ** -0.5)
    cd = cd_ref[0]
    us = []
    for c in range(cps):
        rows = slice(c * RET_CHUNK, (c + 1) * RET_CHUNK)
        kz = (k_ref[rows, :] * zeta).astype(BF16)
        v = v_ref[rows, :].astype(BF16)
        heads = []
        for h in range(RET_HEADS):
            sl = slice(h * RET_HEAD_DIM, (h + 1) * RET_HEAD_DIM)
            heads.append(lax.dot_general(kz[:, sl], v[:, sl], (((0,), (0,)), ((), ())),
                                         preferred_element_type=F32))
        us.append(jnp.concatenate(heads, axis=0))

    def scan(order):
        s = s_ref[...]
        for c in order:
            sprev_ref[0, c] = s
            s = cd * s + us[c]
        s_ref[...] = s
        sfin_ref[0] = s

    @pl.when(d == 0)
    def _():
        scan(range(cps))

    @pl.when(d == 1)
    def _():
        scan(reversed(range(cps)))


def _retention_states(p_ret, zeta, chunk_decay, s0):
    nc = p_ret.shape[0] // RET_CHUNK
    cps = min(nc, RET_STATE_CHUNKS)
    ng = nc // cps
    rows = cps * RET_CHUNK

    def group(d, i):
        return i + d * (ng - 1 - 2 * i)

    return pl.pallas_call(
        functools.partial(_ret_state_kernel, cps=cps),
        out_shape=(jax.ShapeDtypeStruct((2, nc, RET_WIDTH, RET_HEAD_DIM), F32),
                   jax.ShapeDtypeStruct((2, RET_WIDTH, RET_HEAD_DIM), F32)),
        grid=(2, ng),
        in_specs=[pl.BlockSpec((rows, RET_WIDTH), lambda d, i: (group(d, i), 1)),
                  pl.BlockSpec((rows, RET_WIDTH), lambda d, i: (group(d, i), 2)),
                  pl.BlockSpec((1, RET_CHUNK, RET_WIDTH), lambda d, i: (d, 0, 0)),
                  pl.BlockSpec((1, RET_WIDTH, 1), lambda d, i: (d, 0, 0)),
                  pl.BlockSpec((1, RET_WIDTH, RET_HEAD_DIM), lambda d, i: (d, 0, 0))],
        out_specs=(pl.BlockSpec((1, cps, RET_WIDTH, RET_HEAD_DIM), lambda d, i: (d, group(d, i), 0, 0)),
                   pl.BlockSpec((1, RET_WIDTH, RET_HEAD_DIM), lambda d, i: (d, 0, 0))),
        scratch_shapes=[pltpu.VMEM((RET_WIDTH, RET_HEAD_DIM), F32)],
        compiler_params=_cparams("arbitrary", "arbitrary"),
        name="retention_states",
    )(p_ret, p_ret, zeta, chunk_decay, s0)


def _ret_out_kernel(q_ref, k_ref, v_ref, g_ref, mask_ref, xi_ref, sf_ref, sb_ref, y_ref, *, cpo):
    for c in range(cpo):
        rows = slice(c * RET_CHUNK, (c + 1) * RET_CHUNK)
        q = q_ref[rows, :]
        k = (k_ref[rows, :] * (RET_HEAD_DIM ** -0.5)).astype(BF16)
        v = v_ref[rows, :].astype(BF16)
        qb = q.astype(BF16)
        qf = (q * xi_ref[0]).astype(BF16)
        qr = (q * xi_ref[1]).astype(BF16)
        sf = sf_ref[0, c].astype(BF16)
        sb = sb_ref[0, c].astype(BF16)
        ys = []
        for h in range(RET_HEADS):
            sl = slice(h * RET_HEAD_DIM, (h + 1) * RET_HEAD_DIM)
            scores = lax.dot_general(qb[:, sl], k[:, sl], (((1,), (1,)), ((), ())),
                                     preferred_element_type=F32)
            y = _dot((scores * mask_ref[h]).astype(BF16), v[:, sl])
            y = y + _dot(qf[:, sl], sf[sl, :]) + _dot(qr[:, sl], sb[sl, :])
            mu = jnp.mean(y, axis=-1, keepdims=True)
            yc = y - mu
            var = jnp.mean(yc * yc, axis=-1, keepdims=True)
            ys.append(yc * lax.rsqrt(var + EPS))
        y_ref[rows, :] = (_silu(g_ref[rows, :]) * jnp.concatenate(ys, axis=1)).astype(y_ref.dtype)


def _retention_outputs(p_ret, mask, xi, sprev):
    nc = p_ret.shape[0] // RET_CHUNK
    cpo = min(nc, RET_OUT_CHUNKS)
    rows = cpo * RET_CHUNK
    col = lambda c: pl.BlockSpec((rows, RET_WIDTH), lambda i: (i, c))
    return pl.pallas_call(
        functools.partial(_ret_out_kernel, cpo=cpo),
        out_shape=jax.ShapeDtypeStruct((p_ret.shape[0], RET_WIDTH), BF16),
        grid=(nc // cpo,),
        in_specs=[col(0), col(1), col(2), col(3),
                  pl.BlockSpec((RET_HEADS, RET_CHUNK, RET_CHUNK), lambda i: (0, 0, 0)),
                  pl.BlockSpec((2, RET_CHUNK, RET_WIDTH), lambda i: (0, 0, 0)),
                  pl.BlockSpec((1, cpo, RET_WIDTH, RET_HEAD_DIM), lambda i: (0, i, 0, 0)),
                  pl.BlockSpec((1, cpo, RET_WIDTH, RET_HEAD_DIM), lambda i: (1, i, 0, 0))],
        out_specs=pl.BlockSpec((rows, RET_WIDTH), lambda i: (i, 0)),
        compiler_params=_cparams("parallel"),
        name="retention_outputs",
    )(p_ret, p_ret, p_ret, p_ret, mask, xi, sprev, sprev)


def _hy_filter_kernel(feat_ref, w1_ref, b1_ref, w2_ref, b2_ref, w3_ref, b3_ref, freq_ref,
                      delta_ref, taps_ref, asum_ref, *, seq_len):
    i = pl.program_id(0)
    feat = feat_ref[...]
    tm = feat.shape[0]
    half = tm // 2
    freq = freq_ref[...]
    packed = jnp.concatenate([feat[:half], feat[half:]], axis=1)
    h = jnp.sin(freq * (_dot3(packed, w1_ref[...]) + b1_ref[...]))
    h = jnp.sin(freq * (_dot3(h, w2_ref[...]) + b2_ref[...]))
    h = _dot3(h, w3_ref[0])
    h = jnp.concatenate([h[:, :HY_WIDTH], h[:, HY_WIDTH:]], axis=0) + b3_ref[0]
    j = i * tm + lax.broadcasted_iota(jnp.int32, (tm, HY_WIDTH), 0)
    taps = jnp.where(j == seq_len, 0.0, h * jnp.exp(-feat[:, 0:1] * delta_ref[...]))
    taps_ref[...] = taps

    @pl.when(i == 0)
    def _():
        asum_ref[...] = jnp.zeros_like(asum_ref)

    asum_ref[...] += jnp.sum(jnp.abs(taps).reshape(tm // SUBLANES, SUBLANES, HY_WIDTH), axis=0)


def _position_features(seq_len):
    lo = 256
    hi = seq_len // lo + 1
    bands = np.linspace(1e-4, HY_POS_BANDS - 1, HY_POS_BANDS).astype(np.float32).astype(np.float64)
    ang_hi = (2.0 * np.pi / seq_len) * lo * np.arange(hi)[:, None] * bands[None, :]
    ang_lo = (2.0 * np.pi / seq_len) * np.arange(lo)[:, None] * bands[None, :]
    tab = lambda a: jnp.asarray(a, F32)
    ch, sh = tab(np.cos(ang_hi))[:, None, :], tab(np.sin(ang_hi))[:, None, :]
    cl, sl = tab(np.cos(ang_lo))[None, :, :], tab(np.sin(ang_lo))[None, :, :]
    npos = seq_len + 1
    cos = (ch * cl - sh * sl).reshape(hi * lo, HY_POS_BANDS)[:npos]
    sin = (sh * cl + ch * sl).reshape(hi * lo, HY_POS_BANDS)[:npos]
    c_end, s_end = tab(np.cos(2.0 * np.pi * bands))[None, :], tab(np.sin(2.0 * np.pi * bands))[None, :]
    cos_b = c_end * cos[1:seq_len] + s_end * sin[1:seq_len]
    sin_b = s_end * cos[1:seq_len] - c_end * sin[1:seq_len]
    lag = jnp.concatenate([jnp.arange(npos, dtype=F32), jnp.arange(seq_len - 1, 0, -1, dtype=F32)])
    t = lag / max(seq_len - 1, 1)
    cos = jnp.concatenate([cos, cos_b], axis=0)
    sin = jnp.concatenate([sin, sin_b], axis=0)
    return jnp.concatenate([t[:, None], cos, -sin,
                            jnp.zeros((2 * seq_len, LANES - 1 - 2 * HY_POS_BANDS), F32)], axis=-1)


def _hyena_taps(seq_len, w1, b1, w2, b2, w3, b3, freq):
    n = 2 * seq_len
    feat = _position_features(seq_len)
    w1p = jnp.concatenate([w1, jnp.zeros((LANES - w1.shape[0], w1.shape[1]), F32)], axis=0)
    pair = lambda w: jnp.concatenate(
        [jnp.concatenate([w, jnp.zeros_like(w)], axis=1),
         jnp.concatenate([jnp.zeros_like(w), w], axis=1)], axis=0)
    twice = lambda v: jnp.tile(v.reshape(1, -1), (1, 2))
    w3_dirs = jnp.stack([pair(w3[:, :HY_WIDTH]), pair(w3[:, HY_WIDTH:])])
    b3_dirs = b3.reshape(2, 1, HY_WIDTH)
    deltas = jnp.abs(jnp.linspace(math.log(HY_DECAY_TARGET) / HY_LONG_DECAY_PCT,
                                  math.log(HY_DECAY_TARGET) / HY_SHORT_DECAY_PCT,
                                  HY_WIDTH, dtype=F32)).reshape(1, HY_WIDTH)
    tm = min(seq_len, 1024)
    per_dir = seq_len // tm
    hid2 = 2 * w1.shape[1]
    const = lambda shape: pl.BlockSpec(shape, lambda i: (0, 0))
    by_dir = lambda shape: pl.BlockSpec(shape, lambda i: (i // per_dir, 0, 0))
    return pl.pallas_call(
        functools.partial(_hy_filter_kernel, seq_len=seq_len),
        out_shape=(jax.ShapeDtypeStruct((n, HY_WIDTH), F32),
                   jax.ShapeDtypeStruct((8, HY_WIDTH), F32)),
        grid=(n // tm,),
        in_specs=[pl.BlockSpec((tm, LANES), lambda i: (i, 0)),
                  const((2 * LANES, hid2)), const((1, hid2)), const((hid2, hid2)), const((1, hid2)),
                  by_dir((1, hid2, 2 * HY_WIDTH)), by_dir((1, 1, HY_WIDTH)), const((1, hid2)),
                  const((1, HY_WIDTH))],
        out_specs=(pl.BlockSpec((tm, HY_WIDTH), lambda i: (i, 0)), const((8, HY_WIDTH))),
        compiler_params=_cparams("arbitrary"),
        name="hyena_taps",
    )(feat, pair(w1p), twice(b1), pair(w2), twice(b2), w3_dirs, b3_dirs, twice(freq), deltas)


FFT_N2 = 256
FFT_K1_PER_STEP = 8
FFT_N2_PER_STEP = 16


def _kept_slabs(n1):
    kept = n1 // 2 + 1
    return kept, -(-kept // SUBLANES) * SUBLANES


def _dft_tables(n_total, n1_in):
    n1 = n_total // FFT_N2
    kept, padded = _kept_slabs(n1)
    k1 = np.arange(padded)
    a = 2.0 * np.pi * np.outer(k1, np.arange(n1_in)) / n1
    live = (k1 < kept)[:, None]
    lvl1 = np.concatenate([np.cos(a) * live, -np.sin(a) * live], axis=0)
    n2 = np.arange(FFT_N2)
    th = 2.0 * np.pi * np.outer(n2, n2) / FFT_N2
    fc, fs = np.cos(th), np.sin(th)
    g = np.block([[fc, fs], [-fs, fc]])
    ph = 2.0 * np.pi * np.outer(k1, n2) / n_total
    tw = np.stack([np.cos(ph), np.sin(ph)])[..., None]
    return (jnp.asarray(lvl1, F32), jnp.asarray(g, F32), jnp.asarray(g.T, F32), jnp.asarray(tw, F32))


def _inverse_lvl1_table(n_total, n1_out):
    n1 = n_total // FFT_N2
    kept, padded = _kept_slabs(n1)
    k1 = np.arange(padded)
    a = 2.0 * np.pi * np.outer(np.arange(n1_out), k1) / n1
    weight = np.where(k1 >= kept, 0.0, np.where((k1 == 0) | (k1 == n1 // 2), 1.0, 2.0))[None, :]
    return jnp.asarray(np.concatenate([np.cos(a) * weight, -np.sin(a) * weight], axis=1), F32)


def _lvl2_kernel(b_ref, tw_ref, gh_ref, gl_ref, *rest, inverse):
    for j in range(b_ref.shape[1]):
        c = tw_ref[0, j]
        s = tw_ref[1, j]
        br = b_ref[0, j]
        bi = b_ref[1, j]
        d = jnp.concatenate([c * br + s * bi, c * bi - s * br], axis=0)
        dh, dl = _split(d)
        x = _dot(gh_ref[...], dh) + _dot(gh_ref[...], dl) + _dot(gl_ref[...], dh)
        if not inverse:
            (o_ref,) = rest
            o_ref[0, j] = x[:FFT_N2]
            o_ref[1, j] = x[FFT_N2:]
            continue
        h_ref, gth_ref, gtl_ref, o_ref = rest
        xr, xi = x[:FFT_N2], x[FFT_N2:]
        hr, hi = h_ref[0, j], h_ref[1, j]
        y = jnp.concatenate([xr * hr - xi * hi, xr * hi + xi * hr], axis=0)
        yh, yl = _split(y)
        cc = _dot(gth_ref[...], yh) + _dot(gth_ref[...], yl) + _dot(gtl_ref[...], yh)
        cr, ci = cc[:FFT_N2], cc[FFT_N2:]
        o_ref[0, j] = c * cr - s * ci
        o_ref[1, j] = c * ci + s * cr


def _level2(b, tw, g, h=None, gt=None):
    _, n1, n2, ch = b.shape
    kb = FFT_K1_PER_STEP
    blk = pl.BlockSpec((2, kb, n2, ch), lambda i: (0, i, 0, 0))
    const = pl.BlockSpec((2 * n2, 2 * n2), lambda i: (0, 0))
    gh, gl = _split(g)
    in_specs = [blk, pl.BlockSpec((2, kb, n2, 1), lambda i: (0, i, 0, 0)), const, const]
    args = [b, tw, gh, gl]
    if h is not None:
        gth, gtl = _split(gt)
        in_specs += [blk, const, const]
        args += [h, gth, gtl]
    return pl.pallas_call(
        functools.partial(_lvl2_kernel, inverse=h is not None),
        out_shape=jax.ShapeDtypeStruct(b.shape, F32),
        grid=(n1 // kb,),
        in_specs=in_specs,
        out_specs=blk,
        compiler_params=_cparams("parallel"),
        name="hyena_fft_level2" + ("_conv" if h is not None else ""),
    )(*args)


def _lvl1_kernel(a_ref, x_ref, o_ref, *, inverse):
    tn2 = x_ref.shape[-2]
    ch = x_ref.shape[-1]
    if inverse:
        x = jnp.concatenate([x_ref[:, :, j, :].reshape(-1, ch) for j in range(tn2)], axis=1)
    else:
        x = jnp.concatenate([x_ref[:, j, :] for j in range(tn2)], axis=1)
    out = _dot3(a_ref[...], x)
    for j in range(tn2):
        slab = out[:, j * ch:(j + 1) * ch]
        if inverse:
            o_ref[:, j, :] = slab
        else:
            o_ref[:, :, j, :] = slab.reshape(2, slab.shape[0] // 2, ch)


def _level1(a, x, name):
    n1_in, n2, ch = x.shape
    n1 = a.shape[0] // 2
    tn2 = FFT_N2_PER_STEP
    return pl.pallas_call(
        functools.partial(_lvl1_kernel, inverse=False),
        out_shape=jax.ShapeDtypeStruct((2, n1, n2, ch), F32),
        grid=(n2 // tn2,),
        in_specs=[pl.BlockSpec(a.shape, lambda i: (0, 0)),
                  pl.BlockSpec((n1_in, tn2, ch), lambda i: (0, i, 0))],
        out_specs=pl.BlockSpec((2, n1, tn2, ch), lambda i: (0, 0, i, 0)),
        compiler_params=_cparams("parallel"),
        name=name,
    )(a, x)


def _inverse_level1(a, x):
    _, n1, n2, ch = x.shape
    n1_out = a.shape[0]
    tn2 = FFT_N2_PER_STEP
    return pl.pallas_call(
        functools.partial(_lvl1_kernel, inverse=True),
        out_shape=jax.ShapeDtypeStruct((n1_out, n2, ch), F32),
        grid=(n2 // tn2,),
        in_specs=[pl.BlockSpec(a.shape, lambda i: (0, 0)),
                  pl.BlockSpec((2, n1, tn2, ch), lambda i: (0, 0, i, 0))],
        out_specs=pl.BlockSpec((n1_out, tn2, ch), lambda i: (0, i, 0)),
        compiler_params=_cparams("parallel"),
        name="hyena_fft_inverse_level1",
    )(a, x)


def _hyena_long_conv(z, taps):
    seq_len, ch = z.shape
    n = 2 * seq_len
    n1 = n // FFT_N2
    lvl1_z, g, gt, tw = _dft_tables(n, n1 // 2)
    lvl1_t = _dft_tables(n, n1)[0]
    inv1 = _inverse_lvl1_table(n, n1 // 2)
    hb = _level1(lvl1_t, taps.reshape(n1, FFT_N2, ch), "hyena_fft_level1_taps")
    hspec = _level2(hb, tw, g)
    zb = _level1(lvl1_z, z.reshape(n1 // 2, FFT_N2, ch), "hyena_fft_level1_z")
    cspec = _level2(zb, tw, g, hspec, gt)
    return _inverse_level1(inv1, cspec).reshape(seq_len, ch)


def _small_conv_kernel(z_ref, taps_ref, fc_ref, fs_ref, o_ref):
    seq_len = z_ref.shape[0]
    fc = fc_ref[...]
    fs = fs_ref[...]
    z = z_ref[...]
    taps = taps_ref[...]
    zr = _dot3(fc[:, :seq_len], z)
    zi = -_dot3(fs[:, :seq_len], z)
    hr = _dot3(fc, taps)
    hi = -_dot3(fs, taps)
    yr = zr * hr - zi * hi
    yi = zr * hi + zi * hr
    o_ref[...] = _dot3(fc[:seq_len, :], yr) - _dot3(fs[:seq_len, :], yi)


def _hyena_small_conv(z, taps):
    seq_len, ch = z.shape
    n = 2 * seq_len
    th = 2.0 * np.pi * np.outer(np.arange(n), np.arange(n)) / n
    full = lambda shape: pl.BlockSpec(shape, lambda: (0, 0))
    return pl.pallas_call(
        _small_conv_kernel,
        out_shape=jax.ShapeDtypeStruct((seq_len, ch), F32),
        in_specs=[full((seq_len, ch)), full((n, ch)), full((n, n)), full((n, n))],
        out_specs=full((seq_len, ch)),
        compiler_params=pltpu.CompilerParams(vmem_limit_bytes=VMEM_LIMIT),
        name="hyena_small_conv",
    )(z, taps, jnp.asarray(np.cos(th), F32), jnp.asarray(np.sin(th), F32))


def _hyena_mixer(x0, z, filt, bias):
    seq_len = x0.shape[0]
    taps, asum = _hyena_taps(seq_len, *filt)
    if 2 * seq_len // FFT_N2 >= 16:
        conv = _hyena_long_conv(z, taps)
    else:
        conv = _hyena_small_conv(z, taps)
    return x0, z, conv, asum, bias.reshape(1, HY_WIDTH)


def _mixed_residual(x0_ref, z_ref, conv_ref, asum_ref, bias_ref, ret_ref, att_ref, wo_ref, x_ref,
                    mgate_ref):
    norm = jnp.sum(asum_ref[...], axis=0, keepdims=True) + EPS
    conv = conv_ref[...] * (1.0 / (2 * x_ref.shape[0] * pl.num_programs(0) * norm))
    y_hy = (x0_ref[...] * (conv + bias_ref[...] * z_ref[...])).astype(BF16)
    y = (_dot(y_hy, wo_ref[:HY_WIDTH, :])
         + _dot(ret_ref[...], wo_ref[HY_WIDTH:HY_WIDTH + RET_WIDTH, :])
         + _dot(att_ref[...], wo_ref[HY_WIDTH + RET_WIDTH:, :]))
    return x_ref[...] + mgate_ref[...] * y


N_MIXED = 10


def _mixer_specs(tm, index):
    row = lambda w: pl.BlockSpec((tm, w), lambda i, *_: (i, 0))
    return [row(HY_WIDTH), row(HY_WIDTH), row(HY_WIDTH),
            pl.BlockSpec((SUBLANES, HY_WIDTH), index), pl.BlockSpec((1, HY_WIDTH), index),
            row(RET_WIDTH), row(ATT_WIDTH),
            pl.BlockSpec((MIX_WIDTH, D_MODEL), index), row(D_MODEL),
            pl.BlockSpec((1, D_MODEL), index)]


def _swiglu(x, wg_ref, wu_ref, wd_ref):
    hidden = wg_ref.shape[1]
    y = None
    for lo in range(0, hidden, SWIGLU_CHUNK):
        hi = min(lo + SWIGLU_CHUNK, hidden)
        a = _silu(_dot(x, wg_ref[:, lo:hi])) * _dot(x, wu_ref[:, lo:hi])
        part = _dot(a.astype(BF16), wd_ref[lo:hi, :])
        y = part if y is None else y + part
    return y


def _ffn_kernel(*refs):
    g_ref, sh_ref, sc_ref, gate_ref, wg_ref, wu_ref, wd_ref, o_ref = refs[N_MIXED:]
    x = _mixed_residual(*refs[:N_MIXED])
    h = _norm_mod(x, g_ref[...], sh_ref[...], sc_ref[...]).astype(BF16)
    o_ref[...] = x + gate_ref[...] * _swiglu(h, wg_ref, wu_ref, wd_ref)


def _dense_ffn(mixed, g, shift, scale, gate, wg, wu, wd):
    t = mixed[N_MIXED - 2].shape[0]
    tm = min(t, 512)
    index = lambda i: (0, 0)
    vec = pl.BlockSpec((1, D_MODEL), index)
    resident = lambda shape: pl.BlockSpec(shape, index, pipeline_mode=pl.Buffered(1))
    return pl.pallas_call(
        _ffn_kernel,
        out_shape=jax.ShapeDtypeStruct((t, D_MODEL), F32),
        grid=(t // tm,),
        in_specs=_mixer_specs(tm, index) + [vec, vec, vec, vec,
                  resident((D_MODEL, FFN_HIDDEN)), resident((D_MODEL, FFN_HIDDEN)),
                  resident((FFN_HIDDEN, D_MODEL))],
        out_specs=pl.BlockSpec((tm, D_MODEL), lambda i: (i, 0)),
        compiler_params=_cparams("parallel"),
        name="dense_ffn",
    )(*mixed, g, shift, scale, gate, wg, wu, wd)


def _moe_route_kernel(*refs):
    g_ref, sh_ref, sc_ref, rw_ref, rb_ref, xo_ref, h_ref, info_ref = refs[N_MIXED:]
    x = _mixed_residual(*refs[:N_MIXED])
    xo_ref[...] = x
    h = _norm_mod(x, g_ref[...], sh_ref[...], sc_ref[...])
    _to_row_tiles(h_ref, h)
    logits = _dot3(h, rw_ref[...]) + rb_ref[...]
    lane = lax.broadcasted_iota(jnp.int32, logits.shape, 1)
    v1 = jnp.max(logits, axis=-1, keepdims=True)
    i1 = jnp.min(jnp.where(logits == v1, lane, LANES), axis=-1, keepdims=True)
    rest = jnp.where(lane == i1, -jnp.inf, logits)
    v2 = jnp.max(rest, axis=-1, keepdims=True)
    i2 = jnp.min(jnp.where(rest == v2, lane, LANES), axis=-1, keepdims=True)
    e = jnp.exp(v2 - v1)
    g1 = 1.0 / (1.0 + e)
    g2 = e * g1
    info_ref[...] = jnp.where(lane == 0, i1.astype(F32),
                              jnp.where(lane == 1, i2.astype(F32),
                                        jnp.where(lane == 2, g1, jnp.where(lane == 3, g2, 0.0))))


def _moe_route(mixed, g, shift, scale, router_w, router_b):
    t = mixed[N_MIXED - 2].shape[0]
    tm = min(t, 512)
    rw = jnp.concatenate([router_w, jnp.zeros((D_MODEL, LANES - N_EXPERTS), F32)], axis=1)
    rb = jnp.concatenate([router_b, jnp.full((LANES - N_EXPERTS,), -jnp.inf, F32)]).reshape(1, LANES)
    vec = pl.BlockSpec((1, D_MODEL), lambda i: (0, 0))
    return pl.pallas_call(
        _moe_route_kernel,
        out_shape=(jax.ShapeDtypeStruct((t, D_MODEL), F32),
                   jax.ShapeDtypeStruct((t * SUBLANES, LANES), F32),
                   jax.ShapeDtypeStruct((t, LANES), F32)),
        grid=(t // tm,),
        in_specs=_mixer_specs(tm, lambda i: (0, 0)) + [vec, vec, vec,
                  pl.BlockSpec((D_MODEL, LANES), lambda i: (0, 0)),
                  pl.BlockSpec((1, LANES), lambda i: (0, 0))],
        out_specs=(pl.BlockSpec((tm, D_MODEL), lambda i: (i, 0)),
                   pl.BlockSpec((tm * SUBLANES, LANES), lambda i: (i, 0)),
                   pl.BlockSpec((tm, LANES), lambda i: (i, 0))),
        compiler_params=_cparams("parallel"),
        name="moe_route",
    )(*mixed, g, shift, scale, rw, rb)


def _to_row_tiles(ref, x):
    rows = x.shape[0]
    for k in range(D_MODEL // LANES):
        ref[pl.ds(k, rows, stride=SUBLANES), :] = x[:, k * LANES:(k + 1) * LANES]


def _from_row_tiles(ref, rows):
    return jnp.concatenate(
        [ref[pl.ds(k, rows, stride=SUBLANES), :] for k in range(D_MODEL // LANES)], axis=1)


def _row_copy(src_hbm, dst_vmem, sem, src_row, dst_row):
    src = pl.multiple_of(src_row * SUBLANES, SUBLANES)
    dst = pl.multiple_of(dst_row * SUBLANES, SUBLANES)
    return pltpu.make_async_copy(src_hbm.at[pl.ds(src, SUBLANES)], dst_vmem.at[pl.ds(dst, SUBLANES)], sem)


def _moe_expert_kernel(blk_e_ref, nused_ref, dest_ref, fill_ref, h_hbm, wg_ref, wu_ref, wd_ref,
                       y_ref, xbuf_ref, src_ref, sem):
    b = pl.program_id(0)
    n_used = nused_ref[0]

    @pl.when(b == 0)
    def _():
        def invert(t, carry):
            for j in range(TOP_K):
                src_ref[dest_ref[TOP_K * t + j]] = t
            return carry

        lax.fori_loop(0, dest_ref.shape[0] // TOP_K, invert, 0, unroll=DMA_UNROLL)
        for e in range(N_EXPERTS):
            def pad(d, carry):
                src_ref[d] = 0
                return carry

            lax.fori_loop(fill_ref[0, e], fill_ref[1, e], pad, 0)

    def issue_rows(blk, slot):
        def body(pair, carry):
            for prio in range(2):
                r = 2 * pair + prio
                _row_copy(h_hbm, xbuf_ref.at[slot], sem.at[slot], src_ref[blk * MOE_BLOCK + r],
                          r).start(priority=prio)
            return carry

        lax.fori_loop(0, MOE_BLOCK // 2, body, 0, unroll=DMA_UNROLL // 2)

    def wait_rows(slot):
        def body(r, carry):
            _row_copy(h_hbm, xbuf_ref.at[slot], sem.at[slot], 0, r).wait()
            return carry

        lax.fori_loop(0, MOE_BLOCK, body, 0, unroll=DMA_UNROLL)

    @pl.when(b < n_used)
    def _():
        slot = b % 2

        @pl.when(b == 0)
        def _():
            issue_rows(0, 0)

        wait_rows(slot)

        @pl.when(b + 1 < n_used)
        def _():
            issue_rows(b + 1, 1 - slot)

        x = _from_row_tiles(xbuf_ref.at[slot], MOE_BLOCK).astype(BF16)
        _to_row_tiles(y_ref, _swiglu(x, wg_ref.at[0], wu_ref.at[0], wd_ref.at[0]))

    @pl.when(b >= n_used)
    def _():
        y_ref[...] = jnp.zeros_like(y_ref)


def _moe_experts(h, blk_expert, n_used, dest, fill, wg, wu, wd, n_blk):
    def w_index(b, be, nu, de, fi):
        return be[jnp.minimum(b, nu[0] - 1)], 0, 0

    grid_spec = pltpu.PrefetchScalarGridSpec(
        num_scalar_prefetch=4,
        grid=(n_blk,),
        in_specs=[pl.BlockSpec(memory_space=pl.ANY),
                  pl.BlockSpec((1, D_MODEL, FFN_HIDDEN), w_index),
                  pl.BlockSpec((1, D_MODEL, FFN_HIDDEN), w_index),
                  pl.BlockSpec((1, FFN_HIDDEN, D_MODEL), w_index)],
        out_specs=pl.BlockSpec((MOE_BLOCK * SUBLANES, LANES), lambda b, be, nu, de, fi: (b, 0)),
        scratch_shapes=[pltpu.VMEM((2, MOE_BLOCK * SUBLANES, LANES), F32),
                        pltpu.SMEM((n_blk * MOE_BLOCK,), jnp.int32),
                        pltpu.SemaphoreType.DMA((2,))],
    )
    return pl.pallas_call(
        _moe_expert_kernel,
        out_shape=jax.ShapeDtypeStruct((n_blk * MOE_BLOCK * SUBLANES, LANES), F32),
        grid_spec=grid_spec,
        compiler_params=_cparams("arbitrary"),
        name="moe_experts",
    )(blk_expert, n_used, dest, fill, h, wg, wu, wd)


def _moe_combine_kernel(dest_ref, x_ref, info_ref, gate_ref, ng_ref, y_hbm, o_ref,
                        y1_ref, y2_ref, sem, *, final_norm):
    i = pl.program_id(0)
    tm = x_ref.shape[0]
    slot = i % 2

    def issue_rows(blk, slot):
        def body(r, carry):
            a = 2 * (blk * tm + r)
            _row_copy(y_hbm, y1_ref.at[slot], sem.at[slot], dest_ref[a], r).start(priority=0)
            _row_copy(y_hbm, y2_ref.at[slot], sem.at[slot], dest_ref[a + 1], r).start(priority=1)
            return carry

        lax.fori_loop(0, tm, body, 0, unroll=DMA_UNROLL)

    def wait_rows(slot):
        def body(r, carry):
            _row_copy(y_hbm, y1_ref.at[slot], sem.at[slot], 0, r).wait()
            _row_copy(y_hbm, y2_ref.at[slot], sem.at[slot], 0, r).wait()
            return carry

        lax.fori_loop(0, tm, body, 0, unroll=DMA_UNROLL)

    @pl.when(i == 0)
    def _():
        issue_rows(0, 0)

    wait_rows(slot)

    @pl.when(i + 1 < pl.num_programs(0))
    def _():
        issue_rows(i + 1, 1 - slot)

    info = info_ref[...]
    y = (_from_row_tiles(y1_ref.at[slot], tm) * info[:, 2:3]
         + _from_row_tiles(y2_ref.at[slot], tm) * info[:, 3:4])
    x = x_ref[...] + gate_ref[...] * y
    if final_norm:
        ms = jnp.mean(x * x, axis=-1, keepdims=True)
        x = x * lax.rsqrt(ms + EPS) * ng_ref[...]
    o_ref[...] = x


def _moe_combine(dest, x, info, gate, norm_g, y, final_norm):
    t = x.shape[0]
    tm = min(t, 256)
    vec = pl.BlockSpec((1, D_MODEL), lambda i, d: (0, 0))
    grid_spec = pltpu.PrefetchScalarGridSpec(
        num_scalar_prefetch=1,
        grid=(t // tm,),
        in_specs=[pl.BlockSpec((tm, D_MODEL), lambda i, d: (i, 0)),
                  pl.BlockSpec((tm, LANES), lambda i, d: (i, 0)), vec, vec,
                  pl.BlockSpec(memory_space=pl.ANY)],
        out_specs=pl.BlockSpec((tm, D_MODEL), lambda i, d: (i, 0)),
        scratch_shapes=[pltpu.VMEM((2, tm * SUBLANES, LANES), F32),
                        pltpu.VMEM((2, tm * SUBLANES, LANES), F32),
                        pltpu.SemaphoreType.DMA((2,))],
    )
    return pl.pallas_call(
        functools.partial(_moe_combine_kernel, final_norm=final_norm),
        out_shape=jax.ShapeDtypeStruct((t, D_MODEL), F32),
        grid_spec=grid_spec,
        compiler_params=_cparams("arbitrary"),
        name="moe_combine",
    )(dest, x, info, gate, norm_g, y)


def _moe_layer(mixed, g, shift, scale, gate, router_w, router_b, wg, wu, wd, norm_g, final_norm):
    t = mixed[N_MIXED - 2].shape[0]
    n_asg = t * TOP_K
    x, h, info = _moe_route(mixed, g, shift, scale, router_w, router_b)
    expert = info[:, :TOP_K].astype(jnp.int32).reshape(-1)
    onehot = (expert[:, None] == jnp.arange(N_EXPERTS, dtype=jnp.int32)[None, :]).astype(jnp.int32)
    csum = jnp.cumsum(onehot, axis=0)
    counts = csum[-1]
    padded = (counts + MOE_BLOCK - 1) // MOE_BLOCK * MOE_BLOCK
    pad_end = jnp.cumsum(padded)
    pad_start = pad_end - padded
    dest = jnp.sum(onehot * (csum - 1 + pad_start[None, :]), axis=1).astype(jnp.int32)
    n_blk = -(-n_asg // MOE_BLOCK) + N_EXPERTS
    blk_start = jnp.arange(n_blk, dtype=jnp.int32) * MOE_BLOCK
    blk_expert = jnp.minimum(jnp.sum(blk_start[:, None] >= pad_end[None, :], axis=1),
                             N_EXPERTS - 1).astype(jnp.int32)
    n_used = (pad_end[-1:] // MOE_BLOCK).astype(jnp.int32)
    fill = jnp.stack([pad_start + counts, pad_end]).astype(jnp.int32)
    y = _moe_experts(h, blk_expert, n_used, dest, fill, wg, wu, wd, n_blk)
    return _moe_combine(dest, x, info, gate, norm_g, y, final_norm)


def _final_norm_kernel(x_ref, g_ref, o_ref):
    x = x_ref[...]
    ms = jnp.mean(x * x, axis=-1, keepdims=True)
    o_ref[...] = x * lax.rsqrt(ms + EPS) * g_ref[...]


def _final_norm(x, g):
    t = x.shape[0]
    tm = min(t, 1024)
    return pl.pallas_call(
        _final_norm_kernel,
        out_shape=jax.ShapeDtypeStruct((t, D_MODEL), F32),
        grid=(t // tm,),
        in_specs=[pl.BlockSpec((tm, D_MODEL), lambda i: (i, 0)),
                  pl.BlockSpec((1, D_MODEL), lambda i: (0, 0))],
        out_specs=pl.BlockSpec((tm, D_MODEL), lambda i: (i, 0)),
        compiler_params=_cparams("parallel"),
        name="final_norm",
    )(x, g)


def kernel(x, c, ctx, c_ctx, ada_w, ada_b, norm1_g, norm2_g, w_in, w_out, hy_conv_w, hy_conv_b, hy_filt_w1, hy_filt_b1, hy_filt_w2, hy_filt_b2, hy_filt_w3, hy_filt_b3, hy_filt_freq, hy_bias, ret_log_rate, attn_q_g, attn_k_g, ffn_w_gate, ffn_w_up, ffn_w_down, moe_router_w, moe_router_b, moe_w_gate, moe_w_up, moe_w_down, final_norm_g):
    assert x.shape[0] == 1 and c.shape[0] == 1
    seq_len = x.shape[1]
    x_lat = x[0]
    x_ctx = ctx[0]
    cos_t, sin_t = _rope_tables(seq_len)
    cvec = jnp.concatenate([c, c_ctx[None, :], jnp.zeros((6, D_MODEL), F32)], axis=0)
    cvec = _silu(cvec)
    row = lambda v: v.reshape(1, D_MODEL)
    zero_state = jnp.zeros((2, RET_WIDTH, RET_HEAD_DIM), F32)
    zero_tab = jnp.zeros((x_ctx.shape[0], LANES), F32)

    for l in range(DEPTH):
        last = l == DEPTH - 1
        mods = _ada_modulation(cvec, ada_w, ada_b, l).reshape(8, 6, D_MODEL)
        mod = [row(mods[0, i]) for i in range(6)]
        mod_c = [row(mods[1, i]) for i in range(6)]
        filt = (hy_filt_w1[l], hy_filt_b1[l], hy_filt_w2[l], hy_filt_b2[l],
                hy_filt_w3[l], hy_filt_b3[l], hy_filt_freq[l])
        lg_f = -jnp.exp(ret_log_rate[l, 0].astype(F32))
        lg_b = -jnp.exp(ret_log_rate[l, 1].astype(F32))
        mask, zeta, xi, chunk_decay = _ret_tables(lg_f, lg_b)
        w_in_l = w_in[l].astype(BF16)
        w_out_l = w_out[l].astype(BF16)
        n1 = row(norm1_g[l])
        n2 = row(norm2_g[l])

        x0, z, p_ret, qt, k_aug, vt, kmax_sq = _in_projection(
            x_lat, n1, mod[0], mod[1], w_in_l, hy_conv_w[l], hy_conv_b[l], cos_t, sin_t,
            attn_q_g[l], attn_k_g[l], rope=True)
        x0_c, z_c, pc_ret, qct, kc, vct, _ = _in_projection(
            x_ctx, n1, mod_c[0], mod_c[1], w_in_l, hy_conv_w[l], hy_conv_b[l], zero_tab, zero_tab,
            attn_q_g[l], attn_k_g[l], rope=False)

        sprev_c, s_ctx = _retention_states(pc_ret, zeta, chunk_decay, zero_state)

        y_hy = _hyena_mixer(x0, z, filt, hy_bias[l])
        sprev, _ = _retention_states(p_ret, zeta, chunk_decay, s_ctx)
        y_ret = _retention_outputs(p_ret, mask, xi, sprev)
        kmax = jnp.sqrt(jnp.max(kmax_sq[:, 0, ::ATT_HEAD_DIM], axis=0)) * KMAX_SLACK
        y_att = _attention(qt, kc, vct, kmax, k_aug, vt, tq=ATT_TQ, tk=ATT_TK, sub=ATT_SUB)

        if not last:
            yc_hy = _hyena_mixer(x0_c, z_c, filt, hy_bias[l])
            yc_ret = _retention_outputs(pc_ret, mask, xi, sprev_c)
            yc_att = _attention(qct, kc, vct, tq=x_ctx.shape[0])
            mixed_c = (*yc_hy, yc_ret, yc_att, w_out_l, x_ctx, mod_c[2])
        mixed = (*y_hy, y_ret, y_att, w_out_l, x_lat, mod[2])

        i = l // 2
        if l % 2 == 0:
            wg, wu, wd = (ffn_w_gate[i].astype(BF16), ffn_w_up[i].astype(BF16),
                          ffn_w_down[i].astype(BF16))
            x_lat = _dense_ffn(mixed, n2, mod[3], mod[4], mod[5], wg, wu, wd)
            if not last:
                x_ctx = _dense_ffn(mixed_c, n2, mod_c[3], mod_c[4], mod_c[5], wg, wu, wd)
        else:
            wg, wu, wd = (moe_w_gate[i].astype(BF16), moe_w_up[i].astype(BF16),
                          moe_w_down[i].astype(BF16))
            fg = row(final_norm_g)
            x_lat = _moe_layer(mixed, n2, mod[3], mod[4], mod[5], moe_router_w[i], moe_router_b[i],
                               wg, wu, wd, fg, final_norm=last)
            if not last:
                x_ctx = _moe_layer(mixed_c, n2, mod_c[3], mod_c[4], mod_c[5], moe_router_w[i],
                                   moe_router_b[i], wg, wu, wd, fg, final_norm=False)
    if DEPTH % 2 == 1:
        x_lat = _final_norm(x_lat, row(final_norm_g))
    return x_lat[None]
```
